```python
import jax, jax.numpy as jnp
from jax import lax
import numpy as np

D_MODEL = 1024
BATCH = 8
SEQ = 8192
DEPTH = 2

D_MIX = D_MODEL
D_RNN = D_MIX // 2
RNN_HEADS = 8
RNN_HEAD_DIM = D_RNN // RNN_HEADS
D_POOL = D_MIX // 4
POOL_WINDOWS = (2, 4, 8, 16)
POOL_GROUPS = len(POOL_WINDOWS)
POOL_GROUP_DIM = D_POOL // POOL_GROUPS
D_SGU = D_MIX // 4
SGU_HEADS = 4
SGU_HEAD_DIM = D_SGU // SGU_HEADS
CHUNK = 128
CONV_WIDTH = 4
LRU_C = 8.0
D_IN = 2 * D_RNN + D_POOL + 2 * D_SGU
D_FF = 64 * ((8 * D_MODEL // 3 + 63) // 64)
EPS = 1e-6

kernel_name = "hybrid_rglru_pool_sgu_macaron"


def rmsnorm(x, g):
    x32 = x.astype(jnp.float32)
    y = x32 * lax.rsqrt(jnp.mean(x32 * x32, axis=-1, keepdims=True) + EPS)
    return (y * g.astype(jnp.float32)).astype(x.dtype)


def swiglu(h, w_in, w_out):
    g, u = jnp.split(h @ w_in, 2, axis=-1)
    return (jax.nn.silu(g) * u) @ w_out


def causal_dwconv(x, w, b):
    S = x.shape[1]
    xp = jnp.pad(x, ((0, 0), (CONV_WIDTH - 1, 0), (0, 0)))
    y = b
    for k in range(CONV_WIDTH):
        y = y + xp[:, k:k + S] * w[k]
    return y


def rglru_branch(gate, xa, conv_w, conv_b, w_a, b_a, w_x, b_x, lam):
    B, S, _ = xa.shape
    xc = causal_dwconv(xa, conv_w, conv_b)
    xh = xc.reshape(B, S, RNN_HEADS, RNN_HEAD_DIM)
    r = jax.nn.sigmoid(jnp.einsum('bshi,hij->bshj', xh, w_a) + b_a)
    i = jax.nn.sigmoid(jnp.einsum('bshi,hij->bshj', xh, w_x) + b_x)
    r32 = r.astype(jnp.float32).reshape(B, S, D_RNN)
    i32 = i.astype(jnp.float32).reshape(B, S, D_RNN)
    x32 = xc.astype(jnp.float32)
    log_a = -LRU_C * r32 * jax.nn.softplus(-lam.astype(jnp.float32))
    a = jnp.exp(log_a)
    mult = jnp.sqrt(-jnp.expm1(2.0 * log_a))
    bvals = mult * (i32 * x32)

    def combine(left, right):
        a_l, b_l = left
        a_r, b_r = right
        return a_l * a_r, a_r * b_l + b_r

    _, h = lax.associative_scan(combine, (a, bvals), axis=1)
    return jax.nn.gelu(gate) * h.astype(gate.dtype)


def causal_window_mean(x32, w):
    S = x32.shape[1]
    cs = jnp.cumsum(x32, axis=1)
    prev = jnp.pad(cs, ((0, 0), (w, 0), (0, 0)))[:, :S]
    count = jnp.minimum(jnp.arange(S) + 1, w).astype(jnp.float32)
    return (cs - prev) / count[None, :, None]


def pool_branch(xp, pool_w, pool_scale):
    x32 = xp.astype(jnp.float32)
    outs = []
    for g, w in enumerate(POOL_WINDOWS):
        xg = x32[..., g * POOL_GROUP_DIM:(g + 1) * POOL_GROUP_DIM]
        d = (causal_window_mean(xg, w) - xg).astype(xp.dtype)
        outs.append(d @ pool_w[g])
    return jnp.concatenate(outs, axis=-1) * pool_scale


def sgu_branch(u, v, sgu_norm, sgu_w, sgu_b):
    B, S, _ = u.shape
    u = jax.nn.gelu(u)
    v = rmsnorm(jax.nn.gelu(v), sgu_norm)
    vh = v.reshape(B, S // CHUNK, CHUNK, SGU_HEADS, SGU_HEAD_DIM)
    mask = jnp.tril(jnp.ones((CHUNK, CHUNK), dtype=bool))
    ws = jnp.where(mask[None], sgu_w, jnp.zeros_like(sgu_w))
    z = jnp.einsum('hts,bnshd->bnthd', ws, vh) + jnp.transpose(sgu_b)[None, None, :, :, None]
    return u * z.reshape(B, S, D_SGU)


def _fwd_setup_inputs(seed: int = 0) -> dict:
    key = jax.random.key(seed)
    ks = jax.random.split(key, 24)
    f32 = jnp.float32

    def nrm(k, shape, scale):
        return jax.random.normal(k, shape, f32) * scale

    def gain(k, shape):
        return 1.0 + 0.05 * jax.random.normal(k, shape, f32)

    u_a = jax.random.uniform(ks[9], (DEPTH, D_RNN), f32, 0.9, 0.999)
    s = u_a ** (1.0 / LRU_C)
    lru_lambda = jnp.log(s) - jnp.log1p(-s)

    return {
        "x": jax.random.normal(ks[0], (BATCH, SEQ, D_MODEL), f32),
        "ffn1_norm": gain(ks[1], (DEPTH, D_MODEL)),
        "ffn1_w_in": nrm(ks[2], (DEPTH, D_MODEL, 2 * D_FF), D_MODEL ** -0.5),
        "ffn1_w_out": nrm(ks[3], (DEPTH, D_FF, D_MODEL), D_FF ** -0.5),
        "mix_norm": gain(ks[4], (DEPTH, D_MODEL)),
        "w_in": nrm(ks[5], (DEPTH, D_MODEL, D_IN), D_MODEL ** -0.5),
        "conv_w": nrm(ks[6], (DEPTH, CONV_WIDTH, D_RNN), CONV_WIDTH ** -0.5),
        "conv_b": nrm(ks[7], (DEPTH, D_RNN), 0.02),
        "rg_w_a": nrm(ks[8], (DEPTH, RNN_HEADS, RNN_HEAD_DIM, RNN_HEAD_DIM), RNN_HEAD_DIM ** -0.5),
        "rg_b_a": nrm(ks[10], (DEPTH, RNN_HEADS, RNN_HEAD_DIM), 0.02),
        "rg_w_x": nrm(ks[11], (DEPTH, RNN_HEADS, RNN_HEAD_DIM, RNN_HEAD_DIM), RNN_HEAD_DIM ** -0.5),
        "rg_b_x": nrm(ks[12], (DEPTH, RNN_HEADS, RNN_HEAD_DIM), 0.02),
        "lru_lambda": lru_lambda,
        "pool_w": nrm(ks[13], (DEPTH, POOL_GROUPS, POOL_GROUP_DIM, POOL_GROUP_DIM), POOL_GROUP_DIM ** -0.5),
        "pool_scale": gain(ks[14], (DEPTH, D_POOL)),
        "sgu_norm": gain(ks[15], (DEPTH, D_SGU)),
        "sgu_w": nrm(ks[16], (DEPTH, SGU_HEADS, CHUNK, CHUNK), CHUNK ** -0.5),
        "sgu_b": gain(ks[17], (DEPTH, SGU_HEADS, CHUNK)),
        "w_out": nrm(ks[18], (DEPTH, D_MIX, D_MODEL), D_MIX ** -0.5),
        "ffn2_norm": gain(ks[19], (DEPTH, D_MODEL)),
        "ffn2_w_in": nrm(ks[20], (DEPTH, D_MODEL, 2 * D_FF), D_MODEL ** -0.5),
        "ffn2_w_out": nrm(ks[21], (DEPTH, D_FF, D_MODEL), D_FF ** -0.5),
        "final_norm": gain(ks[22], (D_MODEL,)),
    }


def _fwd_reference(x, ffn1_norm, ffn1_w_in, ffn1_w_out, mix_norm, w_in, conv_w, conv_b,
              rg_w_a, rg_b_a, rg_w_x, rg_b_x, lru_lambda, pool_w, pool_scale,
              sgu_norm, sgu_w, sgu_b, w_out, ffn2_norm, ffn2_w_in, ffn2_w_out, final_norm):
    s1 = D_RNN
    s2 = 2 * D_RNN
    s3 = s2 + D_POOL
    s4 = s3 + D_SGU
    for l in range(DEPTH):
        x = x + 0.5 * swiglu(rmsnorm(x, ffn1_norm[l]), ffn1_w_in[l], ffn1_w_out[l])
        h = rmsnorm(x, mix_norm[l])
        p = h @ w_in[l]
        gate_a, xa, xp, u, v = jnp.split(p, [s1, s2, s3, s4], axis=-1)
        ya = rglru_branch(gate_a, xa, conv_w[l], conv_b[l], rg_w_a[l], rg_b_a[l],
                          rg_w_x[l], rg_b_x[l], lru_lambda[l])
        yb = pool_branch(xp, pool_w[l], pool_scale[l])
        yc = sgu_branch(u, v, sgu_norm[l], sgu_w[l], sgu_b[l])
        x = x + jnp.concatenate([ya, yb, yc], axis=-1) @ w_out[l]
        x = x + 0.5 * swiglu(rmsnorm(x, ffn2_norm[l]), ffn2_w_in[l], ffn2_w_out[l])
    return rmsnorm(x, final_norm)


import jax as _jax
import jax.numpy as _jnp

TWIN_FORMAT = 'train_step'
FWD_PARAMS = ['x', 'ffn1_norm', 'ffn1_w_in', 'ffn1_w_out', 'mix_norm', 'w_in', 'conv_w', 'conv_b', 'rg_w_a', 'rg_b_a', 'rg_w_x', 'rg_b_x', 'lru_lambda', 'pool_w', 'pool_scale', 'sgu_norm', 'sgu_w', 'sgu_b', 'w_out', 'ffn2_norm', 'ffn2_w_in', 'ffn2_w_out', 'final_norm']
TWIN_WEIGHTS = ['ffn1_norm', 'ffn1_w_in', 'ffn1_w_out', 'mix_norm', 'w_in', 'conv_w', 'conv_b', 'rg_w_a', 'rg_b_a', 'rg_w_x', 'rg_b_x', 'lru_lambda', 'pool_w', 'pool_scale', 'sgu_norm', 'sgu_w', 'sgu_b', 'w_out', 'ffn2_norm', 'ffn2_w_in', 'ffn2_w_out', 'final_norm']
TWIN_DIFF_INPUT = 'x'
TWIN_INPUTS = ['x', 'ffn1_norm', 'ffn1_w_in', 'ffn1_w_out', 'mix_norm', 'w_in', 'conv_w', 'conv_b', 'rg_w_a', 'rg_b_a', 'rg_w_x', 'rg_b_x', 'lru_lambda', 'pool_w', 'pool_scale', 'sgu_norm', 'sgu_w', 'sgu_b', 'w_out', 'ffn2_norm', 'ffn2_w_in', 'ffn2_w_out', 'final_norm', 'loss_target', 'm_ffn1_norm', 'm_ffn1_w_in', 'm_ffn1_w_out', 'm_mix_norm', 'm_w_in', 'm_conv_w', 'm_conv_b', 'm_rg_w_a', 'm_rg_b_a', 'm_rg_w_x', 'm_rg_b_x', 'm_lru_lambda', 'm_pool_w', 'm_pool_scale', 'm_sgu_norm', 'm_sgu_w', 'm_sgu_b', 'm_w_out', 'm_ffn2_norm', 'm_ffn2_w_in', 'm_ffn2_w_out', 'm_final_norm', 'v_ffn1_norm', 'v_ffn1_w_in', 'v_ffn1_w_out', 'v_mix_norm', 'v_w_in', 'v_conv_w', 'v_conv_b', 'v_rg_w_a', 'v_rg_b_a', 'v_rg_w_x', 'v_rg_b_x', 'v_lru_lambda', 'v_pool_w', 'v_pool_scale', 'v_sgu_norm', 'v_sgu_w', 'v_sgu_b', 'v_w_out', 'v_ffn2_norm', 'v_ffn2_w_in', 'v_ffn2_w_out', 'v_final_norm']
TWIN_OUTPUTS = ['loss', 'grad_x', 'grad_ffn1_norm', 'grad_ffn1_w_in', 'grad_ffn1_w_out', 'grad_mix_norm', 'grad_w_in', 'grad_conv_w', 'grad_conv_b', 'grad_rg_w_a', 'grad_rg_b_a', 'grad_rg_w_x', 'grad_rg_b_x', 'grad_lru_lambda', 'grad_pool_w', 'grad_pool_scale', 'grad_sgu_norm', 'grad_sgu_w', 'grad_sgu_b', 'grad_w_out', 'grad_ffn2_norm', 'grad_ffn2_w_in', 'grad_ffn2_w_out', 'grad_final_norm', 'delta_ffn1_norm', 'delta_ffn1_w_in', 'delta_ffn1_w_out', 'delta_mix_norm', 'delta_w_in', 'delta_conv_w', 'delta_conv_b', 'delta_rg_w_a', 'delta_rg_b_a', 'delta_rg_w_x', 'delta_rg_b_x', 'delta_lru_lambda', 'delta_pool_w', 'delta_pool_scale', 'delta_sgu_norm', 'delta_sgu_w', 'delta_sgu_b', 'delta_w_out', 'delta_ffn2_norm', 'delta_ffn2_w_in', 'delta_ffn2_w_out', 'delta_final_norm', 'new_m_ffn1_norm', 'new_m_ffn1_w_in', 'new_m_ffn1_w_out', 'new_m_mix_norm', 'new_m_w_in', 'new_m_conv_w', 'new_m_conv_b', 'new_m_rg_w_a', 'new_m_rg_b_a', 'new_m_rg_w_x', 'new_m_rg_b_x', 'new_m_lru_lambda', 'new_m_pool_w', 'new_m_pool_scale', 'new_m_sgu_norm', 'new_m_sgu_w', 'new_m_sgu_b', 'new_m_w_out', 'new_m_ffn2_norm', 'new_m_ffn2_w_in', 'new_m_ffn2_w_out', 'new_m_final_norm', 'new_v_ffn1_norm', 'new_v_ffn1_w_in', 'new_v_ffn1_w_out', 'new_v_mix_norm', 'new_v_w_in', 'new_v_conv_w', 'new_v_conv_b', 'new_v_rg_w_a', 'new_v_rg_b_a', 'new_v_rg_w_x', 'new_v_rg_b_x', 'new_v_lru_lambda', 'new_v_pool_w', 'new_v_pool_scale', 'new_v_sgu_norm', 'new_v_sgu_w', 'new_v_sgu_b', 'new_v_w_out', 'new_v_ffn2_norm', 'new_v_ffn2_w_in', 'new_v_ffn2_w_out', 'new_v_final_norm']
TWIN_LEAF_KINDS = {'loss': 'loss', 'grad_x': 'grad_x', 'grad_ffn1_norm': 'grad_w', 'grad_ffn1_w_in': 'grad_w', 'grad_ffn1_w_out': 'grad_w', 'grad_mix_norm': 'grad_w', 'grad_w_in': 'grad_w', 'grad_conv_w': 'grad_w', 'grad_conv_b': 'grad_w', 'grad_rg_w_a': 'grad_w', 'grad_rg_b_a': 'grad_w', 'grad_rg_w_x': 'grad_w', 'grad_rg_b_x': 'grad_w', 'grad_lru_lambda': 'grad_w', 'grad_pool_w': 'grad_w', 'grad_pool_scale': 'grad_w', 'grad_sgu_norm': 'grad_w', 'grad_sgu_w': 'grad_w', 'grad_sgu_b': 'grad_w', 'grad_w_out': 'grad_w', 'grad_ffn2_norm': 'grad_w', 'grad_ffn2_w_in': 'grad_w', 'grad_ffn2_w_out': 'grad_w', 'grad_final_norm': 'grad_w', 'delta_ffn1_norm': 'delta_w', 'delta_ffn1_w_in': 'delta_w', 'delta_ffn1_w_out': 'delta_w', 'delta_mix_norm': 'delta_w', 'delta_w_in': 'delta_w', 'delta_conv_w': 'delta_w', 'delta_conv_b': 'delta_w', 'delta_rg_w_a': 'delta_w', 'delta_rg_b_a': 'delta_w', 'delta_rg_w_x': 'delta_w', 'delta_rg_b_x': 'delta_w', 'delta_lru_lambda': 'delta_w', 'delta_pool_w': 'delta_w', 'delta_pool_scale': 'delta_w', 'delta_sgu_norm': 'delta_w', 'delta_sgu_w': 'delta_w', 'delta_sgu_b': 'delta_w', 'delta_w_out': 'delta_w', 'delta_ffn2_norm': 'delta_w', 'delta_ffn2_w_in': 'delta_w', 'delta_ffn2_w_out': 'delta_w', 'delta_final_norm': 'delta_w', 'new_m_ffn1_norm': 'new_m', 'new_m_ffn1_w_in': 'new_m', 'new_m_ffn1_w_out': 'new_m', 'new_m_mix_norm': 'new_m', 'new_m_w_in': 'new_m', 'new_m_conv_w': 'new_m', 'new_m_conv_b': 'new_m', 'new_m_rg_w_a': 'new_m', 'new_m_rg_b_a': 'new_m', 'new_m_rg_w_x': 'new_m', 'new_m_rg_b_x': 'new_m', 'new_m_lru_lambda': 'new_m', 'new_m_pool_w': 'new_m', 'new_m_pool_scale': 'new_m', 'new_m_sgu_norm': 'new_m', 'new_m_sgu_w': 'new_m', 'new_m_sgu_b': 'new_m', 'new_m_w_out': 'new_m', 'new_m_ffn2_norm': 'new_m', 'new_m_ffn2_w_in': 'new_m', 'new_m_ffn2_w_out': 'new_m', 'new_m_final_norm': 'new_m', 'new_v_ffn1_norm': 'new_v', 'new_v_ffn1_w_in': 'new_v', 'new_v_ffn1_w_out': 'new_v', 'new_v_mix_norm': 'new_v', 'new_v_w_in': 'new_v', 'new_v_conv_w': 'new_v', 'new_v_conv_b': 'new_v', 'new_v_rg_w_a': 'new_v', 'new_v_rg_b_a': 'new_v', 'new_v_rg_w_x': 'new_v', 'new_v_rg_b_x': 'new_v', 'new_v_lru_lambda': 'new_v', 'new_v_pool_w': 'new_v', 'new_v_pool_scale': 'new_v', 'new_v_sgu_norm': 'new_v', 'new_v_sgu_w': 'new_v', 'new_v_sgu_b': 'new_v', 'new_v_w_out': 'new_v', 'new_v_ffn2_norm': 'new_v', 'new_v_ffn2_w_in': 'new_v', 'new_v_ffn2_w_out': 'new_v', 'new_v_final_norm': 'new_v'}


def _forward(args):
    return _fwd_reference(*[args[k] for k in FWD_PARAMS])


def _output_shape():
    out = _jax.eval_shape(lambda: _forward(_fwd_setup_inputs(0)))
    return out.shape, out.dtype

N_MICROBATCH = 1
ADAM_LR = 0.001
ADAM_B1 = 0.9
ADAM_B2 = 0.999
ADAM_EPS = 1e-08
ADAM_WD = 0.01
ADAM_STEP = 10
PER_EXAMPLE_BATCH_AXIS = {'x': 0, 'loss_target': 0}
SHARED_INPUTS = []
_WEIGHT_DTYPES = {'ffn1_norm': _jnp.float32, 'ffn1_w_in': _jnp.float32, 'ffn1_w_out': _jnp.float32, 'mix_norm': _jnp.float32, 'w_in': _jnp.float32, 'conv_w': _jnp.float32, 'conv_b': _jnp.float32, 'rg_w_a': _jnp.float32, 'rg_b_a': _jnp.float32, 'rg_w_x': _jnp.float32, 'rg_b_x': _jnp.float32, 'lru_lambda': _jnp.float32, 'pool_w': _jnp.float32, 'pool_scale': _jnp.float32, 'sgu_norm': _jnp.float32, 'sgu_w': _jnp.float32, 'sgu_b': _jnp.float32, 'w_out': _jnp.float32, 'ffn2_norm': _jnp.float32, 'ffn2_w_in': _jnp.float32, 'ffn2_w_out': _jnp.float32, 'final_norm': _jnp.float32}
MOMENT_SCALE = {'ffn1_norm': 1.071639e-01, 'ffn1_w_in': 4.594349e-02, 'ffn1_w_out': 7.442921e-02, 'mix_norm': 1.790276e-01, 'w_in': 1.416042e-01, 'conv_w': 1.579627e-01, 'conv_b': 1.865377e+00, 'rg_w_a': 6.922453e-02, 'rg_b_a': 6.124765e-02, 'rg_w_x': 1.263769e-01, 'rg_b_x': 5.997468e-02, 'lru_lambda': 1.006726e-01, 'pool_w': 1.785546e-01, 'pool_scale': 1.847693e-01, 'sgu_norm': 1.048515e-01, 'sgu_w': 6.580621e-02, 'sgu_b': 9.419380e-02, 'w_out': 1.782555e-01, 'ffn2_norm': 8.440143e-02, 'ffn2_w_in': 3.597510e-02, 'ffn2_w_out': 5.808257e-02, 'final_norm': 6.401533e+01}


def _to_microbatches(a, axis):
    t = _jnp.moveaxis(a, axis, 0)
    t = t.reshape((N_MICROBATCH, t.shape[0] // N_MICROBATCH) + t.shape[1:])
    return _jnp.moveaxis(t, 1, axis + 1)


def setup_inputs(seed: int = 0) -> dict:
    inp = _fwd_setup_inputs(seed)
    key = _jax.random.fold_in(_jax.random.key(seed), 7919)
    shape, _ = _output_shape()
    out = dict(inp)
    out["loss_target"] = _jax.random.normal(_jax.random.fold_in(key, 0), shape, _jnp.float32)
    for i, name in enumerate(TWIN_WEIGHTS):
        w = inp[name].astype(_jnp.float32)
        if MOMENT_SCALE is None:
            s = _jnp.sqrt(_jnp.mean(_jnp.square(w)) + 1e-30)
        else:
            s = MOMENT_SCALE[name]
        km, kv = _jax.random.split(_jax.random.fold_in(key, i + 1))
        out[name] = w
        out["m_" + name] = s * _jax.random.normal(km, w.shape, _jnp.float32)
        out["v_" + name] = (s * s) * _jax.random.uniform(kv, w.shape, _jnp.float32, 0.5, 1.5)
    if N_MICROBATCH > 1:
        for name, axis in PER_EXAMPLE_BATCH_AXIS.items():
            out[name] = _to_microbatches(out[name], axis)
    return {'x': out['x'], 'ffn1_norm': out['ffn1_norm'], 'ffn1_w_in': out['ffn1_w_in'], 'ffn1_w_out': out['ffn1_w_out'], 'mix_norm': out['mix_norm'], 'w_in': out['w_in'], 'conv_w': out['conv_w'], 'conv_b': out['conv_b'], 'rg_w_a': out['rg_w_a'], 'rg_b_a': out['rg_b_a'], 'rg_w_x': out['rg_w_x'], 'rg_b_x': out['rg_b_x'], 'lru_lambda': out['lru_lambda'], 'pool_w': out['pool_w'], 'pool_scale': out['pool_scale'], 'sgu_norm': out['sgu_norm'], 'sgu_w': out['sgu_w'], 'sgu_b': out['sgu_b'], 'w_out': out['w_out'], 'ffn2_norm': out['ffn2_norm'], 'ffn2_w_in': out['ffn2_w_in'], 'ffn2_w_out': out['ffn2_w_out'], 'final_norm': out['final_norm'], 'loss_target': out['loss_target'], 'm_ffn1_norm': out['m_ffn1_norm'], 'm_ffn1_w_in': out['m_ffn1_w_in'], 'm_ffn1_w_out': out['m_ffn1_w_out'], 'm_mix_norm': out['m_mix_norm'], 'm_w_in': out['m_w_in'], 'm_conv_w': out['m_conv_w'], 'm_conv_b': out['m_conv_b'], 'm_rg_w_a': out['m_rg_w_a'], 'm_rg_b_a': out['m_rg_b_a'], 'm_rg_w_x': out['m_rg_w_x'], 'm_rg_b_x': out['m_rg_b_x'], 'm_lru_lambda': out['m_lru_lambda'], 'm_pool_w': out['m_pool_w'], 'm_pool_scale': out['m_pool_scale'], 'm_sgu_norm': out['m_sgu_norm'], 'm_sgu_w': out['m_sgu_w'], 'm_sgu_b': out['m_sgu_b'], 'm_w_out': out['m_w_out'], 'm_ffn2_norm': out['m_ffn2_norm'], 'm_ffn2_w_in': out['m_ffn2_w_in'], 'm_ffn2_w_out': out['m_ffn2_w_out'], 'm_final_norm': out['m_final_norm'], 'v_ffn1_norm': out['v_ffn1_norm'], 'v_ffn1_w_in': out['v_ffn1_w_in'], 'v_ffn1_w_out': out['v_ffn1_w_out'], 'v_mix_norm': out['v_mix_norm'], 'v_w_in': out['v_w_in'], 'v_conv_w': out['v_conv_w'], 'v_conv_b': out['v_conv_b'], 'v_rg_w_a': out['v_rg_w_a'], 'v_rg_b_a': out['v_rg_b_a'], 'v_rg_w_x': out['v_rg_w_x'], 'v_rg_b_x': out['v_rg_b_x'], 'v_lru_lambda': out['v_lru_lambda'], 'v_pool_w': out['v_pool_w'], 'v_pool_scale': out['v_pool_scale'], 'v_sgu_norm': out['v_sgu_norm'], 'v_sgu_w': out['v_sgu_w'], 'v_sgu_b': out['v_sgu_b'], 'v_w_out': out['v_w_out'], 'v_ffn2_norm': out['v_ffn2_norm'], 'v_ffn2_w_in': out['v_ffn2_w_in'], 'v_ffn2_w_out': out['v_ffn2_w_out'], 'v_final_norm': out['v_final_norm']}


def _loss(weights, diff, rest, loss_target):
    with _jax.named_scope("forward"):
        args = {**rest, TWIN_DIFF_INPUT: diff, **{k: w.astype(_WEIGHT_DTYPES[k]) for k, w in weights.items()}}
        y = _forward(args)
    with _jax.named_scope("loss_head"):
        err = _jnp.square(y.astype(_jnp.float32) - loss_target)
        return 0.5 * _jnp.sum(_jnp.mean(err, axis=-1)) if err.ndim else 0.5 * err


def _adamw(w, g, m, v):
    m = ADAM_B1 * m + (1.0 - ADAM_B1) * g
    v = ADAM_B2 * v + (1.0 - ADAM_B2) * _jnp.square(g)
    m_hat = m / (1.0 - ADAM_B1 ** ADAM_STEP)
    v_hat = v / (1.0 - ADAM_B2 ** ADAM_STEP)
    delta = -ADAM_LR * (m_hat / (_jnp.sqrt(v_hat) + ADAM_EPS) + ADAM_WD * w)
    return delta, m, v


def reference(x, ffn1_norm, ffn1_w_in, ffn1_w_out, mix_norm, w_in, conv_w, conv_b, rg_w_a, rg_b_a, rg_w_x, rg_b_x, lru_lambda, pool_w, pool_scale, sgu_norm, sgu_w, sgu_b, w_out, ffn2_norm, ffn2_w_in, ffn2_w_out, final_norm, loss_target, m_ffn1_norm, m_ffn1_w_in, m_ffn1_w_out, m_mix_norm, m_w_in, m_conv_w, m_conv_b, m_rg_w_a, m_rg_b_a, m_rg_w_x, m_rg_b_x, m_lru_lambda, m_pool_w, m_pool_scale, m_sgu_norm, m_sgu_w, m_sgu_b, m_w_out, m_ffn2_norm, m_ffn2_w_in, m_ffn2_w_out, m_final_norm, v_ffn1_norm, v_ffn1_w_in, v_ffn1_w_out, v_mix_norm, v_w_in, v_conv_w, v_conv_b, v_rg_w_a, v_rg_b_a, v_rg_w_x, v_rg_b_x, v_lru_lambda, v_pool_w, v_pool_scale, v_sgu_norm, v_sgu_w, v_sgu_b, v_w_out, v_ffn2_norm, v_ffn2_w_in, v_ffn2_w_out, v_final_norm):
    given = dict(x=x, ffn1_norm=ffn1_norm, ffn1_w_in=ffn1_w_in, ffn1_w_out=ffn1_w_out, mix_norm=mix_norm, w_in=w_in, conv_w=conv_w, conv_b=conv_b, rg_w_a=rg_w_a, rg_b_a=rg_b_a, rg_w_x=rg_w_x, rg_b_x=rg_b_x, lru_lambda=lru_lambda, pool_w=pool_w, pool_scale=pool_scale, sgu_norm=sgu_norm, sgu_w=sgu_w, sgu_b=sgu_b, w_out=w_out, ffn2_norm=ffn2_norm, ffn2_w_in=ffn2_w_in, ffn2_w_out=ffn2_w_out, final_norm=final_norm, loss_target=loss_target, m_ffn1_norm=m_ffn1_norm, m_ffn1_w_in=m_ffn1_w_in, m_ffn1_w_out=m_ffn1_w_out, m_mix_norm=m_mix_norm, m_w_in=m_w_in, m_conv_w=m_conv_w, m_conv_b=m_conv_b, m_rg_w_a=m_rg_w_a, m_rg_b_a=m_rg_b_a, m_rg_w_x=m_rg_w_x, m_rg_b_x=m_rg_b_x, m_lru_lambda=m_lru_lambda, m_pool_w=m_pool_w, m_pool_scale=m_pool_scale, m_sgu_norm=m_sgu_norm, m_sgu_w=m_sgu_w, m_sgu_b=m_sgu_b, m_w_out=m_w_out, m_ffn2_norm=m_ffn2_norm, m_ffn2_w_in=m_ffn2_w_in, m_ffn2_w_out=m_ffn2_w_out, m_final_norm=m_final_norm, v_ffn1_norm=v_ffn1_norm, v_ffn1_w_in=v_ffn1_w_in, v_ffn1_w_out=v_ffn1_w_out, v_mix_norm=v_mix_norm, v_w_in=v_w_in, v_conv_w=v_conv_w, v_conv_b=v_conv_b, v_rg_w_a=v_rg_w_a, v_rg_b_a=v_rg_b_a, v_rg_w_x=v_rg_w_x, v_rg_b_x=v_rg_b_x, v_lru_lambda=v_lru_lambda, v_pool_w=v_pool_w, v_pool_scale=v_pool_scale, v_sgu_norm=v_sgu_norm, v_sgu_w=v_sgu_w, v_sgu_b=v_sgu_b, v_w_out=v_w_out, v_ffn2_norm=v_ffn2_norm, v_ffn2_w_in=v_ffn2_w_in, v_ffn2_w_out=v_ffn2_w_out, v_final_norm=v_final_norm)
    weights = {n: given[n] for n in TWIN_WEIGHTS}
    shared = {n: given[n] for n in SHARED_INPUTS}
    per_example = {n: given[n] for n in ['x']}
    grad_fn = _jax.value_and_grad(_loss, argnums=(0, 1))

    def one_microbatch(ex, loss_target):
        ex = dict(ex)
        diff = ex.pop(TWIN_DIFF_INPUT)
        return grad_fn(weights, diff, {**shared, **ex}, loss_target)

    if N_MICROBATCH == 1:
        loss, (grad_w, grad_x) = one_microbatch(per_example, given["loss_target"])
    else:
        def body(carry, xs):
            loss_sum, grad_sum = carry
            l_k, (gw_k, gx_k) = one_microbatch(xs[0], xs[1])
            with _jax.named_scope("update"):
                return (loss_sum + l_k, _jax.tree.map(_jnp.add, grad_sum, gw_k)), gx_k

        init = (_jnp.zeros((), _jnp.float32), _jax.tree.map(_jnp.zeros_like, weights))
        (loss, grad_w), grad_x = _jax.lax.scan(body, init, (per_example, given["loss_target"]))
    with _jax.named_scope("update"):
        delta_w, new_m, new_v = {}, {}, {}
        for n in TWIN_WEIGHTS:
            delta_w[n], new_m[n], new_v[n] = _adamw(weights[n], grad_w[n], given["m_" + n], given["v_" + n])
    return (loss, grad_x, *[grad_w[n] for n in TWIN_WEIGHTS], *[delta_w[n] for n in TWIN_WEIGHTS],
            *[new_m[n] for n in TWIN_WEIGHTS], *[new_v[n] for n in TWIN_WEIGHTS])
```

```python
import math

import jax
import jax.numpy as jnp
import numpy as np
from jax import lax
from jax.experimental import pallas as pl
from jax.experimental.pallas import tpu as pltpu

F32 = jnp.float32
BF16 = jnp.bfloat16
MESH = pl.DeviceIdType.MESH

D = 1024
D_RNN = 512
D_POOL = 256
D_SGU = 256
D_IN = 1792
D_FF = 2752
D_FFP = 2816
CHUNK = 128
HALO = 16
EPS = 1e-6
LRU_C = 8.0
N_DEV = 8
N_CHIP = 4
LANES = 1024
VMEM_LIMIT = 56 * 1024 * 1024

ADAM_LR, ADAM_B1, ADAM_B2, ADAM_EPS, ADAM_WD, ADAM_STEP = 0.001, 0.9, 0.999, 1e-08, 0.01, 10

BIG = ("ffn1_w_in", "ffn1_w_out", "w_in", "w_out", "ffn2_w_in", "ffn2_w_out")
SMALL = ("ffn1_norm", "mix_norm", "conv_w", "conv_b", "rg_w_a", "rg_b_a", "rg_w_x", "rg_b_x", "lru_lambda",
         "pool_w", "pool_scale", "sgu_norm", "sgu_w", "sgu_b", "ffn2_norm", "final_norm")
WEIGHTS = ("ffn1_norm", "ffn1_w_in", "ffn1_w_out", "mix_norm", "w_in", "conv_w", "conv_b", "rg_w_a", "rg_b_a",
           "rg_w_x", "rg_b_x", "lru_lambda", "pool_w", "pool_scale", "sgu_norm", "sgu_w", "sgu_b", "w_out",
           "ffn2_norm", "ffn2_w_in", "ffn2_w_out", "final_norm")


def _params(*sem):
    return pltpu.CompilerParams(dimension_semantics=sem, vmem_limit_bytes=VMEM_LIMIT)


def _gelu(x):
    c = math.sqrt(2.0 / math.pi)
    t = jnp.tanh(c * (x + 0.044715 * (x * x * x)))
    return 0.5 * x * (1.0 + t)


def _gelu_and_grad(x):
    c = math.sqrt(2.0 / math.pi)
    x2 = x * x
    t = jnp.tanh(c * (x + 0.044715 * (x2 * x)))
    g = 0.5 * x * (1.0 + t)
    dg = 0.5 * (1.0 + t) + 0.5 * x * (1.0 - t * t) * (c * (1.0 + 3.0 * 0.044715 * x2))
    return g, dg


def _sigmoid(x):
    return 1.0 / (1.0 + jnp.exp(-x))


def _dot(a, b):
    return jnp.dot(a, b, preferred_element_type=F32)


def _dot_tn(a, b):
    return lax.dot_general(a, b, (((0,), (0,)), ((), ())), preferred_element_type=F32)


def _dot_nt(a, b):
    return lax.dot_general(a, b, (((1,), (1,)), ((), ())), preferred_element_type=F32)


def _rms_fwd(x, g, tm=512):
    T = x.shape[0]
    tm = min(tm, T)

    def body(x_ref, g_ref, o_ref):
        xv = x_ref[...]
        r = lax.rsqrt(jnp.mean(xv * xv, axis=-1, keepdims=True) + EPS)
        o_ref[...] = (xv * r * g_ref[...]).astype(BF16)

    return pl.pallas_call(
        body, name="rms_fwd", grid=(T // tm,),
        in_specs=[pl.BlockSpec((tm, D), lambda i: (i, 0)), pl.BlockSpec((1, D), lambda i: (0, 0))],
        out_specs=pl.BlockSpec((tm, D), lambda i: (i, 0)),
        out_shape=jax.ShapeDtypeStruct((T, D), BF16),
        compiler_params=_params("parallel"),
    )(x, g)


def _rms_bwd(x, dh, g, dres, tm=512):
    T = x.shape[0]
    tm = min(tm, T)

    def body(x_ref, dh_ref, g_ref, dres_ref, dx_ref, dxb_ref, dg_ref):
        xv = x_ref[...]
        r = lax.rsqrt(jnp.mean(xv * xv, axis=-1, keepdims=True) + EPS)
        xhat = xv * r
        dy = dh_ref[...]
        dxhat = dy * g_ref[...]
        dx = dres_ref[...] + r * (dxhat - xhat * jnp.mean(dxhat * xhat, axis=-1, keepdims=True))
        dx_ref[...] = dx
        dxb_ref[...] = dx.astype(BF16)

        @pl.when(pl.program_id(0) == 0)
        def _():
            dg_ref[...] = jnp.zeros_like(dg_ref)

        dg_ref[...] += jnp.sum(dy * xhat, axis=0, keepdims=True)

    row = pl.BlockSpec((tm, D), lambda i: (i, 0))
    vec = pl.BlockSpec((1, D), lambda i: (0, 0))
    return pl.pallas_call(
        body, name="rms_bwd", grid=(T // tm,),
        in_specs=[row, row, vec, row], out_specs=[row, row, vec],
        out_shape=[jax.ShapeDtypeStruct((T, D), F32), jax.ShapeDtypeStruct((T, D), BF16),
                   jax.ShapeDtypeStruct((1, D), F32)],
        compiler_params=_params("arbitrary"),
    )(x, dh, g, dres)


def _tile(n, limit):
    best = 128
    for t in range(128, min(n, limit) + 1, 128):
        if n % t == 0:
            best = t
    assert n % best == 0, (n, limit)
    return best


def _mm(a, b, out_dtype, name, tm=1024, tn=512):
    M, K = a.shape
    N = b.shape[1]
    tm, tn = min(tm, M), _tile(N, tn)

    def body(a_ref, b_ref, o_ref):
        o_ref[...] = _dot(a_ref[...], b_ref[...]).astype(out_dtype)

    return pl.pallas_call(
        body, name=name, grid=(M // tm, N // tn),
        in_specs=[pl.BlockSpec((tm, K), lambda i, j: (i, 0)), pl.BlockSpec((K, tn), lambda i, j: (0, j))],
        out_specs=pl.BlockSpec((tm, tn), lambda i, j: (i, j)),
        out_shape=jax.ShapeDtypeStruct((M, N), out_dtype),
        compiler_params=_params("parallel", "parallel"),
    )(a, b)


def _mm_res(a, b, res, scale, name, tm=1024, tn=512):
    M, K = a.shape
    N = b.shape[1]
    tm, tn = min(tm, M), _tile(N, tn)

    def body(a_ref, b_ref, r_ref, o_ref):
        o_ref[...] = r_ref[...] + scale * _dot(a_ref[...], b_ref[...])

    return pl.pallas_call(
        body, name=name, grid=(M // tm, N // tn),
        in_specs=[pl.BlockSpec((tm, K), lambda i, j: (i, 0)), pl.BlockSpec((K, tn), lambda i, j: (0, j)),
                  pl.BlockSpec((tm, tn), lambda i, j: (i, j))],
        out_specs=pl.BlockSpec((tm, tn), lambda i, j: (i, j)),
        out_shape=jax.ShapeDtypeStruct((M, N), F32),
        compiler_params=_params("parallel", "parallel"),
    )(a, b, res)


def _mm_nt(a, b, name, tm=1024, tn=512):
    M, K = a.shape
    N = b.shape[0]
    tm, tn = min(tm, M), _tile(N, tn)

    def body(a_ref, b_ref, o_ref):
        o_ref[...] = _dot_nt(a_ref[...], b_ref[...])

    return pl.pallas_call(
        body, name=name, grid=(M // tm, N // tn),
        in_specs=[pl.BlockSpec((tm, K), lambda i, j: (i, 0)), pl.BlockSpec((tn, K), lambda i, j: (j, 0))],
        out_specs=pl.BlockSpec((tm, tn), lambda i, j: (i, j)),
        out_shape=jax.ShapeDtypeStruct((M, N), F32),
        compiler_params=_params("parallel", "parallel"),
    )(a, b)


def _mm2_nt(a1, b1, a2, b2, name, tm=1024, tn=512):
    M, K = a1.shape
    N = b1.shape[0]
    tm, tn = min(tm, M), _tile(N, tn)

    def body(a1_ref, b1_ref, a2_ref, b2_ref, o_ref):
        o_ref[...] = _dot_nt(a1_ref[...], b1_ref[...]) + _dot_nt(a2_ref[...], b2_ref[...])

    aspec = pl.BlockSpec((tm, K), lambda i, j: (i, 0))
    bspec = pl.BlockSpec((tn, K), lambda i, j: (j, 0))
    return pl.pallas_call(
        body, name=name, grid=(M // tm, N // tn),
        in_specs=[aspec, bspec, aspec, bspec],
        out_specs=pl.BlockSpec((tm, tn), lambda i, j: (i, j)),
        out_shape=jax.ShapeDtypeStruct((M, N), F32),
        compiler_params=_params("parallel", "parallel"),
    )(a1, b1, a2, b2)


def _mm_tn(a, b, scale, name, tm=1792, tn=1792, tk=512):
    T, M = a.shape
    N = b.shape[1]
    tm, tn, tk = _tile(M, tm), _tile(N, tn), min(tk, T)
    nk = T // tk

    def body(a_ref, b_ref, o_ref, acc_ref):
        k = pl.program_id(2)

        @pl.when(k == 0)
        def _():
            acc_ref[...] = jnp.zeros_like(acc_ref)

        acc_ref[...] += _dot_tn(a_ref[...], b_ref[...])

        @pl.when(k == nk - 1)
        def _():
            o_ref[...] = (scale * acc_ref[...]).astype(BF16)

    return pl.pallas_call(
        body, name=name, grid=(M // tm, N // tn, nk),
        in_specs=[pl.BlockSpec((tk, tm), lambda i, j, k: (k, i)), pl.BlockSpec((tk, tn), lambda i, j, k: (k, j))],
        out_specs=pl.BlockSpec((tm, tn), lambda i, j, k: (i, j)),
        out_shape=jax.ShapeDtypeStruct((M, N), BF16),
        scratch_shapes=[pltpu.VMEM((tm, tn), F32)],
        compiler_params=_params("parallel", "parallel", "arbitrary"),
    )(a, b)


def _ffn_in(h, wg, wu, tm=1024, tn=256):
    T = h.shape[0]
    tm = min(tm, T)

    def body(h_ref, wg_ref, wu_ref, g_ref, u_ref, a_ref):
        hv = h_ref[...]
        g = _dot(hv, wg_ref[...])
        u = _dot(hv, wu_ref[...])
        g_ref[...] = g.astype(BF16)
        u_ref[...] = u.astype(BF16)
        a_ref[...] = (g * _sigmoid(g) * u).astype(BF16)

    wspec = pl.BlockSpec((D, tn), lambda i, j: (0, j))
    ospec = pl.BlockSpec((tm, tn), lambda i, j: (i, j))
    oshape = jax.ShapeDtypeStruct((T, D_FFP), BF16)
    return pl.pallas_call(
        body, name="ffn_in", grid=(T // tm, D_FFP // tn),
        in_specs=[pl.BlockSpec((tm, D), lambda i, j: (i, 0)), wspec, wspec],
        out_specs=[ospec, ospec, ospec], out_shape=[oshape, oshape, oshape],
        compiler_params=_params("parallel", "parallel"),
    )(h, wg, wu)


def _ffn_mid_bwd(dy, wout, g, u, tm=1024, tn=256):
    T = dy.shape[0]
    tm = min(tm, T)

    def body(dy_ref, w_ref, g_ref, u_ref, dg_ref, du_ref):
        da = 0.5 * _dot_nt(dy_ref[...], w_ref[...])
        g = g_ref[...].astype(F32)
        u = u_ref[...].astype(F32)
        s = _sigmoid(g)
        dg_ref[...] = (da * u * (s * (1.0 + g * (1.0 - s)))).astype(BF16)
        du_ref[...] = (da * (g * s)).astype(BF16)

    ospec = pl.BlockSpec((tm, tn), lambda i, j: (i, j))
    oshape = jax.ShapeDtypeStruct((T, D_FFP), BF16)
    return pl.pallas_call(
        body, name="ffn_mid_bwd", grid=(T // tm, D_FFP // tn),
        in_specs=[pl.BlockSpec((tm, D), lambda i, j: (i, 0)), pl.BlockSpec((tn, D), lambda i, j: (j, 0)),
                  ospec, ospec],
        out_specs=[ospec, ospec], out_shape=[oshape, oshape],
        compiler_params=_params("parallel", "parallel"),
    )(dy, wout, g, u)


def _final(x, tgt, gf, tm=512):
    T = x.shape[0]
    tm = min(tm, T)

    def body(x_ref, t_ref, g_ref, dx_ref, dxb_ref, dg_ref, loss_ref):
        xv = x_ref[...]
        r = lax.rsqrt(jnp.mean(xv * xv, axis=-1, keepdims=True) + EPS)
        xhat = xv * r
        err = xhat * g_ref[...] - t_ref[...]
        dy = err * (1.0 / D)
        dxhat = dy * g_ref[...]
        dx = r * (dxhat - xhat * jnp.mean(dxhat * xhat, axis=-1, keepdims=True))
        dx_ref[...] = dx
        dxb_ref[...] = dx.astype(BF16)

        @pl.when(pl.program_id(0) == 0)
        def _():
            dg_ref[...] = jnp.zeros_like(dg_ref)
            loss_ref[...] = jnp.zeros_like(loss_ref)

        dg_ref[...] += jnp.sum(dy * xhat, axis=0, keepdims=True)
        loss_ref[...] += (0.5 / D) * jnp.sum(err * err)

    row = pl.BlockSpec((tm, D), lambda i: (i, 0))
    vec = pl.BlockSpec((1, D), lambda i: (0, 0))
    return pl.pallas_call(
        body, name="final_loss", grid=(T // tm,),
        in_specs=[row, row, vec],
        out_specs=[row, row, vec, pl.BlockSpec((1, 128), lambda i: (0, 0))],
        out_shape=[jax.ShapeDtypeStruct((T, D), F32), jax.ShapeDtypeStruct((T, D), BF16),
                   jax.ShapeDtypeStruct((1, D), F32), jax.ShapeDtypeStruct((1, 128), F32)],
        compiler_params=_params("arbitrary"),
    )(x, tgt, gf)


def _mix_block(T, limit):
    return min(limit, T // 2)


def _rows(tb, width):
    return lax.broadcasted_iota(jnp.int32, (tb, width), 0)


def _rglru_gates(xc, wax_ref, bax_ref, lam_ref):
    pre = _dot(xc.astype(BF16), wax_ref[...]) + bax_ref[...]
    r = _sigmoid(pre[:, :D_RNN])
    ig = _sigmoid(pre[:, D_RNN:])
    z = -lam_ref[...]
    sp = jnp.maximum(z, 0.0) + jnp.log(1.0 + jnp.exp(-jnp.abs(z)))
    log_a = (-LRU_C) * r * sp
    a = jnp.exp(log_a)
    mult = jnp.sqrt(-jnp.tanh(log_a) * (1.0 + a * a))
    return r, ig, sp, a, mult


def _conv(xa_ext, cw_ref, cb_ref):
    y = cb_ref[...] + cw_ref[3:4, :] * xa_ext
    for k in range(1, 4):
        y = y + cw_ref[3 - k:4 - k, :] * pltpu.roll(xa_ext, k, 0)
    return y[HALO:]


def _pool_window_lanes():
    lane = lax.broadcasted_iota(jnp.int32, (1, D_POOL), 1)
    return jnp.where(lane < 64, 2, jnp.where(lane < 128, 4, jnp.where(lane < 192, 8, 16)))


def _pool_select(s2, s4, s8, s16):
    lane = lax.broadcasted_iota(jnp.int32, s2.shape, 1)
    return jnp.where(lane < 64, s2, jnp.where(lane < 128, s4, jnp.where(lane < 192, s8, s16)))


def _pool_diff(xp_ext, t0, tb):
    s2 = xp_ext + pltpu.roll(xp_ext, 1, 0)
    s4 = s2 + pltpu.roll(s2, 2, 0)
    s8 = s4 + pltpu.roll(s4, 4, 0)
    s16 = s8 + pltpu.roll(s8, 8, 0)
    sel = _pool_select(s2, s4, s8, s16)[HALO:]
    cnt = jnp.minimum(t0 + _rows(tb, D_POOL) + 1, _pool_window_lanes()).astype(F32)
    return sel / cnt - xp_ext[HALO:], cnt


def _head_masks():
    lane = lax.broadcasted_iota(jnp.int32, (1, D_SGU), 1)
    return [((lane >= 64 * h) & (lane < 64 * (h + 1))).astype(F32) for h in range(4)]


def _sgu_mix(w_ref, vch, masks):
    z = masks[0] * _dot(w_ref[0], vch)
    for h in range(1, 4):
        z = z + masks[h] * _dot(w_ref[h], vch)
    return z


def _mix_fwd(p, prm):
    T = p.shape[0]
    tb = _mix_block(T, 512)
    nb = T // tb

    def body(p_ref, xah_ref, xph_ref, cw_ref, cb_ref, wax_ref, bax_ref, lam_ref, wp_ref, ps_ref, sgn_ref,
             ws_ref, bz_ref, y_ref, hs_ref, carry_ref):
        i = pl.program_id(0)
        keep = (i > 0).astype(F32)

        @pl.when(i == 0)
        def _():
            carry_ref[...] = jnp.zeros_like(carry_ref)

        xa_ext = jnp.concatenate([xah_ref[...] * keep, p_ref[:, 512:1024]], axis=0)
        xc = _conv(xa_ext, cw_ref, cb_ref)
        r, ig, sp, a, mult = _rglru_gates(xc, wax_ref, bax_ref, lam_ref)
        bv = mult * (ig * xc)
        row = _rows(tb, D_RNN)
        s = 1
        while s < tb:
            m = row >= s
            bv = jnp.where(m, a * pltpu.roll(bv, s, 0) + bv, bv)
            a = jnp.where(m, a * pltpu.roll(a, s, 0), a)
            s *= 2
        h = bv + a * carry_ref[0:1, :]
        hs_ref[...] = h
        last = jnp.sum(jnp.where(_rows(8, D_RNN) == 7, hs_ref[tb - 8:tb, :], 0.0), axis=0, keepdims=True)
        carry_ref[...] = jnp.broadcast_to(last, carry_ref.shape)
        y_ref[:, 0:512] = (_gelu(p_ref[:, 0:512]) * h).astype(BF16)

        xp_ext = jnp.concatenate([xph_ref[...] * keep, p_ref[:, 1024:1280]], axis=0)
        d, _ = _pool_diff(xp_ext, i * tb, tb)
        y_ref[:, 512:768] = (_dot(d.astype(BF16), wp_ref[...]) * ps_ref[...]).astype(BF16)

        ug = _gelu(p_ref[:, 1280:1536])
        vg = _gelu(p_ref[:, 1536:1792])
        rv = lax.rsqrt(jnp.mean(vg * vg, axis=-1, keepdims=True) + EPS)
        vn = (vg * rv * sgn_ref[...]).astype(BF16)
        masks = _head_masks()
        for ci in range(tb // CHUNK):
            sl = slice(ci * CHUNK, (ci + 1) * CHUNK)
            z = _sgu_mix(ws_ref, vn[sl], masks) + bz_ref[...]
            y_ref[sl, 768:1024] = (ug[sl] * z).astype(BF16)

    hb = tb // HALO

    def halo(i):
        return jnp.maximum(i * hb - 1, 0)

    def full(shape):
        return pl.BlockSpec(shape, lambda i: (0,) * len(shape))

    return pl.pallas_call(
        body, name="mix_fwd", grid=(nb,),
        in_specs=[pl.BlockSpec((tb, D_IN), lambda i: (i, 0)),
                  pl.BlockSpec((HALO, D_RNN), lambda i: (halo(i), 1)),
                  pl.BlockSpec((HALO, D_POOL), lambda i: (halo(i), 4)),
                  full((4, D_RNN)), full((1, D_RNN)), full((D_RNN, 2 * D_RNN)), full((1, 2 * D_RNN)),
                  full((1, D_RNN)), full((D_POOL, D_POOL)), full((1, D_POOL)), full((1, D_SGU)),
                  full((4, CHUNK, CHUNK)), full((CHUNK, D_SGU))],
        out_specs=[pl.BlockSpec((tb, D), lambda i: (i, 0)), pl.BlockSpec((tb, D_RNN), lambda i: (i, 0))],
        out_shape=[jax.ShapeDtypeStruct((T, D), BF16), jax.ShapeDtypeStruct((T, D_RNN), F32)],
        scratch_shapes=[pltpu.VMEM((8, D_RNN), F32)],
        compiler_params=_params("arbitrary"),
    )(p, p, p, prm["conv_w"], prm["conv_b"], prm["wax"], prm["bax"], prm["lam"], prm["wpool"], prm["pool_scale"],
      prm["sgu_norm"], prm["ws"], prm["bz"])


def _mix_bwd(dy, p, hs, prm):
    T = p.shape[0]
    tb = _mix_block(T, 256)
    nb = T // tb
    hb = tb // HALO

    def body(dy_ref, p_ref, xah_ref, xph_ref, hs_ref, hsh_ref, cw_ref, cb_ref, wax_ref, waxt_ref, bax_ref,
             lam_ref, wp_ref, wpt_ref, ps_ref, sgn_ref, ws_ref, wst_ref, bz_ref,
             dp_ref, dcw_ref, dcb_ref, dwax_ref, dbax_ref, dlam_ref, dwp_ref, dps_ref, dsgn_ref, dws_ref,
             dbz_ref, gcarry_ref, xcfut_ref, mfut_ref):
        i = pl.program_id(0)
        bi = nb - 1 - i
        keep = (bi > 0).astype(F32)

        @pl.when(i == 0)
        def _():
            for ref in (dcw_ref, dcb_ref, dwax_ref, dbax_ref, dlam_ref, dwp_ref, dps_ref, dsgn_ref, dws_ref,
                        dbz_ref, gcarry_ref, xcfut_ref, mfut_ref):
                ref[...] = jnp.zeros_like(ref)

        xa_ext = jnp.concatenate([xah_ref[...] * keep, p_ref[:, 512:1024]], axis=0)
        xc = _conv(xa_ext, cw_ref, cb_ref)
        r, ig, sp, a, mult = _rglru_gates(xc, wax_ref, bax_ref, lam_ref)
        gg, dgg = _gelu_and_grad(p_ref[:, 0:512])
        dya = dy_ref[:, 0:512]
        dp_ref[:, 0:512] = (dya * hs_ref[...] * dgg).astype(BF16)
        row = _rows(tb, D_RNN)
        g = dya * gg + jnp.where(row == tb - 1, gcarry_ref[0:1, :], 0.0)
        al = pltpu.roll(a, tb - 1, 0)
        s = 1
        while s < tb:
            m = row < tb - s
            g = jnp.where(m, al * pltpu.roll(g, tb - s, 0) + g, g)
            al = jnp.where(m, al * pltpu.roll(al, tb - s, 0), al)
            s *= 2
        first = jnp.sum(jnp.where(_rows(8, D_RNN) == 0, (a * g)[0:8], 0.0), axis=0, keepdims=True)
        gcarry_ref[...] = jnp.broadcast_to(first, gcarry_ref.shape)
        hs_ext = jnp.concatenate([hsh_ref[...] * keep, hs_ref[...]], axis=0)
        h_prev = pltpu.roll(hs_ext, 1, 0)[HALO:]
        ix = ig * xc
        dlog_a = g * h_prev * a - (g * ix) * (a * a / mult)
        dlam_ref[...] += jnp.sum(dlog_a * r, axis=0, keepdims=True) * (LRU_C * _sigmoid(-lam_ref[...]))
        dpre_r = dlog_a * ((-LRU_C) * sp) * (r * (1.0 - r))
        dpre_i = (g * mult * xc) * (ig * (1.0 - ig))
        dpre = jnp.concatenate([dpre_r, dpre_i], axis=1)
        dbax_ref[...] += jnp.sum(dpre, axis=0, keepdims=True)
        dpre_b = dpre.astype(BF16)
        dwax_ref[...] += _dot_tn(xc.astype(BF16), dpre_b)
        dxc = g * mult * ig + _dot(dpre_b, waxt_ref[...])
        dcb_ref[...] += jnp.sum(dxc, axis=0, keepdims=True)
        for k in range(4):
            xs = xa_ext[HALO:] if k == 3 else pltpu.roll(xa_ext, 3 - k, 0)[HALO:]
            dcw_ref[k:k + 1, :] += jnp.sum(dxc * xs, axis=0, keepdims=True)
        dxc_ext = jnp.concatenate([dxc, xcfut_ref[...]], axis=0)
        n = tb + HALO
        dxa = cw_ref[3:4, :] * dxc_ext
        for k in range(1, 4):
            dxa = dxa + cw_ref[3 - k:4 - k, :] * pltpu.roll(dxc_ext, n - k, 0)
        dp_ref[:, 512:1024] = dxa[:tb].astype(BF16)
        xcfut_ref[...] = dxc[0:HALO]

        xp_ext = jnp.concatenate([xph_ref[...] * keep, p_ref[:, 1024:1280]], axis=0)
        d, cnt = _pool_diff(xp_ext, bi * tb, tb)
        db = d.astype(BF16)
        dyb = dy_ref[:, 512:768]
        dps_ref[...] += jnp.sum(dyb * _dot(db, wp_ref[...]), axis=0, keepdims=True)
        dq = (dyb * ps_ref[...]).astype(BF16)
        dwp_ref[...] += _dot_tn(db, dq)
        dd = _dot(dq, wpt_ref[...])
        mm = dd / cnt
        m_ext = jnp.concatenate([mm, mfut_ref[...]], axis=0)
        f2 = m_ext + pltpu.roll(m_ext, n - 1, 0)
        f4 = f2 + pltpu.roll(f2, n - 2, 0)
        f8 = f4 + pltpu.roll(f4, n - 4, 0)
        f16 = f8 + pltpu.roll(f8, n - 8, 0)
        dp_ref[:, 1024:1280] = (_pool_select(f2, f4, f8, f16)[:tb] - dd).astype(BF16)
        mfut_ref[...] = mm[0:HALO]

        ug, dug = _gelu_and_grad(p_ref[:, 1280:1536])
        vg, dvg = _gelu_and_grad(p_ref[:, 1536:1792])
        rv = lax.rsqrt(jnp.mean(vg * vg, axis=-1, keepdims=True) + EPS)
        vhat = vg * rv
        vn = (vhat * sgn_ref[...]).astype(BF16)
        dyc = dy_ref[:, 768:1024]
        masks = _head_masks()
        dz = dyc * ug
        dzb = dz.astype(BF16)
        dvn_parts = []
        for ci in range(tb // CHUNK):
            sl = slice(ci * CHUNK, (ci + 1) * CHUNK)
            z = _sgu_mix(ws_ref, vn[sl], masks) + bz_ref[...]
            dp_ref[sl, 1280:1536] = (dyc[sl] * z * dug[sl]).astype(BF16)
            dbz_ref[...] += dz[sl]
            for h in range(4):
                dws_ref[h] += _dot_nt((dz[sl] * masks[h]).astype(BF16), vn[sl])
            dvn_parts.append(_sgu_mix(wst_ref, dzb[sl], masks))
        dvn = jnp.concatenate(dvn_parts, axis=0)
        dsgn_ref[...] += jnp.sum(dvn * vhat, axis=0, keepdims=True)
        dvhat = dvn * sgn_ref[...]
        dvg_in = rv * (dvhat - vhat * jnp.mean(dvhat * vhat, axis=-1, keepdims=True))
        dp_ref[:, 1536:1792] = (dvg_in * dvg).astype(BF16)

        @pl.when(i == nb - 1)
        def _():
            tril = (lax.broadcasted_iota(jnp.int32, (CHUNK, CHUNK), 0)
                    >= lax.broadcasted_iota(jnp.int32, (CHUNK, CHUNK), 1)).astype(F32)
            for h in range(4):
                dws_ref[h] = dws_ref[h] * tril

    def blk(i):
        return nb - 1 - i

    def halo(i):
        return jnp.maximum(blk(i) * hb - 1, 0)

    def full(shape):
        return pl.BlockSpec(shape, lambda i: (0,) * len(shape))

    small_shapes = [(4, D_RNN), (1, D_RNN), (D_RNN, 2 * D_RNN), (1, 2 * D_RNN), (1, D_RNN), (D_POOL, D_POOL),
                    (1, D_POOL), (1, D_SGU), (4, CHUNK, CHUNK), (CHUNK, D_SGU)]
    outs = pl.pallas_call(
        body, name="mix_bwd", grid=(nb,),
        in_specs=[pl.BlockSpec((tb, D), lambda i: (blk(i), 0)),
                  pl.BlockSpec((tb, D_IN), lambda i: (blk(i), 0)),
                  pl.BlockSpec((HALO, D_RNN), lambda i: (halo(i), 1)),
                  pl.BlockSpec((HALO, D_POOL), lambda i: (halo(i), 4)),
                  pl.BlockSpec((tb, D_RNN), lambda i: (blk(i), 0)),
                  pl.BlockSpec((HALO, D_RNN), lambda i: (halo(i), 0)),
                  full((4, D_RNN)), full((1, D_RNN)), full((D_RNN, 2 * D_RNN)), full((2 * D_RNN, D_RNN)),
                  full((1, 2 * D_RNN)), full((1, D_RNN)), full((D_POOL, D_POOL)), full((D_POOL, D_POOL)),
                  full((1, D_POOL)), full((1, D_SGU)), full((4, CHUNK, CHUNK)), full((4, CHUNK, CHUNK)),
                  full((CHUNK, D_SGU))],
        out_specs=[pl.BlockSpec((tb, D_IN), lambda i: (blk(i), 0))] + [full(s) for s in small_shapes],
        out_shape=[jax.ShapeDtypeStruct((T, D_IN), BF16)] + [jax.ShapeDtypeStruct(s, F32) for s in small_shapes],
        scratch_shapes=[pltpu.VMEM((8, D_RNN), F32), pltpu.VMEM((HALO, D_RNN), F32),
                        pltpu.VMEM((HALO, D_POOL), F32)],
        compiler_params=_params("arbitrary"),
    )(dy, p, p, p, hs, hs, prm["conv_w"], prm["conv_b"], prm["wax"], prm["wax_t"], prm["bax"], prm["lam"],
      prm["wpool"], prm["wpool_t"], prm["pool_scale"], prm["sgu_norm"], prm["ws"], prm["ws_t"], prm["bz"])
    names = ("dp", "conv_w", "conv_b", "wax", "bax", "lam", "wpool", "pool_scale", "sgu_norm", "ws", "bz")
    return dict(zip(names, outs))


ANY = pl.BlockSpec(memory_space=pl.ANY)


def _place():
    x, y, c = lax.axis_index("x"), lax.axis_index("y"), lax.axis_index("c")
    return x, y, c


def _all_to_all(xs, name):
    def body(in_ref, out_ref, send_sems, recv_sems, local_sem):
        x, y, c = _place()
        me = 4 * x + 2 * y + c
        mine = pltpu.make_async_copy(in_ref.at[me], out_ref.at[me], local_sem)
        mine.start()
        copies = []
        for rel in range(1, N_DEV):
            tx = 1 - x if rel & 4 else x
            ty = 1 - y if rel & 2 else y
            tc = 1 - c if rel & 1 else c
            cp = pltpu.make_async_remote_copy(
                src_ref=in_ref.at[4 * tx + 2 * ty + tc], dst_ref=out_ref.at[me],
                send_sem=send_sems.at[rel - 1], recv_sem=recv_sems.at[rel - 1],
                device_id=(tx, ty, tc), device_id_type=MESH)
            cp.start()
            copies.append(cp)
        for cp in copies:
            cp.wait()
        mine.wait()

    return pl.pallas_call(
        body, name=name, in_specs=[ANY], out_specs=ANY,
        out_shape=jax.ShapeDtypeStruct(xs.shape, xs.dtype),
        scratch_shapes=[pltpu.SemaphoreType.DMA((N_DEV - 1,)), pltpu.SemaphoreType.DMA((N_DEV - 1,)),
                        pltpu.SemaphoreType.DMA],
    )(xs)


def _all_gather8(xs, name):
    def body(x_ref, out_ref, send_sems, recv_sems, local_sem):
        x, y, c = _place()
        me, sibling = (x, y, c), (x, y, 1 - c)
        chips = [(1 - x, y), (x, 1 - y), (1 - x, 1 - y)]

        def rows(px, py, pc):
            return out_ref.at[4 * px + 2 * py + pc]

        def copy(k, block, to, src=None):
            return pltpu.make_async_remote_copy(
                src_ref=rows(*block) if src is None else src, dst_ref=rows(*block),
                send_sem=send_sems.at[k], recv_sem=recv_sems.at[k], device_id=to, device_id_type=MESH)

        mine = pltpu.make_async_copy(x_ref, rows(*me), local_sem)
        mine.start()
        first = [copy(0, me, sibling, src=x_ref)]
        first += [copy(1 + j, me, (*chip, c), src=x_ref) for j, chip in enumerate(chips)]
        for cp in first:
            cp.start()
        passed = [copy(4 + j, (*chip, c), sibling) for j, chip in enumerate(chips)]
        for j, chip in enumerate(chips):
            copy(1 + j, (*chip, c), me).wait_recv()
            passed[j].start()
        copy(0, sibling, me).wait_recv()
        for j, chip in enumerate(chips):
            copy(4 + j, (*chip, 1 - c), me).wait_recv()
        for cp in first + passed:
            cp.wait_send()
        mine.wait()

    return pl.pallas_call(
        body, name=name, in_specs=[ANY], out_specs=ANY,
        out_shape=jax.ShapeDtypeStruct((N_DEV,) + xs.shape, xs.dtype),
        scratch_shapes=[pltpu.SemaphoreType.DMA((7,)), pltpu.SemaphoreType.DMA((7,)), pltpu.SemaphoreType.DMA],
    )(xs)


def _gather_big(shards):
    nt = len(shards)

    def body(*refs):
        in_refs, out_refs = refs[:nt], refs[nt:2 * nt]
        send_sems, recv_sems, local_sems = refs[2 * nt:]
        x, y, c = _place()
        me, sibling = (x, y, c), (x, y, 1 - c)
        chips = [(1 - x, y), (x, 1 - y), (1 - x, 1 - y)]

        def block(t, px, py, pc):
            return out_refs[t].at[2 * px + py, pc]

        def copy(t, k, blk, to, src=None):
            return pltpu.make_async_remote_copy(
                src_ref=block(t, *blk) if src is None else src, dst_ref=block(t, *blk),
                send_sem=send_sems.at[7 * t + k], recv_sem=recv_sems.at[7 * t + k], device_id=to,
                device_id_type=MESH)

        mine, first, passed = [], [], []
        for t in range(nt):
            own = in_refs[t].at[c]
            mine.append(pltpu.make_async_copy(own, block(t, *me), local_sems.at[t]))
            first.append(copy(t, 0, me, sibling, src=own))
            first += [copy(t, 1 + j, me, (*chip, c), src=own) for j, chip in enumerate(chips)]
        for cp in mine + first:
            cp.start()
        for j, chip in enumerate(chips):
            for t in range(nt):
                copy(t, 1 + j, (*chip, c), me).wait_recv()
                fwd = copy(t, 4 + j, (*chip, c), sibling)
                fwd.start()
                passed.append(fwd)
        for t in range(nt):
            copy(t, 0, sibling, me).wait_recv()
            for j, chip in enumerate(chips):
                copy(t, 4 + j, (*chip, 1 - c), me).wait_recv()
        for cp in first + passed:
            cp.wait_send()
        for cp in mine:
            cp.wait()

    return pl.pallas_call(
        body, name="gather_weights", in_specs=[ANY] * nt, out_specs=[ANY] * nt,
        out_shape=[jax.ShapeDtypeStruct((N_CHIP,) + s.shape, s.dtype) for s in shards],
        scratch_shapes=[pltpu.SemaphoreType.DMA((7 * nt,)), pltpu.SemaphoreType.DMA((7 * nt,)),
                        pltpu.SemaphoreType.DMA((nt,))],
    )(*shards)


def _exchange_big(sources, shapes):
    arrays = []
    for per_dev in sources:
        for arr, _ in per_dev:
            if not any(arr is a for a in arrays):
                arrays.append(arr)
    na, nt = len(arrays), len(sources)

    def index_of(arr):
        return next(i for i, a in enumerate(arrays) if a is arr)

    def body(*refs):
        in_refs, out_refs = refs[:na], refs[na:na + nt]
        send_sems, recv_sems = refs[na + nt:]
        x, y, c = _place()
        me = 4 * x + 2 * y + c
        for t in range(nt):
            for k in range(N_DEV):
                arr, window = sources[t][k]
                src = window(in_refs[index_of(arr)])
                to = (k // 4, (k // 2) % 2, k % 2)

                @pl.when(me == k)
                def _():
                    pltpu.make_async_copy(src, out_refs[t].at[k], send_sems.at[N_DEV * t + k]).start()

                @pl.when(me != k)
                def _():
                    pltpu.make_async_remote_copy(
                        src_ref=src, dst_ref=out_refs[t].at[me], send_sem=send_sems.at[N_DEV * t + k],
                        recv_sem=recv_sems.at[N_DEV * t + me], device_id=to, device_id_type=MESH).start()
        for t in range(nt):
            for k in range(N_DEV):
                arr, window = sources[t][k]
                src = window(in_refs[index_of(arr)])

                @pl.when(me == k)
                def _():
                    pltpu.make_async_copy(src, out_refs[t].at[k], send_sems.at[N_DEV * t + k]).wait()

                @pl.when(me != k)
                def _():
                    cp = pltpu.make_async_remote_copy(
                        src_ref=src, dst_ref=out_refs[t].at[k], send_sem=send_sems.at[N_DEV * t + k],
                        recv_sem=recv_sems.at[N_DEV * t + k], device_id=(k // 4, (k // 2) % 2, k % 2),
                        device_id_type=MESH)
                    cp.wait_send()
                    cp.wait_recv()

    return pl.pallas_call(
        body, name="exchange_big_grads", in_specs=[ANY] * na, out_specs=[ANY] * nt,
        out_shape=[jax.ShapeDtypeStruct((N_DEV,) + s, BF16) for s in shapes],
        scratch_shapes=[pltpu.SemaphoreType.DMA((N_DEV * nt,)), pltpu.SemaphoreType.DMA((N_DEV * nt,))],
    )(*arrays)


def _share_pair(xs):
    nt = len(xs)

    def body(*refs):
        in_refs, out_refs = refs[:nt], refs[nt:2 * nt]
        send_sems, recv_sems, local_sems = refs[2 * nt:]
        x, y, c = _place()
        copies = []
        for t in range(nt):
            copies.append(pltpu.make_async_copy(in_refs[t], out_refs[t].at[c], local_sems.at[t]))
            copies.append(pltpu.make_async_remote_copy(
                src_ref=in_refs[t], dst_ref=out_refs[t].at[c], send_sem=send_sems.at[t],
                recv_sem=recv_sems.at[t], device_id=(x, y, 1 - c), device_id_type=MESH))
        for cp in copies:
            cp.start()
        for cp in copies:
            cp.wait()

    return pl.pallas_call(
        body, name="share_big_grads", in_specs=[ANY] * nt, out_specs=[ANY] * nt,
        out_shape=[jax.ShapeDtypeStruct((2,) + a.shape, a.dtype) for a in xs],
        scratch_shapes=[pltpu.SemaphoreType.DMA((nt,)), pltpu.SemaphoreType.DMA((nt,)),
                        pltpu.SemaphoreType.DMA((nt,))],
    )(*xs)


def _sum8(xs, name):
    _, r, cols = xs.shape
    tr = 8
    for cand in (256, 128, 64, 32, 16):
        if r % cand == 0:
            tr = cand
            break

    def body(x_ref, o_ref):
        acc = x_ref[0].astype(F32)
        for k in range(1, N_DEV):
            acc = acc + x_ref[k].astype(F32)
        o_ref[...] = acc

    return pl.pallas_call(
        body, name=name, grid=(r // tr,),
        in_specs=[pl.BlockSpec((N_DEV, tr, cols), lambda i: (0, i, 0))],
        out_specs=pl.BlockSpec((tr, cols), lambda i: (i, 0)),
        out_shape=jax.ShapeDtypeStruct((r, cols), F32),
        compiler_params=_params("parallel"),
    )(xs)


def _adamw(w, g, m, v, name):
    R, C = w.shape
    tr = R
    for cand in (256, 128, 64, 32, 16, 8):
        if R % cand == 0:
            tr = cand
            break
    c1 = 1.0 / (1.0 - ADAM_B1 ** ADAM_STEP)
    c2 = 1.0 / (1.0 - ADAM_B2 ** ADAM_STEP)

    def body(w_ref, g_ref, m_ref, v_ref, d_ref, nm_ref, nv_ref):
        gv = g_ref[...]
        nm = ADAM_B1 * m_ref[...] + (1.0 - ADAM_B1) * gv
        nv = ADAM_B2 * v_ref[...] + (1.0 - ADAM_B2) * (gv * gv)
        d_ref[...] = (-ADAM_LR) * ((nm * c1) / (jnp.sqrt(nv * c2) + ADAM_EPS) + ADAM_WD * w_ref[...])
        nm_ref[...] = nm
        nv_ref[...] = nv

    spec = pl.BlockSpec((tr, C), lambda i: (i, 0))
    shape = jax.ShapeDtypeStruct((R, C), F32)
    return pl.pallas_call(
        body, name=name, grid=(R // tr,), in_specs=[spec] * 4, out_specs=[spec] * 3, out_shape=[shape] * 3,
        compiler_params=_params("parallel"),
    )(w, g, m, v)


def _flat_rows(parts, rows):
    flat = jnp.concatenate([q.reshape(-1) for q in parts])
    flat = jnp.pad(flat, (0, rows * LANES - flat.shape[0]))
    return flat.reshape(rows, LANES)


def _round_up(n, m):
    return (n + m - 1) // m * m


def _block_diag(w):
    H, n, _ = w.shape
    eye = jnp.eye(H, dtype=w.dtype)
    return (eye[:, None, :, None] * w[:, :, None, :]).reshape(H * n, H * n)


def _diag_blocks(w, H, n):
    w4 = w.reshape(H, n, H, n)
    return jnp.stack([w4[h, :, h, :] for h in range(H)])


FF_SHARD = D_FF // 2
FF_HALF = D_FFP // 2
FF_ROWS = D_FF // N_CHIP


def _ffn_weights(g_in, g_out, l):
    pad = lambda a: jnp.pad(a, ((0, 0), (0, FF_HALF - FF_SHARD)))
    wg = jnp.concatenate([pad(g_in[0, l]), pad(g_in[1, l])], axis=1)
    wu = jnp.concatenate([pad(g_in[2, l]), pad(g_in[3, l])], axis=1)
    zeros = jnp.zeros((FF_HALF - FF_SHARD, D), g_out.dtype)
    wout = jnp.concatenate([g_out[0, l], g_out[1, l], zeros, g_out[2, l], g_out[3, l], zeros], axis=0)
    return wg, wu, wout


def _grad_windows(dwg, dwu, dwout, dwin, dwo):
    w_in, w_out, win, wo = [], [], [], []
    for k in range(N_DEV):
        j, l = k // 2, k % 2
        col = (j % 2) * FF_HALF
        row = (j // 2) * FF_HALF + (j % 2) * FF_ROWS
        w_in.append(((dwg if j < 2 else dwu)[l], lambda r, col=col: r.at[:, pl.ds(col, FF_HALF)]))
        w_out.append((dwout[l], lambda r, row=row: r.at[pl.ds(row, FF_ROWS), :]))
        win.append((dwin[l], lambda r, j=j: r.at[j]))
        wo.append((dwo[l], lambda r, j=j: r.at[pl.ds(j * (D // N_CHIP), D // N_CHIP), :]))
    return w_in, w_out, win, wo


def kernel(x, ffn1_norm, ffn1_w_in, ffn1_w_out, mix_norm, w_in, conv_w, conv_b, rg_w_a, rg_b_a, rg_w_x, rg_b_x, lru_lambda, pool_w, pool_scale, sgu_norm, sgu_w, sgu_b, w_out, ffn2_norm, ffn2_w_in, ffn2_w_out, final_norm, loss_target, m_ffn1_norm, m_ffn1_w_in, m_ffn1_w_out, m_mix_norm, m_w_in, m_conv_w, m_conv_b, m_rg_w_a, m_rg_b_a, m_rg_w_x, m_rg_b_x, m_lru_lambda, m_pool_w, m_pool_scale, m_sgu_norm, m_sgu_w, m_sgu_b, m_w_out, m_ffn2_norm, m_ffn2_w_in, m_ffn2_w_out, m_final_norm, v_ffn1_norm, v_ffn1_w_in, v_ffn1_w_out, v_mix_norm, v_w_in, v_conv_w, v_conv_b, v_rg_w_a, v_rg_b_a, v_rg_w_x, v_rg_b_x, v_lru_lambda, v_pool_w, v_pool_scale, v_sgu_norm, v_sgu_w, v_sgu_b, v_w_out, v_ffn2_norm, v_ffn2_w_in, v_ffn2_w_out, v_final_norm):
    args = locals()
    W = {n: args[n] for n in WEIGHTS}
    M = {n: args["m_" + n] for n in WEIGHTS}
    V = {n: args["v_" + n] for n in WEIGHTS}
    depth = ffn1_norm.shape[0]
    T = x.shape[1]
    xi, yi, ci = _place()
    chip = 2 * xi + yi

    assert depth == 2, "core c of a chip sends and reduces layer c"
    gathered = dict(zip(BIG, _gather_big([W[n].astype(BF16) for n in BIG])))
    conv_rows = _round_up(conv_w.size, 8 * LANES) // LANES
    conv_all = _all_gather8(_flat_rows([conv_w], conv_rows), "gather_conv_w").reshape(N_CHIP, 2, -1)
    conv_all = conv_all[:, 0, :conv_w.size].reshape((N_CHIP,) + conv_w.shape)
    conv_full = jnp.concatenate([conv_all[j] for j in range(N_CHIP)], axis=2)

    layers = []
    for l in range(depth):
        L = {}
        for f in ("ffn1", "ffn2"):
            wg, wu, wout = _ffn_weights(gathered[f + "_w_in"], gathered[f + "_w_out"], l)
            L[f] = dict(norm=W[f + "_norm"][l][None], wg=wg, wu=wu, wout=wout)
        ws = jnp.where(jnp.tril(jnp.ones((CHUNK, CHUNK), bool))[None], sgu_w[l], 0.0)
        wax = jnp.concatenate([_block_diag(rg_w_a[l]), _block_diag(rg_w_x[l])], axis=1)
        wpool = _block_diag(pool_w[l])
        L["mix"] = dict(
            conv_w=conv_full[l], conv_b=conv_b[l][None], wax=wax.astype(BF16), wax_t=wax.T.astype(BF16),
            bax=jnp.concatenate([rg_b_a[l].reshape(-1), rg_b_x[l].reshape(-1)])[None], lam=lru_lambda[l][None],
            wpool=wpool.astype(BF16), wpool_t=wpool.T.astype(BF16), pool_scale=pool_scale[l][None],
            sgu_norm=sgu_norm[l][None], ws=ws.astype(BF16), ws_t=jnp.swapaxes(ws, 1, 2).astype(BF16),
            bz=jnp.repeat(sgu_b[l].T, 64, axis=1))
        L["mix_norm"] = mix_norm[l][None]
        L["w_in"] = jnp.concatenate([gathered["w_in"][j, l] for j in range(N_CHIP)], axis=1)
        L["w_out"] = jnp.concatenate([gathered["w_out"][j, l] for j in range(N_CHIP)], axis=0)
        layers.append(L)

    def ffn_fwd(xin, F):
        h = _rms_fwd(xin, F["norm"])
        g, u, a = _ffn_in(h, F["wg"], F["wu"])
        return _mm_res(a, F["wout"], xin, 0.5, "ffn_out"), (xin, h, g, u, a)

    xs = x[0]
    saved = []
    for L in layers:
        x1, s1 = ffn_fwd(xs, L["ffn1"])
        hm = _rms_fwd(x1, L["mix_norm"])
        p = _mm(hm, L["w_in"], F32, "mix_in")
        ycat, hs = _mix_fwd(p, L["mix"])
        x2 = _mm_res(ycat, L["w_out"], x1, 1.0, "mix_out")
        x3, s2 = ffn_fwd(x2, L["ffn2"])
        saved.append((s1, (x1, hm, p, ycat, hs), s2))
        xs = x3

    dx, dxb, d_final, loss_part = _final(xs, loss_target[0], final_norm[None])

    G = {n: [None] * depth for n in SMALL if n != "final_norm"}
    DW = {k: [None] * depth for k in ("ffn1_wg", "ffn1_wu", "ffn1_wout", "ffn2_wg", "ffn2_wu", "ffn2_wout",
                                      "w_in", "w_out")}

    def ffn_bwd(dx, dxb, F, s, f, l):
        xin, h, g, u, a = s
        dg, du = _ffn_mid_bwd(dxb, F["wout"], g, u)
        DW[f + "_wout"][l] = _mm_tn(a, dxb, 0.5, "ffn_dwout")
        DW[f + "_wg"][l] = _mm_tn(h, dg, 1.0, "ffn_dwg")
        DW[f + "_wu"][l] = _mm_tn(h, du, 1.0, "ffn_dwu")
        dh = _mm2_nt(dg, F["wg"], du, F["wu"], "ffn_dh")
        dx, dxb, dn = _rms_bwd(xin, dh, F["norm"], dx)
        G[f + "_norm"][l] = dn[0]
        return dx, dxb

    for l in reversed(range(depth)):
        L = layers[l]
        s1, (x1, hm, p, ycat, hs), s2 = saved[l]
        dx, dxb = ffn_bwd(dx, dxb, L["ffn2"], s2, "ffn2", l)
        dycat = _mm_nt(dxb, L["w_out"], "mix_dy")
        DW["w_out"][l] = _mm_tn(ycat, dxb, 1.0, "mix_dwout")
        mg = _mix_bwd(dycat, p, hs, L["mix"])
        dwin = _mm_tn(hm, mg["dp"], 1.0, "mix_dwin")
        DW["w_in"][l] = jnp.stack(jnp.split(dwin, N_CHIP, axis=1))
        dhm = _mm_nt(mg["dp"], L["w_in"], "mix_dh")
        dx, dxb, dn = _rms_bwd(x1, dhm, L["mix_norm"], dx)
        G["mix_norm"][l] = dn[0]
        G["conv_w"][l], G["conv_b"][l] = mg["conv_w"], mg["conv_b"][0]
        G["rg_w_a"][l] = _diag_blocks(mg["wax"][:, :D_RNN], 8, 64)
        G["rg_w_x"][l] = _diag_blocks(mg["wax"][:, D_RNN:], 8, 64)
        G["rg_b_a"][l] = mg["bax"][0, :D_RNN].reshape(8, 64)
        G["rg_b_x"][l] = mg["bax"][0, D_RNN:].reshape(8, 64)
        G["lru_lambda"][l] = mg["lam"][0]
        G["pool_w"][l] = _diag_blocks(mg["wpool"], 4, 64)
        G["pool_scale"][l], G["sgu_norm"][l] = mg["pool_scale"][0], mg["sgu_norm"][0]
        G["sgu_w"][l] = mg["ws"]
        G["sgu_b"][l] = mg["bz"].reshape(CHUNK, 4, 64).sum(-1).T
        dx, dxb = ffn_bwd(dx, dxb, L["ffn1"], s1, "ffn1", l)
    grad_x = dx[None]
    G = {n: jnp.stack(v) for n, v in G.items()}
    G["final_norm"] = d_final[0]

    w1 = _grad_windows(DW["ffn1_wg"], DW["ffn1_wu"], DW["ffn1_wout"], DW["w_in"], DW["w_out"])
    w2 = _grad_windows(DW["ffn2_wg"], DW["ffn2_wu"], DW["ffn2_wout"], DW["w_in"], DW["w_out"])
    sources = [w1[0], w1[1], w1[2], w1[3], w2[0], w2[1]]
    shapes = [(D, FF_HALF), (FF_ROWS, D), (D, D_IN // N_CHIP), (D // N_CHIP, D), (D, FF_HALF), (FF_ROWS, D)]
    got = _exchange_big(sources, shapes)
    both = _share_pair([_sum8(a, "sum_" + n) for n, a in zip(BIG, got)])
    grads = {n: (a[:, :, :FF_SHARD] if n.endswith("w_in") and n != "w_in" else a) for n, a in zip(BIG, both)}

    small_sizes = [int(np.prod(G[n].shape)) for n in SMALL]
    srows = _round_up(sum(small_sizes) + 1, N_DEV * 8 * LANES) // (N_DEV * LANES)
    sflat = _flat_rows([G[n] for n in SMALL] + [loss_part[0, :1]], N_DEV * srows)
    sgot = _all_to_all(sflat.reshape(N_DEV, srows, LANES), "exchange_small_grads")
    sall = _all_gather8(_sum8(sgot, "sum_small_grads"), "share_small_grads").reshape(-1)
    off = 0
    for n, size in zip(SMALL, small_sizes):
        grads[n] = sall[off:off + size].reshape(G[n].shape)
        off += size
    loss = sall[off]
    grads["conv_w"] = lax.dynamic_slice_in_dim(grads["conv_w"], chip * conv_w.shape[2], conv_w.shape[2], axis=2)

    delta, new_m, new_v = {}, {}, {}
    for n in BIG:
        shp = W[n].shape
        two_d = (shp[0] * shp[1], shp[2])
        outs = _adamw(W[n].reshape(two_d), grads[n].reshape(two_d), M[n].reshape(two_d), V[n].reshape(two_d),
                      "adamw_" + n)
        delta[n], new_m[n], new_v[n] = (o.reshape(shp) for o in outs)
    arows = _round_up(sum(int(np.prod(W[n].shape)) for n in SMALL), 8 * LANES) // LANES
    outs = _adamw(*(_flat_rows([src[n] for n in SMALL], arows) for src in (W, grads, M, V)), "adamw_small")
    outs = [o.reshape(-1) for o in outs]
    off = 0
    for n in SMALL:
        size = int(np.prod(W[n].shape))
        delta[n], new_m[n], new_v[n] = (o[off:off + size].reshape(W[n].shape) for o in outs)
        off += size

    return (loss, grad_x, *[grads[n] for n in WEIGHTS], *[delta[n] for n in WEIGHTS],
            *[new_m[n] for n in WEIGHTS], *[new_v[n] for n in WEIGHTS])
```

```python
import math

import jax
import jax.numpy as jnp
import numpy as np
from jax import lax
from jax.experimental import pallas as pl
from jax.experimental.pallas import tpu as pltpu

F32 = jnp.float32
BF16 = jnp.bfloat16
MESH = pl.DeviceIdType.MESH

D = 1024
D_RNN = 512
D_POOL = 256
D_SGU = 256
D_IN = 1792
D_FF = 2752
D_FFP = 2816
CHUNK = 128
HALO = 16
EPS = 1e-6
LRU_C = 8.0
N_DEV = 8
N_CHIP = 4
LANES = 1024
VMEM_LIMIT = 56 * 1024 * 1024

ADAM_LR, ADAM_B1, ADAM_B2, ADAM_EPS, ADAM_WD, ADAM_STEP = 0.001, 0.9, 0.999, 1e-08, 0.01, 10

BIG = ("ffn1_w_in", "ffn1_w_out", "w_in", "w_out", "ffn2_w_in", "ffn2_w_out")
SMALL = ("ffn1_norm", "mix_norm", "conv_w", "conv_b", "rg_w_a", "rg_b_a", "rg_w_x", "rg_b_x", "lru_lambda",
         "pool_w", "pool_scale", "sgu_norm", "sgu_w", "sgu_b", "ffn2_norm", "final_norm")
WEIGHTS = ("ffn1_norm", "ffn1_w_in", "ffn1_w_out", "mix_norm", "w_in", "conv_w", "conv_b", "rg_w_a", "rg_b_a",
           "rg_w_x", "rg_b_x", "lru_lambda", "pool_w", "pool_scale", "sgu_norm", "sgu_w", "sgu_b", "w_out",
           "ffn2_norm", "ffn2_w_in", "ffn2_w_out", "final_norm")


def _params(*sem):
    return pltpu.CompilerParams(dimension_semantics=sem, vmem_limit_bytes=VMEM_LIMIT)


def _gelu(x):
    c = math.sqrt(2.0 / math.pi)
    t = jnp.tanh(c * (x + 0.044715 * (x * x * x)))
    return 0.5 * x * (1.0 + t)


def _gelu_and_grad(x):
    c = math.sqrt(2.0 / math.pi)
    x2 = x * x
    t = jnp.tanh(c * (x + 0.044715 * (x2 * x)))
    g = 0.5 * x * (1.0 + t)
    dg = 0.5 * (1.0 + t) + 0.5 * x * (1.0 - t * t) * (c * (1.0 + 3.0 * 0.044715 * x2))
    return g, dg


def _sigmoid(x):
    return 0.5 * jnp.tanh(0.5 * x) + 0.5


def _dot(a, b):
    return jnp.dot(a, b, preferred_element_type=F32)


def _dot_tn(a, b):
    return lax.dot_general(a, b, (((0,), (0,)), ((), ())), preferred_element_type=F32)


def _dot_nt(a, b):
    return lax.dot_general(a, b, (((1,), (1,)), ((), ())), preferred_element_type=F32)


def _rms_fwd(x, g, tm=512):
    T = x.shape[0]
    tm = min(tm, T)

    def body(x_ref, g_ref, o_ref):
        xv = x_ref[...]
        r = lax.rsqrt(jnp.mean(xv * xv, axis=-1, keepdims=True) + EPS)
        o_ref[...] = (xv * r * g_ref[...]).astype(BF16)

    return pl.pallas_call(
        body, name="rms_fwd", grid=(T // tm,),
        in_specs=[pl.BlockSpec((tm, D), lambda i: (i, 0)), pl.BlockSpec((1, D), lambda i: (0, 0))],
        out_specs=pl.BlockSpec((tm, D), lambda i: (i, 0)),
        out_shape=jax.ShapeDtypeStruct((T, D), BF16),
        compiler_params=_params("parallel"),
    )(x, g)


def _rms_bwd(x, dh, g, dres, tm=512):
    T = x.shape[0]
    tm = min(tm, T)

    def body(x_ref, dh_ref, g_ref, dres_ref, dx_ref, dxb_ref, dg_ref):
        xv = x_ref[...]
        r = lax.rsqrt(jnp.mean(xv * xv, axis=-1, keepdims=True) + EPS)
        xhat = xv * r
        dy = dh_ref[...]
        dxhat = dy * g_ref[...]
        dx = dres_ref[...] + r * (dxhat - xhat * jnp.mean(dxhat * xhat, axis=-1, keepdims=True))
        dx_ref[...] = dx
        dxb_ref[...] = dx.astype(BF16)

        @pl.when(pl.program_id(0) == 0)
        def _():
            dg_ref[...] = jnp.zeros_like(dg_ref)

        dg_ref[...] += jnp.sum(dy * xhat, axis=0, keepdims=True)

    row = pl.BlockSpec((tm, D), lambda i: (i, 0))
    vec = pl.BlockSpec((1, D), lambda i: (0, 0))
    return pl.pallas_call(
        body, name="rms_bwd", grid=(T // tm,),
        in_specs=[row, row, vec, row], out_specs=[row, row, vec],
        out_shape=[jax.ShapeDtypeStruct((T, D), F32), jax.ShapeDtypeStruct((T, D), BF16),
                   jax.ShapeDtypeStruct((1, D), F32)],
        compiler_params=_params("arbitrary"),
    )(x, dh, g, dres)


def _tile(n, limit):
    best = 128
    for t in range(128, min(n, limit) + 1, 128):
        if n % t == 0:
            best = t
    assert n % best == 0, (n, limit)
    return best


def _mm(a, b, out_dtype, name, tm=1024, tn=512):
    M, K = a.shape
    N = b.shape[1]
    tm, tn = min(tm, M), _tile(N, tn)

    def body(a_ref, b_ref, o_ref):
        o_ref[...] = _dot(a_ref[...], b_ref[...]).astype(out_dtype)

    return pl.pallas_call(
        body, name=name, grid=(M // tm, N // tn),
        in_specs=[pl.BlockSpec((tm, K), lambda i, j: (i, 0)), pl.BlockSpec((K, tn), lambda i, j: (0, j))],
        out_specs=pl.BlockSpec((tm, tn), lambda i, j: (i, j)),
        out_shape=jax.ShapeDtypeStruct((M, N), out_dtype),
        compiler_params=_params("parallel", "parallel"),
    )(a, b)


def _mm_res(a, b, res, scale, name, tm=1024, tn=512):
    M, K = a.shape
    N = b.shape[1]
    tm, tn = min(tm, M), _tile(N, tn)

    def body(a_ref, b_ref, r_ref, o_ref):
        o_ref[...] = r_ref[...] + scale * _dot(a_ref[...], b_ref[...])

    return pl.pallas_call(
        body, name=name, grid=(M // tm, N // tn),
        in_specs=[pl.BlockSpec((tm, K), lambda i, j: (i, 0)), pl.BlockSpec((K, tn), lambda i, j: (0, j)),
                  pl.BlockSpec((tm, tn), lambda i, j: (i, j))],
        out_specs=pl.BlockSpec((tm, tn), lambda i, j: (i, j)),
        out_shape=jax.ShapeDtypeStruct((M, N), F32),
        compiler_params=_params("parallel", "parallel"),
    )(a, b, res)


def _mm_nt(a, b, name, tm=1024, tn=512):
    M, K = a.shape
    N = b.shape[0]
    tm, tn = min(tm, M), _tile(N, tn)

    def body(a_ref, b_ref, o_ref):
        o_ref[...] = _dot_nt(a_ref[...], b_ref[...])

    return pl.pallas_call(
        body, name=name, grid=(M // tm, N // tn),
        in_specs=[pl.BlockSpec((tm, K), lambda i, j: (i, 0)), pl.BlockSpec((tn, K), lambda i, j: (j, 0))],
        out_specs=pl.BlockSpec((tm, tn), lambda i, j: (i, j)),
        out_shape=jax.ShapeDtypeStruct((M, N), F32),
        compiler_params=_params("parallel", "parallel"),
    )(a, b)


def _mm2_nt(a1, b1, a2, b2, name, tm=1024, tn=512):
    M, K = a1.shape
    N = b1.shape[0]
    tm, tn = min(tm, M), _tile(N, tn)

    def body(a1_ref, b1_ref, a2_ref, b2_ref, o_ref):
        o_ref[...] = _dot_nt(a1_ref[...], b1_ref[...]) + _dot_nt(a2_ref[...], b2_ref[...])

    aspec = pl.BlockSpec((tm, K), lambda i, j: (i, 0))
    bspec = pl.BlockSpec((tn, K), lambda i, j: (j, 0))
    return pl.pallas_call(
        body, name=name, grid=(M // tm, N // tn),
        in_specs=[aspec, bspec, aspec, bspec],
        out_specs=pl.BlockSpec((tm, tn), lambda i, j: (i, j)),
        out_shape=jax.ShapeDtypeStruct((M, N), F32),
        compiler_params=_params("parallel", "parallel"),
    )(a1, b1, a2, b2)


def _mm_tn(a, b, scale, name, tm=1792, tn=1792, tk=512):
    T, M = a.shape
    N = b.shape[1]
    tm, tn, tk = _tile(M, tm), _tile(N, tn), min(tk, T)
    nk = T // tk

    def body(a_ref, b_ref, o_ref, acc_ref):
        k = pl.program_id(2)

        @pl.when(k == 0)
        def _():
            acc_ref[...] = jnp.zeros_like(acc_ref)

        acc_ref[...] += _dot_tn(a_ref[...], b_ref[...])

        @pl.when(k == nk - 1)
        def _():
            o_ref[...] = (scale * acc_ref[...]).astype(BF16)

    return pl.pallas_call(
        body, name=name, grid=(M // tm, N // tn, nk),
        in_specs=[pl.BlockSpec((tk, tm), lambda i, j, k: (k, i)), pl.BlockSpec((tk, tn), lambda i, j, k: (k, j))],
        out_specs=pl.BlockSpec((tm, tn), lambda i, j, k: (i, j)),
        out_shape=jax.ShapeDtypeStruct((M, N), BF16),
        scratch_shapes=[pltpu.VMEM((tm, tn), F32)],
        compiler_params=_params("parallel", "parallel", "arbitrary"),
    )(a, b)


def _ffn_in(h, wg, wu, tm=1024, tn=256):
    T = h.shape[0]
    tm = min(tm, T)

    def body(h_ref, wg_ref, wu_ref, g_ref, u_ref, a_ref):
        hv = h_ref[...]
        g = _dot(hv, wg_ref[...])
        u = _dot(hv, wu_ref[...])
        g_ref[...] = g.astype(BF16)
        u_ref[...] = u.astype(BF16)
        a_ref[...] = (g * _sigmoid(g) * u).astype(BF16)

    wspec = pl.BlockSpec((D, tn), lambda i, j: (0, j))
    ospec = pl.BlockSpec((tm, tn), lambda i, j: (i, j))
    oshape = jax.ShapeDtypeStruct((T, D_FFP), BF16)
    return pl.pallas_call(
        body, name="ffn_in", grid=(T // tm, D_FFP // tn),
        in_specs=[pl.BlockSpec((tm, D), lambda i, j: (i, 0)), wspec, wspec],
        out_specs=[ospec, ospec, ospec], out_shape=[oshape, oshape, oshape],
        compiler_params=_params("parallel", "parallel"),
    )(h, wg, wu)


def _ffn_mid_bwd(dy, wout, g, u, tm=1024, tn=256):
    T = dy.shape[0]
    tm = min(tm, T)

    def body(dy_ref, w_ref, g_ref, u_ref, dg_ref, du_ref):
        da = 0.5 * _dot_nt(dy_ref[...], w_ref[...])
        g = g_ref[...].astype(F32)
        u = u_ref[...].astype(F32)
        s = _sigmoid(g)
        dg_ref[...] = (da * u * (s * (1.0 + g * (1.0 - s)))).astype(BF16)
        du_ref[...] = (da * (g * s)).astype(BF16)

    ospec = pl.BlockSpec((tm, tn), lambda i, j: (i, j))
    oshape = jax.ShapeDtypeStruct((T, D_FFP), BF16)
    return pl.pallas_call(
        body, name="ffn_mid_bwd", grid=(T // tm, D_FFP // tn),
        in_specs=[pl.BlockSpec((tm, D), lambda i, j: (i, 0)), pl.BlockSpec((tn, D), lambda i, j: (j, 0)),
                  ospec, ospec],
        out_specs=[ospec, ospec], out_shape=[oshape, oshape],
        compiler_params=_params("parallel", "parallel"),
    )(dy, wout, g, u)


def _final(x, tgt, gf, tm=512):
    T = x.shape[0]
    tm = min(tm, T)

    def body(x_ref, t_ref, g_ref, dx_ref, dxb_ref, dg_ref, loss_ref):
        xv = x_ref[...]
        r = lax.rsqrt(jnp.mean(xv * xv, axis=-1, keepdims=True) + EPS)
        xhat = xv * r
        err = xhat * g_ref[...] - t_ref[...]
        dy = err * (1.0 / D)
        dxhat = dy * g_ref[...]
        dx = r * (dxhat - xhat * jnp.mean(dxhat * xhat, axis=-1, keepdims=True))
        dx_ref[...] = dx
        dxb_ref[...] = dx.astype(BF16)

        @pl.when(pl.program_id(0) == 0)
        def _():
            dg_ref[...] = jnp.zeros_like(dg_ref)
            loss_ref[...] = jnp.zeros_like(loss_ref)

        dg_ref[...] += jnp.sum(dy * xhat, axis=0, keepdims=True)
        loss_ref[...] += (0.5 / D) * jnp.sum(err * err)

    row = pl.BlockSpec((tm, D), lambda i: (i, 0))
    vec = pl.BlockSpec((1, D), lambda i: (0, 0))
    return pl.pallas_call(
        body, name="final_loss", grid=(T // tm,),
        in_specs=[row, row, vec],
        out_specs=[row, row, vec, pl.BlockSpec((1, 128), lambda i: (0, 0))],
        out_shape=[jax.ShapeDtypeStruct((T, D), F32), jax.ShapeDtypeStruct((T, D), BF16),
                   jax.ShapeDtypeStruct((1, D), F32), jax.ShapeDtypeStruct((1, 128), F32)],
        compiler_params=_params("arbitrary"),
    )(x, tgt, gf)


def _mix_block(T, limit):
    return min(limit, T // 2)


def _rows(tb, width):
    return lax.broadcasted_iota(jnp.int32, (tb, width), 0)


def _rglru_gates(xc, wax_ref, bax_ref, lam_ref):
    pre = _dot(xc.astype(BF16), wax_ref[...]) + bax_ref[...]
    r = _sigmoid(pre[:, :D_RNN])
    ig = _sigmoid(pre[:, D_RNN:])
    z = -lam_ref[...]
    sp = jnp.maximum(z, 0.0) + jnp.log(1.0 + jnp.exp(-jnp.abs(z)))
    log_a = (-LRU_C) * r * sp
    a = jnp.exp(log_a)
    mult = jnp.sqrt(-jnp.tanh(log_a) * (1.0 + a * a))
    return r, ig, sp, a, mult


def _conv(xa_ext, cw_ref, cb_ref):
    y = cb_ref[...] + cw_ref[3:4, :] * xa_ext
    for k in range(1, 4):
        y = y + cw_ref[3 - k:4 - k, :] * pltpu.roll(xa_ext, k, 0)
    return y[HALO:]


def _pool_window_lanes():
    lane = lax.broadcasted_iota(jnp.int32, (1, D_POOL), 1)
    return jnp.where(lane < 64, 2, jnp.where(lane < 128, 4, jnp.where(lane < 192, 8, 16)))


def _pool_select(s2, s4, s8, s16):
    lane = lax.broadcasted_iota(jnp.int32, s2.shape, 1)
    return jnp.where(lane < 64, s2, jnp.where(lane < 128, s4, jnp.where(lane < 192, s8, s16)))


def _pool_diff(xp_ext, t0, tb):
    s2 = xp_ext + pltpu.roll(xp_ext, 1, 0)
    s4 = s2 + pltpu.roll(s2, 2, 0)
    s8 = s4 + pltpu.roll(s4, 4, 0)
    s16 = s8 + pltpu.roll(s8, 8, 0)
    sel = _pool_select(s2, s4, s8, s16)[HALO:]
    cnt = jnp.minimum(t0 + _rows(tb, D_POOL) + 1, _pool_window_lanes()).astype(F32)
    return sel / cnt - xp_ext[HALO:], cnt


def _head_masks():
    lane = lax.broadcasted_iota(jnp.int32, (1, D_SGU), 1)
    return [((lane >= 64 * h) & (lane < 64 * (h + 1))).astype(F32) for h in range(4)]


def _sgu_mix(w_ref, vch, masks):
    z = masks[0] * _dot(w_ref[0], vch)
    for h in range(1, 4):
        z = z + masks[h] * _dot(w_ref[h], vch)
    return z


def _mix_fwd(p, prm):
    T = p.shape[0]
    tb = _mix_block(T, 512)
    nb = T // tb

    def body(p_ref, xah_ref, xph_ref, cw_ref, cb_ref, wax_ref, bax_ref, lam_ref, wp_ref, ps_ref, sgn_ref,
             ws_ref, bz_ref, y_ref, hs_ref, carry_ref):
        i = pl.program_id(0)
        keep = (i > 0).astype(F32)

        @pl.when(i == 0)
        def _():
            carry_ref[...] = jnp.zeros_like(carry_ref)

        xa_ext = jnp.concatenate([xah_ref[...] * keep, p_ref[:, 512:1024]], axis=0)
        xc = _conv(xa_ext, cw_ref, cb_ref)
        r, ig, sp, a, mult = _rglru_gates(xc, wax_ref, bax_ref, lam_ref)
        bv = mult * (ig * xc)
        row = _rows(tb, D_RNN)
        s = 1
        while s < tb:
            m = row >= s
            bv = jnp.where(m, a * pltpu.roll(bv, s, 0) + bv, bv)
            a = jnp.where(m, a * pltpu.roll(a, s, 0), a)
            s *= 2
        h = bv + a * carry_ref[0:1, :]
        hs_ref[...] = h
        last = jnp.sum(jnp.where(_rows(8, D_RNN) == 7, hs_ref[tb - 8:tb, :], 0.0), axis=0, keepdims=True)
        carry_ref[...] = jnp.broadcast_to(last, carry_ref.shape)
        y_ref[:, 0:512] = (_gelu(p_ref[:, 0:512]) * h).astype(BF16)

        xp_ext = jnp.concatenate([xph_ref[...] * keep, p_ref[:, 1024:1280]], axis=0)
        d, _ = _pool_diff(xp_ext, i * tb, tb)
        y_ref[:, 512:768] = (_dot(d.astype(BF16), wp_ref[...]) * ps_ref[...]).astype(BF16)

        ug = _gelu(p_ref[:, 1280:1536])
        vg = _gelu(p_ref[:, 1536:1792])
        rv = lax.rsqrt(jnp.mean(vg * vg, axis=-1, keepdims=True) + EPS)
        vn = (vg * rv * sgn_ref[...]).astype(BF16)
        masks = _head_masks()
        for ci in range(tb // CHUNK):
            sl = slice(ci * CHUNK, (ci + 1) * CHUNK)
            z = _sgu_mix(ws_ref, vn[sl], masks) + bz_ref[...]
            y_ref[sl, 768:1024] = (ug[sl] * z).astype(BF16)

    hb = tb // HALO

    def halo(i):
        return jnp.maximum(i * hb - 1, 0)

    def full(shape):
        return pl.BlockSpec(shape, lambda i: (0,) * len(shape))

    return pl.pallas_call(
        body, name="mix_fwd", grid=(nb,),
        in_specs=[pl.BlockSpec((tb, D_IN), lambda i: (i, 0)),
                  pl.BlockSpec((HALO, D_RNN), lambda i: (halo(i), 1)),
                  pl.BlockSpec((HALO, D_POOL), lambda i: (halo(i), 4)),
                  full((4, D_RNN)), full((1, D_RNN)), full((D_RNN, 2 * D_RNN)), full((1, 2 * D_RNN)),
                  full((1, D_RNN)), full((D_POOL, D_POOL)), full((1, D_POOL)), full((1, D_SGU)),
                  full((4, CHUNK, CHUNK)), full((CHUNK, D_SGU))],
        out_specs=[pl.BlockSpec((tb, D), lambda i: (i, 0)), pl.BlockSpec((tb, D_RNN), lambda i: (i, 0))],
        out_shape=[jax.ShapeDtypeStruct((T, D), BF16), jax.ShapeDtypeStruct((T, D_RNN), F32)],
        scratch_shapes=[pltpu.VMEM((8, D_RNN), F32)],
        compiler_params=_params("arbitrary"),
    )(p, p, p, prm["conv_w"], prm["conv_b"], prm["wax"], prm["bax"], prm["lam"], prm["wpool"], prm["pool_scale"],
      prm["sgu_norm"], prm["ws"], prm["bz"])


def _mix_bwd(dy, p, hs, prm):
    T = p.shape[0]
    tb = _mix_block(T, 256)
    nb = T // tb
    hb = tb // HALO

    def body(dy_ref, p_ref, xah_ref, xph_ref, hs_ref, hsh_ref, cw_ref, cb_ref, wax_ref, waxt_ref, bax_ref,
             lam_ref, wp_ref, wpt_ref, ps_ref, sgn_ref, ws_ref, wst_ref, bz_ref,
             dp_ref, dcw_ref, dcb_ref, dwax_ref, dbax_ref, dlam_ref, dwp_ref, dps_ref, dsgn_ref, dws_ref,
             dbz_ref, gcarry_ref, xcfut_ref, mfut_ref):
        i = pl.program_id(0)
        bi = nb - 1 - i
        keep = (bi > 0).astype(F32)

        @pl.when(i == 0)
        def _():
            for ref in (dcw_ref, dcb_ref, dwax_ref, dbax_ref, dlam_ref, dwp_ref, dps_ref, dsgn_ref, dws_ref,
                        dbz_ref, gcarry_ref, xcfut_ref, mfut_ref):
                ref[...] = jnp.zeros_like(ref)

        xa_ext = jnp.concatenate([xah_ref[...] * keep, p_ref[:, 512:1024]], axis=0)
        xc = _conv(xa_ext, cw_ref, cb_ref)
        r, ig, sp, a, mult = _rglru_gates(xc, wax_ref, bax_ref, lam_ref)
        gg, dgg = _gelu_and_grad(p_ref[:, 0:512])
        dya = dy_ref[:, 0:512]
        dp_ref[:, 0:512] = (dya * hs_ref[...] * dgg).astype(BF16)
        row = _rows(tb, D_RNN)
        g = dya * gg + jnp.where(row == tb - 1, gcarry_ref[0:1, :], 0.0)
        al = pltpu.roll(a, tb - 1, 0)
        s = 1
        while s < tb:
            m = row < tb - s
            g = jnp.where(m, al * pltpu.roll(g, tb - s, 0) + g, g)
            al = jnp.where(m, al * pltpu.roll(al, tb - s, 0), al)
            s *= 2
        first = jnp.sum(jnp.where(_rows(8, D_RNN) == 0, (a * g)[0:8], 0.0), axis=0, keepdims=True)
        gcarry_ref[...] = jnp.broadcast_to(first, gcarry_ref.shape)
        hs_ext = jnp.concatenate([hsh_ref[...] * keep, hs_ref[...]], axis=0)
        h_prev = pltpu.roll(hs_ext, 1, 0)[HALO:]
        ix = ig * xc
        dlog_a = g * h_prev * a - (g * ix) * (a * a / mult)
        dlam_ref[...] += jnp.sum(dlog_a * r, axis=0, keepdims=True) * (LRU_C * _sigmoid(-lam_ref[...]))
        dpre_r = dlog_a * ((-LRU_C) * sp) * (r * (1.0 - r))
        dpre_i = (g * mult * xc) * (ig * (1.0 - ig))
        dpre = jnp.concatenate([dpre_r, dpre_i], axis=1)
        dbax_ref[...] += jnp.sum(dpre, axis=0, keepdims=True)
        dpre_b = dpre.astype(BF16)
        dwax_ref[...] += _dot_tn(xc.astype(BF16), dpre_b)
        dxc = g * mult * ig + _dot(dpre_b, waxt_ref[...])
        dcb_ref[...] += jnp.sum(dxc, axis=0, keepdims=True)
        for k in range(4):
            xs = xa_ext[HALO:] if k == 3 else pltpu.roll(xa_ext, 3 - k, 0)[HALO:]
            dcw_ref[k:k + 1, :] += jnp.sum(dxc * xs, axis=0, keepdims=True)
        dxc_ext = jnp.concatenate([dxc, xcfut_ref[...]], axis=0)
        n = tb + HALO
        dxa = cw_ref[3:4, :] * dxc_ext
        for k in range(1, 4):
            dxa = dxa + cw_ref[3 - k:4 - k, :] * pltpu.roll(dxc_ext, n - k, 0)
        dp_ref[:, 512:1024] = dxa[:tb].astype(BF16)
        xcfut_ref[...] = dxc[0:HALO]

        xp_ext = jnp.concatenate([xph_ref[...] * keep, p_ref[:, 1024:1280]], axis=0)
        d, cnt = _pool_diff(xp_ext, bi * tb, tb)
        db = d.astype(BF16)
        dyb = dy_ref[:, 512:768]
        dps_ref[...] += jnp.sum(dyb * _dot(db, wp_ref[...]), axis=0, keepdims=True)
        dq = (dyb * ps_ref[...]).astype(BF16)
        dwp_ref[...] += _dot_tn(db, dq)
        dd = _dot(dq, wpt_ref[...])
        mm = dd / cnt
        m_ext = jnp.concatenate([mm, mfut_ref[...]], axis=0)
        f2 = m_ext + pltpu.roll(m_ext, n - 1, 0)
        f4 = f2 + pltpu.roll(f2, n - 2, 0)
        f8 = f4 + pltpu.roll(f4, n - 4, 0)
        f16 = f8 + pltpu.roll(f8, n - 8, 0)
        dp_ref[:, 1024:1280] = (_pool_select(f2, f4, f8, f16)[:tb] - dd).astype(BF16)
        mfut_ref[...] = mm[0:HALO]

        ug, dug = _gelu_and_grad(p_ref[:, 1280:1536])
        vg, dvg = _gelu_and_grad(p_ref[:, 1536:1792])
        rv = lax.rsqrt(jnp.mean(vg * vg, axis=-1, keepdims=True) + EPS)
        vhat = vg * rv
        vn = (vhat * sgn_ref[...]).astype(BF16)
        dyc = dy_ref[:, 768:1024]
        masks = _head_masks()
        dz = dyc * ug
        dzb = dz.astype(BF16)
        dvn_parts = []
        for ci in range(tb // CHUNK):
            sl = slice(ci * CHUNK, (ci + 1) * CHUNK)
            z = _sgu_mix(ws_ref, vn[sl], masks) + bz_ref[...]
            dp_ref[sl, 1280:1536] = (dyc[sl] * z * dug[sl]).astype(BF16)
            dbz_ref[...] += dz[sl]
            for h in range(4):
                dws_ref[h] += _dot_nt((dz[sl] * masks[h]).astype(BF16), vn[sl])
            dvn_parts.append(_sgu_mix(wst_ref, dzb[sl], masks))
        dvn = jnp.concatenate(dvn_parts, axis=0)
        dsgn_ref[...] += jnp.sum(dvn * vhat, axis=0, keepdims=True)
        dvhat = dvn * sgn_ref[...]
        dvg_in = rv * (dvhat - vhat * jnp.mean(dvhat * vhat, axis=-1, keepdims=True))
        dp_ref[:, 1536:1792] = (dvg_in * dvg).astype(BF16)

        @pl.when(i == nb - 1)
        def _():
            tril = (lax.broadcasted_iota(jnp.int32, (CHUNK, CHUNK), 0)
                    >= lax.broadcasted_iota(jnp.int32, (CHUNK, CHUNK), 1)).astype(F32)
            for h in range(4):
                dws_ref[h] = dws_ref[h] * tril

    def blk(i):
        return nb - 1 - i

    def halo(i):
        return jnp.maximum(blk(i) * hb - 1, 0)

    def full(shape):
        return pl.BlockSpec(shape, lambda i: (0,) * len(shape))

    small_shapes = [(4, D_RNN), (1, D_RNN), (D_RNN, 2 * D_RNN), (1, 2 * D_RNN), (1, D_RNN), (D_POOL, D_POOL),
                    (1, D_POOL), (1, D_SGU), (4, CHUNK, CHUNK), (CHUNK, D_SGU)]
    outs = pl.pallas_call(
        body, name="mix_bwd", grid=(nb,),
        in_specs=[pl.BlockSpec((tb, D), lambda i: (blk(i), 0)),
                  pl.BlockSpec((tb, D_IN), lambda i: (blk(i), 0)),
                  pl.BlockSpec((HALO, D_RNN), lambda i: (halo(i), 1)),
                  pl.BlockSpec((HALO, D_POOL), lambda i: (halo(i), 4)),
                  pl.BlockSpec((tb, D_RNN), lambda i: (blk(i), 0)),
                  pl.BlockSpec((HALO, D_RNN), lambda i: (halo(i), 0)),
                  full((4, D_RNN)), full((1, D_RNN)), full((D_RNN, 2 * D_RNN)), full((2 * D_RNN, D_RNN)),
                  full((1, 2 * D_RNN)), full((1, D_RNN)), full((D_POOL, D_POOL)), full((D_POOL, D_POOL)),
                  full((1, D_POOL)), full((1, D_SGU)), full((4, CHUNK, CHUNK)), full((4, CHUNK, CHUNK)),
                  full((CHUNK, D_SGU))],
        out_specs=[pl.BlockSpec((tb, D_IN), lambda i: (blk(i), 0))] + [full(s) for s in small_shapes],
        out_shape=[jax.ShapeDtypeStruct((T, D_IN), BF16)] + [jax.ShapeDtypeStruct(s, F32) for s in small_shapes],
        scratch_shapes=[pltpu.VMEM((8, D_RNN), F32), pltpu.VMEM((HALO, D_RNN), F32),
                        pltpu.VMEM((HALO, D_POOL), F32)],
        compiler_params=_params("arbitrary"),
    )(dy, p, p, p, hs, hs, prm["conv_w"], prm["conv_b"], prm["wax"], prm["wax_t"], prm["bax"], prm["lam"],
      prm["wpool"], prm["wpool_t"], prm["pool_scale"], prm["sgu_norm"], prm["ws"], prm["ws_t"], prm["bz"])
    names = ("dp", "conv_w", "conv_b", "wax", "bax", "lam", "wpool", "pool_scale", "sgu_norm", "ws", "bz")
    return dict(zip(names, outs))


ANY = pl.BlockSpec(memory_space=pl.ANY)


def _place():
    x, y, c = lax.axis_index("x"), lax.axis_index("y"), lax.axis_index("c")
    return x, y, c


def _all_to_all(xs, name):
    def body(in_ref, out_ref, send_sems, recv_sems, local_sem):
        x, y, c = _place()
        me = 4 * x + 2 * y + c
        mine = pltpu.make_async_copy(in_ref.at[me], out_ref.at[me], local_sem)
        mine.start()
        copies = []
        for rel in range(1, N_DEV):
            tx = 1 - x if rel & 4 else x
            ty = 1 - y if rel & 2 else y
            tc = 1 - c if rel & 1 else c
            cp = pltpu.make_async_remote_copy(
                src_ref=in_ref.at[4 * tx + 2 * ty + tc], dst_ref=out_ref.at[me],
                send_sem=send_sems.at[rel - 1], recv_sem=recv_sems.at[rel - 1],
                device_id=(tx, ty, tc), device_id_type=MESH)
            cp.start()
            copies.append(cp)
        for cp in copies:
            cp.wait()
        mine.wait()

    return pl.pallas_call(
        body, name=name, in_specs=[ANY], out_specs=ANY,
        out_shape=jax.ShapeDtypeStruct(xs.shape, xs.dtype),
        scratch_shapes=[pltpu.SemaphoreType.DMA((N_DEV - 1,)), pltpu.SemaphoreType.DMA((N_DEV - 1,)),
                        pltpu.SemaphoreType.DMA],
    )(xs)


def _all_gather8(xs, name):
    def body(x_ref, out_ref, send_sems, recv_sems, local_sem):
        x, y, c = _place()
        me, sibling = (x, y, c), (x, y, 1 - c)
        chips = [(1 - x, y), (x, 1 - y), (1 - x, 1 - y)]

        def rows(px, py, pc):
            return out_ref.at[4 * px + 2 * py + pc]

        def copy(k, block, to, src=None):
            return pltpu.make_async_remote_copy(
                src_ref=rows(*block) if src is None else src, dst_ref=rows(*block),
                send_sem=send_sems.at[k], recv_sem=recv_sems.at[k], device_id=to, device_id_type=MESH)

        mine = pltpu.make_async_copy(x_ref, rows(*me), local_sem)
        mine.start()
        first = [copy(0, me, sibling, src=x_ref)]
        first += [copy(1 + j, me, (*chip, c), src=x_ref) for j, chip in enumerate(chips)]
        for cp in first:
            cp.start()
        passed = [copy(4 + j, (*chip, c), sibling) for j, chip in enumerate(chips)]
        for j, chip in enumerate(chips):
            copy(1 + j, (*chip, c), me).wait_recv()
            passed[j].start()
        copy(0, sibling, me).wait_recv()
        for j, chip in enumerate(chips):
            copy(4 + j, (*chip, 1 - c), me).wait_recv()
        for cp in first + passed:
            cp.wait_send()
        mine.wait()

    return pl.pallas_call(
        body, name=name, in_specs=[ANY], out_specs=ANY,
        out_shape=jax.ShapeDtypeStruct((N_DEV,) + xs.shape, xs.dtype),
        scratch_shapes=[pltpu.SemaphoreType.DMA((7,)), pltpu.SemaphoreType.DMA((7,)), pltpu.SemaphoreType.DMA],
    )(xs)


def _gather_big(shards):
    nt = len(shards)

    def body(*refs):
        in_refs, out_refs = refs[:nt], refs[nt:2 * nt]
        send_sems, recv_sems, local_sems = refs[2 * nt:]
        x, y, c = _place()
        me, sibling = (x, y, c), (x, y, 1 - c)
        chips = [(1 - x, y), (x, 1 - y), (1 - x, 1 - y)]

        def block(t, px, py, pc):
            return out_refs[t].at[2 * px + py, pc]

        def copy(t, k, blk, to, src=None):
            return pltpu.make_async_remote_copy(
                src_ref=block(t, *blk) if src is None else src, dst_ref=block(t, *blk),
                send_sem=send_sems.at[7 * t + k], recv_sem=recv_sems.at[7 * t + k], device_id=to,
                device_id_type=MESH)

        mine, first, passed = [], [], []
        for t in range(nt):
            own = in_refs[t].at[c]
            mine.append(pltpu.make_async_copy(own, block(t, *me), local_sems.at[t]))
            first.append(copy(t, 0, me, sibling, src=own))
            first += [copy(t, 1 + j, me, (*chip, c), src=own) for j, chip in enumerate(chips)]
        for cp in mine + first:
            cp.start()
        for j, chip in enumerate(chips):
            for t in range(nt):
                copy(t, 1 + j, (*chip, c), me).wait_recv()
                fwd = copy(t, 4 + j, (*chip, c), sibling)
                fwd.start()
                passed.append(fwd)
        for t in range(nt):
            copy(t, 0, sibling, me).wait_recv()
            for j, chip in enumerate(chips):
                copy(t, 4 + j, (*chip, 1 - c), me).wait_recv()
        for cp in first + passed:
            cp.wait_send()
        for cp in mine:
            cp.wait()

    return pl.pallas_call(
        body, name="gather_weights", in_specs=[ANY] * nt, out_specs=[ANY] * nt,
        out_shape=[jax.ShapeDtypeStruct((N_CHIP,) + s.shape, s.dtype) for s in shards],
        scratch_shapes=[pltpu.SemaphoreType.DMA((7 * nt,)), pltpu.SemaphoreType.DMA((7 * nt,)),
                        pltpu.SemaphoreType.DMA((nt,))],
    )(*shards)


def _exchange_big(sources, shapes):
    arrays = []
    for per_dev in sources:
        for arr, _ in per_dev:
            if not any(arr is a for a in arrays):
                arrays.append(arr)
    na, nt = len(arrays), len(sources)

    def index_of(arr):
        return next(i for i, a in enumerate(arrays) if a is arr)

    def body(*refs):
        in_refs, out_refs = refs[:na], refs[na:na + nt]
        send_sems, recv_sems = refs[na + nt:]
        x, y, c = _place()
        me = 4 * x + 2 * y + c
        for t in range(nt):
            for k in range(N_DEV):
                arr, window = sources[t][k]
                src = window(in_refs[index_of(arr)])
                to = (k // 4, (k // 2) % 2, k % 2)

                @pl.when(me == k)
                def _():
                    pltpu.make_async_copy(src, out_refs[t].at[k], send_sems.at[N_DEV * t + k]).start()

                @pl.when(me != k)
                def _():
                    pltpu.make_async_remote_copy(
                        src_ref=src, dst_ref=out_refs[t].at[me], send_sem=send_sems.at[N_DEV * t + k],
                        recv_sem=recv_sems.at[N_DEV * t + me], device_id=to, device_id_type=MESH).start()
        for t in range(nt):
            for k in range(N_DEV):
                arr, window = sources[t][k]
                src = window(in_refs[index_of(arr)])

                @pl.when(me == k)
                def _():
                    pltpu.make_async_copy(src, out_refs[t].at[k], send_sems.at[N_DEV * t + k]).wait()

                @pl.when(me != k)
                def _():
                    cp = pltpu.make_async_remote_copy(
                        src_ref=src, dst_ref=out_refs[t].at[k], send_sem=send_sems.at[N_DEV * t + k],
                        recv_sem=recv_sems.at[N_DEV * t + k], device_id=(k // 4, (k // 2) % 2, k % 2),
                        device_id_type=MESH)
                    cp.wait_send()
                    cp.wait_recv()

    return pl.pallas_call(
        body, name="exchange_big_grads", in_specs=[ANY] * na, out_specs=[ANY] * nt,
        out_shape=[jax.ShapeDtypeStruct((N_DEV,) + s, BF16) for s in shapes],
        scratch_shapes=[pltpu.SemaphoreType.DMA((N_DEV * nt,)), pltpu.SemaphoreType.DMA((N_DEV * nt,))],
    )(*arrays)


def _sum_share(xs, name):
    _, r, cols = xs.shape
    tr = 256 if r % 256 == 0 else r
    nblk = r // tr

    def body(x_ref, out_ref, acc_ref, send_sems, local_sems, recv_sem):
        i = pl.program_id(0)
        slot = i % 2
        x, y, c = _place()

        def copies(s, blk):
            dst = out_ref.at[c, pl.ds(blk * tr, tr), :]
            loc = pltpu.make_async_copy(acc_ref.at[s], dst, local_sems.at[s])
            rem = pltpu.make_async_remote_copy(src_ref=acc_ref.at[s], dst_ref=dst, send_sem=send_sems.at[s],
                                               recv_sem=recv_sem, device_id=(x, y, 1 - c), device_id_type=MESH)
            return loc, rem

        @pl.when(i >= 2)
        def _():
            loc, rem = copies(slot, i - 2)
            loc.wait()
            rem.wait_send()

        acc = x_ref[0].astype(F32)
        for k in range(1, N_DEV):
            acc = acc + x_ref[k].astype(F32)
        acc_ref[slot] = acc
        loc, rem = copies(slot, i)
        loc.start()
        rem.start()

        @pl.when(i == nblk - 1)
        def _():
            for back in range(min(2, nblk)):
                blk = nblk - 1 - back
                loc, rem = copies(blk % 2, blk)
                loc.wait()
                rem.wait_send()
            theirs = out_ref.at[1 - c]
            pltpu.make_async_remote_copy(src_ref=theirs, dst_ref=theirs, send_sem=send_sems.at[0],
                                         recv_sem=recv_sem, device_id=(x, y, 1 - c),
                                         device_id_type=MESH).wait_recv()

    return pl.pallas_call(
        body, name=name, grid=(nblk,),
        in_specs=[pl.BlockSpec((N_DEV, tr, cols), lambda i: (0, i, 0))],
        out_specs=ANY,
        out_shape=jax.ShapeDtypeStruct((2, r, cols), F32),
        scratch_shapes=[pltpu.VMEM((2, tr, cols), F32), pltpu.SemaphoreType.DMA((2,)),
                        pltpu.SemaphoreType.DMA((2,)), pltpu.SemaphoreType.DMA],
        compiler_params=_params("arbitrary"),
    )(xs)


def _sum8(xs, name):
    _, r, cols = xs.shape
    tr = 8
    for cand in (256, 128, 64, 32, 16):
        if r % cand == 0:
            tr = cand
            break

    def body(x_ref, o_ref):
        acc = x_ref[0].astype(F32)
        for k in range(1, N_DEV):
            acc = acc + x_ref[k].astype(F32)
        o_ref[...] = acc

    return pl.pallas_call(
        body, name=name, grid=(r // tr,),
        in_specs=[pl.BlockSpec((N_DEV, tr, cols), lambda i: (0, i, 0))],
        out_specs=pl.BlockSpec((tr, cols), lambda i: (i, 0)),
        out_shape=jax.ShapeDtypeStruct((r, cols), F32),
        compiler_params=_params("parallel"),
    )(xs)


def _adamw(w, g, m, v, name):
    R, C = w.shape
    tr = R
    for cand in (256, 128, 64, 32, 16, 8):
        if R % cand == 0:
            tr = cand
            break
    c1 = 1.0 / (1.0 - ADAM_B1 ** ADAM_STEP)
    c2 = 1.0 / (1.0 - ADAM_B2 ** ADAM_STEP)

    def body(w_ref, g_ref, m_ref, v_ref, d_ref, nm_ref, nv_ref):
        gv = g_ref[...]
        nm = ADAM_B1 * m_ref[...] + (1.0 - ADAM_B1) * gv
        nv = ADAM_B2 * v_ref[...] + (1.0 - ADAM_B2) * (gv * gv)
        d_ref[...] = (-ADAM_LR) * ((nm * c1) / (jnp.sqrt(nv * c2) + ADAM_EPS) + ADAM_WD * w_ref[...])
        nm_ref[...] = nm
        nv_ref[...] = nv

    spec = pl.BlockSpec((tr, C), lambda i: (i, 0))
    shape = jax.ShapeDtypeStruct((R, C), F32)
    return pl.pallas_call(
        body, name=name, grid=(R // tr,), in_specs=[spec] * 4, out_specs=[spec] * 3, out_shape=[shape] * 3,
        compiler_params=_params("parallel"),
    )(w, g, m, v)


def _flat_rows(parts, rows):
    flat = jnp.concatenate([q.reshape(-1) for q in parts])
    flat = jnp.pad(flat, (0, rows * LANES - flat.shape[0]))
    return flat.reshape(rows, LANES)


def _round_up(n, m):
    return (n + m - 1) // m * m


def _block_diag(w):
    H, n, _ = w.shape
    eye = jnp.eye(H, dtype=w.dtype)
    return (eye[:, None, :, None] * w[:, :, None, :]).reshape(H * n, H * n)


def _diag_blocks(w, H, n):
    w4 = w.reshape(H, n, H, n)
    return jnp.stack([w4[h, :, h, :] for h in range(H)])


FF_SHARD = D_FF // 2
FF_HALF = D_FFP // 2
FF_ROWS = D_FF // N_CHIP


def _ffn_weights(g_in, g_out, l):
    pad = lambda a: jnp.pad(a, ((0, 0), (0, FF_HALF - FF_SHARD)))
    wg = jnp.concatenate([pad(g_in[0, l]), pad(g_in[1, l])], axis=1)
    wu = jnp.concatenate([pad(g_in[2, l]), pad(g_in[3, l])], axis=1)
    zeros = jnp.zeros((FF_HALF - FF_SHARD, D), g_out.dtype)
    wout = jnp.concatenate([g_out[0, l], g_out[1, l], zeros, g_out[2, l], g_out[3, l], zeros], axis=0)
    return wg, wu, wout


def _grad_windows(dwg, dwu, dwout, dwin, dwo):
    w_in, w_out, win, wo = [], [], [], []
    for k in range(N_DEV):
        j, l = k // 2, k % 2
        col = (j % 2) * FF_HALF
        row = (j // 2) * FF_HALF + (j % 2) * FF_ROWS
        w_in.append(((dwg if j < 2 else dwu)[l], lambda r, col=col: r.at[:, pl.ds(col, FF_HALF)]))
        w_out.append((dwout[l], lambda r, row=row: r.at[pl.ds(row, FF_ROWS), :]))
        win.append((dwin[l], lambda r, j=j: r.at[j]))
        wo.append((dwo[l], lambda r, j=j: r.at[pl.ds(j * (D // N_CHIP), D // N_CHIP), :]))
    return w_in, w_out, win, wo


def kernel(x, ffn1_norm, ffn1_w_in, ffn1_w_out, mix_norm, w_in, conv_w, conv_b, rg_w_a, rg_b_a, rg_w_x, rg_b_x, lru_lambda, pool_w, pool_scale, sgu_norm, sgu_w, sgu_b, w_out, ffn2_norm, ffn2_w_in, ffn2_w_out, final_norm, loss_target, m_ffn1_norm, m_ffn1_w_in, m_ffn1_w_out, m_mix_norm, m_w_in, m_conv_w, m_conv_b, m_rg_w_a, m_rg_b_a, m_rg_w_x, m_rg_b_x, m_lru_lambda, m_pool_w, m_pool_scale, m_sgu_norm, m_sgu_w, m_sgu_b, m_w_out, m_ffn2_norm, m_ffn2_w_in, m_ffn2_w_out, m_final_norm, v_ffn1_norm, v_ffn1_w_in, v_ffn1_w_out, v_mix_norm, v_w_in, v_conv_w, v_conv_b, v_rg_w_a, v_rg_b_a, v_rg_w_x, v_rg_b_x, v_lru_lambda, v_pool_w, v_pool_scale, v_sgu_norm, v_sgu_w, v_sgu_b, v_w_out, v_ffn2_norm, v_ffn2_w_in, v_ffn2_w_out, v_final_norm):
    args = locals()
    W = {n: args[n] for n in WEIGHTS}
    M = {n: args["m_" + n] for n in WEIGHTS}
    V = {n: args["v_" + n] for n in WEIGHTS}
    depth = ffn1_norm.shape[0]
    T = x.shape[1]
    xi, yi, ci = _place()
    chip = 2 * xi + yi

    assert depth == 2, "core c of a chip sends and reduces layer c"
    gathered = dict(zip(BIG, _gather_big([W[n].astype(BF16) for n in BIG])))
    conv_rows = _round_up(conv_w.size, 8 * LANES) // LANES
    conv_all = _all_gather8(_flat_rows([conv_w], conv_rows), "gather_conv_w").reshape(N_CHIP, 2, -1)
    conv_all = conv_all[:, 0, :conv_w.size].reshape((N_CHIP,) + conv_w.shape)
    conv_full = jnp.concatenate([conv_all[j] for j in range(N_CHIP)], axis=2)

    layers = []
    for l in range(depth):
        L = {}
        for f in ("ffn1", "ffn2"):
            wg, wu, wout = _ffn_weights(gathered[f + "_w_in"], gathered[f + "_w_out"], l)
            L[f] = dict(norm=W[f + "_norm"][l][None], wg=wg, wu=wu, wout=wout)
        ws = jnp.where(jnp.tril(jnp.ones((CHUNK, CHUNK), bool))[None], sgu_w[l], 0.0)
        wax = jnp.concatenate([_block_diag(rg_w_a[l]), _block_diag(rg_w_x[l])], axis=1)
        wpool = _block_diag(pool_w[l])
        L["mix"] = dict(
            conv_w=conv_full[l], conv_b=conv_b[l][None], wax=wax.astype(BF16), wax_t=wax.T.astype(BF16),
            bax=jnp.concatenate([rg_b_a[l].reshape(-1), rg_b_x[l].reshape(-1)])[None], lam=lru_lambda[l][None],
            wpool=wpool.astype(BF16), wpool_t=wpool.T.astype(BF16), pool_scale=pool_scale[l][None],
            sgu_norm=sgu_norm[l][None], ws=ws.astype(BF16), ws_t=jnp.swapaxes(ws, 1, 2).astype(BF16),
            bz=jnp.repeat(sgu_b[l].T, 64, axis=1))
        L["mix_norm"] = mix_norm[l][None]
        L["w_in"] = jnp.concatenate([gathered["w_in"][j, l] for j in range(N_CHIP)], axis=1)
        L["w_out"] = jnp.concatenate([gathered["w_out"][j, l] for j in range(N_CHIP)], axis=0)
        layers.append(L)

    def ffn_fwd(xin, F):
        h = _rms_fwd(xin, F["norm"])
        g, u, a = _ffn_in(h, F["wg"], F["wu"])
        return _mm_res(a, F["wout"], xin, 0.5, "ffn_out"), (xin, h, g, u, a)

    xs = x[0]
    saved = []
    for L in layers:
        x1, s1 = ffn_fwd(xs, L["ffn1"])
        hm = _rms_fwd(x1, L["mix_norm"])
        p = _mm(hm, L["w_in"], F32, "mix_in")
        ycat, hs = _mix_fwd(p, L["mix"])
        x2 = _mm_res(ycat, L["w_out"], x1, 1.0, "mix_out")
        x3, s2 = ffn_fwd(x2, L["ffn2"])
        saved.append((s1, (x1, hm, p, ycat, hs), s2))
        xs = x3

    dx, dxb, d_final, loss_part = _final(xs, loss_target[0], final_norm[None])

    G = {n: [None] * depth for n in SMALL if n != "final_norm"}
    DW = {k: [None] * depth for k in ("ffn1_wg", "ffn1_wu", "ffn1_wout", "ffn2_wg", "ffn2_wu", "ffn2_wout",
                                      "w_in", "w_out")}

    def ffn_bwd(dx, dxb, F, s, f, l):
        xin, h, g, u, a = s
        dg, du = _ffn_mid_bwd(dxb, F["wout"], g, u)
        DW[f + "_wout"][l] = _mm_tn(a, dxb, 0.5, "ffn_dwout")
        DW[f + "_wg"][l] = _mm_tn(h, dg, 1.0, "ffn_dwg")
        DW[f + "_wu"][l] = _mm_tn(h, du, 1.0, "ffn_dwu")
        dh = _mm2_nt(dg, F["wg"], du, F["wu"], "ffn_dh")
        dx, dxb, dn = _rms_bwd(xin, dh, F["norm"], dx)
        G[f + "_norm"][l] = dn[0]
        return dx, dxb

    for l in reversed(range(depth)):
        L = layers[l]
        s1, (x1, hm, p, ycat, hs), s2 = saved[l]
        dx, dxb = ffn_bwd(dx, dxb, L["ffn2"], s2, "ffn2", l)
        dycat = _mm_nt(dxb, L["w_out"], "mix_dy")
        DW["w_out"][l] = _mm_tn(ycat, dxb, 1.0, "mix_dwout")
        mg = _mix_bwd(dycat, p, hs, L["mix"])
        dwin = _mm_tn(hm, mg["dp"], 1.0, "mix_dwin")
        DW["w_in"][l] = jnp.stack(jnp.split(dwin, N_CHIP, axis=1))
        dhm = _mm_nt(mg["dp"], L["w_in"], "mix_dh")
        dx, dxb, dn = _rms_bwd(x1, dhm, L["mix_norm"], dx)
        G["mix_norm"][l] = dn[0]
        G["conv_w"][l], G["conv_b"][l] = mg["conv_w"], mg["conv_b"][0]
        G["rg_w_a"][l] = _diag_blocks(mg["wax"][:, :D_RNN], 8, 64)
        G["rg_w_x"][l] = _diag_blocks(mg["wax"][:, D_RNN:], 8, 64)
        G["rg_b_a"][l] = mg["bax"][0, :D_RNN].reshape(8, 64)
        G["rg_b_x"][l] = mg["bax"][0, D_RNN:].reshape(8, 64)
        G["lru_lambda"][l] = mg["lam"][0]
        G["pool_w"][l] = _diag_blocks(mg["wpool"], 4, 64)
        G["pool_scale"][l], G["sgu_norm"][l] = mg["pool_scale"][0], mg["sgu_norm"][0]
        G["sgu_w"][l] = mg["ws"]
        G["sgu_b"][l] = mg["bz"].reshape(CHUNK, 4, 64).sum(-1).T
        dx, dxb = ffn_bwd(dx, dxb, L["ffn1"], s1, "ffn1", l)
    grad_x = dx[None]
    G = {n: jnp.stack(v) for n, v in G.items()}
    G["final_norm"] = d_final[0]

    w1 = _grad_windows(DW["ffn1_wg"], DW["ffn1_wu"], DW["ffn1_wout"], DW["w_in"], DW["w_out"])
    w2 = _grad_windows(DW["ffn2_wg"], DW["ffn2_wu"], DW["ffn2_wout"], DW["w_in"], DW["w_out"])
    sources = [w1[0], w1[1], w1[2], w1[3], w2[0], w2[1]]
    shapes = [(D, FF_HALF), (FF_ROWS, D), (D, D_IN // N_CHIP), (D // N_CHIP, D), (D, FF_HALF), (FF_ROWS, D)]
    got = _exchange_big(sources, shapes)
    both = [_sum_share(a, "sum_share_" + n) for n, a in zip(BIG, got)]
    grads = {n: (a[:, :, :FF_SHARD] if n.endswith("w_in") and n != "w_in" else a) for n, a in zip(BIG, both)}

    small_sizes = [int(np.prod(G[n].shape)) for n in SMALL]
    srows = _round_up(sum(small_sizes) + 1, N_DEV * 8 * LANES) // (N_DEV * LANES)
    sflat = _flat_rows([G[n] for n in SMALL] + [loss_part[0, :1]], N_DEV * srows)
    sgot = _all_to_all(sflat.reshape(N_DEV, srows, LANES), "exchange_small_grads")
    sall = _all_gather8(_sum8(sgot, "sum_small_grads"), "share_small_grads").reshape(-1)
    off = 0
    for n, size in zip(SMALL, small_sizes):
        grads[n] = sall[off:off + size].reshape(G[n].shape)
        off += size
    loss = sall[off]
    grads["conv_w"] = lax.dynamic_slice_in_dim(grads["conv_w"], chip * conv_w.shape[2], conv_w.shape[2], axis=2)

    delta, new_m, new_v = {}, {}, {}
    for n in BIG:
        shp = W[n].shape
        two_d = (shp[0] * shp[1], shp[2])
        outs = _adamw(W[n].reshape(two_d), grads[n].reshape(two_d), M[n].reshape(two_d), V[n].reshape(two_d),
                      "adamw_" + n)
        delta[n], new_m[n], new_v[n] = (o.reshape(shp) for o in outs)
    arows = _round_up(sum(int(np.prod(W[n].shape)) for n in SMALL), 8 * LANES) // LANES
    outs = _adamw(*(_flat_rows([src[n] for n in SMALL], arows) for src in (W, grads, M, V)), "adamw_small")
    outs = [o.reshape(-1) for o in outs]
    off = 0
    for n in SMALL:
        size = int(np.prod(W[n].shape))
        delta[n], new_m[n], new_v[n] = (o[off:off + size].reshape(W[n].shape) for o in outs)
        off += size

    return (loss, grad_x, *[grads[n] for n in WEIGHTS], *[delta[n] for n in WEIGHTS],
            *[new_m[n] for n in WEIGHTS], *[new_v[n] for n in WEIGHTS])
```

```python
import math

import jax
import jax.numpy as jnp
import numpy as np
from jax import lax
from jax.experimental import pallas as pl
from jax.experimental.pallas import tpu as pltpu

F32 = jnp.float32
BF16 = jnp.bfloat16
MESH = pl.DeviceIdType.MESH

D = 1024
D_RNN = 512
D_POOL = 256
D_SGU = 256
D_IN = 1792
D_FF = 2752
D_FFP = 2816
CHUNK = 128
HALO = 16
EPS = 1e-6
LRU_C = 8.0
N_DEV = 8
N_CHIP = 4
LANES = 1024
VMEM_LIMIT = 56 * 1024 * 1024

ADAM_LR, ADAM_B1, ADAM_B2, ADAM_EPS, ADAM_WD, ADAM_STEP = 0.001, 0.9, 0.999, 1e-08, 0.01, 10

BIG = ("ffn1_w_in", "ffn1_w_out", "w_in", "w_out", "ffn2_w_in", "ffn2_w_out")
SMALL = ("ffn1_norm", "mix_norm", "conv_w", "conv_b", "rg_w_a", "rg_b_a", "rg_w_x", "rg_b_x", "lru_lambda",
         "pool_w", "pool_scale", "sgu_norm", "sgu_w", "sgu_b", "ffn2_norm", "final_norm")
WEIGHTS = ("ffn1_norm", "ffn1_w_in", "ffn1_w_out", "mix_norm", "w_in", "conv_w", "conv_b", "rg_w_a", "rg_b_a",
           "rg_w_x", "rg_b_x", "lru_lambda", "pool_w", "pool_scale", "sgu_norm", "sgu_w", "sgu_b", "w_out",
           "ffn2_norm", "ffn2_w_in", "ffn2_w_out", "final_norm")


def _params(*sem):
    return pltpu.CompilerParams(dimension_semantics=sem, vmem_limit_bytes=VMEM_LIMIT)


def _gelu(x):
    c = math.sqrt(2.0 / math.pi)
    t = jnp.tanh(c * (x + 0.044715 * (x * x * x)))
    return 0.5 * x * (1.0 + t)


def _gelu_and_grad(x):
    c = math.sqrt(2.0 / math.pi)
    x2 = x * x
    t = jnp.tanh(c * (x + 0.044715 * (x2 * x)))
    g = 0.5 * x * (1.0 + t)
    dg = 0.5 * (1.0 + t) + 0.5 * x * (1.0 - t * t) * (c * (1.0 + 3.0 * 0.044715 * x2))
    return g, dg


def _sigmoid(x):
    return 0.5 * jnp.tanh(0.5 * x) + 0.5


def _dot(a, b):
    return jnp.dot(a, b, preferred_element_type=F32)


def _dot_tn(a, b):
    return lax.dot_general(a, b, (((0,), (0,)), ((), ())), preferred_element_type=F32)


def _dot_nt(a, b):
    return lax.dot_general(a, b, (((1,), (1,)), ((), ())), preferred_element_type=F32)


def _rms_fwd(x, g, tm=512):
    T = x.shape[0]
    tm = min(tm, T)

    def body(x_ref, g_ref, o_ref):
        xv = x_ref[...]
        r = lax.rsqrt(jnp.mean(xv * xv, axis=-1, keepdims=True) + EPS)
        o_ref[...] = (xv * r * g_ref[...]).astype(BF16)

    return pl.pallas_call(
        body, name="rms_fwd", grid=(T // tm,),
        in_specs=[pl.BlockSpec((tm, D), lambda i: (i, 0)), pl.BlockSpec((1, D), lambda i: (0, 0))],
        out_specs=pl.BlockSpec((tm, D), lambda i: (i, 0)),
        out_shape=jax.ShapeDtypeStruct((T, D), BF16),
        compiler_params=_params("parallel"),
    )(x, g)


def _rms_bwd(x, dh, g, dres, tm=512):
    T = x.shape[0]
    tm = min(tm, T)

    def body(x_ref, dh_ref, g_ref, dres_ref, dx_ref, dxb_ref, dg_ref):
        xv = x_ref[...]
        r = lax.rsqrt(jnp.mean(xv * xv, axis=-1, keepdims=True) + EPS)
        xhat = xv * r
        dy = dh_ref[...]
        dxhat = dy * g_ref[...]
        dx = dres_ref[...] + r * (dxhat - xhat * jnp.mean(dxhat * xhat, axis=-1, keepdims=True))
        dx_ref[...] = dx
        dxb_ref[...] = dx.astype(BF16)

        @pl.when(pl.program_id(0) == 0)
        def _():
            dg_ref[...] = jnp.zeros_like(dg_ref)

        dg_ref[...] += jnp.sum(dy * xhat, axis=0, keepdims=True)

    row = pl.BlockSpec((tm, D), lambda i: (i, 0))
    vec = pl.BlockSpec((1, D), lambda i: (0, 0))
    return pl.pallas_call(
        body, name="rms_bwd", grid=(T // tm,),
        in_specs=[row, row, vec, row], out_specs=[row, row, vec],
        out_shape=[jax.ShapeDtypeStruct((T, D), F32), jax.ShapeDtypeStruct((T, D), BF16),
                   jax.ShapeDtypeStruct((1, D), F32)],
        compiler_params=_params("arbitrary"),
    )(x, dh, g, dres)


def _tile(n, limit):
    best = 128
    for t in range(128, min(n, limit) + 1, 128):
        if n % t == 0:
            best = t
    assert n % best == 0, (n, limit)
    return best


def _mm(a, b, out_dtype, name, tm=1024, tn=512):
    M, K = a.shape
    N = b.shape[1]
    tm, tn = min(tm, M), _tile(N, tn)

    def body(a_ref, b_ref, o_ref):
        o_ref[...] = _dot(a_ref[...], b_ref[...]).astype(out_dtype)

    return pl.pallas_call(
        body, name=name, grid=(M // tm, N // tn),
        in_specs=[pl.BlockSpec((tm, K), lambda i, j: (i, 0)), pl.BlockSpec((K, tn), lambda i, j: (0, j))],
        out_specs=pl.BlockSpec((tm, tn), lambda i, j: (i, j)),
        out_shape=jax.ShapeDtypeStruct((M, N), out_dtype),
        compiler_params=_params("parallel", "parallel"),
    )(a, b)


def _mm_res(a, b, res, scale, name, tm=1024, tn=512):
    M, K = a.shape
    N = b.shape[1]
    tm, tn = min(tm, M), _tile(N, tn)

    def body(a_ref, b_ref, r_ref, o_ref):
        o_ref[...] = r_ref[...] + scale * _dot(a_ref[...], b_ref[...])

    return pl.pallas_call(
        body, name=name, grid=(M // tm, N // tn),
        in_specs=[pl.BlockSpec((tm, K), lambda i, j: (i, 0)), pl.BlockSpec((K, tn), lambda i, j: (0, j)),
                  pl.BlockSpec((tm, tn), lambda i, j: (i, j))],
        out_specs=pl.BlockSpec((tm, tn), lambda i, j: (i, j)),
        out_shape=jax.ShapeDtypeStruct((M, N), F32),
        compiler_params=_params("parallel", "parallel"),
    )(a, b, res)


def _mm_nt(a, b, name, tm=1024, tn=512):
    M, K = a.shape
    N = b.shape[0]
    tm, tn = min(tm, M), _tile(N, tn)

    def body(a_ref, b_ref, o_ref):
        o_ref[...] = _dot_nt(a_ref[...], b_ref[...])

    return pl.pallas_call(
        body, name=name, grid=(M // tm, N // tn),
        in_specs=[pl.BlockSpec((tm, K), lambda i, j: (i, 0)), pl.BlockSpec((tn, K), lambda i, j: (j, 0))],
        out_specs=pl.BlockSpec((tm, tn), lambda i, j: (i, j)),
        out_shape=jax.ShapeDtypeStruct((M, N), F32),
        compiler_params=_params("parallel", "parallel"),
    )(a, b)


def _mm2_nt(a1, b1, a2, b2, name, tm=1024, tn=512):
    M, K = a1.shape
    N = b1.shape[0]
    tm, tn = min(tm, M), _tile(N, tn)

    def body(a1_ref, b1_ref, a2_ref, b2_ref, o_ref):
        o_ref[...] = _dot_nt(a1_ref[...], b1_ref[...]) + _dot_nt(a2_ref[...], b2_ref[...])

    aspec = pl.BlockSpec((tm, K), lambda i, j: (i, 0))
    bspec = pl.BlockSpec((tn, K), lambda i, j: (j, 0))
    return pl.pallas_call(
        body, name=name, grid=(M // tm, N // tn),
        in_specs=[aspec, bspec, aspec, bspec],
        out_specs=pl.BlockSpec((tm, tn), lambda i, j: (i, j)),
        out_shape=jax.ShapeDtypeStruct((M, N), F32),
        compiler_params=_params("parallel", "parallel"),
    )(a1, b1, a2, b2)


def _mm_tn(a, b, scale, name, tm=1792, tn=1792, tk=512):
    T, M = a.shape
    N = b.shape[1]
    tm, tn, tk = _tile(M, tm), _tile(N, tn), min(tk, T)
    nk = T // tk

    def body(a_ref, b_ref, o_ref, acc_ref):
        k = pl.program_id(2)

        @pl.when(k == 0)
        def _():
            acc_ref[...] = jnp.zeros_like(acc_ref)

        acc_ref[...] += _dot_tn(a_ref[...], b_ref[...])

        @pl.when(k == nk - 1)
        def _():
            o_ref[...] = (scale * acc_ref[...]).astype(BF16)

    return pl.pallas_call(
        body, name=name, grid=(M // tm, N // tn, nk),
        in_specs=[pl.BlockSpec((tk, tm), lambda i, j, k: (k, i)), pl.BlockSpec((tk, tn), lambda i, j, k: (k, j))],
        out_specs=pl.BlockSpec((tm, tn), lambda i, j, k: (i, j)),
        out_shape=jax.ShapeDtypeStruct((M, N), BF16),
        scratch_shapes=[pltpu.VMEM((tm, tn), F32)],
        compiler_params=_params("parallel", "parallel", "arbitrary"),
    )(a, b)


def _ffn_in(h, wg, wu, tm=1024, tn=256):
    T = h.shape[0]
    tm = min(tm, T)

    def body(h_ref, wg_ref, wu_ref, g_ref, u_ref, a_ref):
        hv = h_ref[...]
        g = _dot(hv, wg_ref[...])
        u = _dot(hv, wu_ref[...])
        g_ref[...] = g.astype(BF16)
        u_ref[...] = u.astype(BF16)
        a_ref[...] = (g * _sigmoid(g) * u).astype(BF16)

    wspec = pl.BlockSpec((D, tn), lambda i, j: (0, j))
    ospec = pl.BlockSpec((tm, tn), lambda i, j: (i, j))
    oshape = jax.ShapeDtypeStruct((T, D_FFP), BF16)
    return pl.pallas_call(
        body, name="ffn_in", grid=(T // tm, D_FFP // tn),
        in_specs=[pl.BlockSpec((tm, D), lambda i, j: (i, 0)), wspec, wspec],
        out_specs=[ospec, ospec, ospec], out_shape=[oshape, oshape, oshape],
        compiler_params=_params("parallel", "parallel"),
    )(h, wg, wu)


def _ffn_mid_bwd(dy, wout, g, u, tm=1024, tn=256):
    T = dy.shape[0]
    tm = min(tm, T)

    def body(dy_ref, w_ref, g_ref, u_ref, dg_ref, du_ref):
        da = 0.5 * _dot_nt(dy_ref[...], w_ref[...])
        g = g_ref[...].astype(F32)
        u = u_ref[...].astype(F32)
        s = _sigmoid(g)
        dg_ref[...] = (da * u * (s * (1.0 + g * (1.0 - s)))).astype(BF16)
        du_ref[...] = (da * (g * s)).astype(BF16)

    ospec = pl.BlockSpec((tm, tn), lambda i, j: (i, j))
    oshape = jax.ShapeDtypeStruct((T, D_FFP), BF16)
    return pl.pallas_call(
        body, name="ffn_mid_bwd", grid=(T // tm, D_FFP // tn),
        in_specs=[pl.BlockSpec((tm, D), lambda i, j: (i, 0)), pl.BlockSpec((tn, D), lambda i, j: (j, 0)),
                  ospec, ospec],
        out_specs=[ospec, ospec], out_shape=[oshape, oshape],
        compiler_params=_params("parallel", "parallel"),
    )(dy, wout, g, u)


def _final(x, tgt, gf, tm=512):
    T = x.shape[0]
    tm = min(tm, T)

    def body(x_ref, t_ref, g_ref, dx_ref, dxb_ref, dg_ref, loss_ref):
        xv = x_ref[...]
        r = lax.rsqrt(jnp.mean(xv * xv, axis=-1, keepdims=True) + EPS)
        xhat = xv * r
        err = xhat * g_ref[...] - t_ref[...]
        dy = err * (1.0 / D)
        dxhat = dy * g_ref[...]
        dx = r * (dxhat - xhat * jnp.mean(dxhat * xhat, axis=-1, keepdims=True))
        dx_ref[...] = dx
        dxb_ref[...] = dx.astype(BF16)

        @pl.when(pl.program_id(0) == 0)
        def _():
            dg_ref[...] = jnp.zeros_like(dg_ref)
            loss_ref[...] = jnp.zeros_like(loss_ref)

        dg_ref[...] += jnp.sum(dy * xhat, axis=0, keepdims=True)
        loss_ref[...] += (0.5 / D) * jnp.sum(err * err)

    row = pl.BlockSpec((tm, D), lambda i: (i, 0))
    vec = pl.BlockSpec((1, D), lambda i: (0, 0))
    return pl.pallas_call(
        body, name="final_loss", grid=(T // tm,),
        in_specs=[row, row, vec],
        out_specs=[row, row, vec, pl.BlockSpec((1, 128), lambda i: (0, 0))],
        out_shape=[jax.ShapeDtypeStruct((T, D), F32), jax.ShapeDtypeStruct((T, D), BF16),
                   jax.ShapeDtypeStruct((1, D), F32), jax.ShapeDtypeStruct((1, 128), F32)],
        compiler_params=_params("arbitrary"),
    )(x, tgt, gf)


def _mix_block(T, limit):
    return min(limit, T // 2)


def _rows(tb, width):
    return lax.broadcasted_iota(jnp.int32, (tb, width), 0)


def _rglru_gates(xc, wax_ref, bax_ref, lam_ref):
    pre = _dot(xc.astype(BF16), wax_ref[...]) + bax_ref[...]
    r = _sigmoid(pre[:, :D_RNN])
    ig = _sigmoid(pre[:, D_RNN:])
    z = -lam_ref[...]
    sp = jnp.maximum(z, 0.0) + jnp.log(1.0 + jnp.exp(-jnp.abs(z)))
    log_a = (-LRU_C) * r * sp
    a = jnp.exp(log_a)
    mult = jnp.sqrt(-jnp.tanh(log_a) * (1.0 + a * a))
    return r, ig, sp, a, mult


def _conv(xa_ext, cw_ref, cb_ref):
    y = cb_ref[...] + cw_ref[3:4, :] * xa_ext
    for k in range(1, 4):
        y = y + cw_ref[3 - k:4 - k, :] * pltpu.roll(xa_ext, k, 0)
    return y[HALO:]


def _pool_window_lanes():
    lane = lax.broadcasted_iota(jnp.int32, (1, D_POOL), 1)
    return jnp.where(lane < 64, 2, jnp.where(lane < 128, 4, jnp.where(lane < 192, 8, 16)))


def _pool_select(s2, s4, s8, s16):
    lane = lax.broadcasted_iota(jnp.int32, s2.shape, 1)
    return jnp.where(lane < 64, s2, jnp.where(lane < 128, s4, jnp.where(lane < 192, s8, s16)))


def _pool_diff(xp_ext, t0, tb):
    s2 = xp_ext + pltpu.roll(xp_ext, 1, 0)
    s4 = s2 + pltpu.roll(s2, 2, 0)
    s8 = s4 + pltpu.roll(s4, 4, 0)
    s16 = s8 + pltpu.roll(s8, 8, 0)
    sel = _pool_select(s2, s4, s8, s16)[HALO:]
    cnt = jnp.minimum(t0 + _rows(tb, D_POOL) + 1, _pool_window_lanes()).astype(F32)
    return sel / cnt - xp_ext[HALO:], cnt


def _head_masks():
    lane = lax.broadcasted_iota(jnp.int32, (1, D_SGU), 1)
    return [((lane >= 64 * h) & (lane < 64 * (h + 1))).astype(F32) for h in range(4)]


def _sgu_mix(w_ref, vch, masks):
    z = masks[0] * _dot(w_ref[0], vch)
    for h in range(1, 4):
        z = z + masks[h] * _dot(w_ref[h], vch)
    return z


def _mix_fwd(p, prm):
    T = p.shape[0]
    tb = _mix_block(T, 512)
    nb = T // tb

    def body(p_ref, xah_ref, xph_ref, cw_ref, cb_ref, wax_ref, bax_ref, lam_ref, wp_ref, ps_ref, sgn_ref,
             ws_ref, bz_ref, y_ref, hs_ref, carry_ref):
        i = pl.program_id(0)
        keep = (i > 0).astype(F32)

        @pl.when(i == 0)
        def _():
            carry_ref[...] = jnp.zeros_like(carry_ref)

        xa_ext = jnp.concatenate([xah_ref[...] * keep, p_ref[:, 512:1024]], axis=0)
        xc = _conv(xa_ext, cw_ref, cb_ref)
        r, ig, sp, a, mult = _rglru_gates(xc, wax_ref, bax_ref, lam_ref)
        bv = mult * (ig * xc)
        row = _rows(tb, D_RNN)
        s = 1
        while s < tb:
            m = row >= s
            bv = jnp.where(m, a * pltpu.roll(bv, s, 0) + bv, bv)
            a = jnp.where(m, a * pltpu.roll(a, s, 0), a)
            s *= 2
        h = bv + a * carry_ref[0:1, :]
        hs_ref[...] = h
        last = jnp.sum(jnp.where(_rows(8, D_RNN) == 7, hs_ref[tb - 8:tb, :], 0.0), axis=0, keepdims=True)
        carry_ref[...] = jnp.broadcast_to(last, carry_ref.shape)
        y_ref[:, 0:512] = (_gelu(p_ref[:, 0:512]) * h).astype(BF16)

        xp_ext = jnp.concatenate([xph_ref[...] * keep, p_ref[:, 1024:1280]], axis=0)
        d, _ = _pool_diff(xp_ext, i * tb, tb)
        y_ref[:, 512:768] = (_dot(d.astype(BF16), wp_ref[...]) * ps_ref[...]).astype(BF16)

        ug = _gelu(p_ref[:, 1280:1536])
        vg = _gelu(p_ref[:, 1536:1792])
        rv = lax.rsqrt(jnp.mean(vg * vg, axis=-1, keepdims=True) + EPS)
        vn = (vg * rv * sgn_ref[...]).astype(BF16)
        masks = _head_masks()
        for ci in range(tb // CHUNK):
            sl = slice(ci * CHUNK, (ci + 1) * CHUNK)
            z = _sgu_mix(ws_ref, vn[sl], masks) + bz_ref[...]
            y_ref[sl, 768:1024] = (ug[sl] * z).astype(BF16)

    hb = tb // HALO

    def halo(i):
        return jnp.maximum(i * hb - 1, 0)

    def full(shape):
        return pl.BlockSpec(shape, lambda i: (0,) * len(shape))

    return pl.pallas_call(
        body, name="mix_fwd", grid=(nb,),
        in_specs=[pl.BlockSpec((tb, D_IN), lambda i: (i, 0)),
                  pl.BlockSpec((HALO, D_RNN), lambda i: (halo(i), 1)),
                  pl.BlockSpec((HALO, D_POOL), lambda i: (halo(i), 4)),
                  full((4, D_RNN)), full((1, D_RNN)), full((D_RNN, 2 * D_RNN)), full((1, 2 * D_RNN)),
                  full((1, D_RNN)), full((D_POOL, D_POOL)), full((1, D_POOL)), full((1, D_SGU)),
                  full((4, CHUNK, CHUNK)), full((CHUNK, D_SGU))],
        out_specs=[pl.BlockSpec((tb, D), lambda i: (i, 0)), pl.BlockSpec((tb, D_RNN), lambda i: (i, 0))],
        out_shape=[jax.ShapeDtypeStruct((T, D), BF16), jax.ShapeDtypeStruct((T, D_RNN), F32)],
        scratch_shapes=[pltpu.VMEM((8, D_RNN), F32)],
        compiler_params=_params("arbitrary"),
    )(p, p, p, prm["conv_w"], prm["conv_b"], prm["wax"], prm["bax"], prm["lam"], prm["wpool"], prm["pool_scale"],
      prm["sgu_norm"], prm["ws"], prm["bz"])


def _mix_bwd(dy, p, hs, prm):
    T = p.shape[0]
    tb = _mix_block(T, 256)
    nb = T // tb
    hb = tb // HALO

    def body(dy_ref, p_ref, xah_ref, xph_ref, hs_ref, hsh_ref, cw_ref, cb_ref, wax_ref, waxt_ref, bax_ref,
             lam_ref, wp_ref, wpt_ref, ps_ref, sgn_ref, ws_ref, wst_ref, bz_ref,
             dp_ref, dcw_ref, dcb_ref, dwax_ref, dbax_ref, dlam_ref, dwp_ref, dps_ref, dsgn_ref, dws_ref,
             dbz_ref, gcarry_ref, xcfut_ref, mfut_ref):
        i = pl.program_id(0)
        bi = nb - 1 - i
        keep = (bi > 0).astype(F32)

        @pl.when(i == 0)
        def _():
            for ref in (dcw_ref, dcb_ref, dwax_ref, dbax_ref, dlam_ref, dwp_ref, dps_ref, dsgn_ref, dws_ref,
                        dbz_ref, gcarry_ref, xcfut_ref, mfut_ref):
                ref[...] = jnp.zeros_like(ref)

        xa_ext = jnp.concatenate([xah_ref[...] * keep, p_ref[:, 512:1024]], axis=0)
        xc = _conv(xa_ext, cw_ref, cb_ref)
        r, ig, sp, a, mult = _rglru_gates(xc, wax_ref, bax_ref, lam_ref)
        gg, dgg = _gelu_and_grad(p_ref[:, 0:512])
        dya = dy_ref[:, 0:512]
        dp_ref[:, 0:512] = (dya * hs_ref[...] * dgg).astype(BF16)
        row = _rows(tb, D_RNN)
        g = dya * gg + jnp.where(row == tb - 1, gcarry_ref[0:1, :], 0.0)
        al = pltpu.roll(a, tb - 1, 0)
        s = 1
        while s < tb:
            m = row < tb - s
            g = jnp.where(m, al * pltpu.roll(g, tb - s, 0) + g, g)
            al = jnp.where(m, al * pltpu.roll(al, tb - s, 0), al)
            s *= 2
        first = jnp.sum(jnp.where(_rows(8, D_RNN) == 0, (a * g)[0:8], 0.0), axis=0, keepdims=True)
        gcarry_ref[...] = jnp.broadcast_to(first, gcarry_ref.shape)
        hs_ext = jnp.concatenate([hsh_ref[...] * keep, hs_ref[...]], axis=0)
        h_prev = pltpu.roll(hs_ext, 1, 0)[HALO:]
        ix = ig * xc
        dlog_a = g * h_prev * a - (g * ix) * (a * a / mult)
        dlam_ref[...] += jnp.sum(dlog_a * r, axis=0, keepdims=True) * (LRU_C * _sigmoid(-lam_ref[...]))
        dpre_r = dlog_a * ((-LRU_C) * sp) * (r * (1.0 - r))
        dpre_i = (g * mult * xc) * (ig * (1.0 - ig))
        dpre = jnp.concatenate([dpre_r, dpre_i], axis=1)
        dbax_ref[...] += jnp.sum(dpre, axis=0, keepdims=True)
        dpre_b = dpre.astype(BF16)
        dwax_ref[...] += _dot_tn(xc.astype(BF16), dpre_b)
        dxc = g * mult * ig + _dot(dpre_b, waxt_ref[...])
        dcb_ref[...] += jnp.sum(dxc, axis=0, keepdims=True)
        for k in range(4):
            xs = xa_ext[HALO:] if k == 3 else pltpu.roll(xa_ext, 3 - k, 0)[HALO:]
            dcw_ref[k:k + 1, :] += jnp.sum(dxc * xs, axis=0, keepdims=True)
        dxc_ext = jnp.concatenate([dxc, xcfut_ref[...]], axis=0)
        n = tb + HALO
        dxa = cw_ref[3:4, :] * dxc_ext
        for k in range(1, 4):
            dxa = dxa + cw_ref[3 - k:4 - k, :] * pltpu.roll(dxc_ext, n - k, 0)
        dp_ref[:, 512:1024] = dxa[:tb].astype(BF16)
        xcfut_ref[...] = dxc[0:HALO]

        xp_ext = jnp.concatenate([xph_ref[...] * keep, p_ref[:, 1024:1280]], axis=0)
        d, cnt = _pool_diff(xp_ext, bi * tb, tb)
        db = d.astype(BF16)
        dyb = dy_ref[:, 512:768]
        dps_ref[...] += jnp.sum(dyb * _dot(db, wp_ref[...]), axis=0, keepdims=True)
        dq = (dyb * ps_ref[...]).astype(BF16)
        dwp_ref[...] += _dot_tn(db, dq)
        dd = _dot(dq, wpt_ref[...])
        mm = dd / cnt
        m_ext = jnp.concatenate([mm, mfut_ref[...]], axis=0)
        f2 = m_ext + pltpu.roll(m_ext, n - 1, 0)
        f4 = f2 + pltpu.roll(f2, n - 2, 0)
        f8 = f4 + pltpu.roll(f4, n - 4, 0)
        f16 = f8 + pltpu.roll(f8, n - 8, 0)
        dp_ref[:, 1024:1280] = (_pool_select(f2, f4, f8, f16)[:tb] - dd).astype(BF16)
        mfut_ref[...] = mm[0:HALO]

        ug, dug = _gelu_and_grad(p_ref[:, 1280:1536])
        vg, dvg = _gelu_and_grad(p_ref[:, 1536:1792])
        rv = lax.rsqrt(jnp.mean(vg * vg, axis=-1, keepdims=True) + EPS)
        vhat = vg * rv
        vn = (vhat * sgn_ref[...]).astype(BF16)
        dyc = dy_ref[:, 768:1024]
        masks = _head_masks()
        dz = dyc * ug
        dzb = dz.astype(BF16)
        dvn_parts = []
        for ci in range(tb // CHUNK):
            sl = slice(ci * CHUNK, (ci + 1) * CHUNK)
            z = _sgu_mix(ws_ref, vn[sl], masks) + bz_ref[...]
            dp_ref[sl, 1280:1536] = (dyc[sl] * z * dug[sl]).astype(BF16)
            dbz_ref[...] += dz[sl]
            for h in range(4):
                dws_ref[h] += _dot_nt((dz[sl] * masks[h]).astype(BF16), vn[sl])
            dvn_parts.append(_sgu_mix(wst_ref, dzb[sl], masks))
        dvn = jnp.concatenate(dvn_parts, axis=0)
        dsgn_ref[...] += jnp.sum(dvn * vhat, axis=0, keepdims=True)
        dvhat = dvn * sgn_ref[...]
        dvg_in = rv * (dvhat - vhat * jnp.mean(dvhat * vhat, axis=-1, keepdims=True))
        dp_ref[:, 1536:1792] = (dvg_in * dvg).astype(BF16)

        @pl.when(i == nb - 1)
        def _():
            tril = (lax.broadcasted_iota(jnp.int32, (CHUNK, CHUNK), 0)
                    >= lax.broadcasted_iota(jnp.int32, (CHUNK, CHUNK), 1)).astype(F32)
            for h in range(4):
                dws_ref[h] = dws_ref[h] * tril

    def blk(i):
        return nb - 1 - i

    def halo(i):
        return jnp.maximum(blk(i) * hb - 1, 0)

    def full(shape):
        return pl.BlockSpec(shape, lambda i: (0,) * len(shape))

    small_shapes = [(4, D_RNN), (1, D_RNN), (D_RNN, 2 * D_RNN), (1, 2 * D_RNN), (1, D_RNN), (D_POOL, D_POOL),
                    (1, D_POOL), (1, D_SGU), (4, CHUNK, CHUNK), (CHUNK, D_SGU)]
    outs = pl.pallas_call(
        body, name="mix_bwd", grid=(nb,),
        in_specs=[pl.BlockSpec((tb, D), lambda i: (blk(i), 0)),
                  pl.BlockSpec((tb, D_IN), lambda i: (blk(i), 0)),
                  pl.BlockSpec((HALO, D_RNN), lambda i: (halo(i), 1)),
                  pl.BlockSpec((HALO, D_POOL), lambda i: (halo(i), 4)),
                  pl.BlockSpec((tb, D_RNN), lambda i: (blk(i), 0)),
                  pl.BlockSpec((HALO, D_RNN), lambda i: (halo(i), 0)),
                  full((4, D_RNN)), full((1, D_RNN)), full((D_RNN, 2 * D_RNN)), full((2 * D_RNN, D_RNN)),
                  full((1, 2 * D_RNN)), full((1, D_RNN)), full((D_POOL, D_POOL)), full((D_POOL, D_POOL)),
                  full((1, D_POOL)), full((1, D_SGU)), full((4, CHUNK, CHUNK)), full((4, CHUNK, CHUNK)),
                  full((CHUNK, D_SGU))],
        out_specs=[pl.BlockSpec((tb, D_IN), lambda i: (blk(i), 0))] + [full(s) for s in small_shapes],
        out_shape=[jax.ShapeDtypeStruct((T, D_IN), BF16)] + [jax.ShapeDtypeStruct(s, F32) for s in small_shapes],
        scratch_shapes=[pltpu.VMEM((8, D_RNN), F32), pltpu.VMEM((HALO, D_RNN), F32),
                        pltpu.VMEM((HALO, D_POOL), F32)],
        compiler_params=_params("arbitrary"),
    )(dy, p, p, p, hs, hs, prm["conv_w"], prm["conv_b"], prm["wax"], prm["wax_t"], prm["bax"], prm["lam"],
      prm["wpool"], prm["wpool_t"], prm["pool_scale"], prm["sgu_norm"], prm["ws"], prm["ws_t"], prm["bz"])
    names = ("dp", "conv_w", "conv_b", "wax", "bax", "lam", "wpool", "pool_scale", "sgu_norm", "ws", "bz")
    return dict(zip(names, outs))


ANY = pl.BlockSpec(memory_space=pl.ANY)


def _place():
    x, y, c = lax.axis_index("x"), lax.axis_index("y"), lax.axis_index("c")
    return x, y, c


def _all_to_all(xs, name):
    def body(in_ref, out_ref, send_sems, recv_sems, local_sem):
        x, y, c = _place()
        me = 4 * x + 2 * y + c
        mine = pltpu.make_async_copy(in_ref.at[me], out_ref.at[me], local_sem)
        mine.start()
        copies = []
        for rel in range(1, N_DEV):
            tx = 1 - x if rel & 4 else x
            ty = 1 - y if rel & 2 else y
            tc = 1 - c if rel & 1 else c
            cp = pltpu.make_async_remote_copy(
                src_ref=in_ref.at[4 * tx + 2 * ty + tc], dst_ref=out_ref.at[me],
                send_sem=send_sems.at[rel - 1], recv_sem=recv_sems.at[rel - 1],
                device_id=(tx, ty, tc), device_id_type=MESH)
            cp.start()
            copies.append(cp)
        for cp in copies:
            cp.wait()
        mine.wait()

    return pl.pallas_call(
        body, name=name, in_specs=[ANY], out_specs=ANY,
        out_shape=jax.ShapeDtypeStruct(xs.shape, xs.dtype),
        scratch_shapes=[pltpu.SemaphoreType.DMA((N_DEV - 1,)), pltpu.SemaphoreType.DMA((N_DEV - 1,)),
                        pltpu.SemaphoreType.DMA],
    )(xs)


def _all_gather8(xs, name):
    def body(x_ref, out_ref, send_sems, recv_sems, local_sem):
        x, y, c = _place()
        me, sibling = (x, y, c), (x, y, 1 - c)
        chips = [(1 - x, y), (x, 1 - y), (1 - x, 1 - y)]

        def rows(px, py, pc):
            return out_ref.at[4 * px + 2 * py + pc]

        def copy(k, block, to, src=None):
            return pltpu.make_async_remote_copy(
                src_ref=rows(*block) if src is None else src, dst_ref=rows(*block),
                send_sem=send_sems.at[k], recv_sem=recv_sems.at[k], device_id=to, device_id_type=MESH)

        mine = pltpu.make_async_copy(x_ref, rows(*me), local_sem)
        mine.start()
        first = [copy(0, me, sibling, src=x_ref)]
        first += [copy(1 + j, me, (*chip, c), src=x_ref) for j, chip in enumerate(chips)]
        for cp in first:
            cp.start()
        passed = [copy(4 + j, (*chip, c), sibling) for j, chip in enumerate(chips)]
        for j, chip in enumerate(chips):
            copy(1 + j, (*chip, c), me).wait_recv()
            passed[j].start()
        copy(0, sibling, me).wait_recv()
        for j, chip in enumerate(chips):
            copy(4 + j, (*chip, 1 - c), me).wait_recv()
        for cp in first + passed:
            cp.wait_send()
        mine.wait()

    return pl.pallas_call(
        body, name=name, in_specs=[ANY], out_specs=ANY,
        out_shape=jax.ShapeDtypeStruct((N_DEV,) + xs.shape, xs.dtype),
        scratch_shapes=[pltpu.SemaphoreType.DMA((7,)), pltpu.SemaphoreType.DMA((7,)), pltpu.SemaphoreType.DMA],
    )(xs)


def _gather_big(shards):
    nt = len(shards)

    def body(*refs):
        in_refs, out_refs = refs[:nt], refs[nt:2 * nt]
        send_sems, recv_sems, local_sems = refs[2 * nt:]
        x, y, c = _place()
        me, sibling = (x, y, c), (x, y, 1 - c)
        chips = [(1 - x, y), (x, 1 - y), (1 - x, 1 - y)]

        def block(t, px, py, pc):
            return out_refs[t].at[2 * px + py, pc]

        def copy(t, k, blk, to, src=None):
            return pltpu.make_async_remote_copy(
                src_ref=block(t, *blk) if src is None else src, dst_ref=block(t, *blk),
                send_sem=send_sems.at[7 * t + k], recv_sem=recv_sems.at[7 * t + k], device_id=to,
                device_id_type=MESH)

        mine, first, passed = [], [], []
        for t in range(nt):
            own = in_refs[t].at[c]
            mine.append(pltpu.make_async_copy(own, block(t, *me), local_sems.at[t]))
            first.append(copy(t, 0, me, sibling, src=own))
            first += [copy(t, 1 + j, me, (*chip, c), src=own) for j, chip in enumerate(chips)]
        for cp in mine + first:
            cp.start()
        for j, chip in enumerate(chips):
            for t in range(nt):
                copy(t, 1 + j, (*chip, c), me).wait_recv()
                fwd = copy(t, 4 + j, (*chip, c), sibling)
                fwd.start()
                passed.append(fwd)
        for t in range(nt):
            copy(t, 0, sibling, me).wait_recv()
            for j, chip in enumerate(chips):
                copy(t, 4 + j, (*chip, 1 - c), me).wait_recv()
        for cp in first + passed:
            cp.wait_send()
        for cp in mine:
            cp.wait()

    return pl.pallas_call(
        body, name="gather_weights", in_specs=[ANY] * nt, out_specs=[ANY] * nt,
        out_shape=[jax.ShapeDtypeStruct((N_CHIP,) + s.shape, s.dtype) for s in shards],
        scratch_shapes=[pltpu.SemaphoreType.DMA((7 * nt,)), pltpu.SemaphoreType.DMA((7 * nt,)),
                        pltpu.SemaphoreType.DMA((nt,))],
    )(*shards)


HBM = pl.BlockSpec(memory_space=pltpu.HBM)
SEM = pl.BlockSpec(memory_space=pltpu.SEMAPHORE)
EFFECT = pltpu.SideEffectType.DATAFLOW_SIDE_EFFECTING


def _in_hbm(a):
    return pltpu.with_memory_space_constraint(a, pltpu.HBM)


def _unique(windows):
    arrays = []
    for per_chip in windows:
        for arr, _ in per_chip:
            if not any(arr is a for a in arrays):
                arrays.append(arr)
    return arrays


def _exchange_start(layer, windows, lands, name):
    arrays = _unique(windows)
    na, nt = len(arrays), len(windows)

    def body(*refs):
        in_refs, land_refs = refs[:na], refs[na:na + nt]
        send_sems, recv_sems = refs[na + nt], refs[na + nt + 1]
        token = refs[-1]
        x, y, c = _place()
        me = 4 * x + 2 * y + c
        for t in range(nt):
            for j in range(N_CHIP):
                arr, window = windows[t][j]
                src = window(in_refs[next(i for i, a in enumerate(arrays) if a is arr)])

                @pl.when(me != 2 * j + layer)
                def _():
                    pltpu.make_async_remote_copy(
                        src_ref=src, dst_ref=land_refs[t].at[me], send_sem=send_sems.at[N_CHIP * t + j],
                        recv_sem=recv_sems.at[N_DEV * t + me], device_id=(j // 2, j % 2, layer),
                        device_id_type=MESH).start()
        token[...] = jnp.zeros_like(token)

    outs = pl.pallas_call(
        body, name=name,
        out_shape=(pltpu.SemaphoreType.DMA((N_CHIP * nt,)), pltpu.SemaphoreType.DMA((N_DEV * nt,)),
                   *[pltpu.HBM(a.shape, a.dtype) for a in lands], jax.ShapeDtypeStruct((8, 128), F32)),
        in_specs=[HBM] * (na + nt),
        out_specs=(SEM, SEM, *[HBM] * nt, pl.BlockSpec(memory_space=pltpu.VMEM)),
        input_output_aliases={na + t: 2 + t for t in range(nt)},
        compiler_params=pltpu.CompilerParams(has_side_effects=EFFECT),
    )(*[_in_hbm(a) for a in arrays], *[_in_hbm(a) for a in lands])
    return outs[0], outs[1], list(outs[2:2 + nt]), outs[-1]


def _exchange_wait(layer, windows, lands, send_sems, recv_sems, after, name):
    arrays = _unique(windows)
    na, nt = len(arrays), len(windows)

    def body(*refs):
        in_refs, land_refs = refs[:na], refs[na:na + nt]
        send_sems, recv_sems = refs[na + nt], refs[na + nt + 1]
        local_sem = refs[-1]
        x, y, c = _place()
        me = 4 * x + 2 * y + c

        def source(t, j):
            arr, window = windows[t][j]
            return window(in_refs[next(i for i, a in enumerate(arrays) if a is arr)])

        for t in range(nt):
            for j in range(N_CHIP):
                @pl.when(me != 2 * j + layer)
                def _():
                    pltpu.make_async_remote_copy(
                        src_ref=source(t, j), dst_ref=land_refs[t].at[me], send_sem=send_sems.at[N_CHIP * t + j],
                        recv_sem=recv_sems.at[N_DEV * t + me], device_id=(j // 2, j % 2, layer),
                        device_id_type=MESH).wait_send()

        @pl.when(c == layer)
        def _():
            for t in range(nt):
                for s in range(N_DEV):
                    @pl.when(me != s)
                    def _():
                        slot = land_refs[t].at[s]
                        pltpu.make_async_remote_copy(
                            src_ref=slot, dst_ref=slot, send_sem=send_sems.at[N_CHIP * t],
                            recv_sem=recv_sems.at[N_DEV * t + s], device_id=(x, y, c),
                            device_id_type=MESH).wait_recv()
                for j in range(N_CHIP):
                    @pl.when(me == 2 * j + layer)
                    def _():
                        mine = pltpu.make_async_copy(source(t, j), land_refs[t].at[me], local_sem)
                        mine.start()
                        mine.wait()

    outs = pl.pallas_call(
        body, name=name,
        out_shape=tuple(pltpu.HBM(a.shape, a.dtype) for a in lands),
        in_specs=[HBM] * (na + nt) + [SEM, SEM, ANY],
        out_specs=tuple([HBM] * nt),
        input_output_aliases={na + t: t for t in range(nt)},
        scratch_shapes=[pltpu.SemaphoreType.DMA],
        compiler_params=pltpu.CompilerParams(has_side_effects=EFFECT),
    )(*[_in_hbm(a) for a in arrays], *lands, send_sems, recv_sems, after)
    return list(outs)


def _tie(a, token):
    def body(a_ref, token_ref, o_ref):
        pass

    return pl.pallas_call(
        body, name="tie", in_specs=[ANY, ANY], out_specs=ANY,
        out_shape=jax.ShapeDtypeStruct(a.shape, a.dtype), input_output_aliases={0: 0},
    )(a, token)


def _sum_share(xs, name):
    _, r, cols = xs.shape
    tr = 256 if r % 256 == 0 else r
    nblk = r // tr

    def body(x_ref, out_ref, acc_ref, send_sems, local_sems, recv_sem):
        i = pl.program_id(0)
        slot = i % 2
        x, y, c = _place()

        def copies(s, blk):
            dst = out_ref.at[c, pl.ds(blk * tr, tr), :]
            loc = pltpu.make_async_copy(acc_ref.at[s], dst, local_sems.at[s])
            rem = pltpu.make_async_remote_copy(src_ref=acc_ref.at[s], dst_ref=dst, send_sem=send_sems.at[s],
                                               recv_sem=recv_sem, device_id=(x, y, 1 - c), device_id_type=MESH)
            return loc, rem

        @pl.when(i >= 2)
        def _():
            loc, rem = copies(slot, i - 2)
            loc.wait()
            rem.wait_send()

        acc = x_ref[0].astype(F32)
        for k in range(1, N_DEV):
            acc = acc + x_ref[k].astype(F32)
        acc_ref[slot] = acc
        loc, rem = copies(slot, i)
        loc.start()
        rem.start()

        @pl.when(i == nblk - 1)
        def _():
            for back in range(min(2, nblk)):
                blk = nblk - 1 - back
                loc, rem = copies(blk % 2, blk)
                loc.wait()
                rem.wait_send()
            theirs = out_ref.at[1 - c]
            pltpu.make_async_remote_copy(src_ref=theirs, dst_ref=theirs, send_sem=send_sems.at[0],
                                         recv_sem=recv_sem, device_id=(x, y, 1 - c),
                                         device_id_type=MESH).wait_recv()

    return pl.pallas_call(
        body, name=name, grid=(nblk,),
        in_specs=[pl.BlockSpec((N_DEV, tr, cols), lambda i: (0, i, 0))],
        out_specs=ANY,
        out_shape=jax.ShapeDtypeStruct((2, r, cols), F32),
        scratch_shapes=[pltpu.VMEM((2, tr, cols), F32), pltpu.SemaphoreType.DMA((2,)),
                        pltpu.SemaphoreType.DMA((2,)), pltpu.SemaphoreType.DMA],
        compiler_params=_params("arbitrary"),
    )(xs)


def _sum8(xs, name):
    _, r, cols = xs.shape
    tr = 8
    for cand in (256, 128, 64, 32, 16):
        if r % cand == 0:
            tr = cand
            break

    def body(x_ref, o_ref):
        acc = x_ref[0].astype(F32)
        for k in range(1, N_DEV):
            acc = acc + x_ref[k].astype(F32)
        o_ref[...] = acc

    return pl.pallas_call(
        body, name=name, grid=(r // tr,),
        in_specs=[pl.BlockSpec((N_DEV, tr, cols), lambda i: (0, i, 0))],
        out_specs=pl.BlockSpec((tr, cols), lambda i: (i, 0)),
        out_shape=jax.ShapeDtypeStruct((r, cols), F32),
        compiler_params=_params("parallel"),
    )(xs)


def _adamw(w, g, m, v, name):
    R, C = w.shape
    tr = R
    for cand in (256, 128, 64, 32, 16, 8):
        if R % cand == 0:
            tr = cand
            break
    c1 = 1.0 / (1.0 - ADAM_B1 ** ADAM_STEP)
    c2 = 1.0 / (1.0 - ADAM_B2 ** ADAM_STEP)

    def body(w_ref, g_ref, m_ref, v_ref, d_ref, nm_ref, nv_ref):
        gv = g_ref[...]
        nm = ADAM_B1 * m_ref[...] + (1.0 - ADAM_B1) * gv
        nv = ADAM_B2 * v_ref[...] + (1.0 - ADAM_B2) * (gv * gv)
        d_ref[...] = (-ADAM_LR) * ((nm * c1) / (jnp.sqrt(nv * c2) + ADAM_EPS) + ADAM_WD * w_ref[...])
        nm_ref[...] = nm
        nv_ref[...] = nv

    spec = pl.BlockSpec((tr, C), lambda i: (i, 0))
    shape = jax.ShapeDtypeStruct((R, C), F32)
    return pl.pallas_call(
        body, name=name, grid=(R // tr,), in_specs=[spec] * 4, out_specs=[spec] * 3, out_shape=[shape] * 3,
        compiler_params=_params("parallel"),
    )(w, g, m, v)


def _flat_rows(parts, rows):
    flat = jnp.concatenate([q.reshape(-1) for q in parts])
    flat = jnp.pad(flat, (0, rows * LANES - flat.shape[0]))
    return flat.reshape(rows, LANES)


def _round_up(n, m):
    return (n + m - 1) // m * m


def _block_diag(w):
    H, n, _ = w.shape
    eye = jnp.eye(H, dtype=w.dtype)
    return (eye[:, None, :, None] * w[:, :, None, :]).reshape(H * n, H * n)


def _diag_blocks(w, H, n):
    w4 = w.reshape(H, n, H, n)
    return jnp.stack([w4[h, :, h, :] for h in range(H)])


FF_SHARD = D_FF // 2
FF_HALF = D_FFP // 2
FF_ROWS = D_FF // N_CHIP


def _ffn_weights(g_in, g_out, l):
    pad = lambda a: jnp.pad(a, ((0, 0), (0, FF_HALF - FF_SHARD)))
    wg = jnp.concatenate([pad(g_in[0, l]), pad(g_in[1, l])], axis=1)
    wu = jnp.concatenate([pad(g_in[2, l]), pad(g_in[3, l])], axis=1)
    zeros = jnp.zeros((FF_HALF - FF_SHARD, D), g_out.dtype)
    wout = jnp.concatenate([g_out[0, l], g_out[1, l], zeros, g_out[2, l], g_out[3, l], zeros], axis=0)
    return wg, wu, wout


LAND_SHAPES = {"ffn1_w_in": (D, FF_HALF), "ffn1_w_out": (FF_ROWS, D), "w_in": (D, D_IN // N_CHIP),
               "w_out": (D // N_CHIP, D), "ffn2_w_in": (D, FF_HALF), "ffn2_w_out": (FF_ROWS, D)}


def _ffn_grad_windows(dwg, dwu, dwout):
    w_in, w_out = [], []
    for j in range(N_CHIP):
        col = (j % 2) * FF_HALF
        row = (j // 2) * FF_HALF + (j % 2) * FF_ROWS
        w_in.append((dwg if j < 2 else dwu, lambda r, col=col: r.at[:, pl.ds(col, FF_HALF)]))
        w_out.append((dwout, lambda r, row=row: r.at[pl.ds(row, FF_ROWS), :]))
    return w_in, w_out


def _mix_grad_windows(dwin, dwo):
    rows = D // N_CHIP
    win = [(dwin, lambda r, j=j: r.at[j]) for j in range(N_CHIP)]
    wo = [(dwo, lambda r, j=j: r.at[pl.ds(j * rows, rows), :]) for j in range(N_CHIP)]
    return win, wo


def kernel(x, ffn1_norm, ffn1_w_in, ffn1_w_out, mix_norm, w_in, conv_w, conv_b, rg_w_a, rg_b_a, rg_w_x, rg_b_x, lru_lambda, pool_w, pool_scale, sgu_norm, sgu_w, sgu_b, w_out, ffn2_norm, ffn2_w_in, ffn2_w_out, final_norm, loss_target, m_ffn1_norm, m_ffn1_w_in, m_ffn1_w_out, m_mix_norm, m_w_in, m_conv_w, m_conv_b, m_rg_w_a, m_rg_b_a, m_rg_w_x, m_rg_b_x, m_lru_lambda, m_pool_w, m_pool_scale, m_sgu_norm, m_sgu_w, m_sgu_b, m_w_out, m_ffn2_norm, m_ffn2_w_in, m_ffn2_w_out, m_final_norm, v_ffn1_norm, v_ffn1_w_in, v_ffn1_w_out, v_mix_norm, v_w_in, v_conv_w, v_conv_b, v_rg_w_a, v_rg_b_a, v_rg_w_x, v_rg_b_x, v_lru_lambda, v_pool_w, v_pool_scale, v_sgu_norm, v_sgu_w, v_sgu_b, v_w_out, v_ffn2_norm, v_ffn2_w_in, v_ffn2_w_out, v_final_norm):
    args = locals()
    W = {n: args[n] for n in WEIGHTS}
    M = {n: args["m_" + n] for n in WEIGHTS}
    V = {n: args["v_" + n] for n in WEIGHTS}
    depth = ffn1_norm.shape[0]
    T = x.shape[1]
    xi, yi, ci = _place()
    chip = 2 * xi + yi

    assert depth == 2, "core c of a chip sends and reduces layer c"
    gathered = dict(zip(BIG, _gather_big([W[n].astype(BF16) for n in BIG])))
    conv_rows = _round_up(conv_w.size, 8 * LANES) // LANES
    conv_all = _all_gather8(_flat_rows([conv_w], conv_rows), "gather_conv_w").reshape(N_CHIP, 2, -1)
    conv_all = conv_all[:, 0, :conv_w.size].reshape((N_CHIP,) + conv_w.shape)
    conv_full = jnp.concatenate([conv_all[j] for j in range(N_CHIP)], axis=2)

    layers = []
    for l in range(depth):
        L = {}
        for f in ("ffn1", "ffn2"):
            wg, wu, wout = _ffn_weights(gathered[f + "_w_in"], gathered[f + "_w_out"], l)
            L[f] = dict(norm=W[f + "_norm"][l][None], wg=wg, wu=wu, wout=wout)
        ws = jnp.where(jnp.tril(jnp.ones((CHUNK, CHUNK), bool))[None], sgu_w[l], 0.0)
        wax = jnp.concatenate([_block_diag(rg_w_a[l]), _block_diag(rg_w_x[l])], axis=1)
        wpool = _block_diag(pool_w[l])
        L["mix"] = dict(
            conv_w=conv_full[l], conv_b=conv_b[l][None], wax=wax.astype(BF16), wax_t=wax.T.astype(BF16),
            bax=jnp.concatenate([rg_b_a[l].reshape(-1), rg_b_x[l].reshape(-1)])[None], lam=lru_lambda[l][None],
            wpool=wpool.astype(BF16), wpool_t=wpool.T.astype(BF16), pool_scale=pool_scale[l][None],
            sgu_norm=sgu_norm[l][None], ws=ws.astype(BF16), ws_t=jnp.swapaxes(ws, 1, 2).astype(BF16),
            bz=jnp.repeat(sgu_b[l].T, 64, axis=1))
        L["mix_norm"] = mix_norm[l][None]
        L["w_in"] = jnp.concatenate([gathered["w_in"][j, l] for j in range(N_CHIP)], axis=1)
        L["w_out"] = jnp.concatenate([gathered["w_out"][j, l] for j in range(N_CHIP)], axis=0)
        layers.append(L)

    def ffn_fwd(xin, F):
        h = _rms_fwd(xin, F["norm"])
        g, u, a = _ffn_in(h, F["wg"], F["wu"])
        return _mm_res(a, F["wout"], xin, 0.5, "ffn_out"), (xin, h, g, u, a)

    xs = x[0]
    saved = []
    for L in layers:
        x1, s1 = ffn_fwd(xs, L["ffn1"])
        hm = _rms_fwd(x1, L["mix_norm"])
        p = _mm(hm, L["w_in"], F32, "mix_in")
        ycat, hs = _mix_fwd(p, L["mix"])
        x2 = _mm_res(ycat, L["w_out"], x1, 1.0, "mix_out")
        x3, s2 = ffn_fwd(x2, L["ffn2"])
        saved.append((s1, (x1, hm, p, ycat, hs), s2))
        xs = x3

    dx, dxb, d_final, loss_part = _final(xs, loss_target[0], final_norm[None])

    G = {n: [None] * depth for n in SMALL if n != "final_norm"}
    lands = {n: lax.empty((N_DEV,) + LAND_SHAPES[n], BF16) for n in BIG}
    in_flight = []

    def send_grads(l, windows, tag):
        names = list(windows)
        send, recv, thru, token = _exchange_start(l, [windows[n] for n in names], [lands[n] for n in names],
                                                  "grads_start_" + tag)
        lands.update(zip(names, thru))
        in_flight.append((l, names, [windows[n] for n in names], send, recv, "grads_wait_" + tag))
        return token

    def ffn_bwd(dx, dxb, F, s, f, l, pending, send_now):
        xin, h, g, u, a = s
        dg, du = _ffn_mid_bwd(dxb, F["wout"], g, u)
        dwout = _mm_tn(a, dxb, 0.5, "ffn_dwout")
        dwg = _mm_tn(h, dg, 1.0, "ffn_dwg")
        dwu = _mm_tn(h, du, 1.0, "ffn_dwu")
        pending[f + "_w_in"], pending[f + "_w_out"] = _ffn_grad_windows(dwg, dwu, dwout)
        if send_now:
            dg = _tie(dg, send_grads(l, pending, "l%d_%s" % (l, f)))
        dh = _mm2_nt(dg, F["wg"], du, F["wu"], "ffn_dh")
        dx, dxb, dn = _rms_bwd(xin, dh, F["norm"], dx)
        G[f + "_norm"][l] = dn[0]
        return dx, dxb

    for l in reversed(range(depth)):
        L = layers[l]
        s1, (x1, hm, p, ycat, hs), s2 = saved[l]
        pending = {}
        dx, dxb = ffn_bwd(dx, dxb, L["ffn2"], s2, "ffn2", l, pending, False)
        dycat = _mm_nt(dxb, L["w_out"], "mix_dy")
        dwo = _mm_tn(ycat, dxb, 1.0, "mix_dwout")
        mg = _mix_bwd(dycat, p, hs, L["mix"])
        dwin = _mm_tn(hm, mg["dp"], 1.0, "mix_dwin")
        pending["w_in"], pending["w_out"] = _mix_grad_windows(jnp.stack(jnp.split(dwin, N_CHIP, axis=1)), dwo)
        if l == 0:
            dp = _tie(mg["dp"], send_grads(l, pending, "l0_mix"))
            pending = {}
        else:
            dp = mg["dp"]
        dhm = _mm_nt(dp, L["w_in"], "mix_dh")
        dx, dxb, dn = _rms_bwd(x1, dhm, L["mix_norm"], dx)
        G["mix_norm"][l] = dn[0]
        G["conv_w"][l], G["conv_b"][l] = mg["conv_w"], mg["conv_b"][0]
        G["rg_w_a"][l] = _diag_blocks(mg["wax"][:, :D_RNN], 8, 64)
        G["rg_w_x"][l] = _diag_blocks(mg["wax"][:, D_RNN:], 8, 64)
        G["rg_b_a"][l] = mg["bax"][0, :D_RNN].reshape(8, 64)
        G["rg_b_x"][l] = mg["bax"][0, D_RNN:].reshape(8, 64)
        G["lru_lambda"][l] = mg["lam"][0]
        G["pool_w"][l] = _diag_blocks(mg["wpool"], 4, 64)
        G["pool_scale"][l], G["sgu_norm"][l] = mg["pool_scale"][0], mg["sgu_norm"][0]
        G["sgu_w"][l] = mg["ws"]
        G["sgu_b"][l] = mg["bz"].reshape(CHUNK, 4, 64).sum(-1).T
        dx, dxb = ffn_bwd(dx, dxb, L["ffn1"], s1, "ffn1", l, pending, l == 0)
        if l > 0:
            dxb = _tie(dxb, send_grads(l, pending, "l%d" % l))
    grad_x = dx[None]
    G = {n: jnp.stack(v) for n, v in G.items()}
    G["final_norm"] = d_final[0]

    for l, names, windows, send, recv, tag in in_flight:
        lands.update(zip(names, _exchange_wait(l, windows, [lands[n] for n in names], send, recv, dx, tag)))
    both = [_sum_share(lands[n], "sum_share_" + n) for n in BIG]
    grads = {n: (a[:, :, :FF_SHARD] if n.endswith("w_in") and n != "w_in" else a) for n, a in zip(BIG, both)}

    small_sizes = [int(np.prod(G[n].shape)) for n in SMALL]
    srows = _round_up(sum(small_sizes) + 1, N_DEV * 8 * LANES) // (N_DEV * LANES)
    sflat = _flat_rows([G[n] for n in SMALL] + [loss_part[0, :1]], N_DEV * srows)
    sgot = _all_to_all(sflat.reshape(N_DEV, srows, LANES), "exchange_small_grads")
    sall = _all_gather8(_sum8(sgot, "sum_small_grads"), "share_small_grads").reshape(-1)
    off = 0
    for n, size in zip(SMALL, small_sizes):
        grads[n] = sall[off:off + size].reshape(G[n].shape)
        off += size
    loss = sall[off]
    grads["conv_w"] = lax.dynamic_slice_in_dim(grads["conv_w"], chip * conv_w.shape[2], conv_w.shape[2], axis=2)

    delta, new_m, new_v = {}, {}, {}
    for n in BIG:
        shp = W[n].shape
        two_d = (shp[0] * shp[1], shp[2])
        outs = _adamw(W[n].reshape(two_d), grads[n].reshape(two_d), M[n].reshape(two_d), V[n].reshape(two_d),
                      "adamw_" + n)
        delta[n], new_m[n], new_v[n] = (o.reshape(shp) for o in outs)
    arows = _round_up(sum(int(np.prod(W[n].shape)) for n in SMALL), 8 * LANES) // LANES
    outs = _adamw(*(_flat_rows([src[n] for n in SMALL], arows) for src in (W, grads, M, V)), "adamw_small")
    outs = [o.reshape(-1) for o in outs]
    off = 0
    for n in SMALL:
        size = int(np.prod(W[n].shape))
        delta[n], new_m[n], new_v[n] = (o[off:off + size].reshape(W[n].shape) for o in outs)
        off += size

    return (loss, grad_x, *[grads[n] for n in WEIGHTS], *[delta[n] for n in WEIGHTS],
            *[new_m[n] for n in WEIGHTS], *[new_v[n] for n in WEIGHTS])
```

```python
import math

import jax
import jax.numpy as jnp
import numpy as np
from jax import lax
from jax.experimental import pallas as pl
from jax.experimental.pallas import tpu as pltpu

F32 = jnp.float32
BF16 = jnp.bfloat16
MESH = pl.DeviceIdType.MESH

D = 1024
D_RNN = 512
D_POOL = 256
D_SGU = 256
D_IN = 1792
D_FF = 2752
D_FFP = 2816
CHUNK = 128
HALO = 16
EPS = 1e-6
LRU_C = 8.0
N_DEV = 8
N_CHIP = 4
LANES = 1024
VMEM_LIMIT = 56 * 1024 * 1024

ADAM_LR, ADAM_B1, ADAM_B2, ADAM_EPS, ADAM_WD, ADAM_STEP = 0.001, 0.9, 0.999, 1e-08, 0.01, 10

BIG = ("ffn1_w_in", "ffn1_w_out", "w_in", "w_out", "ffn2_w_in", "ffn2_w_out")
SMALL = ("ffn1_norm", "mix_norm", "conv_w", "conv_b", "rg_w_a", "rg_b_a", "rg_w_x", "rg_b_x", "lru_lambda",
         "pool_w", "pool_scale", "sgu_norm", "sgu_w", "sgu_b", "ffn2_norm", "final_norm")
WEIGHTS = ("ffn1_norm", "ffn1_w_in", "ffn1_w_out", "mix_norm", "w_in", "conv_w", "conv_b", "rg_w_a", "rg_b_a",
           "rg_w_x", "rg_b_x", "lru_lambda", "pool_w", "pool_scale", "sgu_norm", "sgu_w", "sgu_b", "w_out",
           "ffn2_norm", "ffn2_w_in", "ffn2_w_out", "final_norm")


def _params(*sem):
    return pltpu.CompilerParams(dimension_semantics=sem, vmem_limit_bytes=VMEM_LIMIT)


def _gelu(x):
    c = math.sqrt(2.0 / math.pi)
    t = jnp.tanh(c * (x + 0.044715 * (x * x * x)))
    return 0.5 * x * (1.0 + t)


def _gelu_and_grad(x):
    c = math.sqrt(2.0 / math.pi)
    x2 = x * x
    t = jnp.tanh(c * (x + 0.044715 * (x2 * x)))
    g = 0.5 * x * (1.0 + t)
    dg = 0.5 * (1.0 + t) + 0.5 * x * (1.0 - t * t) * (c * (1.0 + 3.0 * 0.044715 * x2))
    return g, dg


def _sigmoid(x):
    return 0.5 * jnp.tanh(0.5 * x) + 0.5


def _dot(a, b):
    return jnp.dot(a, b, preferred_element_type=F32)


def _dot_tn(a, b):
    return lax.dot_general(a, b, (((0,), (0,)), ((), ())), preferred_element_type=F32)


def _dot_nt(a, b):
    return lax.dot_general(a, b, (((1,), (1,)), ((), ())), preferred_element_type=F32)


def _rms_fwd(x, g, tm=512):
    T = x.shape[0]
    tm = min(tm, T)

    def body(x_ref, g_ref, o_ref):
        xv = x_ref[...]
        r = lax.rsqrt(jnp.mean(xv * xv, axis=-1, keepdims=True) + EPS)
        o_ref[...] = (xv * r * g_ref[...]).astype(BF16)

    return pl.pallas_call(
        body, name="rms_fwd", grid=(T // tm,),
        in_specs=[pl.BlockSpec((tm, D), lambda i: (i, 0)), pl.BlockSpec((1, D), lambda i: (0, 0))],
        out_specs=pl.BlockSpec((tm, D), lambda i: (i, 0)),
        out_shape=jax.ShapeDtypeStruct((T, D), BF16),
        compiler_params=_params("parallel"),
    )(x, g)


def _rms_bwd(x, dh, g, dres, tm=512):
    T = x.shape[0]
    tm = min(tm, T)

    def body(x_ref, dh_ref, g_ref, dres_ref, dx_ref, dxb_ref, dg_ref):
        xv = x_ref[...]
        r = lax.rsqrt(jnp.mean(xv * xv, axis=-1, keepdims=True) + EPS)
        xhat = xv * r
        dy = dh_ref[...]
        dxhat = dy * g_ref[...]
        dx = dres_ref[...] + r * (dxhat - xhat * jnp.mean(dxhat * xhat, axis=-1, keepdims=True))
        dx_ref[...] = dx
        dxb_ref[...] = dx.astype(BF16)

        @pl.when(pl.program_id(0) == 0)
        def _():
            dg_ref[...] = jnp.zeros_like(dg_ref)

        dg_ref[...] += jnp.sum(dy * xhat, axis=0, keepdims=True)

    row = pl.BlockSpec((tm, D), lambda i: (i, 0))
    vec = pl.BlockSpec((1, D), lambda i: (0, 0))
    return pl.pallas_call(
        body, name="rms_bwd", grid=(T // tm,),
        in_specs=[row, row, vec, row], out_specs=[row, row, vec],
        out_shape=[jax.ShapeDtypeStruct((T, D), F32), jax.ShapeDtypeStruct((T, D), BF16),
                   jax.ShapeDtypeStruct((1, D), F32)],
        compiler_params=_params("arbitrary"),
    )(x, dh, g, dres)


def _tile(n, limit):
    best = 128
    for t in range(128, min(n, limit) + 1, 128):
        if n % t == 0:
            best = t
    assert n % best == 0, (n, limit)
    return best


def _mm(a, b, out_dtype, name, tm=1024, tn=512):
    M, K = a.shape
    N = b.shape[1]
    tm, tn = min(tm, M), _tile(N, tn)

    def body(a_ref, b_ref, o_ref):
        o_ref[...] = _dot(a_ref[...], b_ref[...]).astype(out_dtype)

    return pl.pallas_call(
        body, name=name, grid=(M // tm, N // tn),
        in_specs=[pl.BlockSpec((tm, K), lambda i, j: (i, 0)), pl.BlockSpec((K, tn), lambda i, j: (0, j))],
        out_specs=pl.BlockSpec((tm, tn), lambda i, j: (i, j)),
        out_shape=jax.ShapeDtypeStruct((M, N), out_dtype),
        compiler_params=_params("parallel", "parallel"),
    )(a, b)


def _mm_res(a, b, res, scale, name, tm=1024, tn=512):
    M, K = a.shape
    N = b.shape[1]
    tm, tn = min(tm, M), _tile(N, tn)

    def body(a_ref, b_ref, r_ref, o_ref):
        o_ref[...] = r_ref[...] + scale * _dot(a_ref[...], b_ref[...])

    return pl.pallas_call(
        body, name=name, grid=(M // tm, N // tn),
        in_specs=[pl.BlockSpec((tm, K), lambda i, j: (i, 0)), pl.BlockSpec((K, tn), lambda i, j: (0, j)),
                  pl.BlockSpec((tm, tn), lambda i, j: (i, j))],
        out_specs=pl.BlockSpec((tm, tn), lambda i, j: (i, j)),
        out_shape=jax.ShapeDtypeStruct((M, N), F32),
        compiler_params=_params("parallel", "parallel"),
    )(a, b, res)


def _mm_nt(a, b, name, tm=1024, tn=512):
    M, K = a.shape
    N = b.shape[0]
    tm, tn = min(tm, M), _tile(N, tn)

    def body(a_ref, b_ref, o_ref):
        o_ref[...] = _dot_nt(a_ref[...], b_ref[...])

    return pl.pallas_call(
        body, name=name, grid=(M // tm, N // tn),
        in_specs=[pl.BlockSpec((tm, K), lambda i, j: (i, 0)), pl.BlockSpec((tn, K), lambda i, j: (j, 0))],
        out_specs=pl.BlockSpec((tm, tn), lambda i, j: (i, j)),
        out_shape=jax.ShapeDtypeStruct((M, N), F32),
        compiler_params=_params("parallel", "parallel"),
    )(a, b)


def _mm2_nt(a1, b1, a2, b2, name, tm=1024, tn=512):
    M, K = a1.shape
    N = b1.shape[0]
    tm, tn = min(tm, M), _tile(N, tn)

    def body(a1_ref, b1_ref, a2_ref, b2_ref, o_ref):
        o_ref[...] = _dot_nt(a1_ref[...], b1_ref[...]) + _dot_nt(a2_ref[...], b2_ref[...])

    aspec = pl.BlockSpec((tm, K), lambda i, j: (i, 0))
    bspec = pl.BlockSpec((tn, K), lambda i, j: (j, 0))
    return pl.pallas_call(
        body, name=name, grid=(M // tm, N // tn),
        in_specs=[aspec, bspec, aspec, bspec],
        out_specs=pl.BlockSpec((tm, tn), lambda i, j: (i, j)),
        out_shape=jax.ShapeDtypeStruct((M, N), F32),
        compiler_params=_params("parallel", "parallel"),
    )(a1, b1, a2, b2)


def _mm_tn(a, b, scale, name, tm=1792, tn=1792, tk=512):
    T, M = a.shape
    N = b.shape[1]
    tm, tn, tk = _tile(M, tm), _tile(N, tn), min(tk, T)
    nk = T // tk

    def body(a_ref, b_ref, o_ref, acc_ref):
        k = pl.program_id(2)

        @pl.when(k == 0)
        def _():
            acc_ref[...] = jnp.zeros_like(acc_ref)

        acc_ref[...] += _dot_tn(a_ref[...], b_ref[...])

        @pl.when(k == nk - 1)
        def _():
            o_ref[...] = (scale * acc_ref[...]).astype(BF16)

    return pl.pallas_call(
        body, name=name, grid=(M // tm, N // tn, nk),
        in_specs=[pl.BlockSpec((tk, tm), lambda i, j, k: (k, i)), pl.BlockSpec((tk, tn), lambda i, j, k: (k, j))],
        out_specs=pl.BlockSpec((tm, tn), lambda i, j, k: (i, j)),
        out_shape=jax.ShapeDtypeStruct((M, N), BF16),
        scratch_shapes=[pltpu.VMEM((tm, tn), F32)],
        compiler_params=_params("parallel", "parallel", "arbitrary"),
    )(a, b)


def _ffn_in(h, wg, wu, tm=1024, tn=256):
    T = h.shape[0]
    tm = min(tm, T)

    def body(h_ref, wg_ref, wu_ref, g_ref, u_ref, a_ref):
        hv = h_ref[...]
        g = _dot(hv, wg_ref[...])
        u = _dot(hv, wu_ref[...])
        g_ref[...] = g.astype(BF16)
        u_ref[...] = u.astype(BF16)
        a_ref[...] = (g * _sigmoid(g) * u).astype(BF16)

    wspec = pl.BlockSpec((D, tn), lambda i, j: (0, j))
    ospec = pl.BlockSpec((tm, tn), lambda i, j: (i, j))
    oshape = jax.ShapeDtypeStruct((T, D_FFP), BF16)
    return pl.pallas_call(
        body, name="ffn_in", grid=(T // tm, D_FFP // tn),
        in_specs=[pl.BlockSpec((tm, D), lambda i, j: (i, 0)), wspec, wspec],
        out_specs=[ospec, ospec, ospec], out_shape=[oshape, oshape, oshape],
        compiler_params=_params("parallel", "parallel"),
    )(h, wg, wu)


def _ffn_mid_bwd(dy, wout, g, u, tm=1024, tn=256):
    T = dy.shape[0]
    tm = min(tm, T)

    def body(dy_ref, w_ref, g_ref, u_ref, dg_ref, du_ref):
        da = 0.5 * _dot_nt(dy_ref[...], w_ref[...])
        g = g_ref[...].astype(F32)
        u = u_ref[...].astype(F32)
        s = _sigmoid(g)
        dg_ref[...] = (da * u * (s * (1.0 + g * (1.0 - s)))).astype(BF16)
        du_ref[...] = (da * (g * s)).astype(BF16)

    ospec = pl.BlockSpec((tm, tn), lambda i, j: (i, j))
    oshape = jax.ShapeDtypeStruct((T, D_FFP), BF16)
    return pl.pallas_call(
        body, name="ffn_mid_bwd", grid=(T // tm, D_FFP // tn),
        in_specs=[pl.BlockSpec((tm, D), lambda i, j: (i, 0)), pl.BlockSpec((tn, D), lambda i, j: (j, 0)),
                  ospec, ospec],
        out_specs=[ospec, ospec], out_shape=[oshape, oshape],
        compiler_params=_params("parallel", "parallel"),
    )(dy, wout, g, u)


def _final(x, tgt, gf, tm=512):
    T = x.shape[0]
    tm = min(tm, T)

    def body(x_ref, t_ref, g_ref, dx_ref, dxb_ref, dg_ref, loss_ref):
        xv = x_ref[...]
        r = lax.rsqrt(jnp.mean(xv * xv, axis=-1, keepdims=True) + EPS)
        xhat = xv * r
        err = xhat * g_ref[...] - t_ref[...]
        dy = err * (1.0 / D)
        dxhat = dy * g_ref[...]
        dx = r * (dxhat - xhat * jnp.mean(dxhat * xhat, axis=-1, keepdims=True))
        dx_ref[...] = dx
        dxb_ref[...] = dx.astype(BF16)

        @pl.when(pl.program_id(0) == 0)
        def _():
            dg_ref[...] = jnp.zeros_like(dg_ref)
            loss_ref[...] = jnp.zeros_like(loss_ref)

        dg_ref[...] += jnp.sum(dy * xhat, axis=0, keepdims=True)
        loss_ref[...] += (0.5 / D) * jnp.sum(err * err)

    row = pl.BlockSpec((tm, D), lambda i: (i, 0))
    vec = pl.BlockSpec((1, D), lambda i: (0, 0))
    return pl.pallas_call(
        body, name="final_loss", grid=(T // tm,),
        in_specs=[row, row, vec],
        out_specs=[row, row, vec, pl.BlockSpec((1, 128), lambda i: (0, 0))],
        out_shape=[jax.ShapeDtypeStruct((T, D), F32), jax.ShapeDtypeStruct((T, D), BF16),
                   jax.ShapeDtypeStruct((1, D), F32), jax.ShapeDtypeStruct((1, 128), F32)],
        compiler_params=_params("arbitrary"),
    )(x, tgt, gf)


def _mix_block(T, limit):
    return min(limit, T // 2)


def _rows(tb, width):
    return lax.broadcasted_iota(jnp.int32, (tb, width), 0)


def _rglru_gates(xc, wax_ref, bax_ref, lam_ref):
    pre = _dot(xc.astype(BF16), wax_ref[...]) + bax_ref[...]
    r = _sigmoid(pre[:, :D_RNN])
    ig = _sigmoid(pre[:, D_RNN:])
    z = -lam_ref[...]
    sp = jnp.maximum(z, 0.0) + jnp.log(1.0 + jnp.exp(-jnp.abs(z)))
    log_a = (-LRU_C) * r * sp
    a = jnp.exp(log_a)
    mult = jnp.sqrt(-jnp.tanh(log_a) * (1.0 + a * a))
    return r, ig, sp, a, mult


def _conv(xa_ext, cw_ref, cb_ref):
    y = cb_ref[...] + cw_ref[3:4, :] * xa_ext
    for k in range(1, 4):
        y = y + cw_ref[3 - k:4 - k, :] * pltpu.roll(xa_ext, k, 0)
    return y[HALO:]


def _pool_window_lanes():
    lane = lax.broadcasted_iota(jnp.int32, (1, D_POOL), 1)
    return jnp.where(lane < 64, 2, jnp.where(lane < 128, 4, jnp.where(lane < 192, 8, 16)))


def _pool_select(s2, s4, s8, s16):
    lane = lax.broadcasted_iota(jnp.int32, s2.shape, 1)
    return jnp.where(lane < 64, s2, jnp.where(lane < 128, s4, jnp.where(lane < 192, s8, s16)))


def _pool_diff(xp_ext, t0, tb):
    s2 = xp_ext + pltpu.roll(xp_ext, 1, 0)
    s4 = s2 + pltpu.roll(s2, 2, 0)
    s8 = s4 + pltpu.roll(s4, 4, 0)
    s16 = s8 + pltpu.roll(s8, 8, 0)
    sel = _pool_select(s2, s4, s8, s16)[HALO:]
    cnt = jnp.minimum(t0 + _rows(tb, D_POOL) + 1, _pool_window_lanes()).astype(F32)
    return sel / cnt - xp_ext[HALO:], cnt


def _head_masks():
    lane = lax.broadcasted_iota(jnp.int32, (1, D_SGU), 1)
    return [((lane >= 64 * h) & (lane < 64 * (h + 1))).astype(F32) for h in range(4)]


def _sgu_mix(w_ref, vch, masks):
    z = masks[0] * _dot(w_ref[0], vch)
    for h in range(1, 4):
        z = z + masks[h] * _dot(w_ref[h], vch)
    return z


def _mix_fwd(p, prm):
    T = p.shape[0]
    tb = _mix_block(T, 512)
    nb = T // tb

    def body(p_ref, xah_ref, xph_ref, cw_ref, cb_ref, wax_ref, bax_ref, lam_ref, wp_ref, ps_ref, sgn_ref,
             ws_ref, bz_ref, y_ref, hs_ref, carry_ref):
        i = pl.program_id(0)
        keep = (i > 0).astype(F32)

        @pl.when(i == 0)
        def _():
            carry_ref[...] = jnp.zeros_like(carry_ref)

        xa_ext = jnp.concatenate([xah_ref[...] * keep, p_ref[:, 512:1024]], axis=0)
        xc = _conv(xa_ext, cw_ref, cb_ref)
        r, ig, sp, a, mult = _rglru_gates(xc, wax_ref, bax_ref, lam_ref)
        bv = mult * (ig * xc)
        row = _rows(tb, D_RNN)
        s = 1
        while s < tb:
            m = row >= s
            bv = jnp.where(m, a * pltpu.roll(bv, s, 0) + bv, bv)
            a = jnp.where(m, a * pltpu.roll(a, s, 0), a)
            s *= 2
        h = bv + a * carry_ref[0:1, :]
        hs_ref[...] = h
        last = jnp.sum(jnp.where(_rows(8, D_RNN) == 7, hs_ref[tb - 8:tb, :], 0.0), axis=0, keepdims=True)
        carry_ref[...] = jnp.broadcast_to(last, carry_ref.shape)
        y_ref[:, 0:512] = (_gelu(p_ref[:, 0:512]) * h).astype(BF16)

        xp_ext = jnp.concatenate([xph_ref[...] * keep, p_ref[:, 1024:1280]], axis=0)
        d, _ = _pool_diff(xp_ext, i * tb, tb)
        y_ref[:, 512:768] = (_dot(d.astype(BF16), wp_ref[...]) * ps_ref[...]).astype(BF16)

        ug = _gelu(p_ref[:, 1280:1536])
        vg = _gelu(p_ref[:, 1536:1792])
        rv = lax.rsqrt(jnp.mean(vg * vg, axis=-1, keepdims=True) + EPS)
        vn = (vg * rv * sgn_ref[...]).astype(BF16)
        masks = _head_masks()
        for ci in range(tb // CHUNK):
            sl = slice(ci * CHUNK, (ci + 1) * CHUNK)
            z = _sgu_mix(ws_ref, vn[sl], masks) + bz_ref[...]
            y_ref[sl, 768:1024] = (ug[sl] * z).astype(BF16)

    hb = tb // HALO

    def halo(i):
        return jnp.maximum(i * hb - 1, 0)

    def full(shape):
        return pl.BlockSpec(shape, lambda i: (0,) * len(shape))

    return pl.pallas_call(
        body, name="mix_fwd", grid=(nb,),
        in_specs=[pl.BlockSpec((tb, D_IN), lambda i: (i, 0)),
                  pl.BlockSpec((HALO, D_RNN), lambda i: (halo(i), 1)),
                  pl.BlockSpec((HALO, D_POOL), lambda i: (halo(i), 4)),
                  full((4, D_RNN)), full((1, D_RNN)), full((D_RNN, 2 * D_RNN)), full((1, 2 * D_RNN)),
                  full((1, D_RNN)), full((D_POOL, D_POOL)), full((1, D_POOL)), full((1, D_SGU)),
                  full((4, CHUNK, CHUNK)), full((CHUNK, D_SGU))],
        out_specs=[pl.BlockSpec((tb, D), lambda i: (i, 0)), pl.BlockSpec((tb, D_RNN), lambda i: (i, 0))],
        out_shape=[jax.ShapeDtypeStruct((T, D), BF16), jax.ShapeDtypeStruct((T, D_RNN), F32)],
        scratch_shapes=[pltpu.VMEM((8, D_RNN), F32)],
        compiler_params=_params("arbitrary"),
    )(p, p, p, prm["conv_w"], prm["conv_b"], prm["wax"], prm["bax"], prm["lam"], prm["wpool"], prm["pool_scale"],
      prm["sgu_norm"], prm["ws"], prm["bz"])


def _mix_bwd(dy, p, hs, prm):
    T = p.shape[0]
    tb = _mix_block(T, 256)
    nb = T // tb
    hb = tb // HALO

    def body(dy_ref, p_ref, xah_ref, xph_ref, hs_ref, hsh_ref, cw_ref, cb_ref, wax_ref, waxt_ref, bax_ref,
             lam_ref, wp_ref, wpt_ref, ps_ref, sgn_ref, ws_ref, wst_ref, bz_ref,
             dp_ref, dcw_ref, dcb_ref, dwax_ref, dbax_ref, dlam_ref, dwp_ref, dps_ref, dsgn_ref, dws_ref,
             dbz_ref, gcarry_ref, xcfut_ref, mfut_ref):
        i = pl.program_id(0)
        bi = nb - 1 - i
        keep = (bi > 0).astype(F32)

        @pl.when(i == 0)
        def _():
            for ref in (dcw_ref, dcb_ref, dwax_ref, dbax_ref, dlam_ref, dwp_ref, dps_ref, dsgn_ref, dws_ref,
                        dbz_ref, gcarry_ref, xcfut_ref, mfut_ref):
                ref[...] = jnp.zeros_like(ref)

        xa_ext = jnp.concatenate([xah_ref[...] * keep, p_ref[:, 512:1024]], axis=0)
        xc = _conv(xa_ext, cw_ref, cb_ref)
        r, ig, sp, a, mult = _rglru_gates(xc, wax_ref, bax_ref, lam_ref)
        gg, dgg = _gelu_and_grad(p_ref[:, 0:512])
        dya = dy_ref[:, 0:512]
        dp_ref[:, 0:512] = (dya * hs_ref[...] * dgg).astype(BF16)
        row = _rows(tb, D_RNN)
        g = dya * gg + jnp.where(row == tb - 1, gcarry_ref[0:1, :], 0.0)
        al = pltpu.roll(a, tb - 1, 0)
        s = 1
        while s < tb:
            m = row < tb - s
            g = jnp.where(m, al * pltpu.roll(g, tb - s, 0) + g, g)
            al = jnp.where(m, al * pltpu.roll(al, tb - s, 0), al)
            s *= 2
        first = jnp.sum(jnp.where(_rows(8, D_RNN) == 0, (a * g)[0:8], 0.0), axis=0, keepdims=True)
        gcarry_ref[...] = jnp.broadcast_to(first, gcarry_ref.shape)
        hs_ext = jnp.concatenate([hsh_ref[...] * keep, hs_ref[...]], axis=0)
        h_prev = pltpu.roll(hs_ext, 1, 0)[HALO:]
        ix = ig * xc
        dlog_a = g * h_prev * a - (g * ix) * (a * a / mult)
        dlam_ref[...] += jnp.sum(dlog_a * r, axis=0, keepdims=True) * (LRU_C * _sigmoid(-lam_ref[...]))
        dpre_r = dlog_a * ((-LRU_C) * sp) * (r * (1.0 - r))
        dpre_i = (g * mult * xc) * (ig * (1.0 - ig))
        dpre = jnp.concatenate([dpre_r, dpre_i], axis=1)
        dbax_ref[...] += jnp.sum(dpre, axis=0, keepdims=True)
        dpre_b = dpre.astype(BF16)
        dwax_ref[...] += _dot_tn(xc.astype(BF16), dpre_b)
        dxc = g * mult * ig + _dot(dpre_b, waxt_ref[...])
        dcb_ref[...] += jnp.sum(dxc, axis=0, keepdims=True)
        for k in range(4):
            xs = xa_ext[HALO:] if k == 3 else pltpu.roll(xa_ext, 3 - k, 0)[HALO:]
            dcw_ref[k:k + 1, :] += jnp.sum(dxc * xs, axis=0, keepdims=True)
        dxc_ext = jnp.concatenate([dxc, xcfut_ref[...]], axis=0)
        n = tb + HALO
        dxa = cw_ref[3:4, :] * dxc_ext
        for k in range(1, 4):
            dxa = dxa + cw_ref[3 - k:4 - k, :] * pltpu.roll(dxc_ext, n - k, 0)
        dp_ref[:, 512:1024] = dxa[:tb].astype(BF16)
        xcfut_ref[...] = dxc[0:HALO]

        xp_ext = jnp.concatenate([xph_ref[...] * keep, p_ref[:, 1024:1280]], axis=0)
        d, cnt = _pool_diff(xp_ext, bi * tb, tb)
        db = d.astype(BF16)
        dyb = dy_ref[:, 512:768]
        dps_ref[...] += jnp.sum(dyb * _dot(db, wp_ref[...]), axis=0, keepdims=True)
        dq = (dyb * ps_ref[...]).astype(BF16)
        dwp_ref[...] += _dot_tn(db, dq)
        dd = _dot(dq, wpt_ref[...])
        mm = dd / cnt
        m_ext = jnp.concatenate([mm, mfut_ref[...]], axis=0)
        f2 = m_ext + pltpu.roll(m_ext, n - 1, 0)
        f4 = f2 + pltpu.roll(f2, n - 2, 0)
        f8 = f4 + pltpu.roll(f4, n - 4, 0)
        f16 = f8 + pltpu.roll(f8, n - 8, 0)
        dp_ref[:, 1024:1280] = (_pool_select(f2, f4, f8, f16)[:tb] - dd).astype(BF16)
        mfut_ref[...] = mm[0:HALO]

        ug, dug = _gelu_and_grad(p_ref[:, 1280:1536])
        vg, dvg = _gelu_and_grad(p_ref[:, 1536:1792])
        rv = lax.rsqrt(jnp.mean(vg * vg, axis=-1, keepdims=True) + EPS)
        vhat = vg * rv
        vn = (vhat * sgn_ref[...]).astype(BF16)
        dyc = dy_ref[:, 768:1024]
        masks = _head_masks()
        dz = dyc * ug
        dzb = dz.astype(BF16)
        dvn_parts = []
        for ci in range(tb // CHUNK):
            sl = slice(ci * CHUNK, (ci + 1) * CHUNK)
            z = _sgu_mix(ws_ref, vn[sl], masks) + bz_ref[...]
            dp_ref[sl, 1280:1536] = (dyc[sl] * z * dug[sl]).astype(BF16)
            dbz_ref[...] += dz[sl]
            for h in range(4):
                dws_ref[h] += _dot_nt((dz[sl] * masks[h]).astype(BF16), vn[sl])
            dvn_parts.append(_sgu_mix(wst_ref, dzb[sl], masks))
        dvn = jnp.concatenate(dvn_parts, axis=0)
        dsgn_ref[...] += jnp.sum(dvn * vhat, axis=0, keepdims=True)
        dvhat = dvn * sgn_ref[...]
        dvg_in = rv * (dvhat - vhat * jnp.mean(dvhat * vhat, axis=-1, keepdims=True))
        dp_ref[:, 1536:1792] = (dvg_in * dvg).astype(BF16)

        @pl.when(i == nb - 1)
        def _():
            tril = (lax.broadcasted_iota(jnp.int32, (CHUNK, CHUNK), 0)
                    >= lax.broadcasted_iota(jnp.int32, (CHUNK, CHUNK), 1)).astype(F32)
            for h in range(4):
                dws_ref[h] = dws_ref[h] * tril

    def blk(i):
        return nb - 1 - i

    def halo(i):
        return jnp.maximum(blk(i) * hb - 1, 0)

    def full(shape):
        return pl.BlockSpec(shape, lambda i: (0,) * len(shape))

    small_shapes = [(4, D_RNN), (1, D_RNN), (D_RNN, 2 * D_RNN), (1, 2 * D_RNN), (1, D_RNN), (D_POOL, D_POOL),
                    (1, D_POOL), (1, D_SGU), (4, CHUNK, CHUNK), (CHUNK, D_SGU)]
    outs = pl.pallas_call(
        body, name="mix_bwd", grid=(nb,),
        in_specs=[pl.BlockSpec((tb, D), lambda i: (blk(i), 0)),
                  pl.BlockSpec((tb, D_IN), lambda i: (blk(i), 0)),
                  pl.BlockSpec((HALO, D_RNN), lambda i: (halo(i), 1)),
                  pl.BlockSpec((HALO, D_POOL), lambda i: (halo(i), 4)),
                  pl.BlockSpec((tb, D_RNN), lambda i: (blk(i), 0)),
                  pl.BlockSpec((HALO, D_RNN), lambda i: (halo(i), 0)),
                  full((4, D_RNN)), full((1, D_RNN)), full((D_RNN, 2 * D_RNN)), full((2 * D_RNN, D_RNN)),
                  full((1, 2 * D_RNN)), full((1, D_RNN)), full((D_POOL, D_POOL)), full((D_POOL, D_POOL)),
                  full((1, D_POOL)), full((1, D_SGU)), full((4, CHUNK, CHUNK)), full((4, CHUNK, CHUNK)),
                  full((CHUNK, D_SGU))],
        out_specs=[pl.BlockSpec((tb, D_IN), lambda i: (blk(i), 0))] + [full(s) for s in small_shapes],
        out_shape=[jax.ShapeDtypeStruct((T, D_IN), BF16)] + [jax.ShapeDtypeStruct(s, F32) for s in small_shapes],
        scratch_shapes=[pltpu.VMEM((8, D_RNN), F32), pltpu.VMEM((HALO, D_RNN), F32),
                        pltpu.VMEM((HALO, D_POOL), F32)],
        compiler_params=_params("arbitrary"),
    )(dy, p, p, p, hs, hs, prm["conv_w"], prm["conv_b"], prm["wax"], prm["wax_t"], prm["bax"], prm["lam"],
      prm["wpool"], prm["wpool_t"], prm["pool_scale"], prm["sgu_norm"], prm["ws"], prm["ws_t"], prm["bz"])
    names = ("dp", "conv_w", "conv_b", "wax", "bax", "lam", "wpool", "pool_scale", "sgu_norm", "ws", "bz")
    return dict(zip(names, outs))


ANY = pl.BlockSpec(memory_space=pl.ANY)


def _place():
    x, y, c = lax.axis_index("x"), lax.axis_index("y"), lax.axis_index("c")
    return x, y, c


def _all_to_all(xs, name):
    def body(in_ref, out_ref, send_sems, recv_sems, local_sem):
        x, y, c = _place()
        me = 4 * x + 2 * y + c
        mine = pltpu.make_async_copy(in_ref.at[me], out_ref.at[me], local_sem)
        mine.start()
        copies = []
        for rel in range(1, N_DEV):
            tx = 1 - x if rel & 4 else x
            ty = 1 - y if rel & 2 else y
            tc = 1 - c if rel & 1 else c
            cp = pltpu.make_async_remote_copy(
                src_ref=in_ref.at[4 * tx + 2 * ty + tc], dst_ref=out_ref.at[me],
                send_sem=send_sems.at[rel - 1], recv_sem=recv_sems.at[rel - 1],
                device_id=(tx, ty, tc), device_id_type=MESH)
            cp.start()
            copies.append(cp)
        for cp in copies:
            cp.wait()
        mine.wait()

    return pl.pallas_call(
        body, name=name, in_specs=[ANY], out_specs=ANY,
        out_shape=jax.ShapeDtypeStruct(xs.shape, xs.dtype),
        scratch_shapes=[pltpu.SemaphoreType.DMA((N_DEV - 1,)), pltpu.SemaphoreType.DMA((N_DEV - 1,)),
                        pltpu.SemaphoreType.DMA],
    )(xs)


def _all_gather8(xs, name):
    def body(x_ref, out_ref, send_sems, recv_sems, local_sem):
        x, y, c = _place()
        me, sibling = (x, y, c), (x, y, 1 - c)
        chips = [(1 - x, y), (x, 1 - y), (1 - x, 1 - y)]

        def rows(px, py, pc):
            return out_ref.at[4 * px + 2 * py + pc]

        def copy(k, block, to, src=None):
            return pltpu.make_async_remote_copy(
                src_ref=rows(*block) if src is None else src, dst_ref=rows(*block),
                send_sem=send_sems.at[k], recv_sem=recv_sems.at[k], device_id=to, device_id_type=MESH)

        mine = pltpu.make_async_copy(x_ref, rows(*me), local_sem)
        mine.start()
        first = [copy(0, me, sibling, src=x_ref)]
        first += [copy(1 + j, me, (*chip, c), src=x_ref) for j, chip in enumerate(chips)]
        for cp in first:
            cp.start()
        passed = [copy(4 + j, (*chip, c), sibling) for j, chip in enumerate(chips)]
        for j, chip in enumerate(chips):
            copy(1 + j, (*chip, c), me).wait_recv()
            passed[j].start()
        copy(0, sibling, me).wait_recv()
        for j, chip in enumerate(chips):
            copy(4 + j, (*chip, 1 - c), me).wait_recv()
        for cp in first + passed:
            cp.wait_send()
        mine.wait()

    return pl.pallas_call(
        body, name=name, in_specs=[ANY], out_specs=ANY,
        out_shape=jax.ShapeDtypeStruct((N_DEV,) + xs.shape, xs.dtype),
        scratch_shapes=[pltpu.SemaphoreType.DMA((7,)), pltpu.SemaphoreType.DMA((7,)), pltpu.SemaphoreType.DMA],
    )(xs)


HBM = pl.BlockSpec(memory_space=pltpu.HBM)
SEM = pl.BlockSpec(memory_space=pltpu.SEMAPHORE)
EFFECT = pltpu.SideEffectType.DATAFLOW_SIDE_EFFECTING


def _in_hbm(a):
    return pltpu.with_memory_space_constraint(a, pltpu.HBM)


def _unique(windows):
    arrays = []
    for per_chip in windows:
        for arr, _ in per_chip:
            if not any(arr is a for a in arrays):
                arrays.append(arr)
    return arrays


def _exchange_start(layer, windows, lands, name):
    arrays = _unique(windows)
    na, nt = len(arrays), len(windows)

    def body(*refs):
        in_refs, land_refs = refs[:na], refs[na:na + nt]
        send_sems, recv_sems = refs[na + nt], refs[na + nt + 1]
        token = refs[-1]
        x, y, c = _place()
        me = 4 * x + 2 * y + c
        for t in range(nt):
            for j in range(N_CHIP):
                arr, window = windows[t][j]
                src = window(in_refs[next(i for i, a in enumerate(arrays) if a is arr)])

                @pl.when(me != 2 * j + layer)
                def _():
                    pltpu.make_async_remote_copy(
                        src_ref=src, dst_ref=land_refs[t].at[me], send_sem=send_sems.at[N_CHIP * t + j],
                        recv_sem=recv_sems.at[N_DEV * t + me], device_id=(j // 2, j % 2, layer),
                        device_id_type=MESH).start()
        token[...] = jnp.zeros_like(token)

    outs = pl.pallas_call(
        body, name=name,
        out_shape=(pltpu.SemaphoreType.DMA((N_CHIP * nt,)), pltpu.SemaphoreType.DMA((N_DEV * nt,)),
                   *[pltpu.HBM(a.shape, a.dtype) for a in lands], jax.ShapeDtypeStruct((8, 128), F32)),
        in_specs=[HBM] * (na + nt),
        out_specs=(SEM, SEM, *[HBM] * nt, pl.BlockSpec(memory_space=pltpu.VMEM)),
        input_output_aliases={na + t: 2 + t for t in range(nt)},
        compiler_params=pltpu.CompilerParams(has_side_effects=EFFECT),
    )(*[_in_hbm(a) for a in arrays], *[_in_hbm(a) for a in lands])
    return outs[0], outs[1], list(outs[2:2 + nt]), outs[-1]


def _exchange_wait(layer, windows, lands, send_sems, recv_sems, after, name):
    arrays = _unique(windows)
    na, nt = len(arrays), len(windows)

    def body(*refs):
        in_refs, land_refs = refs[:na], refs[na:na + nt]
        send_sems, recv_sems = refs[na + nt], refs[na + nt + 1]
        local_sem = refs[-1]
        x, y, c = _place()
        me = 4 * x + 2 * y + c

        def source(t, j):
            arr, window = windows[t][j]
            return window(in_refs[next(i for i, a in enumerate(arrays) if a is arr)])

        for t in range(nt):
            for j in range(N_CHIP):
                @pl.when(me != 2 * j + layer)
                def _():
                    pltpu.make_async_remote_copy(
                        src_ref=source(t, j), dst_ref=land_refs[t].at[me], send_sem=send_sems.at[N_CHIP * t + j],
                        recv_sem=recv_sems.at[N_DEV * t + me], device_id=(j // 2, j % 2, layer),
                        device_id_type=MESH).wait_send()

        @pl.when(c == layer)
        def _():
            for t in range(nt):
                for s in range(N_DEV):
                    @pl.when(me != s)
                    def _():
                        slot = land_refs[t].at[s]
                        pltpu.make_async_remote_copy(
                            src_ref=slot, dst_ref=slot, send_sem=send_sems.at[N_CHIP * t],
                            recv_sem=recv_sems.at[N_DEV * t + s], device_id=(x, y, c),
                            device_id_type=MESH).wait_recv()
                for j in range(N_CHIP):
                    @pl.when(me == 2 * j + layer)
                    def _():
                        mine = pltpu.make_async_copy(source(t, j), land_refs[t].at[me], local_sem)
                        mine.start()
                        mine.wait()

    outs = pl.pallas_call(
        body, name=name,
        out_shape=tuple(pltpu.HBM(a.shape, a.dtype) for a in lands),
        in_specs=[HBM] * (na + nt) + [SEM, SEM, ANY],
        out_specs=tuple([HBM] * nt),
        input_output_aliases={na + t: t for t in range(nt)},
        scratch_shapes=[pltpu.SemaphoreType.DMA],
        compiler_params=pltpu.CompilerParams(has_side_effects=EFFECT),
    )(*[_in_hbm(a) for a in arrays], *lands, send_sems, recv_sems, after)
    return list(outs)


def _other_chips(x, y):
    return [(1 - x, y), (x, 1 - y), (1 - x, 1 - y)]


def _gather_start(layer, shards, lands, name):
    nt = len(shards)

    def body(*refs):
        in_refs, land_refs = refs[:nt], refs[nt:2 * nt]
        send_sems, recv_sems = refs[2 * nt], refs[2 * nt + 1]
        token = refs[-1]
        x, y, c = _place()

        @pl.when(c == layer)
        def _():
            for t in range(nt):
                for rel, (tx, ty) in enumerate(_other_chips(x, y)):
                    for tc in range(2):
                        pltpu.make_async_remote_copy(
                            src_ref=in_refs[t], dst_ref=land_refs[t].at[2 * x + y],
                            send_sem=send_sems.at[6 * t + 2 * rel + tc], recv_sem=recv_sems.at[3 * t + rel],
                            device_id=(tx, ty, tc), device_id_type=MESH).start()

        token[...] = jnp.zeros_like(token)

    outs = pl.pallas_call(
        body, name=name,
        out_shape=(pltpu.SemaphoreType.DMA((6 * nt,)), pltpu.SemaphoreType.DMA((3 * nt,)),
                   *[pltpu.HBM(a.shape, a.dtype) for a in lands], jax.ShapeDtypeStruct((8, 128), F32)),
        in_specs=[HBM] * (2 * nt),
        out_specs=(SEM, SEM, *[HBM] * nt, pl.BlockSpec(memory_space=pltpu.VMEM)),
        input_output_aliases={nt + t: 2 + t for t in range(nt)},
        compiler_params=pltpu.CompilerParams(has_side_effects=EFFECT),
    )(*[_in_hbm(a) for a in shards], *[_in_hbm(a) for a in lands])
    return outs[0], outs[1], list(outs[2:2 + nt]), outs[-1]


def _gather_wait(layer, shards, lands, send_sems, recv_sems, after, name):
    nt = len(shards)

    def body(*refs):
        in_refs, land_refs = refs[:nt], refs[nt:2 * nt]
        send_sems, recv_sems = refs[2 * nt], refs[2 * nt + 1]
        local_sem = refs[-1]
        x, y, c = _place()

        @pl.when(c == layer)
        def _():
            for t in range(nt):
                for rel, (tx, ty) in enumerate(_other_chips(x, y)):
                    for tc in range(2):
                        pltpu.make_async_remote_copy(
                            src_ref=in_refs[t], dst_ref=land_refs[t].at[2 * x + y],
                            send_sem=send_sems.at[6 * t + 2 * rel + tc], recv_sem=recv_sems.at[3 * t + rel],
                            device_id=(tx, ty, tc), device_id_type=MESH).wait_send()

        for t in range(nt):
            for rel, (tx, ty) in enumerate(_other_chips(x, y)):
                slot = land_refs[t].at[2 * tx + ty]
                pltpu.make_async_remote_copy(
                    src_ref=slot, dst_ref=slot, send_sem=send_sems.at[6 * t], recv_sem=recv_sems.at[3 * t + rel],
                    device_id=(x, y, c), device_id_type=MESH).wait_recv()
            mine = pltpu.make_async_copy(in_refs[t], land_refs[t].at[2 * x + y], local_sem)
            mine.start()
            mine.wait()

    outs = pl.pallas_call(
        body, name=name,
        out_shape=tuple(pltpu.HBM(a.shape, a.dtype) for a in lands),
        in_specs=[HBM] * (2 * nt) + [SEM, SEM, ANY],
        out_specs=tuple([HBM] * nt),
        input_output_aliases={nt + t: t for t in range(nt)},
        scratch_shapes=[pltpu.SemaphoreType.DMA],
        compiler_params=pltpu.CompilerParams(has_side_effects=EFFECT),
    )(*[_in_hbm(a) for a in shards], *lands, send_sems, recv_sems, after)
    return list(outs)


def _tie(a, token):
    def body(a_ref, token_ref, o_ref):
        pass

    return pl.pallas_call(
        body, name="tie", in_specs=[ANY, ANY], out_specs=ANY,
        out_shape=jax.ShapeDtypeStruct(a.shape, a.dtype), input_output_aliases={0: 0},
    )(a, token)


def _sum_share(xs, name):
    _, r, cols = xs.shape
    tr = 256 if r % 256 == 0 else r
    nblk = r // tr

    def body(x_ref, out_ref, acc_ref, send_sems, local_sems, recv_sem):
        i = pl.program_id(0)
        slot = i % 2
        x, y, c = _place()

        def copies(s, blk):
            dst = out_ref.at[c, pl.ds(blk * tr, tr), :]
            loc = pltpu.make_async_copy(acc_ref.at[s], dst, local_sems.at[s])
            rem = pltpu.make_async_remote_copy(src_ref=acc_ref.at[s], dst_ref=dst, send_sem=send_sems.at[s],
                                               recv_sem=recv_sem, device_id=(x, y, 1 - c), device_id_type=MESH)
            return loc, rem

        @pl.when(i >= 2)
        def _():
            loc, rem = copies(slot, i - 2)
            loc.wait()
            rem.wait_send()

        acc = x_ref[0].astype(F32)
        for k in range(1, N_DEV):
            acc = acc + x_ref[k].astype(F32)
        acc_ref[slot] = acc
        loc, rem = copies(slot, i)
        loc.start()
        rem.start()

        @pl.when(i == nblk - 1)
        def _():
            for back in range(min(2, nblk)):
                blk = nblk - 1 - back
                loc, rem = copies(blk % 2, blk)
                loc.wait()
                rem.wait_send()
            theirs = out_ref.at[1 - c]
            pltpu.make_async_remote_copy(src_ref=theirs, dst_ref=theirs, send_sem=send_sems.at[0],
                                         recv_sem=recv_sem, device_id=(x, y, 1 - c),
                                         device_id_type=MESH).wait_recv()

    return pl.pallas_call(
        body, name=name, grid=(nblk,),
        in_specs=[pl.BlockSpec((N_DEV, tr, cols), lambda i: (0, i, 0))],
        out_specs=ANY,
        out_shape=jax.ShapeDtypeStruct((2, r, cols), F32),
        scratch_shapes=[pltpu.VMEM((2, tr, cols), F32), pltpu.SemaphoreType.DMA((2,)),
                        pltpu.SemaphoreType.DMA((2,)), pltpu.SemaphoreType.DMA],
        compiler_params=_params("arbitrary"),
    )(xs)


def _sum8(xs, name):
    _, r, cols = xs.shape
    tr = 8
    for cand in (256, 128, 64, 32, 16):
        if r % cand == 0:
            tr = cand
            break

    def body(x_ref, o_ref):
        acc = x_ref[0].astype(F32)
        for k in range(1, N_DEV):
            acc = acc + x_ref[k].astype(F32)
        o_ref[...] = acc

    return pl.pallas_call(
        body, name=name, grid=(r // tr,),
        in_specs=[pl.BlockSpec((N_DEV, tr, cols), lambda i: (0, i, 0))],
        out_specs=pl.BlockSpec((tr, cols), lambda i: (i, 0)),
        out_shape=jax.ShapeDtypeStruct((r, cols), F32),
        compiler_params=_params("parallel"),
    )(xs)


def _adamw(w, g, m, v, name):
    R, C = w.shape
    tr = R
    for cand in (256, 128, 64, 32, 16, 8):
        if R % cand == 0:
            tr = cand
            break
    c1 = 1.0 / (1.0 - ADAM_B1 ** ADAM_STEP)
    c2 = 1.0 / (1.0 - ADAM_B2 ** ADAM_STEP)

    def body(w_ref, g_ref, m_ref, v_ref, d_ref, nm_ref, nv_ref):
        gv = g_ref[...]
        nm = ADAM_B1 * m_ref[...] + (1.0 - ADAM_B1) * gv
        nv = ADAM_B2 * v_ref[...] + (1.0 - ADAM_B2) * (gv * gv)
        d_ref[...] = (-ADAM_LR) * ((nm * c1) / (jnp.sqrt(nv * c2) + ADAM_EPS) + ADAM_WD * w_ref[...])
        nm_ref[...] = nm
        nv_ref[...] = nv

    spec = pl.BlockSpec((tr, C), lambda i: (i, 0))
    shape = jax.ShapeDtypeStruct((R, C), F32)
    return pl.pallas_call(
        body, name=name, grid=(R // tr,), in_specs=[spec] * 4, out_specs=[spec] * 3, out_shape=[shape] * 3,
        compiler_params=_params("parallel"),
    )(w, g, m, v)


def _flat_rows(parts, rows):
    flat = jnp.concatenate([q.reshape(-1) for q in parts])
    flat = jnp.pad(flat, (0, rows * LANES - flat.shape[0]))
    return flat.reshape(rows, LANES)


def _round_up(n, m):
    return (n + m - 1) // m * m


def _block_diag(w):
    H, n, _ = w.shape
    eye = jnp.eye(H, dtype=w.dtype)
    return (eye[:, None, :, None] * w[:, :, None, :]).reshape(H * n, H * n)


def _diag_blocks(w, H, n):
    w4 = w.reshape(H, n, H, n)
    return jnp.stack([w4[h, :, h, :] for h in range(H)])


FF_SHARD = D_FF // 2
FF_HALF = D_FFP // 2
FF_ROWS = D_FF // N_CHIP


def _ffn_in_weights(g_in):
    pad = lambda a: jnp.pad(a, ((0, 0), (0, FF_HALF - FF_SHARD)))
    wg = jnp.concatenate([pad(g_in[0]), pad(g_in[1])], axis=1)
    wu = jnp.concatenate([pad(g_in[2]), pad(g_in[3])], axis=1)
    return wg, wu


def _ffn_out_weights(g_out):
    zeros = jnp.zeros((FF_HALF - FF_SHARD, D), g_out.dtype)
    return jnp.concatenate([g_out[0], g_out[1], zeros, g_out[2], g_out[3], zeros], axis=0)


LAND_SHAPES = {"ffn1_w_in": (D, FF_HALF), "ffn1_w_out": (FF_ROWS, D), "w_in": (D, D_IN // N_CHIP),
               "w_out": (D // N_CHIP, D), "ffn2_w_in": (D, FF_HALF), "ffn2_w_out": (FF_ROWS, D)}


def _ffn_grad_windows(dwg, dwu, dwout):
    w_in, w_out = [], []
    for j in range(N_CHIP):
        col = (j % 2) * FF_HALF
        row = (j // 2) * FF_HALF + (j % 2) * FF_ROWS
        w_in.append((dwg if j < 2 else dwu, lambda r, col=col: r.at[:, pl.ds(col, FF_HALF)]))
        w_out.append((dwout, lambda r, row=row: r.at[pl.ds(row, FF_ROWS), :]))
    return w_in, w_out


def _mix_grad_windows(dwin, dwo):
    rows = D // N_CHIP
    win = [(dwin, lambda r, j=j: r.at[j]) for j in range(N_CHIP)]
    wo = [(dwo, lambda r, j=j: r.at[pl.ds(j * rows, rows), :]) for j in range(N_CHIP)]
    return win, wo


def kernel(x, ffn1_norm, ffn1_w_in, ffn1_w_out, mix_norm, w_in, conv_w, conv_b, rg_w_a, rg_b_a, rg_w_x, rg_b_x, lru_lambda, pool_w, pool_scale, sgu_norm, sgu_w, sgu_b, w_out, ffn2_norm, ffn2_w_in, ffn2_w_out, final_norm, loss_target, m_ffn1_norm, m_ffn1_w_in, m_ffn1_w_out, m_mix_norm, m_w_in, m_conv_w, m_conv_b, m_rg_w_a, m_rg_b_a, m_rg_w_x, m_rg_b_x, m_lru_lambda, m_pool_w, m_pool_scale, m_sgu_norm, m_sgu_w, m_sgu_b, m_w_out, m_ffn2_norm, m_ffn2_w_in, m_ffn2_w_out, m_final_norm, v_ffn1_norm, v_ffn1_w_in, v_ffn1_w_out, v_mix_norm, v_w_in, v_conv_w, v_conv_b, v_rg_w_a, v_rg_b_a, v_rg_w_x, v_rg_b_x, v_lru_lambda, v_pool_w, v_pool_scale, v_sgu_norm, v_sgu_w, v_sgu_b, v_w_out, v_ffn2_norm, v_ffn2_w_in, v_ffn2_w_out, v_final_norm):
    args = locals()
    W = {n: args[n] for n in WEIGHTS}
    M = {n: args["m_" + n] for n in WEIGHTS}
    V = {n: args["v_" + n] for n in WEIGHTS}
    depth = ffn1_norm.shape[0]
    T = x.shape[1]
    xi, yi, ci = _place()
    chip = 2 * xi + yi

    assert depth == 2, "core c of a chip sends and reduces layer c"
    groups = [(l, names) for l in range(depth)
              for names in (["ffn1_w_in"], ["ffn1_w_out", "w_in", "w_out"], ["ffn2_w_in", "ffn2_w_out"])]
    wb = {n: W[n].astype(BF16) for n in BIG}
    flights = {}

    def weights_start(k, dep=None):
        l, names = groups[k]
        shards = [wb[n][l] for n in names]
        if dep is not None:
            shards[0] = _tie(shards[0], dep)
        lands = [lax.empty((N_CHIP,) + s.shape, BF16) for s in shards]
        send, recv, lands, token = _gather_start(l, shards, lands, "weights_start_%d" % k)
        flights[k] = (shards, lands, send, recv)
        return token

    def weights_wait(k, after):
        l, names = groups[k]
        shards, lands, send, recv = flights[k]
        got = _gather_wait(l, shards, lands, send, recv, after, "weights_wait_%d" % k)
        token = weights_start(k + 2, got[0]) if k + 2 < len(groups) else None
        return dict(zip(names, got)), token

    def after_start(a, token):
        return a if token is None else _tie(a, token)

    first_token = weights_start(0)
    weights_start(1)
    conv_rows = _round_up(conv_w.size, 8 * LANES) // LANES
    conv_all = _all_gather8(_flat_rows([conv_w], conv_rows), "gather_conv_w").reshape(N_CHIP, 2, -1)
    conv_all = conv_all[:, 0, :conv_w.size].reshape((N_CHIP,) + conv_w.shape)
    conv_full = jnp.concatenate([conv_all[j] for j in range(N_CHIP)], axis=2)

    layers = []
    for l in range(depth):
        L = {f: dict(norm=W[f + "_norm"][l][None]) for f in ("ffn1", "ffn2")}
        ws = jnp.where(jnp.tril(jnp.ones((CHUNK, CHUNK), bool))[None], sgu_w[l], 0.0)
        wax = jnp.concatenate([_block_diag(rg_w_a[l]), _block_diag(rg_w_x[l])], axis=1)
        wpool = _block_diag(pool_w[l])
        L["mix"] = dict(
            conv_w=conv_full[l], conv_b=conv_b[l][None], wax=wax.astype(BF16), wax_t=wax.T.astype(BF16),
            bax=jnp.concatenate([rg_b_a[l].reshape(-1), rg_b_x[l].reshape(-1)])[None], lam=lru_lambda[l][None],
            wpool=wpool.astype(BF16), wpool_t=wpool.T.astype(BF16), pool_scale=pool_scale[l][None],
            sgu_norm=sgu_norm[l][None], ws=ws.astype(BF16), ws_t=jnp.swapaxes(ws, 1, 2).astype(BF16),
            bz=jnp.repeat(sgu_b[l].T, 64, axis=1))
        L["mix_norm"] = mix_norm[l][None]
        layers.append(L)
    layers[0]["ffn1"]["norm"] = _tie(layers[0]["ffn1"]["norm"], first_token)

    xs = x[0]
    saved = []
    for l, L in enumerate(layers):
        F1, F2 = L["ffn1"], L["ffn2"]
        h = _rms_fwd(xs, F1["norm"])
        got, token = weights_wait(3 * l, h)
        F1["wg"], F1["wu"] = _ffn_in_weights(got["ffn1_w_in"])
        h = after_start(h, token)
        g, u, a = _ffn_in(h, F1["wg"], F1["wu"])
        got, token = weights_wait(3 * l + 1, a)
        F1["wout"] = _ffn_out_weights(got["ffn1_w_out"])
        L["w_in"] = jnp.concatenate([got["w_in"][j] for j in range(N_CHIP)], axis=1)
        L["w_out"] = jnp.concatenate([got["w_out"][j] for j in range(N_CHIP)], axis=0)
        a = after_start(a, token)
        x1 = _mm_res(a, F1["wout"], xs, 0.5, "ffn_out")
        s1 = (xs, h, g, u, a)
        hm = _rms_fwd(x1, L["mix_norm"])
        p = _mm(hm, L["w_in"], F32, "mix_in")
        ycat, hs = _mix_fwd(p, L["mix"])
        x2 = _mm_res(ycat, L["w_out"], x1, 1.0, "mix_out")
        h2 = _rms_fwd(x2, F2["norm"])
        got, token = weights_wait(3 * l + 2, h2)
        F2["wg"], F2["wu"] = _ffn_in_weights(got["ffn2_w_in"])
        F2["wout"] = _ffn_out_weights(got["ffn2_w_out"])
        h2 = after_start(h2, token)
        g2, u2, a2 = _ffn_in(h2, F2["wg"], F2["wu"])
        x3 = _mm_res(a2, F2["wout"], x2, 0.5, "ffn_out")
        saved.append((s1, (x1, hm, p, ycat, hs), (x2, h2, g2, u2, a2)))
        xs = x3

    dx, dxb, d_final, loss_part = _final(xs, loss_target[0], final_norm[None])

    G = {n: [None] * depth for n in SMALL if n != "final_norm"}
    lands = {n: lax.empty((N_DEV,) + LAND_SHAPES[n], BF16) for n in BIG}
    in_flight = []

    def send_grads(l, windows, tag):
        names = list(windows)
        send, recv, thru, token = _exchange_start(l, [windows[n] for n in names], [lands[n] for n in names],
                                                  "grads_start_" + tag)
        lands.update(zip(names, thru))
        in_flight.append((l, names, [windows[n] for n in names], send, recv, "grads_wait_" + tag))
        return token

    def ffn_bwd(dx, dxb, F, s, f, l, pending, send_now):
        xin, h, g, u, a = s
        dg, du = _ffn_mid_bwd(dxb, F["wout"], g, u)
        dwout = _mm_tn(a, dxb, 0.5, "ffn_dwout")
        dwg = _mm_tn(h, dg, 1.0, "ffn_dwg")
        dwu = _mm_tn(h, du, 1.0, "ffn_dwu")
        pending[f + "_w_in"], pending[f + "_w_out"] = _ffn_grad_windows(dwg, dwu, dwout)
        if send_now:
            dg = _tie(dg, send_grads(l, pending, "l%d_%s" % (l, f)))
        dh = _mm2_nt(dg, F["wg"], du, F["wu"], "ffn_dh")
        dx, dxb, dn = _rms_bwd(xin, dh, F["norm"], dx)
        G[f + "_norm"][l] = dn[0]
        return dx, dxb

    for l in reversed(range(depth)):
        L = layers[l]
        s1, (x1, hm, p, ycat, hs), s2 = saved[l]
        pending = {}
        dx, dxb = ffn_bwd(dx, dxb, L["ffn2"], s2, "ffn2", l, pending, l == 0)
        if l == 0:
            pending = {}
        dycat = _mm_nt(dxb, L["w_out"], "mix_dy")
        dwo = _mm_tn(ycat, dxb, 1.0, "mix_dwout")
        mg = _mix_bwd(dycat, p, hs, L["mix"])
        dwin = _mm_tn(hm, mg["dp"], 1.0, "mix_dwin")
        pending["w_in"], pending["w_out"] = _mix_grad_windows(jnp.stack(jnp.split(dwin, N_CHIP, axis=1)), dwo)
        if l == 0:
            dp = _tie(mg["dp"], send_grads(l, pending, "l0_mix"))
            pending = {}
        else:
            dp = mg["dp"]
        dhm = _mm_nt(dp, L["w_in"], "mix_dh")
        dx, dxb, dn = _rms_bwd(x1, dhm, L["mix_norm"], dx)
        G["mix_norm"][l] = dn[0]
        G["conv_w"][l], G["conv_b"][l] = mg["conv_w"], mg["conv_b"][0]
        G["rg_w_a"][l] = _diag_blocks(mg["wax"][:, :D_RNN], 8, 64)
        G["rg_w_x"][l] = _diag_blocks(mg["wax"][:, D_RNN:], 8, 64)
        G["rg_b_a"][l] = mg["bax"][0, :D_RNN].reshape(8, 64)
        G["rg_b_x"][l] = mg["bax"][0, D_RNN:].reshape(8, 64)
        G["lru_lambda"][l] = mg["lam"][0]
        G["pool_w"][l] = _diag_blocks(mg["wpool"], 4, 64)
        G["pool_scale"][l], G["sgu_norm"][l] = mg["pool_scale"][0], mg["sgu_norm"][0]
        G["sgu_w"][l] = mg["ws"]
        G["sgu_b"][l] = mg["bz"].reshape(CHUNK, 4, 64).sum(-1).T
        dx, dxb = ffn_bwd(dx, dxb, L["ffn1"], s1, "ffn1", l, pending, l == 0)
        if l > 0:
            dxb = _tie(dxb, send_grads(l, pending, "l%d" % l))
    grad_x = dx[None]
    G = {n: jnp.stack(v) for n, v in G.items()}
    G["final_norm"] = d_final[0]

    for l, names, windows, send, recv, tag in in_flight:
        lands.update(zip(names, _exchange_wait(l, windows, [lands[n] for n in names], send, recv, dx, tag)))
    both = [_sum_share(lands[n], "sum_share_" + n) for n in BIG]
    grads = {n: (a[:, :, :FF_SHARD] if n.endswith("w_in") and n != "w_in" else a) for n, a in zip(BIG, both)}

    small_sizes = [int(np.prod(G[n].shape)) for n in SMALL]
    srows = _round_up(sum(small_sizes) + 1, N_DEV * 8 * LANES) // (N_DEV * LANES)
    sflat = _flat_rows([G[n] for n in SMALL] + [loss_part[0, :1]], N_DEV * srows)
    sgot = _all_to_all(sflat.reshape(N_DEV, srows, LANES), "exchange_small_grads")
    sall = _all_gather8(_sum8(sgot, "sum_small_grads"), "share_small_grads").reshape(-1)
    off = 0
    for n, size in zip(SMALL, small_sizes):
        grads[n] = sall[off:off + size].reshape(G[n].shape)
        off += size
    loss = sall[off]
    grads["conv_w"] = lax.dynamic_slice_in_dim(grads["conv_w"], chip * conv_w.shape[2], conv_w.shape[2], axis=2)

    delta, new_m, new_v = {}, {}, {}
    for n in BIG:
        shp = W[n].shape
        two_d = (shp[0] * shp[1], shp[2])
        outs = _adamw(W[n].reshape(two_d), grads[n].reshape(two_d), M[n].reshape(two_d), V[n].reshape(two_d),
                      "adamw_" + n)
        delta[n], new_m[n], new_v[n] = (o.reshape(shp) for o in outs)
    arows = _round_up(sum(int(np.prod(W[n].shape)) for n in SMALL), 8 * LANES) // LANES
    outs = _adamw(*(_flat_rows([src[n] for n in SMALL], arows) for src in (W, grads, M, V)), "adamw_small")
    outs = [o.reshape(-1) for o in outs]
    off = 0
    for n in SMALL:
        size = int(np.prod(W[n].shape))
        delta[n], new_m[n], new_v[n] = (o[off:off + size].reshape(W[n].shape) for o in outs)
        off += size

    return (loss, grad_x, *[grads[n] for n in WEIGHTS], *[delta[n] for n in WEIGHTS],
            *[new_m[n] for n in WEIGHTS], *[new_v[n] for n in WEIGHTS])
```

```python
import math

import jax
import jax.numpy as jnp
import numpy as np
from jax import lax
from jax.experimental import pallas as pl
from jax.experimental.pallas import tpu as pltpu

F32 = jnp.float32
BF16 = jnp.bfloat16
MESH = pl.DeviceIdType.MESH

D = 1024
D_RNN = 512
D_POOL = 256
D_SGU = 256
D_IN = 1792
D_FF = 2752
D_FFP = 2816
CHUNK = 128
HALO = 16
EPS = 1e-6
LRU_C = 8.0
N_DEV = 8
N_CHIP = 4
LANES = 1024
VMEM_LIMIT = 56 * 1024 * 1024

ADAM_LR, ADAM_B1, ADAM_B2, ADAM_EPS, ADAM_WD, ADAM_STEP = 0.001, 0.9, 0.999, 1e-08, 0.01, 10

BIG = ("ffn1_w_in", "ffn1_w_out", "w_in", "w_out", "ffn2_w_in", "ffn2_w_out")
SMALL = ("ffn1_norm", "mix_norm", "conv_w", "conv_b", "rg_w_a", "rg_b_a", "rg_w_x", "rg_b_x", "lru_lambda",
         "pool_w", "pool_scale", "sgu_norm", "sgu_w", "sgu_b", "ffn2_norm", "final_norm")
WEIGHTS = ("ffn1_norm", "ffn1_w_in", "ffn1_w_out", "mix_norm", "w_in", "conv_w", "conv_b", "rg_w_a", "rg_b_a",
           "rg_w_x", "rg_b_x", "lru_lambda", "pool_w", "pool_scale", "sgu_norm", "sgu_w", "sgu_b", "w_out",
           "ffn2_norm", "ffn2_w_in", "ffn2_w_out", "final_norm")


def _params(*sem):
    return pltpu.CompilerParams(dimension_semantics=sem, vmem_limit_bytes=VMEM_LIMIT)


def _gelu(x):
    c = math.sqrt(2.0 / math.pi)
    t = jnp.tanh(c * (x + 0.044715 * (x * x * x)))
    return 0.5 * x * (1.0 + t)


def _gelu_and_grad(x):
    c = math.sqrt(2.0 / math.pi)
    x2 = x * x
    t = jnp.tanh(c * (x + 0.044715 * (x2 * x)))
    g = 0.5 * x * (1.0 + t)
    dg = 0.5 * (1.0 + t) + 0.5 * x * (1.0 - t * t) * (c * (1.0 + 3.0 * 0.044715 * x2))
    return g, dg


def _sigmoid(x):
    return 0.5 * jnp.tanh(0.5 * x) + 0.5


def _dot(a, b):
    return jnp.dot(a, b, preferred_element_type=F32)


def _dot_tn(a, b):
    return lax.dot_general(a, b, (((0,), (0,)), ((), ())), preferred_element_type=F32)


def _dot_nt(a, b):
    return lax.dot_general(a, b, (((1,), (1,)), ((), ())), preferred_element_type=F32)


def _rms_fwd(x, g, tm=512):
    T = x.shape[0]
    tm = min(tm, T)

    def body(x_ref, g_ref, o_ref):
        xv = x_ref[...]
        r = lax.rsqrt(jnp.mean(xv * xv, axis=-1, keepdims=True) + EPS)
        o_ref[...] = (xv * r * g_ref[...]).astype(BF16)

    return pl.pallas_call(
        body, name="rms_fwd", grid=(T // tm,),
        in_specs=[pl.BlockSpec((tm, D), lambda i: (i, 0)), pl.BlockSpec((1, D), lambda i: (0, 0))],
        out_specs=pl.BlockSpec((tm, D), lambda i: (i, 0)),
        out_shape=jax.ShapeDtypeStruct((T, D), BF16),
        compiler_params=_params("parallel"),
    )(x, g)


def _rms_bwd(x, dh, g, dres, tm=512):
    T = x.shape[0]
    tm = min(tm, T)

    def body(x_ref, dh_ref, g_ref, dres_ref, dx_ref, dxb_ref, dg_ref):
        xv = x_ref[...]
        r = lax.rsqrt(jnp.mean(xv * xv, axis=-1, keepdims=True) + EPS)
        xhat = xv * r
        dy = dh_ref[...]
        dxhat = dy * g_ref[...]
        dx = dres_ref[...] + r * (dxhat - xhat * jnp.mean(dxhat * xhat, axis=-1, keepdims=True))
        dx_ref[...] = dx
        dxb_ref[...] = dx.astype(BF16)

        @pl.when(pl.program_id(0) == 0)
        def _():
            dg_ref[...] = jnp.zeros_like(dg_ref)

        dg_ref[...] += jnp.sum(dy * xhat, axis=0, keepdims=True)

    row = pl.BlockSpec((tm, D), lambda i: (i, 0))
    vec = pl.BlockSpec((1, D), lambda i: (0, 0))
    return pl.pallas_call(
        body, name="rms_bwd", grid=(T // tm,),
        in_specs=[row, row, vec, row], out_specs=[row, row, vec],
        out_shape=[jax.ShapeDtypeStruct((T, D), F32), jax.ShapeDtypeStruct((T, D), BF16),
                   jax.ShapeDtypeStruct((1, D), F32)],
        compiler_params=_params("arbitrary"),
    )(x, dh, g, dres)


def _tile(n, limit):
    best = 128
    for t in range(128, min(n, limit) + 1, 128):
        if n % t == 0:
            best = t
    assert n % best == 0, (n, limit)
    return best


def _mm(a, b, out_dtype, name, tm=1024, tn=512):
    M, K = a.shape
    N = b.shape[1]
    tm, tn = min(tm, M), _tile(N, tn)

    def body(a_ref, b_ref, o_ref):
        o_ref[...] = _dot(a_ref[...], b_ref[...]).astype(out_dtype)

    return pl.pallas_call(
        body, name=name, grid=(M // tm, N // tn),
        in_specs=[pl.BlockSpec((tm, K), lambda i, j: (i, 0)), pl.BlockSpec((K, tn), lambda i, j: (0, j))],
        out_specs=pl.BlockSpec((tm, tn), lambda i, j: (i, j)),
        out_shape=jax.ShapeDtypeStruct((M, N), out_dtype),
        compiler_params=_params("parallel", "parallel"),
    )(a, b)


def _mm_res(a, b, res, scale, name, tm=1024, tn=512):
    M, K = a.shape
    N = b.shape[1]
    tm, tn = min(tm, M), _tile(N, tn)

    def body(a_ref, b_ref, r_ref, o_ref):
        o_ref[...] = r_ref[...] + scale * _dot(a_ref[...], b_ref[...])

    return pl.pallas_call(
        body, name=name, grid=(M // tm, N // tn),
        in_specs=[pl.BlockSpec((tm, K), lambda i, j: (i, 0)), pl.BlockSpec((K, tn), lambda i, j: (0, j)),
                  pl.BlockSpec((tm, tn), lambda i, j: (i, j))],
        out_specs=pl.BlockSpec((tm, tn), lambda i, j: (i, j)),
        out_shape=jax.ShapeDtypeStruct((M, N), F32),
        compiler_params=_params("parallel", "parallel"),
    )(a, b, res)


def _mm_nt(a, b, name, tm=1024, tn=512):
    M, K = a.shape
    N = b.shape[0]
    tm, tn = min(tm, M), _tile(N, tn)

    def body(a_ref, b_ref, o_ref):
        o_ref[...] = _dot_nt(a_ref[...], b_ref[...])

    return pl.pallas_call(
        body, name=name, grid=(M // tm, N // tn),
        in_specs=[pl.BlockSpec((tm, K), lambda i, j: (i, 0)), pl.BlockSpec((tn, K), lambda i, j: (j, 0))],
        out_specs=pl.BlockSpec((tm, tn), lambda i, j: (i, j)),
        out_shape=jax.ShapeDtypeStruct((M, N), F32),
        compiler_params=_params("parallel", "parallel"),
    )(a, b)


def _mm2_nt(a1, b1, a2, b2, name, tm=1024, tn=512):
    M, K = a1.shape
    N = b1.shape[0]
    tm, tn = min(tm, M), _tile(N, tn)

    def body(a1_ref, b1_ref, a2_ref, b2_ref, o_ref):
        o_ref[...] = _dot_nt(a1_ref[...], b1_ref[...]) + _dot_nt(a2_ref[...], b2_ref[...])

    aspec = pl.BlockSpec((tm, K), lambda i, j: (i, 0))
    bspec = pl.BlockSpec((tn, K), lambda i, j: (j, 0))
    return pl.pallas_call(
        body, name=name, grid=(M // tm, N // tn),
        in_specs=[aspec, bspec, aspec, bspec],
        out_specs=pl.BlockSpec((tm, tn), lambda i, j: (i, j)),
        out_shape=jax.ShapeDtypeStruct((M, N), F32),
        compiler_params=_params("parallel", "parallel"),
    )(a1, b1, a2, b2)


def _mm_tn(a, b, scale, name, tm=1792, tn=1792, tk=512):
    T, M = a.shape
    N = b.shape[1]
    tm, tn, tk = _tile(M, tm), _tile(N, tn), min(tk, T)
    nk = T // tk

    def body(a_ref, b_ref, o_ref, acc_ref):
        k = pl.program_id(2)

        @pl.when(k == 0)
        def _():
            acc_ref[...] = jnp.zeros_like(acc_ref)

        acc_ref[...] += _dot_tn(a_ref[...], b_ref[...])

        @pl.when(k == nk - 1)
        def _():
            o_ref[...] = (scale * acc_ref[...]).astype(BF16)

    return pl.pallas_call(
        body, name=name, grid=(M // tm, N // tn, nk),
        in_specs=[pl.BlockSpec((tk, tm), lambda i, j, k: (k, i)), pl.BlockSpec((tk, tn), lambda i, j, k: (k, j))],
        out_specs=pl.BlockSpec((tm, tn), lambda i, j, k: (i, j)),
        out_shape=jax.ShapeDtypeStruct((M, N), BF16),
        scratch_shapes=[pltpu.VMEM((tm, tn), F32)],
        compiler_params=_params("parallel", "parallel", "arbitrary"),
    )(a, b)


def _ffn_in(h, wg, wu, tm=1024, tn=256):
    T = h.shape[0]
    tm = min(tm, T)

    def body(h_ref, wg_ref, wu_ref, g_ref, u_ref, a_ref):
        hv = h_ref[...]
        g = _dot(hv, wg_ref[...])
        u = _dot(hv, wu_ref[...])
        g_ref[...] = g.astype(BF16)
        u_ref[...] = u.astype(BF16)
        a_ref[...] = (g * _sigmoid(g) * u).astype(BF16)

    wspec = pl.BlockSpec((D, tn), lambda i, j: (0, j))
    ospec = pl.BlockSpec((tm, tn), lambda i, j: (i, j))
    oshape = jax.ShapeDtypeStruct((T, D_FFP), BF16)
    return pl.pallas_call(
        body, name="ffn_in", grid=(T // tm, D_FFP // tn),
        in_specs=[pl.BlockSpec((tm, D), lambda i, j: (i, 0)), wspec, wspec],
        out_specs=[ospec, ospec, ospec], out_shape=[oshape, oshape, oshape],
        compiler_params=_params("parallel", "parallel"),
    )(h, wg, wu)


def _ffn_mid_bwd(dy, wout, g, u, tm=1024, tn=256):
    T = dy.shape[0]
    tm = min(tm, T)

    def body(dy_ref, w_ref, g_ref, u_ref, dg_ref, du_ref):
        da = 0.5 * _dot_nt(dy_ref[...], w_ref[...])
        g = g_ref[...].astype(F32)
        u = u_ref[...].astype(F32)
        s = _sigmoid(g)
        dg_ref[...] = (da * u * (s * (1.0 + g * (1.0 - s)))).astype(BF16)
        du_ref[...] = (da * (g * s)).astype(BF16)

    ospec = pl.BlockSpec((tm, tn), lambda i, j: (i, j))
    oshape = jax.ShapeDtypeStruct((T, D_FFP), BF16)
    return pl.pallas_call(
        body, name="ffn_mid_bwd", grid=(T // tm, D_FFP // tn),
        in_specs=[pl.BlockSpec((tm, D), lambda i, j: (i, 0)), pl.BlockSpec((tn, D), lambda i, j: (j, 0)),
                  ospec, ospec],
        out_specs=[ospec, ospec], out_shape=[oshape, oshape],
        compiler_params=_params("parallel", "parallel"),
    )(dy, wout, g, u)


def _final(x, tgt, gf, tm=512):
    T = x.shape[0]
    tm = min(tm, T)

    def body(x_ref, t_ref, g_ref, dx_ref, dxb_ref, dg_ref, loss_ref):
        xv = x_ref[...]
        r = lax.rsqrt(jnp.mean(xv * xv, axis=-1, keepdims=True) + EPS)
        xhat = xv * r
        err = xhat * g_ref[...] - t_ref[...]
        dy = err * (1.0 / D)
        dxhat = dy * g_ref[...]
        dx = r * (dxhat - xhat * jnp.mean(dxhat * xhat, axis=-1, keepdims=True))
        dx_ref[...] = dx
        dxb_ref[...] = dx.astype(BF16)

        @pl.when(pl.program_id(0) == 0)
        def _():
            dg_ref[...] = jnp.zeros_like(dg_ref)
            loss_ref[...] = jnp.zeros_like(loss_ref)

        dg_ref[...] += jnp.sum(dy * xhat, axis=0, keepdims=True)
        loss_ref[...] += (0.5 / D) * jnp.sum(err * err)

    row = pl.BlockSpec((tm, D), lambda i: (i, 0))
    vec = pl.BlockSpec((1, D), lambda i: (0, 0))
    return pl.pallas_call(
        body, name="final_loss", grid=(T // tm,),
        in_specs=[row, row, vec],
        out_specs=[row, row, vec, pl.BlockSpec((1, 128), lambda i: (0, 0))],
        out_shape=[jax.ShapeDtypeStruct((T, D), F32), jax.ShapeDtypeStruct((T, D), BF16),
                   jax.ShapeDtypeStruct((1, D), F32), jax.ShapeDtypeStruct((1, 128), F32)],
        compiler_params=_params("arbitrary"),
    )(x, tgt, gf)


def _mix_block(T, limit):
    return min(limit, T // 2)


def _rows(tb, width):
    return lax.broadcasted_iota(jnp.int32, (tb, width), 0)


def _rglru_gates(xc, wax_ref, bax_ref, lam_ref):
    pre = _dot(xc.astype(BF16), wax_ref[...]) + bax_ref[...]
    r = _sigmoid(pre[:, :D_RNN])
    ig = _sigmoid(pre[:, D_RNN:])
    z = -lam_ref[...]
    sp = jnp.maximum(z, 0.0) + jnp.log(1.0 + jnp.exp(-jnp.abs(z)))
    log_a = (-LRU_C) * r * sp
    a = jnp.exp(log_a)
    mult = jnp.sqrt(-jnp.tanh(log_a) * (1.0 + a * a))
    return r, ig, sp, a, mult


def _conv(xa_ext, cw_ref, cb_ref):
    y = cb_ref[...] + cw_ref[3:4, :] * xa_ext
    for k in range(1, 4):
        y = y + cw_ref[3 - k:4 - k, :] * pltpu.roll(xa_ext, k, 0)
    return y[HALO:]


def _pool_window_lanes():
    lane = lax.broadcasted_iota(jnp.int32, (1, D_POOL), 1)
    return jnp.where(lane < 64, 2, jnp.where(lane < 128, 4, jnp.where(lane < 192, 8, 16)))


def _pool_select(s2, s4, s8, s16):
    lane = lax.broadcasted_iota(jnp.int32, s2.shape, 1)
    return jnp.where(lane < 64, s2, jnp.where(lane < 128, s4, jnp.where(lane < 192, s8, s16)))


def _pool_diff(xp_ext, t0, tb):
    s2 = xp_ext + pltpu.roll(xp_ext, 1, 0)
    s4 = s2 + pltpu.roll(s2, 2, 0)
    s8 = s4 + pltpu.roll(s4, 4, 0)
    s16 = s8 + pltpu.roll(s8, 8, 0)
    sel = _pool_select(s2, s4, s8, s16)[HALO:]
    cnt = jnp.minimum(t0 + _rows(tb, D_POOL) + 1, _pool_window_lanes()).astype(F32)
    return sel / cnt - xp_ext[HALO:], cnt


def _head_masks():
    lane = lax.broadcasted_iota(jnp.int32, (1, D_SGU), 1)
    return [((lane >= 64 * h) & (lane < 64 * (h + 1))).astype(F32) for h in range(4)]


def _sgu_mix(w_ref, vch, masks):
    z = masks[0] * _dot(w_ref[0], vch)
    for h in range(1, 4):
        z = z + masks[h] * _dot(w_ref[h], vch)
    return z


def _mix_fwd(p, prm):
    T = p.shape[0]
    tb = _mix_block(T, 512)
    nb = T // tb

    def body(p_ref, xah_ref, xph_ref, cw_ref, cb_ref, wax_ref, bax_ref, lam_ref, wp_ref, ps_ref, sgn_ref,
             ws_ref, bz_ref, y_ref, hs_ref, carry_ref):
        i = pl.program_id(0)
        keep = (i > 0).astype(F32)

        @pl.when(i == 0)
        def _():
            carry_ref[...] = jnp.zeros_like(carry_ref)

        xa_ext = jnp.concatenate([xah_ref[...] * keep, p_ref[:, 512:1024]], axis=0)
        xc = _conv(xa_ext, cw_ref, cb_ref)
        r, ig, sp, a, mult = _rglru_gates(xc, wax_ref, bax_ref, lam_ref)
        bv = mult * (ig * xc)
        row = _rows(tb, D_RNN)
        s = 1
        while s < tb:
            m = row >= s
            bv = jnp.where(m, a * pltpu.roll(bv, s, 0) + bv, bv)
            a = jnp.where(m, a * pltpu.roll(a, s, 0), a)
            s *= 2
        h = bv + a * carry_ref[0:1, :]
        hs_ref[...] = h
        last = jnp.sum(jnp.where(_rows(8, D_RNN) == 7, hs_ref[tb - 8:tb, :], 0.0), axis=0, keepdims=True)
        carry_ref[...] = jnp.broadcast_to(last, carry_ref.shape)
        y_ref[:, 0:512] = (_gelu(p_ref[:, 0:512]) * h).astype(BF16)

        xp_ext = jnp.concatenate([xph_ref[...] * keep, p_ref[:, 1024:1280]], axis=0)
        d, _ = _pool_diff(xp_ext, i * tb, tb)
        y_ref[:, 512:768] = (_dot(d.astype(BF16), wp_ref[...]) * ps_ref[...]).astype(BF16)

        ug = _gelu(p_ref[:, 1280:1536])
        vg = _gelu(p_ref[:, 1536:1792])
        rv = lax.rsqrt(jnp.mean(vg * vg, axis=-1, keepdims=True) + EPS)
        vn = (vg * rv * sgn_ref[...]).astype(BF16)
        masks = _head_masks()
        for ci in range(tb // CHUNK):
            sl = slice(ci * CHUNK, (ci + 1) * CHUNK)
            z = _sgu_mix(ws_ref, vn[sl], masks) + bz_ref[...]
            y_ref[sl, 768:1024] = (ug[sl] * z).astype(BF16)

    hb = tb // HALO

    def halo(i):
        return jnp.maximum(i * hb - 1, 0)

    def full(shape):
        return pl.BlockSpec(shape, lambda i: (0,) * len(shape))

    return pl.pallas_call(
        body, name="mix_fwd", grid=(nb,),
        in_specs=[pl.BlockSpec((tb, D_IN), lambda i: (i, 0)),
                  pl.BlockSpec((HALO, D_RNN), lambda i: (halo(i), 1)),
                  pl.BlockSpec((HALO, D_POOL), lambda i: (halo(i), 4)),
                  full((4, D_RNN)), full((1, D_RNN)), full((D_RNN, 2 * D_RNN)), full((1, 2 * D_RNN)),
                  full((1, D_RNN)), full((D_POOL, D_POOL)), full((1, D_POOL)), full((1, D_SGU)),
                  full((4, CHUNK, CHUNK)), full((CHUNK, D_SGU))],
        out_specs=[pl.BlockSpec((tb, D), lambda i: (i, 0)), pl.BlockSpec((tb, D_RNN), lambda i: (i, 0))],
        out_shape=[jax.ShapeDtypeStruct((T, D), BF16), jax.ShapeDtypeStruct((T, D_RNN), F32)],
        scratch_shapes=[pltpu.VMEM((8, D_RNN), F32)],
        compiler_params=_params("arbitrary"),
    )(p, p, p, prm["conv_w"], prm["conv_b"], prm["wax"], prm["bax"], prm["lam"], prm["wpool"], prm["pool_scale"],
      prm["sgu_norm"], prm["ws"], prm["bz"])


def _mix_bwd(dy, p, hs, prm):
    T = p.shape[0]
    tb = _mix_block(T, 256)
    nb = T // tb
    hb = tb // HALO

    def body(dy_ref, p_ref, xah_ref, xph_ref, hs_ref, hsh_ref, cw_ref, cb_ref, wax_ref, waxt_ref, bax_ref,
             lam_ref, wp_ref, wpt_ref, ps_ref, sgn_ref, ws_ref, wst_ref, bz_ref,
             dp_ref, dcw_ref, dcb_ref, dwax_ref, dbax_ref, dlam_ref, dwp_ref, dps_ref, dsgn_ref, dws_ref,
             dbz_ref, gcarry_ref, xcfut_ref, mfut_ref):
        i = pl.program_id(0)
        bi = nb - 1 - i
        keep = (bi > 0).astype(F32)

        @pl.when(i == 0)
        def _():
            for ref in (dcw_ref, dcb_ref, dwax_ref, dbax_ref, dlam_ref, dwp_ref, dps_ref, dsgn_ref, dws_ref,
                        dbz_ref, gcarry_ref, xcfut_ref, mfut_ref):
                ref[...] = jnp.zeros_like(ref)

        xa_ext = jnp.concatenate([xah_ref[...] * keep, p_ref[:, 512:1024]], axis=0)
        xc = _conv(xa_ext, cw_ref, cb_ref)
        r, ig, sp, a, mult = _rglru_gates(xc, wax_ref, bax_ref, lam_ref)
        gg, dgg = _gelu_and_grad(p_ref[:, 0:512])
        dya = dy_ref[:, 0:512]
        dp_ref[:, 0:512] = (dya * hs_ref[...] * dgg).astype(BF16)
        row = _rows(tb, D_RNN)
        g = dya * gg + jnp.where(row == tb - 1, gcarry_ref[0:1, :], 0.0)
        al = pltpu.roll(a, tb - 1, 0)
        s = 1
        while s < tb:
            m = row < tb - s
            g = jnp.where(m, al * pltpu.roll(g, tb - s, 0) + g, g)
            al = jnp.where(m, al * pltpu.roll(al, tb - s, 0), al)
            s *= 2
        first = jnp.sum(jnp.where(_rows(8, D_RNN) == 0, (a * g)[0:8], 0.0), axis=0, keepdims=True)
        gcarry_ref[...] = jnp.broadcast_to(first, gcarry_ref.shape)
        hs_ext = jnp.concatenate([hsh_ref[...] * keep, hs_ref[...]], axis=0)
        h_prev = pltpu.roll(hs_ext, 1, 0)[HALO:]
        ix = ig * xc
        dlog_a = g * h_prev * a - (g * ix) * (a * a / mult)
        dlam_ref[...] += jnp.sum(dlog_a * r, axis=0, keepdims=True) * (LRU_C * _sigmoid(-lam_ref[...]))
        dpre_r = dlog_a * ((-LRU_C) * sp) * (r * (1.0 - r))
        dpre_i = (g * mult * xc) * (ig * (1.0 - ig))
        dpre = jnp.concatenate([dpre_r, dpre_i], axis=1)
        dbax_ref[...] += jnp.sum(dpre, axis=0, keepdims=True)
        dpre_b = dpre.astype(BF16)
        dwax_ref[...] += _dot_tn(xc.astype(BF16), dpre_b)
        dxc = g * mult * ig + _dot(dpre_b, waxt_ref[...])
        dcb_ref[...] += jnp.sum(dxc, axis=0, keepdims=True)
        for k in range(4):
            xs = xa_ext[HALO:] if k == 3 else pltpu.roll(xa_ext, 3 - k, 0)[HALO:]
            dcw_ref[k:k + 1, :] += jnp.sum(dxc * xs, axis=0, keepdims=True)
        dxc_ext = jnp.concatenate([dxc, xcfut_ref[...]], axis=0)
        n = tb + HALO
        dxa = cw_ref[3:4, :] * dxc_ext
        for k in range(1, 4):
            dxa = dxa + cw_ref[3 - k:4 - k, :] * pltpu.roll(dxc_ext, n - k, 0)
        dp_ref[:, 512:1024] = dxa[:tb].astype(BF16)
        xcfut_ref[...] = dxc[0:HALO]

        xp_ext = jnp.concatenate([xph_ref[...] * keep, p_ref[:, 1024:1280]], axis=0)
        d, cnt = _pool_diff(xp_ext, bi * tb, tb)
        db = d.astype(BF16)
        dyb = dy_ref[:, 512:768]
        dps_ref[...] += jnp.sum(dyb * _dot(db, wp_ref[...]), axis=0, keepdims=True)
        dq = (dyb * ps_ref[...]).astype(BF16)
        dwp_ref[...] += _dot_tn(db, dq)
        dd = _dot(dq, wpt_ref[...])
        mm = dd / cnt
        m_ext = jnp.concatenate([mm, mfut_ref[...]], axis=0)
        f2 = m_ext + pltpu.roll(m_ext, n - 1, 0)
        f4 = f2 + pltpu.roll(f2, n - 2, 0)
        f8 = f4 + pltpu.roll(f4, n - 4, 0)
        f16 = f8 + pltpu.roll(f8, n - 8, 0)
        dp_ref[:, 1024:1280] = (_pool_select(f2, f4, f8, f16)[:tb] - dd).astype(BF16)
        mfut_ref[...] = mm[0:HALO]

        ug, dug = _gelu_and_grad(p_ref[:, 1280:1536])
        vg, dvg = _gelu_and_grad(p_ref[:, 1536:1792])
        rv = lax.rsqrt(jnp.mean(vg * vg, axis=-1, keepdims=True) + EPS)
        vhat = vg * rv
        vn = (vhat * sgn_ref[...]).astype(BF16)
        dyc = dy_ref[:, 768:1024]
        masks = _head_masks()
        dz = dyc * ug
        dzb = dz.astype(BF16)
        dvn_parts = []
        for ci in range(tb // CHUNK):
            sl = slice(ci * CHUNK, (ci + 1) * CHUNK)
            z = _sgu_mix(ws_ref, vn[sl], masks) + bz_ref[...]
            dp_ref[sl, 1280:1536] = (dyc[sl] * z * dug[sl]).astype(BF16)
            dbz_ref[...] += dz[sl]
            for h in range(4):
                dws_ref[h] += _dot_nt((dz[sl] * masks[h]).astype(BF16), vn[sl])
            dvn_parts.append(_sgu_mix(wst_ref, dzb[sl], masks))
        dvn = jnp.concatenate(dvn_parts, axis=0)
        dsgn_ref[...] += jnp.sum(dvn * vhat, axis=0, keepdims=True)
        dvhat = dvn * sgn_ref[...]
        dvg_in = rv * (dvhat - vhat * jnp.mean(dvhat * vhat, axis=-1, keepdims=True))
        dp_ref[:, 1536:1792] = (dvg_in * dvg).astype(BF16)

        @pl.when(i == nb - 1)
        def _():
            tril = (lax.broadcasted_iota(jnp.int32, (CHUNK, CHUNK), 0)
                    >= lax.broadcasted_iota(jnp.int32, (CHUNK, CHUNK), 1)).astype(F32)
            for h in range(4):
                dws_ref[h] = dws_ref[h] * tril

    def blk(i):
        return nb - 1 - i

    def halo(i):
        return jnp.maximum(blk(i) * hb - 1, 0)

    def full(shape):
        return pl.BlockSpec(shape, lambda i: (0,) * len(shape))

    small_shapes = [(4, D_RNN), (1, D_RNN), (D_RNN, 2 * D_RNN), (1, 2 * D_RNN), (1, D_RNN), (D_POOL, D_POOL),
                    (1, D_POOL), (1, D_SGU), (4, CHUNK, CHUNK), (CHUNK, D_SGU)]
    outs = pl.pallas_call(
        body, name="mix_bwd", grid=(nb,),
        in_specs=[pl.BlockSpec((tb, D), lambda i: (blk(i), 0)),
                  pl.BlockSpec((tb, D_IN), lambda i: (blk(i), 0)),
                  pl.BlockSpec((HALO, D_RNN), lambda i: (halo(i), 1)),
                  pl.BlockSpec((HALO, D_POOL), lambda i: (halo(i), 4)),
                  pl.BlockSpec((tb, D_RNN), lambda i: (blk(i), 0)),
                  pl.BlockSpec((HALO, D_RNN), lambda i: (halo(i), 0)),
                  full((4, D_RNN)), full((1, D_RNN)), full((D_RNN, 2 * D_RNN)), full((2 * D_RNN, D_RNN)),
                  full((1, 2 * D_RNN)), full((1, D_RNN)), full((D_POOL, D_POOL)), full((D_POOL, D_POOL)),
                  full((1, D_POOL)), full((1, D_SGU)), full((4, CHUNK, CHUNK)), full((4, CHUNK, CHUNK)),
                  full((CHUNK, D_SGU))],
        out_specs=[pl.BlockSpec((tb, D_IN), lambda i: (blk(i), 0))] + [full(s) for s in small_shapes],
        out_shape=[jax.ShapeDtypeStruct((T, D_IN), BF16)] + [jax.ShapeDtypeStruct(s, F32) for s in small_shapes],
        scratch_shapes=[pltpu.VMEM((8, D_RNN), F32), pltpu.VMEM((HALO, D_RNN), F32),
                        pltpu.VMEM((HALO, D_POOL), F32)],
        compiler_params=_params("arbitrary"),
    )(dy, p, p, p, hs, hs, prm["conv_w"], prm["conv_b"], prm["wax"], prm["wax_t"], prm["bax"], prm["lam"],
      prm["wpool"], prm["wpool_t"], prm["pool_scale"], prm["sgu_norm"], prm["ws"], prm["ws_t"], prm["bz"])
    names = ("dp", "conv_w", "conv_b", "wax", "bax", "lam", "wpool", "pool_scale", "sgu_norm", "ws", "bz")
    return dict(zip(names, outs))


ANY = pl.BlockSpec(memory_space=pl.ANY)


def _place():
    x, y, c = lax.axis_index("x"), lax.axis_index("y"), lax.axis_index("c")
    return x, y, c


def _all_to_all(xs, name):
    def body(in_ref, out_ref, send_sems, recv_sems, local_sem):
        x, y, c = _place()
        me = 4 * x + 2 * y + c
        mine = pltpu.make_async_copy(in_ref.at[me], out_ref.at[me], local_sem)
        mine.start()
        copies = []
        for rel in range(1, N_DEV):
            tx = 1 - x if rel & 4 else x
            ty = 1 - y if rel & 2 else y
            tc = 1 - c if rel & 1 else c
            cp = pltpu.make_async_remote_copy(
                src_ref=in_ref.at[4 * tx + 2 * ty + tc], dst_ref=out_ref.at[me],
                send_sem=send_sems.at[rel - 1], recv_sem=recv_sems.at[rel - 1],
                device_id=(tx, ty, tc), device_id_type=MESH)
            cp.start()
            copies.append(cp)
        for cp in copies:
            cp.wait()
        mine.wait()

    return pl.pallas_call(
        body, name=name, in_specs=[ANY], out_specs=ANY,
        out_shape=jax.ShapeDtypeStruct(xs.shape, xs.dtype),
        scratch_shapes=[pltpu.SemaphoreType.DMA((N_DEV - 1,)), pltpu.SemaphoreType.DMA((N_DEV - 1,)),
                        pltpu.SemaphoreType.DMA],
    )(xs)


def _all_gather8(xs, name):
    def body(x_ref, out_ref, send_sems, recv_sems, local_sem):
        x, y, c = _place()
        me, sibling = (x, y, c), (x, y, 1 - c)
        chips = [(1 - x, y), (x, 1 - y), (1 - x, 1 - y)]

        def rows(px, py, pc):
            return out_ref.at[4 * px + 2 * py + pc]

        def copy(k, block, to, src=None):
            return pltpu.make_async_remote_copy(
                src_ref=rows(*block) if src is None else src, dst_ref=rows(*block),
                send_sem=send_sems.at[k], recv_sem=recv_sems.at[k], device_id=to, device_id_type=MESH)

        mine = pltpu.make_async_copy(x_ref, rows(*me), local_sem)
        mine.start()
        first = [copy(0, me, sibling, src=x_ref)]
        first += [copy(1 + j, me, (*chip, c), src=x_ref) for j, chip in enumerate(chips)]
        for cp in first:
            cp.start()
        passed = [copy(4 + j, (*chip, c), sibling) for j, chip in enumerate(chips)]
        for j, chip in enumerate(chips):
            copy(1 + j, (*chip, c), me).wait_recv()
            passed[j].start()
        copy(0, sibling, me).wait_recv()
        for j, chip in enumerate(chips):
            copy(4 + j, (*chip, 1 - c), me).wait_recv()
        for cp in first + passed:
            cp.wait_send()
        mine.wait()

    return pl.pallas_call(
        body, name=name, in_specs=[ANY], out_specs=ANY,
        out_shape=jax.ShapeDtypeStruct((N_DEV,) + xs.shape, xs.dtype),
        scratch_shapes=[pltpu.SemaphoreType.DMA((7,)), pltpu.SemaphoreType.DMA((7,)), pltpu.SemaphoreType.DMA],
    )(xs)


HBM = pl.BlockSpec(memory_space=pltpu.HBM)
SEM = pl.BlockSpec(memory_space=pltpu.SEMAPHORE)
EFFECT = pltpu.SideEffectType.DATAFLOW_SIDE_EFFECTING


def _in_hbm(a):
    return pltpu.with_memory_space_constraint(a, pltpu.HBM)


LOCAL_CHUNKS = 16


def _local_copy(src, dst, sems):
    rows = src.shape[0]
    step = -(-rows // (16 * LOCAL_CHUNKS)) * 16
    copies = []
    for n, r0 in enumerate(range(0, rows, step)):
        size = min(step, rows - r0)
        copies.append(pltpu.make_async_copy(src.at[pl.ds(r0, size)], dst.at[pl.ds(r0, size)], sems.at[n]))
    for cp in copies:
        cp.start()
    for cp in copies:
        cp.wait()


def _unique(windows):
    arrays = []
    for per_chip in windows:
        for arr, _ in per_chip:
            if not any(arr is a for a in arrays):
                arrays.append(arr)
    return arrays


def _exchange_start(layer, windows, lands, name):
    arrays = _unique(windows)
    na, nt = len(arrays), len(windows)

    def body(*refs):
        in_refs, land_refs = refs[:na], refs[na:na + nt]
        send_sems, recv_sems = refs[na + nt], refs[na + nt + 1]
        token = refs[-1]
        x, y, c = _place()
        me = 4 * x + 2 * y + c
        for t in range(nt):
            for j in range(N_CHIP):
                arr, window = windows[t][j]
                src = window(in_refs[next(i for i, a in enumerate(arrays) if a is arr)])

                @pl.when(me != 2 * j + layer)
                def _():
                    pltpu.make_async_remote_copy(
                        src_ref=src, dst_ref=land_refs[t].at[me], send_sem=send_sems.at[N_CHIP * t + j],
                        recv_sem=recv_sems.at[N_DEV * t + me], device_id=(j // 2, j % 2, layer),
                        device_id_type=MESH).start()
        token[...] = jnp.zeros_like(token)

    outs = pl.pallas_call(
        body, name=name,
        out_shape=(pltpu.SemaphoreType.DMA((N_CHIP * nt,)), pltpu.SemaphoreType.DMA((N_DEV * nt,)),
                   *[pltpu.HBM(a.shape, a.dtype) for a in lands], jax.ShapeDtypeStruct((8, 128), F32)),
        in_specs=[HBM] * (na + nt),
        out_specs=(SEM, SEM, *[HBM] * nt, pl.BlockSpec(memory_space=pltpu.VMEM)),
        input_output_aliases={na + t: 2 + t for t in range(nt)},
        compiler_params=pltpu.CompilerParams(has_side_effects=EFFECT),
    )(*[_in_hbm(a) for a in arrays], *[_in_hbm(a) for a in lands])
    return outs[0], outs[1], list(outs[2:2 + nt]), outs[-1]


def _exchange_wait(layer, windows, lands, send_sems, recv_sems, after, name):
    arrays = _unique(windows)
    na, nt = len(arrays), len(windows)

    def body(*refs):
        in_refs, land_refs = refs[:na], refs[na:na + nt]
        send_sems, recv_sems = refs[na + nt], refs[na + nt + 1]
        local_sem = refs[-1]
        x, y, c = _place()
        me = 4 * x + 2 * y + c

        def source(t, j):
            arr, window = windows[t][j]
            return window(in_refs[next(i for i, a in enumerate(arrays) if a is arr)])

        for t in range(nt):
            for j in range(N_CHIP):
                @pl.when(me != 2 * j + layer)
                def _():
                    pltpu.make_async_remote_copy(
                        src_ref=source(t, j), dst_ref=land_refs[t].at[me], send_sem=send_sems.at[N_CHIP * t + j],
                        recv_sem=recv_sems.at[N_DEV * t + me], device_id=(j // 2, j % 2, layer),
                        device_id_type=MESH).wait_send()

        @pl.when(c == layer)
        def _():
            for t in range(nt):
                for s in range(N_DEV):
                    @pl.when(me != s)
                    def _():
                        slot = land_refs[t].at[s]
                        pltpu.make_async_remote_copy(
                            src_ref=slot, dst_ref=slot, send_sem=send_sems.at[N_CHIP * t],
                            recv_sem=recv_sems.at[N_DEV * t + s], device_id=(x, y, c),
                            device_id_type=MESH).wait_recv()
                for j in range(N_CHIP):
                    @pl.when(me == 2 * j + layer)
                    def _():
                        _local_copy(source(t, j), land_refs[t].at[me], local_sem)

    outs = pl.pallas_call(
        body, name=name,
        out_shape=tuple(pltpu.HBM(a.shape, a.dtype) for a in lands),
        in_specs=[HBM] * (na + nt) + [SEM, SEM, ANY],
        out_specs=tuple([HBM] * nt),
        input_output_aliases={na + t: t for t in range(nt)},
        scratch_shapes=[pltpu.SemaphoreType.DMA((LOCAL_CHUNKS,))],
        compiler_params=pltpu.CompilerParams(has_side_effects=EFFECT),
    )(*[_in_hbm(a) for a in arrays], *lands, send_sems, recv_sems, after)
    return list(outs)


def _other_chips(x, y):
    return [(1 - x, y), (x, 1 - y), (1 - x, 1 - y)]


def _gather_start(layer, shards, lands, name):
    nt = len(shards)

    def body(*refs):
        in_refs, land_refs = refs[:nt], refs[nt:2 * nt]
        send_sems, recv_sems = refs[2 * nt], refs[2 * nt + 1]
        token = refs[-1]
        x, y, c = _place()

        @pl.when(c == layer)
        def _():
            for t in range(nt):
                for rel, (tx, ty) in enumerate(_other_chips(x, y)):
                    for tc in range(2):
                        pltpu.make_async_remote_copy(
                            src_ref=in_refs[t], dst_ref=land_refs[t].at[2 * x + y],
                            send_sem=send_sems.at[6 * t + 2 * rel + tc], recv_sem=recv_sems.at[3 * t + rel],
                            device_id=(tx, ty, tc), device_id_type=MESH).start()

        token[...] = jnp.zeros_like(token)

    outs = pl.pallas_call(
        body, name=name,
        out_shape=(pltpu.SemaphoreType.DMA((6 * nt,)), pltpu.SemaphoreType.DMA((3 * nt,)),
                   *[pltpu.HBM(a.shape, a.dtype) for a in lands], jax.ShapeDtypeStruct((8, 128), F32)),
        in_specs=[HBM] * (2 * nt),
        out_specs=(SEM, SEM, *[HBM] * nt, pl.BlockSpec(memory_space=pltpu.VMEM)),
        input_output_aliases={nt + t: 2 + t for t in range(nt)},
        compiler_params=pltpu.CompilerParams(has_side_effects=EFFECT),
    )(*[_in_hbm(a) for a in shards], *[_in_hbm(a) for a in lands])
    return outs[0], outs[1], list(outs[2:2 + nt]), outs[-1]


def _gather_wait(layer, shards, lands, send_sems, recv_sems, after, name):
    nt = len(shards)

    def body(*refs):
        in_refs, land_refs = refs[:nt], refs[nt:2 * nt]
        send_sems, recv_sems = refs[2 * nt], refs[2 * nt + 1]
        local_sem = refs[-1]
        x, y, c = _place()

        @pl.when(c == layer)
        def _():
            for t in range(nt):
                for rel, (tx, ty) in enumerate(_other_chips(x, y)):
                    for tc in range(2):
                        pltpu.make_async_remote_copy(
                            src_ref=in_refs[t], dst_ref=land_refs[t].at[2 * x + y],
                            send_sem=send_sems.at[6 * t + 2 * rel + tc], recv_sem=recv_sems.at[3 * t + rel],
                            device_id=(tx, ty, tc), device_id_type=MESH).wait_send()

        for t in range(nt):
            for rel, (tx, ty) in enumerate(_other_chips(x, y)):
                slot = land_refs[t].at[2 * tx + ty]
                pltpu.make_async_remote_copy(
                    src_ref=slot, dst_ref=slot, send_sem=send_sems.at[6 * t], recv_sem=recv_sems.at[3 * t + rel],
                    device_id=(x, y, c), device_id_type=MESH).wait_recv()
            _local_copy(in_refs[t], land_refs[t].at[2 * x + y], local_sem)

    outs = pl.pallas_call(
        body, name=name,
        out_shape=tuple(pltpu.HBM(a.shape, a.dtype) for a in lands),
        in_specs=[HBM] * (2 * nt) + [SEM, SEM, ANY],
        out_specs=tuple([HBM] * nt),
        input_output_aliases={nt + t: t for t in range(nt)},
        scratch_shapes=[pltpu.SemaphoreType.DMA((LOCAL_CHUNKS,))],
        compiler_params=pltpu.CompilerParams(has_side_effects=EFFECT),
    )(*[_in_hbm(a) for a in shards], *lands, send_sems, recv_sems, after)
    return list(outs)


def _tie(a, token):
    def body(a_ref, token_ref, o_ref):
        pass

    return pl.pallas_call(
        body, name="tie", in_specs=[ANY, ANY], out_specs=ANY,
        out_shape=jax.ShapeDtypeStruct(a.shape, a.dtype), input_output_aliases={0: 0},
    )(a, token)


def _sum_share(xs, name):
    _, r, cols = xs.shape
    tr = 256 if r % 256 == 0 else r
    nblk = r // tr

    def body(x_ref, out_ref, acc_ref, send_sems, local_sems, recv_sem):
        i = pl.program_id(0)
        slot = i % 2
        x, y, c = _place()

        def copies(s, blk):
            dst = out_ref.at[c, pl.ds(blk * tr, tr), :]
            loc = pltpu.make_async_copy(acc_ref.at[s], dst, local_sems.at[s])
            rem = pltpu.make_async_remote_copy(src_ref=acc_ref.at[s], dst_ref=dst, send_sem=send_sems.at[s],
                                               recv_sem=recv_sem, device_id=(x, y, 1 - c), device_id_type=MESH)
            return loc, rem

        @pl.when(i >= 2)
        def _():
            loc, rem = copies(slot, i - 2)
            loc.wait()
            rem.wait_send()

        acc = x_ref[0].astype(F32)
        for k in range(1, N_DEV):
            acc = acc + x_ref[k].astype(F32)
        acc_ref[slot] = acc
        loc, rem = copies(slot, i)
        loc.start()
        rem.start()

        @pl.when(i == nblk - 1)
        def _():
            for back in range(min(2, nblk)):
                blk = nblk - 1 - back
                loc, rem = copies(blk % 2, blk)
                loc.wait()
                rem.wait_send()
            theirs = out_ref.at[1 - c]
            pltpu.make_async_remote_copy(src_ref=theirs, dst_ref=theirs, send_sem=send_sems.at[0],
                                         recv_sem=recv_sem, device_id=(x, y, 1 - c),
                                         device_id_type=MESH).wait_recv()

    return pl.pallas_call(
        body, name=name, grid=(nblk,),
        in_specs=[pl.BlockSpec((N_DEV, tr, cols), lambda i: (0, i, 0))],
        out_specs=ANY,
        out_shape=jax.ShapeDtypeStruct((2, r, cols), F32),
        scratch_shapes=[pltpu.VMEM((2, tr, cols), F32), pltpu.SemaphoreType.DMA((2,)),
                        pltpu.SemaphoreType.DMA((2,)), pltpu.SemaphoreType.DMA],
        compiler_params=_params("arbitrary"),
    )(xs)


def _sum8(xs, name):
    _, r, cols = xs.shape
    tr = 8
    for cand in (256, 128, 64, 32, 16):
        if r % cand == 0:
            tr = cand
            break

    def body(x_ref, o_ref):
        acc = x_ref[0].astype(F32)
        for k in range(1, N_DEV):
            acc = acc + x_ref[k].astype(F32)
        o_ref[...] = acc

    return pl.pallas_call(
        body, name=name, grid=(r // tr,),
        in_specs=[pl.BlockSpec((N_DEV, tr, cols), lambda i: (0, i, 0))],
        out_specs=pl.BlockSpec((tr, cols), lambda i: (i, 0)),
        out_shape=jax.ShapeDtypeStruct((r, cols), F32),
        compiler_params=_params("parallel"),
    )(xs)


def _adamw(w, g, m, v, name):
    R, C = w.shape
    tr = R
    for cand in (256, 128, 64, 32, 16, 8):
        if R % cand == 0:
            tr = cand
            break
    c1 = 1.0 / (1.0 - ADAM_B1 ** ADAM_STEP)
    c2 = 1.0 / (1.0 - ADAM_B2 ** ADAM_STEP)

    def body(w_ref, g_ref, m_ref, v_ref, d_ref, nm_ref, nv_ref):
        gv = g_ref[...]
        nm = ADAM_B1 * m_ref[...] + (1.0 - ADAM_B1) * gv
        nv = ADAM_B2 * v_ref[...] + (1.0 - ADAM_B2) * (gv * gv)
        d_ref[...] = (-ADAM_LR) * ((nm * c1) / (jnp.sqrt(nv * c2) + ADAM_EPS) + ADAM_WD * w_ref[...])
        nm_ref[...] = nm
        nv_ref[...] = nv

    spec = pl.BlockSpec((tr, C), lambda i: (i, 0))
    shape = jax.ShapeDtypeStruct((R, C), F32)
    return pl.pallas_call(
        body, name=name, grid=(R // tr,), in_specs=[spec] * 4, out_specs=[spec] * 3, out_shape=[shape] * 3,
        compiler_params=_params("parallel"),
    )(w, g, m, v)


def _flat_rows(parts, rows):
    flat = jnp.concatenate([q.reshape(-1) for q in parts])
    flat = jnp.pad(flat, (0, rows * LANES - flat.shape[0]))
    return flat.reshape(rows, LANES)


def _round_up(n, m):
    return (n + m - 1) // m * m


def _block_diag(w):
    H, n, _ = w.shape
    eye = jnp.eye(H, dtype=w.dtype)
    return (eye[:, None, :, None] * w[:, :, None, :]).reshape(H * n, H * n)


def _diag_blocks(w, H, n):
    w4 = w.reshape(H, n, H, n)
    return jnp.stack([w4[h, :, h, :] for h in range(H)])


FF_SHARD = D_FF // 2
FF_HALF = D_FFP // 2
FF_ROWS = D_FF // N_CHIP


def _ffn_in_weights(g_in):
    pad = lambda a: jnp.pad(a, ((0, 0), (0, FF_HALF - FF_SHARD)))
    wg = jnp.concatenate([pad(g_in[0]), pad(g_in[1])], axis=1)
    wu = jnp.concatenate([pad(g_in[2]), pad(g_in[3])], axis=1)
    return wg, wu


def _ffn_out_weights(g_out):
    zeros = jnp.zeros((FF_HALF - FF_SHARD, D), g_out.dtype)
    return jnp.concatenate([g_out[0], g_out[1], zeros, g_out[2], g_out[3], zeros], axis=0)


LAND_SHAPES = {"ffn1_w_in": (D, FF_HALF), "ffn1_w_out": (FF_ROWS, D), "w_in": (D, D_IN // N_CHIP),
               "w_out": (D // N_CHIP, D), "ffn2_w_in": (D, FF_HALF), "ffn2_w_out": (FF_ROWS, D)}


def _ffn_grad_windows(dwg, dwu, dwout):
    w_in, w_out = [], []
    for j in range(N_CHIP):
        col = (j % 2) * FF_HALF
        row = (j // 2) * FF_HALF + (j % 2) * FF_ROWS
        w_in.append((dwg if j < 2 else dwu, lambda r, col=col: r.at[:, pl.ds(col, FF_HALF)]))
        w_out.append((dwout, lambda r, row=row: r.at[pl.ds(row, FF_ROWS), :]))
    return w_in, w_out


def _mix_grad_windows(dwin, dwo):
    rows = D // N_CHIP
    win = [(dwin, lambda r, j=j: r.at[j]) for j in range(N_CHIP)]
    wo = [(dwo, lambda r, j=j: r.at[pl.ds(j * rows, rows), :]) for j in range(N_CHIP)]
    return win, wo


def kernel(x, ffn1_norm, ffn1_w_in, ffn1_w_out, mix_norm, w_in, conv_w, conv_b, rg_w_a, rg_b_a, rg_w_x, rg_b_x, lru_lambda, pool_w, pool_scale, sgu_norm, sgu_w, sgu_b, w_out, ffn2_norm, ffn2_w_in, ffn2_w_out, final_norm, loss_target, m_ffn1_norm, m_ffn1_w_in, m_ffn1_w_out, m_mix_norm, m_w_in, m_conv_w, m_conv_b, m_rg_w_a, m_rg_b_a, m_rg_w_x, m_rg_b_x, m_lru_lambda, m_pool_w, m_pool_scale, m_sgu_norm, m_sgu_w, m_sgu_b, m_w_out, m_ffn2_norm, m_ffn2_w_in, m_ffn2_w_out, m_final_norm, v_ffn1_norm, v_ffn1_w_in, v_ffn1_w_out, v_mix_norm, v_w_in, v_conv_w, v_conv_b, v_rg_w_a, v_rg_b_a, v_rg_w_x, v_rg_b_x, v_lru_lambda, v_pool_w, v_pool_scale, v_sgu_norm, v_sgu_w, v_sgu_b, v_w_out, v_ffn2_norm, v_ffn2_w_in, v_ffn2_w_out, v_final_norm):
    args = locals()
    W = {n: args[n] for n in WEIGHTS}
    M = {n: args["m_" + n] for n in WEIGHTS}
    V = {n: args["v_" + n] for n in WEIGHTS}
    depth = ffn1_norm.shape[0]
    T = x.shape[1]
    xi, yi, ci = _place()
    chip = 2 * xi + yi

    assert depth == 2, "core c of a chip sends and reduces layer c"
    groups = [(l, names) for l in range(depth)
              for names in (["ffn1_w_in"], ["ffn1_w_out", "w_in", "w_out"], ["ffn2_w_in", "ffn2_w_out"])]
    wb = {n: W[n].astype(BF16) for n in BIG}
    groups[1][1].append("conv_w")
    conv_shard = conv_w.reshape(-1, conv_w.shape[-1])
    flights = {}

    def weights_start(k, dep=None):
        l, names = groups[k]
        shards = [conv_shard if n == "conv_w" else wb[n][l] for n in names]
        if dep is not None:
            shards[0] = _tie(shards[0], dep)
        lands = [lax.empty((N_CHIP,) + s.shape, s.dtype) for s in shards]
        send, recv, lands, token = _gather_start(l, shards, lands, "weights_start_%d" % k)
        flights[k] = (shards, lands, send, recv)
        return token

    def weights_wait(k, after):
        l, names = groups[k]
        shards, lands, send, recv = flights[k]
        got = _gather_wait(l, shards, lands, send, recv, after, "weights_wait_%d" % k)
        token = weights_start(k + 2, got[0]) if k + 2 < len(groups) else None
        return dict(zip(names, got)), token

    def after_start(a, token):
        return a if token is None else _tie(a, token)

    first_tokens = [weights_start(0), weights_start(1)]

    layers = []
    for l in range(depth):
        L = {f: dict(norm=W[f + "_norm"][l][None]) for f in ("ffn1", "ffn2")}
        ws = jnp.where(jnp.tril(jnp.ones((CHUNK, CHUNK), bool))[None], sgu_w[l], 0.0)
        wax = jnp.concatenate([_block_diag(rg_w_a[l]), _block_diag(rg_w_x[l])], axis=1)
        wpool = _block_diag(pool_w[l])
        L["mix"] = dict(
            conv_b=conv_b[l][None], wax=wax.astype(BF16), wax_t=wax.T.astype(BF16),
            bax=jnp.concatenate([rg_b_a[l].reshape(-1), rg_b_x[l].reshape(-1)])[None], lam=lru_lambda[l][None],
            wpool=wpool.astype(BF16), wpool_t=wpool.T.astype(BF16), pool_scale=pool_scale[l][None],
            sgu_norm=sgu_norm[l][None], ws=ws.astype(BF16), ws_t=jnp.swapaxes(ws, 1, 2).astype(BF16),
            bz=jnp.repeat(sgu_b[l].T, 64, axis=1))
        L["mix_norm"] = mix_norm[l][None]
        layers.append(L)
    for token in first_tokens:
        layers[0]["ffn1"]["norm"] = _tie(layers[0]["ffn1"]["norm"], token)

    xs = x[0]
    saved = []
    for l, L in enumerate(layers):
        F1, F2 = L["ffn1"], L["ffn2"]
        h = _rms_fwd(xs, F1["norm"])
        got, token = weights_wait(3 * l, h)
        F1["wg"], F1["wu"] = _ffn_in_weights(got["ffn1_w_in"])
        h = after_start(h, token)
        g, u, a = _ffn_in(h, F1["wg"], F1["wu"])
        got, token = weights_wait(3 * l + 1, a)
        F1["wout"] = _ffn_out_weights(got["ffn1_w_out"])
        L["w_in"] = jnp.concatenate([got["w_in"][j] for j in range(N_CHIP)], axis=1)
        L["w_out"] = jnp.concatenate([got["w_out"][j] for j in range(N_CHIP)], axis=0)
        if "conv_w" in got:
            conv_full = jnp.concatenate([got["conv_w"][j] for j in range(N_CHIP)], axis=1)
            for ll in range(depth):
                layers[ll]["mix"]["conv_w"] = conv_full.reshape(depth, 4, D_RNN)[ll]
        a = after_start(a, token)
        x1 = _mm_res(a, F1["wout"], xs, 0.5, "ffn_out")
        s1 = (xs, h, g, u, a)
        hm = _rms_fwd(x1, L["mix_norm"])
        p = _mm(hm, L["w_in"], F32, "mix_in")
        ycat, hs = _mix_fwd(p, L["mix"])
        x2 = _mm_res(ycat, L["w_out"], x1, 1.0, "mix_out")
        h2 = _rms_fwd(x2, F2["norm"])
        got, token = weights_wait(3 * l + 2, h2)
        F2["wg"], F2["wu"] = _ffn_in_weights(got["ffn2_w_in"])
        F2["wout"] = _ffn_out_weights(got["ffn2_w_out"])
        h2 = after_start(h2, token)
        g2, u2, a2 = _ffn_in(h2, F2["wg"], F2["wu"])
        x3 = _mm_res(a2, F2["wout"], x2, 0.5, "ffn_out")
        saved.append((s1, (x1, hm, p, ycat, hs), (x2, h2, g2, u2, a2)))
        xs = x3

    dx, dxb, d_final, loss_part = _final(xs, loss_target[0], final_norm[None])

    G = {n: [None] * depth for n in SMALL if n != "final_norm"}
    lands = {n: lax.empty((N_DEV,) + LAND_SHAPES[n], BF16) for n in BIG}
    in_flight = []

    def send_grads(l, windows, tag):
        names = list(windows)
        send, recv, thru, token = _exchange_start(l, [windows[n] for n in names], [lands[n] for n in names],
                                                  "grads_start_" + tag)
        lands.update(zip(names, thru))
        in_flight.append((l, names, [windows[n] for n in names], send, recv, "grads_wait_" + tag))
        return token

    def ffn_bwd(dx, dxb, F, s, f, l, pending, send_now):
        xin, h, g, u, a = s
        dg, du = _ffn_mid_bwd(dxb, F["wout"], g, u)
        dwout = _mm_tn(a, dxb, 0.5, "ffn_dwout")
        dwg = _mm_tn(h, dg, 1.0, "ffn_dwg")
        dwu = _mm_tn(h, du, 1.0, "ffn_dwu")
        pending[f + "_w_in"], pending[f + "_w_out"] = _ffn_grad_windows(dwg, dwu, dwout)
        if send_now:
            dg = _tie(dg, send_grads(l, pending, "l%d_%s" % (l, f)))
        dh = _mm2_nt(dg, F["wg"], du, F["wu"], "ffn_dh")
        dx, dxb, dn = _rms_bwd(xin, dh, F["norm"], dx)
        G[f + "_norm"][l] = dn[0]
        return dx, dxb

    for l in reversed(range(depth)):
        L = layers[l]
        s1, (x1, hm, p, ycat, hs), s2 = saved[l]
        pending = {}
        dx, dxb = ffn_bwd(dx, dxb, L["ffn2"], s2, "ffn2", l, pending, l == 0)
        if l == 0:
            pending = {}
        dycat = _mm_nt(dxb, L["w_out"], "mix_dy")
        dwo = _mm_tn(ycat, dxb, 1.0, "mix_dwout")
        mg = _mix_bwd(dycat, p, hs, L["mix"])
        dwin = _mm_tn(hm, mg["dp"], 1.0, "mix_dwin")
        pending["w_in"], pending["w_out"] = _mix_grad_windows(jnp.stack(jnp.split(dwin, N_CHIP, axis=1)), dwo)
        if l == 0:
            dp = _tie(mg["dp"], send_grads(l, pending, "l0_mix"))
            pending = {}
        else:
            dp = mg["dp"]
        dhm = _mm_nt(dp, L["w_in"], "mix_dh")
        dx, dxb, dn = _rms_bwd(x1, dhm, L["mix_norm"], dx)
        G["mix_norm"][l] = dn[0]
        G["conv_w"][l], G["conv_b"][l] = mg["conv_w"], mg["conv_b"][0]
        G["rg_w_a"][l] = _diag_blocks(mg["wax"][:, :D_RNN], 8, 64)
        G["rg_w_x"][l] = _diag_blocks(mg["wax"][:, D_RNN:], 8, 64)
        G["rg_b_a"][l] = mg["bax"][0, :D_RNN].reshape(8, 64)
        G["rg_b_x"][l] = mg["bax"][0, D_RNN:].reshape(8, 64)
        G["lru_lambda"][l] = mg["lam"][0]
        G["pool_w"][l] = _diag_blocks(mg["wpool"], 4, 64)
        G["pool_scale"][l], G["sgu_norm"][l] = mg["pool_scale"][0], mg["sgu_norm"][0]
        G["sgu_w"][l] = mg["ws"]
        G["sgu_b"][l] = mg["bz"].reshape(CHUNK, 4, 64).sum(-1).T
        dx, dxb = ffn_bwd(dx, dxb, L["ffn1"], s1, "ffn1", l, pending, l == 0)
        if l > 0:
            dxb = _tie(dxb, send_grads(l, pending, "l%d" % l))
    grad_x = dx[None]
    G = {n: jnp.stack(v) for n, v in G.items()}
    G["final_norm"] = d_final[0]

    for l, names, windows, send, recv, tag in in_flight:
        lands.update(zip(names, _exchange_wait(l, windows, [lands[n] for n in names], send, recv, dx, tag)))
    both = [_sum_share(lands[n], "sum_share_" + n) for n in BIG]
    grads = {n: (a[:, :, :FF_SHARD] if n.endswith("w_in") and n != "w_in" else a) for n, a in zip(BIG, both)}

    small_sizes = [int(np.prod(G[n].shape)) for n in SMALL]
    srows = _round_up(sum(small_sizes) + 1, N_DEV * 8 * LANES) // (N_DEV * LANES)
    sflat = _flat_rows([G[n] for n in SMALL] + [loss_part[0, :1]], N_DEV * srows)
    sgot = _all_to_all(sflat.reshape(N_DEV, srows, LANES), "exchange_small_grads")
    sall = _all_gather8(_sum8(sgot, "sum_small_grads"), "share_small_grads").reshape(-1)
    off = 0
    for n, size in zip(SMALL, small_sizes):
        grads[n] = sall[off:off + size].reshape(G[n].shape)
        off += size
    loss = sall[off]
    grads["conv_w"] = lax.dynamic_slice_in_dim(grads["conv_w"], chip * conv_w.shape[2], conv_w.shape[2], axis=2)

    delta, new_m, new_v = {}, {}, {}
    for n in BIG:
        shp = W[n].shape
        two_d = (shp[0] * shp[1], shp[2])
        outs = _adamw(W[n].reshape(two_d), grads[n].reshape(two_d), M[n].reshape(two_d), V[n].reshape(two_d),
                      "adamw_" + n)
        delta[n], new_m[n], new_v[n] = (o.reshape(shp) for o in outs)
    arows = _round_up(sum(int(np.prod(W[n].shape)) for n in SMALL), 8 * LANES) // LANES
    outs = _adamw(*(_flat_rows([src[n] for n in SMALL], arows) for src in (W, grads, M, V)), "adamw_small")
    outs = [o.reshape(-1) for o in outs]
    off = 0
    for n in SMALL:
        size = int(np.prod(W[n].shape))
        delta[n], new_m[n], new_v[n] = (o[off:off + size].reshape(W[n].shape) for o in outs)
        off += size

    return (loss, grad_x, *[grads[n] for n in WEIGHTS], *[delta[n] for n in WEIGHTS],
            *[new_m[n] for n in WEIGHTS], *[new_v[n] for n in WEIGHTS])
```

```python
import math

import jax
import jax.numpy as jnp
import numpy as np
from jax import lax
from jax.experimental import pallas as pl
from jax.experimental.pallas import tpu as pltpu

F32 = jnp.float32
BF16 = jnp.bfloat16
MESH = pl.DeviceIdType.MESH

D = 1024
D_RNN = 512
D_POOL = 256
D_SGU = 256
D_IN = 1792
D_FF = 2752
D_FFP = 2816
CHUNK = 128
HALO = 16
EPS = 1e-6
LRU_C = 8.0
N_DEV = 8
N_CHIP = 4
LANES = 1024
VMEM_LIMIT = 56 * 1024 * 1024

ADAM_LR, ADAM_B1, ADAM_B2, ADAM_EPS, ADAM_WD, ADAM_STEP = 0.001, 0.9, 0.999, 1e-08, 0.01, 10

BIG = ("ffn1_w_in", "ffn1_w_out", "w_in", "w_out", "ffn2_w_in", "ffn2_w_out")
SMALL = ("ffn1_norm", "mix_norm", "conv_w", "conv_b", "rg_w_a", "rg_b_a", "rg_w_x", "rg_b_x", "lru_lambda",
         "pool_w", "pool_scale", "sgu_norm", "sgu_w", "sgu_b", "ffn2_norm", "final_norm")
WEIGHTS = ("ffn1_norm", "ffn1_w_in", "ffn1_w_out", "mix_norm", "w_in", "conv_w", "conv_b", "rg_w_a", "rg_b_a",
           "rg_w_x", "rg_b_x", "lru_lambda", "pool_w", "pool_scale", "sgu_norm", "sgu_w", "sgu_b", "w_out",
           "ffn2_norm", "ffn2_w_in", "ffn2_w_out", "final_norm")


def _params(*sem):
    return pltpu.CompilerParams(dimension_semantics=sem, vmem_limit_bytes=VMEM_LIMIT)


def _gelu(x):
    c = math.sqrt(2.0 / math.pi)
    t = jnp.tanh(c * (x + 0.044715 * (x * x * x)))
    return 0.5 * x * (1.0 + t)


def _gelu_and_grad(x):
    c = math.sqrt(2.0 / math.pi)
    x2 = x * x
    t = jnp.tanh(c * (x + 0.044715 * (x2 * x)))
    g = 0.5 * x * (1.0 + t)
    dg = 0.5 * (1.0 + t) + 0.5 * x * (1.0 - t * t) * (c * (1.0 + 3.0 * 0.044715 * x2))
    return g, dg


def _sigmoid(x):
    return 0.5 * jnp.tanh(0.5 * x) + 0.5


def _dot(a, b):
    return jnp.dot(a, b, preferred_element_type=F32)


def _dot_tn(a, b):
    return lax.dot_general(a, b, (((0,), (0,)), ((), ())), preferred_element_type=F32)


def _dot_nt(a, b):
    return lax.dot_general(a, b, (((1,), (1,)), ((), ())), preferred_element_type=F32)


def _rms_fwd(x, g, tm=512):
    T = x.shape[0]
    tm = min(tm, T)

    def body(x_ref, g_ref, o_ref):
        xv = x_ref[...]
        r = lax.rsqrt(jnp.mean(xv * xv, axis=-1, keepdims=True) + EPS)
        o_ref[...] = (xv * r * g_ref[...]).astype(BF16)

    return pl.pallas_call(
        body, name="rms_fwd", grid=(T // tm,),
        in_specs=[pl.BlockSpec((tm, D), lambda i: (i, 0)), pl.BlockSpec((1, D), lambda i: (0, 0))],
        out_specs=pl.BlockSpec((tm, D), lambda i: (i, 0)),
        out_shape=jax.ShapeDtypeStruct((T, D), BF16),
        compiler_params=_params("parallel"),
    )(x, g)


def _rms_bwd(x, dh, g, dres, tm=512):
    T = x.shape[0]
    tm = min(tm, T)

    def body(x_ref, dh_ref, g_ref, dres_ref, dx_ref, dxb_ref, dg_ref):
        xv = x_ref[...]
        r = lax.rsqrt(jnp.mean(xv * xv, axis=-1, keepdims=True) + EPS)
        xhat = xv * r
        dy = dh_ref[...]
        dxhat = dy * g_ref[...]
        dx = dres_ref[...] + r * (dxhat - xhat * jnp.mean(dxhat * xhat, axis=-1, keepdims=True))
        dx_ref[...] = dx
        dxb_ref[...] = dx.astype(BF16)

        @pl.when(pl.program_id(0) == 0)
        def _():
            dg_ref[...] = jnp.zeros_like(dg_ref)

        dg_ref[...] += jnp.sum(dy * xhat, axis=0, keepdims=True)

    row = pl.BlockSpec((tm, D), lambda i: (i, 0))
    vec = pl.BlockSpec((1, D), lambda i: (0, 0))
    return pl.pallas_call(
        body, name="rms_bwd", grid=(T // tm,),
        in_specs=[row, row, vec, row], out_specs=[row, row, vec],
        out_shape=[jax.ShapeDtypeStruct((T, D), F32), jax.ShapeDtypeStruct((T, D), BF16),
                   jax.ShapeDtypeStruct((1, D), F32)],
        compiler_params=_params("arbitrary"),
    )(x, dh, g, dres)


def _tile(n, limit):
    best = 128
    for t in range(128, min(n, limit) + 1, 128):
        if n % t == 0:
            best = t
    assert n % best == 0, (n, limit)
    return best


def _mm(a, b, out_dtype, name, tm=1024, tn=512):
    M, K = a.shape
    N = b.shape[1]
    tm, tn = min(tm, M), _tile(N, tn)

    def body(a_ref, b_ref, o_ref):
        o_ref[...] = _dot(a_ref[...], b_ref[...]).astype(out_dtype)

    return pl.pallas_call(
        body, name=name, grid=(M // tm, N // tn),
        in_specs=[pl.BlockSpec((tm, K), lambda i, j: (i, 0)), pl.BlockSpec((K, tn), lambda i, j: (0, j))],
        out_specs=pl.BlockSpec((tm, tn), lambda i, j: (i, j)),
        out_shape=jax.ShapeDtypeStruct((M, N), out_dtype),
        compiler_params=_params("parallel", "parallel"),
    )(a, b)


def _mm_res(a, b, res, scale, name, tm=1024, tn=512):
    M, K = a.shape
    N = b.shape[1]
    tm, tn = min(tm, M), _tile(N, tn)

    def body(a_ref, b_ref, r_ref, o_ref):
        o_ref[...] = r_ref[...] + scale * _dot(a_ref[...], b_ref[...])

    return pl.pallas_call(
        body, name=name, grid=(M // tm, N // tn),
        in_specs=[pl.BlockSpec((tm, K), lambda i, j: (i, 0)), pl.BlockSpec((K, tn), lambda i, j: (0, j)),
                  pl.BlockSpec((tm, tn), lambda i, j: (i, j))],
        out_specs=pl.BlockSpec((tm, tn), lambda i, j: (i, j)),
        out_shape=jax.ShapeDtypeStruct((M, N), F32),
        compiler_params=_params("parallel", "parallel"),
    )(a, b, res)


def _mm_nt(a, b, name, tm=1024, tn=512):
    M, K = a.shape
    N = b.shape[0]
    tm, tn = min(tm, M), _tile(N, tn)

    def body(a_ref, b_ref, o_ref):
        o_ref[...] = _dot_nt(a_ref[...], b_ref[...])

    return pl.pallas_call(
        body, name=name, grid=(M // tm, N // tn),
        in_specs=[pl.BlockSpec((tm, K), lambda i, j: (i, 0)), pl.BlockSpec((tn, K), lambda i, j: (j, 0))],
        out_specs=pl.BlockSpec((tm, tn), lambda i, j: (i, j)),
        out_shape=jax.ShapeDtypeStruct((M, N), F32),
        compiler_params=_params("parallel", "parallel"),
    )(a, b)


def _mm2_nt(a1, b1, a2, b2, name, tm=1024, tn=512):
    M, K = a1.shape
    N = b1.shape[0]
    tm, tn = min(tm, M), _tile(N, tn)

    def body(a1_ref, b1_ref, a2_ref, b2_ref, o_ref):
        o_ref[...] = _dot_nt(a1_ref[...], b1_ref[...]) + _dot_nt(a2_ref[...], b2_ref[...])

    aspec = pl.BlockSpec((tm, K), lambda i, j: (i, 0))
    bspec = pl.BlockSpec((tn, K), lambda i, j: (j, 0))
    return pl.pallas_call(
        body, name=name, grid=(M // tm, N // tn),
        in_specs=[aspec, bspec, aspec, bspec],
        out_specs=pl.BlockSpec((tm, tn), lambda i, j: (i, j)),
        out_shape=jax.ShapeDtypeStruct((M, N), F32),
        compiler_params=_params("parallel", "parallel"),
    )(a1, b1, a2, b2)


def _mm_tn(a, b, scale, name, tm=1792, tn=1792, tk=512):
    T, M = a.shape
    N = b.shape[1]
    tm, tn, tk = _tile(M, tm), _tile(N, tn), min(tk, T)
    nk = T // tk

    def body(a_ref, b_ref, o_ref, acc_ref):
        k = pl.program_id(2)

        @pl.when(k == 0)
        def _():
            acc_ref[...] = jnp.zeros_like(acc_ref)

        acc_ref[...] += _dot_tn(a_ref[...], b_ref[...])

        @pl.when(k == nk - 1)
        def _():
            o_ref[...] = (scale * acc_ref[...]).astype(BF16)

    return pl.pallas_call(
        body, name=name, grid=(M // tm, N // tn, nk),
        in_specs=[pl.BlockSpec((tk, tm), lambda i, j, k: (k, i)), pl.BlockSpec((tk, tn), lambda i, j, k: (k, j))],
        out_specs=pl.BlockSpec((tm, tn), lambda i, j, k: (i, j)),
        out_shape=jax.ShapeDtypeStruct((M, N), BF16),
        scratch_shapes=[pltpu.VMEM((tm, tn), F32)],
        compiler_params=_params("parallel", "parallel", "arbitrary"),
    )(a, b)


def _ffn_in(h, wg, wu, tm=1024, tn=256):
    T = h.shape[0]
    tm = min(tm, T)

    def body(h_ref, wg_ref, wu_ref, g_ref, u_ref, a_ref):
        hv = h_ref[...]
        g = _dot(hv, wg_ref[...])
        u = _dot(hv, wu_ref[...])
        g_ref[...] = g.astype(BF16)
        u_ref[...] = u.astype(BF16)
        a_ref[...] = (g * _sigmoid(g) * u).astype(BF16)

    wspec = pl.BlockSpec((D, tn), lambda i, j: (0, j))
    ospec = pl.BlockSpec((tm, tn), lambda i, j: (i, j))
    oshape = jax.ShapeDtypeStruct((T, D_FFP), BF16)
    return pl.pallas_call(
        body, name="ffn_in", grid=(T // tm, D_FFP // tn),
        in_specs=[pl.BlockSpec((tm, D), lambda i, j: (i, 0)), wspec, wspec],
        out_specs=[ospec, ospec, ospec], out_shape=[oshape, oshape, oshape],
        compiler_params=_params("parallel", "parallel"),
    )(h, wg, wu)


def _ffn_mid_bwd(dy, wout, g, u, tm=1024, tn=256):
    T = dy.shape[0]
    tm = min(tm, T)

    def body(dy_ref, w_ref, g_ref, u_ref, dg_ref, du_ref):
        da = 0.5 * _dot_nt(dy_ref[...], w_ref[...])
        g = g_ref[...].astype(F32)
        u = u_ref[...].astype(F32)
        s = _sigmoid(g)
        dg_ref[...] = (da * u * (s * (1.0 + g * (1.0 - s)))).astype(BF16)
        du_ref[...] = (da * (g * s)).astype(BF16)

    ospec = pl.BlockSpec((tm, tn), lambda i, j: (i, j))
    oshape = jax.ShapeDtypeStruct((T, D_FFP), BF16)
    return pl.pallas_call(
        body, name="ffn_mid_bwd", grid=(T // tm, D_FFP // tn),
        in_specs=[pl.BlockSpec((tm, D), lambda i, j: (i, 0)), pl.BlockSpec((tn, D), lambda i, j: (j, 0)),
                  ospec, ospec],
        out_specs=[ospec, ospec], out_shape=[oshape, oshape],
        compiler_params=_params("parallel", "parallel"),
    )(dy, wout, g, u)


def _final(x, tgt, gf, tm=512):
    T = x.shape[0]
    tm = min(tm, T)

    def body(x_ref, t_ref, g_ref, dx_ref, dxb_ref, dg_ref, loss_ref):
        xv = x_ref[...]
        r = lax.rsqrt(jnp.mean(xv * xv, axis=-1, keepdims=True) + EPS)
        xhat = xv * r
        err = xhat * g_ref[...] - t_ref[...]
        dy = err * (1.0 / D)
        dxhat = dy * g_ref[...]
        dx = r * (dxhat - xhat * jnp.mean(dxhat * xhat, axis=-1, keepdims=True))
        dx_ref[...] = dx
        dxb_ref[...] = dx.astype(BF16)

        @pl.when(pl.program_id(0) == 0)
        def _():
            dg_ref[...] = jnp.zeros_like(dg_ref)
            loss_ref[...] = jnp.zeros_like(loss_ref)

        dg_ref[...] += jnp.sum(dy * xhat, axis=0, keepdims=True)
        loss_ref[...] += (0.5 / D) * jnp.sum(err * err)

    row = pl.BlockSpec((tm, D), lambda i: (i, 0))
    vec = pl.BlockSpec((1, D), lambda i: (0, 0))
    return pl.pallas_call(
        body, name="final_loss", grid=(T // tm,),
        in_specs=[row, row, vec],
        out_specs=[row, row, vec, pl.BlockSpec((1, 128), lambda i: (0, 0))],
        out_shape=[jax.ShapeDtypeStruct((T, D), F32), jax.ShapeDtypeStruct((T, D), BF16),
                   jax.ShapeDtypeStruct((1, D), F32), jax.ShapeDtypeStruct((1, 128), F32)],
        compiler_params=_params("arbitrary"),
    )(x, tgt, gf)


def _mix_block(T, limit):
    return min(limit, T // 2)


def _rows(tb, width):
    return lax.broadcasted_iota(jnp.int32, (tb, width), 0)


def _rglru_gates(xc, wax_ref, bax_ref, lam_ref):
    pre = _dot(xc.astype(BF16), wax_ref[...]) + bax_ref[...]
    r = _sigmoid(pre[:, :D_RNN])
    ig = _sigmoid(pre[:, D_RNN:])
    z = -lam_ref[...]
    sp = jnp.maximum(z, 0.0) + jnp.log(1.0 + jnp.exp(-jnp.abs(z)))
    log_a = (-LRU_C) * r * sp
    a = jnp.exp(log_a)
    mult = jnp.sqrt(-jnp.tanh(log_a) * (1.0 + a * a))
    return r, ig, sp, a, mult


def _conv(xa_ext, cw_ref, cb_ref):
    y = cb_ref[...] + cw_ref[3:4, :] * xa_ext
    for k in range(1, 4):
        y = y + cw_ref[3 - k:4 - k, :] * pltpu.roll(xa_ext, k, 0)
    return y[HALO:]


def _pool_window_lanes():
    lane = lax.broadcasted_iota(jnp.int32, (1, D_POOL), 1)
    return jnp.where(lane < 64, 2, jnp.where(lane < 128, 4, jnp.where(lane < 192, 8, 16)))


def _pool_select(s2, s4, s8, s16):
    lane = lax.broadcasted_iota(jnp.int32, s2.shape, 1)
    return jnp.where(lane < 64, s2, jnp.where(lane < 128, s4, jnp.where(lane < 192, s8, s16)))


def _pool_diff(xp_ext, t0, tb):
    s2 = xp_ext + pltpu.roll(xp_ext, 1, 0)
    s4 = s2 + pltpu.roll(s2, 2, 0)
    s8 = s4 + pltpu.roll(s4, 4, 0)
    s16 = s8 + pltpu.roll(s8, 8, 0)
    sel = _pool_select(s2, s4, s8, s16)[HALO:]
    cnt = jnp.minimum(t0 + _rows(tb, D_POOL) + 1, _pool_window_lanes()).astype(F32)
    return sel / cnt - xp_ext[HALO:], cnt


def _head_masks():
    lane = lax.broadcasted_iota(jnp.int32, (1, D_SGU), 1)
    return [((lane >= 64 * h) & (lane < 64 * (h + 1))).astype(F32) for h in range(4)]


def _sgu_mix(w_ref, vch, masks):
    z = masks[0] * _dot(w_ref[0], vch)
    for h in range(1, 4):
        z = z + masks[h] * _dot(w_ref[h], vch)
    return z


def _mix_fwd(p, prm):
    T = p.shape[0]
    tb = _mix_block(T, 512)
    nb = T // tb

    def body(p_ref, xah_ref, xph_ref, cw_ref, cb_ref, wax_ref, bax_ref, lam_ref, wp_ref, ps_ref, sgn_ref,
             ws_ref, bz_ref, y_ref, hs_ref, carry_ref):
        i = pl.program_id(0)
        keep = (i > 0).astype(F32)

        @pl.when(i == 0)
        def _():
            carry_ref[...] = jnp.zeros_like(carry_ref)

        xa_ext = jnp.concatenate([xah_ref[...] * keep, p_ref[:, 512:1024]], axis=0)
        xc = _conv(xa_ext, cw_ref, cb_ref)
        r, ig, sp, a, mult = _rglru_gates(xc, wax_ref, bax_ref, lam_ref)
        bv = mult * (ig * xc)
        row = _rows(tb, D_RNN)
        s = 1
        while s < tb:
            m = row >= s
            bv = jnp.where(m, a * pltpu.roll(bv, s, 0) + bv, bv)
            a = jnp.where(m, a * pltpu.roll(a, s, 0), a)
            s *= 2
        h = bv + a * carry_ref[0:1, :]
        hs_ref[...] = h
        last = jnp.sum(jnp.where(_rows(8, D_RNN) == 7, hs_ref[tb - 8:tb, :], 0.0), axis=0, keepdims=True)
        carry_ref[...] = jnp.broadcast_to(last, carry_ref.shape)
        y_ref[:, 0:512] = (_gelu(p_ref[:, 0:512]) * h).astype(BF16)

        xp_ext = jnp.concatenate([xph_ref[...] * keep, p_ref[:, 1024:1280]], axis=0)
        d, _ = _pool_diff(xp_ext, i * tb, tb)
        y_ref[:, 512:768] = (_dot(d.astype(BF16), wp_ref[...]) * ps_ref[...]).astype(BF16)

        ug = _gelu(p_ref[:, 1280:1536])
        vg = _gelu(p_ref[:, 1536:1792])
        rv = lax.rsqrt(jnp.mean(vg * vg, axis=-1, keepdims=True) + EPS)
        vn = (vg * rv * sgn_ref[...]).astype(BF16)
        masks = _head_masks()
        for ci in range(tb // CHUNK):
            sl = slice(ci * CHUNK, (ci + 1) * CHUNK)
            z = _sgu_mix(ws_ref, vn[sl], masks) + bz_ref[...]
            y_ref[sl, 768:1024] = (ug[sl] * z).astype(BF16)

    hb = tb // HALO

    def halo(i):
        return jnp.maximum(i * hb - 1, 0)

    def full(shape):
        return pl.BlockSpec(shape, lambda i: (0,) * len(shape))

    return pl.pallas_call(
        body, name="mix_fwd", grid=(nb,),
        in_specs=[pl.BlockSpec((tb, D_IN), lambda i: (i, 0)),
                  pl.BlockSpec((HALO, D_RNN), lambda i: (halo(i), 1)),
                  pl.BlockSpec((HALO, D_POOL), lambda i: (halo(i), 4)),
                  full((4, D_RNN)), full((1, D_RNN)), full((D_RNN, 2 * D_RNN)), full((1, 2 * D_RNN)),
                  full((1, D_RNN)), full((D_POOL, D_POOL)), full((1, D_POOL)), full((1, D_SGU)),
                  full((4, CHUNK, CHUNK)), full((CHUNK, D_SGU))],
        out_specs=[pl.BlockSpec((tb, D), lambda i: (i, 0)), pl.BlockSpec((tb, D_RNN), lambda i: (i, 0))],
        out_shape=[jax.ShapeDtypeStruct((T, D), BF16), jax.ShapeDtypeStruct((T, D_RNN), F32)],
        scratch_shapes=[pltpu.VMEM((8, D_RNN), F32)],
        compiler_params=_params("arbitrary"),
    )(p, p, p, prm["conv_w"], prm["conv_b"], prm["wax"], prm["bax"], prm["lam"], prm["wpool"], prm["pool_scale"],
      prm["sgu_norm"], prm["ws"], prm["bz"])


def _mix_bwd(dy, p, hs, prm):
    T = p.shape[0]
    tb = _mix_block(T, 256)
    nb = T // tb
    hb = tb // HALO

    def body(dy_ref, p_ref, xah_ref, xph_ref, hs_ref, hsh_ref, cw_ref, cb_ref, wax_ref, waxt_ref, bax_ref,
             lam_ref, wp_ref, wpt_ref, ps_ref, sgn_ref, ws_ref, wst_ref, bz_ref,
             dp_ref, dcw_ref, dcb_ref, dwax_ref, dbax_ref, dlam_ref, dwp_ref, dps_ref, dsgn_ref, dws_ref,
             dbz_ref, gcarry_ref, xcfut_ref, mfut_ref):
        i = pl.program_id(0)
        bi = nb - 1 - i
        keep = (bi > 0).astype(F32)

        @pl.when(i == 0)
        def _():
            for ref in (dcw_ref, dcb_ref, dwax_ref, dbax_ref, dlam_ref, dwp_ref, dps_ref, dsgn_ref, dws_ref,
                        dbz_ref, gcarry_ref, xcfut_ref, mfut_ref):
                ref[...] = jnp.zeros_like(ref)

        xa_ext = jnp.concatenate([xah_ref[...] * keep, p_ref[:, 512:1024]], axis=0)
        xc = _conv(xa_ext, cw_ref, cb_ref)
        r, ig, sp, a, mult = _rglru_gates(xc, wax_ref, bax_ref, lam_ref)
        gg, dgg = _gelu_and_grad(p_ref[:, 0:512])
        dya = dy_ref[:, 0:512]
        dp_ref[:, 0:512] = (dya * hs_ref[...] * dgg).astype(BF16)
        row = _rows(tb, D_RNN)
        g = dya * gg + jnp.where(row == tb - 1, gcarry_ref[0:1, :], 0.0)
        al = pltpu.roll(a, tb - 1, 0)
        s = 1
        while s < tb:
            m = row < tb - s
            g = jnp.where(m, al * pltpu.roll(g, tb - s, 0) + g, g)
            al = jnp.where(m, al * pltpu.roll(al, tb - s, 0), al)
            s *= 2
        first = jnp.sum(jnp.where(_rows(8, D_RNN) == 0, (a * g)[0:8], 0.0), axis=0, keepdims=True)
        gcarry_ref[...] = jnp.broadcast_to(first, gcarry_ref.shape)
        hs_ext = jnp.concatenate([hsh_ref[...] * keep, hs_ref[...]], axis=0)
        h_prev = pltpu.roll(hs_ext, 1, 0)[HALO:]
        ix = ig * xc
        dlog_a = g * h_prev * a - (g * ix) * (a * a / mult)
        dlam_ref[...] += jnp.sum(dlog_a * r, axis=0, keepdims=True) * (LRU_C * _sigmoid(-lam_ref[...]))
        dpre_r = dlog_a * ((-LRU_C) * sp) * (r * (1.0 - r))
        dpre_i = (g * mult * xc) * (ig * (1.0 - ig))
        dpre = jnp.concatenate([dpre_r, dpre_i], axis=1)
        dbax_ref[...] += jnp.sum(dpre, axis=0, keepdims=True)
        dpre_b = dpre.astype(BF16)
        dwax_ref[...] += _dot_tn(xc.astype(BF16), dpre_b)
        dxc = g * mult * ig + _dot(dpre_b, waxt_ref[...])
        dcb_ref[...] += jnp.sum(dxc, axis=0, keepdims=True)
        for k in range(4):
            xs = xa_ext[HALO:] if k == 3 else pltpu.roll(xa_ext, 3 - k, 0)[HALO:]
            dcw_ref[k:k + 1, :] += jnp.sum(dxc * xs, axis=0, keepdims=True)
        dxc_ext = jnp.concatenate([dxc, xcfut_ref[...]], axis=0)
        n = tb + HALO
        dxa = cw_ref[3:4, :] * dxc_ext
        for k in range(1, 4):
            dxa = dxa + cw_ref[3 - k:4 - k, :] * pltpu.roll(dxc_ext, n - k, 0)
        dp_ref[:, 512:1024] = dxa[:tb].astype(BF16)
        xcfut_ref[...] = dxc[0:HALO]

        xp_ext = jnp.concatenate([xph_ref[...] * keep, p_ref[:, 1024:1280]], axis=0)
        d, cnt = _pool_diff(xp_ext, bi * tb, tb)
        db = d.astype(BF16)
        dyb = dy_ref[:, 512:768]
        dps_ref[...] += jnp.sum(dyb * _dot(db, wp_ref[...]), axis=0, keepdims=True)
        dq = (dyb * ps_ref[...]).astype(BF16)
        dwp_ref[...] += _dot_tn(db, dq)
        dd = _dot(dq, wpt_ref[...])
        mm = dd / cnt
        m_ext = jnp.concatenate([mm, mfut_ref[...]], axis=0)
        f2 = m_ext + pltpu.roll(m_ext, n - 1, 0)
        f4 = f2 + pltpu.roll(f2, n - 2, 0)
        f8 = f4 + pltpu.roll(f4, n - 4, 0)
        f16 = f8 + pltpu.roll(f8, n - 8, 0)
        dp_ref[:, 1024:1280] = (_pool_select(f2, f4, f8, f16)[:tb] - dd).astype(BF16)
        mfut_ref[...] = mm[0:HALO]

        ug, dug = _gelu_and_grad(p_ref[:, 1280:1536])
        vg, dvg = _gelu_and_grad(p_ref[:, 1536:1792])
        rv = lax.rsqrt(jnp.mean(vg * vg, axis=-1, keepdims=True) + EPS)
        vhat = vg * rv
        vn = (vhat * sgn_ref[...]).astype(BF16)
        dyc = dy_ref[:, 768:1024]
        masks = _head_masks()
        dz = dyc * ug
        dzb = dz.astype(BF16)
        dvn_parts = []
        for ci in range(tb // CHUNK):
            sl = slice(ci * CHUNK, (ci + 1) * CHUNK)
            z = _sgu_mix(ws_ref, vn[sl], masks) + bz_ref[...]
            dp_ref[sl, 1280:1536] = (dyc[sl] * z * dug[sl]).astype(BF16)
            dbz_ref[...] += dz[sl]
            for h in range(4):
                dws_ref[h] += _dot_nt((dz[sl] * masks[h]).astype(BF16), vn[sl])
            dvn_parts.append(_sgu_mix(wst_ref, dzb[sl], masks))
        dvn = jnp.concatenate(dvn_parts, axis=0)
        dsgn_ref[...] += jnp.sum(dvn * vhat, axis=0, keepdims=True)
        dvhat = dvn * sgn_ref[...]
        dvg_in = rv * (dvhat - vhat * jnp.mean(dvhat * vhat, axis=-1, keepdims=True))
        dp_ref[:, 1536:1792] = (dvg_in * dvg).astype(BF16)

        @pl.when(i == nb - 1)
        def _():
            tril = (lax.broadcasted_iota(jnp.int32, (CHUNK, CHUNK), 0)
                    >= lax.broadcasted_iota(jnp.int32, (CHUNK, CHUNK), 1)).astype(F32)
            for h in range(4):
                dws_ref[h] = dws_ref[h] * tril

    def blk(i):
        return nb - 1 - i

    def halo(i):
        return jnp.maximum(blk(i) * hb - 1, 0)

    def full(shape):
        return pl.BlockSpec(shape, lambda i: (0,) * len(shape))

    small_shapes = [(4, D_RNN), (1, D_RNN), (D_RNN, 2 * D_RNN), (1, 2 * D_RNN), (1, D_RNN), (D_POOL, D_POOL),
                    (1, D_POOL), (1, D_SGU), (4, CHUNK, CHUNK), (CHUNK, D_SGU)]
    outs = pl.pallas_call(
        body, name="mix_bwd", grid=(nb,),
        in_specs=[pl.BlockSpec((tb, D), lambda i: (blk(i), 0)),
                  pl.BlockSpec((tb, D_IN), lambda i: (blk(i), 0)),
                  pl.BlockSpec((HALO, D_RNN), lambda i: (halo(i), 1)),
                  pl.BlockSpec((HALO, D_POOL), lambda i: (halo(i), 4)),
                  pl.BlockSpec((tb, D_RNN), lambda i: (blk(i), 0)),
                  pl.BlockSpec((HALO, D_RNN), lambda i: (halo(i), 0)),
                  full((4, D_RNN)), full((1, D_RNN)), full((D_RNN, 2 * D_RNN)), full((2 * D_RNN, D_RNN)),
                  full((1, 2 * D_RNN)), full((1, D_RNN)), full((D_POOL, D_POOL)), full((D_POOL, D_POOL)),
                  full((1, D_POOL)), full((1, D_SGU)), full((4, CHUNK, CHUNK)), full((4, CHUNK, CHUNK)),
                  full((CHUNK, D_SGU))],
        out_specs=[pl.BlockSpec((tb, D_IN), lambda i: (blk(i), 0))] + [full(s) for s in small_shapes],
        out_shape=[jax.ShapeDtypeStruct((T, D_IN), BF16)] + [jax.ShapeDtypeStruct(s, F32) for s in small_shapes],
        scratch_shapes=[pltpu.VMEM((8, D_RNN), F32), pltpu.VMEM((HALO, D_RNN), F32),
                        pltpu.VMEM((HALO, D_POOL), F32)],
        compiler_params=_params("arbitrary"),
    )(dy, p, p, p, hs, hs, prm["conv_w"], prm["conv_b"], prm["wax"], prm["wax_t"], prm["bax"], prm["lam"],
      prm["wpool"], prm["wpool_t"], prm["pool_scale"], prm["sgu_norm"], prm["ws"], prm["ws_t"], prm["bz"])
    names = ("dp", "conv_w", "conv_b", "wax", "bax", "lam", "wpool", "pool_scale", "sgu_norm", "ws", "bz")
    return dict(zip(names, outs))


ANY = pl.BlockSpec(memory_space=pl.ANY)


def _place():
    x, y, c = lax.axis_index("x"), lax.axis_index("y"), lax.axis_index("c")
    return x, y, c


def _all_to_all(xs, name):
    def body(in_ref, out_ref, send_sems, recv_sems, local_sem):
        x, y, c = _place()
        me = 4 * x + 2 * y + c
        mine = pltpu.make_async_copy(in_ref.at[me], out_ref.at[me], local_sem)
        mine.start()
        copies = []
        for rel in range(1, N_DEV):
            tx = 1 - x if rel & 4 else x
            ty = 1 - y if rel & 2 else y
            tc = 1 - c if rel & 1 else c
            cp = pltpu.make_async_remote_copy(
                src_ref=in_ref.at[4 * tx + 2 * ty + tc], dst_ref=out_ref.at[me],
                send_sem=send_sems.at[rel - 1], recv_sem=recv_sems.at[rel - 1],
                device_id=(tx, ty, tc), device_id_type=MESH)
            cp.start()
            copies.append(cp)
        for cp in copies:
            cp.wait()
        mine.wait()

    return pl.pallas_call(
        body, name=name, in_specs=[ANY], out_specs=ANY,
        out_shape=jax.ShapeDtypeStruct(xs.shape, xs.dtype),
        scratch_shapes=[pltpu.SemaphoreType.DMA((N_DEV - 1,)), pltpu.SemaphoreType.DMA((N_DEV - 1,)),
                        pltpu.SemaphoreType.DMA],
    )(xs)


def _all_gather8(xs, name):
    def body(x_ref, out_ref, send_sems, recv_sems, local_sem):
        x, y, c = _place()
        me, sibling = (x, y, c), (x, y, 1 - c)
        chips = [(1 - x, y), (x, 1 - y), (1 - x, 1 - y)]

        def rows(px, py, pc):
            return out_ref.at[4 * px + 2 * py + pc]

        def copy(k, block, to, src=None):
            return pltpu.make_async_remote_copy(
                src_ref=rows(*block) if src is None else src, dst_ref=rows(*block),
                send_sem=send_sems.at[k], recv_sem=recv_sems.at[k], device_id=to, device_id_type=MESH)

        mine = pltpu.make_async_copy(x_ref, rows(*me), local_sem)
        mine.start()
        first = [copy(0, me, sibling, src=x_ref)]
        first += [copy(1 + j, me, (*chip, c), src=x_ref) for j, chip in enumerate(chips)]
        for cp in first:
            cp.start()
        passed = [copy(4 + j, (*chip, c), sibling) for j, chip in enumerate(chips)]
        for j, chip in enumerate(chips):
            copy(1 + j, (*chip, c), me).wait_recv()
            passed[j].start()
        copy(0, sibling, me).wait_recv()
        for j, chip in enumerate(chips):
            copy(4 + j, (*chip, 1 - c), me).wait_recv()
        for cp in first + passed:
            cp.wait_send()
        mine.wait()

    return pl.pallas_call(
        body, name=name, in_specs=[ANY], out_specs=ANY,
        out_shape=jax.ShapeDtypeStruct((N_DEV,) + xs.shape, xs.dtype),
        scratch_shapes=[pltpu.SemaphoreType.DMA((7,)), pltpu.SemaphoreType.DMA((7,)), pltpu.SemaphoreType.DMA],
    )(xs)


HBM = pl.BlockSpec(memory_space=pltpu.HBM)
SEM = pl.BlockSpec(memory_space=pltpu.SEMAPHORE)
EFFECT = pltpu.SideEffectType.DATAFLOW_SIDE_EFFECTING


def _in_hbm(a):
    return pltpu.with_memory_space_constraint(a, pltpu.HBM)


def _local_copy(src, dst, stage, sem):
    load = pltpu.make_async_copy(src, stage, sem)
    load.start()
    load.wait()
    store = pltpu.make_async_copy(stage, dst, sem)
    store.start()
    store.wait()


def _unique(windows):
    arrays = []
    for per_chip in windows:
        for arr, _ in per_chip:
            if not any(arr is a for a in arrays):
                arrays.append(arr)
    return arrays


def _exchange_start(layer, windows, lands, name):
    arrays = _unique(windows)
    na, nt = len(arrays), len(windows)

    def body(*refs):
        in_refs, land_refs = refs[:na], refs[na:na + nt]
        send_sems, recv_sems = refs[na + nt], refs[na + nt + 1]
        token = refs[-1]
        x, y, c = _place()
        me = 4 * x + 2 * y + c
        for t in range(nt):
            for j in range(N_CHIP):
                arr, window = windows[t][j]
                src = window(in_refs[next(i for i, a in enumerate(arrays) if a is arr)])

                @pl.when(me != 2 * j + layer)
                def _():
                    pltpu.make_async_remote_copy(
                        src_ref=src, dst_ref=land_refs[t].at[me], send_sem=send_sems.at[N_CHIP * t + j],
                        recv_sem=recv_sems.at[N_DEV * t + me], device_id=(j // 2, j % 2, layer),
                        device_id_type=MESH).start()
        token[...] = jnp.zeros_like(token)

    outs = pl.pallas_call(
        body, name=name,
        out_shape=(pltpu.SemaphoreType.DMA((N_CHIP * nt,)), pltpu.SemaphoreType.DMA((N_DEV * nt,)),
                   *[pltpu.HBM(a.shape, a.dtype) for a in lands], jax.ShapeDtypeStruct((8, 128), F32)),
        in_specs=[HBM] * (na + nt),
        out_specs=(SEM, SEM, *[HBM] * nt, pl.BlockSpec(memory_space=pltpu.VMEM)),
        input_output_aliases={na + t: 2 + t for t in range(nt)},
        compiler_params=pltpu.CompilerParams(has_side_effects=EFFECT, vmem_limit_bytes=VMEM_LIMIT),
    )(*[_in_hbm(a) for a in arrays], *[_in_hbm(a) for a in lands])
    return outs[0], outs[1], list(outs[2:2 + nt]), outs[-1]


def _exchange_wait(layer, windows, lands, send_sems, recv_sems, after, name):
    arrays = _unique(windows)
    na, nt = len(arrays), len(windows)

    def body(*refs):
        in_refs, land_refs = refs[:na], refs[na:na + nt]
        send_sems, recv_sems = refs[na + nt], refs[na + nt + 1]
        stages, local_sem = refs[-1 - nt:-1], refs[-1]
        x, y, c = _place()
        me = 4 * x + 2 * y + c

        def source(t, j):
            arr, window = windows[t][j]
            return window(in_refs[next(i for i, a in enumerate(arrays) if a is arr)])

        @pl.when(c == layer)
        def _():
            for t in range(nt):
                for j in range(N_CHIP):
                    @pl.when(me == 2 * j + layer)
                    def _():
                        _local_copy(source(t, j), land_refs[t].at[me], stages[t], local_sem)

        for t in range(nt):
            for j in range(N_CHIP):
                @pl.when(me != 2 * j + layer)
                def _():
                    pltpu.make_async_remote_copy(
                        src_ref=source(t, j), dst_ref=land_refs[t].at[me], send_sem=send_sems.at[N_CHIP * t + j],
                        recv_sem=recv_sems.at[N_DEV * t + me], device_id=(j // 2, j % 2, layer),
                        device_id_type=MESH).wait_send()

        @pl.when(c == layer)
        def _():
            for t in range(nt):
                for s in range(N_DEV):
                    @pl.when(me != s)
                    def _():
                        slot = land_refs[t].at[s]
                        pltpu.make_async_remote_copy(
                            src_ref=slot, dst_ref=slot, send_sem=send_sems.at[N_CHIP * t],
                            recv_sem=recv_sems.at[N_DEV * t + s], device_id=(x, y, c),
                            device_id_type=MESH).wait_recv()

    outs = pl.pallas_call(
        body, name=name,
        out_shape=tuple(pltpu.HBM(a.shape, a.dtype) for a in lands),
        in_specs=[HBM] * (na + nt) + [SEM, SEM, ANY],
        out_specs=tuple([HBM] * nt),
        input_output_aliases={na + t: t for t in range(nt)},
        scratch_shapes=[pltpu.VMEM(a.shape[1:], a.dtype) for a in lands] + [pltpu.SemaphoreType.DMA],
        compiler_params=pltpu.CompilerParams(has_side_effects=EFFECT, vmem_limit_bytes=VMEM_LIMIT),
    )(*[_in_hbm(a) for a in arrays], *lands, send_sems, recv_sems, after)
    return list(outs)


def _other_chips(x, y):
    return [(1 - x, y), (x, 1 - y), (1 - x, 1 - y)]


def _gather_start(layer, shards, lands, name):
    nt = len(shards)

    def body(*refs):
        in_refs, land_refs = refs[:nt], refs[nt:2 * nt]
        send_sems, recv_sems = refs[2 * nt], refs[2 * nt + 1]
        token = refs[-1]
        x, y, c = _place()

        @pl.when(c == layer)
        def _():
            for t in range(nt):
                for rel, (tx, ty) in enumerate(_other_chips(x, y)):
                    for tc in range(2):
                        pltpu.make_async_remote_copy(
                            src_ref=in_refs[t], dst_ref=land_refs[t].at[2 * x + y],
                            send_sem=send_sems.at[6 * t + 2 * rel + tc], recv_sem=recv_sems.at[3 * t + rel],
                            device_id=(tx, ty, tc), device_id_type=MESH).start()

        token[...] = jnp.zeros_like(token)

    outs = pl.pallas_call(
        body, name=name,
        out_shape=(pltpu.SemaphoreType.DMA((6 * nt,)), pltpu.SemaphoreType.DMA((3 * nt,)),
                   *[pltpu.HBM(a.shape, a.dtype) for a in lands], jax.ShapeDtypeStruct((8, 128), F32)),
        in_specs=[HBM] * (2 * nt),
        out_specs=(SEM, SEM, *[HBM] * nt, pl.BlockSpec(memory_space=pltpu.VMEM)),
        input_output_aliases={nt + t: 2 + t for t in range(nt)},
        compiler_params=pltpu.CompilerParams(has_side_effects=EFFECT, vmem_limit_bytes=VMEM_LIMIT),
    )(*[_in_hbm(a) for a in shards], *[_in_hbm(a) for a in lands])
    return outs[0], outs[1], list(outs[2:2 + nt]), outs[-1]


def _gather_wait(layer, shards, lands, send_sems, recv_sems, after, name):
    nt = len(shards)

    def body(*refs):
        in_refs, land_refs = refs[:nt], refs[nt:2 * nt]
        send_sems, recv_sems = refs[2 * nt], refs[2 * nt + 1]
        stages, local_sem = refs[-1 - nt:-1], refs[-1]
        x, y, c = _place()

        for t in range(nt):
            _local_copy(in_refs[t], land_refs[t].at[2 * x + y], stages[t], local_sem)

        @pl.when(c == layer)
        def _():
            for t in range(nt):
                for rel, (tx, ty) in enumerate(_other_chips(x, y)):
                    for tc in range(2):
                        pltpu.make_async_remote_copy(
                            src_ref=in_refs[t], dst_ref=land_refs[t].at[2 * x + y],
                            send_sem=send_sems.at[6 * t + 2 * rel + tc], recv_sem=recv_sems.at[3 * t + rel],
                            device_id=(tx, ty, tc), device_id_type=MESH).wait_send()

        for t in range(nt):
            for rel, (tx, ty) in enumerate(_other_chips(x, y)):
                slot = land_refs[t].at[2 * tx + ty]
                pltpu.make_async_remote_copy(
                    src_ref=slot, dst_ref=slot, send_sem=send_sems.at[6 * t], recv_sem=recv_sems.at[3 * t + rel],
                    device_id=(x, y, c), device_id_type=MESH).wait_recv()

    outs = pl.pallas_call(
        body, name=name,
        out_shape=tuple(pltpu.HBM(a.shape, a.dtype) for a in lands),
        in_specs=[HBM] * (2 * nt) + [SEM, SEM, ANY],
        out_specs=tuple([HBM] * nt),
        input_output_aliases={nt + t: t for t in range(nt)},
        scratch_shapes=[pltpu.VMEM(a.shape, a.dtype) for a in shards] + [pltpu.SemaphoreType.DMA],
        compiler_params=pltpu.CompilerParams(has_side_effects=EFFECT, vmem_limit_bytes=VMEM_LIMIT),
    )(*[_in_hbm(a) for a in shards], *lands, send_sems, recv_sems, after)
    return list(outs)


def _tie(a, token):
    def body(a_ref, token_ref, o_ref):
        pass

    return pl.pallas_call(
        body, name="tie", in_specs=[ANY, ANY], out_specs=ANY,
        out_shape=jax.ShapeDtypeStruct(a.shape, a.dtype), input_output_aliases={0: 0},
    )(a, token)


def _sum_share(xs, name):
    _, r, cols = xs.shape
    tr = 256 if r % 256 == 0 else r
    nblk = r // tr

    def body(x_ref, out_ref, acc_ref, send_sems, local_sems, recv_sem):
        i = pl.program_id(0)
        slot = i % 2
        x, y, c = _place()

        def copies(s, blk):
            dst = out_ref.at[c, pl.ds(blk * tr, tr), :]
            loc = pltpu.make_async_copy(acc_ref.at[s], dst, local_sems.at[s])
            rem = pltpu.make_async_remote_copy(src_ref=acc_ref.at[s], dst_ref=dst, send_sem=send_sems.at[s],
                                               recv_sem=recv_sem, device_id=(x, y, 1 - c), device_id_type=MESH)
            return loc, rem

        @pl.when(i >= 2)
        def _():
            loc, rem = copies(slot, i - 2)
            loc.wait()
            rem.wait_send()

        acc = x_ref[0].astype(F32)
        for k in range(1, N_DEV):
            acc = acc + x_ref[k].astype(F32)
        acc_ref[slot] = acc
        loc, rem = copies(slot, i)
        loc.start()
        rem.start()

        @pl.when(i == nblk - 1)
        def _():
            for back in range(min(2, nblk)):
                blk = nblk - 1 - back
                loc, rem = copies(blk % 2, blk)
                loc.wait()
                rem.wait_send()
            theirs = out_ref.at[1 - c]
            pltpu.make_async_remote_copy(src_ref=theirs, dst_ref=theirs, send_sem=send_sems.at[0],
                                         recv_sem=recv_sem, device_id=(x, y, 1 - c),
                                         device_id_type=MESH).wait_recv()

    return pl.pallas_call(
        body, name=name, grid=(nblk,),
        in_specs=[pl.BlockSpec((N_DEV, tr, cols), lambda i: (0, i, 0))],
        out_specs=ANY,
        out_shape=jax.ShapeDtypeStruct((2, r, cols), F32),
        scratch_shapes=[pltpu.VMEM((2, tr, cols), F32), pltpu.SemaphoreType.DMA((2,)),
                        pltpu.SemaphoreType.DMA((2,)), pltpu.SemaphoreType.DMA],
        compiler_params=_params("arbitrary"),
    )(xs)


def _sum8(xs, name):
    _, r, cols = xs.shape
    tr = 8
    for cand in (256, 128, 64, 32, 16):
        if r % cand == 0:
            tr = cand
            break

    def body(x_ref, o_ref):
        acc = x_ref[0].astype(F32)
        for k in range(1, N_DEV):
            acc = acc + x_ref[k].astype(F32)
        o_ref[...] = acc

    return pl.pallas_call(
        body, name=name, grid=(r // tr,),
        in_specs=[pl.BlockSpec((N_DEV, tr, cols), lambda i: (0, i, 0))],
        out_specs=pl.BlockSpec((tr, cols), lambda i: (i, 0)),
        out_shape=jax.ShapeDtypeStruct((r, cols), F32),
        compiler_params=_params("parallel"),
    )(xs)


def _adamw(w, g, m, v, name):
    R, C = w.shape
    tr = R
    for cand in (256, 128, 64, 32, 16, 8):
        if R % cand == 0:
            tr = cand
            break
    c1 = 1.0 / (1.0 - ADAM_B1 ** ADAM_STEP)
    c2 = 1.0 / (1.0 - ADAM_B2 ** ADAM_STEP)

    def body(w_ref, g_ref, m_ref, v_ref, d_ref, nm_ref, nv_ref):
        gv = g_ref[...]
        nm = ADAM_B1 * m_ref[...] + (1.0 - ADAM_B1) * gv
        nv = ADAM_B2 * v_ref[...] + (1.0 - ADAM_B2) * (gv * gv)
        d_ref[...] = (-ADAM_LR) * ((nm * c1) / (jnp.sqrt(nv * c2) + ADAM_EPS) + ADAM_WD * w_ref[...])
        nm_ref[...] = nm
        nv_ref[...] = nv

    spec = pl.BlockSpec((tr, C), lambda i: (i, 0))
    shape = jax.ShapeDtypeStruct((R, C), F32)
    return pl.pallas_call(
        body, name=name, grid=(R // tr,), in_specs=[spec] * 4, out_specs=[spec] * 3, out_shape=[shape] * 3,
        compiler_params=_params("parallel"),
    )(w, g, m, v)


def _flat_rows(parts, rows):
    flat = jnp.concatenate([q.reshape(-1) for q in parts])
    flat = jnp.pad(flat, (0, rows * LANES - flat.shape[0]))
    return flat.reshape(rows, LANES)


def _round_up(n, m):
    return (n + m - 1) // m * m


def _block_diag(w):
    H, n, _ = w.shape
    eye = jnp.eye(H, dtype=w.dtype)
    return (eye[:, None, :, None] * w[:, :, None, :]).reshape(H * n, H * n)


def _diag_blocks(w, H, n):
    w4 = w.reshape(H, n, H, n)
    return jnp.stack([w4[h, :, h, :] for h in range(H)])


FF_SHARD = D_FF // 2
FF_HALF = D_FFP // 2
FF_ROWS = D_FF // N_CHIP


def _ffn_in_weights(g_in):
    pad = lambda a: jnp.pad(a, ((0, 0), (0, FF_HALF - FF_SHARD)))
    wg = jnp.concatenate([pad(g_in[0]), pad(g_in[1])], axis=1)
    wu = jnp.concatenate([pad(g_in[2]), pad(g_in[3])], axis=1)
    return wg, wu


def _ffn_out_weights(g_out):
    zeros = jnp.zeros((FF_HALF - FF_SHARD, D), g_out.dtype)
    return jnp.concatenate([g_out[0], g_out[1], zeros, g_out[2], g_out[3], zeros], axis=0)


LAND_SHAPES = {"ffn1_w_in": (D, FF_HALF), "ffn1_w_out": (FF_ROWS, D), "w_in": (D, D_IN // N_CHIP),
               "w_out": (D // N_CHIP, D), "ffn2_w_in": (D, FF_HALF), "ffn2_w_out": (FF_ROWS, D)}


def _ffn_grad_windows(dwg, dwu, dwout):
    w_in, w_out = [], []
    for j in range(N_CHIP):
        col = (j % 2) * FF_HALF
        row = (j // 2) * FF_HALF + (j % 2) * FF_ROWS
        w_in.append((dwg if j < 2 else dwu, lambda r, col=col: r.at[:, pl.ds(col, FF_HALF)]))
        w_out.append((dwout, lambda r, row=row: r.at[pl.ds(row, FF_ROWS), :]))
    return w_in, w_out


def _mix_grad_windows(dwin, dwo):
    rows = D // N_CHIP
    win = [(dwin, lambda r, j=j: r.at[j]) for j in range(N_CHIP)]
    wo = [(dwo, lambda r, j=j: r.at[pl.ds(j * rows, rows), :]) for j in range(N_CHIP)]
    return win, wo


def kernel(x, ffn1_norm, ffn1_w_in, ffn1_w_out, mix_norm, w_in, conv_w, conv_b, rg_w_a, rg_b_a, rg_w_x, rg_b_x, lru_lambda, pool_w, pool_scale, sgu_norm, sgu_w, sgu_b, w_out, ffn2_norm, ffn2_w_in, ffn2_w_out, final_norm, loss_target, m_ffn1_norm, m_ffn1_w_in, m_ffn1_w_out, m_mix_norm, m_w_in, m_conv_w, m_conv_b, m_rg_w_a, m_rg_b_a, m_rg_w_x, m_rg_b_x, m_lru_lambda, m_pool_w, m_pool_scale, m_sgu_norm, m_sgu_w, m_sgu_b, m_w_out, m_ffn2_norm, m_ffn2_w_in, m_ffn2_w_out, m_final_norm, v_ffn1_norm, v_ffn1_w_in, v_ffn1_w_out, v_mix_norm, v_w_in, v_conv_w, v_conv_b, v_rg_w_a, v_rg_b_a, v_rg_w_x, v_rg_b_x, v_lru_lambda, v_pool_w, v_pool_scale, v_sgu_norm, v_sgu_w, v_sgu_b, v_w_out, v_ffn2_norm, v_ffn2_w_in, v_ffn2_w_out, v_final_norm):
    args = locals()
    W = {n: args[n] for n in WEIGHTS}
    M = {n: args["m_" + n] for n in WEIGHTS}
    V = {n: args["v_" + n] for n in WEIGHTS}
    depth = ffn1_norm.shape[0]
    T = x.shape[1]
    xi, yi, ci = _place()
    chip = 2 * xi + yi

    assert depth == 2, "core c of a chip sends and reduces layer c"
    groups = [(l, names) for l in range(depth)
              for names in (["ffn1_w_in"], ["ffn1_w_out", "w_in", "w_out"], ["ffn2_w_in", "ffn2_w_out"])]
    wb = {n: W[n].astype(BF16) for n in BIG}
    groups[1][1].append("conv_w")
    conv_shard = conv_w.reshape(-1, conv_w.shape[-1])
    flights = {}

    def weights_start(k, dep=None):
        l, names = groups[k]
        shards = [conv_shard if n == "conv_w" else wb[n][l] for n in names]
        if dep is not None:
            shards[0] = _tie(shards[0], dep)
        lands = [lax.empty((N_CHIP,) + s.shape, s.dtype) for s in shards]
        send, recv, lands, token = _gather_start(l, shards, lands, "weights_start_%d" % k)
        flights[k] = (shards, lands, send, recv)
        return token

    def weights_wait(k, after):
        l, names = groups[k]
        shards, lands, send, recv = flights[k]
        got = _gather_wait(l, shards, lands, send, recv, after, "weights_wait_%d" % k)
        token = weights_start(k + 2, got[0]) if k + 2 < len(groups) else None
        return dict(zip(names, got)), token

    def after_start(a, token):
        return a if token is None else _tie(a, token)

    first_tokens = [weights_start(0), weights_start(1)]

    layers = []
    for l in range(depth):
        L = {f: dict(norm=W[f + "_norm"][l][None]) for f in ("ffn1", "ffn2")}
        ws = jnp.where(jnp.tril(jnp.ones((CHUNK, CHUNK), bool))[None], sgu_w[l], 0.0)
        wax = jnp.concatenate([_block_diag(rg_w_a[l]), _block_diag(rg_w_x[l])], axis=1)
        wpool = _block_diag(pool_w[l])
        L["mix"] = dict(
            conv_b=conv_b[l][None], wax=wax.astype(BF16), wax_t=wax.T.astype(BF16),
            bax=jnp.concatenate([rg_b_a[l].reshape(-1), rg_b_x[l].reshape(-1)])[None], lam=lru_lambda[l][None],
            wpool=wpool.astype(BF16), wpool_t=wpool.T.astype(BF16), pool_scale=pool_scale[l][None],
            sgu_norm=sgu_norm[l][None], ws=ws.astype(BF16), ws_t=jnp.swapaxes(ws, 1, 2).astype(BF16),
            bz=jnp.repeat(sgu_b[l].T, 64, axis=1))
        L["mix_norm"] = mix_norm[l][None]
        layers.append(L)
    for token in first_tokens:
        layers[0]["ffn1"]["norm"] = _tie(layers[0]["ffn1"]["norm"], token)

    xs = x[0]
    saved = []
    for l, L in enumerate(layers):
        F1, F2 = L["ffn1"], L["ffn2"]
        h = _rms_fwd(xs, F1["norm"])
        got, token = weights_wait(3 * l, h)
        F1["wg"], F1["wu"] = _ffn_in_weights(got["ffn1_w_in"])
        h = after_start(h, token)
        g, u, a = _ffn_in(h, F1["wg"], F1["wu"])
        got, token = weights_wait(3 * l + 1, a)
        F1["wout"] = _ffn_out_weights(got["ffn1_w_out"])
        L["w_in"] = jnp.concatenate([got["w_in"][j] for j in range(N_CHIP)], axis=1)
        L["w_out"] = jnp.concatenate([got["w_out"][j] for j in range(N_CHIP)], axis=0)
        if "conv_w" in got:
            conv_full = jnp.concatenate([got["conv_w"][j] for j in range(N_CHIP)], axis=1)
            for ll in range(depth):
                layers[ll]["mix"]["conv_w"] = conv_full.reshape(depth, 4, D_RNN)[ll]
        a = after_start(a, token)
        x1 = _mm_res(a, F1["wout"], xs, 0.5, "ffn_out")
        s1 = (xs, h, g, u, a)
        hm = _rms_fwd(x1, L["mix_norm"])
        p = _mm(hm, L["w_in"], F32, "mix_in")
        ycat, hs = _mix_fwd(p, L["mix"])
        x2 = _mm_res(ycat, L["w_out"], x1, 1.0, "mix_out")
        h2 = _rms_fwd(x2, F2["norm"])
        got, token = weights_wait(3 * l + 2, h2)
        F2["wg"], F2["wu"] = _ffn_in_weights(got["ffn2_w_in"])
        F2["wout"] = _ffn_out_weights(got["ffn2_w_out"])
        h2 = after_start(h2, token)
        g2, u2, a2 = _ffn_in(h2, F2["wg"], F2["wu"])
        x3 = _mm_res(a2, F2["wout"], x2, 0.5, "ffn_out")
        saved.append((s1, (x1, hm, p, ycat, hs), (x2, h2, g2, u2, a2)))
        xs = x3

    dx, dxb, d_final, loss_part = _final(xs, loss_target[0], final_norm[None])

    G = {n: [None] * depth for n in SMALL if n != "final_norm"}
    lands = {n: lax.empty((N_DEV,) + LAND_SHAPES[n], BF16) for n in BIG}
    in_flight = []

    def send_grads(l, windows, tag):
        names = list(windows)
        send, recv, thru, token = _exchange_start(l, [windows[n] for n in names], [lands[n] for n in names],
                                                  "grads_start_" + tag)
        lands.update(zip(names, thru))
        in_flight.append((l, names, [windows[n] for n in names], send, recv, "grads_wait_" + tag))
        return token

    def ffn_bwd(dx, dxb, F, s, f, l, pending, send_now):
        xin, h, g, u, a = s
        dg, du = _ffn_mid_bwd(dxb, F["wout"], g, u)
        dwout = _mm_tn(a, dxb, 0.5, "ffn_dwout")
        dwg = _mm_tn(h, dg, 1.0, "ffn_dwg")
        dwu = _mm_tn(h, du, 1.0, "ffn_dwu")
        pending[f + "_w_in"], pending[f + "_w_out"] = _ffn_grad_windows(dwg, dwu, dwout)
        if send_now:
            dg = _tie(dg, send_grads(l, pending, "l%d_%s" % (l, f)))
        dh = _mm2_nt(dg, F["wg"], du, F["wu"], "ffn_dh")
        dx, dxb, dn = _rms_bwd(xin, dh, F["norm"], dx)
        G[f + "_norm"][l] = dn[0]
        return dx, dxb

    for l in reversed(range(depth)):
        L = layers[l]
        s1, (x1, hm, p, ycat, hs), s2 = saved[l]
        pending = {}
        dx, dxb = ffn_bwd(dx, dxb, L["ffn2"], s2, "ffn2", l, pending, l == 0)
        if l == 0:
            pending = {}
        dycat = _mm_nt(dxb, L["w_out"], "mix_dy")
        dwo = _mm_tn(ycat, dxb, 1.0, "mix_dwout")
        mg = _mix_bwd(dycat, p, hs, L["mix"])
        dwin = _mm_tn(hm, mg["dp"], 1.0, "mix_dwin")
        pending["w_in"], pending["w_out"] = _mix_grad_windows(jnp.stack(jnp.split(dwin, N_CHIP, axis=1)), dwo)
        if l == 0:
            dp = _tie(mg["dp"], send_grads(l, pending, "l0_mix"))
            pending = {}
        else:
            dp = mg["dp"]
        dhm = _mm_nt(dp, L["w_in"], "mix_dh")
        dx, dxb, dn = _rms_bwd(x1, dhm, L["mix_norm"], dx)
        G["mix_norm"][l] = dn[0]
        G["conv_w"][l], G["conv_b"][l] = mg["conv_w"], mg["conv_b"][0]
        G["rg_w_a"][l] = _diag_blocks(mg["wax"][:, :D_RNN], 8, 64)
        G["rg_w_x"][l] = _diag_blocks(mg["wax"][:, D_RNN:], 8, 64)
        G["rg_b_a"][l] = mg["bax"][0, :D_RNN].reshape(8, 64)
        G["rg_b_x"][l] = mg["bax"][0, D_RNN:].reshape(8, 64)
        G["lru_lambda"][l] = mg["lam"][0]
        G["pool_w"][l] = _diag_blocks(mg["wpool"], 4, 64)
        G["pool_scale"][l], G["sgu_norm"][l] = mg["pool_scale"][0], mg["sgu_norm"][0]
        G["sgu_w"][l] = mg["ws"]
        G["sgu_b"][l] = mg["bz"].reshape(CHUNK, 4, 64).sum(-1).T
        dx, dxb = ffn_bwd(dx, dxb, L["ffn1"], s1, "ffn1", l, pending, l == 0)
        if l > 0:
            dxb = _tie(dxb, send_grads(l, pending, "l%d" % l))
    grad_x = dx[None]
    G = {n: jnp.stack(v) for n, v in G.items()}
    G["final_norm"] = d_final[0]

    for l, names, windows, send, recv, tag in in_flight:
        lands.update(zip(names, _exchange_wait(l, windows, [lands[n] for n in names], send, recv, dx, tag)))
    both = [_sum_share(lands[n], "sum_share_" + n) for n in BIG]
    grads = {n: (a[:, :, :FF_SHARD] if n.endswith("w_in") and n != "w_in" else a) for n, a in zip(BIG, both)}

    small_sizes = [int(np.prod(G[n].shape)) for n in SMALL]
    srows = _round_up(sum(small_sizes) + 1, N_DEV * 8 * LANES) // (N_DEV * LANES)
    sflat = _flat_rows([G[n] for n in SMALL] + [loss_part[0, :1]], N_DEV * srows)
    sgot = _all_to_all(sflat.reshape(N_DEV, srows, LANES), "exchange_small_grads")
    sall = _all_gather8(_sum8(sgot, "sum_small_grads"), "share_small_grads").reshape(-1)
    off = 0
    for n, size in zip(SMALL, small_sizes):
        grads[n] = sall[off:off + size].reshape(G[n].shape)
        off += size
    loss = sall[off]
    grads["conv_w"] = lax.dynamic_slice_in_dim(grads["conv_w"], chip * conv_w.shape[2], conv_w.shape[2], axis=2)

    delta, new_m, new_v = {}, {}, {}
    for n in BIG:
        shp = W[n].shape
        two_d = (shp[0] * shp[1], shp[2])
        outs = _adamw(W[n].reshape(two_d), grads[n].reshape(two_d), M[n].reshape(two_d), V[n].reshape(two_d),
                      "adamw_" + n)
        delta[n], new_m[n], new_v[n] = (o.reshape(shp) for o in outs)
    arows = _round_up(sum(int(np.prod(W[n].shape)) for n in SMALL), 8 * LANES) // LANES
    outs = _adamw(*(_flat_rows([src[n] for n in SMALL], arows) for src in (W, grads, M, V)), "adamw_small")
    outs = [o.reshape(-1) for o in outs]
    off = 0
    for n in SMALL:
        size = int(np.prod(W[n].shape))
        delta[n], new_m[n], new_v[n] = (o[off:off + size].reshape(W[n].shape) for o in outs)
        off += size

    return (loss, grad_x, *[grads[n] for n in WEIGHTS], *[delta[n] for n in WEIGHTS],
            *[new_m[n] for n in WEIGHTS], *[new_v[n] for n in WEIGHTS])
```

```python
import math

import jax
import jax.numpy as jnp
import numpy as np
from jax import lax
from jax.experimental import pallas as pl
from jax.experimental.pallas import tpu as pltpu

F32 = jnp.float32
BF16 = jnp.bfloat16
MESH = pl.DeviceIdType.MESH

D = 1024
D_RNN = 512
D_POOL = 256
D_SGU = 256
D_IN = 1792
D_FF = 2752
D_FFP = 2816
N_CHIP = 4
FF_SHARD = D_FF // 2
FF_HALF = D_FFP // 2
FF_ROWS = D_FF // N_CHIP
CHUNK = 128
HALO = 16
EPS = 1e-6
LRU_C = 8.0
N_DEV = 8
LANES = 1024
VMEM_LIMIT = 56 * 1024 * 1024

ADAM_LR, ADAM_B1, ADAM_B2, ADAM_EPS, ADAM_WD, ADAM_STEP = 0.001, 0.9, 0.999, 1e-08, 0.01, 10

BIG = ("ffn1_w_in", "ffn1_w_out", "w_in", "w_out", "ffn2_w_in", "ffn2_w_out")
SMALL = ("ffn1_norm", "mix_norm", "conv_w", "conv_b", "rg_w_a", "rg_b_a", "rg_w_x", "rg_b_x", "lru_lambda",
         "pool_w", "pool_scale", "sgu_norm", "sgu_w", "sgu_b", "ffn2_norm", "final_norm")
WEIGHTS = ("ffn1_norm", "ffn1_w_in", "ffn1_w_out", "mix_norm", "w_in", "conv_w", "conv_b", "rg_w_a", "rg_b_a",
           "rg_w_x", "rg_b_x", "lru_lambda", "pool_w", "pool_scale", "sgu_norm", "sgu_w", "sgu_b", "w_out",
           "ffn2_norm", "ffn2_w_in", "ffn2_w_out", "final_norm")


def _params(*sem):
    return pltpu.CompilerParams(dimension_semantics=sem, vmem_limit_bytes=VMEM_LIMIT)


def _gelu(x):
    c = math.sqrt(2.0 / math.pi)
    t = jnp.tanh(c * (x + 0.044715 * (x * x * x)))
    return 0.5 * x * (1.0 + t)


def _gelu_and_grad(x):
    c = math.sqrt(2.0 / math.pi)
    x2 = x * x
    t = jnp.tanh(c * (x + 0.044715 * (x2 * x)))
    g = 0.5 * x * (1.0 + t)
    dg = 0.5 * (1.0 + t) + 0.5 * x * (1.0 - t * t) * (c * (1.0 + 3.0 * 0.044715 * x2))
    return g, dg


def _sigmoid(x):
    return 0.5 * jnp.tanh(0.5 * x) + 0.5


def _dot(a, b):
    return jnp.dot(a, b, preferred_element_type=F32)


def _dot_tn(a, b):
    return lax.dot_general(a, b, (((0,), (0,)), ((), ())), preferred_element_type=F32)


def _dot_nt(a, b):
    return lax.dot_general(a, b, (((1,), (1,)), ((), ())), preferred_element_type=F32)


def _rms_fwd(x, g, tm=512):
    T = x.shape[0]
    tm = min(tm, T)

    def body(x_ref, g_ref, o_ref):
        xv = x_ref[...]
        r = lax.rsqrt(jnp.mean(xv * xv, axis=-1, keepdims=True) + EPS)
        o_ref[...] = (xv * r * g_ref[...]).astype(BF16)

    return pl.pallas_call(
        body, name="rms_fwd", grid=(T // tm,),
        in_specs=[pl.BlockSpec((tm, D), lambda i: (i, 0)), pl.BlockSpec((1, D), lambda i: (0, 0))],
        out_specs=pl.BlockSpec((tm, D), lambda i: (i, 0)),
        out_shape=jax.ShapeDtypeStruct((T, D), BF16),
        compiler_params=_params("parallel"),
    )(x, g)


def _rms_bwd(x, dh, g, dres, copy_scale, tm=512):
    T = x.shape[0]
    tm = min(tm, T)

    def body(x_ref, dh_ref, g_ref, dres_ref, dx_ref, dxb_ref, dg_ref):
        xv = x_ref[...]
        r = lax.rsqrt(jnp.mean(xv * xv, axis=-1, keepdims=True) + EPS)
        xhat = xv * r
        dy = dh_ref[...]
        dxhat = dy * g_ref[...]
        dx = dres_ref[...] + r * (dxhat - xhat * jnp.mean(dxhat * xhat, axis=-1, keepdims=True))
        dx_ref[...] = dx
        dxb_ref[...] = (copy_scale * dx).astype(BF16)

        @pl.when(pl.program_id(0) == 0)
        def _():
            dg_ref[...] = jnp.zeros_like(dg_ref)

        dg_ref[...] += jnp.sum(dy * xhat, axis=0, keepdims=True)

    row = pl.BlockSpec((tm, D), lambda i: (i, 0))
    vec = pl.BlockSpec((1, D), lambda i: (0, 0))
    return pl.pallas_call(
        body, name="rms_bwd", grid=(T // tm,),
        in_specs=[row, row, vec, row], out_specs=[row, row, vec],
        out_shape=[jax.ShapeDtypeStruct((T, D), F32), jax.ShapeDtypeStruct((T, D), BF16),
                   jax.ShapeDtypeStruct((1, D), F32)],
        compiler_params=_params("arbitrary"),
    )(x, dh, g, dres)


def _tile(n, limit):
    best = 128
    for t in range(128, min(n, limit) + 1, 128):
        if n % t == 0:
            best = t
    assert n % best == 0, (n, limit)
    return best


def _mm(a, b, out_dtype, name, tm=1024, tn=512):
    M, K = a.shape
    N = b.shape[1]
    tm, tn = min(tm, M), _tile(N, tn)

    def body(a_ref, b_ref, o_ref):
        o_ref[...] = _dot(a_ref[...], b_ref[...]).astype(out_dtype)

    return pl.pallas_call(
        body, name=name, grid=(M // tm, N // tn),
        in_specs=[pl.BlockSpec((tm, K), lambda i, j: (i, 0)), pl.BlockSpec((K, tn), lambda i, j: (0, j))],
        out_specs=pl.BlockSpec((tm, tn), lambda i, j: (i, j)),
        out_shape=jax.ShapeDtypeStruct((M, N), out_dtype),
        compiler_params=_params("parallel", "parallel"),
    )(a, b)


def _mm_res(a, b, res, scale, name, tm=1024, tn=512):
    M, K = a.shape
    N = b.shape[1]
    tm, tn = min(tm, M), _tile(N, tn)

    def body(a_ref, b_ref, r_ref, o_ref):
        o_ref[...] = r_ref[...] + scale * _dot(a_ref[...], b_ref[...])

    return pl.pallas_call(
        body, name=name, grid=(M // tm, N // tn),
        in_specs=[pl.BlockSpec((tm, K), lambda i, j: (i, 0)), pl.BlockSpec((K, tn), lambda i, j: (0, j)),
                  pl.BlockSpec((tm, tn), lambda i, j: (i, j))],
        out_specs=pl.BlockSpec((tm, tn), lambda i, j: (i, j)),
        out_shape=jax.ShapeDtypeStruct((M, N), F32),
        compiler_params=_params("parallel", "parallel"),
    )(a, b, res)


def _mm_nt(a, b, name, tm=1024, tn=512):
    M, K = a.shape
    N = b.shape[0]
    tm, tn = min(tm, M), _tile(N, tn)

    def body(a_ref, b_ref, o_ref):
        o_ref[...] = _dot_nt(a_ref[...], b_ref[...])

    return pl.pallas_call(
        body, name=name, grid=(M // tm, N // tn),
        in_specs=[pl.BlockSpec((tm, K), lambda i, j: (i, 0)), pl.BlockSpec((tn, K), lambda i, j: (j, 0))],
        out_specs=pl.BlockSpec((tm, tn), lambda i, j: (i, j)),
        out_shape=jax.ShapeDtypeStruct((M, N), F32),
        compiler_params=_params("parallel", "parallel"),
    )(a, b)


def _mm2_nt(a1, b1, a2, b2, name, tm=1024, tn=512):
    M, K = a1.shape
    N = b1.shape[0]
    tm, tn = min(tm, M), _tile(N, tn)

    def body(a1_ref, b1_ref, a2_ref, b2_ref, o_ref):
        o_ref[...] = _dot_nt(a1_ref[...], b1_ref[...]) + _dot_nt(a2_ref[...], b2_ref[...])

    aspec = pl.BlockSpec((tm, K), lambda i, j: (i, 0))
    bspec = pl.BlockSpec((tn, K), lambda i, j: (j, 0))
    return pl.pallas_call(
        body, name=name, grid=(M // tm, N // tn),
        in_specs=[aspec, bspec, aspec, bspec],
        out_specs=pl.BlockSpec((tm, tn), lambda i, j: (i, j)),
        out_shape=jax.ShapeDtypeStruct((M, N), F32),
        compiler_params=_params("parallel", "parallel"),
    )(a1, b1, a2, b2)


def _mm_tn(a, b, scale, name, tm=1792, tn=1792, tk=512):
    T, M = a.shape
    N = b.shape[1]
    tm, tn, tk = _tile(M, tm), _tile(N, tn), min(tk, T)
    nk = T // tk

    def body(a_ref, b_ref, o_ref, acc_ref):
        k = pl.program_id(2)

        @pl.when(k == 0)
        def _():
            acc_ref[...] = jnp.zeros_like(acc_ref)

        acc_ref[...] += _dot_tn(a_ref[...], b_ref[...])

        @pl.when(k == nk - 1)
        def _():
            o_ref[...] = (scale * acc_ref[...]).astype(BF16)

    return pl.pallas_call(
        body, name=name, grid=(M // tm, N // tn, nk),
        in_specs=[pl.BlockSpec((tk, tm), lambda i, j, k: (k, i)), pl.BlockSpec((tk, tn), lambda i, j, k: (k, j))],
        out_specs=pl.BlockSpec((tm, tn), lambda i, j, k: (i, j)),
        out_shape=jax.ShapeDtypeStruct((M, N), BF16),
        scratch_shapes=[pltpu.VMEM((tm, tn), F32)],
        compiler_params=_params("parallel", "parallel", "arbitrary"),
    )(a, b)


def _ffn_in(h, wg, wu, tm=512, tn=FF_HALF):
    T = h.shape[0]
    tm = min(tm, T)

    def body(h_ref, wg_ref, wu_ref, g_ref, u_ref, a_ref):
        hv = h_ref[...]
        g = _dot(hv, wg_ref[...])
        u = _dot(hv, wu_ref[...])
        g_ref[...] = g.astype(BF16)
        u_ref[...] = u.astype(BF16)
        a_ref[...] = (g * _sigmoid(g) * u).astype(BF16)

    wspec = pl.BlockSpec((D, tn), lambda j, i: (0, j))
    ospec = pl.BlockSpec((tm, tn), lambda j, i: (i, j))
    oshape = jax.ShapeDtypeStruct((T, D_FFP), BF16)
    return pl.pallas_call(
        body, name="ffn_in", grid=(D_FFP // tn, T // tm),
        in_specs=[pl.BlockSpec((tm, D), lambda j, i: (i, 0)), wspec, wspec],
        out_specs=[ospec, ospec, ospec], out_shape=[oshape, oshape, oshape],
        compiler_params=_params("parallel", "parallel"),
    )(h, wg, wu)


def _ffn_mid_bwd(dyh, wout, g, u, tm=512, tn=FF_HALF):
    T = dyh.shape[0]
    tm = min(tm, T)

    def body(dy_ref, w_ref, g_ref, u_ref, dg_ref, du_ref):
        da = _dot_nt(dy_ref[...], w_ref[...])
        g = g_ref[...].astype(F32)
        s = _sigmoid(g)
        gs = g * s
        du_ref[...] = (da * gs).astype(BF16)
        dg_ref[...] = ((da * u_ref[...].astype(F32)) * (s + gs - gs * s)).astype(BF16)

    ospec = pl.BlockSpec((tm, tn), lambda j, i: (i, j))
    oshape = jax.ShapeDtypeStruct((T, D_FFP), BF16)
    return pl.pallas_call(
        body, name="ffn_mid_bwd", grid=(D_FFP // tn, T // tm),
        in_specs=[pl.BlockSpec((tm, D), lambda j, i: (i, 0)), pl.BlockSpec((tn, D), lambda j, i: (j, 0)),
                  ospec, ospec],
        out_specs=[ospec, ospec], out_shape=[oshape, oshape],
        compiler_params=_params("parallel", "parallel"),
    )(dyh, wout, g, u)


def _final(x, tgt, gf, copy_scale, tm=512):
    T = x.shape[0]
    tm = min(tm, T)

    def body(x_ref, t_ref, g_ref, dx_ref, dxb_ref, dg_ref, loss_ref):
        xv = x_ref[...]
        r = lax.rsqrt(jnp.mean(xv * xv, axis=-1, keepdims=True) + EPS)
        xhat = xv * r
        err = xhat * g_ref[...] - t_ref[...]
        dy = err * (1.0 / D)
        dxhat = dy * g_ref[...]
        dx = r * (dxhat - xhat * jnp.mean(dxhat * xhat, axis=-1, keepdims=True))
        dx_ref[...] = dx
        dxb_ref[...] = (copy_scale * dx).astype(BF16)

        @pl.when(pl.program_id(0) == 0)
        def _():
            dg_ref[...] = jnp.zeros_like(dg_ref)
            loss_ref[...] = jnp.zeros_like(loss_ref)

        dg_ref[...] += jnp.sum(dy * xhat, axis=0, keepdims=True)
        loss_ref[...] += (0.5 / D) * jnp.sum(err * err)

    row = pl.BlockSpec((tm, D), lambda i: (i, 0))
    vec = pl.BlockSpec((1, D), lambda i: (0, 0))
    return pl.pallas_call(
        body, name="final_loss", grid=(T // tm,),
        in_specs=[row, row, vec],
        out_specs=[row, row, vec, pl.BlockSpec((1, 128), lambda i: (0, 0))],
        out_shape=[jax.ShapeDtypeStruct((T, D), F32), jax.ShapeDtypeStruct((T, D), BF16),
                   jax.ShapeDtypeStruct((1, D), F32), jax.ShapeDtypeStruct((1, 128), F32)],
        compiler_params=_params("arbitrary"),
    )(x, tgt, gf)


def _mix_block(T, limit):
    return min(limit, T // 2)


def _rows(tb, width):
    return lax.broadcasted_iota(jnp.int32, (tb, width), 0)


def _rglru_gates(xc, wax_ref, bax_ref, lam_ref):
    pre = _dot(xc.astype(BF16), wax_ref[...]) + bax_ref[...]
    r = _sigmoid(pre[:, :D_RNN])
    ig = _sigmoid(pre[:, D_RNN:])
    z = -lam_ref[...]
    sp = jnp.maximum(z, 0.0) + jnp.log(1.0 + jnp.exp(-jnp.abs(z)))
    log_a = (-LRU_C) * r * sp
    a = jnp.exp(log_a)
    mult = jnp.sqrt(-jnp.tanh(log_a) * (1.0 + a * a))
    return r, ig, sp, a, mult


def _conv(xa_ext, cw_ref, cb_ref):
    y = cb_ref[...] + cw_ref[3:4, :] * xa_ext
    for k in range(1, 4):
        y = y + cw_ref[3 - k:4 - k, :] * pltpu.roll(xa_ext, k, 0)
    return y[HALO:]


def _pool_window_lanes():
    lane = lax.broadcasted_iota(jnp.int32, (1, D_POOL), 1)
    return jnp.where(lane < 64, 2, jnp.where(lane < 128, 4, jnp.where(lane < 192, 8, 16)))


def _pool_select(s2, s4, s8, s16):
    lane = lax.broadcasted_iota(jnp.int32, s2.shape, 1)
    return jnp.where(lane < 64, s2, jnp.where(lane < 128, s4, jnp.where(lane < 192, s8, s16)))


def _pool_diff(xp_ext, t0, tb):
    s2 = xp_ext + pltpu.roll(xp_ext, 1, 0)
    s4 = s2 + pltpu.roll(s2, 2, 0)
    s8 = s4 + pltpu.roll(s4, 4, 0)
    s16 = s8 + pltpu.roll(s8, 8, 0)
    sel = _pool_select(s2, s4, s8, s16)[HALO:]
    cnt = jnp.minimum(t0 + _rows(tb, D_POOL) + 1, _pool_window_lanes()).astype(F32)
    return sel / cnt - xp_ext[HALO:], cnt


def _head_masks():
    lane = lax.broadcasted_iota(jnp.int32, (1, D_SGU), 1)
    return [((lane >= 64 * h) & (lane < 64 * (h + 1))).astype(F32) for h in range(4)]


def _sgu_mix(w_ref, vch, masks):
    z = masks[0] * _dot(w_ref[0], vch)
    for h in range(1, 4):
        z = z + masks[h] * _dot(w_ref[h], vch)
    return z


def _mix_fwd(p, prm):
    T = p.shape[0]
    tb = _mix_block(T, 512)
    nb = T // tb

    def body(p_ref, xah_ref, xph_ref, cw_ref, cb_ref, wax_ref, bax_ref, lam_ref, wp_ref, ps_ref, sgn_ref,
             ws_ref, bz_ref, y_ref, hs_ref, carry_ref):
        i = pl.program_id(0)
        keep = (i > 0).astype(F32)

        @pl.when(i == 0)
        def _():
            carry_ref[...] = jnp.zeros_like(carry_ref)

        xa_ext = jnp.concatenate([xah_ref[...] * keep, p_ref[:, 512:1024]], axis=0)
        xc = _conv(xa_ext, cw_ref, cb_ref)
        r, ig, sp, a, mult = _rglru_gates(xc, wax_ref, bax_ref, lam_ref)
        bv = mult * (ig * xc)
        row = _rows(tb, D_RNN)
        s = 1
        while s < tb:
            m = row >= s
            bv = jnp.where(m, a * pltpu.roll(bv, s, 0) + bv, bv)
            a = jnp.where(m, a * pltpu.roll(a, s, 0), a)
            s *= 2
        h = bv + a * carry_ref[0:1, :]
        hs_ref[...] = h
        last = jnp.sum(jnp.where(_rows(8, D_RNN) == 7, hs_ref[tb - 8:tb, :], 0.0), axis=0, keepdims=True)
        carry_ref[...] = jnp.broadcast_to(last, carry_ref.shape)
        y_ref[:, 0:512] = (_gelu(p_ref[:, 0:512]) * h).astype(BF16)

        xp_ext = jnp.concatenate([xph_ref[...] * keep, p_ref[:, 1024:1280]], axis=0)
        d, _ = _pool_diff(xp_ext, i * tb, tb)
        y_ref[:, 512:768] = (_dot(d.astype(BF16), wp_ref[...]) * ps_ref[...]).astype(BF16)

        ug = _gelu(p_ref[:, 1280:1536])
        vg = _gelu(p_ref[:, 1536:1792])
        rv = lax.rsqrt(jnp.mean(vg * vg, axis=-1, keepdims=True) + EPS)
        vn = (vg * rv * sgn_ref[...]).astype(BF16)
        masks = _head_masks()
        for ci in range(tb // CHUNK):
            sl = slice(ci * CHUNK, (ci + 1) * CHUNK)
            z = _sgu_mix(ws_ref, vn[sl], masks) + bz_ref[...]
            y_ref[sl, 768:1024] = (ug[sl] * z).astype(BF16)

    hb = tb // HALO

    def halo(i):
        return jnp.maximum(i * hb - 1, 0)

    def full(shape):
        return pl.BlockSpec(shape, lambda i: (0,) * len(shape))

    return pl.pallas_call(
        body, name="mix_fwd", grid=(nb,),
        in_specs=[pl.BlockSpec((tb, D_IN), lambda i: (i, 0)),
                  pl.BlockSpec((HALO, D_RNN), lambda i: (halo(i), 1)),
                  pl.BlockSpec((HALO, D_POOL), lambda i: (halo(i), 4)),
                  full((4, D_RNN)), full((1, D_RNN)), full((D_RNN, 2 * D_RNN)), full((1, 2 * D_RNN)),
                  full((1, D_RNN)), full((D_POOL, D_POOL)), full((1, D_POOL)), full((1, D_SGU)),
                  full((4, CHUNK, CHUNK)), full((CHUNK, D_SGU))],
        out_specs=[pl.BlockSpec((tb, D), lambda i: (i, 0)), pl.BlockSpec((tb, D_RNN), lambda i: (i, 0))],
        out_shape=[jax.ShapeDtypeStruct((T, D), BF16), jax.ShapeDtypeStruct((T, D_RNN), F32)],
        scratch_shapes=[pltpu.VMEM((8, D_RNN), F32)],
        compiler_params=_params("arbitrary"),
    )(p, p, p, prm["conv_w"], prm["conv_b"], prm["wax"], prm["bax"], prm["lam"], prm["wpool"], prm["pool_scale"],
      prm["sgu_norm"], prm["ws"], prm["bz"])


def _mix_bwd(dy, p, hs, prm):
    T = p.shape[0]
    tb = _mix_block(T, 256)
    nb = T // tb
    hb = tb // HALO

    def body(dy_ref, p_ref, xah_ref, xph_ref, hs_ref, hsh_ref, cw_ref, cb_ref, wax_ref, waxt_ref, bax_ref,
             lam_ref, wp_ref, wpt_ref, ps_ref, sgn_ref, ws_ref, wst_ref, bz_ref,
             dp_ref, dcw_ref, dcb_ref, dwax_ref, dbax_ref, dlam_ref, dwp_ref, dps_ref, dsgn_ref, dws_ref,
             dbz_ref, gcarry_ref, xcfut_ref, mfut_ref):
        i = pl.program_id(0)
        bi = nb - 1 - i
        keep = (bi > 0).astype(F32)

        @pl.when(i == 0)
        def _():
            for ref in (dcw_ref, dcb_ref, dwax_ref, dbax_ref, dlam_ref, dwp_ref, dps_ref, dsgn_ref, dws_ref,
                        dbz_ref, gcarry_ref, xcfut_ref, mfut_ref):
                ref[...] = jnp.zeros_like(ref)

        xa_ext = jnp.concatenate([xah_ref[...] * keep, p_ref[:, 512:1024]], axis=0)
        xc = _conv(xa_ext, cw_ref, cb_ref)
        r, ig, sp, a, mult = _rglru_gates(xc, wax_ref, bax_ref, lam_ref)
        gg, dgg = _gelu_and_grad(p_ref[:, 0:512])
        dya = dy_ref[:, 0:512]
        dp_ref[:, 0:512] = (dya * hs_ref[...] * dgg).astype(BF16)
        row = _rows(tb, D_RNN)
        g = dya * gg + jnp.where(row == tb - 1, gcarry_ref[0:1, :], 0.0)
        al = pltpu.roll(a, tb - 1, 0)
        s = 1
        while s < tb:
            m = row < tb - s
            g = jnp.where(m, al * pltpu.roll(g, tb - s, 0) + g, g)
            al = jnp.where(m, al * pltpu.roll(al, tb - s, 0), al)
            s *= 2
        first = jnp.sum(jnp.where(_rows(8, D_RNN) == 0, (a * g)[0:8], 0.0), axis=0, keepdims=True)
        gcarry_ref[...] = jnp.broadcast_to(first, gcarry_ref.shape)
        hs_ext = jnp.concatenate([hsh_ref[...] * keep, hs_ref[...]], axis=0)
        h_prev = pltpu.roll(hs_ext, 1, 0)[HALO:]
        ix = ig * xc
        dlog_a = g * h_prev * a - (g * ix) * (a * a / mult)
        dlam_ref[...] += jnp.sum(dlog_a * r, axis=0, keepdims=True) * (LRU_C * _sigmoid(-lam_ref[...]))
        dpre_r = dlog_a * ((-LRU_C) * sp) * (r * (1.0 - r))
        dpre_i = (g * mult * xc) * (ig * (1.0 - ig))
        dpre = jnp.concatenate([dpre_r, dpre_i], axis=1)
        dbax_ref[...] += jnp.sum(dpre, axis=0, keepdims=True)
        dpre_b = dpre.astype(BF16)
        dwax_ref[...] += _dot_tn(xc.astype(BF16), dpre_b)
        dxc = g * mult * ig + _dot(dpre_b, waxt_ref[...])
        dcb_ref[...] += jnp.sum(dxc, axis=0, keepdims=True)
        for k in range(4):
            xs = xa_ext[HALO:] if k == 3 else pltpu.roll(xa_ext, 3 - k, 0)[HALO:]
            dcw_ref[k:k + 1, :] += jnp.sum(dxc * xs, axis=0, keepdims=True)
        dxc_ext = jnp.concatenate([dxc, xcfut_ref[...]], axis=0)
        n = tb + HALO
        dxa = cw_ref[3:4, :] * dxc_ext
        for k in range(1, 4):
            dxa = dxa + cw_ref[3 - k:4 - k, :] * pltpu.roll(dxc_ext, n - k, 0)
        dp_ref[:, 512:1024] = dxa[:tb].astype(BF16)
        xcfut_ref[...] = dxc[0:HALO]

        xp_ext = jnp.concatenate([xph_ref[...] * keep, p_ref[:, 1024:1280]], axis=0)
        d, cnt = _pool_diff(xp_ext, bi * tb, tb)
        db = d.astype(BF16)
        dyb = dy_ref[:, 512:768]
        dps_ref[...] += jnp.sum(dyb * _dot(db, wp_ref[...]), axis=0, keepdims=True)
        dq = (dyb * ps_ref[...]).astype(BF16)
        dwp_ref[...] += _dot_tn(db, dq)
        dd = _dot(dq, wpt_ref[...])
        mm = dd / cnt
        m_ext = jnp.concatenate([mm, mfut_ref[...]], axis=0)
        f2 = m_ext + pltpu.roll(m_ext, n - 1, 0)
        f4 = f2 + pltpu.roll(f2, n - 2, 0)
        f8 = f4 + pltpu.roll(f4, n - 4, 0)
        f16 = f8 + pltpu.roll(f8, n - 8, 0)
        dp_ref[:, 1024:1280] = (_pool_select(f2, f4, f8, f16)[:tb] - dd).astype(BF16)
        mfut_ref[...] = mm[0:HALO]

        ug, dug = _gelu_and_grad(p_ref[:, 1280:1536])
        vg, dvg = _gelu_and_grad(p_ref[:, 1536:1792])
        rv = lax.rsqrt(jnp.mean(vg * vg, axis=-1, keepdims=True) + EPS)
        vhat = vg * rv
        vn = (vhat * sgn_ref[...]).astype(BF16)
        dyc = dy_ref[:, 768:1024]
        masks = _head_masks()
        dz = dyc * ug
        dzb = dz.astype(BF16)
        dvn_parts = []
        for ci in range(tb // CHUNK):
            sl = slice(ci * CHUNK, (ci + 1) * CHUNK)
            z = _sgu_mix(ws_ref, vn[sl], masks) + bz_ref[...]
            dp_ref[sl, 1280:1536] = (dyc[sl] * z * dug[sl]).astype(BF16)
            dbz_ref[...] += dz[sl]
            for h in range(4):
                dws_ref[h] += _dot_nt((dz[sl] * masks[h]).astype(BF16), vn[sl])
            dvn_parts.append(_sgu_mix(wst_ref, dzb[sl], masks))
        dvn = jnp.concatenate(dvn_parts, axis=0)
        dsgn_ref[...] += jnp.sum(dvn * vhat, axis=0, keepdims=True)
        dvhat = dvn * sgn_ref[...]
        dvg_in = rv * (dvhat - vhat * jnp.mean(dvhat * vhat, axis=-1, keepdims=True))
        dp_ref[:, 1536:1792] = (dvg_in * dvg).astype(BF16)

        @pl.when(i == nb - 1)
        def _():
            tril = (lax.broadcasted_iota(jnp.int32, (CHUNK, CHUNK), 0)
                    >= lax.broadcasted_iota(jnp.int32, (CHUNK, CHUNK), 1)).astype(F32)
            for h in range(4):
                dws_ref[h] = dws_ref[h] * tril

    def blk(i):
        return nb - 1 - i

    def halo(i):
        return jnp.maximum(blk(i) * hb - 1, 0)

    def full(shape):
        return pl.BlockSpec(shape, lambda i: (0,) * len(shape))

    small_shapes = [(4, D_RNN), (1, D_RNN), (D_RNN, 2 * D_RNN), (1, 2 * D_RNN), (1, D_RNN), (D_POOL, D_POOL),
                    (1, D_POOL), (1, D_SGU), (4, CHUNK, CHUNK), (CHUNK, D_SGU)]
    outs = pl.pallas_call(
        body, name="mix_bwd", grid=(nb,),
        in_specs=[pl.BlockSpec((tb, D), lambda i: (blk(i), 0)),
                  pl.BlockSpec((tb, D_IN), lambda i: (blk(i), 0)),
                  pl.BlockSpec((HALO, D_RNN), lambda i: (halo(i), 1)),
                  pl.BlockSpec((HALO, D_POOL), lambda i: (halo(i), 4)),
                  pl.BlockSpec((tb, D_RNN), lambda i: (blk(i), 0)),
                  pl.BlockSpec((HALO, D_RNN), lambda i: (halo(i), 0)),
                  full((4, D_RNN)), full((1, D_RNN)), full((D_RNN, 2 * D_RNN)), full((2 * D_RNN, D_RNN)),
                  full((1, 2 * D_RNN)), full((1, D_RNN)), full((D_POOL, D_POOL)), full((D_POOL, D_POOL)),
                  full((1, D_POOL)), full((1, D_SGU)), full((4, CHUNK, CHUNK)), full((4, CHUNK, CHUNK)),
                  full((CHUNK, D_SGU))],
        out_specs=[pl.BlockSpec((tb, D_IN), lambda i: (blk(i), 0))] + [full(s) for s in small_shapes],
        out_shape=[jax.ShapeDtypeStruct((T, D_IN), BF16)] + [jax.ShapeDtypeStruct(s, F32) for s in small_shapes],
        scratch_shapes=[pltpu.VMEM((8, D_RNN), F32), pltpu.VMEM((HALO, D_RNN), F32),
                        pltpu.VMEM((HALO, D_POOL), F32)],
        compiler_params=_params("arbitrary"),
    )(dy, p, p, p, hs, hs, prm["conv_w"], prm["conv_b"], prm["wax"], prm["wax_t"], prm["bax"], prm["lam"],
      prm["wpool"], prm["wpool_t"], prm["pool_scale"], prm["sgu_norm"], prm["ws"], prm["ws_t"], prm["bz"])
    names = ("dp", "conv_w", "conv_b", "wax", "bax", "lam", "wpool", "pool_scale", "sgu_norm", "ws", "bz")
    return dict(zip(names, outs))


ANY = pl.BlockSpec(memory_space=pl.ANY)


def _place():
    x, y, c = lax.axis_index("x"), lax.axis_index("y"), lax.axis_index("c")
    return x, y, c


def _all_to_all(xs, name):
    def body(in_ref, out_ref, send_sems, recv_sems, local_sem):
        x, y, c = _place()
        me = 4 * x + 2 * y + c
        mine = pltpu.make_async_copy(in_ref.at[me], out_ref.at[me], local_sem)
        mine.start()
        copies = []
        for rel in range(1, N_DEV):
            tx = 1 - x if rel & 4 else x
            ty = 1 - y if rel & 2 else y
            tc = 1 - c if rel & 1 else c
            cp = pltpu.make_async_remote_copy(
                src_ref=in_ref.at[4 * tx + 2 * ty + tc], dst_ref=out_ref.at[me],
                send_sem=send_sems.at[rel - 1], recv_sem=recv_sems.at[rel - 1],
                device_id=(tx, ty, tc), device_id_type=MESH)
            cp.start()
            copies.append(cp)
        for cp in copies:
            cp.wait()
        mine.wait()

    return pl.pallas_call(
        body, name=name, in_specs=[ANY], out_specs=ANY,
        out_shape=jax.ShapeDtypeStruct(xs.shape, xs.dtype),
        scratch_shapes=[pltpu.SemaphoreType.DMA((N_DEV - 1,)), pltpu.SemaphoreType.DMA((N_DEV - 1,)),
                        pltpu.SemaphoreType.DMA],
    )(xs)


def _all_gather8(xs, name):
    def body(x_ref, out_ref, send_sems, recv_sems, local_sem):
        x, y, c = _place()
        me, sibling = (x, y, c), (x, y, 1 - c)
        chips = [(1 - x, y), (x, 1 - y), (1 - x, 1 - y)]

        def rows(px, py, pc):
            return out_ref.at[4 * px + 2 * py + pc]

        def copy(k, block, to, src=None):
            return pltpu.make_async_remote_copy(
                src_ref=rows(*block) if src is None else src, dst_ref=rows(*block),
                send_sem=send_sems.at[k], recv_sem=recv_sems.at[k], device_id=to, device_id_type=MESH)

        mine = pltpu.make_async_copy(x_ref, rows(*me), local_sem)
        mine.start()
        first = [copy(0, me, sibling, src=x_ref)]
        first += [copy(1 + j, me, (*chip, c), src=x_ref) for j, chip in enumerate(chips)]
        for cp in first:
            cp.start()
        passed = [copy(4 + j, (*chip, c), sibling) for j, chip in enumerate(chips)]
        for j, chip in enumerate(chips):
            copy(1 + j, (*chip, c), me).wait_recv()
            passed[j].start()
        copy(0, sibling, me).wait_recv()
        for j, chip in enumerate(chips):
            copy(4 + j, (*chip, 1 - c), me).wait_recv()
        for cp in first + passed:
            cp.wait_send()
        mine.wait()

    return pl.pallas_call(
        body, name=name, in_specs=[ANY], out_specs=ANY,
        out_shape=jax.ShapeDtypeStruct((N_DEV,) + xs.shape, xs.dtype),
        scratch_shapes=[pltpu.SemaphoreType.DMA((7,)), pltpu.SemaphoreType.DMA((7,)), pltpu.SemaphoreType.DMA],
    )(xs)


HBM = pl.BlockSpec(memory_space=pltpu.HBM)
SEM = pl.BlockSpec(memory_space=pltpu.SEMAPHORE)
EFFECT = pltpu.SideEffectType.DATAFLOW_SIDE_EFFECTING


def _in_hbm(a):
    return pltpu.with_memory_space_constraint(a, pltpu.HBM)


def _local_copy(src, dst, stage, sem):
    load = pltpu.make_async_copy(src, stage, sem)
    load.start()
    load.wait()
    store = pltpu.make_async_copy(stage, dst, sem)
    store.start()
    store.wait()


def _unique(windows):
    arrays = []
    for per_chip in windows:
        for arr, _ in per_chip:
            if not any(arr is a for a in arrays):
                arrays.append(arr)
    return arrays


def _exchange_start(layer, windows, lands, name):
    arrays = _unique(windows)
    na, nt = len(arrays), len(windows)

    def body(*refs):
        in_refs, land_refs = refs[:na], refs[na:na + nt]
        send_sems, recv_sems = refs[na + nt], refs[na + nt + 1]
        token = refs[-1]
        x, y, c = _place()
        me = 4 * x + 2 * y + c
        for t in range(nt):
            for j in range(N_CHIP):
                arr, window = windows[t][j]
                src = window(in_refs[next(i for i, a in enumerate(arrays) if a is arr)])

                @pl.when(me != 2 * j + layer)
                def _():
                    pltpu.make_async_remote_copy(
                        src_ref=src, dst_ref=land_refs[t].at[me], send_sem=send_sems.at[N_CHIP * t + j],
                        recv_sem=recv_sems.at[N_DEV * t + me], device_id=(j // 2, j % 2, layer),
                        device_id_type=MESH).start()
        token[...] = jnp.zeros_like(token)

    outs = pl.pallas_call(
        body, name=name,
        out_shape=(pltpu.SemaphoreType.DMA((N_CHIP * nt,)), pltpu.SemaphoreType.DMA((N_DEV * nt,)),
                   *[pltpu.HBM(a.shape, a.dtype) for a in lands], jax.ShapeDtypeStruct((8, 128), F32)),
        in_specs=[HBM] * (na + nt),
        out_specs=(SEM, SEM, *[HBM] * nt, pl.BlockSpec(memory_space=pltpu.VMEM)),
        input_output_aliases={na + t: 2 + t for t in range(nt)},
        compiler_params=pltpu.CompilerParams(has_side_effects=EFFECT, vmem_limit_bytes=VMEM_LIMIT),
    )(*[_in_hbm(a) for a in arrays], *[_in_hbm(a) for a in lands])
    return outs[0], outs[1], list(outs[2:2 + nt]), outs[-1]


def _exchange_wait(layer, windows, lands, send_sems, recv_sems, after, name):
    arrays = _unique(windows)
    na, nt = len(arrays), len(windows)

    def body(*refs):
        in_refs, land_refs = refs[:na], refs[na:na + nt]
        send_sems, recv_sems = refs[na + nt], refs[na + nt + 1]
        stages, local_sem = refs[-1 - nt:-1], refs[-1]
        x, y, c = _place()
        me = 4 * x + 2 * y + c

        def source(t, j):
            arr, window = windows[t][j]
            return window(in_refs[next(i for i, a in enumerate(arrays) if a is arr)])

        @pl.when(c == layer)
        def _():
            for t in range(nt):
                for j in range(N_CHIP):
                    @pl.when(me == 2 * j + layer)
                    def _():
                        _local_copy(source(t, j), land_refs[t].at[me], stages[t], local_sem)

        for t in range(nt):
            for j in range(N_CHIP):
                @pl.when(me != 2 * j + layer)
                def _():
                    pltpu.make_async_remote_copy(
                        src_ref=source(t, j), dst_ref=land_refs[t].at[me], send_sem=send_sems.at[N_CHIP * t + j],
                        recv_sem=recv_sems.at[N_DEV * t + me], device_id=(j // 2, j % 2, layer),
                        device_id_type=MESH).wait_send()

        @pl.when(c == layer)
        def _():
            for t in range(nt):
                for s in range(N_DEV):
                    @pl.when(me != s)
                    def _():
                        slot = land_refs[t].at[s]
                        pltpu.make_async_remote_copy(
                            src_ref=slot, dst_ref=slot, send_sem=send_sems.at[N_CHIP * t],
                            recv_sem=recv_sems.at[N_DEV * t + s], device_id=(x, y, c),
                            device_id_type=MESH).wait_recv()

    outs = pl.pallas_call(
        body, name=name,
        out_shape=tuple(pltpu.HBM(a.shape, a.dtype) for a in lands),
        in_specs=[HBM] * (na + nt) + [SEM, SEM, ANY],
        out_specs=tuple([HBM] * nt),
        input_output_aliases={na + t: t for t in range(nt)},
        scratch_shapes=[pltpu.VMEM(a.shape[1:], a.dtype) for a in lands] + [pltpu.SemaphoreType.DMA],
        compiler_params=pltpu.CompilerParams(has_side_effects=EFFECT, vmem_limit_bytes=VMEM_LIMIT),
    )(*[_in_hbm(a) for a in arrays], *lands, send_sems, recv_sems, after)
    return list(outs)


def _other_chips(x, y):
    return [(1 - x, y), (x, 1 - y), (1 - x, 1 - y)]


def _gather_start(layer, shards, lands, name):
    nt = len(shards)

    def body(*refs):
        in_refs, land_refs = refs[:nt], refs[nt:2 * nt]
        send_sems, recv_sems = refs[2 * nt], refs[2 * nt + 1]
        token = refs[-1]
        x, y, c = _place()

        @pl.when(c == layer)
        def _():
            for t in range(nt):
                for rel, (tx, ty) in enumerate(_other_chips(x, y)):
                    for tc in range(2):
                        pltpu.make_async_remote_copy(
                            src_ref=in_refs[t], dst_ref=land_refs[t].at[2 * x + y],
                            send_sem=send_sems.at[6 * t + 2 * rel + tc], recv_sem=recv_sems.at[3 * t + rel],
                            device_id=(tx, ty, tc), device_id_type=MESH).start()

        token[...] = jnp.zeros_like(token)

    outs = pl.pallas_call(
        body, name=name,
        out_shape=(pltpu.SemaphoreType.DMA((6 * nt,)), pltpu.SemaphoreType.DMA((3 * nt,)),
                   *[pltpu.HBM(a.shape, a.dtype) for a in lands], jax.ShapeDtypeStruct((8, 128), F32)),
        in_specs=[HBM] * (2 * nt),
        out_specs=(SEM, SEM, *[HBM] * nt, pl.BlockSpec(memory_space=pltpu.VMEM)),
        input_output_aliases={nt + t: 2 + t for t in range(nt)},
        compiler_params=pltpu.CompilerParams(has_side_effects=EFFECT, vmem_limit_bytes=VMEM_LIMIT),
    )(*[_in_hbm(a) for a in shards], *[_in_hbm(a) for a in lands])
    return outs[0], outs[1], list(outs[2:2 + nt]), outs[-1]


def _gather_wait(layer, shards, lands, send_sems, recv_sems, after, name):
    nt = len(shards)

    def body(*refs):
        in_refs, land_refs = refs[:nt], refs[nt:2 * nt]
        send_sems, recv_sems = refs[2 * nt], refs[2 * nt + 1]
        stages, local_sem = refs[-1 - nt:-1], refs[-1]
        x, y, c = _place()

        for t in range(nt):
            _local_copy(in_refs[t], land_refs[t].at[2 * x + y], stages[t], local_sem)

        @pl.when(c == layer)
        def _():
            for t in range(nt):
                for rel, (tx, ty) in enumerate(_other_chips(x, y)):
                    for tc in range(2):
                        pltpu.make_async_remote_copy(
                            src_ref=in_refs[t], dst_ref=land_refs[t].at[2 * x + y],
                            send_sem=send_sems.at[6 * t + 2 * rel + tc], recv_sem=recv_sems.at[3 * t + rel],
                            device_id=(tx, ty, tc), device_id_type=MESH).wait_send()

        for t in range(nt):
            for rel, (tx, ty) in enumerate(_other_chips(x, y)):
                slot = land_refs[t].at[2 * tx + ty]
                pltpu.make_async_remote_copy(
                    src_ref=slot, dst_ref=slot, send_sem=send_sems.at[6 * t], recv_sem=recv_sems.at[3 * t + rel],
                    device_id=(x, y, c), device_id_type=MESH).wait_recv()

    outs = pl.pallas_call(
        body, name=name,
        out_shape=tuple(pltpu.HBM(a.shape, a.dtype) for a in lands),
        in_specs=[HBM] * (2 * nt) + [SEM, SEM, ANY],
        out_specs=tuple([HBM] * nt),
        input_output_aliases={nt + t: t for t in range(nt)},
        scratch_shapes=[pltpu.VMEM(a.shape, a.dtype) for a in shards] + [pltpu.SemaphoreType.DMA],
        compiler_params=pltpu.CompilerParams(has_side_effects=EFFECT, vmem_limit_bytes=VMEM_LIMIT),
    )(*[_in_hbm(a) for a in shards], *lands, send_sems, recv_sems, after)
    return list(outs)


def _tie(a, token):
    def body(a_ref, token_ref, o_ref):
        pass

    return pl.pallas_call(
        body, name="tie", in_specs=[ANY, ANY], out_specs=ANY,
        out_shape=jax.ShapeDtypeStruct(a.shape, a.dtype), input_output_aliases={0: 0},
    )(a, token)


def _sum_share(xs, name):
    _, r, cols = xs.shape
    tr = 256 if r % 256 == 0 else r
    nblk = r // tr

    def body(x_ref, out_ref, acc_ref, send_sems, local_sems, recv_sem):
        i = pl.program_id(0)
        slot = i % 2
        x, y, c = _place()

        def copies(s, blk):
            dst = out_ref.at[c, pl.ds(blk * tr, tr), :]
            loc = pltpu.make_async_copy(acc_ref.at[s], dst, local_sems.at[s])
            rem = pltpu.make_async_remote_copy(src_ref=acc_ref.at[s], dst_ref=dst, send_sem=send_sems.at[s],
                                               recv_sem=recv_sem, device_id=(x, y, 1 - c), device_id_type=MESH)
            return loc, rem

        @pl.when(i >= 2)
        def _():
            loc, rem = copies(slot, i - 2)
            loc.wait()
            rem.wait_send()

        acc = x_ref[0].astype(F32)
        for k in range(1, N_DEV):
            acc = acc + x_ref[k].astype(F32)
        acc_ref[slot] = acc
        loc, rem = copies(slot, i)
        loc.start()
        rem.start()

        @pl.when(i == nblk - 1)
        def _():
            for back in range(min(2, nblk)):
                blk = nblk - 1 - back
                loc, rem = copies(blk % 2, blk)
                loc.wait()
                rem.wait_send()
            theirs = out_ref.at[1 - c]
            pltpu.make_async_remote_copy(src_ref=theirs, dst_ref=theirs, send_sem=send_sems.at[0],
                                         recv_sem=recv_sem, device_id=(x, y, 1 - c),
                                         device_id_type=MESH).wait_recv()

    return pl.pallas_call(
        body, name=name, grid=(nblk,),
        in_specs=[pl.BlockSpec((N_DEV, tr, cols), lambda i: (0, i, 0))],
        out_specs=ANY,
        out_shape=jax.ShapeDtypeStruct((2, r, cols), F32),
        scratch_shapes=[pltpu.VMEM((2, tr, cols), F32), pltpu.SemaphoreType.DMA((2,)),
                        pltpu.SemaphoreType.DMA((2,)), pltpu.SemaphoreType.DMA],
        compiler_params=_params("arbitrary"),
    )(xs)


def _sum8(xs, name):
    _, r, cols = xs.shape
    tr = 8
    for cand in (256, 128, 64, 32, 16):
        if r % cand == 0:
            tr = cand
            break

    def body(x_ref, o_ref):
        acc = x_ref[0].astype(F32)
        for k in range(1, N_DEV):
            acc = acc + x_ref[k].astype(F32)
        o_ref[...] = acc

    return pl.pallas_call(
        body, name=name, grid=(r // tr,),
        in_specs=[pl.BlockSpec((N_DEV, tr, cols), lambda i: (0, i, 0))],
        out_specs=pl.BlockSpec((tr, cols), lambda i: (i, 0)),
        out_shape=jax.ShapeDtypeStruct((r, cols), F32),
        compiler_params=_params("parallel"),
    )(xs)


def _adamw(w, g, m, v, name):
    R, C = w.shape
    tr = R
    for cand in (256, 128, 64, 32, 16, 8):
        if R % cand == 0:
            tr = cand
            break
    c1 = 1.0 / (1.0 - ADAM_B1 ** ADAM_STEP)
    c2 = 1.0 / (1.0 - ADAM_B2 ** ADAM_STEP)

    def body(w_ref, g_ref, m_ref, v_ref, d_ref, nm_ref, nv_ref):
        gv = g_ref[...]
        nm = ADAM_B1 * m_ref[...] + (1.0 - ADAM_B1) * gv
        nv = ADAM_B2 * v_ref[...] + (1.0 - ADAM_B2) * (gv * gv)
        d_ref[...] = (-ADAM_LR) * ((nm * c1) / (jnp.sqrt(nv * c2) + ADAM_EPS) + ADAM_WD * w_ref[...])
        nm_ref[...] = nm
        nv_ref[...] = nv

    spec = pl.BlockSpec((tr, C), lambda i: (i, 0))
    shape = jax.ShapeDtypeStruct((R, C), F32)
    return pl.pallas_call(
        body, name=name, grid=(R // tr,), in_specs=[spec] * 4, out_specs=[spec] * 3, out_shape=[shape] * 3,
        compiler_params=_params("parallel"),
    )(w, g, m, v)


def _flat_rows(parts, rows):
    flat = jnp.concatenate([q.reshape(-1) for q in parts])
    flat = jnp.pad(flat, (0, rows * LANES - flat.shape[0]))
    return flat.reshape(rows, LANES)


def _round_up(n, m):
    return (n + m - 1) // m * m


def _block_diag(w):
    H, n, _ = w.shape
    eye = jnp.eye(H, dtype=w.dtype)
    return (eye[:, None, :, None] * w[:, :, None, :]).reshape(H * n, H * n)


def _diag_blocks(w, H, n):
    w4 = w.reshape(H, n, H, n)
    return jnp.stack([w4[h, :, h, :] for h in range(H)])


def _ffn_in_weights(g_in):
    pad = lambda a: jnp.pad(a, ((0, 0), (0, FF_HALF - FF_SHARD)))
    wg = jnp.concatenate([pad(g_in[0]), pad(g_in[1])], axis=1)
    wu = jnp.concatenate([pad(g_in[2]), pad(g_in[3])], axis=1)
    return wg, wu


def _ffn_out_weights(g_out):
    zeros = jnp.zeros((FF_HALF - FF_SHARD, D), g_out.dtype)
    return jnp.concatenate([g_out[0], g_out[1], zeros, g_out[2], g_out[3], zeros], axis=0)


LAND_SHAPES = {"ffn1_w_in": (D, FF_HALF), "ffn1_w_out": (FF_ROWS, D), "w_in": (D, D_IN // N_CHIP),
               "w_out": (D // N_CHIP, D), "ffn2_w_in": (D, FF_HALF), "ffn2_w_out": (FF_ROWS, D)}


def _ffn_grad_windows(dwg, dwu, dwout):
    w_in, w_out = [], []
    for j in range(N_CHIP):
        col = (j % 2) * FF_HALF
        row = (j // 2) * FF_HALF + (j % 2) * FF_ROWS
        w_in.append((dwg if j < 2 else dwu, lambda r, col=col: r.at[:, pl.ds(col, FF_HALF)]))
        w_out.append((dwout, lambda r, row=row: r.at[pl.ds(row, FF_ROWS), :]))
    return w_in, w_out


def _mix_grad_windows(dwin, dwo):
    rows = D // N_CHIP
    win = [(dwin, lambda r, j=j: r.at[j]) for j in range(N_CHIP)]
    wo = [(dwo, lambda r, j=j: r.at[pl.ds(j * rows, rows), :]) for j in range(N_CHIP)]
    return win, wo


def kernel(x, ffn1_norm, ffn1_w_in, ffn1_w_out, mix_norm, w_in, conv_w, conv_b, rg_w_a, rg_b_a, rg_w_x, rg_b_x, lru_lambda, pool_w, pool_scale, sgu_norm, sgu_w, sgu_b, w_out, ffn2_norm, ffn2_w_in, ffn2_w_out, final_norm, loss_target, m_ffn1_norm, m_ffn1_w_in, m_ffn1_w_out, m_mix_norm, m_w_in, m_conv_w, m_conv_b, m_rg_w_a, m_rg_b_a, m_rg_w_x, m_rg_b_x, m_lru_lambda, m_pool_w, m_pool_scale, m_sgu_norm, m_sgu_w, m_sgu_b, m_w_out, m_ffn2_norm, m_ffn2_w_in, m_ffn2_w_out, m_final_norm, v_ffn1_norm, v_ffn1_w_in, v_ffn1_w_out, v_mix_norm, v_w_in, v_conv_w, v_conv_b, v_rg_w_a, v_rg_b_a, v_rg_w_x, v_rg_b_x, v_lru_lambda, v_pool_w, v_pool_scale, v_sgu_norm, v_sgu_w, v_sgu_b, v_w_out, v_ffn2_norm, v_ffn2_w_in, v_ffn2_w_out, v_final_norm):
    args = locals()
    W = {n: args[n] for n in WEIGHTS}
    M = {n: args["m_" + n] for n in WEIGHTS}
    V = {n: args["v_" + n] for n in WEIGHTS}
    depth = ffn1_norm.shape[0]
    T = x.shape[1]
    xi, yi, ci = _place()
    chip = 2 * xi + yi

    assert depth == 2, "core c of a chip sends and reduces layer c"
    groups = [(l, names) for l in range(depth)
              for names in (["ffn1_w_in"], ["ffn1_w_out", "w_in", "w_out"], ["ffn2_w_in", "ffn2_w_out"])]
    wb = {n: W[n].astype(BF16) for n in BIG}
    groups[1][1].append("conv_w")
    conv_shard = conv_w.reshape(-1, conv_w.shape[-1])
    flights = {}

    def weights_start(k, dep=None):
        l, names = groups[k]
        shards = [conv_shard if n == "conv_w" else wb[n][l] for n in names]
        if dep is not None:
            shards[0] = _tie(shards[0], dep)
        lands = [lax.empty((N_CHIP,) + s.shape, s.dtype) for s in shards]
        send, recv, lands, token = _gather_start(l, shards, lands, "weights_start_%d" % k)
        flights[k] = (shards, lands, send, recv)
        return token

    def weights_wait(k, after):
        l, names = groups[k]
        shards, lands, send, recv = flights[k]
        got = _gather_wait(l, shards, lands, send, recv, after, "weights_wait_%d" % k)
        token = weights_start(k + 2, got[0]) if k + 2 < len(groups) else None
        return dict(zip(names, got)), token

    def after_start(a, token):
        return a if token is None else _tie(a, token)

    first_tokens = [weights_start(0), weights_start(1)]

    layers = []
    for l in range(depth):
        L = {f: dict(norm=W[f + "_norm"][l][None]) for f in ("ffn1", "ffn2")}
        ws = jnp.where(jnp.tril(jnp.ones((CHUNK, CHUNK), bool))[None], sgu_w[l], 0.0)
        wax = jnp.concatenate([_block_diag(rg_w_a[l]), _block_diag(rg_w_x[l])], axis=1)
        wpool = _block_diag(pool_w[l])
        L["mix"] = dict(
            conv_b=conv_b[l][None], wax=wax.astype(BF16), wax_t=wax.T.astype(BF16),
            bax=jnp.concatenate([rg_b_a[l].reshape(-1), rg_b_x[l].reshape(-1)])[None], lam=lru_lambda[l][None],
            wpool=wpool.astype(BF16), wpool_t=wpool.T.astype(BF16), pool_scale=pool_scale[l][None],
            sgu_norm=sgu_norm[l][None], ws=ws.astype(BF16), ws_t=jnp.swapaxes(ws, 1, 2).astype(BF16),
            bz=jnp.repeat(sgu_b[l].T, 64, axis=1))
        L["mix_norm"] = mix_norm[l][None]
        layers.append(L)
    for token in first_tokens:
        layers[0]["ffn1"]["norm"] = _tie(layers[0]["ffn1"]["norm"], token)

    xs = x[0]
    saved = []
    for l, L in enumerate(layers):
        F1, F2 = L["ffn1"], L["ffn2"]
        h = _rms_fwd(xs, F1["norm"])
        got, token = weights_wait(3 * l, h)
        F1["wg"], F1["wu"] = _ffn_in_weights(got["ffn1_w_in"])
        h = after_start(h, token)
        g, u, a = _ffn_in(h, F1["wg"], F1["wu"])
        got, token = weights_wait(3 * l + 1, a)
        F1["wout"] = _ffn_out_weights(got["ffn1_w_out"])
        L["w_in"] = jnp.concatenate([got["w_in"][j] for j in range(N_CHIP)], axis=1)
        L["w_out"] = jnp.concatenate([got["w_out"][j] for j in range(N_CHIP)], axis=0)
        if "conv_w" in got:
            conv_full = jnp.concatenate([got["conv_w"][j] for j in range(N_CHIP)], axis=1)
            for ll in range(depth):
                layers[ll]["mix"]["conv_w"] = conv_full.reshape(depth, 4, D_RNN)[ll]
        a = after_start(a, token)
        x1 = _mm_res(a, F1["wout"], xs, 0.5, "ffn_out", tm=512, tn=D)
        s1 = (xs, h, g, u, a)
        hm = _rms_fwd(x1, L["mix_norm"])
        p = _mm(hm, L["w_in"], F32, "mix_in", tm=512, tn=D_IN)
        ycat, hs = _mix_fwd(p, L["mix"])
        x2 = _mm_res(ycat, L["w_out"], x1, 1.0, "mix_out", tm=512, tn=D)
        h2 = _rms_fwd(x2, F2["norm"])
        got, token = weights_wait(3 * l + 2, h2)
        F2["wg"], F2["wu"] = _ffn_in_weights(got["ffn2_w_in"])
        F2["wout"] = _ffn_out_weights(got["ffn2_w_out"])
        h2 = after_start(h2, token)
        g2, u2, a2 = _ffn_in(h2, F2["wg"], F2["wu"])
        x3 = _mm_res(a2, F2["wout"], x2, 0.5, "ffn_out", tm=512, tn=D)
        saved.append((s1, (x1, hm, p, ycat, hs), (x2, h2, g2, u2, a2)))
        xs = x3

    dx, dxb, d_final, loss_part = _final(xs, loss_target[0], final_norm[None], 0.5)

    G = {n: [None] * depth for n in SMALL if n != "final_norm"}
    lands = {n: lax.empty((N_DEV,) + LAND_SHAPES[n], BF16) for n in BIG}
    in_flight = []

    def send_grads(l, windows, tag):
        names = list(windows)
        send, recv, thru, token = _exchange_start(l, [windows[n] for n in names], [lands[n] for n in names],
                                                  "grads_start_" + tag)
        lands.update(zip(names, thru))
        in_flight.append((l, names, [windows[n] for n in names], send, recv, "grads_wait_" + tag))
        return token

    def ffn_bwd(dx, dxb, F, s, f, l, pending, send_now):
        xin, h, g, u, a = s
        dg, du = _ffn_mid_bwd(dxb, F["wout"], g, u)
        dwout = _mm_tn(a, dxb, 1.0, "ffn_dwout")
        dwg = _mm_tn(h, dg, 1.0, "ffn_dwg")
        dwu = _mm_tn(h, du, 1.0, "ffn_dwu")
        pending[f + "_w_in"], pending[f + "_w_out"] = _ffn_grad_windows(dwg, dwu, dwout)
        if send_now:
            dg = _tie(dg, send_grads(l, pending, "l%d_%s" % (l, f)))
        dh = _mm2_nt(dg, F["wg"], du, F["wu"], "ffn_dh", tm=512, tn=D)
        dx, dxb, dn = _rms_bwd(xin, dh, F["norm"], dx, 1.0 if f == "ffn2" else 0.5)
        G[f + "_norm"][l] = dn[0]
        return dx, dxb

    for l in reversed(range(depth)):
        L = layers[l]
        s1, (x1, hm, p, ycat, hs), s2 = saved[l]
        pending = {}
        dx, dxb = ffn_bwd(dx, dxb, L["ffn2"], s2, "ffn2", l, pending, l == 0)
        if l == 0:
            pending = {}
        dycat = _mm_nt(dxb, L["w_out"], "mix_dy", tm=512, tn=D)
        dwo = _mm_tn(ycat, dxb, 1.0, "mix_dwout")
        mg = _mix_bwd(dycat, p, hs, L["mix"])
        dwin = _mm_tn(hm, mg["dp"], 1.0, "mix_dwin")
        pending["w_in"], pending["w_out"] = _mix_grad_windows(jnp.stack(jnp.split(dwin, N_CHIP, axis=1)), dwo)
        if l == 0:
            dp = _tie(mg["dp"], send_grads(l, pending, "l0_mix"))
            pending = {}
        else:
            dp = mg["dp"]
        dhm = _mm_nt(dp, L["w_in"], "mix_dh", tm=512, tn=D)
        dx, dxb, dn = _rms_bwd(x1, dhm, L["mix_norm"], dx, 0.5)
        G["mix_norm"][l] = dn[0]
        G["conv_w"][l], G["conv_b"][l] = mg["conv_w"], mg["conv_b"][0]
        G["rg_w_a"][l] = _diag_blocks(mg["wax"][:, :D_RNN], 8, 64)
        G["rg_w_x"][l] = _diag_blocks(mg["wax"][:, D_RNN:], 8, 64)
        G["rg_b_a"][l] = mg["bax"][0, :D_RNN].reshape(8, 64)
        G["rg_b_x"][l] = mg["bax"][0, D_RNN:].reshape(8, 64)
        G["lru_lambda"][l] = mg["lam"][0]
        G["pool_w"][l] = _diag_blocks(mg["wpool"], 4, 64)
        G["pool_scale"][l], G["sgu_norm"][l] = mg["pool_scale"][0], mg["sgu_norm"][0]
        G["sgu_w"][l] = mg["ws"]
        G["sgu_b"][l] = mg["bz"].reshape(CHUNK, 4, 64).sum(-1).T
        dx, dxb = ffn_bwd(dx, dxb, L["ffn1"], s1, "ffn1", l, pending, l == 0)
        if l > 0:
            dxb = _tie(dxb, send_grads(l, pending, "l%d" % l))
    grad_x = dx[None]
    G = {n: jnp.stack(v) for n, v in G.items()}
    G["final_norm"] = d_final[0]

    for l, names, windows, send, recv, tag in in_flight:
        lands.update(zip(names, _exchange_wait(l, windows, [lands[n] for n in names], send, recv, dx, tag)))
    both = [_sum_share(lands[n], "sum_share_" + n) for n in BIG]
    grads = {n: (a[:, :, :FF_SHARD] if n.endswith("w_in") and n != "w_in" else a) for n, a in zip(BIG, both)}

    small_sizes = [int(np.prod(G[n].shape)) for n in SMALL]
    srows = _round_up(sum(small_sizes) + 1, N_DEV * 8 * LANES) // (N_DEV * LANES)
    sflat = _flat_rows([G[n] for n in SMALL] + [loss_part[0, :1]], N_DEV * srows)
    sgot = _all_to_all(sflat.reshape(N_DEV, srows, LANES), "exchange_small_grads")
    sall = _all_gather8(_sum8(sgot, "sum_small_grads"), "share_small_grads").reshape(-1)
    off = 0
    for n, size in zip(SMALL, small_sizes):
        grads[n] = sall[off:off + size].reshape(G[n].shape)
        off += size
    loss = sall[off]
    grads["conv_w"] = lax.dynamic_slice_in_dim(grads["conv_w"], chip * conv_w.shape[2], conv_w.shape[2], axis=2)

    delta, new_m, new_v = {}, {}, {}
    for n in BIG:
        shp = W[n].shape
        two_d = (shp[0] * shp[1], shp[2])
        outs = _adamw(W[n].reshape(two_d), grads[n].reshape(two_d), M[n].reshape(two_d), V[n].reshape(two_d),
                      "adamw_" + n)
        delta[n], new_m[n], new_v[n] = (o.reshape(shp) for o in outs)
    arows = _round_up(sum(int(np.prod(W[n].shape)) for n in SMALL), 8 * LANES) // LANES
    outs = _adamw(*(_flat_rows([src[n] for n in SMALL], arows) for src in (W, grads, M, V)), "adamw_small")
    outs = [o.reshape(-1) for o in outs]
    off = 0
    for n in SMALL:
        size = int(np.prod(W[n].shape))
        delta[n], new_m[n], new_v[n] = (o[off:off + size].reshape(W[n].shape) for o in outs)
        off += size

    return (loss, grad_x, *[grads[n] for n in WEIGHTS], *[delta[n] for n in WEIGHTS],
            *[new_m[n] for n in WEIGHTS], *[new_v[n] for n in WEIGHTS])
```

```python
import math

import jax
import jax.numpy as jnp
import numpy as np
from jax import lax
from jax.experimental import pallas as pl
from jax.experimental.pallas import tpu as pltpu

F32 = jnp.float32
BF16 = jnp.bfloat16
MESH = pl.DeviceIdType.MESH

D = 1024
D_RNN = 512
D_POOL = 256
D_SGU = 256
D_IN = 1792
D_FF = 2752
D_FFP = 2816
N_CHIP = 4
FF_SHARD = D_FF // 2
FF_HALF = D_FFP // 2
FF_ROWS = D_FF // N_CHIP
CHUNK = 128
HALO = 16
EPS = 1e-6
LRU_C = 8.0
N_DEV = 8
LANES = 1024
VMEM_LIMIT = 56 * 1024 * 1024

ADAM_LR, ADAM_B1, ADAM_B2, ADAM_EPS, ADAM_WD, ADAM_STEP = 0.001, 0.9, 0.999, 1e-08, 0.01, 10

BIG = ("ffn1_w_in", "ffn1_w_out", "w_in", "w_out", "ffn2_w_in", "ffn2_w_out")
SMALL = ("ffn1_norm", "mix_norm", "conv_w", "conv_b", "rg_w_a", "rg_b_a", "rg_w_x", "rg_b_x", "lru_lambda",
         "pool_w", "pool_scale", "sgu_norm", "sgu_w", "sgu_b", "ffn2_norm", "final_norm")
WEIGHTS = ("ffn1_norm", "ffn1_w_in", "ffn1_w_out", "mix_norm", "w_in", "conv_w", "conv_b", "rg_w_a", "rg_b_a",
           "rg_w_x", "rg_b_x", "lru_lambda", "pool_w", "pool_scale", "sgu_norm", "sgu_w", "sgu_b", "w_out",
           "ffn2_norm", "ffn2_w_in", "ffn2_w_out", "final_norm")


def _params(*sem):
    return pltpu.CompilerParams(dimension_semantics=sem, vmem_limit_bytes=VMEM_LIMIT)


def _gelu(x):
    c = math.sqrt(2.0 / math.pi)
    t = jnp.tanh(c * (x + 0.044715 * (x * x * x)))
    return 0.5 * x * (1.0 + t)


def _gelu_and_grad(x):
    c = math.sqrt(2.0 / math.pi)
    x2 = x * x
    t = jnp.tanh(c * (x + 0.044715 * (x2 * x)))
    g = 0.5 * x * (1.0 + t)
    dg = 0.5 * (1.0 + t) + 0.5 * x * (1.0 - t * t) * (c * (1.0 + 3.0 * 0.044715 * x2))
    return g, dg


def _sigmoid(x):
    return 0.5 * jnp.tanh(0.5 * x) + 0.5


def _dot(a, b):
    return jnp.dot(a, b, preferred_element_type=F32)


def _dot_tn(a, b):
    return lax.dot_general(a, b, (((0,), (0,)), ((), ())), preferred_element_type=F32)


def _dot_nt(a, b):
    return lax.dot_general(a, b, (((1,), (1,)), ((), ())), preferred_element_type=F32)


def _rms_fwd(x, g, tm=512):
    T = x.shape[0]
    tm = min(tm, T)

    def body(x_ref, g_ref, o_ref):
        xv = x_ref[...]
        r = lax.rsqrt(jnp.mean(xv * xv, axis=-1, keepdims=True) + EPS)
        o_ref[...] = (xv * r * g_ref[...]).astype(BF16)

    return pl.pallas_call(
        body, name="rms_fwd", grid=(T // tm,),
        in_specs=[pl.BlockSpec((tm, D), lambda i: (i, 0)), pl.BlockSpec((1, D), lambda i: (0, 0))],
        out_specs=pl.BlockSpec((tm, D), lambda i: (i, 0)),
        out_shape=jax.ShapeDtypeStruct((T, D), BF16),
        compiler_params=_params("parallel"),
    )(x, g)


def _tile(n, limit):
    best = 128
    for t in range(128, min(n, limit) + 1, 128):
        if n % t == 0:
            best = t
    assert n % best == 0, (n, limit)
    return best


def _mm(a, b, out_dtype, name, tm=1024, tn=512):
    M, K = a.shape
    N = b.shape[1]
    tm, tn = min(tm, M), _tile(N, tn)

    def body(a_ref, b_ref, o_ref):
        o_ref[...] = _dot(a_ref[...], b_ref[...]).astype(out_dtype)

    return pl.pallas_call(
        body, name=name, grid=(M // tm, N // tn),
        in_specs=[pl.BlockSpec((tm, K), lambda i, j: (i, 0)), pl.BlockSpec((K, tn), lambda i, j: (0, j))],
        out_specs=pl.BlockSpec((tm, tn), lambda i, j: (i, j)),
        out_shape=jax.ShapeDtypeStruct((M, N), out_dtype),
        compiler_params=_params("parallel", "parallel"),
    )(a, b)


def _mm_res(a, b, res, scale, name, tm=1024, tn=512):
    M, K = a.shape
    N = b.shape[1]
    tm, tn = min(tm, M), _tile(N, tn)

    def body(a_ref, b_ref, r_ref, o_ref):
        o_ref[...] = r_ref[...] + scale * _dot(a_ref[...], b_ref[...])

    return pl.pallas_call(
        body, name=name, grid=(M // tm, N // tn),
        in_specs=[pl.BlockSpec((tm, K), lambda i, j: (i, 0)), pl.BlockSpec((K, tn), lambda i, j: (0, j)),
                  pl.BlockSpec((tm, tn), lambda i, j: (i, j))],
        out_specs=pl.BlockSpec((tm, tn), lambda i, j: (i, j)),
        out_shape=jax.ShapeDtypeStruct((M, N), F32),
        compiler_params=_params("parallel", "parallel"),
    )(a, b, res)


def _mm_nt(a, b, name, tm=1024, tn=512):
    M, K = a.shape
    N = b.shape[0]
    tm, tn = min(tm, M), _tile(N, tn)

    def body(a_ref, b_ref, o_ref):
        o_ref[...] = _dot_nt(a_ref[...], b_ref[...])

    return pl.pallas_call(
        body, name=name, grid=(M // tm, N // tn),
        in_specs=[pl.BlockSpec((tm, K), lambda i, j: (i, 0)), pl.BlockSpec((tn, K), lambda i, j: (j, 0))],
        out_specs=pl.BlockSpec((tm, tn), lambda i, j: (i, j)),
        out_shape=jax.ShapeDtypeStruct((M, N), F32),
        compiler_params=_params("parallel", "parallel"),
    )(a, b)


def _dh_rms_bwd(pairs, x, g, dres, copy_scale, name, tm=512):
    T = x.shape[0]
    tm = min(tm, T)
    n = len(pairs)

    def body(*refs):
        ab = refs[:2 * n]
        x_ref, g_ref, dres_ref, dx_ref, dxb_ref, dg_ref = refs[2 * n:]
        dy = _dot_nt(ab[0][...], ab[1][...])
        for k in range(1, n):
            dy = dy + _dot_nt(ab[2 * k][...], ab[2 * k + 1][...])
        xv = x_ref[...]
        r = lax.rsqrt(jnp.mean(xv * xv, axis=-1, keepdims=True) + EPS)
        xhat = xv * r
        dxhat = dy * g_ref[...]
        dx = dres_ref[...] + r * (dxhat - xhat * jnp.mean(dxhat * xhat, axis=-1, keepdims=True))
        dx_ref[...] = dx
        dxb_ref[...] = (copy_scale * dx).astype(BF16)

        @pl.when(pl.program_id(0) == 0)
        def _():
            dg_ref[...] = jnp.zeros_like(dg_ref)

        dg_ref[...] += jnp.sum(dy * xhat, axis=0, keepdims=True)

    row = pl.BlockSpec((tm, D), lambda i: (i, 0))
    vec = pl.BlockSpec((1, D), lambda i: (0, 0))
    in_specs, operands = [], []
    for a, b in pairs:
        in_specs += [pl.BlockSpec((tm, a.shape[1]), lambda i: (i, 0)),
                     pl.BlockSpec(b.shape, lambda i: (0, 0), pipeline_mode=pl.Buffered(1))]
        operands += [a, b]
    return pl.pallas_call(
        body, name=name, grid=(T // tm,),
        in_specs=in_specs + [row, vec, row], out_specs=[row, row, vec],
        out_shape=[jax.ShapeDtypeStruct((T, D), F32), jax.ShapeDtypeStruct((T, D), BF16),
                   jax.ShapeDtypeStruct((1, D), F32)],
        compiler_params=_params("arbitrary"),
    )(*operands, x, g, dres)


def _mm_tn(a, b, scale, name, tm=1792, tn=1792, tk=2048):
    T, M = a.shape
    N = b.shape[1]
    tm, tn, tk = _tile(M, tm), _tile(N, tn), min(tk, T)
    nk = T // tk

    def body(a_ref, b_ref, o_ref, acc_ref):
        k = pl.program_id(2)

        @pl.when(k == 0)
        def _():
            acc_ref[...] = jnp.zeros_like(acc_ref)

        acc_ref[...] += _dot_tn(a_ref[...], b_ref[...])

        @pl.when(k == nk - 1)
        def _():
            o_ref[...] = (scale * acc_ref[...]).astype(BF16)

    return pl.pallas_call(
        body, name=name, grid=(M // tm, N // tn, nk),
        in_specs=[pl.BlockSpec((tk, tm), lambda i, j, k: (k, i)), pl.BlockSpec((tk, tn), lambda i, j, k: (k, j))],
        out_specs=pl.BlockSpec((tm, tn), lambda i, j, k: (i, j)),
        out_shape=jax.ShapeDtypeStruct((M, N), BF16),
        scratch_shapes=[pltpu.VMEM((tm, tn), F32)],
        compiler_params=_params("parallel", "parallel", "arbitrary"),
    )(a, b)


def _ffn_in(h, wg, wu, tm=512, tn=FF_HALF):
    T = h.shape[0]
    tm = min(tm, T)

    def body(h_ref, wg_ref, wu_ref, g_ref, u_ref, a_ref):
        hv = h_ref[...]
        g = _dot(hv, wg_ref[...])
        u = _dot(hv, wu_ref[...])
        g_ref[...] = g.astype(BF16)
        u_ref[...] = u.astype(BF16)
        a_ref[...] = (g * _sigmoid(g) * u).astype(BF16)

    wspec = pl.BlockSpec((D, tn), lambda j, i: (0, j))
    ospec = pl.BlockSpec((tm, tn), lambda j, i: (i, j))
    oshape = jax.ShapeDtypeStruct((T, D_FFP), BF16)
    return pl.pallas_call(
        body, name="ffn_in", grid=(D_FFP // tn, T // tm),
        in_specs=[pl.BlockSpec((tm, D), lambda j, i: (i, 0)), wspec, wspec],
        out_specs=[ospec, ospec, ospec], out_shape=[oshape, oshape, oshape],
        compiler_params=_params("parallel", "parallel"),
    )(h, wg, wu)


def _ffn_mid_bwd(dyh, wout, g, u, tm=512, tn=FF_HALF):
    T = dyh.shape[0]
    tm = min(tm, T)

    def body(dy_ref, w_ref, g_ref, u_ref, dg_ref, du_ref):
        da = _dot_nt(dy_ref[...], w_ref[...])
        g = g_ref[...].astype(F32)
        s = _sigmoid(g)
        gs = g * s
        du_ref[...] = (da * gs).astype(BF16)
        dg_ref[...] = ((da * u_ref[...].astype(F32)) * (s + gs - gs * s)).astype(BF16)

    ospec = pl.BlockSpec((tm, tn), lambda j, i: (i, j))
    oshape = jax.ShapeDtypeStruct((T, D_FFP), BF16)
    return pl.pallas_call(
        body, name="ffn_mid_bwd", grid=(D_FFP // tn, T // tm),
        in_specs=[pl.BlockSpec((tm, D), lambda j, i: (i, 0)), pl.BlockSpec((tn, D), lambda j, i: (j, 0)),
                  ospec, ospec],
        out_specs=[ospec, ospec], out_shape=[oshape, oshape],
        compiler_params=_params("parallel", "parallel"),
    )(dyh, wout, g, u)


def _final(x, tgt, gf, copy_scale, tm=512):
    T = x.shape[0]
    tm = min(tm, T)

    def body(x_ref, t_ref, g_ref, dx_ref, dxb_ref, dg_ref, loss_ref):
        xv = x_ref[...]
        r = lax.rsqrt(jnp.mean(xv * xv, axis=-1, keepdims=True) + EPS)
        xhat = xv * r
        err = xhat * g_ref[...] - t_ref[...]
        dy = err * (1.0 / D)
        dxhat = dy * g_ref[...]
        dx = r * (dxhat - xhat * jnp.mean(dxhat * xhat, axis=-1, keepdims=True))
        dx_ref[...] = dx
        dxb_ref[...] = (copy_scale * dx).astype(BF16)

        @pl.when(pl.program_id(0) == 0)
        def _():
            dg_ref[...] = jnp.zeros_like(dg_ref)
            loss_ref[...] = jnp.zeros_like(loss_ref)

        dg_ref[...] += jnp.sum(dy * xhat, axis=0, keepdims=True)
        loss_ref[...] += (0.5 / D) * jnp.sum(err * err)

    row = pl.BlockSpec((tm, D), lambda i: (i, 0))
    vec = pl.BlockSpec((1, D), lambda i: (0, 0))
    return pl.pallas_call(
        body, name="final_loss", grid=(T // tm,),
        in_specs=[row, row, vec],
        out_specs=[row, row, vec, pl.BlockSpec((1, 128), lambda i: (0, 0))],
        out_shape=[jax.ShapeDtypeStruct((T, D), F32), jax.ShapeDtypeStruct((T, D), BF16),
                   jax.ShapeDtypeStruct((1, D), F32), jax.ShapeDtypeStruct((1, 128), F32)],
        compiler_params=_params("arbitrary"),
    )(x, tgt, gf)


def _mix_block(T, limit):
    return min(limit, T // 2)


def _rows(tb, width):
    return lax.broadcasted_iota(jnp.int32, (tb, width), 0)


def _rglru_gates(xc, wax_ref, bax_ref, lam_ref):
    pre = _dot(xc.astype(BF16), wax_ref[...]) + bax_ref[...]
    r = _sigmoid(pre[:, :D_RNN])
    ig = _sigmoid(pre[:, D_RNN:])
    z = -lam_ref[...]
    sp = jnp.maximum(z, 0.0) + jnp.log(1.0 + jnp.exp(-jnp.abs(z)))
    log_a = (-LRU_C) * r * sp
    a = jnp.exp(log_a)
    mult = jnp.sqrt(-jnp.tanh(log_a) * (1.0 + a * a))
    return r, ig, sp, a, mult


def _conv(xa_ext, cw_ref, cb_ref):
    y = cb_ref[...] + cw_ref[3:4, :] * xa_ext
    for k in range(1, 4):
        y = y + cw_ref[3 - k:4 - k, :] * pltpu.roll(xa_ext, k, 0)
    return y[HALO:]


def _pool_window_lanes():
    lane = lax.broadcasted_iota(jnp.int32, (1, D_POOL), 1)
    return jnp.where(lane < 64, 2, jnp.where(lane < 128, 4, jnp.where(lane < 192, 8, 16)))


def _pool_select(s2, s4, s8, s16):
    lane = lax.broadcasted_iota(jnp.int32, s2.shape, 1)
    return jnp.where(lane < 64, s2, jnp.where(lane < 128, s4, jnp.where(lane < 192, s8, s16)))


def _pool_diff(xp_ext, t0, tb):
    s2 = xp_ext + pltpu.roll(xp_ext, 1, 0)
    s4 = s2 + pltpu.roll(s2, 2, 0)
    s8 = s4 + pltpu.roll(s4, 4, 0)
    s16 = s8 + pltpu.roll(s8, 8, 0)
    sel = _pool_select(s2, s4, s8, s16)[HALO:]
    cnt = jnp.minimum(t0 + _rows(tb, D_POOL) + 1, _pool_window_lanes()).astype(F32)
    return sel / cnt - xp_ext[HALO:], cnt


def _head_masks():
    lane = lax.broadcasted_iota(jnp.int32, (1, D_SGU), 1)
    return [((lane >= 64 * h) & (lane < 64 * (h + 1))).astype(F32) for h in range(4)]


def _sgu_mix(w_ref, vch, masks):
    z = masks[0] * _dot(w_ref[0], vch)
    for h in range(1, 4):
        z = z + masks[h] * _dot(w_ref[h], vch)
    return z


def _mix_fwd(p, prm):
    T = p.shape[0]
    tb = _mix_block(T, 512)
    nb = T // tb

    def body(p_ref, xah_ref, xph_ref, cw_ref, cb_ref, wax_ref, bax_ref, lam_ref, wp_ref, ps_ref, sgn_ref,
             ws_ref, bz_ref, y_ref, hs_ref, carry_ref):
        i = pl.program_id(0)
        keep = (i > 0).astype(F32)

        @pl.when(i == 0)
        def _():
            carry_ref[...] = jnp.zeros_like(carry_ref)

        xa_ext = jnp.concatenate([xah_ref[...] * keep, p_ref[:, 512:1024]], axis=0)
        xc = _conv(xa_ext, cw_ref, cb_ref)
        r, ig, sp, a, mult = _rglru_gates(xc, wax_ref, bax_ref, lam_ref)
        bv = mult * (ig * xc)
        row = _rows(tb, D_RNN)
        s = 1
        while s < tb:
            m = row >= s
            bv = jnp.where(m, a * pltpu.roll(bv, s, 0) + bv, bv)
            a = jnp.where(m, a * pltpu.roll(a, s, 0), a)
            s *= 2
        h = bv + a * carry_ref[0:1, :]
        hs_ref[...] = h
        last = jnp.sum(jnp.where(_rows(8, D_RNN) == 7, hs_ref[tb - 8:tb, :], 0.0), axis=0, keepdims=True)
        carry_ref[...] = jnp.broadcast_to(last, carry_ref.shape)
        y_ref[:, 0:512] = (_gelu(p_ref[:, 0:512]) * h).astype(BF16)

        xp_ext = jnp.concatenate([xph_ref[...] * keep, p_ref[:, 1024:1280]], axis=0)
        d, _ = _pool_diff(xp_ext, i * tb, tb)
        y_ref[:, 512:768] = (_dot(d.astype(BF16), wp_ref[...]) * ps_ref[...]).astype(BF16)

        ug = _gelu(p_ref[:, 1280:1536])
        vg = _gelu(p_ref[:, 1536:1792])
        rv = lax.rsqrt(jnp.mean(vg * vg, axis=-1, keepdims=True) + EPS)
        vn = (vg * rv * sgn_ref[...]).astype(BF16)
        masks = _head_masks()
        for ci in range(tb // CHUNK):
            sl = slice(ci * CHUNK, (ci + 1) * CHUNK)
            z = _sgu_mix(ws_ref, vn[sl], masks) + bz_ref[...]
            y_ref[sl, 768:1024] = (ug[sl] * z).astype(BF16)

    hb = tb // HALO

    def halo(i):
        return jnp.maximum(i * hb - 1, 0)

    def full(shape):
        return pl.BlockSpec(shape, lambda i: (0,) * len(shape))

    return pl.pallas_call(
        body, name="mix_fwd", grid=(nb,),
        in_specs=[pl.BlockSpec((tb, D_IN), lambda i: (i, 0)),
                  pl.BlockSpec((HALO, D_RNN), lambda i: (halo(i), 1)),
                  pl.BlockSpec((HALO, D_POOL), lambda i: (halo(i), 4)),
                  full((4, D_RNN)), full((1, D_RNN)), full((D_RNN, 2 * D_RNN)), full((1, 2 * D_RNN)),
                  full((1, D_RNN)), full((D_POOL, D_POOL)), full((1, D_POOL)), full((1, D_SGU)),
                  full((4, CHUNK, CHUNK)), full((CHUNK, D_SGU))],
        out_specs=[pl.BlockSpec((tb, D), lambda i: (i, 0)), pl.BlockSpec((tb, D_RNN), lambda i: (i, 0))],
        out_shape=[jax.ShapeDtypeStruct((T, D), BF16), jax.ShapeDtypeStruct((T, D_RNN), F32)],
        scratch_shapes=[pltpu.VMEM((8, D_RNN), F32)],
        compiler_params=_params("arbitrary"),
    )(p, p, p, prm["conv_w"], prm["conv_b"], prm["wax"], prm["bax"], prm["lam"], prm["wpool"], prm["pool_scale"],
      prm["sgu_norm"], prm["ws"], prm["bz"])


def _mix_bwd(dy, p, hs, prm):
    T = p.shape[0]
    tb = _mix_block(T, 256)
    nb = T // tb
    hb = tb // HALO

    def body(dy_ref, p_ref, xah_ref, xph_ref, hs_ref, hsh_ref, cw_ref, cb_ref, wax_ref, waxt_ref, bax_ref,
             lam_ref, wp_ref, wpt_ref, ps_ref, sgn_ref, ws_ref, wst_ref, bz_ref,
             dp_ref, dcw_ref, dcb_ref, dwax_ref, dbax_ref, dlam_ref, dwp_ref, dps_ref, dsgn_ref, dws_ref,
             dbz_ref, gcarry_ref, xcfut_ref, mfut_ref):
        i = pl.program_id(0)
        bi = nb - 1 - i
        keep = (bi > 0).astype(F32)

        @pl.when(i == 0)
        def _():
            for ref in (dcw_ref, dcb_ref, dwax_ref, dbax_ref, dlam_ref, dwp_ref, dps_ref, dsgn_ref, dws_ref,
                        dbz_ref, gcarry_ref, xcfut_ref, mfut_ref):
                ref[...] = jnp.zeros_like(ref)

        xa_ext = jnp.concatenate([xah_ref[...] * keep, p_ref[:, 512:1024]], axis=0)
        xc = _conv(xa_ext, cw_ref, cb_ref)
        r, ig, sp, a, mult = _rglru_gates(xc, wax_ref, bax_ref, lam_ref)
        gg, dgg = _gelu_and_grad(p_ref[:, 0:512])
        dya = dy_ref[:, 0:512]
        dp_ref[:, 0:512] = (dya * hs_ref[...] * dgg).astype(BF16)
        row = _rows(tb, D_RNN)
        g = dya * gg + jnp.where(row == tb - 1, gcarry_ref[0:1, :], 0.0)
        al = pltpu.roll(a, tb - 1, 0)
        s = 1
        while s < tb:
            m = row < tb - s
            g = jnp.where(m, al * pltpu.roll(g, tb - s, 0) + g, g)
            al = jnp.where(m, al * pltpu.roll(al, tb - s, 0), al)
            s *= 2
        first = jnp.sum(jnp.where(_rows(8, D_RNN) == 0, (a * g)[0:8], 0.0), axis=0, keepdims=True)
        gcarry_ref[...] = jnp.broadcast_to(first, gcarry_ref.shape)
        hs_ext = jnp.concatenate([hsh_ref[...] * keep, hs_ref[...]], axis=0)
        h_prev = pltpu.roll(hs_ext, 1, 0)[HALO:]
        ix = ig * xc
        dlog_a = g * h_prev * a - (g * ix) * (a * a / mult)
        dlam_ref[...] += jnp.sum(dlog_a * r, axis=0, keepdims=True) * (LRU_C * _sigmoid(-lam_ref[...]))
        dpre_r = dlog_a * ((-LRU_C) * sp) * (r * (1.0 - r))
        dpre_i = (g * mult * xc) * (ig * (1.0 - ig))
        dpre = jnp.concatenate([dpre_r, dpre_i], axis=1)
        dbax_ref[...] += jnp.sum(dpre, axis=0, keepdims=True)
        dpre_b = dpre.astype(BF16)
        dwax_ref[...] += _dot_tn(xc.astype(BF16), dpre_b)
        dxc = g * mult * ig + _dot(dpre_b, waxt_ref[...])
        dcb_ref[...] += jnp.sum(dxc, axis=0, keepdims=True)
        for k in range(4):
            xs = xa_ext[HALO:] if k == 3 else pltpu.roll(xa_ext, 3 - k, 0)[HALO:]
            dcw_ref[k:k + 1, :] += jnp.sum(dxc * xs, axis=0, keepdims=True)
        dxc_ext = jnp.concatenate([dxc, xcfut_ref[...]], axis=0)
        n = tb + HALO
        dxa = cw_ref[3:4, :] * dxc_ext
        for k in range(1, 4):
            dxa = dxa + cw_ref[3 - k:4 - k, :] * pltpu.roll(dxc_ext, n - k, 0)
        dp_ref[:, 512:1024] = dxa[:tb].astype(BF16)
        xcfut_ref[...] = dxc[0:HALO]

        xp_ext = jnp.concatenate([xph_ref[...] * keep, p_ref[:, 1024:1280]], axis=0)
        d, cnt = _pool_diff(xp_ext, bi * tb, tb)
        db = d.astype(BF16)
        dyb = dy_ref[:, 512:768]
        dps_ref[...] += jnp.sum(dyb * _dot(db, wp_ref[...]), axis=0, keepdims=True)
        dq = (dyb * ps_ref[...]).astype(BF16)
        dwp_ref[...] += _dot_tn(db, dq)
        dd = _dot(dq, wpt_ref[...])
        mm = dd / cnt
        m_ext = jnp.concatenate([mm, mfut_ref[...]], axis=0)
        f2 = m_ext + pltpu.roll(m_ext, n - 1, 0)
        f4 = f2 + pltpu.roll(f2, n - 2, 0)
        f8 = f4 + pltpu.roll(f4, n - 4, 0)
        f16 = f8 + pltpu.roll(f8, n - 8, 0)
        dp_ref[:, 1024:1280] = (_pool_select(f2, f4, f8, f16)[:tb] - dd).astype(BF16)
        mfut_ref[...] = mm[0:HALO]

        ug, dug = _gelu_and_grad(p_ref[:, 1280:1536])
        vg, dvg = _gelu_and_grad(p_ref[:, 1536:1792])
        rv = lax.rsqrt(jnp.mean(vg * vg, axis=-1, keepdims=True) + EPS)
        vhat = vg * rv
        vn = (vhat * sgn_ref[...]).astype(BF16)
        dyc = dy_ref[:, 768:1024]
        masks = _head_masks()
        dz = dyc * ug
        dzb = dz.astype(BF16)
        dvn_parts = []
        for ci in range(tb // CHUNK):
            sl = slice(ci * CHUNK, (ci + 1) * CHUNK)
            z = _sgu_mix(ws_ref, vn[sl], masks) + bz_ref[...]
            dp_ref[sl, 1280:1536] = (dyc[sl] * z * dug[sl]).astype(BF16)
            dbz_ref[...] += dz[sl]
            for h in range(4):
                dws_ref[h] += _dot_nt((dz[sl] * masks[h]).astype(BF16), vn[sl])
            dvn_parts.append(_sgu_mix(wst_ref, dzb[sl], masks))
        dvn = jnp.concatenate(dvn_parts, axis=0)
        dsgn_ref[...] += jnp.sum(dvn * vhat, axis=0, keepdims=True)
        dvhat = dvn * sgn_ref[...]
        dvg_in = rv * (dvhat - vhat * jnp.mean(dvhat * vhat, axis=-1, keepdims=True))
        dp_ref[:, 1536:1792] = (dvg_in * dvg).astype(BF16)

        @pl.when(i == nb - 1)
        def _():
            tril = (lax.broadcasted_iota(jnp.int32, (CHUNK, CHUNK), 0)
                    >= lax.broadcasted_iota(jnp.int32, (CHUNK, CHUNK), 1)).astype(F32)
            for h in range(4):
                dws_ref[h] = dws_ref[h] * tril

    def blk(i):
        return nb - 1 - i

    def halo(i):
        return jnp.maximum(blk(i) * hb - 1, 0)

    def full(shape):
        return pl.BlockSpec(shape, lambda i: (0,) * len(shape))

    small_shapes = [(4, D_RNN), (1, D_RNN), (D_RNN, 2 * D_RNN), (1, 2 * D_RNN), (1, D_RNN), (D_POOL, D_POOL),
                    (1, D_POOL), (1, D_SGU), (4, CHUNK, CHUNK), (CHUNK, D_SGU)]
    outs = pl.pallas_call(
        body, name="mix_bwd", grid=(nb,),
        in_specs=[pl.BlockSpec((tb, D), lambda i: (blk(i), 0)),
                  pl.BlockSpec((tb, D_IN), lambda i: (blk(i), 0)),
                  pl.BlockSpec((HALO, D_RNN), lambda i: (halo(i), 1)),
                  pl.BlockSpec((HALO, D_POOL), lambda i: (halo(i), 4)),
                  pl.BlockSpec((tb, D_RNN), lambda i: (blk(i), 0)),
                  pl.BlockSpec((HALO, D_RNN), lambda i: (halo(i), 0)),
                  full((4, D_RNN)), full((1, D_RNN)), full((D_RNN, 2 * D_RNN)), full((2 * D_RNN, D_RNN)),
                  full((1, 2 * D_RNN)), full((1, D_RNN)), full((D_POOL, D_POOL)), full((D_POOL, D_POOL)),
                  full((1, D_POOL)), full((1, D_SGU)), full((4, CHUNK, CHUNK)), full((4, CHUNK, CHUNK)),
                  full((CHUNK, D_SGU))],
        out_specs=[pl.BlockSpec((tb, D_IN), lambda i: (blk(i), 0))] + [full(s) for s in small_shapes],
        out_shape=[jax.ShapeDtypeStruct((T, D_IN), BF16)] + [jax.ShapeDtypeStruct(s, F32) for s in small_shapes],
        scratch_shapes=[pltpu.VMEM((8, D_RNN), F32), pltpu.VMEM((HALO, D_RNN), F32),
                        pltpu.VMEM((HALO, D_POOL), F32)],
        compiler_params=_params("arbitrary"),
    )(dy, p, p, p, hs, hs, prm["conv_w"], prm["conv_b"], prm["wax"], prm["wax_t"], prm["bax"], prm["lam"],
      prm["wpool"], prm["wpool_t"], prm["pool_scale"], prm["sgu_norm"], prm["ws"], prm["ws_t"], prm["bz"])
    names = ("dp", "conv_w", "conv_b", "wax", "bax", "lam", "wpool", "pool_scale", "sgu_norm", "ws", "bz")
    return dict(zip(names, outs))


ANY = pl.BlockSpec(memory_space=pl.ANY)


def _place():
    x, y, c = lax.axis_index("x"), lax.axis_index("y"), lax.axis_index("c")
    return x, y, c


def _all_to_all(xs, name):
    def body(in_ref, out_ref, send_sems, recv_sems, local_sem):
        x, y, c = _place()
        me = 4 * x + 2 * y + c
        mine = pltpu.make_async_copy(in_ref.at[me], out_ref.at[me], local_sem)
        mine.start()
        copies = []
        for rel in range(1, N_DEV):
            tx = 1 - x if rel & 4 else x
            ty = 1 - y if rel & 2 else y
            tc = 1 - c if rel & 1 else c
            cp = pltpu.make_async_remote_copy(
                src_ref=in_ref.at[4 * tx + 2 * ty + tc], dst_ref=out_ref.at[me],
                send_sem=send_sems.at[rel - 1], recv_sem=recv_sems.at[rel - 1],
                device_id=(tx, ty, tc), device_id_type=MESH)
            cp.start()
            copies.append(cp)
        for cp in copies:
            cp.wait()
        mine.wait()

    return pl.pallas_call(
        body, name=name, in_specs=[ANY], out_specs=ANY,
        out_shape=jax.ShapeDtypeStruct(xs.shape, xs.dtype),
        scratch_shapes=[pltpu.SemaphoreType.DMA((N_DEV - 1,)), pltpu.SemaphoreType.DMA((N_DEV - 1,)),
                        pltpu.SemaphoreType.DMA],
    )(xs)


def _all_gather8(xs, name):
    def body(x_ref, out_ref, send_sems, recv_sems, local_sem):
        x, y, c = _place()
        me, sibling = (x, y, c), (x, y, 1 - c)
        chips = [(1 - x, y), (x, 1 - y), (1 - x, 1 - y)]

        def rows(px, py, pc):
            return out_ref.at[4 * px + 2 * py + pc]

        def copy(k, block, to, src=None):
            return pltpu.make_async_remote_copy(
                src_ref=rows(*block) if src is None else src, dst_ref=rows(*block),
                send_sem=send_sems.at[k], recv_sem=recv_sems.at[k], device_id=to, device_id_type=MESH)

        mine = pltpu.make_async_copy(x_ref, rows(*me), local_sem)
        mine.start()
        first = [copy(0, me, sibling, src=x_ref)]
        first += [copy(1 + j, me, (*chip, c), src=x_ref) for j, chip in enumerate(chips)]
        for cp in first:
            cp.start()
        passed = [copy(4 + j, (*chip, c), sibling) for j, chip in enumerate(chips)]
        for j, chip in enumerate(chips):
            copy(1 + j, (*chip, c), me).wait_recv()
            passed[j].start()
        copy(0, sibling, me).wait_recv()
        for j, chip in enumerate(chips):
            copy(4 + j, (*chip, 1 - c), me).wait_recv()
        for cp in first + passed:
            cp.wait_send()
        mine.wait()

    return pl.pallas_call(
        body, name=name, in_specs=[ANY], out_specs=ANY,
        out_shape=jax.ShapeDtypeStruct((N_DEV,) + xs.shape, xs.dtype),
        scratch_shapes=[pltpu.SemaphoreType.DMA((7,)), pltpu.SemaphoreType.DMA((7,)), pltpu.SemaphoreType.DMA],
    )(xs)


HBM = pl.BlockSpec(memory_space=pltpu.HBM)
SEM = pl.BlockSpec(memory_space=pltpu.SEMAPHORE)
EFFECT = pltpu.SideEffectType.DATAFLOW_SIDE_EFFECTING


def _in_hbm(a):
    return pltpu.with_memory_space_constraint(a, pltpu.HBM)


def _local_copy(src, dst, stage, sem):
    load = pltpu.make_async_copy(src, stage, sem)
    load.start()
    load.wait()
    store = pltpu.make_async_copy(stage, dst, sem)
    store.start()
    store.wait()


def _unique(windows):
    arrays = []
    for per_chip in windows:
        for arr, _ in per_chip:
            if not any(arr is a for a in arrays):
                arrays.append(arr)
    return arrays


def _exchange_start(layer, windows, lands, name):
    arrays = _unique(windows)
    na, nt = len(arrays), len(windows)

    def body(*refs):
        in_refs, land_refs = refs[:na], refs[na:na + nt]
        send_sems, recv_sems = refs[na + nt], refs[na + nt + 1]
        token = refs[-1]
        x, y, c = _place()
        me = 4 * x + 2 * y + c
        for t in range(nt):
            for j in range(N_CHIP):
                arr, window = windows[t][j]
                src = window(in_refs[next(i for i, a in enumerate(arrays) if a is arr)])

                @pl.when(me != 2 * j + layer)
                def _():
                    pltpu.make_async_remote_copy(
                        src_ref=src, dst_ref=land_refs[t].at[me], send_sem=send_sems.at[N_CHIP * t + j],
                        recv_sem=recv_sems.at[N_DEV * t + me], device_id=(j // 2, j % 2, layer),
                        device_id_type=MESH).start()
        token[...] = jnp.zeros_like(token)

    outs = pl.pallas_call(
        body, name=name,
        out_shape=(pltpu.SemaphoreType.DMA((N_CHIP * nt,)), pltpu.SemaphoreType.DMA((N_DEV * nt,)),
                   *[pltpu.HBM(a.shape, a.dtype) for a in lands], jax.ShapeDtypeStruct((8, 128), F32)),
        in_specs=[HBM] * (na + nt),
        out_specs=(SEM, SEM, *[HBM] * nt, pl.BlockSpec(memory_space=pltpu.VMEM)),
        input_output_aliases={na + t: 2 + t for t in range(nt)},
        compiler_params=pltpu.CompilerParams(has_side_effects=EFFECT, vmem_limit_bytes=VMEM_LIMIT),
    )(*[_in_hbm(a) for a in arrays], *[_in_hbm(a) for a in lands])
    return outs[0], outs[1], list(outs[2:2 + nt]), outs[-1]


def _exchange_wait(layer, windows, lands, send_sems, recv_sems, after, name):
    arrays = _unique(windows)
    na, nt = len(arrays), len(windows)

    def body(*refs):
        in_refs, land_refs = refs[:na], refs[na:na + nt]
        send_sems, recv_sems = refs[na + nt], refs[na + nt + 1]
        stages, local_sem = refs[-1 - nt:-1], refs[-1]
        x, y, c = _place()
        me = 4 * x + 2 * y + c

        def source(t, j):
            arr, window = windows[t][j]
            return window(in_refs[next(i for i, a in enumerate(arrays) if a is arr)])

        @pl.when(c == layer)
        def _():
            for t in range(nt):
                for j in range(N_CHIP):
                    @pl.when(me == 2 * j + layer)
                    def _():
                        _local_copy(source(t, j), land_refs[t].at[me], stages[t], local_sem)

        for t in range(nt):
            for j in range(N_CHIP):
                @pl.when(me != 2 * j + layer)
                def _():
                    pltpu.make_async_remote_copy(
                        src_ref=source(t, j), dst_ref=land_refs[t].at[me], send_sem=send_sems.at[N_CHIP * t + j],
                        recv_sem=recv_sems.at[N_DEV * t + me], device_id=(j // 2, j % 2, layer),
                        device_id_type=MESH).wait_send()

        @pl.when(c == layer)
        def _():
            for t in range(nt):
                for s in range(N_DEV):
                    @pl.when(me != s)
                    def _():
                        slot = land_refs[t].at[s]
                        pltpu.make_async_remote_copy(
                            src_ref=slot, dst_ref=slot, send_sem=send_sems.at[N_CHIP * t],
                            recv_sem=recv_sems.at[N_DEV * t + s], device_id=(x, y, c),
                            device_id_type=MESH).wait_recv()

    outs = pl.pallas_call(
        body, name=name,
        out_shape=tuple(pltpu.HBM(a.shape, a.dtype) for a in lands),
        in_specs=[HBM] * (na + nt) + [SEM, SEM, ANY],
        out_specs=tuple([HBM] * nt),
        input_output_aliases={na + t: t for t in range(nt)},
        scratch_shapes=[pltpu.VMEM(a.shape[1:], a.dtype) for a in lands] + [pltpu.SemaphoreType.DMA],
        compiler_params=pltpu.CompilerParams(has_side_effects=EFFECT, vmem_limit_bytes=VMEM_LIMIT),
    )(*[_in_hbm(a) for a in arrays], *lands, send_sems, recv_sems, after)
    return list(outs)


def _other_chips(x, y):
    return [(1 - x, y), (x, 1 - y), (1 - x, 1 - y)]


def _gather_start(layer, shards, lands, name):
    nt = len(shards)

    def body(*refs):
        in_refs, land_refs = refs[:nt], refs[nt:2 * nt]
        send_sems, recv_sems = refs[2 * nt], refs[2 * nt + 1]
        token = refs[-1]
        x, y, c = _place()

        @pl.when(c == layer)
        def _():
            for t in range(nt):
                for rel, (tx, ty) in enumerate(_other_chips(x, y)):
                    for tc in range(2):
                        pltpu.make_async_remote_copy(
                            src_ref=in_refs[t], dst_ref=land_refs[t].at[2 * x + y],
                            send_sem=send_sems.at[6 * t + 2 * rel + tc], recv_sem=recv_sems.at[3 * t + rel],
                            device_id=(tx, ty, tc), device_id_type=MESH).start()

        token[...] = jnp.zeros_like(token)

    outs = pl.pallas_call(
        body, name=name,
        out_shape=(pltpu.SemaphoreType.DMA((6 * nt,)), pltpu.SemaphoreType.DMA((3 * nt,)),
                   *[pltpu.HBM(a.shape, a.dtype) for a in lands], jax.ShapeDtypeStruct((8, 128), F32)),
        in_specs=[HBM] * (2 * nt),
        out_specs=(SEM, SEM, *[HBM] * nt, pl.BlockSpec(memory_space=pltpu.VMEM)),
        input_output_aliases={nt + t: 2 + t for t in range(nt)},
        compiler_params=pltpu.CompilerParams(has_side_effects=EFFECT, vmem_limit_bytes=VMEM_LIMIT),
    )(*[_in_hbm(a) for a in shards], *[_in_hbm(a) for a in lands])
    return outs[0], outs[1], list(outs[2:2 + nt]), outs[-1]


def _gather_wait(layer, shards, lands, send_sems, recv_sems, after, name):
    nt = len(shards)

    def body(*refs):
        in_refs, land_refs = refs[:nt], refs[nt:2 * nt]
        send_sems, recv_sems = refs[2 * nt], refs[2 * nt + 1]
        stages, local_sem = refs[-1 - nt:-1], refs[-1]
        x, y, c = _place()

        for t in range(nt):
            _local_copy(in_refs[t], land_refs[t].at[2 * x + y], stages[t], local_sem)

        @pl.when(c == layer)
        def _():
            for t in range(nt):
                for rel, (tx, ty) in enumerate(_other_chips(x, y)):
                    for tc in range(2):
                        pltpu.make_async_remote_copy(
                            src_ref=in_refs[t], dst_ref=land_refs[t].at[2 * x + y],
                            send_sem=send_sems.at[6 * t + 2 * rel + tc], recv_sem=recv_sems.at[3 * t + rel],
                            device_id=(tx, ty, tc), device_id_type=MESH).wait_send()

        for t in range(nt):
            for rel, (tx, ty) in enumerate(_other_chips(x, y)):
                slot = land_refs[t].at[2 * tx + ty]
                pltpu.make_async_remote_copy(
                    src_ref=slot, dst_ref=slot, send_sem=send_sems.at[6 * t], recv_sem=recv_sems.at[3 * t + rel],
                    device_id=(x, y, c), device_id_type=MESH).wait_recv()

    outs = pl.pallas_call(
        body, name=name,
        out_shape=tuple(pltpu.HBM(a.shape, a.dtype) for a in lands),
        in_specs=[HBM] * (2 * nt) + [SEM, SEM, ANY],
        out_specs=tuple([HBM] * nt),
        input_output_aliases={nt + t: t for t in range(nt)},
        scratch_shapes=[pltpu.VMEM(a.shape, a.dtype) for a in shards] + [pltpu.SemaphoreType.DMA],
        compiler_params=pltpu.CompilerParams(has_side_effects=EFFECT, vmem_limit_bytes=VMEM_LIMIT),
    )(*[_in_hbm(a) for a in shards], *lands, send_sems, recv_sems, after)
    return list(outs)


def _tie(a, token):
    def body(a_ref, token_ref, o_ref):
        pass

    return pl.pallas_call(
        body, name="tie", in_specs=[ANY, ANY], out_specs=ANY,
        out_shape=jax.ShapeDtypeStruct(a.shape, a.dtype), input_output_aliases={0: 0},
    )(a, token)


def _sum_share(xs, name):
    _, r, cols = xs.shape
    tr = 256 if r % 256 == 0 else r
    nblk = r // tr

    def body(x_ref, out_ref, acc_ref, send_sems, local_sems, recv_sem):
        i = pl.program_id(0)
        slot = i % 2
        x, y, c = _place()

        def copies(s, blk):
            dst = out_ref.at[c, pl.ds(blk * tr, tr), :]
            loc = pltpu.make_async_copy(acc_ref.at[s], dst, local_sems.at[s])
            rem = pltpu.make_async_remote_copy(src_ref=acc_ref.at[s], dst_ref=dst, send_sem=send_sems.at[s],
                                               recv_sem=recv_sem, device_id=(x, y, 1 - c), device_id_type=MESH)
            return loc, rem

        @pl.when(i >= 2)
        def _():
            loc, rem = copies(slot, i - 2)
            loc.wait()
            rem.wait_send()

        acc = x_ref[0].astype(F32)
        for k in range(1, N_DEV):
            acc = acc + x_ref[k].astype(F32)
        acc_ref[slot] = acc
        loc, rem = copies(slot, i)
        loc.start()
        rem.start()

        @pl.when(i == nblk - 1)
        def _():
            for back in range(min(2, nblk)):
                blk = nblk - 1 - back
                loc, rem = copies(blk % 2, blk)
                loc.wait()
                rem.wait_send()
            theirs = out_ref.at[1 - c]
            pltpu.make_async_remote_copy(src_ref=theirs, dst_ref=theirs, send_sem=send_sems.at[0],
                                         recv_sem=recv_sem, device_id=(x, y, 1 - c),
                                         device_id_type=MESH).wait_recv()

    return pl.pallas_call(
        body, name=name, grid=(nblk,),
        in_specs=[pl.BlockSpec((N_DEV, tr, cols), lambda i: (0, i, 0))],
        out_specs=ANY,
        out_shape=jax.ShapeDtypeStruct((2, r, cols), F32),
        scratch_shapes=[pltpu.VMEM((2, tr, cols), F32), pltpu.SemaphoreType.DMA((2,)),
                        pltpu.SemaphoreType.DMA((2,)), pltpu.SemaphoreType.DMA],
        compiler_params=_params("arbitrary"),
    )(xs)


def _sum8(xs, name):
    _, r, cols = xs.shape
    tr = 8
    for cand in (256, 128, 64, 32, 16):
        if r % cand == 0:
            tr = cand
            break

    def body(x_ref, o_ref):
        acc = x_ref[0].astype(F32)
        for k in range(1, N_DEV):
            acc = acc + x_ref[k].astype(F32)
        o_ref[...] = acc

    return pl.pallas_call(
        body, name=name, grid=(r // tr,),
        in_specs=[pl.BlockSpec((N_DEV, tr, cols), lambda i: (0, i, 0))],
        out_specs=pl.BlockSpec((tr, cols), lambda i: (i, 0)),
        out_shape=jax.ShapeDtypeStruct((r, cols), F32),
        compiler_params=_params("parallel"),
    )(xs)


def _adamw(w, g, m, v, name):
    R, C = w.shape
    tr = R
    for cand in (256, 128, 64, 32, 16, 8):
        if R % cand == 0:
            tr = cand
            break
    c1 = 1.0 / (1.0 - ADAM_B1 ** ADAM_STEP)
    c2 = 1.0 / (1.0 - ADAM_B2 ** ADAM_STEP)

    def body(w_ref, g_ref, m_ref, v_ref, d_ref, nm_ref, nv_ref):
        gv = g_ref[...]
        nm = ADAM_B1 * m_ref[...] + (1.0 - ADAM_B1) * gv
        nv = ADAM_B2 * v_ref[...] + (1.0 - ADAM_B2) * (gv * gv)
        d_ref[...] = (-ADAM_LR) * ((nm * c1) / (jnp.sqrt(nv * c2) + ADAM_EPS) + ADAM_WD * w_ref[...])
        nm_ref[...] = nm
        nv_ref[...] = nv

    spec = pl.BlockSpec((tr, C), lambda i: (i, 0))
    shape = jax.ShapeDtypeStruct((R, C), F32)
    return pl.pallas_call(
        body, name=name, grid=(R // tr,), in_specs=[spec] * 4, out_specs=[spec] * 3, out_shape=[shape] * 3,
        compiler_params=_params("parallel"),
    )(w, g, m, v)


def _flat_rows(parts, rows):
    flat = jnp.concatenate([q.reshape(-1) for q in parts])
    flat = jnp.pad(flat, (0, rows * LANES - flat.shape[0]))
    return flat.reshape(rows, LANES)


def _round_up(n, m):
    return (n + m - 1) // m * m


def _block_diag(w):
    H, n, _ = w.shape
    eye = jnp.eye(H, dtype=w.dtype)
    return (eye[:, None, :, None] * w[:, :, None, :]).reshape(H * n, H * n)


def _diag_blocks(w, H, n):
    w4 = w.reshape(H, n, H, n)
    return jnp.stack([w4[h, :, h, :] for h in range(H)])


def _ffn_in_weights(g_in):
    pad = lambda a: jnp.pad(a, ((0, 0), (0, FF_HALF - FF_SHARD)))
    wg = jnp.concatenate([pad(g_in[0]), pad(g_in[1])], axis=1)
    wu = jnp.concatenate([pad(g_in[2]), pad(g_in[3])], axis=1)
    return wg, wu


def _ffn_out_weights(g_out):
    zeros = jnp.zeros((FF_HALF - FF_SHARD, D), g_out.dtype)
    return jnp.concatenate([g_out[0], g_out[1], zeros, g_out[2], g_out[3], zeros], axis=0)


LAND_SHAPES = {"ffn1_w_in": (D, FF_HALF), "ffn1_w_out": (FF_ROWS, D), "w_in": (D, D_IN // N_CHIP),
               "w_out": (D // N_CHIP, D), "ffn2_w_in": (D, FF_HALF), "ffn2_w_out": (FF_ROWS, D)}


def _ffn_grad_windows(dwg, dwu, dwout):
    w_in, w_out = [], []
    for j in range(N_CHIP):
        col = (j % 2) * FF_HALF
        row = (j // 2) * FF_HALF + (j % 2) * FF_ROWS
        w_in.append((dwg if j < 2 else dwu, lambda r, col=col: r.at[:, pl.ds(col, FF_HALF)]))
        w_out.append((dwout, lambda r, row=row: r.at[pl.ds(row, FF_ROWS), :]))
    return w_in, w_out


def _mix_grad_windows(dwin, dwo):
    rows = D // N_CHIP
    win = [(dwin, lambda r, j=j: r.at[j]) for j in range(N_CHIP)]
    wo = [(dwo, lambda r, j=j: r.at[pl.ds(j * rows, rows), :]) for j in range(N_CHIP)]
    return win, wo


def kernel(x, ffn1_norm, ffn1_w_in, ffn1_w_out, mix_norm, w_in, conv_w, conv_b, rg_w_a, rg_b_a, rg_w_x, rg_b_x, lru_lambda, pool_w, pool_scale, sgu_norm, sgu_w, sgu_b, w_out, ffn2_norm, ffn2_w_in, ffn2_w_out, final_norm, loss_target, m_ffn1_norm, m_ffn1_w_in, m_ffn1_w_out, m_mix_norm, m_w_in, m_conv_w, m_conv_b, m_rg_w_a, m_rg_b_a, m_rg_w_x, m_rg_b_x, m_lru_lambda, m_pool_w, m_pool_scale, m_sgu_norm, m_sgu_w, m_sgu_b, m_w_out, m_ffn2_norm, m_ffn2_w_in, m_ffn2_w_out, m_final_norm, v_ffn1_norm, v_ffn1_w_in, v_ffn1_w_out, v_mix_norm, v_w_in, v_conv_w, v_conv_b, v_rg_w_a, v_rg_b_a, v_rg_w_x, v_rg_b_x, v_lru_lambda, v_pool_w, v_pool_scale, v_sgu_norm, v_sgu_w, v_sgu_b, v_w_out, v_ffn2_norm, v_ffn2_w_in, v_ffn2_w_out, v_final_norm):
    args = locals()
    W = {n: args[n] for n in WEIGHTS}
    M = {n: args["m_" + n] for n in WEIGHTS}
    V = {n: args["v_" + n] for n in WEIGHTS}
    depth = ffn1_norm.shape[0]
    T = x.shape[1]
    xi, yi, ci = _place()
    chip = 2 * xi + yi

    assert depth == 2, "core c of a chip sends and reduces layer c"
    groups = [(l, names) for l in range(depth)
              for names in (["ffn1_w_in"], ["ffn1_w_out", "w_in", "w_out"], ["ffn2_w_in", "ffn2_w_out"])]
    wb = {n: W[n].astype(BF16) for n in BIG}
    groups[1][1].append("conv_w")
    conv_shard = conv_w.reshape(-1, conv_w.shape[-1])
    flights = {}

    def weights_start(k, dep=None):
        l, names = groups[k]
        shards = [conv_shard if n == "conv_w" else wb[n][l] for n in names]
        if dep is not None:
            shards[0] = _tie(shards[0], dep)
        lands = [lax.empty((N_CHIP,) + s.shape, s.dtype) for s in shards]
        send, recv, lands, token = _gather_start(l, shards, lands, "weights_start_%d" % k)
        flights[k] = (shards, lands, send, recv)
        return token

    def weights_wait(k, after):
        l, names = groups[k]
        shards, lands, send, recv = flights[k]
        got = _gather_wait(l, shards, lands, send, recv, after, "weights_wait_%d" % k)
        token = weights_start(k + 2, got[0]) if k + 2 < len(groups) else None
        return dict(zip(names, got)), token

    def after_start(a, token):
        return a if token is None else _tie(a, token)

    first_tokens = [weights_start(0), weights_start(1)]

    layers = []
    for l in range(depth):
        L = {f: dict(norm=W[f + "_norm"][l][None]) for f in ("ffn1", "ffn2")}
        ws = jnp.where(jnp.tril(jnp.ones((CHUNK, CHUNK), bool))[None], sgu_w[l], 0.0)
        wax = jnp.concatenate([_block_diag(rg_w_a[l]), _block_diag(rg_w_x[l])], axis=1)
        wpool = _block_diag(pool_w[l])
        L["mix"] = dict(
            conv_b=conv_b[l][None], wax=wax.astype(BF16), wax_t=wax.T.astype(BF16),
            bax=jnp.concatenate([rg_b_a[l].reshape(-1), rg_b_x[l].reshape(-1)])[None], lam=lru_lambda[l][None],
            wpool=wpool.astype(BF16), wpool_t=wpool.T.astype(BF16), pool_scale=pool_scale[l][None],
            sgu_norm=sgu_norm[l][None], ws=ws.astype(BF16), ws_t=jnp.swapaxes(ws, 1, 2).astype(BF16),
            bz=jnp.repeat(sgu_b[l].T, 64, axis=1))
        L["mix_norm"] = mix_norm[l][None]
        layers.append(L)
    for token in first_tokens:
        layers[0]["ffn1"]["norm"] = _tie(layers[0]["ffn1"]["norm"], token)

    xs = x[0]
    saved = []
    for l, L in enumerate(layers):
        F1, F2 = L["ffn1"], L["ffn2"]
        h = _rms_fwd(xs, F1["norm"])
        got, token = weights_wait(3 * l, h)
        F1["wg"], F1["wu"] = _ffn_in_weights(got["ffn1_w_in"])
        h = after_start(h, token)
        g, u, a = _ffn_in(h, F1["wg"], F1["wu"])
        got, token = weights_wait(3 * l + 1, a)
        F1["wout"] = _ffn_out_weights(got["ffn1_w_out"])
        L["w_in"] = jnp.concatenate([got["w_in"][j] for j in range(N_CHIP)], axis=1)
        L["w_out"] = jnp.concatenate([got["w_out"][j] for j in range(N_CHIP)], axis=0)
        if "conv_w" in got:
            conv_full = jnp.concatenate([got["conv_w"][j] for j in range(N_CHIP)], axis=1)
            for ll in range(depth):
                layers[ll]["mix"]["conv_w"] = conv_full.reshape(depth, 4, D_RNN)[ll]
        a = after_start(a, token)
        x1 = _mm_res(a, F1["wout"], xs, 0.5, "ffn_out", tm=512, tn=D)
        s1 = (xs, h, g, u, a)
        hm = _rms_fwd(x1, L["mix_norm"])
        p = _mm(hm, L["w_in"], F32, "mix_in", tm=512, tn=D_IN)
        ycat, hs = _mix_fwd(p, L["mix"])
        x2 = _mm_res(ycat, L["w_out"], x1, 1.0, "mix_out", tm=512, tn=D)
        h2 = _rms_fwd(x2, F2["norm"])
        got, token = weights_wait(3 * l + 2, h2)
        F2["wg"], F2["wu"] = _ffn_in_weights(got["ffn2_w_in"])
        F2["wout"] = _ffn_out_weights(got["ffn2_w_out"])
        h2 = after_start(h2, token)
        g2, u2, a2 = _ffn_in(h2, F2["wg"], F2["wu"])
        x3 = _mm_res(a2, F2["wout"], x2, 0.5, "ffn_out", tm=512, tn=D)
        saved.append((s1, (x1, hm, p, ycat, hs), (x2, h2, g2, u2, a2)))
        xs = x3

    dx, dxb, d_final, loss_part = _final(xs, loss_target[0], final_norm[None], 0.5)

    G = {n: [None] * depth for n in SMALL if n != "final_norm"}
    lands = {n: lax.empty((N_DEV,) + LAND_SHAPES[n], BF16) for n in BIG}
    in_flight = []

    def send_grads(l, windows, tag):
        names = list(windows)
        send, recv, thru, token = _exchange_start(l, [windows[n] for n in names], [lands[n] for n in names],
                                                  "grads_start_" + tag)
        lands.update(zip(names, thru))
        in_flight.append((l, names, [windows[n] for n in names], send, recv, "grads_wait_" + tag))
        return token

    def ffn_bwd(dx, dxb, F, s, f, l, pending, send_now):
        xin, h, g, u, a = s
        dg, du = _ffn_mid_bwd(dxb, F["wout"], g, u)
        dwout = _mm_tn(a, dxb, 1.0, "ffn_dwout")
        dwg = _mm_tn(h, dg, 1.0, "ffn_dwg")
        dwu = _mm_tn(h, du, 1.0, "ffn_dwu")
        pending[f + "_w_in"], pending[f + "_w_out"] = _ffn_grad_windows(dwg, dwu, dwout)
        if send_now:
            dg = _tie(dg, send_grads(l, pending, "l%d_%s" % (l, f)))
        dx, dxb, dn = _dh_rms_bwd([(dg, F["wg"]), (du, F["wu"])], xin, F["norm"], dx,
                                  1.0 if f == "ffn2" else 0.5, "ffn_dh")
        G[f + "_norm"][l] = dn[0]
        return dx, dxb

    for l in reversed(range(depth)):
        L = layers[l]
        s1, (x1, hm, p, ycat, hs), s2 = saved[l]
        pending = {}
        dx, dxb = ffn_bwd(dx, dxb, L["ffn2"], s2, "ffn2", l, pending, l == 0)
        if l == 0:
            pending = {}
        dycat = _mm_nt(dxb, L["w_out"], "mix_dy", tm=512, tn=D)
        dwo = _mm_tn(ycat, dxb, 1.0, "mix_dwout")
        mg = _mix_bwd(dycat, p, hs, L["mix"])
        dwin = _mm_tn(hm, mg["dp"], 1.0, "mix_dwin")
        pending["w_in"], pending["w_out"] = _mix_grad_windows(jnp.stack(jnp.split(dwin, N_CHIP, axis=1)), dwo)
        if l == 0:
            dp = _tie(mg["dp"], send_grads(l, pending, "l0_mix"))
            pending = {}
        else:
            dp = mg["dp"]
        dx, dxb, dn = _dh_rms_bwd([(dp, L["w_in"])], x1, L["mix_norm"], dx, 0.5, "mix_dh")
        G["mix_norm"][l] = dn[0]
        G["conv_w"][l], G["conv_b"][l] = mg["conv_w"], mg["conv_b"][0]
        G["rg_w_a"][l] = _diag_blocks(mg["wax"][:, :D_RNN], 8, 64)
        G["rg_w_x"][l] = _diag_blocks(mg["wax"][:, D_RNN:], 8, 64)
        G["rg_b_a"][l] = mg["bax"][0, :D_RNN].reshape(8, 64)
        G["rg_b_x"][l] = mg["bax"][0, D_RNN:].reshape(8, 64)
        G["lru_lambda"][l] = mg["lam"][0]
        G["pool_w"][l] = _diag_blocks(mg["wpool"], 4, 64)
        G["pool_scale"][l], G["sgu_norm"][l] = mg["pool_scale"][0], mg["sgu_norm"][0]
        G["sgu_w"][l] = mg["ws"]
        G["sgu_b"][l] = mg["bz"].reshape(CHUNK, 4, 64).sum(-1).T
        dx, dxb = ffn_bwd(dx, dxb, L["ffn1"], s1, "ffn1", l, pending, l == 0)
        if l > 0:
            dxb = _tie(dxb, send_grads(l, pending, "l%d" % l))
    grad_x = dx[None]
    G = {n: jnp.stack(v) for n, v in G.items()}
    G["final_norm"] = d_final[0]

    for l, names, windows, send, recv, tag in in_flight:
        lands.update(zip(names, _exchange_wait(l, windows, [lands[n] for n in names], send, recv, dx, tag)))
    both = [_sum_share(lands[n], "sum_share_" + n) for n in BIG]
    grads = {n: (a[:, :, :FF_SHARD] if n.endswith("w_in") and n != "w_in" else a) for n, a in zip(BIG, both)}

    small_sizes = [int(np.prod(G[n].shape)) for n in SMALL]
    srows = _round_up(sum(small_sizes) + 1, N_DEV * 8 * LANES) // (N_DEV * LANES)
    sflat = _flat_rows([G[n] for n in SMALL] + [loss_part[0, :1]], N_DEV * srows)
    sgot = _all_to_all(sflat.reshape(N_DEV, srows, LANES), "exchange_small_grads")
    sall = _all_gather8(_sum8(sgot, "sum_small_grads"), "share_small_grads").reshape(-1)
    off = 0
    for n, size in zip(SMALL, small_sizes):
        grads[n] = sall[off:off + size].reshape(G[n].shape)
        off += size
    loss = sall[off]
    grads["conv_w"] = lax.dynamic_slice_in_dim(grads["conv_w"], chip * conv_w.shape[2], conv_w.shape[2], axis=2)

    delta, new_m, new_v = {}, {}, {}
    for n in BIG:
        shp = W[n].shape
        two_d = (shp[0] * shp[1], shp[2])
        outs = _adamw(W[n].reshape(two_d), grads[n].reshape(two_d), M[n].reshape(two_d), V[n].reshape(two_d),
                      "adamw_" + n)
        delta[n], new_m[n], new_v[n] = (o.reshape(shp) for o in outs)
    arows = _round_up(sum(int(np.prod(W[n].shape)) for n in SMALL), 8 * LANES) // LANES
    outs = _adamw(*(_flat_rows([src[n] for n in SMALL], arows) for src in (W, grads, M, V)), "adamw_small")
    outs = [o.reshape(-1) for o in outs]
    off = 0
    for n in SMALL:
        size = int(np.prod(W[n].shape))
        delta[n], new_m[n], new_v[n] = (o[off:off + size].reshape(W[n].shape) for o in outs)
        off += size

    return (loss, grad_x, *[grads[n] for n in WEIGHTS], *[delta[n] for n in WEIGHTS],
            *[new_m[n] for n in WEIGHTS], *[new_v[n] for n in WEIGHTS])
```

```python
import math

import jax
import jax.numpy as jnp
import numpy as np
from jax import lax
from jax.experimental import pallas as pl
from jax.experimental.pallas import tpu as pltpu

F32 = jnp.float32
BF16 = jnp.bfloat16
MESH = pl.DeviceIdType.MESH

D = 1024
D_RNN = 512
D_POOL = 256
D_SGU = 256
D_IN = 1792
D_FF = 2752
D_FFP = 2816
N_CHIP = 4
FF_SHARD = D_FF // 2
FF_HALF = D_FFP // 2
FF_ROWS = D_FF // N_CHIP
CHUNK = 128
HALO = 16
EPS = 1e-6
LRU_C = 8.0
N_DEV = 8
LANES = 1024
VMEM_LIMIT = 56 * 1024 * 1024

ADAM_LR, ADAM_B1, ADAM_B2, ADAM_EPS, ADAM_WD, ADAM_STEP = 0.001, 0.9, 0.999, 1e-08, 0.01, 10

BIG = ("ffn1_w_in", "ffn1_w_out", "w_in", "w_out", "ffn2_w_in", "ffn2_w_out")
SMALL = ("ffn1_norm", "mix_norm", "conv_w", "conv_b", "rg_w_a", "rg_b_a", "rg_w_x", "rg_b_x", "lru_lambda",
         "pool_w", "pool_scale", "sgu_norm", "sgu_w", "sgu_b", "ffn2_norm", "final_norm")
WEIGHTS = ("ffn1_norm", "ffn1_w_in", "ffn1_w_out", "mix_norm", "w_in", "conv_w", "conv_b", "rg_w_a", "rg_b_a",
           "rg_w_x", "rg_b_x", "lru_lambda", "pool_w", "pool_scale", "sgu_norm", "sgu_w", "sgu_b", "w_out",
           "ffn2_norm", "ffn2_w_in", "ffn2_w_out", "final_norm")


def _params(*sem):
    return pltpu.CompilerParams(dimension_semantics=sem, vmem_limit_bytes=VMEM_LIMIT)


def _gelu(x):
    c = math.sqrt(2.0 / math.pi)
    t = jnp.tanh(c * (x + 0.044715 * (x * x * x)))
    return 0.5 * x * (1.0 + t)


def _gelu_and_grad(x):
    c = math.sqrt(2.0 / math.pi)
    x2 = x * x
    t = jnp.tanh(c * (x + 0.044715 * (x2 * x)))
    g = 0.5 * x * (1.0 + t)
    dg = 0.5 * (1.0 + t) + 0.5 * x * (1.0 - t * t) * (c * (1.0 + 3.0 * 0.044715 * x2))
    return g, dg


def _sigmoid(x):
    return 0.5 * jnp.tanh(0.5 * x) + 0.5


def _dot(a, b):
    return jnp.dot(a, b, preferred_element_type=F32)


def _dot_tn(a, b):
    return lax.dot_general(a, b, (((0,), (0,)), ((), ())), preferred_element_type=F32)


def _dot_nt(a, b):
    return lax.dot_general(a, b, (((1,), (1,)), ((), ())), preferred_element_type=F32)


def _rms_fwd(x, g, tm=512):
    T = x.shape[0]
    tm = min(tm, T)

    def body(x_ref, g_ref, o_ref):
        xv = x_ref[...]
        r = lax.rsqrt(jnp.mean(xv * xv, axis=-1, keepdims=True) + EPS)
        o_ref[...] = (xv * r * g_ref[...]).astype(BF16)

    return pl.pallas_call(
        body, name="rms_fwd", grid=(T // tm,),
        in_specs=[pl.BlockSpec((tm, D), lambda i: (i, 0)), pl.BlockSpec((1, D), lambda i: (0, 0))],
        out_specs=pl.BlockSpec((tm, D), lambda i: (i, 0)),
        out_shape=jax.ShapeDtypeStruct((T, D), BF16),
        compiler_params=_params("parallel"),
    )(x, g)


def _tile(n, limit):
    best = 128
    for t in range(128, min(n, limit) + 1, 128):
        if n % t == 0:
            best = t
    assert n % best == 0, (n, limit)
    return best


def _mm_res(a, b, res, scale, name, tm=1024, tn=512):
    M, K = a.shape
    N = b.shape[1]
    tm, tn = min(tm, M), _tile(N, tn)

    def body(a_ref, b_ref, r_ref, o_ref):
        o_ref[...] = r_ref[...] + scale * _dot(a_ref[...], b_ref[...])

    return pl.pallas_call(
        body, name=name, grid=(M // tm, N // tn),
        in_specs=[pl.BlockSpec((tm, K), lambda i, j: (i, 0)), pl.BlockSpec((K, tn), lambda i, j: (0, j)),
                  pl.BlockSpec((tm, tn), lambda i, j: (i, j))],
        out_specs=pl.BlockSpec((tm, tn), lambda i, j: (i, j)),
        out_shape=jax.ShapeDtypeStruct((M, N), F32),
        compiler_params=_params("parallel", "parallel"),
    )(a, b, res)


def _mm_nt(a, b, name, tm=1024, tn=512):
    M, K = a.shape
    N = b.shape[0]
    tm, tn = min(tm, M), _tile(N, tn)

    def body(a_ref, b_ref, o_ref):
        o_ref[...] = _dot_nt(a_ref[...], b_ref[...])

    return pl.pallas_call(
        body, name=name, grid=(M // tm, N // tn),
        in_specs=[pl.BlockSpec((tm, K), lambda i, j: (i, 0)), pl.BlockSpec((tn, K), lambda i, j: (j, 0))],
        out_specs=pl.BlockSpec((tm, tn), lambda i, j: (i, j)),
        out_shape=jax.ShapeDtypeStruct((M, N), F32),
        compiler_params=_params("parallel", "parallel"),
    )(a, b)


def _dh_rms_bwd(pairs, x, g, dres, copy_scale, name, tm=512):
    T = x.shape[0]
    tm = min(tm, T)
    n = len(pairs)

    def body(*refs):
        ab = refs[:2 * n]
        x_ref, g_ref, dres_ref, dx_ref, dxb_ref, dg_ref = refs[2 * n:]
        dy = _dot(ab[0][...], ab[1][...])
        for k in range(1, n):
            dy = dy + _dot(ab[2 * k][...], ab[2 * k + 1][...])
        xv = x_ref[...]
        r = lax.rsqrt(jnp.mean(xv * xv, axis=-1, keepdims=True) + EPS)
        xhat = xv * r
        dxhat = dy * g_ref[...]
        dx = dres_ref[...] + r * (dxhat - xhat * jnp.mean(dxhat * xhat, axis=-1, keepdims=True))
        dx_ref[...] = dx
        dxb_ref[...] = (copy_scale * dx).astype(BF16)

        @pl.when(pl.program_id(0) == 0)
        def _():
            dg_ref[...] = jnp.zeros_like(dg_ref)

        dg_ref[...] += jnp.sum(dy * xhat, axis=0, keepdims=True)

    row = pl.BlockSpec((tm, D), lambda i: (i, 0))
    vec = pl.BlockSpec((1, D), lambda i: (0, 0))
    in_specs, operands = [], []
    for a, b in pairs:
        in_specs += [pl.BlockSpec((tm, a.shape[1]), lambda i: (i, 0)),
                     pl.BlockSpec(b.shape, lambda i: (0, 0), pipeline_mode=pl.Buffered(1))]
        operands += [a, b]
    return pl.pallas_call(
        body, name=name, grid=(T // tm,),
        in_specs=in_specs + [row, vec, row], out_specs=[row, row, vec],
        out_shape=[jax.ShapeDtypeStruct((T, D), F32), jax.ShapeDtypeStruct((T, D), BF16),
                   jax.ShapeDtypeStruct((1, D), F32)],
        compiler_params=_params("arbitrary"),
    )(*operands, x, g, dres)


def _mm_tn(a, b, scale, name, tm=1792, tn=1792, tk=2048):
    T, M = a.shape
    N = b.shape[1]
    tm, tn, tk = _tile(M, tm), _tile(N, tn), min(tk, T)
    nk = T // tk

    def body(a_ref, b_ref, o_ref, acc_ref):
        k = pl.program_id(2)

        @pl.when(k == 0)
        def _():
            acc_ref[...] = jnp.zeros_like(acc_ref)

        acc_ref[...] += _dot_tn(a_ref[...], b_ref[...])

        @pl.when(k == nk - 1)
        def _():
            o_ref[...] = (scale * acc_ref[...]).astype(BF16)

    return pl.pallas_call(
        body, name=name, grid=(M // tm, N // tn, nk),
        in_specs=[pl.BlockSpec((tk, tm), lambda i, j, k: (k, i)), pl.BlockSpec((tk, tn), lambda i, j, k: (k, j))],
        out_specs=pl.BlockSpec((tm, tn), lambda i, j, k: (i, j)),
        out_shape=jax.ShapeDtypeStruct((M, N), BF16),
        scratch_shapes=[pltpu.VMEM((tm, tn), F32)],
        compiler_params=_params("parallel", "parallel", "arbitrary"),
    )(a, b)


def _ffn_in(h, wg_t, wu_t, tm=512, tn=FF_HALF):
    T = h.shape[0]
    tm = min(tm, T)

    def body(h_ref, wg_ref, wu_ref, g_ref, u_ref, a_ref):
        hv = h_ref[...]
        g = _dot_nt(hv, wg_ref[...])
        u = _dot_nt(hv, wu_ref[...])
        g_ref[...] = g.astype(BF16)
        u_ref[...] = u.astype(BF16)
        a_ref[...] = (g * _sigmoid(g) * u).astype(BF16)

    wspec = pl.BlockSpec((tn, D), lambda j, i: (j, 0))
    ospec = pl.BlockSpec((tm, tn), lambda j, i: (i, j))
    oshape = jax.ShapeDtypeStruct((T, D_FFP), BF16)
    return pl.pallas_call(
        body, name="ffn_in", grid=(D_FFP // tn, T // tm),
        in_specs=[pl.BlockSpec((tm, D), lambda j, i: (i, 0)), wspec, wspec],
        out_specs=[ospec, ospec, ospec], out_shape=[oshape, oshape, oshape],
        compiler_params=_params("parallel", "parallel"),
    )(h, wg_t, wu_t)


def _ffn_mid_bwd(dyh, wout, g, u, tm=512, tn=FF_HALF):
    T = dyh.shape[0]
    tm = min(tm, T)

    def body(dy_ref, w_ref, g_ref, u_ref, dg_ref, du_ref):
        da = _dot_nt(dy_ref[...], w_ref[...])
        g = g_ref[...].astype(F32)
        s = _sigmoid(g)
        gs = g * s
        du_ref[...] = (da * gs).astype(BF16)
        dg_ref[...] = ((da * u_ref[...].astype(F32)) * (s + gs - gs * s)).astype(BF16)

    ospec = pl.BlockSpec((tm, tn), lambda j, i: (i, j))
    oshape = jax.ShapeDtypeStruct((T, D_FFP), BF16)
    return pl.pallas_call(
        body, name="ffn_mid_bwd", grid=(D_FFP // tn, T // tm),
        in_specs=[pl.BlockSpec((tm, D), lambda j, i: (i, 0)), pl.BlockSpec((tn, D), lambda j, i: (j, 0)),
                  ospec, ospec],
        out_specs=[ospec, ospec], out_shape=[oshape, oshape],
        compiler_params=_params("parallel", "parallel"),
    )(dyh, wout, g, u)


def _final(x, tgt, gf, copy_scale, tm=512):
    T = x.shape[0]
    tm = min(tm, T)

    def body(x_ref, t_ref, g_ref, dx_ref, dxb_ref, dg_ref, loss_ref):
        xv = x_ref[...]
        r = lax.rsqrt(jnp.mean(xv * xv, axis=-1, keepdims=True) + EPS)
        xhat = xv * r
        err = xhat * g_ref[...] - t_ref[...]
        dy = err * (1.0 / D)
        dxhat = dy * g_ref[...]
        dx = r * (dxhat - xhat * jnp.mean(dxhat * xhat, axis=-1, keepdims=True))
        dx_ref[...] = dx
        dxb_ref[...] = (copy_scale * dx).astype(BF16)

        @pl.when(pl.program_id(0) == 0)
        def _():
            dg_ref[...] = jnp.zeros_like(dg_ref)
            loss_ref[...] = jnp.zeros_like(loss_ref)

        dg_ref[...] += jnp.sum(dy * xhat, axis=0, keepdims=True)
        loss_ref[...] += (0.5 / D) * jnp.sum(err * err)

    row = pl.BlockSpec((tm, D), lambda i: (i, 0))
    vec = pl.BlockSpec((1, D), lambda i: (0, 0))
    return pl.pallas_call(
        body, name="final_loss", grid=(T // tm,),
        in_specs=[row, row, vec],
        out_specs=[row, row, vec, pl.BlockSpec((1, 128), lambda i: (0, 0))],
        out_shape=[jax.ShapeDtypeStruct((T, D), F32), jax.ShapeDtypeStruct((T, D), BF16),
                   jax.ShapeDtypeStruct((1, D), F32), jax.ShapeDtypeStruct((1, 128), F32)],
        compiler_params=_params("arbitrary"),
    )(x, tgt, gf)


def _mix_block(T, limit):
    return min(limit, T // 2)


def _rows(tb, width):
    return lax.broadcasted_iota(jnp.int32, (tb, width), 0)


def _rglru_gates(xc, wax_ref, bax_ref, lam_ref):
    pre = _dot(xc.astype(BF16), wax_ref[...]) + bax_ref[...]
    r = _sigmoid(pre[:, :D_RNN])
    ig = _sigmoid(pre[:, D_RNN:])
    z = -lam_ref[...]
    sp = jnp.maximum(z, 0.0) + jnp.log(1.0 + jnp.exp(-jnp.abs(z)))
    log_a = (-LRU_C) * r * sp
    a = jnp.exp(log_a)
    mult = jnp.sqrt(-jnp.tanh(log_a) * (1.0 + a * a))
    return r, ig, sp, a, mult


def _conv(xa_ext, cw_ref, cb_ref):
    y = cb_ref[...] + cw_ref[3:4, :] * xa_ext
    for k in range(1, 4):
        y = y + cw_ref[3 - k:4 - k, :] * pltpu.roll(xa_ext, k, 0)
    return y[HALO:]


def _pool_window_lanes():
    lane = lax.broadcasted_iota(jnp.int32, (1, D_POOL), 1)
    return jnp.where(lane < 64, 2, jnp.where(lane < 128, 4, jnp.where(lane < 192, 8, 16)))


def _pool_select(s2, s4, s8, s16):
    lane = lax.broadcasted_iota(jnp.int32, s2.shape, 1)
    return jnp.where(lane < 64, s2, jnp.where(lane < 128, s4, jnp.where(lane < 192, s8, s16)))


def _pool_diff(xp_ext, t0, tb):
    s2 = xp_ext + pltpu.roll(xp_ext, 1, 0)
    s4 = s2 + pltpu.roll(s2, 2, 0)
    s8 = s4 + pltpu.roll(s4, 4, 0)
    s16 = s8 + pltpu.roll(s8, 8, 0)
    sel = _pool_select(s2, s4, s8, s16)[HALO:]
    cnt = jnp.minimum(t0 + _rows(tb, D_POOL) + 1, _pool_window_lanes()).astype(F32)
    return sel / cnt - xp_ext[HALO:], cnt


def _head_masks():
    lane = lax.broadcasted_iota(jnp.int32, (1, D_SGU), 1)
    return [((lane >= 64 * h) & (lane < 64 * (h + 1))).astype(F32) for h in range(4)]


def _sgu_mix(w_ref, vch, masks):
    z = masks[0] * _dot(w_ref[0], vch)
    for h in range(1, 4):
        z = z + masks[h] * _dot(w_ref[h], vch)
    return z


def _mix_fwd(p, prm):
    T = p.shape[0]
    tb = _mix_block(T, 512)
    nb = T // tb

    def body(p_ref, xah_ref, xph_ref, cw_ref, cb_ref, wax_ref, bax_ref, lam_ref, wp_ref, ps_ref, sgn_ref,
             ws_ref, bz_ref, y_ref, hs_ref, carry_ref):
        i = pl.program_id(0)
        keep = (i > 0).astype(F32)

        @pl.when(i == 0)
        def _():
            carry_ref[...] = jnp.zeros_like(carry_ref)

        xa_ext = jnp.concatenate([xah_ref[...] * keep, p_ref[:, 512:1024]], axis=0)
        xc = _conv(xa_ext, cw_ref, cb_ref)
        r, ig, sp, a, mult = _rglru_gates(xc, wax_ref, bax_ref, lam_ref)
        bv = mult * (ig * xc)
        row = _rows(tb, D_RNN)
        s = 1
        while s < tb:
            m = row >= s
            bv = jnp.where(m, a * pltpu.roll(bv, s, 0) + bv, bv)
            a = jnp.where(m, a * pltpu.roll(a, s, 0), a)
            s *= 2
        h = bv + a * carry_ref[0:1, :]
        hs_ref[...] = h
        last = jnp.sum(jnp.where(_rows(8, D_RNN) == 7, hs_ref[tb - 8:tb, :], 0.0), axis=0, keepdims=True)
        carry_ref[...] = jnp.broadcast_to(last, carry_ref.shape)
        y_ref[:, 0:512] = (_gelu(p_ref[:, 0:512]) * h).astype(BF16)

        xp_ext = jnp.concatenate([xph_ref[...] * keep, p_ref[:, 1024:1280]], axis=0)
        d, _ = _pool_diff(xp_ext, i * tb, tb)
        y_ref[:, 512:768] = (_dot(d.astype(BF16), wp_ref[...]) * ps_ref[...]).astype(BF16)

        ug = _gelu(p_ref[:, 1280:1536])
        vg = _gelu(p_ref[:, 1536:1792])
        rv = lax.rsqrt(jnp.mean(vg * vg, axis=-1, keepdims=True) + EPS)
        vn = (vg * rv * sgn_ref[...]).astype(BF16)
        masks = _head_masks()
        for ci in range(tb // CHUNK):
            sl = slice(ci * CHUNK, (ci + 1) * CHUNK)
            z = _sgu_mix(ws_ref, vn[sl], masks) + bz_ref[...]
            y_ref[sl, 768:1024] = (ug[sl] * z).astype(BF16)

    hb = tb // HALO

    def halo(i):
        return jnp.maximum(i * hb - 1, 0)

    def full(shape):
        return pl.BlockSpec(shape, lambda i: (0,) * len(shape))

    return pl.pallas_call(
        body, name="mix_fwd", grid=(nb,),
        in_specs=[pl.BlockSpec((tb, D_IN), lambda i: (i, 0)),
                  pl.BlockSpec((HALO, D_RNN), lambda i: (halo(i), 1)),
                  pl.BlockSpec((HALO, D_POOL), lambda i: (halo(i), 4)),
                  full((4, D_RNN)), full((1, D_RNN)), full((D_RNN, 2 * D_RNN)), full((1, 2 * D_RNN)),
                  full((1, D_RNN)), full((D_POOL, D_POOL)), full((1, D_POOL)), full((1, D_SGU)),
                  full((4, CHUNK, CHUNK)), full((CHUNK, D_SGU))],
        out_specs=[pl.BlockSpec((tb, D), lambda i: (i, 0)), pl.BlockSpec((tb, D_RNN), lambda i: (i, 0))],
        out_shape=[jax.ShapeDtypeStruct((T, D), BF16), jax.ShapeDtypeStruct((T, D_RNN), F32)],
        scratch_shapes=[pltpu.VMEM((8, D_RNN), F32)],
        compiler_params=_params("arbitrary"),
    )(p, p, p, prm["conv_w"], prm["conv_b"], prm["wax"], prm["bax"], prm["lam"], prm["wpool"], prm["pool_scale"],
      prm["sgu_norm"], prm["ws"], prm["bz"])


def _mix_bwd(dy, p, hs, prm):
    T = p.shape[0]
    tb = _mix_block(T, 256)
    nb = T // tb
    hb = tb // HALO

    def body(dy_ref, p_ref, xah_ref, xph_ref, hs_ref, hsh_ref, cw_ref, cb_ref, wax_ref, waxt_ref, bax_ref,
             lam_ref, wp_ref, wpt_ref, ps_ref, sgn_ref, ws_ref, wst_ref, bz_ref,
             dp_ref, dcw_ref, dcb_ref, dwax_ref, dbax_ref, dlam_ref, dwp_ref, dps_ref, dsgn_ref, dws_ref,
             dbz_ref, gcarry_ref, xcfut_ref, mfut_ref):
        i = pl.program_id(0)
        bi = nb - 1 - i
        keep = (bi > 0).astype(F32)

        @pl.when(i == 0)
        def _():
            for ref in (dcw_ref, dcb_ref, dwax_ref, dbax_ref, dlam_ref, dwp_ref, dps_ref, dsgn_ref, dws_ref,
                        dbz_ref, gcarry_ref, xcfut_ref, mfut_ref):
                ref[...] = jnp.zeros_like(ref)

        xa_ext = jnp.concatenate([xah_ref[...] * keep, p_ref[:, 512:1024]], axis=0)
        xc = _conv(xa_ext, cw_ref, cb_ref)
        r, ig, sp, a, mult = _rglru_gates(xc, wax_ref, bax_ref, lam_ref)
        gg, dgg = _gelu_and_grad(p_ref[:, 0:512])
        dya = dy_ref[:, 0:512]
        dp_ref[:, 0:512] = (dya * hs_ref[...] * dgg).astype(BF16)
        row = _rows(tb, D_RNN)
        g = dya * gg + jnp.where(row == tb - 1, gcarry_ref[0:1, :], 0.0)
        al = pltpu.roll(a, tb - 1, 0)
        s = 1
        while s < tb:
            m = row < tb - s
            g = jnp.where(m, al * pltpu.roll(g, tb - s, 0) + g, g)
            al = jnp.where(m, al * pltpu.roll(al, tb - s, 0), al)
            s *= 2
        first = jnp.sum(jnp.where(_rows(8, D_RNN) == 0, (a * g)[0:8], 0.0), axis=0, keepdims=True)
        gcarry_ref[...] = jnp.broadcast_to(first, gcarry_ref.shape)
        hs_ext = jnp.concatenate([hsh_ref[...] * keep, hs_ref[...]], axis=0)
        h_prev = pltpu.roll(hs_ext, 1, 0)[HALO:]
        ix = ig * xc
        dlog_a = g * h_prev * a - (g * ix) * (a * a / mult)
        dlam_ref[...] += jnp.sum(dlog_a * r, axis=0, keepdims=True) * (LRU_C * _sigmoid(-lam_ref[...]))
        dpre_r = dlog_a * ((-LRU_C) * sp) * (r * (1.0 - r))
        dpre_i = (g * mult * xc) * (ig * (1.0 - ig))
        dpre = jnp.concatenate([dpre_r, dpre_i], axis=1)
        dbax_ref[...] += jnp.sum(dpre, axis=0, keepdims=True)
        dpre_b = dpre.astype(BF16)
        dwax_ref[...] += _dot_tn(xc.astype(BF16), dpre_b)
        dxc = g * mult * ig + _dot(dpre_b, waxt_ref[...])
        dcb_ref[...] += jnp.sum(dxc, axis=0, keepdims=True)
        for k in range(4):
            xs = xa_ext[HALO:] if k == 3 else pltpu.roll(xa_ext, 3 - k, 0)[HALO:]
            dcw_ref[k:k + 1, :] += jnp.sum(dxc * xs, axis=0, keepdims=True)
        dxc_ext = jnp.concatenate([dxc, xcfut_ref[...]], axis=0)
        n = tb + HALO
        dxa = cw_ref[3:4, :] * dxc_ext
        for k in range(1, 4):
            dxa = dxa + cw_ref[3 - k:4 - k, :] * pltpu.roll(dxc_ext, n - k, 0)
        dp_ref[:, 512:1024] = dxa[:tb].astype(BF16)
        xcfut_ref[...] = dxc[0:HALO]

        xp_ext = jnp.concatenate([xph_ref[...] * keep, p_ref[:, 1024:1280]], axis=0)
        d, cnt = _pool_diff(xp_ext, bi * tb, tb)
        db = d.astype(BF16)
        dyb = dy_ref[:, 512:768]
        dps_ref[...] += jnp.sum(dyb * _dot(db, wp_ref[...]), axis=0, keepdims=True)
        dq = (dyb * ps_ref[...]).astype(BF16)
        dwp_ref[...] += _dot_tn(db, dq)
        dd = _dot(dq, wpt_ref[...])
        mm = dd / cnt
        m_ext = jnp.concatenate([mm, mfut_ref[...]], axis=0)
        f2 = m_ext + pltpu.roll(m_ext, n - 1, 0)
        f4 = f2 + pltpu.roll(f2, n - 2, 0)
        f8 = f4 + pltpu.roll(f4, n - 4, 0)
        f16 = f8 + pltpu.roll(f8, n - 8, 0)
        dp_ref[:, 1024:1280] = (_pool_select(f2, f4, f8, f16)[:tb] - dd).astype(BF16)
        mfut_ref[...] = mm[0:HALO]

        ug, dug = _gelu_and_grad(p_ref[:, 1280:1536])
        vg, dvg = _gelu_and_grad(p_ref[:, 1536:1792])
        rv = lax.rsqrt(jnp.mean(vg * vg, axis=-1, keepdims=True) + EPS)
        vhat = vg * rv
        vn = (vhat * sgn_ref[...]).astype(BF16)
        dyc = dy_ref[:, 768:1024]
        masks = _head_masks()
        dz = dyc * ug
        dzb = dz.astype(BF16)
        dvn_parts = []
        for ci in range(tb // CHUNK):
            sl = slice(ci * CHUNK, (ci + 1) * CHUNK)
            z = _sgu_mix(ws_ref, vn[sl], masks) + bz_ref[...]
            dp_ref[sl, 1280:1536] = (dyc[sl] * z * dug[sl]).astype(BF16)
            dbz_ref[...] += dz[sl]
            for h in range(4):
                dws_ref[h] += _dot_nt((dz[sl] * masks[h]).astype(BF16), vn[sl])
            dvn_parts.append(_sgu_mix(wst_ref, dzb[sl], masks))
        dvn = jnp.concatenate(dvn_parts, axis=0)
        dsgn_ref[...] += jnp.sum(dvn * vhat, axis=0, keepdims=True)
        dvhat = dvn * sgn_ref[...]
        dvg_in = rv * (dvhat - vhat * jnp.mean(dvhat * vhat, axis=-1, keepdims=True))
        dp_ref[:, 1536:1792] = (dvg_in * dvg).astype(BF16)

        @pl.when(i == nb - 1)
        def _():
            tril = (lax.broadcasted_iota(jnp.int32, (CHUNK, CHUNK), 0)
                    >= lax.broadcasted_iota(jnp.int32, (CHUNK, CHUNK), 1)).astype(F32)
            for h in range(4):
                dws_ref[h] = dws_ref[h] * tril

    def blk(i):
        return nb - 1 - i

    def halo(i):
        return jnp.maximum(blk(i) * hb - 1, 0)

    def full(shape):
        return pl.BlockSpec(shape, lambda i: (0,) * len(shape))

    small_shapes = [(4, D_RNN), (1, D_RNN), (D_RNN, 2 * D_RNN), (1, 2 * D_RNN), (1, D_RNN), (D_POOL, D_POOL),
                    (1, D_POOL), (1, D_SGU), (4, CHUNK, CHUNK), (CHUNK, D_SGU)]
    outs = pl.pallas_call(
        body, name="mix_bwd", grid=(nb,),
        in_specs=[pl.BlockSpec((tb, D), lambda i: (blk(i), 0)),
                  pl.BlockSpec((tb, D_IN), lambda i: (blk(i), 0)),
                  pl.BlockSpec((HALO, D_RNN), lambda i: (halo(i), 1)),
                  pl.BlockSpec((HALO, D_POOL), lambda i: (halo(i), 4)),
                  pl.BlockSpec((tb, D_RNN), lambda i: (blk(i), 0)),
                  pl.BlockSpec((HALO, D_RNN), lambda i: (halo(i), 0)),
                  full((4, D_RNN)), full((1, D_RNN)), full((D_RNN, 2 * D_RNN)), full((2 * D_RNN, D_RNN)),
                  full((1, 2 * D_RNN)), full((1, D_RNN)), full((D_POOL, D_POOL)), full((D_POOL, D_POOL)),
                  full((1, D_POOL)), full((1, D_SGU)), full((4, CHUNK, CHUNK)), full((4, CHUNK, CHUNK)),
                  full((CHUNK, D_SGU))],
        out_specs=[pl.BlockSpec((tb, D_IN), lambda i: (blk(i), 0))] + [full(s) for s in small_shapes],
        out_shape=[jax.ShapeDtypeStruct((T, D_IN), BF16)] + [jax.ShapeDtypeStruct(s, F32) for s in small_shapes],
        scratch_shapes=[pltpu.VMEM((8, D_RNN), F32), pltpu.VMEM((HALO, D_RNN), F32),
                        pltpu.VMEM((HALO, D_POOL), F32)],
        compiler_params=_params("arbitrary"),
    )(dy, p, p, p, hs, hs, prm["conv_w"], prm["conv_b"], prm["wax"], prm["wax_t"], prm["bax"], prm["lam"],
      prm["wpool"], prm["wpool_t"], prm["pool_scale"], prm["sgu_norm"], prm["ws"], prm["ws_t"], prm["bz"])
    names = ("dp", "conv_w", "conv_b", "wax", "bax", "lam", "wpool", "pool_scale", "sgu_norm", "ws", "bz")
    return dict(zip(names, outs))


ANY = pl.BlockSpec(memory_space=pl.ANY)


def _place():
    x, y, c = lax.axis_index("x"), lax.axis_index("y"), lax.axis_index("c")
    return x, y, c


def _all_to_all(xs, name):
    def body(in_ref, out_ref, send_sems, recv_sems, local_sem):
        x, y, c = _place()
        me = 4 * x + 2 * y + c
        mine = pltpu.make_async_copy(in_ref.at[me], out_ref.at[me], local_sem)
        mine.start()
        copies = []
        for rel in range(1, N_DEV):
            tx = 1 - x if rel & 4 else x
            ty = 1 - y if rel & 2 else y
            tc = 1 - c if rel & 1 else c
            cp = pltpu.make_async_remote_copy(
                src_ref=in_ref.at[4 * tx + 2 * ty + tc], dst_ref=out_ref.at[me],
                send_sem=send_sems.at[rel - 1], recv_sem=recv_sems.at[rel - 1],
                device_id=(tx, ty, tc), device_id_type=MESH)
            cp.start()
            copies.append(cp)
        for cp in copies:
            cp.wait()
        mine.wait()

    return pl.pallas_call(
        body, name=name, in_specs=[ANY], out_specs=ANY,
        out_shape=jax.ShapeDtypeStruct(xs.shape, xs.dtype),
        scratch_shapes=[pltpu.SemaphoreType.DMA((N_DEV - 1,)), pltpu.SemaphoreType.DMA((N_DEV - 1,)),
                        pltpu.SemaphoreType.DMA],
    )(xs)


def _all_gather8(xs, name):
    def body(x_ref, out_ref, send_sems, recv_sems, local_sem):
        x, y, c = _place()
        me, sibling = (x, y, c), (x, y, 1 - c)
        chips = [(1 - x, y), (x, 1 - y), (1 - x, 1 - y)]

        def rows(px, py, pc):
            return out_ref.at[4 * px + 2 * py + pc]

        def copy(k, block, to, src=None):
            return pltpu.make_async_remote_copy(
                src_ref=rows(*block) if src is None else src, dst_ref=rows(*block),
                send_sem=send_sems.at[k], recv_sem=recv_sems.at[k], device_id=to, device_id_type=MESH)

        mine = pltpu.make_async_copy(x_ref, rows(*me), local_sem)
        mine.start()
        first = [copy(0, me, sibling, src=x_ref)]
        first += [copy(1 + j, me, (*chip, c), src=x_ref) for j, chip in enumerate(chips)]
        for cp in first:
            cp.start()
        passed = [copy(4 + j, (*chip, c), sibling) for j, chip in enumerate(chips)]
        for j, chip in enumerate(chips):
            copy(1 + j, (*chip, c), me).wait_recv()
            passed[j].start()
        copy(0, sibling, me).wait_recv()
        for j, chip in enumerate(chips):
            copy(4 + j, (*chip, 1 - c), me).wait_recv()
        for cp in first + passed:
            cp.wait_send()
        mine.wait()

    return pl.pallas_call(
        body, name=name, in_specs=[ANY], out_specs=ANY,
        out_shape=jax.ShapeDtypeStruct((N_DEV,) + xs.shape, xs.dtype),
        scratch_shapes=[pltpu.SemaphoreType.DMA((7,)), pltpu.SemaphoreType.DMA((7,)), pltpu.SemaphoreType.DMA],
    )(xs)


HBM = pl.BlockSpec(memory_space=pltpu.HBM)
SEM = pl.BlockSpec(memory_space=pltpu.SEMAPHORE)
EFFECT = pltpu.SideEffectType.DATAFLOW_SIDE_EFFECTING


def _in_hbm(a):
    return pltpu.with_memory_space_constraint(a, pltpu.HBM)


def _local_copy(src, dst, stage, sem):
    load = pltpu.make_async_copy(src, stage, sem)
    load.start()
    load.wait()
    store = pltpu.make_async_copy(stage, dst, sem)
    store.start()
    store.wait()


def _unique(windows):
    arrays = []
    for per_chip in windows:
        for arr, _ in per_chip:
            if not any(arr is a for a in arrays):
                arrays.append(arr)
    return arrays


def _exchange_start(layer, windows, lands, name):
    arrays = _unique(windows)
    na, nt = len(arrays), len(windows)

    def body(*refs):
        in_refs, land_refs = refs[:na], refs[na:na + nt]
        send_sems, recv_sems = refs[na + nt], refs[na + nt + 1]
        token = refs[-1]
        x, y, c = _place()
        me = 4 * x + 2 * y + c
        for t in range(nt):
            for j in range(N_CHIP):
                arr, window = windows[t][j]
                src = window(in_refs[next(i for i, a in enumerate(arrays) if a is arr)])

                @pl.when(me != 2 * j + layer)
                def _():
                    pltpu.make_async_remote_copy(
                        src_ref=src, dst_ref=land_refs[t].at[me], send_sem=send_sems.at[N_CHIP * t + j],
                        recv_sem=recv_sems.at[N_DEV * t + me], device_id=(j // 2, j % 2, layer),
                        device_id_type=MESH).start()
        token[...] = jnp.zeros_like(token)

    outs = pl.pallas_call(
        body, name=name,
        out_shape=(pltpu.SemaphoreType.DMA((N_CHIP * nt,)), pltpu.SemaphoreType.DMA((N_DEV * nt,)),
                   *[pltpu.HBM(a.shape, a.dtype) for a in lands], jax.ShapeDtypeStruct((8, 128), F32)),
        in_specs=[HBM] * (na + nt),
        out_specs=(SEM, SEM, *[HBM] * nt, pl.BlockSpec(memory_space=pltpu.VMEM)),
        input_output_aliases={na + t: 2 + t for t in range(nt)},
        compiler_params=pltpu.CompilerParams(has_side_effects=EFFECT, vmem_limit_bytes=VMEM_LIMIT),
    )(*[_in_hbm(a) for a in arrays], *[_in_hbm(a) for a in lands])
    return outs[0], outs[1], list(outs[2:2 + nt]), outs[-1]


def _exchange_wait(layer, windows, lands, send_sems, recv_sems, after, name):
    arrays = _unique(windows)
    na, nt = len(arrays), len(windows)

    def body(*refs):
        in_refs, land_refs = refs[:na], refs[na:na + nt]
        send_sems, recv_sems = refs[na + nt], refs[na + nt + 1]
        stages, local_sem = refs[-1 - nt:-1], refs[-1]
        x, y, c = _place()
        me = 4 * x + 2 * y + c

        def source(t, j):
            arr, window = windows[t][j]
            return window(in_refs[next(i for i, a in enumerate(arrays) if a is arr)])

        @pl.when(c == layer)
        def _():
            for t in range(nt):
                for j in range(N_CHIP):
                    @pl.when(me == 2 * j + layer)
                    def _():
                        _local_copy(source(t, j), land_refs[t].at[me], stages[t], local_sem)

        for t in range(nt):
            for j in range(N_CHIP):
                @pl.when(me != 2 * j + layer)
                def _():
                    pltpu.make_async_remote_copy(
                        src_ref=source(t, j), dst_ref=land_refs[t].at[me], send_sem=send_sems.at[N_CHIP * t + j],
                        recv_sem=recv_sems.at[N_DEV * t + me], device_id=(j // 2, j % 2, layer),
                        device_id_type=MESH).wait_send()

        @pl.when(c == layer)
        def _():
            for t in range(nt):
                for s in range(N_DEV):
                    @pl.when(me != s)
                    def _():
                        slot = land_refs[t].at[s]
                        pltpu.make_async_remote_copy(
                            src_ref=slot, dst_ref=slot, send_sem=send_sems.at[N_CHIP * t],
                            recv_sem=recv_sems.at[N_DEV * t + s], device_id=(x, y, c),
                            device_id_type=MESH).wait_recv()

    outs = pl.pallas_call(
        body, name=name,
        out_shape=tuple(pltpu.HBM(a.shape, a.dtype) for a in lands),
        in_specs=[HBM] * (na + nt) + [SEM, SEM, ANY],
        out_specs=tuple([HBM] * nt),
        input_output_aliases={na + t: t for t in range(nt)},
        scratch_shapes=[pltpu.VMEM(a.shape[1:], a.dtype) for a in lands] + [pltpu.SemaphoreType.DMA],
        compiler_params=pltpu.CompilerParams(has_side_effects=EFFECT, vmem_limit_bytes=VMEM_LIMIT),
    )(*[_in_hbm(a) for a in arrays], *lands, send_sems, recv_sems, after)
    return list(outs)


def _other_chips(x, y):
    return [(1 - x, y), (x, 1 - y), (1 - x, 1 - y)]


def _gather_start(layer, shards, lands, name):
    nt = len(shards)

    def body(*refs):
        in_refs, land_refs = refs[:nt], refs[nt:2 * nt]
        send_sems, recv_sems = refs[2 * nt], refs[2 * nt + 1]
        token = refs[-1]
        x, y, c = _place()

        @pl.when(c == layer)
        def _():
            for t in range(nt):
                for rel, (tx, ty) in enumerate(_other_chips(x, y)):
                    for tc in range(2):
                        pltpu.make_async_remote_copy(
                            src_ref=in_refs[t], dst_ref=land_refs[t].at[2 * x + y],
                            send_sem=send_sems.at[6 * t + 2 * rel + tc], recv_sem=recv_sems.at[3 * t + rel],
                            device_id=(tx, ty, tc), device_id_type=MESH).start()

        token[...] = jnp.zeros_like(token)

    outs = pl.pallas_call(
        body, name=name,
        out_shape=(pltpu.SemaphoreType.DMA((6 * nt,)), pltpu.SemaphoreType.DMA((3 * nt,)),
                   *[pltpu.HBM(a.shape, a.dtype) for a in lands], jax.ShapeDtypeStruct((8, 128), F32)),
        in_specs=[HBM] * (2 * nt),
        out_specs=(SEM, SEM, *[HBM] * nt, pl.BlockSpec(memory_space=pltpu.VMEM)),
        input_output_aliases={nt + t: 2 + t for t in range(nt)},
        compiler_params=pltpu.CompilerParams(has_side_effects=EFFECT, vmem_limit_bytes=VMEM_LIMIT),
    )(*[_in_hbm(a) for a in shards], *[_in_hbm(a) for a in lands])
    return outs[0], outs[1], list(outs[2:2 + nt]), outs[-1]


def _gather_wait(layer, shards, lands, send_sems, recv_sems, after, name):
    nt = len(shards)

    def body(*refs):
        in_refs, land_refs = refs[:nt], refs[nt:2 * nt]
        send_sems, recv_sems = refs[2 * nt], refs[2 * nt + 1]
        stages, local_sem = refs[-1 - nt:-1], refs[-1]
        x, y, c = _place()

        for t in range(nt):
            _local_copy(in_refs[t], land_refs[t].at[2 * x + y], stages[t], local_sem)

        @pl.when(c == layer)
        def _():
            for t in range(nt):
                for rel, (tx, ty) in enumerate(_other_chips(x, y)):
                    for tc in range(2):
                        pltpu.make_async_remote_copy(
                            src_ref=in_refs[t], dst_ref=land_refs[t].at[2 * x + y],
                            send_sem=send_sems.at[6 * t + 2 * rel + tc], recv_sem=recv_sems.at[3 * t + rel],
                            device_id=(tx, ty, tc), device_id_type=MESH).wait_send()

        for t in range(nt):
            for rel, (tx, ty) in enumerate(_other_chips(x, y)):
                slot = land_refs[t].at[2 * tx + ty]
                pltpu.make_async_remote_copy(
                    src_ref=slot, dst_ref=slot, send_sem=send_sems.at[6 * t], recv_sem=recv_sems.at[3 * t + rel],
                    device_id=(x, y, c), device_id_type=MESH).wait_recv()

    outs = pl.pallas_call(
        body, name=name,
        out_shape=tuple(pltpu.HBM(a.shape, a.dtype) for a in lands),
        in_specs=[HBM] * (2 * nt) + [SEM, SEM, ANY],
        out_specs=tuple([HBM] * nt),
        input_output_aliases={nt + t: t for t in range(nt)},
        scratch_shapes=[pltpu.VMEM(a.shape, a.dtype) for a in shards] + [pltpu.SemaphoreType.DMA],
        compiler_params=pltpu.CompilerParams(has_side_effects=EFFECT, vmem_limit_bytes=VMEM_LIMIT),
    )(*[_in_hbm(a) for a in shards], *lands, send_sems, recv_sems, after)
    return list(outs)


def _tie(a, token):
    def body(a_ref, token_ref, o_ref):
        pass

    return pl.pallas_call(
        body, name="tie", in_specs=[ANY, ANY], out_specs=ANY,
        out_shape=jax.ShapeDtypeStruct(a.shape, a.dtype), input_output_aliases={0: 0},
    )(a, token)


def _sum_share(xs, name):
    _, r, cols = xs.shape
    tr = max(t for t in range(16, min(r, 704) + 1, 16) if r % t == 0)
    nblk = r // tr

    def body(x_ref, out_ref, acc_ref, send_sems, local_sems, recv_sem):
        i = pl.program_id(0)
        slot = i % 2
        x, y, c = _place()

        def copies(s, blk):
            dst = out_ref.at[c, pl.ds(blk * tr, tr), :]
            loc = pltpu.make_async_copy(acc_ref.at[s], dst, local_sems.at[s])
            rem = pltpu.make_async_remote_copy(src_ref=acc_ref.at[s], dst_ref=dst, send_sem=send_sems.at[s],
                                               recv_sem=recv_sem, device_id=(x, y, 1 - c), device_id_type=MESH)
            return loc, rem

        @pl.when(i >= 2)
        def _():
            loc, rem = copies(slot, i - 2)
            loc.wait()
            rem.wait_send()

        acc = x_ref[0].astype(F32)
        for k in range(1, N_DEV):
            acc = acc + x_ref[k].astype(F32)
        acc_ref[slot] = acc
        loc, rem = copies(slot, i)
        loc.start()
        rem.start()

        @pl.when(i == nblk - 1)
        def _():
            for back in range(min(2, nblk)):
                blk = nblk - 1 - back
                loc, rem = copies(blk % 2, blk)
                loc.wait()
                rem.wait_send()
            theirs = out_ref.at[1 - c]
            pltpu.make_async_remote_copy(src_ref=theirs, dst_ref=theirs, send_sem=send_sems.at[0],
                                         recv_sem=recv_sem, device_id=(x, y, 1 - c),
                                         device_id_type=MESH).wait_recv()

    return pl.pallas_call(
        body, name=name, grid=(nblk,),
        in_specs=[pl.BlockSpec((N_DEV, tr, cols), lambda i: (0, i, 0))],
        out_specs=ANY,
        out_shape=jax.ShapeDtypeStruct((2, r, cols), F32),
        scratch_shapes=[pltpu.VMEM((2, tr, cols), F32), pltpu.SemaphoreType.DMA((2,)),
                        pltpu.SemaphoreType.DMA((2,)), pltpu.SemaphoreType.DMA],
        compiler_params=_params("arbitrary"),
    )(xs)


def _sum8(xs, name):
    _, r, cols = xs.shape
    tr = 8
    for cand in (256, 128, 64, 32, 16):
        if r % cand == 0:
            tr = cand
            break

    def body(x_ref, o_ref):
        acc = x_ref[0].astype(F32)
        for k in range(1, N_DEV):
            acc = acc + x_ref[k].astype(F32)
        o_ref[...] = acc

    return pl.pallas_call(
        body, name=name, grid=(r // tr,),
        in_specs=[pl.BlockSpec((N_DEV, tr, cols), lambda i: (0, i, 0))],
        out_specs=pl.BlockSpec((tr, cols), lambda i: (i, 0)),
        out_shape=jax.ShapeDtypeStruct((r, cols), F32),
        compiler_params=_params("parallel"),
    )(xs)


def _adamw(w, g, m, v, name):
    R, C = w.shape
    tr = R
    for cand in (256, 128, 64, 32, 16, 8):
        if R % cand == 0:
            tr = cand
            break
    c1 = 1.0 / (1.0 - ADAM_B1 ** ADAM_STEP)
    c2 = 1.0 / (1.0 - ADAM_B2 ** ADAM_STEP)

    def body(w_ref, g_ref, m_ref, v_ref, d_ref, nm_ref, nv_ref):
        gv = g_ref[...]
        nm = ADAM_B1 * m_ref[...] + (1.0 - ADAM_B1) * gv
        nv = ADAM_B2 * v_ref[...] + (1.0 - ADAM_B2) * (gv * gv)
        d_ref[...] = (-ADAM_LR) * ((nm * c1) / (jnp.sqrt(nv * c2) + ADAM_EPS) + ADAM_WD * w_ref[...])
        nm_ref[...] = nm
        nv_ref[...] = nv

    spec = pl.BlockSpec((tr, C), lambda i: (i, 0))
    shape = jax.ShapeDtypeStruct((R, C), F32)
    return pl.pallas_call(
        body, name=name, grid=(R // tr,), in_specs=[spec] * 4, out_specs=[spec] * 3, out_shape=[shape] * 3,
        compiler_params=_params("parallel"),
    )(w, g, m, v)


def _flat_rows(parts, rows):
    flat = jnp.concatenate([q.reshape(-1) for q in parts])
    flat = jnp.pad(flat, (0, rows * LANES - flat.shape[0]))
    return flat.reshape(rows, LANES)


def _round_up(n, m):
    return (n + m - 1) // m * m


def _block_diag(w):
    H, n, _ = w.shape
    eye = jnp.eye(H, dtype=w.dtype)
    return (eye[:, None, :, None] * w[:, :, None, :]).reshape(H * n, H * n)


def _diag_blocks(w, H, n):
    w4 = w.reshape(H, n, H, n)
    return jnp.stack([w4[h, :, h, :] for h in range(H)])


W_IN_T = ("ffn1_w_in", "w_in", "ffn2_w_in")


def _ffn_in_weights(g_in):
    zeros = jnp.zeros((FF_HALF - FF_SHARD, D), g_in.dtype)
    wg_t = jnp.concatenate([g_in[0], zeros, g_in[1], zeros], axis=0)
    wu_t = jnp.concatenate([g_in[2], zeros, g_in[3], zeros], axis=0)
    return wg_t, wu_t


def _ffn_out_weights(g_out):
    zeros = jnp.zeros((FF_HALF - FF_SHARD, D), g_out.dtype)
    return jnp.concatenate([g_out[0], g_out[1], zeros, g_out[2], g_out[3], zeros], axis=0)


LAND_SHAPES = {"ffn1_w_in": (FF_SHARD, D), "ffn1_w_out": (FF_ROWS, D), "w_in": (D_IN // N_CHIP, D),
               "w_out": (D // N_CHIP, D), "ffn2_w_in": (FF_SHARD, D), "ffn2_w_out": (FF_ROWS, D)}


def _rows_window(arr, start, size):
    return arr, lambda r: r.at[pl.ds(start, size), :]


def _ffn_grad_windows(dwg_t, dwu_t, dwout):
    w_in = [_rows_window(dwg_t if j < 2 else dwu_t, (j % 2) * FF_HALF, FF_SHARD) for j in range(N_CHIP)]
    w_out = [_rows_window(dwout, (j // 2) * FF_HALF + (j % 2) * FF_ROWS, FF_ROWS) for j in range(N_CHIP)]
    return w_in, w_out


def _mix_grad_windows(dwin_t, dwo):
    win = [_rows_window(dwin_t, j * (D_IN // N_CHIP), D_IN // N_CHIP) for j in range(N_CHIP)]
    wo = [_rows_window(dwo, j * (D // N_CHIP), D // N_CHIP) for j in range(N_CHIP)]
    return win, wo


def kernel(x, ffn1_norm, ffn1_w_in, ffn1_w_out, mix_norm, w_in, conv_w, conv_b, rg_w_a, rg_b_a, rg_w_x, rg_b_x, lru_lambda, pool_w, pool_scale, sgu_norm, sgu_w, sgu_b, w_out, ffn2_norm, ffn2_w_in, ffn2_w_out, final_norm, loss_target, m_ffn1_norm, m_ffn1_w_in, m_ffn1_w_out, m_mix_norm, m_w_in, m_conv_w, m_conv_b, m_rg_w_a, m_rg_b_a, m_rg_w_x, m_rg_b_x, m_lru_lambda, m_pool_w, m_pool_scale, m_sgu_norm, m_sgu_w, m_sgu_b, m_w_out, m_ffn2_norm, m_ffn2_w_in, m_ffn2_w_out, m_final_norm, v_ffn1_norm, v_ffn1_w_in, v_ffn1_w_out, v_mix_norm, v_w_in, v_conv_w, v_conv_b, v_rg_w_a, v_rg_b_a, v_rg_w_x, v_rg_b_x, v_lru_lambda, v_pool_w, v_pool_scale, v_sgu_norm, v_sgu_w, v_sgu_b, v_w_out, v_ffn2_norm, v_ffn2_w_in, v_ffn2_w_out, v_final_norm):
    args = locals()
    W = {n: args[n] for n in WEIGHTS}
    M = {n: args["m_" + n] for n in WEIGHTS}
    V = {n: args["v_" + n] for n in WEIGHTS}
    depth = ffn1_norm.shape[0]
    T = x.shape[1]
    xi, yi, ci = _place()
    chip = 2 * xi + yi

    assert depth == 2, "core c of a chip sends and reduces layer c"
    groups = [(l, names) for l in range(depth)
              for names in (["ffn1_w_in"], ["ffn1_w_out", "w_in", "w_out"], ["ffn2_w_in", "ffn2_w_out"])]
    def stored(a, n):
        return jnp.swapaxes(a, 1, 2) if n in W_IN_T else a

    wb = {n: stored(W[n], n).astype(BF16) for n in BIG}
    groups[1][1].append("conv_w")
    conv_shard = conv_w.reshape(-1, conv_w.shape[-1])
    flights = {}

    def weights_start(k, dep=None):
        l, names = groups[k]
        shards = [conv_shard if n == "conv_w" else wb[n][l] for n in names]
        if dep is not None:
            shards[0] = _tie(shards[0], dep)
        lands = [lax.empty((N_CHIP,) + s.shape, s.dtype) for s in shards]
        send, recv, lands, token = _gather_start(l, shards, lands, "weights_start_%d" % k)
        flights[k] = (shards, lands, send, recv)
        return token

    def weights_wait(k, after):
        l, names = groups[k]
        shards, lands, send, recv = flights[k]
        got = _gather_wait(l, shards, lands, send, recv, after, "weights_wait_%d" % k)
        token = weights_start(k + 2, got[0]) if k + 2 < len(groups) else None
        return dict(zip(names, got)), token

    def after_start(a, token):
        return a if token is None else _tie(a, token)

    first_tokens = [weights_start(0), weights_start(1)]

    layers = []
    for l in range(depth):
        L = {f: dict(norm=W[f + "_norm"][l][None]) for f in ("ffn1", "ffn2")}
        ws = jnp.where(jnp.tril(jnp.ones((CHUNK, CHUNK), bool))[None], sgu_w[l], 0.0)
        wax = jnp.concatenate([_block_diag(rg_w_a[l]), _block_diag(rg_w_x[l])], axis=1)
        wpool = _block_diag(pool_w[l])
        L["mix"] = dict(
            conv_b=conv_b[l][None], wax=wax.astype(BF16), wax_t=wax.T.astype(BF16),
            bax=jnp.concatenate([rg_b_a[l].reshape(-1), rg_b_x[l].reshape(-1)])[None], lam=lru_lambda[l][None],
            wpool=wpool.astype(BF16), wpool_t=wpool.T.astype(BF16), pool_scale=pool_scale[l][None],
            sgu_norm=sgu_norm[l][None], ws=ws.astype(BF16), ws_t=jnp.swapaxes(ws, 1, 2).astype(BF16),
            bz=jnp.repeat(sgu_b[l].T, 64, axis=1))
        L["mix_norm"] = mix_norm[l][None]
        layers.append(L)
    for token in first_tokens:
        layers[0]["ffn1"]["norm"] = _tie(layers[0]["ffn1"]["norm"], token)

    xs = x[0]
    saved = []
    for l, L in enumerate(layers):
        F1, F2 = L["ffn1"], L["ffn2"]
        h = _rms_fwd(xs, F1["norm"])
        got, token = weights_wait(3 * l, h)
        F1["wg"], F1["wu"] = _ffn_in_weights(got["ffn1_w_in"])
        h = after_start(h, token)
        g, u, a = _ffn_in(h, F1["wg"], F1["wu"])
        got, token = weights_wait(3 * l + 1, a)
        F1["wout"] = _ffn_out_weights(got["ffn1_w_out"])
        L["w_in"] = got["w_in"].reshape(D_IN, D)
        L["w_out"] = jnp.concatenate([got["w_out"][j] for j in range(N_CHIP)], axis=0)
        if "conv_w" in got:
            conv_full = jnp.concatenate([got["conv_w"][j] for j in range(N_CHIP)], axis=1)
            for ll in range(depth):
                layers[ll]["mix"]["conv_w"] = conv_full.reshape(depth, 4, D_RNN)[ll]
        a = after_start(a, token)
        x1 = _mm_res(a, F1["wout"], xs, 0.5, "ffn_out", tm=512, tn=D)
        s1 = (xs, h, g, u, a)
        hm = _rms_fwd(x1, L["mix_norm"])
        p = _mm_nt(hm, L["w_in"], "mix_in", tm=512, tn=D_IN)
        ycat, hs = _mix_fwd(p, L["mix"])
        x2 = _mm_res(ycat, L["w_out"], x1, 1.0, "mix_out", tm=512, tn=D)
        h2 = _rms_fwd(x2, F2["norm"])
        got, token = weights_wait(3 * l + 2, h2)
        F2["wg"], F2["wu"] = _ffn_in_weights(got["ffn2_w_in"])
        F2["wout"] = _ffn_out_weights(got["ffn2_w_out"])
        h2 = after_start(h2, token)
        g2, u2, a2 = _ffn_in(h2, F2["wg"], F2["wu"])
        x3 = _mm_res(a2, F2["wout"], x2, 0.5, "ffn_out", tm=512, tn=D)
        saved.append((s1, (x1, hm, p, ycat, hs), (x2, h2, g2, u2, a2)))
        xs = x3

    dx, dxb, d_final, loss_part = _final(xs, loss_target[0], final_norm[None], 0.5)

    G = {n: [None] * depth for n in SMALL if n != "final_norm"}
    lands = {n: lax.empty((N_DEV,) + LAND_SHAPES[n], BF16) for n in BIG}
    in_flight = []

    def send_grads(l, windows, tag):
        names = list(windows)
        send, recv, thru, token = _exchange_start(l, [windows[n] for n in names], [lands[n] for n in names],
                                                  "grads_start_" + tag)
        lands.update(zip(names, thru))
        in_flight.append((l, names, [windows[n] for n in names], send, recv, "grads_wait_" + tag))
        return token

    def ffn_bwd(dx, dxb, F, s, f, l, pending, send_now):
        xin, h, g, u, a = s
        dg, du = _ffn_mid_bwd(dxb, F["wout"], g, u)
        dwout = _mm_tn(a, dxb, 1.0, "ffn_dwout")
        dwg = _mm_tn(dg, h, 1.0, "ffn_dwg")
        dwu = _mm_tn(du, h, 1.0, "ffn_dwu")
        pending[f + "_w_in"], pending[f + "_w_out"] = _ffn_grad_windows(dwg, dwu, dwout)
        if send_now:
            dg = _tie(dg, send_grads(l, pending, "l%d_%s" % (l, f)))
        dx, dxb, dn = _dh_rms_bwd([(dg, F["wg"]), (du, F["wu"])], xin, F["norm"], dx,
                                  1.0 if f == "ffn2" else 0.5, "ffn_dh")
        G[f + "_norm"][l] = dn[0]
        return dx, dxb

    for l in reversed(range(depth)):
        L = layers[l]
        s1, (x1, hm, p, ycat, hs), s2 = saved[l]
        pending = {}
        dx, dxb = ffn_bwd(dx, dxb, L["ffn2"], s2, "ffn2", l, pending, l == 0)
        if l == 0:
            pending = {}
        dycat = _mm_nt(dxb, L["w_out"], "mix_dy", tm=512, tn=D)
        dwo = _mm_tn(ycat, dxb, 1.0, "mix_dwout")
        mg = _mix_bwd(dycat, p, hs, L["mix"])
        dwin = _mm_tn(mg["dp"], hm, 1.0, "mix_dwin")
        pending["w_in"], pending["w_out"] = _mix_grad_windows(dwin, dwo)
        if l == 0:
            dp = _tie(mg["dp"], send_grads(l, pending, "l0_mix"))
            pending = {}
        else:
            dp = mg["dp"]
        dx, dxb, dn = _dh_rms_bwd([(dp, L["w_in"])], x1, L["mix_norm"], dx, 0.5, "mix_dh")
        G["mix_norm"][l] = dn[0]
        G["conv_w"][l], G["conv_b"][l] = mg["conv_w"], mg["conv_b"][0]
        G["rg_w_a"][l] = _diag_blocks(mg["wax"][:, :D_RNN], 8, 64)
        G["rg_w_x"][l] = _diag_blocks(mg["wax"][:, D_RNN:], 8, 64)
        G["rg_b_a"][l] = mg["bax"][0, :D_RNN].reshape(8, 64)
        G["rg_b_x"][l] = mg["bax"][0, D_RNN:].reshape(8, 64)
        G["lru_lambda"][l] = mg["lam"][0]
        G["pool_w"][l] = _diag_blocks(mg["wpool"], 4, 64)
        G["pool_scale"][l], G["sgu_norm"][l] = mg["pool_scale"][0], mg["sgu_norm"][0]
        G["sgu_w"][l] = mg["ws"]
        G["sgu_b"][l] = mg["bz"].reshape(CHUNK, 4, 64).sum(-1).T
        dx, dxb = ffn_bwd(dx, dxb, L["ffn1"], s1, "ffn1", l, pending, l == 0)
        if l > 0:
            dxb = _tie(dxb, send_grads(l, pending, "l%d" % l))
    grad_x = dx[None]
    G = {n: jnp.stack(v) for n, v in G.items()}
    G["final_norm"] = d_final[0]

    for l, names, windows, send, recv, tag in in_flight:
        lands.update(zip(names, _exchange_wait(l, windows, [lands[n] for n in names], send, recv, dx, tag)))
    both = [_sum_share(lands[n], "sum_share_" + n) for n in BIG]
    grads = dict(zip(BIG, both))

    small_sizes = [int(np.prod(G[n].shape)) for n in SMALL]
    srows = _round_up(sum(small_sizes) + 1, N_DEV * 8 * LANES) // (N_DEV * LANES)
    sflat = _flat_rows([G[n] for n in SMALL] + [loss_part[0, :1]], N_DEV * srows)
    sgot = _all_to_all(sflat.reshape(N_DEV, srows, LANES), "exchange_small_grads")
    sall = _all_gather8(_sum8(sgot, "sum_small_grads"), "share_small_grads").reshape(-1)
    off = 0
    for n, size in zip(SMALL, small_sizes):
        grads[n] = sall[off:off + size].reshape(G[n].shape)
        off += size
    loss = sall[off]
    grads["conv_w"] = lax.dynamic_slice_in_dim(grads["conv_w"], chip * conv_w.shape[2], conv_w.shape[2], axis=2)

    delta, new_m, new_v = {}, {}, {}
    for n in BIG:
        shp = grads[n].shape
        two_d = (shp[0] * shp[1], shp[2])
        outs = _adamw(stored(W[n], n).reshape(two_d), grads[n].reshape(two_d), stored(M[n], n).reshape(two_d),
                      stored(V[n], n).reshape(two_d), "adamw_" + n)
        delta[n], new_m[n], new_v[n] = (stored(o.reshape(shp), n) for o in outs)
        grads[n] = stored(grads[n], n)
    arows = _round_up(sum(int(np.prod(W[n].shape)) for n in SMALL), 8 * LANES) // LANES
    outs = _adamw(*(_flat_rows([src[n] for n in SMALL], arows) for src in (W, grads, M, V)), "adamw_small")
    outs = [o.reshape(-1) for o in outs]
    off = 0
    for n in SMALL:
        size = int(np.prod(W[n].shape))
        delta[n], new_m[n], new_v[n] = (o[off:off + size].reshape(W[n].shape) for o in outs)
        off += size

    return (loss, grad_x, *[grads[n] for n in WEIGHTS], *[delta[n] for n in WEIGHTS],
            *[new_m[n] for n in WEIGHTS], *[new_v[n] for n in WEIGHTS])
```

```python
import math

import jax
import jax.numpy as jnp
import numpy as np
from jax import lax
from jax.experimental import pallas as pl
from jax.experimental.pallas import tpu as pltpu

F32 = jnp.float32
BF16 = jnp.bfloat16
MESH = pl.DeviceIdType.MESH

D = 1024
D_RNN = 512
D_POOL = 256
D_SGU = 256
D_IN = 1792
D_FF = 2752
D_FFP = 2816
N_CHIP = 4
FF_SHARD = D_FF // 2
FF_HALF = D_FFP // 2
FF_ROWS = D_FF // N_CHIP
CHUNK = 128
HALO = 16
EPS = 1e-6
LRU_C = 8.0
N_DEV = 8
LANES = 1024
VMEM_LIMIT = 56 * 1024 * 1024

ADAM_LR, ADAM_B1, ADAM_B2, ADAM_EPS, ADAM_WD, ADAM_STEP = 0.001, 0.9, 0.999, 1e-08, 0.01, 10

BIG = ("ffn1_w_in", "ffn1_w_out", "w_in", "w_out", "ffn2_w_in", "ffn2_w_out")
SMALL = ("ffn1_norm", "mix_norm", "conv_w", "conv_b", "rg_w_a", "rg_b_a", "rg_w_x", "rg_b_x", "lru_lambda",
         "pool_w", "pool_scale", "sgu_norm", "sgu_w", "sgu_b", "ffn2_norm", "final_norm")
WEIGHTS = ("ffn1_norm", "ffn1_w_in", "ffn1_w_out", "mix_norm", "w_in", "conv_w", "conv_b", "rg_w_a", "rg_b_a",
           "rg_w_x", "rg_b_x", "lru_lambda", "pool_w", "pool_scale", "sgu_norm", "sgu_w", "sgu_b", "w_out",
           "ffn2_norm", "ffn2_w_in", "ffn2_w_out", "final_norm")


def _params(*sem):
    return pltpu.CompilerParams(dimension_semantics=sem, vmem_limit_bytes=VMEM_LIMIT)


def _gelu(x):
    c = math.sqrt(2.0 / math.pi)
    t = jnp.tanh(c * (x + 0.044715 * (x * x * x)))
    return 0.5 * x * (1.0 + t)


def _gelu_and_grad(x):
    c = math.sqrt(2.0 / math.pi)
    x2 = x * x
    t = jnp.tanh(c * (x + 0.044715 * (x2 * x)))
    g = 0.5 * x * (1.0 + t)
    dg = 0.5 * (1.0 + t) + 0.5 * x * (1.0 - t * t) * (c * (1.0 + 3.0 * 0.044715 * x2))
    return g, dg


def _sigmoid(x):
    return 0.5 * jnp.tanh(0.5 * x) + 0.5


def _dot(a, b):
    return jnp.dot(a, b, preferred_element_type=F32)


def _dot_tn(a, b):
    return lax.dot_general(a, b, (((0,), (0,)), ((), ())), preferred_element_type=F32)


def _dot_nt(a, b):
    return lax.dot_general(a, b, (((1,), (1,)), ((), ())), preferred_element_type=F32)


def _tile(n, limit):
    best = 128
    for t in range(128, min(n, limit) + 1, 128):
        if n % t == 0:
            best = t
    assert n % best == 0, (n, limit)
    return best


def _mm_res(a, b, res, scale, name, tm=1024, tn=512):
    M, K = a.shape
    N = b.shape[1]
    tm, tn = min(tm, M), _tile(N, tn)

    def body(a_ref, b_ref, r_ref, o_ref):
        o_ref[...] = r_ref[...] + scale * _dot(a_ref[...], b_ref[...])

    return pl.pallas_call(
        body, name=name, grid=(M // tm, N // tn),
        in_specs=[pl.BlockSpec((tm, K), lambda i, j: (i, 0)), pl.BlockSpec((K, tn), lambda i, j: (0, j)),
                  pl.BlockSpec((tm, tn), lambda i, j: (i, j))],
        out_specs=pl.BlockSpec((tm, tn), lambda i, j: (i, j)),
        out_shape=jax.ShapeDtypeStruct((M, N), F32),
        compiler_params=_params("parallel", "parallel"),
    )(a, b, res)


def _mm_nt(a, b, name, tm=1024, tn=512):
    M, K = a.shape
    N = b.shape[0]
    tm, tn = min(tm, M), _tile(N, tn)

    def body(a_ref, b_ref, o_ref):
        o_ref[...] = _dot_nt(a_ref[...], b_ref[...])

    return pl.pallas_call(
        body, name=name, grid=(M // tm, N // tn),
        in_specs=[pl.BlockSpec((tm, K), lambda i, j: (i, 0)), pl.BlockSpec((tn, K), lambda i, j: (j, 0))],
        out_specs=pl.BlockSpec((tm, tn), lambda i, j: (i, j)),
        out_shape=jax.ShapeDtypeStruct((M, N), F32),
        compiler_params=_params("parallel", "parallel"),
    )(a, b)


def _dh_rms_bwd(pairs, x, g, dres, copy_scale, name, tm=512):
    T = x.shape[0]
    tm = min(tm, T)
    n = len(pairs)

    def body(*refs):
        ab = refs[:2 * n]
        x_ref, g_ref, dres_ref, dx_ref, dxb_ref, dg_ref = refs[2 * n:]
        dy = _dot(ab[0][...], ab[1][...])
        for k in range(1, n):
            dy = dy + _dot(ab[2 * k][...], ab[2 * k + 1][...])
        xv = x_ref[...]
        r = lax.rsqrt(jnp.mean(xv * xv, axis=-1, keepdims=True) + EPS)
        xhat = xv * r
        dxhat = dy * g_ref[...]
        dx = dres_ref[...] + r * (dxhat - xhat * jnp.mean(dxhat * xhat, axis=-1, keepdims=True))
        dx_ref[...] = dx
        dxb_ref[...] = (copy_scale * dx).astype(BF16)

        @pl.when(pl.program_id(0) == 0)
        def _():
            dg_ref[...] = jnp.zeros_like(dg_ref)

        dg_ref[...] += jnp.sum(dy * xhat, axis=0, keepdims=True)

    row = pl.BlockSpec((tm, D), lambda i: (i, 0))
    vec = pl.BlockSpec((1, D), lambda i: (0, 0))
    in_specs, operands = [], []
    for a, b in pairs:
        in_specs += [pl.BlockSpec((tm, a.shape[1]), lambda i: (i, 0)),
                     pl.BlockSpec(b.shape, lambda i: (0, 0), pipeline_mode=pl.Buffered(1))]
        operands += [a, b]
    return pl.pallas_call(
        body, name=name, grid=(T // tm,),
        in_specs=in_specs + [row, vec, row], out_specs=[row, row, vec],
        out_shape=[jax.ShapeDtypeStruct((T, D), F32), jax.ShapeDtypeStruct((T, D), BF16),
                   jax.ShapeDtypeStruct((1, D), F32)],
        compiler_params=_params("arbitrary"),
    )(*operands, x, g, dres)


def _mm_tn(a, b, scale, name, tm=1792, tn=1792, tk=2048):
    T, M = a.shape
    N = b.shape[1]
    tm, tn, tk = _tile(M, tm), _tile(N, tn), min(tk, T)
    nk = T // tk

    def body(a_ref, b_ref, o_ref, acc_ref):
        k = pl.program_id(2)

        @pl.when(k == 0)
        def _():
            acc_ref[...] = jnp.zeros_like(acc_ref)

        acc_ref[...] += _dot_tn(a_ref[...], b_ref[...])

        @pl.when(k == nk - 1)
        def _():
            o_ref[...] = (scale * acc_ref[...]).astype(BF16)

    return pl.pallas_call(
        body, name=name, grid=(M // tm, N // tn, nk),
        in_specs=[pl.BlockSpec((tk, tm), lambda i, j, k: (k, i)), pl.BlockSpec((tk, tn), lambda i, j, k: (k, j))],
        out_specs=pl.BlockSpec((tm, tn), lambda i, j, k: (i, j)),
        out_shape=jax.ShapeDtypeStruct((M, N), BF16),
        scratch_shapes=[pltpu.VMEM((tm, tn), F32)],
        compiler_params=_params("parallel", "parallel", "arbitrary"),
    )(a, b)


def _rms_rows(x_ref, gain_ref):
    xv = x_ref[...]
    r = lax.rsqrt(jnp.mean(xv * xv, axis=-1, keepdims=True) + EPS)
    return (xv * r * gain_ref[...]).astype(BF16)


def _ffn_in(x, gain, wg_t, wu_t, tm=512, tn=FF_HALF):
    T = x.shape[0]
    tm = min(tm, T)

    def body(x_ref, gain_ref, wg_ref, wu_ref, h_ref, g_ref, u_ref, a_ref):
        hv = _rms_rows(x_ref, gain_ref)

        @pl.when(pl.program_id(0) == 0)
        def _():
            h_ref[...] = hv

        g = _dot_nt(hv, wg_ref[...])
        u = _dot_nt(hv, wu_ref[...])
        g_ref[...] = g.astype(BF16)
        u_ref[...] = u.astype(BF16)
        a_ref[...] = (g * _sigmoid(g) * u).astype(BF16)

    row = pl.BlockSpec((tm, D), lambda j, i: (i, 0))
    last = T // tm - 1
    h_spec = pl.BlockSpec((tm, D), lambda j, i: (jnp.where(j == 0, i, last), 0))
    wspec = pl.BlockSpec((tn, D), lambda j, i: (j, 0))
    ospec = pl.BlockSpec((tm, tn), lambda j, i: (i, j))
    oshape = jax.ShapeDtypeStruct((T, D_FFP), BF16)
    return pl.pallas_call(
        body, name="ffn_in", grid=(D_FFP // tn, T // tm),
        in_specs=[row, pl.BlockSpec((1, D), lambda j, i: (0, 0)), wspec, wspec],
        out_specs=[h_spec, ospec, ospec, ospec],
        out_shape=[jax.ShapeDtypeStruct((T, D), BF16), oshape, oshape, oshape],
        compiler_params=_params("arbitrary", "arbitrary"),
    )(x, gain, wg_t, wu_t)


def _mix_in(x, gain, w_in_t, tm=512):
    T = x.shape[0]
    tm = min(tm, T)

    def body(x_ref, gain_ref, w_ref, h_ref, p_ref):
        hv = _rms_rows(x_ref, gain_ref)
        h_ref[...] = hv
        p_ref[...] = _dot_nt(hv, w_ref[...])

    row = pl.BlockSpec((tm, D), lambda i: (i, 0))
    return pl.pallas_call(
        body, name="mix_in", grid=(T // tm,),
        in_specs=[row, pl.BlockSpec((1, D), lambda i: (0, 0)), pl.BlockSpec((D_IN, D), lambda i: (0, 0))],
        out_specs=[row, pl.BlockSpec((tm, D_IN), lambda i: (i, 0))],
        out_shape=[jax.ShapeDtypeStruct((T, D), BF16), jax.ShapeDtypeStruct((T, D_IN), F32)],
        compiler_params=_params("parallel"),
    )(x, gain, w_in_t)


def _ffn_mid_bwd(dyh, wout, g, u, tm=512, tn=FF_HALF):
    T = dyh.shape[0]
    tm = min(tm, T)

    def body(dy_ref, w_ref, g_ref, u_ref, dg_ref, du_ref):
        da = _dot_nt(dy_ref[...], w_ref[...])
        g = g_ref[...].astype(F32)
        s = _sigmoid(g)
        gs = g * s
        du_ref[...] = (da * gs).astype(BF16)
        dg_ref[...] = ((da * u_ref[...].astype(F32)) * (s + gs - gs * s)).astype(BF16)

    ospec = pl.BlockSpec((tm, tn), lambda j, i: (i, j))
    oshape = jax.ShapeDtypeStruct((T, D_FFP), BF16)
    return pl.pallas_call(
        body, name="ffn_mid_bwd", grid=(D_FFP // tn, T // tm),
        in_specs=[pl.BlockSpec((tm, D), lambda j, i: (i, 0)), pl.BlockSpec((tn, D), lambda j, i: (j, 0)),
                  ospec, ospec],
        out_specs=[ospec, ospec], out_shape=[oshape, oshape],
        compiler_params=_params("parallel", "parallel"),
    )(dyh, wout, g, u)


def _final(x, tgt, gf, copy_scale, tm=512):
    T = x.shape[0]
    tm = min(tm, T)

    def body(x_ref, t_ref, g_ref, dx_ref, dxb_ref, dg_ref, loss_ref):
        xv = x_ref[...]
        r = lax.rsqrt(jnp.mean(xv * xv, axis=-1, keepdims=True) + EPS)
        xhat = xv * r
        err = xhat * g_ref[...] - t_ref[...]
        dy = err * (1.0 / D)
        dxhat = dy * g_ref[...]
        dx = r * (dxhat - xhat * jnp.mean(dxhat * xhat, axis=-1, keepdims=True))
        dx_ref[...] = dx
        dxb_ref[...] = (copy_scale * dx).astype(BF16)

        @pl.when(pl.program_id(0) == 0)
        def _():
            dg_ref[...] = jnp.zeros_like(dg_ref)
            loss_ref[...] = jnp.zeros_like(loss_ref)

        dg_ref[...] += jnp.sum(dy * xhat, axis=0, keepdims=True)
        loss_ref[...] += (0.5 / D) * jnp.sum(err * err)

    row = pl.BlockSpec((tm, D), lambda i: (i, 0))
    vec = pl.BlockSpec((1, D), lambda i: (0, 0))
    return pl.pallas_call(
        body, name="final_loss", grid=(T // tm,),
        in_specs=[row, row, vec],
        out_specs=[row, row, vec, pl.BlockSpec((1, 128), lambda i: (0, 0))],
        out_shape=[jax.ShapeDtypeStruct((T, D), F32), jax.ShapeDtypeStruct((T, D), BF16),
                   jax.ShapeDtypeStruct((1, D), F32), jax.ShapeDtypeStruct((1, 128), F32)],
        compiler_params=_params("arbitrary"),
    )(x, tgt, gf)


def _mix_block(T, limit):
    return min(limit, T // 2)


def _rows(tb, width):
    return lax.broadcasted_iota(jnp.int32, (tb, width), 0)


def _rglru_gates(xc, wax_ref, bax_ref, lam_ref):
    pre = _dot(xc.astype(BF16), wax_ref[...]) + bax_ref[...]
    r = _sigmoid(pre[:, :D_RNN])
    ig = _sigmoid(pre[:, D_RNN:])
    z = -lam_ref[...]
    sp = jnp.maximum(z, 0.0) + jnp.log(1.0 + jnp.exp(-jnp.abs(z)))
    log_a = (-LRU_C) * r * sp
    a = jnp.exp(log_a)
    mult = jnp.sqrt(-jnp.tanh(log_a) * (1.0 + a * a))
    return r, ig, sp, a, mult


def _conv(xa_ext, cw_ref, cb_ref):
    y = cb_ref[...] + cw_ref[3:4, :] * xa_ext
    for k in range(1, 4):
        y = y + cw_ref[3 - k:4 - k, :] * pltpu.roll(xa_ext, k, 0)
    return y[HALO:]


def _pool_window_lanes():
    lane = lax.broadcasted_iota(jnp.int32, (1, D_POOL), 1)
    return jnp.where(lane < 64, 2, jnp.where(lane < 128, 4, jnp.where(lane < 192, 8, 16)))


def _pool_select(s2, s4, s8, s16):
    lane = lax.broadcasted_iota(jnp.int32, s2.shape, 1)
    return jnp.where(lane < 64, s2, jnp.where(lane < 128, s4, jnp.where(lane < 192, s8, s16)))


def _pool_diff(xp_ext, t0, tb):
    s2 = xp_ext + pltpu.roll(xp_ext, 1, 0)
    s4 = s2 + pltpu.roll(s2, 2, 0)
    s8 = s4 + pltpu.roll(s4, 4, 0)
    s16 = s8 + pltpu.roll(s8, 8, 0)
    sel = _pool_select(s2, s4, s8, s16)[HALO:]
    cnt = jnp.minimum(t0 + _rows(tb, D_POOL) + 1, _pool_window_lanes()).astype(F32)
    return sel / cnt - xp_ext[HALO:], cnt


def _head_masks():
    lane = lax.broadcasted_iota(jnp.int32, (1, D_SGU), 1)
    return [((lane >= 64 * h) & (lane < 64 * (h + 1))).astype(F32) for h in range(4)]


def _sgu_mix(w_ref, vch, masks):
    z = masks[0] * _dot(w_ref[0], vch)
    for h in range(1, 4):
        z = z + masks[h] * _dot(w_ref[h], vch)
    return z


def _mix_fwd(p, prm):
    T = p.shape[0]
    tb = _mix_block(T, 512)
    nb = T // tb

    def body(p_ref, xah_ref, xph_ref, cw_ref, cb_ref, wax_ref, bax_ref, lam_ref, wp_ref, ps_ref, sgn_ref,
             ws_ref, bz_ref, y_ref, hs_ref, carry_ref):
        i = pl.program_id(0)
        keep = (i > 0).astype(F32)

        @pl.when(i == 0)
        def _():
            carry_ref[...] = jnp.zeros_like(carry_ref)

        xa_ext = jnp.concatenate([xah_ref[...] * keep, p_ref[:, 512:1024]], axis=0)
        xc = _conv(xa_ext, cw_ref, cb_ref)
        r, ig, sp, a, mult = _rglru_gates(xc, wax_ref, bax_ref, lam_ref)
        bv = mult * (ig * xc)
        row = _rows(tb, D_RNN)
        s = 1
        while s < tb:
            m = row >= s
            bv = jnp.where(m, a * pltpu.roll(bv, s, 0) + bv, bv)
            a = jnp.where(m, a * pltpu.roll(a, s, 0), a)
            s *= 2
        h = bv + a * carry_ref[0:1, :]
        hs_ref[...] = h
        last = jnp.sum(jnp.where(_rows(8, D_RNN) == 7, hs_ref[tb - 8:tb, :], 0.0), axis=0, keepdims=True)
        carry_ref[...] = jnp.broadcast_to(last, carry_ref.shape)
        y_ref[:, 0:512] = (_gelu(p_ref[:, 0:512]) * h).astype(BF16)

        xp_ext = jnp.concatenate([xph_ref[...] * keep, p_ref[:, 1024:1280]], axis=0)
        d, _ = _pool_diff(xp_ext, i * tb, tb)
        y_ref[:, 512:768] = (_dot(d.astype(BF16), wp_ref[...]) * ps_ref[...]).astype(BF16)

        ug = _gelu(p_ref[:, 1280:1536])
        vg = _gelu(p_ref[:, 1536:1792])
        rv = lax.rsqrt(jnp.mean(vg * vg, axis=-1, keepdims=True) + EPS)
        vn = (vg * rv * sgn_ref[...]).astype(BF16)
        masks = _head_masks()
        for ci in range(tb // CHUNK):
            sl = slice(ci * CHUNK, (ci + 1) * CHUNK)
            z = _sgu_mix(ws_ref, vn[sl], masks) + bz_ref[...]
            y_ref[sl, 768:1024] = (ug[sl] * z).astype(BF16)

    hb = tb // HALO

    def halo(i):
        return jnp.maximum(i * hb - 1, 0)

    def full(shape):
        return pl.BlockSpec(shape, lambda i: (0,) * len(shape))

    return pl.pallas_call(
        body, name="mix_fwd", grid=(nb,),
        in_specs=[pl.BlockSpec((tb, D_IN), lambda i: (i, 0)),
                  pl.BlockSpec((HALO, D_RNN), lambda i: (halo(i), 1)),
                  pl.BlockSpec((HALO, D_POOL), lambda i: (halo(i), 4)),
                  full((4, D_RNN)), full((1, D_RNN)), full((D_RNN, 2 * D_RNN)), full((1, 2 * D_RNN)),
                  full((1, D_RNN)), full((D_POOL, D_POOL)), full((1, D_POOL)), full((1, D_SGU)),
                  full((4, CHUNK, CHUNK)), full((CHUNK, D_SGU))],
        out_specs=[pl.BlockSpec((tb, D), lambda i: (i, 0)), pl.BlockSpec((tb, D_RNN), lambda i: (i, 0))],
        out_shape=[jax.ShapeDtypeStruct((T, D), BF16), jax.ShapeDtypeStruct((T, D_RNN), F32)],
        scratch_shapes=[pltpu.VMEM((8, D_RNN), F32)],
        compiler_params=_params("arbitrary"),
    )(p, p, p, prm["conv_w"], prm["conv_b"], prm["wax"], prm["bax"], prm["lam"], prm["wpool"], prm["pool_scale"],
      prm["sgu_norm"], prm["ws"], prm["bz"])


def _mix_bwd(dy, p, hs, prm):
    T = p.shape[0]
    tb = _mix_block(T, 256)
    nb = T // tb
    hb = tb // HALO

    def body(dy_ref, p_ref, xah_ref, xph_ref, hs_ref, hsh_ref, cw_ref, cb_ref, wax_ref, waxt_ref, bax_ref,
             lam_ref, wp_ref, wpt_ref, ps_ref, sgn_ref, ws_ref, wst_ref, bz_ref,
             dp_ref, dcw_ref, dcb_ref, dwax_ref, dbax_ref, dlam_ref, dwp_ref, dps_ref, dsgn_ref, dws_ref,
             dbz_ref, gcarry_ref, xcfut_ref, mfut_ref):
        i = pl.program_id(0)
        bi = nb - 1 - i
        keep = (bi > 0).astype(F32)

        @pl.when(i == 0)
        def _():
            for ref in (dcw_ref, dcb_ref, dwax_ref, dbax_ref, dlam_ref, dwp_ref, dps_ref, dsgn_ref, dws_ref,
                        dbz_ref, gcarry_ref, xcfut_ref, mfut_ref):
                ref[...] = jnp.zeros_like(ref)

        xa_ext = jnp.concatenate([xah_ref[...] * keep, p_ref[:, 512:1024]], axis=0)
        xc = _conv(xa_ext, cw_ref, cb_ref)
        r, ig, sp, a, mult = _rglru_gates(xc, wax_ref, bax_ref, lam_ref)
        gg, dgg = _gelu_and_grad(p_ref[:, 0:512])
        dya = dy_ref[:, 0:512]
        dp_ref[:, 0:512] = (dya * hs_ref[...] * dgg).astype(BF16)
        row = _rows(tb, D_RNN)
        g = dya * gg + jnp.where(row == tb - 1, gcarry_ref[0:1, :], 0.0)
        al = pltpu.roll(a, tb - 1, 0)
        s = 1
        while s < tb:
            m = row < tb - s
            g = jnp.where(m, al * pltpu.roll(g, tb - s, 0) + g, g)
            al = jnp.where(m, al * pltpu.roll(al, tb - s, 0), al)
            s *= 2
        first = jnp.sum(jnp.where(_rows(8, D_RNN) == 0, (a * g)[0:8], 0.0), axis=0, keepdims=True)
        gcarry_ref[...] = jnp.broadcast_to(first, gcarry_ref.shape)
        hs_ext = jnp.concatenate([hsh_ref[...] * keep, hs_ref[...]], axis=0)
        h_prev = pltpu.roll(hs_ext, 1, 0)[HALO:]
        ix = ig * xc
        dlog_a = g * h_prev * a - (g * ix) * (a * a / mult)
        dlam_ref[...] += jnp.sum(dlog_a * r, axis=0, keepdims=True) * (LRU_C * _sigmoid(-lam_ref[...]))
        dpre_r = dlog_a * ((-LRU_C) * sp) * (r * (1.0 - r))
        dpre_i = (g * mult * xc) * (ig * (1.0 - ig))
        dpre = jnp.concatenate([dpre_r, dpre_i], axis=1)
        dbax_ref[...] += jnp.sum(dpre, axis=0, keepdims=True)
        dpre_b = dpre.astype(BF16)
        dwax_ref[...] += _dot_tn(xc.astype(BF16), dpre_b)
        dxc = g * mult * ig + _dot(dpre_b, waxt_ref[...])
        dcb_ref[...] += jnp.sum(dxc, axis=0, keepdims=True)
        for k in range(4):
            xs = xa_ext[HALO:] if k == 3 else pltpu.roll(xa_ext, 3 - k, 0)[HALO:]
            dcw_ref[k:k + 1, :] += jnp.sum(dxc * xs, axis=0, keepdims=True)
        dxc_ext = jnp.concatenate([dxc, xcfut_ref[...]], axis=0)
        n = tb + HALO
        dxa = cw_ref[3:4, :] * dxc_ext
        for k in range(1, 4):
            dxa = dxa + cw_ref[3 - k:4 - k, :] * pltpu.roll(dxc_ext, n - k, 0)
        dp_ref[:, 512:1024] = dxa[:tb].astype(BF16)
        xcfut_ref[...] = dxc[0:HALO]

        xp_ext = jnp.concatenate([xph_ref[...] * keep, p_ref[:, 1024:1280]], axis=0)
        d, cnt = _pool_diff(xp_ext, bi * tb, tb)
        db = d.astype(BF16)
        dyb = dy_ref[:, 512:768]
        dps_ref[...] += jnp.sum(dyb * _dot(db, wp_ref[...]), axis=0, keepdims=True)
        dq = (dyb * ps_ref[...]).astype(BF16)
        dwp_ref[...] += _dot_tn(db, dq)
        dd = _dot(dq, wpt_ref[...])
        mm = dd / cnt
        m_ext = jnp.concatenate([mm, mfut_ref[...]], axis=0)
        f2 = m_ext + pltpu.roll(m_ext, n - 1, 0)
        f4 = f2 + pltpu.roll(f2, n - 2, 0)
        f8 = f4 + pltpu.roll(f4, n - 4, 0)
        f16 = f8 + pltpu.roll(f8, n - 8, 0)
        dp_ref[:, 1024:1280] = (_pool_select(f2, f4, f8, f16)[:tb] - dd).astype(BF16)
        mfut_ref[...] = mm[0:HALO]

        ug, dug = _gelu_and_grad(p_ref[:, 1280:1536])
        vg, dvg = _gelu_and_grad(p_ref[:, 1536:1792])
        rv = lax.rsqrt(jnp.mean(vg * vg, axis=-1, keepdims=True) + EPS)
        vhat = vg * rv
        vn = (vhat * sgn_ref[...]).astype(BF16)
        dyc = dy_ref[:, 768:1024]
        masks = _head_masks()
        dz = dyc * ug
        dzb = dz.astype(BF16)
        dvn_parts = []
        for ci in range(tb // CHUNK):
            sl = slice(ci * CHUNK, (ci + 1) * CHUNK)
            z = _sgu_mix(ws_ref, vn[sl], masks) + bz_ref[...]
            dp_ref[sl, 1280:1536] = (dyc[sl] * z * dug[sl]).astype(BF16)
            dbz_ref[...] += dz[sl]
            for h in range(4):
                dws_ref[h] += _dot_nt((dz[sl] * masks[h]).astype(BF16), vn[sl])
            dvn_parts.append(_sgu_mix(wst_ref, dzb[sl], masks))
        dvn = jnp.concatenate(dvn_parts, axis=0)
        dsgn_ref[...] += jnp.sum(dvn * vhat, axis=0, keepdims=True)
        dvhat = dvn * sgn_ref[...]
        dvg_in = rv * (dvhat - vhat * jnp.mean(dvhat * vhat, axis=-1, keepdims=True))
        dp_ref[:, 1536:1792] = (dvg_in * dvg).astype(BF16)

        @pl.when(i == nb - 1)
        def _():
            tril = (lax.broadcasted_iota(jnp.int32, (CHUNK, CHUNK), 0)
                    >= lax.broadcasted_iota(jnp.int32, (CHUNK, CHUNK), 1)).astype(F32)
            for h in range(4):
                dws_ref[h] = dws_ref[h] * tril

    def blk(i):
        return nb - 1 - i

    def halo(i):
        return jnp.maximum(blk(i) * hb - 1, 0)

    def full(shape):
        return pl.BlockSpec(shape, lambda i: (0,) * len(shape))

    small_shapes = [(4, D_RNN), (1, D_RNN), (D_RNN, 2 * D_RNN), (1, 2 * D_RNN), (1, D_RNN), (D_POOL, D_POOL),
                    (1, D_POOL), (1, D_SGU), (4, CHUNK, CHUNK), (CHUNK, D_SGU)]
    outs = pl.pallas_call(
        body, name="mix_bwd", grid=(nb,),
        in_specs=[pl.BlockSpec((tb, D), lambda i: (blk(i), 0)),
                  pl.BlockSpec((tb, D_IN), lambda i: (blk(i), 0)),
                  pl.BlockSpec((HALO, D_RNN), lambda i: (halo(i), 1)),
                  pl.BlockSpec((HALO, D_POOL), lambda i: (halo(i), 4)),
                  pl.BlockSpec((tb, D_RNN), lambda i: (blk(i), 0)),
                  pl.BlockSpec((HALO, D_RNN), lambda i: (halo(i), 0)),
                  full((4, D_RNN)), full((1, D_RNN)), full((D_RNN, 2 * D_RNN)), full((2 * D_RNN, D_RNN)),
                  full((1, 2 * D_RNN)), full((1, D_RNN)), full((D_POOL, D_POOL)), full((D_POOL, D_POOL)),
                  full((1, D_POOL)), full((1, D_SGU)), full((4, CHUNK, CHUNK)), full((4, CHUNK, CHUNK)),
                  full((CHUNK, D_SGU))],
        out_specs=[pl.BlockSpec((tb, D_IN), lambda i: (blk(i), 0))] + [full(s) for s in small_shapes],
        out_shape=[jax.ShapeDtypeStruct((T, D_IN), BF16)] + [jax.ShapeDtypeStruct(s, F32) for s in small_shapes],
        scratch_shapes=[pltpu.VMEM((8, D_RNN), F32), pltpu.VMEM((HALO, D_RNN), F32),
                        pltpu.VMEM((HALO, D_POOL), F32)],
        compiler_params=_params("arbitrary"),
    )(dy, p, p, p, hs, hs, prm["conv_w"], prm["conv_b"], prm["wax"], prm["wax_t"], prm["bax"], prm["lam"],
      prm["wpool"], prm["wpool_t"], prm["pool_scale"], prm["sgu_norm"], prm["ws"], prm["ws_t"], prm["bz"])
    names = ("dp", "conv_w", "conv_b", "wax", "bax", "lam", "wpool", "pool_scale", "sgu_norm", "ws", "bz")
    return dict(zip(names, outs))


ANY = pl.BlockSpec(memory_space=pl.ANY)


def _place():
    x, y, c = lax.axis_index("x"), lax.axis_index("y"), lax.axis_index("c")
    return x, y, c


def _all_to_all(xs, name):
    def body(in_ref, out_ref, send_sems, recv_sems, local_sem):
        x, y, c = _place()
        me = 4 * x + 2 * y + c
        mine = pltpu.make_async_copy(in_ref.at[me], out_ref.at[me], local_sem)
        mine.start()
        copies = []
        for rel in range(1, N_DEV):
            tx = 1 - x if rel & 4 else x
            ty = 1 - y if rel & 2 else y
            tc = 1 - c if rel & 1 else c
            cp = pltpu.make_async_remote_copy(
                src_ref=in_ref.at[4 * tx + 2 * ty + tc], dst_ref=out_ref.at[me],
                send_sem=send_sems.at[rel - 1], recv_sem=recv_sems.at[rel - 1],
                device_id=(tx, ty, tc), device_id_type=MESH)
            cp.start()
            copies.append(cp)
        for cp in copies:
            cp.wait()
        mine.wait()

    return pl.pallas_call(
        body, name=name, in_specs=[ANY], out_specs=ANY,
        out_shape=jax.ShapeDtypeStruct(xs.shape, xs.dtype),
        scratch_shapes=[pltpu.SemaphoreType.DMA((N_DEV - 1,)), pltpu.SemaphoreType.DMA((N_DEV - 1,)),
                        pltpu.SemaphoreType.DMA],
    )(xs)


def _all_gather8(xs, name):
    def body(x_ref, out_ref, send_sems, recv_sems, local_sem):
        x, y, c = _place()
        me, sibling = (x, y, c), (x, y, 1 - c)
        chips = [(1 - x, y), (x, 1 - y), (1 - x, 1 - y)]

        def rows(px, py, pc):
            return out_ref.at[4 * px + 2 * py + pc]

        def copy(k, block, to, src=None):
            return pltpu.make_async_remote_copy(
                src_ref=rows(*block) if src is None else src, dst_ref=rows(*block),
                send_sem=send_sems.at[k], recv_sem=recv_sems.at[k], device_id=to, device_id_type=MESH)

        mine = pltpu.make_async_copy(x_ref, rows(*me), local_sem)
        mine.start()
        first = [copy(0, me, sibling, src=x_ref)]
        first += [copy(1 + j, me, (*chip, c), src=x_ref) for j, chip in enumerate(chips)]
        for cp in first:
            cp.start()
        passed = [copy(4 + j, (*chip, c), sibling) for j, chip in enumerate(chips)]
        for j, chip in enumerate(chips):
            copy(1 + j, (*chip, c), me).wait_recv()
            passed[j].start()
        copy(0, sibling, me).wait_recv()
        for j, chip in enumerate(chips):
            copy(4 + j, (*chip, 1 - c), me).wait_recv()
        for cp in first + passed:
            cp.wait_send()
        mine.wait()

    return pl.pallas_call(
        body, name=name, in_specs=[ANY], out_specs=ANY,
        out_shape=jax.ShapeDtypeStruct((N_DEV,) + xs.shape, xs.dtype),
        scratch_shapes=[pltpu.SemaphoreType.DMA((7,)), pltpu.SemaphoreType.DMA((7,)), pltpu.SemaphoreType.DMA],
    )(xs)


HBM = pl.BlockSpec(memory_space=pltpu.HBM)
SEM = pl.BlockSpec(memory_space=pltpu.SEMAPHORE)
EFFECT = pltpu.SideEffectType.DATAFLOW_SIDE_EFFECTING


def _in_hbm(a):
    return pltpu.with_memory_space_constraint(a, pltpu.HBM)


def _local_copy(src, dst, stage, sem):
    load = pltpu.make_async_copy(src, stage, sem)
    load.start()
    load.wait()
    store = pltpu.make_async_copy(stage, dst, sem)
    store.start()
    store.wait()


def _unique(windows):
    arrays = []
    for per_chip in windows:
        for arr, _ in per_chip:
            if not any(arr is a for a in arrays):
                arrays.append(arr)
    return arrays


def _exchange_start(layer, windows, lands, name):
    arrays = _unique(windows)
    na, nt = len(arrays), len(windows)

    def body(*refs):
        in_refs, land_refs = refs[:na], refs[na:na + nt]
        send_sems, recv_sems = refs[na + nt], refs[na + nt + 1]
        token = refs[-1]
        x, y, c = _place()
        me = 4 * x + 2 * y + c
        for t in range(nt):
            for j in range(N_CHIP):
                arr, window = windows[t][j]
                src = window(in_refs[next(i for i, a in enumerate(arrays) if a is arr)])

                @pl.when(me != 2 * j + layer)
                def _():
                    pltpu.make_async_remote_copy(
                        src_ref=src, dst_ref=land_refs[t].at[me], send_sem=send_sems.at[N_CHIP * t + j],
                        recv_sem=recv_sems.at[N_DEV * t + me], device_id=(j // 2, j % 2, layer),
                        device_id_type=MESH).start()
        token[...] = jnp.zeros_like(token)

    outs = pl.pallas_call(
        body, name=name,
        out_shape=(pltpu.SemaphoreType.DMA((N_CHIP * nt,)), pltpu.SemaphoreType.DMA((N_DEV * nt,)),
                   *[pltpu.HBM(a.shape, a.dtype) for a in lands], jax.ShapeDtypeStruct((8, 128), F32)),
        in_specs=[HBM] * (na + nt),
        out_specs=(SEM, SEM, *[HBM] * nt, pl.BlockSpec(memory_space=pltpu.VMEM)),
        input_output_aliases={na + t: 2 + t for t in range(nt)},
        compiler_params=pltpu.CompilerParams(has_side_effects=EFFECT, vmem_limit_bytes=VMEM_LIMIT),
    )(*[_in_hbm(a) for a in arrays], *[_in_hbm(a) for a in lands])
    return outs[0], outs[1], list(outs[2:2 + nt]), outs[-1]


def _exchange_wait(layer, windows, lands, send_sems, recv_sems, after, name):
    arrays = _unique(windows)
    na, nt = len(arrays), len(windows)

    def body(*refs):
        in_refs, land_refs = refs[:na], refs[na:na + nt]
        send_sems, recv_sems = refs[na + nt], refs[na + nt + 1]
        stages, local_sem = refs[-1 - nt:-1], refs[-1]
        x, y, c = _place()
        me = 4 * x + 2 * y + c

        def source(t, j):
            arr, window = windows[t][j]
            return window(in_refs[next(i for i, a in enumerate(arrays) if a is arr)])

        @pl.when(c == layer)
        def _():
            for t in range(nt):
                for j in range(N_CHIP):
                    @pl.when(me == 2 * j + layer)
                    def _():
                        _local_copy(source(t, j), land_refs[t].at[me], stages[t], local_sem)

        for t in range(nt):
            for j in range(N_CHIP):
                @pl.when(me != 2 * j + layer)
                def _():
                    pltpu.make_async_remote_copy(
                        src_ref=source(t, j), dst_ref=land_refs[t].at[me], send_sem=send_sems.at[N_CHIP * t + j],
                        recv_sem=recv_sems.at[N_DEV * t + me], device_id=(j // 2, j % 2, layer),
                        device_id_type=MESH).wait_send()

        @pl.when(c == layer)
        def _():
            for t in range(nt):
                for s in range(N_DEV):
                    @pl.when(me != s)
                    def _():
                        slot = land_refs[t].at[s]
                        pltpu.make_async_remote_copy(
                            src_ref=slot, dst_ref=slot, send_sem=send_sems.at[N_CHIP * t],
                            recv_sem=recv_sems.at[N_DEV * t + s], device_id=(x, y, c),
                            device_id_type=MESH).wait_recv()

    outs = pl.pallas_call(
        body, name=name,
        out_shape=tuple(pltpu.HBM(a.shape, a.dtype) for a in lands),
        in_specs=[HBM] * (na + nt) + [SEM, SEM, ANY],
        out_specs=tuple([HBM] * nt),
        input_output_aliases={na + t: t for t in range(nt)},
        scratch_shapes=[pltpu.VMEM(a.shape[1:], a.dtype) for a in lands] + [pltpu.SemaphoreType.DMA],
        compiler_params=pltpu.CompilerParams(has_side_effects=EFFECT, vmem_limit_bytes=VMEM_LIMIT),
    )(*[_in_hbm(a) for a in arrays], *lands, send_sems, recv_sems, after)
    return list(outs)


def _other_chips(x, y):
    return [(1 - x, y), (x, 1 - y), (1 - x, 1 - y)]


def _split_rows(rows):
    if rows < 32:
        return [(0, rows), (rows, 0)]
    cut = -(-(rows // 2) // 16) * 16
    return [(0, cut), (cut, rows - cut)]


def _gather_copies(in_refs, land_refs, send_sems, recv_sems, x, y, sender_core):
    copies = []
    for t, src in enumerate(in_refs):
        r0, n = _split_rows(src.shape[0])[sender_core]
        if n == 0:
            continue
        for rel, (tx, ty) in enumerate(_other_chips(x, y)):
            for tc in range(2):
                copies.append(pltpu.make_async_remote_copy(
                    src_ref=src.at[pl.ds(r0, n)], dst_ref=land_refs[t].at[2 * x + y].at[pl.ds(r0, n)],
                    send_sem=send_sems.at[6 * t + 2 * rel + tc],
                    recv_sem=recv_sems.at[2 * (3 * t + rel) + sender_core],
                    device_id=(tx, ty, tc), device_id_type=MESH))
    return copies


def _gather_start(shards, lands, name):
    nt = len(shards)

    def body(*refs):
        in_refs, land_refs = refs[:nt], refs[nt:2 * nt]
        send_sems, recv_sems = refs[2 * nt], refs[2 * nt + 1]
        token = refs[-1]
        x, y, c = _place()
        for core in range(2):
            @pl.when(c == core)
            def _():
                for cp in _gather_copies(in_refs, land_refs, send_sems, recv_sems, x, y, core):
                    cp.start()
        token[...] = jnp.zeros_like(token)

    outs = pl.pallas_call(
        body, name=name,
        out_shape=(pltpu.SemaphoreType.DMA((6 * nt,)), pltpu.SemaphoreType.DMA((6 * nt,)),
                   *[pltpu.HBM(a.shape, a.dtype) for a in lands], jax.ShapeDtypeStruct((8, 128), F32)),
        in_specs=[HBM] * (2 * nt),
        out_specs=(SEM, SEM, *[HBM] * nt, pl.BlockSpec(memory_space=pltpu.VMEM)),
        input_output_aliases={nt + t: 2 + t for t in range(nt)},
        compiler_params=pltpu.CompilerParams(has_side_effects=EFFECT, vmem_limit_bytes=VMEM_LIMIT),
    )(*[_in_hbm(a) for a in shards], *[_in_hbm(a) for a in lands])
    return outs[0], outs[1], list(outs[2:2 + nt]), outs[-1]


def _gather_wait(shards, lands, send_sems, recv_sems, after, name):
    nt = len(shards)

    def body(*refs):
        in_refs, land_refs = refs[:nt], refs[nt:2 * nt]
        send_sems, recv_sems = refs[2 * nt], refs[2 * nt + 1]
        stages, local_sem = refs[-1 - nt:-1], refs[-1]
        x, y, c = _place()

        for t in range(nt):
            _local_copy(in_refs[t], land_refs[t].at[2 * x + y], stages[t], local_sem)

        for core in range(2):
            @pl.when(c == core)
            def _():
                for cp in _gather_copies(in_refs, land_refs, send_sems, recv_sems, x, y, core):
                    cp.wait_send()

        for t in range(nt):
            for rel, (tx, ty) in enumerate(_other_chips(x, y)):
                for core, (r0, n) in enumerate(_split_rows(in_refs[t].shape[0])):
                    if n > 0:
                        part = land_refs[t].at[2 * tx + ty].at[pl.ds(r0, n)]
                        pltpu.make_async_remote_copy(
                            src_ref=part, dst_ref=part, send_sem=send_sems.at[6 * t],
                            recv_sem=recv_sems.at[2 * (3 * t + rel) + core], device_id=(x, y, c),
                            device_id_type=MESH).wait_recv()

    outs = pl.pallas_call(
        body, name=name,
        out_shape=tuple(pltpu.HBM(a.shape, a.dtype) for a in lands),
        in_specs=[HBM] * (2 * nt) + [SEM, SEM, ANY],
        out_specs=tuple([HBM] * nt),
        input_output_aliases={nt + t: t for t in range(nt)},
        scratch_shapes=[pltpu.VMEM(a.shape, a.dtype) for a in shards] + [pltpu.SemaphoreType.DMA],
        compiler_params=pltpu.CompilerParams(has_side_effects=EFFECT, vmem_limit_bytes=VMEM_LIMIT),
    )(*[_in_hbm(a) for a in shards], *lands, send_sems, recv_sems, after)
    return list(outs)


def _tie(a, token):
    def body(a_ref, token_ref, o_ref):
        pass

    return pl.pallas_call(
        body, name="tie", in_specs=[ANY, ANY], out_specs=ANY,
        out_shape=jax.ShapeDtypeStruct(a.shape, a.dtype), input_output_aliases={0: 0},
    )(a, token)


def _sum_share(xs, name):
    _, r, cols = xs.shape
    tr = max(t for t in range(16, min(r, 704) + 1, 16) if r % t == 0)
    nblk = r // tr

    def body(x_ref, out_ref, acc_ref, send_sems, local_sems, recv_sem):
        i = pl.program_id(0)
        slot = i % 2
        x, y, c = _place()

        def copies(s, blk):
            dst = out_ref.at[c, pl.ds(blk * tr, tr), :]
            loc = pltpu.make_async_copy(acc_ref.at[s], dst, local_sems.at[s])
            rem = pltpu.make_async_remote_copy(src_ref=acc_ref.at[s], dst_ref=dst, send_sem=send_sems.at[s],
                                               recv_sem=recv_sem, device_id=(x, y, 1 - c), device_id_type=MESH)
            return loc, rem

        @pl.when(i >= 2)
        def _():
            loc, rem = copies(slot, i - 2)
            loc.wait()
            rem.wait_send()

        acc = x_ref[0].astype(F32)
        for k in range(1, N_DEV):
            acc = acc + x_ref[k].astype(F32)
        acc_ref[slot] = acc
        loc, rem = copies(slot, i)
        loc.start()
        rem.start()

        @pl.when(i == nblk - 1)
        def _():
            for back in range(min(2, nblk)):
                blk = nblk - 1 - back
                loc, rem = copies(blk % 2, blk)
                loc.wait()
                rem.wait_send()
            theirs = out_ref.at[1 - c]
            pltpu.make_async_remote_copy(src_ref=theirs, dst_ref=theirs, send_sem=send_sems.at[0],
                                         recv_sem=recv_sem, device_id=(x, y, 1 - c),
                                         device_id_type=MESH).wait_recv()

    return pl.pallas_call(
        body, name=name, grid=(nblk,),
        in_specs=[pl.BlockSpec((N_DEV, tr, cols), lambda i: (0, i, 0))],
        out_specs=ANY,
        out_shape=jax.ShapeDtypeStruct((2, r, cols), F32),
        scratch_shapes=[pltpu.VMEM((2, tr, cols), F32), pltpu.SemaphoreType.DMA((2,)),
                        pltpu.SemaphoreType.DMA((2,)), pltpu.SemaphoreType.DMA],
        compiler_params=_params("arbitrary"),
    )(xs)


def _sum8(xs, name):
    _, r, cols = xs.shape
    tr = 8
    for cand in (256, 128, 64, 32, 16):
        if r % cand == 0:
            tr = cand
            break

    def body(x_ref, o_ref):
        acc = x_ref[0].astype(F32)
        for k in range(1, N_DEV):
            acc = acc + x_ref[k].astype(F32)
        o_ref[...] = acc

    return pl.pallas_call(
        body, name=name, grid=(r // tr,),
        in_specs=[pl.BlockSpec((N_DEV, tr, cols), lambda i: (0, i, 0))],
        out_specs=pl.BlockSpec((tr, cols), lambda i: (i, 0)),
        out_shape=jax.ShapeDtypeStruct((r, cols), F32),
        compiler_params=_params("parallel"),
    )(xs)


def _adamw(w, g, m, v, name):
    R, C = w.shape
    tr = max([t for t in range(8, min(R, 512) + 1, 8) if R % t == 0] or [R])
    c1 = 1.0 / (1.0 - ADAM_B1 ** ADAM_STEP)
    c2 = 1.0 / (1.0 - ADAM_B2 ** ADAM_STEP)

    def body(w_ref, g_ref, m_ref, v_ref, d_ref, nm_ref, nv_ref):
        gv = g_ref[...]
        nm = ADAM_B1 * m_ref[...] + (1.0 - ADAM_B1) * gv
        nv = ADAM_B2 * v_ref[...] + (1.0 - ADAM_B2) * (gv * gv)
        d_ref[...] = (-ADAM_LR) * ((nm * c1) / (jnp.sqrt(nv * c2) + ADAM_EPS) + ADAM_WD * w_ref[...])
        nm_ref[...] = nm
        nv_ref[...] = nv

    spec = pl.BlockSpec((tr, C), lambda i: (i, 0))
    shape = jax.ShapeDtypeStruct((R, C), F32)
    return pl.pallas_call(
        body, name=name, grid=(R // tr,), in_specs=[spec] * 4, out_specs=[spec] * 3, out_shape=[shape] * 3,
        compiler_params=_params("parallel"),
    )(w, g, m, v)


def _flat_rows(parts, rows):
    flat = jnp.concatenate([q.reshape(-1) for q in parts])
    flat = jnp.pad(flat, (0, rows * LANES - flat.shape[0]))
    return flat.reshape(rows, LANES)


def _round_up(n, m):
    return (n + m - 1) // m * m


def _block_diag(w):
    H, n, _ = w.shape
    eye = jnp.eye(H, dtype=w.dtype)
    return (eye[:, None, :, None] * w[:, :, None, :]).reshape(H * n, H * n)


def _diag_blocks(w, H, n):
    w4 = w.reshape(H, n, H, n)
    return jnp.stack([w4[h, :, h, :] for h in range(H)])


W_IN_T = ("ffn1_w_in", "w_in", "ffn2_w_in")


def _ffn_in_weights(g_in):
    zeros = jnp.zeros((FF_HALF - FF_SHARD, D), g_in.dtype)
    wg_t = jnp.concatenate([g_in[0], zeros, g_in[1], zeros], axis=0)
    wu_t = jnp.concatenate([g_in[2], zeros, g_in[3], zeros], axis=0)
    return wg_t, wu_t


def _ffn_out_weights(g_out):
    zeros = jnp.zeros((FF_HALF - FF_SHARD, D), g_out.dtype)
    return jnp.concatenate([g_out[0], g_out[1], zeros, g_out[2], g_out[3], zeros], axis=0)


LAND_SHAPES = {"ffn1_w_in": (FF_SHARD, D), "ffn1_w_out": (FF_ROWS, D), "w_in": (D_IN // N_CHIP, D),
               "w_out": (D // N_CHIP, D), "ffn2_w_in": (FF_SHARD, D), "ffn2_w_out": (FF_ROWS, D)}


def _rows_window(arr, start, size):
    return arr, lambda r: r.at[pl.ds(start, size), :]


def _ffn_grad_windows(dwg_t, dwu_t, dwout):
    w_in = [_rows_window(dwg_t if j < 2 else dwu_t, (j % 2) * FF_HALF, FF_SHARD) for j in range(N_CHIP)]
    w_out = [_rows_window(dwout, (j // 2) * FF_HALF + (j % 2) * FF_ROWS, FF_ROWS) for j in range(N_CHIP)]
    return w_in, w_out


def _mix_grad_windows(dwin_t, dwo):
    win = [_rows_window(dwin_t, j * (D_IN // N_CHIP), D_IN // N_CHIP) for j in range(N_CHIP)]
    wo = [_rows_window(dwo, j * (D // N_CHIP), D // N_CHIP) for j in range(N_CHIP)]
    return win, wo


def kernel(x, ffn1_norm, ffn1_w_in, ffn1_w_out, mix_norm, w_in, conv_w, conv_b, rg_w_a, rg_b_a, rg_w_x, rg_b_x, lru_lambda, pool_w, pool_scale, sgu_norm, sgu_w, sgu_b, w_out, ffn2_norm, ffn2_w_in, ffn2_w_out, final_norm, loss_target, m_ffn1_norm, m_ffn1_w_in, m_ffn1_w_out, m_mix_norm, m_w_in, m_conv_w, m_conv_b, m_rg_w_a, m_rg_b_a, m_rg_w_x, m_rg_b_x, m_lru_lambda, m_pool_w, m_pool_scale, m_sgu_norm, m_sgu_w, m_sgu_b, m_w_out, m_ffn2_norm, m_ffn2_w_in, m_ffn2_w_out, m_final_norm, v_ffn1_norm, v_ffn1_w_in, v_ffn1_w_out, v_mix_norm, v_w_in, v_conv_w, v_conv_b, v_rg_w_a, v_rg_b_a, v_rg_w_x, v_rg_b_x, v_lru_lambda, v_pool_w, v_pool_scale, v_sgu_norm, v_sgu_w, v_sgu_b, v_w_out, v_ffn2_norm, v_ffn2_w_in, v_ffn2_w_out, v_final_norm):
    args = locals()
    W = {n: args[n] for n in WEIGHTS}
    M = {n: args["m_" + n] for n in WEIGHTS}
    V = {n: args["v_" + n] for n in WEIGHTS}
    depth = ffn1_norm.shape[0]
    T = x.shape[1]
    xi, yi, ci = _place()
    chip = 2 * xi + yi

    assert depth == 2, "core c of a chip sends and reduces layer c"
    groups = [(l, names) for l in range(depth)
              for names in (["ffn1_w_in"], ["ffn1_w_out", "w_in", "w_out"], ["ffn2_w_in", "ffn2_w_out"])]
    def stored(a, n):
        return jnp.swapaxes(a, 1, 2) if n in W_IN_T else a

    wb = {n: stored(W[n], n).astype(BF16) for n in BIG}
    groups[1][1].append("conv_w")
    conv_shard = conv_w.reshape(-1, conv_w.shape[-1])
    flights = {}

    def weights_start(k, dep=None):
        l, names = groups[k]
        shards = [conv_shard if n == "conv_w" else wb[n][l] for n in names]
        if dep is not None:
            shards[0] = _tie(shards[0], dep)
        lands = [lax.empty((N_CHIP,) + s.shape, s.dtype) for s in shards]
        send, recv, lands, token = _gather_start(shards, lands, "weights_start_%d" % k)
        flights[k] = (shards, lands, send, recv)
        return token

    def weights_wait(k, after):
        l, names = groups[k]
        shards, lands, send, recv = flights[k]
        got = _gather_wait(shards, lands, send, recv, after, "weights_wait_%d" % k)
        token = weights_start(k + 2, got[0]) if k + 2 < len(groups) else None
        return dict(zip(names, got)), token

    def after_start(a, token):
        return a if token is None else _tie(a, token)

    first_tokens = [weights_start(0), weights_start(1)]

    layers = []
    for l in range(depth):
        L = {f: dict(norm=W[f + "_norm"][l][None]) for f in ("ffn1", "ffn2")}
        ws = jnp.where(jnp.tril(jnp.ones((CHUNK, CHUNK), bool))[None], sgu_w[l], 0.0)
        wax = jnp.concatenate([_block_diag(rg_w_a[l]), _block_diag(rg_w_x[l])], axis=1)
        wpool = _block_diag(pool_w[l])
        L["mix"] = dict(
            conv_b=conv_b[l][None], wax=wax.astype(BF16), wax_t=wax.T.astype(BF16),
            bax=jnp.concatenate([rg_b_a[l].reshape(-1), rg_b_x[l].reshape(-1)])[None], lam=lru_lambda[l][None],
            wpool=wpool.astype(BF16), wpool_t=wpool.T.astype(BF16), pool_scale=pool_scale[l][None],
            sgu_norm=sgu_norm[l][None], ws=ws.astype(BF16), ws_t=jnp.swapaxes(ws, 1, 2).astype(BF16),
            bz=jnp.repeat(sgu_b[l].T, 64, axis=1))
        L["mix_norm"] = mix_norm[l][None]
        layers.append(L)
    for token in first_tokens:
        layers[0]["ffn1"]["norm"] = _tie(layers[0]["ffn1"]["norm"], token)

    xs = x[0]
    saved = []
    for l, L in enumerate(layers):
        F1, F2 = L["ffn1"], L["ffn2"]
        got, token = weights_wait(3 * l, xs)
        F1["wg"], F1["wu"] = _ffn_in_weights(got["ffn1_w_in"])
        F1["norm"] = after_start(F1["norm"], token)
        h, g, u, a = _ffn_in(xs, F1["norm"], F1["wg"], F1["wu"])
        got, token = weights_wait(3 * l + 1, a)
        F1["wout"] = _ffn_out_weights(got["ffn1_w_out"])
        L["w_in"] = got["w_in"].reshape(D_IN, D)
        L["w_out"] = jnp.concatenate([got["w_out"][j] for j in range(N_CHIP)], axis=0)
        if "conv_w" in got:
            conv_full = jnp.concatenate([got["conv_w"][j] for j in range(N_CHIP)], axis=1)
            for ll in range(depth):
                layers[ll]["mix"]["conv_w"] = conv_full.reshape(depth, 4, D_RNN)[ll]
        a = after_start(a, token)
        x1 = _mm_res(a, F1["wout"], xs, 0.5, "ffn_out", tm=512, tn=D)
        s1 = (xs, h, g, u, a)
        hm, p = _mix_in(x1, L["mix_norm"], L["w_in"])
        ycat, hs = _mix_fwd(p, L["mix"])
        x2 = _mm_res(ycat, L["w_out"], x1, 1.0, "mix_out", tm=512, tn=D)
        got, token = weights_wait(3 * l + 2, x2)
        F2["wg"], F2["wu"] = _ffn_in_weights(got["ffn2_w_in"])
        F2["wout"] = _ffn_out_weights(got["ffn2_w_out"])
        F2["norm"] = after_start(F2["norm"], token)
        h2, g2, u2, a2 = _ffn_in(x2, F2["norm"], F2["wg"], F2["wu"])
        x3 = _mm_res(a2, F2["wout"], x2, 0.5, "ffn_out", tm=512, tn=D)
        saved.append((s1, (x1, hm, p, ycat, hs), (x2, h2, g2, u2, a2)))
        xs = x3

    dx, dxb, d_final, loss_part = _final(xs, loss_target[0], final_norm[None], 0.5)

    G = {n: [None] * depth for n in SMALL if n != "final_norm"}
    lands = {n: lax.empty((N_DEV,) + LAND_SHAPES[n], BF16) for n in BIG}
    in_flight = []

    def send_grads(l, windows, tag):
        names = list(windows)
        send, recv, thru, token = _exchange_start(l, [windows[n] for n in names], [lands[n] for n in names],
                                                  "grads_start_" + tag)
        lands.update(zip(names, thru))
        in_flight.append((l, names, [windows[n] for n in names], send, recv, "grads_wait_" + tag))
        return token

    def ffn_bwd(dx, dxb, F, s, f, l, pending, send_now):
        xin, h, g, u, a = s
        dg, du = _ffn_mid_bwd(dxb, F["wout"], g, u)
        dwout = _mm_tn(a, dxb, 1.0, "ffn_dwout")
        dwg = _mm_tn(dg, h, 1.0, "ffn_dwg")
        dwu = _mm_tn(du, h, 1.0, "ffn_dwu")
        pending[f + "_w_in"], pending[f + "_w_out"] = _ffn_grad_windows(dwg, dwu, dwout)
        if send_now:
            dg = _tie(dg, send_grads(l, pending, "l%d_%s" % (l, f)))
        dx, dxb, dn = _dh_rms_bwd([(dg, F["wg"]), (du, F["wu"])], xin, F["norm"], dx,
                                  1.0 if f == "ffn2" else 0.5, "ffn_dh")
        G[f + "_norm"][l] = dn[0]
        return dx, dxb

    for l in reversed(range(depth)):
        L = layers[l]
        s1, (x1, hm, p, ycat, hs), s2 = saved[l]
        pending = {}
        dx, dxb = ffn_bwd(dx, dxb, L["ffn2"], s2, "ffn2", l, pending, l == 0)
        if l == 0:
            pending = {}
        dycat = _mm_nt(dxb, L["w_out"], "mix_dy", tm=512, tn=D)
        dwo = _mm_tn(ycat, dxb, 1.0, "mix_dwout")
        mg = _mix_bwd(dycat, p, hs, L["mix"])
        dwin = _mm_tn(mg["dp"], hm, 1.0, "mix_dwin")
        pending["w_in"], pending["w_out"] = _mix_grad_windows(dwin, dwo)
        if l == 0:
            dp = _tie(mg["dp"], send_grads(l, pending, "l0_mix"))
            pending = {}
        else:
            dp = mg["dp"]
        dx, dxb, dn = _dh_rms_bwd([(dp, L["w_in"])], x1, L["mix_norm"], dx, 0.5, "mix_dh")
        G["mix_norm"][l] = dn[0]
        G["conv_w"][l], G["conv_b"][l] = mg["conv_w"], mg["conv_b"][0]
        G["rg_w_a"][l] = _diag_blocks(mg["wax"][:, :D_RNN], 8, 64)
        G["rg_w_x"][l] = _diag_blocks(mg["wax"][:, D_RNN:], 8, 64)
        G["rg_b_a"][l] = mg["bax"][0, :D_RNN].reshape(8, 64)
        G["rg_b_x"][l] = mg["bax"][0, D_RNN:].reshape(8, 64)
        G["lru_lambda"][l] = mg["lam"][0]
        G["pool_w"][l] = _diag_blocks(mg["wpool"], 4, 64)
        G["pool_scale"][l], G["sgu_norm"][l] = mg["pool_scale"][0], mg["sgu_norm"][0]
        G["sgu_w"][l] = mg["ws"]
        G["sgu_b"][l] = mg["bz"].reshape(CHUNK, 4, 64).sum(-1).T
        dx, dxb = ffn_bwd(dx, dxb, L["ffn1"], s1, "ffn1", l, pending, l == 0)
        if l > 0:
            dxb = _tie(dxb, send_grads(l, pending, "l%d" % l))
    grad_x = dx[None]
    G = {n: jnp.stack(v) for n, v in G.items()}
    G["final_norm"] = d_final[0]

    for l, names, windows, send, recv, tag in in_flight:
        lands.update(zip(names, _exchange_wait(l, windows, [lands[n] for n in names], send, recv, dx, tag)))
    both = [_sum_share(lands[n], "sum_share_" + n) for n in BIG]
    grads = dict(zip(BIG, both))

    small_sizes = [int(np.prod(G[n].shape)) for n in SMALL]
    srows = _round_up(sum(small_sizes) + 1, N_DEV * 8 * LANES) // (N_DEV * LANES)
    sflat = _flat_rows([G[n] for n in SMALL] + [loss_part[0, :1]], N_DEV * srows)
    sgot = _all_to_all(sflat.reshape(N_DEV, srows, LANES), "exchange_small_grads")
    sall = _all_gather8(_sum8(sgot, "sum_small_grads"), "share_small_grads").reshape(-1)
    off = 0
    for n, size in zip(SMALL, small_sizes):
        grads[n] = sall[off:off + size].reshape(G[n].shape)
        off += size
    loss = sall[off]
    grads["conv_w"] = lax.dynamic_slice_in_dim(grads["conv_w"], chip * conv_w.shape[2], conv_w.shape[2], axis=2)

    delta, new_m, new_v = {}, {}, {}
    for n in BIG:
        shp = grads[n].shape
        two_d = (shp[0] * shp[1], shp[2])
        outs = _adamw(stored(W[n], n).reshape(two_d), grads[n].reshape(two_d), stored(M[n], n).reshape(two_d),
                      stored(V[n], n).reshape(two_d), "adamw_" + n)
        delta[n], new_m[n], new_v[n] = (stored(o.reshape(shp), n) for o in outs)
        grads[n] = stored(grads[n], n)
    arows = _round_up(sum(int(np.prod(W[n].shape)) for n in SMALL), 8 * LANES) // LANES
    outs = _adamw(*(_flat_rows([src[n] for n in SMALL], arows) for src in (W, grads, M, V)), "adamw_small")
    outs = [o.reshape(-1) for o in outs]
    off = 0
    for n in SMALL:
        size = int(np.prod(W[n].shape))
        delta[n], new_m[n], new_v[n] = (o[off:off + size].reshape(W[n].shape) for o in outs)
        off += size

    return (loss, grad_x, *[grads[n] for n in WEIGHTS], *[delta[n] for n in WEIGHTS],
            *[new_m[n] for n in WEIGHTS], *[new_v[n] for n in WEIGHTS])
```

```python
import functools
import math

import jax
import jax.numpy as jnp
import numpy as np
from jax import lax
from jax.experimental import pallas as pl
from jax.experimental.pallas import tpu as pltpu

F32 = jnp.float32
BF16 = jnp.bfloat16
MESH = pl.DeviceIdType.MESH

D = 1024
D_RNN = 512
D_POOL = 256
D_SGU = 256
D_IN = 1792
D_FF = 2752
D_FFP = 2816
N_CHIP = 4
FF_SHARD = D_FF // 2
FF_HALF = D_FFP // 2
FF_ROWS = D_FF // N_CHIP
CHUNK = 128
HALO = 16
EPS = 1e-6
LRU_C = 8.0
N_DEV = 8
LANES = 1024
VMEM_LIMIT = 56 * 1024 * 1024

ADAM_LR, ADAM_B1, ADAM_B2, ADAM_EPS, ADAM_WD, ADAM_STEP = 0.001, 0.9, 0.999, 1e-08, 0.01, 10

BIG = ("ffn1_w_in", "ffn1_w_out", "w_in", "w_out", "ffn2_w_in", "ffn2_w_out")
SMALL = ("ffn1_norm", "mix_norm", "conv_w", "conv_b", "rg_w_a", "rg_b_a", "rg_w_x", "rg_b_x", "lru_lambda",
         "pool_w", "pool_scale", "sgu_norm", "sgu_w", "sgu_b", "ffn2_norm", "final_norm")
WEIGHTS = ("ffn1_norm", "ffn1_w_in", "ffn1_w_out", "mix_norm", "w_in", "conv_w", "conv_b", "rg_w_a", "rg_b_a",
           "rg_w_x", "rg_b_x", "lru_lambda", "pool_w", "pool_scale", "sgu_norm", "sgu_w", "sgu_b", "w_out",
           "ffn2_norm", "ffn2_w_in", "ffn2_w_out", "final_norm")


def _params(*sem):
    return pltpu.CompilerParams(dimension_semantics=sem, vmem_limit_bytes=VMEM_LIMIT)


def _gelu(x):
    c = math.sqrt(2.0 / math.pi)
    t = jnp.tanh(c * (x + 0.044715 * (x * x * x)))
    return 0.5 * x * (1.0 + t)


def _gelu_and_grad(x):
    c = math.sqrt(2.0 / math.pi)
    x2 = x * x
    t = jnp.tanh(c * (x + 0.044715 * (x2 * x)))
    g = 0.5 * x * (1.0 + t)
    dg = 0.5 * (1.0 + t) + 0.5 * x * (1.0 - t * t) * (c * (1.0 + 3.0 * 0.044715 * x2))
    return g, dg


def _sigmoid(x):
    return 0.5 * jnp.tanh(0.5 * x) + 0.5


def _dot(a, b):
    return jnp.dot(a, b, preferred_element_type=F32)


def _dot_tn(a, b):
    return lax.dot_general(a, b, (((0,), (0,)), ((), ())), preferred_element_type=F32)


def _dot_nt(a, b):
    return lax.dot_general(a, b, (((1,), (1,)), ((), ())), preferred_element_type=F32)


def _tile(n, limit):
    best = 128
    for t in range(128, min(n, limit) + 1, 128):
        if n % t == 0:
            best = t
    assert n % best == 0, (n, limit)
    return best


def _mm_res(a, b, res, scale, name, tm=1024, tn=512):
    M, K = a.shape
    N = b.shape[1]
    tm, tn = min(tm, M), _tile(N, tn)

    def body(a_ref, b_ref, r_ref, o_ref):
        o_ref[...] = r_ref[...] + scale * _dot(a_ref[...], b_ref[...])

    return pl.pallas_call(
        body, name=name, grid=(M // tm, N // tn),
        in_specs=[pl.BlockSpec((tm, K), lambda i, j: (i, 0)), pl.BlockSpec((K, tn), lambda i, j: (0, j)),
                  pl.BlockSpec((tm, tn), lambda i, j: (i, j))],
        out_specs=pl.BlockSpec((tm, tn), lambda i, j: (i, j)),
        out_shape=jax.ShapeDtypeStruct((M, N), F32),
        compiler_params=_params("parallel", "parallel"),
    )(a, b, res)


def _mm_nt(a, b, name, tm=1024, tn=512):
    M, K = a.shape
    N = b.shape[0]
    tm, tn = min(tm, M), _tile(N, tn)

    def body(a_ref, b_ref, o_ref):
        o_ref[...] = _dot_nt(a_ref[...], b_ref[...])

    return pl.pallas_call(
        body, name=name, grid=(M // tm, N // tn),
        in_specs=[pl.BlockSpec((tm, K), lambda i, j: (i, 0)), pl.BlockSpec((tn, K), lambda i, j: (j, 0))],
        out_specs=pl.BlockSpec((tm, tn), lambda i, j: (i, j)),
        out_shape=jax.ShapeDtypeStruct((M, N), F32),
        compiler_params=_params("parallel", "parallel"),
    )(a, b)


def _dh_rms_bwd(pairs, x, g, dres, copy_scale, name, tm=512):
    T = x.shape[0]
    tm = min(tm, T)
    n = len(pairs)

    def body(*refs):
        ab = refs[:2 * n]
        x_ref, g_ref, dres_ref, dx_ref, dxb_ref, dg_ref = refs[2 * n:]
        dy = _dot(ab[0][...], ab[1][...])
        for k in range(1, n):
            dy = dy + _dot(ab[2 * k][...], ab[2 * k + 1][...])
        xv = x_ref[...]
        r = lax.rsqrt(jnp.mean(xv * xv, axis=-1, keepdims=True) + EPS)
        xhat = xv * r
        dxhat = dy * g_ref[...]
        dx = dres_ref[...] + r * (dxhat - xhat * jnp.mean(dxhat * xhat, axis=-1, keepdims=True))
        dx_ref[...] = dx
        dxb_ref[...] = (copy_scale * dx).astype(BF16)

        @pl.when(pl.program_id(0) == 0)
        def _():
            dg_ref[...] = jnp.zeros_like(dg_ref)

        dg_ref[...] += jnp.sum(dy * xhat, axis=0, keepdims=True)

    row = pl.BlockSpec((tm, D), lambda i: (i, 0))
    vec = pl.BlockSpec((1, D), lambda i: (0, 0))
    in_specs, operands = [], []
    for a, b in pairs:
        in_specs += [pl.BlockSpec((tm, a.shape[1]), lambda i: (i, 0)),
                     pl.BlockSpec(b.shape, lambda i: (0, 0), pipeline_mode=pl.Buffered(1))]
        operands += [a, b]
    return pl.pallas_call(
        body, name=name, grid=(T // tm,),
        in_specs=in_specs + [row, vec, row], out_specs=[row, row, vec],
        out_shape=[jax.ShapeDtypeStruct((T, D), F32), jax.ShapeDtypeStruct((T, D), BF16),
                   jax.ShapeDtypeStruct((1, D), F32)],
        compiler_params=_params("arbitrary"),
    )(*operands, x, g, dres)


def _mm_tn(a, b, scale, name, tm=1792, tn=1792, tk=2048):
    T, M = a.shape
    N = b.shape[1]
    tm, tn, tk = _tile(M, tm), _tile(N, tn), min(tk, T)
    nk = T // tk

    def body(a_ref, b_ref, o_ref, acc_ref):
        k = pl.program_id(2)

        @pl.when(k == 0)
        def _():
            acc_ref[...] = jnp.zeros_like(acc_ref)

        acc_ref[...] += _dot_tn(a_ref[...], b_ref[...])

        @pl.when(k == nk - 1)
        def _():
            o_ref[...] = (scale * acc_ref[...]).astype(BF16)

    return pl.pallas_call(
        body, name=name, grid=(M // tm, N // tn, nk),
        in_specs=[pl.BlockSpec((tk, tm), lambda i, j, k: (k, i)), pl.BlockSpec((tk, tn), lambda i, j, k: (k, j))],
        out_specs=pl.BlockSpec((tm, tn), lambda i, j, k: (i, j)),
        out_shape=jax.ShapeDtypeStruct((M, N), BF16),
        scratch_shapes=[pltpu.VMEM((tm, tn), F32)],
        compiler_params=_params("parallel", "parallel", "arbitrary"),
    )(a, b)


def _rms_rows(x_ref, gain_ref):
    xv = x_ref[...]
    r = lax.rsqrt(jnp.mean(xv * xv, axis=-1, keepdims=True) + EPS)
    return (xv * r * gain_ref[...]).astype(BF16)


def _ffn_in(x, gain, wg_t, wu_t, tm=512, tn=FF_HALF):
    T = x.shape[0]
    tm = min(tm, T)

    def body(x_ref, gain_ref, wg_ref, wu_ref, h_ref, g_ref, u_ref, a_ref):
        hv = _rms_rows(x_ref, gain_ref)

        @pl.when(pl.program_id(0) == 0)
        def _():
            h_ref[...] = hv

        g = _dot_nt(hv, wg_ref[...])
        u = _dot_nt(hv, wu_ref[...])
        g_ref[...] = g.astype(BF16)
        u_ref[...] = u.astype(BF16)
        a_ref[...] = (g * _sigmoid(g) * u).astype(BF16)

    row = pl.BlockSpec((tm, D), lambda j, i: (i, 0))
    last = T // tm - 1
    h_spec = pl.BlockSpec((tm, D), lambda j, i: (jnp.where(j == 0, i, last), 0))
    wspec = pl.BlockSpec((tn, D), lambda j, i: (j, 0))
    ospec = pl.BlockSpec((tm, tn), lambda j, i: (i, j))
    oshape = jax.ShapeDtypeStruct((T, D_FFP), BF16)
    return pl.pallas_call(
        body, name="ffn_in", grid=(D_FFP // tn, T // tm),
        in_specs=[row, pl.BlockSpec((1, D), lambda j, i: (0, 0)), wspec, wspec],
        out_specs=[h_spec, ospec, ospec, ospec],
        out_shape=[jax.ShapeDtypeStruct((T, D), BF16), oshape, oshape, oshape],
        compiler_params=_params("arbitrary", "arbitrary"),
    )(x, gain, wg_t, wu_t)


def _mix_in(x, gain, w_in_t, tm=512):
    T = x.shape[0]
    tm = min(tm, T)

    def body(x_ref, gain_ref, w_ref, h_ref, p_ref):
        hv = _rms_rows(x_ref, gain_ref)
        h_ref[...] = hv
        p_ref[...] = _dot_nt(hv, w_ref[...])

    row = pl.BlockSpec((tm, D), lambda i: (i, 0))
    return pl.pallas_call(
        body, name="mix_in", grid=(T // tm,),
        in_specs=[row, pl.BlockSpec((1, D), lambda i: (0, 0)), pl.BlockSpec((D_IN, D), lambda i: (0, 0))],
        out_specs=[row, pl.BlockSpec((tm, D_IN), lambda i: (i, 0))],
        out_shape=[jax.ShapeDtypeStruct((T, D), BF16), jax.ShapeDtypeStruct((T, D_IN), F32)],
        compiler_params=_params("parallel"),
    )(x, gain, w_in_t)


def _ffn_mid_bwd(dyh, wout, g, u, tm=512, tn=FF_HALF):
    T = dyh.shape[0]
    tm = min(tm, T)

    def body(dy_ref, w_ref, g_ref, u_ref, dg_ref, du_ref):
        da = _dot_nt(dy_ref[...], w_ref[...])
        g = g_ref[...].astype(F32)
        s = _sigmoid(g)
        gs = g * s
        du_ref[...] = (da * gs).astype(BF16)
        dg_ref[...] = ((da * u_ref[...].astype(F32)) * (s + gs - gs * s)).astype(BF16)

    ospec = pl.BlockSpec((tm, tn), lambda j, i: (i, j))
    oshape = jax.ShapeDtypeStruct((T, D_FFP), BF16)
    return pl.pallas_call(
        body, name="ffn_mid_bwd", grid=(D_FFP // tn, T // tm),
        in_specs=[pl.BlockSpec((tm, D), lambda j, i: (i, 0)), pl.BlockSpec((tn, D), lambda j, i: (j, 0)),
                  ospec, ospec],
        out_specs=[ospec, ospec], out_shape=[oshape, oshape],
        compiler_params=_params("parallel", "parallel"),
    )(dyh, wout, g, u)


def _final(x, tgt, gf, copy_scale, tm=512):
    T = x.shape[0]
    tm = min(tm, T)

    def body(x_ref, t_ref, g_ref, dx_ref, dxb_ref, dg_ref, loss_ref):
        xv = x_ref[...]
        r = lax.rsqrt(jnp.mean(xv * xv, axis=-1, keepdims=True) + EPS)
        xhat = xv * r
        err = xhat * g_ref[...] - t_ref[...]
        dy = err * (1.0 / D)
        dxhat = dy * g_ref[...]
        dx = r * (dxhat - xhat * jnp.mean(dxhat * xhat, axis=-1, keepdims=True))
        dx_ref[...] = dx
        dxb_ref[...] = (copy_scale * dx).astype(BF16)

        @pl.when(pl.program_id(0) == 0)
        def _():
            dg_ref[...] = jnp.zeros_like(dg_ref)
            loss_ref[...] = jnp.zeros_like(loss_ref)

        dg_ref[...] += jnp.sum(dy * xhat, axis=0, keepdims=True)
        loss_ref[...] += (0.5 / D) * jnp.sum(err * err)

    row = pl.BlockSpec((tm, D), lambda i: (i, 0))
    vec = pl.BlockSpec((1, D), lambda i: (0, 0))
    return pl.pallas_call(
        body, name="final_loss", grid=(T // tm,),
        in_specs=[row, row, vec],
        out_specs=[row, row, vec, pl.BlockSpec((1, 128), lambda i: (0, 0))],
        out_shape=[jax.ShapeDtypeStruct((T, D), F32), jax.ShapeDtypeStruct((T, D), BF16),
                   jax.ShapeDtypeStruct((1, D), F32), jax.ShapeDtypeStruct((1, 128), F32)],
        compiler_params=_params("arbitrary"),
    )(x, tgt, gf)


def _mix_block(T, limit):
    return min(limit, T // 2)


def _rows(tb, width):
    return lax.broadcasted_iota(jnp.int32, (tb, width), 0)


def _rglru_gates(xc, wax_ref, bax_ref, lam_ref):
    pre = _dot(xc.astype(BF16), wax_ref[...]) + bax_ref[...]
    r = _sigmoid(pre[:, :D_RNN])
    ig = _sigmoid(pre[:, D_RNN:])
    z = -lam_ref[...]
    sp = jnp.maximum(z, 0.0) + jnp.log(1.0 + jnp.exp(-jnp.abs(z)))
    log_a = (-LRU_C) * r * sp
    a = jnp.exp(log_a)
    mult = jnp.sqrt(-jnp.tanh(log_a) * (1.0 + a * a))
    return r, ig, sp, a, mult


def _conv(xa_ext, cw_ref, cb_ref):
    y = cb_ref[...] + cw_ref[3:4, :] * xa_ext
    for k in range(1, 4):
        y = y + cw_ref[3 - k:4 - k, :] * pltpu.roll(xa_ext, k, 0)
    return y[HALO:]


def _pool_window_lanes():
    lane = lax.broadcasted_iota(jnp.int32, (1, D_POOL), 1)
    return jnp.where(lane < 64, 2, jnp.where(lane < 128, 4, jnp.where(lane < 192, 8, 16)))


def _pool_select(s2, s4, s8, s16):
    lane = lax.broadcasted_iota(jnp.int32, s2.shape, 1)
    return jnp.where(lane < 64, s2, jnp.where(lane < 128, s4, jnp.where(lane < 192, s8, s16)))


def _pool_diff(xp_ext, t0, tb):
    s2 = xp_ext + pltpu.roll(xp_ext, 1, 0)
    s4 = s2 + pltpu.roll(s2, 2, 0)
    s8 = s4 + pltpu.roll(s4, 4, 0)
    s16 = s8 + pltpu.roll(s8, 8, 0)
    sel = _pool_select(s2, s4, s8, s16)[HALO:]
    cnt = jnp.minimum(t0 + _rows(tb, D_POOL) + 1, _pool_window_lanes()).astype(F32)
    return sel / cnt - xp_ext[HALO:], cnt


def _head_masks():
    lane = lax.broadcasted_iota(jnp.int32, (1, D_SGU), 1)
    return [((lane >= 64 * h) & (lane < 64 * (h + 1))).astype(F32) for h in range(4)]


def _sgu_mix(w_ref, vch, masks):
    z = masks[0] * _dot(w_ref[0], vch)
    for h in range(1, 4):
        z = z + masks[h] * _dot(w_ref[h], vch)
    return z


def _mix_fwd(p, prm):
    T = p.shape[0]
    tb = _mix_block(T, 512)
    nb = T // tb

    def body(p_ref, xah_ref, xph_ref, cw_ref, cb_ref, wax_ref, bax_ref, lam_ref, wp_ref, ps_ref, sgn_ref,
             ws_ref, bz_ref, y_ref, hs_ref, carry_ref):
        i = pl.program_id(0)
        keep = (i > 0).astype(F32)

        @pl.when(i == 0)
        def _():
            carry_ref[...] = jnp.zeros_like(carry_ref)

        xa_ext = jnp.concatenate([xah_ref[...] * keep, p_ref[:, 512:1024]], axis=0)
        xc = _conv(xa_ext, cw_ref, cb_ref)
        r, ig, sp, a, mult = _rglru_gates(xc, wax_ref, bax_ref, lam_ref)
        bv = mult * (ig * xc)
        row = _rows(tb, D_RNN)
        s = 1
        while s < tb:
            m = row >= s
            bv = jnp.where(m, a * pltpu.roll(bv, s, 0) + bv, bv)
            a = jnp.where(m, a * pltpu.roll(a, s, 0), a)
            s *= 2
        h = bv + a * carry_ref[0:1, :]
        hs_ref[...] = h
        last = jnp.sum(jnp.where(_rows(8, D_RNN) == 7, hs_ref[tb - 8:tb, :], 0.0), axis=0, keepdims=True)
        carry_ref[...] = jnp.broadcast_to(last, carry_ref.shape)
        y_ref[:, 0:512] = (_gelu(p_ref[:, 0:512]) * h).astype(BF16)

        xp_ext = jnp.concatenate([xph_ref[...] * keep, p_ref[:, 1024:1280]], axis=0)
        d, _ = _pool_diff(xp_ext, i * tb, tb)
        y_ref[:, 512:768] = (_dot(d.astype(BF16), wp_ref[...]) * ps_ref[...]).astype(BF16)

        ug = _gelu(p_ref[:, 1280:1536])
        vg = _gelu(p_ref[:, 1536:1792])
        rv = lax.rsqrt(jnp.mean(vg * vg, axis=-1, keepdims=True) + EPS)
        vn = (vg * rv * sgn_ref[...]).astype(BF16)
        masks = _head_masks()
        for ci in range(tb // CHUNK):
            sl = slice(ci * CHUNK, (ci + 1) * CHUNK)
            z = _sgu_mix(ws_ref, vn[sl], masks) + bz_ref[...]
            y_ref[sl, 768:1024] = (ug[sl] * z).astype(BF16)

    hb = tb // HALO

    def halo(i):
        return jnp.maximum(i * hb - 1, 0)

    def full(shape):
        return pl.BlockSpec(shape, lambda i: (0,) * len(shape))

    return pl.pallas_call(
        body, name="mix_fwd", grid=(nb,),
        in_specs=[pl.BlockSpec((tb, D_IN), lambda i: (i, 0)),
                  pl.BlockSpec((HALO, D_RNN), lambda i: (halo(i), 1)),
                  pl.BlockSpec((HALO, D_POOL), lambda i: (halo(i), 4)),
                  full((4, D_RNN)), full((1, D_RNN)), full((D_RNN, 2 * D_RNN)), full((1, 2 * D_RNN)),
                  full((1, D_RNN)), full((D_POOL, D_POOL)), full((1, D_POOL)), full((1, D_SGU)),
                  full((4, CHUNK, CHUNK)), full((CHUNK, D_SGU))],
        out_specs=[pl.BlockSpec((tb, D), lambda i: (i, 0)), pl.BlockSpec((tb, D_RNN), lambda i: (i, 0))],
        out_shape=[jax.ShapeDtypeStruct((T, D), BF16), jax.ShapeDtypeStruct((T, D_RNN), F32)],
        scratch_shapes=[pltpu.VMEM((8, D_RNN), F32)],
        compiler_params=_params("arbitrary"),
    )(p, p, p, prm["conv_w"], prm["conv_b"], prm["wax"], prm["bax"], prm["lam"], prm["wpool"], prm["pool_scale"],
      prm["sgu_norm"], prm["ws"], prm["bz"])


def _mix_bwd(dy, p, hs, prm):
    T = p.shape[0]
    tb = _mix_block(T, 256)
    nb = T // tb
    hb = tb // HALO

    def body(dy_ref, p_ref, xah_ref, xph_ref, hs_ref, hsh_ref, cw_ref, cb_ref, wax_ref, waxt_ref, bax_ref,
             lam_ref, wp_ref, wpt_ref, ps_ref, sgn_ref, ws_ref, wst_ref, bz_ref,
             dp_ref, dcw_ref, dcb_ref, dwax_ref, dbax_ref, dlam_ref, dwp_ref, dps_ref, dsgn_ref, dws_ref,
             dbz_ref, gcarry_ref, xcfut_ref, mfut_ref):
        i = pl.program_id(0)
        bi = nb - 1 - i
        keep = (bi > 0).astype(F32)

        @pl.when(i == 0)
        def _():
            for ref in (dcw_ref, dcb_ref, dwax_ref, dbax_ref, dlam_ref, dwp_ref, dps_ref, dsgn_ref, dws_ref,
                        dbz_ref, gcarry_ref, xcfut_ref, mfut_ref):
                ref[...] = jnp.zeros_like(ref)

        xa_ext = jnp.concatenate([xah_ref[...] * keep, p_ref[:, 512:1024]], axis=0)
        xc = _conv(xa_ext, cw_ref, cb_ref)
        r, ig, sp, a, mult = _rglru_gates(xc, wax_ref, bax_ref, lam_ref)
        gg, dgg = _gelu_and_grad(p_ref[:, 0:512])
        dya = dy_ref[:, 0:512]
        dp_ref[:, 0:512] = (dya * hs_ref[...] * dgg).astype(BF16)
        row = _rows(tb, D_RNN)
        g = dya * gg + jnp.where(row == tb - 1, gcarry_ref[0:1, :], 0.0)
        al = pltpu.roll(a, tb - 1, 0)
        s = 1
        while s < tb:
            m = row < tb - s
            g = jnp.where(m, al * pltpu.roll(g, tb - s, 0) + g, g)
            al = jnp.where(m, al * pltpu.roll(al, tb - s, 0), al)
            s *= 2
        first = jnp.sum(jnp.where(_rows(8, D_RNN) == 0, (a * g)[0:8], 0.0), axis=0, keepdims=True)
        gcarry_ref[...] = jnp.broadcast_to(first, gcarry_ref.shape)
        hs_ext = jnp.concatenate([hsh_ref[...] * keep, hs_ref[...]], axis=0)
        h_prev = pltpu.roll(hs_ext, 1, 0)[HALO:]
        ix = ig * xc
        dlog_a = g * h_prev * a - (g * ix) * (a * a / mult)
        dlam_ref[...] += jnp.sum(dlog_a * r, axis=0, keepdims=True) * (LRU_C * _sigmoid(-lam_ref[...]))
        dpre_r = dlog_a * ((-LRU_C) * sp) * (r * (1.0 - r))
        dpre_i = (g * mult * xc) * (ig * (1.0 - ig))
        dpre = jnp.concatenate([dpre_r, dpre_i], axis=1)
        dbax_ref[...] += jnp.sum(dpre, axis=0, keepdims=True)
        dpre_b = dpre.astype(BF16)
        dwax_ref[...] += _dot_tn(xc.astype(BF16), dpre_b)
        dxc = g * mult * ig + _dot(dpre_b, waxt_ref[...])
        dcb_ref[...] += jnp.sum(dxc, axis=0, keepdims=True)
        for k in range(4):
            xs = xa_ext[HALO:] if k == 3 else pltpu.roll(xa_ext, 3 - k, 0)[HALO:]
            dcw_ref[k:k + 1, :] += jnp.sum(dxc * xs, axis=0, keepdims=True)
        dxc_ext = jnp.concatenate([dxc, xcfut_ref[...]], axis=0)
        n = tb + HALO
        dxa = cw_ref[3:4, :] * dxc_ext
        for k in range(1, 4):
            dxa = dxa + cw_ref[3 - k:4 - k, :] * pltpu.roll(dxc_ext, n - k, 0)
        dp_ref[:, 512:1024] = dxa[:tb].astype(BF16)
        xcfut_ref[...] = dxc[0:HALO]

        xp_ext = jnp.concatenate([xph_ref[...] * keep, p_ref[:, 1024:1280]], axis=0)
        d, cnt = _pool_diff(xp_ext, bi * tb, tb)
        db = d.astype(BF16)
        dyb = dy_ref[:, 512:768]
        dps_ref[...] += jnp.sum(dyb * _dot(db, wp_ref[...]), axis=0, keepdims=True)
        dq = (dyb * ps_ref[...]).astype(BF16)
        dwp_ref[...] += _dot_tn(db, dq)
        dd = _dot(dq, wpt_ref[...])
        mm = dd / cnt
        m_ext = jnp.concatenate([mm, mfut_ref[...]], axis=0)
        f2 = m_ext + pltpu.roll(m_ext, n - 1, 0)
        f4 = f2 + pltpu.roll(f2, n - 2, 0)
        f8 = f4 + pltpu.roll(f4, n - 4, 0)
        f16 = f8 + pltpu.roll(f8, n - 8, 0)
        dp_ref[:, 1024:1280] = (_pool_select(f2, f4, f8, f16)[:tb] - dd).astype(BF16)
        mfut_ref[...] = mm[0:HALO]

        ug, dug = _gelu_and_grad(p_ref[:, 1280:1536])
        vg, dvg = _gelu_and_grad(p_ref[:, 1536:1792])
        rv = lax.rsqrt(jnp.mean(vg * vg, axis=-1, keepdims=True) + EPS)
        vhat = vg * rv
        vn = (vhat * sgn_ref[...]).astype(BF16)
        dyc = dy_ref[:, 768:1024]
        masks = _head_masks()
        dz = dyc * ug
        dzb = dz.astype(BF16)
        dvn_parts = []
        for ci in range(tb // CHUNK):
            sl = slice(ci * CHUNK, (ci + 1) * CHUNK)
            z = _sgu_mix(ws_ref, vn[sl], masks) + bz_ref[...]
            dp_ref[sl, 1280:1536] = (dyc[sl] * z * dug[sl]).astype(BF16)
            dbz_ref[...] += dz[sl]
            for h in range(4):
                dws_ref[h] += _dot_nt((dz[sl] * masks[h]).astype(BF16), vn[sl])
            dvn_parts.append(_sgu_mix(wst_ref, dzb[sl], masks))
        dvn = jnp.concatenate(dvn_parts, axis=0)
        dsgn_ref[...] += jnp.sum(dvn * vhat, axis=0, keepdims=True)
        dvhat = dvn * sgn_ref[...]
        dvg_in = rv * (dvhat - vhat * jnp.mean(dvhat * vhat, axis=-1, keepdims=True))
        dp_ref[:, 1536:1792] = (dvg_in * dvg).astype(BF16)

        @pl.when(i == nb - 1)
        def _():
            tril = (lax.broadcasted_iota(jnp.int32, (CHUNK, CHUNK), 0)
                    >= lax.broadcasted_iota(jnp.int32, (CHUNK, CHUNK), 1)).astype(F32)
            for h in range(4):
                dws_ref[h] = dws_ref[h] * tril

    def blk(i):
        return nb - 1 - i

    def halo(i):
        return jnp.maximum(blk(i) * hb - 1, 0)

    def full(shape):
        return pl.BlockSpec(shape, lambda i: (0,) * len(shape))

    small_shapes = [(4, D_RNN), (1, D_RNN), (D_RNN, 2 * D_RNN), (1, 2 * D_RNN), (1, D_RNN), (D_POOL, D_POOL),
                    (1, D_POOL), (1, D_SGU), (4, CHUNK, CHUNK), (CHUNK, D_SGU)]
    outs = pl.pallas_call(
        body, name="mix_bwd", grid=(nb,),
        in_specs=[pl.BlockSpec((tb, D), lambda i: (blk(i), 0)),
                  pl.BlockSpec((tb, D_IN), lambda i: (blk(i), 0)),
                  pl.BlockSpec((HALO, D_RNN), lambda i: (halo(i), 1)),
                  pl.BlockSpec((HALO, D_POOL), lambda i: (halo(i), 4)),
                  pl.BlockSpec((tb, D_RNN), lambda i: (blk(i), 0)),
                  pl.BlockSpec((HALO, D_RNN), lambda i: (halo(i), 0)),
                  full((4, D_RNN)), full((1, D_RNN)), full((D_RNN, 2 * D_RNN)), full((2 * D_RNN, D_RNN)),
                  full((1, 2 * D_RNN)), full((1, D_RNN)), full((D_POOL, D_POOL)), full((D_POOL, D_POOL)),
                  full((1, D_POOL)), full((1, D_SGU)), full((4, CHUNK, CHUNK)), full((4, CHUNK, CHUNK)),
                  full((CHUNK, D_SGU))],
        out_specs=[pl.BlockSpec((tb, D_IN), lambda i: (blk(i), 0))] + [full(s) for s in small_shapes],
        out_shape=[jax.ShapeDtypeStruct((T, D_IN), BF16)] + [jax.ShapeDtypeStruct(s, F32) for s in small_shapes],
        scratch_shapes=[pltpu.VMEM((8, D_RNN), F32), pltpu.VMEM((HALO, D_RNN), F32),
                        pltpu.VMEM((HALO, D_POOL), F32)],
        compiler_params=_params("arbitrary"),
    )(dy, p, p, p, hs, hs, prm["conv_w"], prm["conv_b"], prm["wax"], prm["wax_t"], prm["bax"], prm["lam"],
      prm["wpool"], prm["wpool_t"], prm["pool_scale"], prm["sgu_norm"], prm["ws"], prm["ws_t"], prm["bz"])
    names = ("dp", "conv_w", "conv_b", "wax", "bax", "lam", "wpool", "pool_scale", "sgu_norm", "ws", "bz")
    return dict(zip(names, outs))


ANY = pl.BlockSpec(memory_space=pl.ANY)


def _place():
    x, y, c = lax.axis_index("x"), lax.axis_index("y"), lax.axis_index("c")
    return x, y, c


def _all_to_all(xs, name):
    def body(in_ref, out_ref, send_sems, recv_sems, local_sem):
        x, y, c = _place()
        me = 4 * x + 2 * y + c
        mine = pltpu.make_async_copy(in_ref.at[me], out_ref.at[me], local_sem)
        mine.start()
        copies = []
        for rel in range(1, N_DEV):
            tx = 1 - x if rel & 4 else x
            ty = 1 - y if rel & 2 else y
            tc = 1 - c if rel & 1 else c
            cp = pltpu.make_async_remote_copy(
                src_ref=in_ref.at[4 * tx + 2 * ty + tc], dst_ref=out_ref.at[me],
                send_sem=send_sems.at[rel - 1], recv_sem=recv_sems.at[rel - 1],
                device_id=(tx, ty, tc), device_id_type=MESH)
            cp.start()
            copies.append(cp)
        for cp in copies:
            cp.wait()
        mine.wait()

    return pl.pallas_call(
        body, name=name, in_specs=[ANY], out_specs=ANY,
        out_shape=jax.ShapeDtypeStruct(xs.shape, xs.dtype),
        scratch_shapes=[pltpu.SemaphoreType.DMA((N_DEV - 1,)), pltpu.SemaphoreType.DMA((N_DEV - 1,)),
                        pltpu.SemaphoreType.DMA],
    )(xs)


def _all_gather8(xs, name):
    def body(x_ref, out_ref, send_sems, recv_sems, local_sem):
        x, y, c = _place()
        me, sibling = (x, y, c), (x, y, 1 - c)
        chips = [(1 - x, y), (x, 1 - y), (1 - x, 1 - y)]

        def rows(px, py, pc):
            return out_ref.at[4 * px + 2 * py + pc]

        def copy(k, block, to, src=None):
            return pltpu.make_async_remote_copy(
                src_ref=rows(*block) if src is None else src, dst_ref=rows(*block),
                send_sem=send_sems.at[k], recv_sem=recv_sems.at[k], device_id=to, device_id_type=MESH)

        mine = pltpu.make_async_copy(x_ref, rows(*me), local_sem)
        mine.start()
        first = [copy(0, me, sibling, src=x_ref)]
        first += [copy(1 + j, me, (*chip, c), src=x_ref) for j, chip in enumerate(chips)]
        for cp in first:
            cp.start()
        passed = [copy(4 + j, (*chip, c), sibling) for j, chip in enumerate(chips)]
        for j, chip in enumerate(chips):
            copy(1 + j, (*chip, c), me).wait_recv()
            passed[j].start()
        copy(0, sibling, me).wait_recv()
        for j, chip in enumerate(chips):
            copy(4 + j, (*chip, 1 - c), me).wait_recv()
        for cp in first + passed:
            cp.wait_send()
        mine.wait()

    return pl.pallas_call(
        body, name=name, in_specs=[ANY], out_specs=ANY,
        out_shape=jax.ShapeDtypeStruct((N_DEV,) + xs.shape, xs.dtype),
        scratch_shapes=[pltpu.SemaphoreType.DMA((7,)), pltpu.SemaphoreType.DMA((7,)), pltpu.SemaphoreType.DMA],
    )(xs)


HBM = pl.BlockSpec(memory_space=pltpu.HBM)
SEM = pl.BlockSpec(memory_space=pltpu.SEMAPHORE)
EFFECT = pltpu.SideEffectType.DATAFLOW_SIDE_EFFECTING


def _in_hbm(a):
    return pltpu.with_memory_space_constraint(a, pltpu.HBM)


def _local_copy(src, dst, stage, sem):
    load = pltpu.make_async_copy(src, stage, sem)
    load.start()
    load.wait()
    store = pltpu.make_async_copy(stage, dst, sem)
    store.start()
    store.wait()


def _unique(windows):
    arrays = []
    for per_chip in windows:
        for arr, _ in per_chip:
            if not any(arr is a for a in arrays):
                arrays.append(arr)
    return arrays


def _exchange_start(layer, windows, lands, name):
    arrays = _unique(windows)
    na, nt = len(arrays), len(windows)

    def body(*refs):
        in_refs, land_refs = refs[:na], refs[na:na + nt]
        send_sems, recv_sems = refs[na + nt], refs[na + nt + 1]
        token = refs[-1]
        x, y, c = _place()
        me = 4 * x + 2 * y + c
        for t in range(nt):
            for j in range(N_CHIP):
                arr, window = windows[t][j]
                src = window(in_refs[next(i for i, a in enumerate(arrays) if a is arr)])

                @pl.when(me != 2 * j + layer)
                def _():
                    pltpu.make_async_remote_copy(
                        src_ref=src, dst_ref=land_refs[t].at[me], send_sem=send_sems.at[N_CHIP * t + j],
                        recv_sem=recv_sems.at[N_DEV * t + me], device_id=(j // 2, j % 2, layer),
                        device_id_type=MESH).start()
        token[...] = jnp.zeros_like(token)

    outs = pl.pallas_call(
        body, name=name,
        out_shape=(pltpu.SemaphoreType.DMA((N_CHIP * nt,)), pltpu.SemaphoreType.DMA((N_DEV * nt,)),
                   *[pltpu.HBM(a.shape, a.dtype) for a in lands], jax.ShapeDtypeStruct((8, 128), F32)),
        in_specs=[HBM] * (na + nt),
        out_specs=(SEM, SEM, *[HBM] * nt, pl.BlockSpec(memory_space=pltpu.VMEM)),
        input_output_aliases={na + t: 2 + t for t in range(nt)},
        compiler_params=pltpu.CompilerParams(has_side_effects=EFFECT, vmem_limit_bytes=VMEM_LIMIT),
    )(*[_in_hbm(a) for a in arrays], *[_in_hbm(a) for a in lands])
    return outs[0], outs[1], list(outs[2:2 + nt]), outs[-1]


def _exchange_wait(layer, windows, lands, send_sems, recv_sems, after, name):
    arrays = _unique(windows)
    na, nt = len(arrays), len(windows)

    def body(*refs):
        in_refs, land_refs = refs[:na], refs[na:na + nt]
        send_sems, recv_sems = refs[na + nt], refs[na + nt + 1]
        stages, local_sem = refs[-1 - nt:-1], refs[-1]
        x, y, c = _place()
        me = 4 * x + 2 * y + c

        def source(t, j):
            arr, window = windows[t][j]
            return window(in_refs[next(i for i, a in enumerate(arrays) if a is arr)])

        @pl.when(c == layer)
        def _():
            for t in range(nt):
                for j in range(N_CHIP):
                    @pl.when(me == 2 * j + layer)
                    def _():
                        _local_copy(source(t, j), land_refs[t].at[me], stages[t], local_sem)

        for t in range(nt):
            for j in range(N_CHIP):
                @pl.when(me != 2 * j + layer)
                def _():
                    pltpu.make_async_remote_copy(
                        src_ref=source(t, j), dst_ref=land_refs[t].at[me], send_sem=send_sems.at[N_CHIP * t + j],
                        recv_sem=recv_sems.at[N_DEV * t + me], device_id=(j // 2, j % 2, layer),
                        device_id_type=MESH).wait_send()

        @pl.when(c == layer)
        def _():
            for t in range(nt):
                for s in range(N_DEV):
                    @pl.when(me != s)
                    def _():
                        slot = land_refs[t].at[s]
                        pltpu.make_async_remote_copy(
                            src_ref=slot, dst_ref=slot, send_sem=send_sems.at[N_CHIP * t],
                            recv_sem=recv_sems.at[N_DEV * t + s], device_id=(x, y, c),
                            device_id_type=MESH).wait_recv()

    outs = pl.pallas_call(
        body, name=name,
        out_shape=tuple(pltpu.HBM(a.shape, a.dtype) for a in lands),
        in_specs=[HBM] * (na + nt) + [SEM, SEM, ANY],
        out_specs=tuple([HBM] * nt),
        input_output_aliases={na + t: t for t in range(nt)},
        scratch_shapes=[pltpu.VMEM(a.shape[1:], a.dtype) for a in lands] + [pltpu.SemaphoreType.DMA],
        compiler_params=pltpu.CompilerParams(has_side_effects=EFFECT, vmem_limit_bytes=VMEM_LIMIT),
    )(*[_in_hbm(a) for a in arrays], *lands, send_sems, recv_sems, after)
    return list(outs)


def _other_chips(x, y):
    return [(1 - x, y), (x, 1 - y), (1 - x, 1 - y)]


def _split_rows(rows):
    if rows < 32:
        return [(0, rows), (rows, 0)]
    cut = -(-(rows // 2) // 16) * 16
    return [(0, cut), (cut, rows - cut)]


def _gather_copies(in_refs, land_refs, send_sems, recv_sems, x, y, sender_core, both):
    copies = []
    for t, src in enumerate(in_refs):
        r0, n = _split_rows(src.shape[0])[sender_core]
        if n == 0:
            continue
        for rel, (tx, ty) in enumerate(_other_chips(x, y)):
            for tc in (0, 1) if both else (sender_core,):
                copies.append(pltpu.make_async_remote_copy(
                    src_ref=src.at[pl.ds(r0, n)], dst_ref=land_refs[t].at[2 * x + y].at[pl.ds(r0, n)],
                    send_sem=send_sems.at[6 * t + 2 * rel + tc],
                    recv_sem=recv_sems.at[2 * (3 * t + rel) + sender_core],
                    device_id=(tx, ty, tc), device_id_type=MESH))
    return copies


def _gather_start(shards, lands, name, both):
    nt = len(shards)

    def body(*refs):
        in_refs, land_refs = refs[:nt], refs[nt:2 * nt]
        send_sems, recv_sems = refs[2 * nt], refs[2 * nt + 1]
        token = refs[-1]
        x, y, c = _place()
        for core in range(2):
            @pl.when(c == core)
            def _():
                for cp in _gather_copies(in_refs, land_refs, send_sems, recv_sems, x, y, core, both):
                    cp.start()
        token[...] = jnp.zeros_like(token)

    outs = pl.pallas_call(
        body, name=name,
        out_shape=(pltpu.SemaphoreType.DMA((6 * nt,)), pltpu.SemaphoreType.DMA((6 * nt,)),
                   *[pltpu.HBM(a.shape, a.dtype) for a in lands], jax.ShapeDtypeStruct((8, 128), F32)),
        in_specs=[HBM] * (2 * nt),
        out_specs=(SEM, SEM, *[HBM] * nt, pl.BlockSpec(memory_space=pltpu.VMEM)),
        input_output_aliases={nt + t: 2 + t for t in range(nt)},
        compiler_params=pltpu.CompilerParams(has_side_effects=EFFECT, vmem_limit_bytes=VMEM_LIMIT),
    )(*[_in_hbm(a) for a in shards], *[_in_hbm(a) for a in lands])
    return outs[0], outs[1], list(outs[2:2 + nt]), outs[-1]


def _gather_wait(shards, lands, send_sems, recv_sems, after, name, both):
    nt = len(shards)

    def body(*refs):
        in_refs, land_refs = refs[:nt], refs[nt:2 * nt]
        send_sems, recv_sems = refs[2 * nt], refs[2 * nt + 1]
        stages, local_sem = refs[-1 - nt:-1], refs[-1]
        x, y, c = _place()

        for t in range(nt):
            _local_copy(in_refs[t], land_refs[t].at[2 * x + y], stages[t], local_sem)

        for core in range(2):
            @pl.when(c == core)
            def _():
                for cp in _gather_copies(in_refs, land_refs, send_sems, recv_sems, x, y, core, both):
                    cp.wait_send()

        def wait_parts_from(core):
            for t in range(nt):
                r0, n = _split_rows(in_refs[t].shape[0])[core]
                for rel, (tx, ty) in enumerate(_other_chips(x, y)):
                    if n > 0:
                        part = land_refs[t].at[2 * tx + ty].at[pl.ds(r0, n)]
                        pltpu.make_async_remote_copy(
                            src_ref=part, dst_ref=part, send_sem=send_sems.at[6 * t],
                            recv_sem=recv_sems.at[2 * (3 * t + rel) + core], device_id=(x, y, c),
                            device_id_type=MESH).wait_recv()

        for core in range(2):
            if both:
                wait_parts_from(core)
            else:
                pl.when(c == core)(functools.partial(wait_parts_from, core))

    outs = pl.pallas_call(
        body, name=name,
        out_shape=tuple(pltpu.HBM(a.shape, a.dtype) for a in lands),
        in_specs=[HBM] * (2 * nt) + [SEM, SEM, ANY],
        out_specs=tuple([HBM] * nt),
        input_output_aliases={nt + t: t for t in range(nt)},
        scratch_shapes=[pltpu.VMEM(a.shape, a.dtype) for a in shards] + [pltpu.SemaphoreType.DMA],
        compiler_params=pltpu.CompilerParams(has_side_effects=EFFECT, vmem_limit_bytes=VMEM_LIMIT),
    )(*[_in_hbm(a) for a in shards], *lands, send_sems, recv_sems, after)
    return list(outs)


def _pair_share(lands, name):
    nt = len(lands)
    splits = [_split_rows(a.shape[1]) for a in lands]

    def body(*refs):
        land_refs = refs[:nt]
        stages = refs[2 * nt:3 * nt]
        load_sems, send_sems, recv_sems = refs[3 * nt:]
        x, y, c = _place()

        def parts(core):
            out = []
            for t in range(nt):
                r0, n = splits[t][core]
                for rel, (tx, ty) in enumerate(_other_chips(x, y)):
                    if n > 0:
                        out.append((3 * t + rel, stages[t].at[rel, pl.ds(0, n)],
                                    land_refs[t].at[2 * tx + ty].at[pl.ds(r0, n)]))
            return out

        def send(core):
            loads = [pltpu.make_async_copy(part, stage, load_sems.at[k]) for k, stage, part in parts(core)]
            for cp in loads:
                cp.start()
            pushes = []
            for cp, (k, stage, part) in zip(loads, parts(core)):
                cp.wait()
                push = pltpu.make_async_remote_copy(src_ref=stage, dst_ref=part, send_sem=send_sems.at[k],
                                                    recv_sem=recv_sems.at[k], device_id=(x, y, 1 - c),
                                                    device_id_type=MESH)
                push.start()
                pushes.append(push)
            for push in pushes:
                push.wait_send()
            for k, _, part in parts(1 - core):
                pltpu.make_async_remote_copy(src_ref=part, dst_ref=part, send_sem=send_sems.at[k],
                                             recv_sem=recv_sems.at[k], device_id=(x, y, 1 - c),
                                             device_id_type=MESH).wait_recv()

        for core in range(2):
            pl.when(c == core)(functools.partial(send, core))

    outs = pl.pallas_call(
        body, name=name, in_specs=[ANY] * nt, out_specs=[ANY] * nt,
        out_shape=[jax.ShapeDtypeStruct(a.shape, a.dtype) for a in lands],
        input_output_aliases={t: t for t in range(nt)},
        scratch_shapes=[pltpu.VMEM((3, max(n for _, n in sp), a.shape[2]), a.dtype) for a, sp in zip(lands, splits)]
        + [pltpu.SemaphoreType.DMA((3 * nt,))] * 3,
        compiler_params=pltpu.CompilerParams(vmem_limit_bytes=VMEM_LIMIT),
    )(*lands)
    return list(outs)


def _tie(a, token):
    def body(a_ref, token_ref, o_ref):
        pass

    return pl.pallas_call(
        body, name="tie", in_specs=[ANY, ANY], out_specs=ANY,
        out_shape=jax.ShapeDtypeStruct(a.shape, a.dtype), input_output_aliases={0: 0},
    )(a, token)


def _sum_share(xs, name):
    _, r, cols = xs.shape
    tr = max(t for t in range(16, min(r, 704) + 1, 16) if r % t == 0)
    nblk = r // tr

    def body(x_ref, out_ref, acc_ref, send_sems, local_sems, recv_sem):
        i = pl.program_id(0)
        slot = i % 2
        x, y, c = _place()

        def copies(s, blk):
            dst = out_ref.at[c, pl.ds(blk * tr, tr), :]
            loc = pltpu.make_async_copy(acc_ref.at[s], dst, local_sems.at[s])
            rem = pltpu.make_async_remote_copy(src_ref=acc_ref.at[s], dst_ref=dst, send_sem=send_sems.at[s],
                                               recv_sem=recv_sem, device_id=(x, y, 1 - c), device_id_type=MESH)
            return loc, rem

        @pl.when(i >= 2)
        def _():
            loc, rem = copies(slot, i - 2)
            loc.wait()
            rem.wait_send()

        acc = x_ref[0].astype(F32)
        for k in range(1, N_DEV):
            acc = acc + x_ref[k].astype(F32)
        acc_ref[slot] = acc
        loc, rem = copies(slot, i)
        loc.start()
        rem.start()

        @pl.when(i == nblk - 1)
        def _():
            for back in range(min(2, nblk)):
                blk = nblk - 1 - back
                loc, rem = copies(blk % 2, blk)
                loc.wait()
                rem.wait_send()
            theirs = out_ref.at[1 - c]
            pltpu.make_async_remote_copy(src_ref=theirs, dst_ref=theirs, send_sem=send_sems.at[0],
                                         recv_sem=recv_sem, device_id=(x, y, 1 - c),
                                         device_id_type=MESH).wait_recv()

    return pl.pallas_call(
        body, name=name, grid=(nblk,),
        in_specs=[pl.BlockSpec((N_DEV, tr, cols), lambda i: (0, i, 0))],
        out_specs=ANY,
        out_shape=jax.ShapeDtypeStruct((2, r, cols), F32),
        scratch_shapes=[pltpu.VMEM((2, tr, cols), F32), pltpu.SemaphoreType.DMA((2,)),
                        pltpu.SemaphoreType.DMA((2,)), pltpu.SemaphoreType.DMA],
        compiler_params=_params("arbitrary"),
    )(xs)


def _sum8(xs, name):
    _, r, cols = xs.shape
    tr = 8
    for cand in (256, 128, 64, 32, 16):
        if r % cand == 0:
            tr = cand
            break

    def body(x_ref, o_ref):
        acc = x_ref[0].astype(F32)
        for k in range(1, N_DEV):
            acc = acc + x_ref[k].astype(F32)
        o_ref[...] = acc

    return pl.pallas_call(
        body, name=name, grid=(r // tr,),
        in_specs=[pl.BlockSpec((N_DEV, tr, cols), lambda i: (0, i, 0))],
        out_specs=pl.BlockSpec((tr, cols), lambda i: (i, 0)),
        out_shape=jax.ShapeDtypeStruct((r, cols), F32),
        compiler_params=_params("parallel"),
    )(xs)


def _adamw(w, g, m, v, name):
    R, C = w.shape
    tr = max([t for t in range(8, min(R, 512) + 1, 8) if R % t == 0] or [R])
    c1 = 1.0 / (1.0 - ADAM_B1 ** ADAM_STEP)
    c2 = 1.0 / (1.0 - ADAM_B2 ** ADAM_STEP)

    def body(w_ref, g_ref, m_ref, v_ref, d_ref, nm_ref, nv_ref):
        gv = g_ref[...]
        nm = ADAM_B1 * m_ref[...] + (1.0 - ADAM_B1) * gv
        nv = ADAM_B2 * v_ref[...] + (1.0 - ADAM_B2) * (gv * gv)
        d_ref[...] = (-ADAM_LR) * ((nm * c1) / (jnp.sqrt(nv * c2) + ADAM_EPS) + ADAM_WD * w_ref[...])
        nm_ref[...] = nm
        nv_ref[...] = nv

    spec = pl.BlockSpec((tr, C), lambda i: (i, 0))
    shape = jax.ShapeDtypeStruct((R, C), F32)
    return pl.pallas_call(
        body, name=name, grid=(R // tr,), in_specs=[spec] * 4, out_specs=[spec] * 3, out_shape=[shape] * 3,
        compiler_params=_params("parallel"),
    )(w, g, m, v)


def _flat_rows(parts, rows):
    flat = jnp.concatenate([q.reshape(-1) for q in parts])
    flat = jnp.pad(flat, (0, rows * LANES - flat.shape[0]))
    return flat.reshape(rows, LANES)


def _round_up(n, m):
    return (n + m - 1) // m * m


def _block_diag(w):
    H, n, _ = w.shape
    eye = jnp.eye(H, dtype=w.dtype)
    return (eye[:, None, :, None] * w[:, :, None, :]).reshape(H * n, H * n)


def _diag_blocks(w, H, n):
    w4 = w.reshape(H, n, H, n)
    return jnp.stack([w4[h, :, h, :] for h in range(H)])


W_IN_T = ("ffn1_w_in", "w_in", "ffn2_w_in")


def _ffn_in_weights(g_in):
    zeros = jnp.zeros((FF_HALF - FF_SHARD, D), g_in.dtype)
    wg_t = jnp.concatenate([g_in[0], zeros, g_in[1], zeros], axis=0)
    wu_t = jnp.concatenate([g_in[2], zeros, g_in[3], zeros], axis=0)
    return wg_t, wu_t


def _ffn_out_weights(g_out):
    zeros = jnp.zeros((FF_HALF - FF_SHARD, D), g_out.dtype)
    return jnp.concatenate([g_out[0], g_out[1], zeros, g_out[2], g_out[3], zeros], axis=0)


LAND_SHAPES = {"ffn1_w_in": (FF_SHARD, D), "ffn1_w_out": (FF_ROWS, D), "w_in": (D_IN // N_CHIP, D),
               "w_out": (D // N_CHIP, D), "ffn2_w_in": (FF_SHARD, D), "ffn2_w_out": (FF_ROWS, D)}


def _rows_window(arr, start, size):
    return arr, lambda r: r.at[pl.ds(start, size), :]


def _ffn_grad_windows(dwg_t, dwu_t, dwout):
    w_in = [_rows_window(dwg_t if j < 2 else dwu_t, (j % 2) * FF_HALF, FF_SHARD) for j in range(N_CHIP)]
    w_out = [_rows_window(dwout, (j // 2) * FF_HALF + (j % 2) * FF_ROWS, FF_ROWS) for j in range(N_CHIP)]
    return w_in, w_out


def _mix_grad_windows(dwin_t, dwo):
    win = [_rows_window(dwin_t, j * (D_IN // N_CHIP), D_IN // N_CHIP) for j in range(N_CHIP)]
    wo = [_rows_window(dwo, j * (D // N_CHIP), D // N_CHIP) for j in range(N_CHIP)]
    return win, wo


def kernel(x, ffn1_norm, ffn1_w_in, ffn1_w_out, mix_norm, w_in, conv_w, conv_b, rg_w_a, rg_b_a, rg_w_x, rg_b_x, lru_lambda, pool_w, pool_scale, sgu_norm, sgu_w, sgu_b, w_out, ffn2_norm, ffn2_w_in, ffn2_w_out, final_norm, loss_target, m_ffn1_norm, m_ffn1_w_in, m_ffn1_w_out, m_mix_norm, m_w_in, m_conv_w, m_conv_b, m_rg_w_a, m_rg_b_a, m_rg_w_x, m_rg_b_x, m_lru_lambda, m_pool_w, m_pool_scale, m_sgu_norm, m_sgu_w, m_sgu_b, m_w_out, m_ffn2_norm, m_ffn2_w_in, m_ffn2_w_out, m_final_norm, v_ffn1_norm, v_ffn1_w_in, v_ffn1_w_out, v_mix_norm, v_w_in, v_conv_w, v_conv_b, v_rg_w_a, v_rg_b_a, v_rg_w_x, v_rg_b_x, v_lru_lambda, v_pool_w, v_pool_scale, v_sgu_norm, v_sgu_w, v_sgu_b, v_w_out, v_ffn2_norm, v_ffn2_w_in, v_ffn2_w_out, v_final_norm):
    args = locals()
    W = {n: args[n] for n in WEIGHTS}
    M = {n: args["m_" + n] for n in WEIGHTS}
    V = {n: args["v_" + n] for n in WEIGHTS}
    depth = ffn1_norm.shape[0]
    T = x.shape[1]
    xi, yi, ci = _place()
    chip = 2 * xi + yi

    assert depth == 2, "core c of a chip sends and reduces layer c"
    groups = [(l, names) for l in range(depth)
              for names in (["ffn1_w_in"], ["ffn1_w_out", "w_in", "w_out"], ["ffn2_w_in", "ffn2_w_out"])]
    def stored(a, n):
        return jnp.swapaxes(a, 1, 2) if n in W_IN_T else a

    wb = {n: stored(W[n], n).astype(BF16) for n in BIG}
    groups[1][1].append("conv_w")
    conv_shard = conv_w.reshape(-1, conv_w.shape[-1])
    flights = {}

    def weights_start(k, dep=None):
        l, names = groups[k]
        shards = [conv_shard if n == "conv_w" else wb[n][l] for n in names]
        if dep is not None:
            shards[0] = _tie(shards[0], dep)
        lands = [lax.empty((N_CHIP,) + s.shape, s.dtype) for s in shards]
        send, recv, lands, token = _gather_start(shards, lands, "weights_start_%d" % k, k >= 2)
        flights[k] = (shards, lands, send, recv)
        return token

    def weights_wait(k, after):
        l, names = groups[k]
        shards, lands, send, recv = flights[k]
        got = _gather_wait(shards, lands, send, recv, after, "weights_wait_%d" % k, k >= 2)
        token = weights_start(k + 2, got[0]) if k + 2 < len(groups) else None
        if k < 2:
            got = _pair_share(got, "weights_share_%d" % k)
        return dict(zip(names, got)), token

    def after_start(a, token):
        return a if token is None else _tie(a, token)

    first_tokens = [weights_start(0), weights_start(1)]

    layers = []
    for l in range(depth):
        L = {f: dict(norm=W[f + "_norm"][l][None]) for f in ("ffn1", "ffn2")}
        ws = jnp.where(jnp.tril(jnp.ones((CHUNK, CHUNK), bool))[None], sgu_w[l], 0.0)
        wax = jnp.concatenate([_block_diag(rg_w_a[l]), _block_diag(rg_w_x[l])], axis=1)
        wpool = _block_diag(pool_w[l])
        L["mix"] = dict(
            conv_b=conv_b[l][None], wax=wax.astype(BF16), wax_t=wax.T.astype(BF16),
            bax=jnp.concatenate([rg_b_a[l].reshape(-1), rg_b_x[l].reshape(-1)])[None], lam=lru_lambda[l][None],
            wpool=wpool.astype(BF16), wpool_t=wpool.T.astype(BF16), pool_scale=pool_scale[l][None],
            sgu_norm=sgu_norm[l][None], ws=ws.astype(BF16), ws_t=jnp.swapaxes(ws, 1, 2).astype(BF16),
            bz=jnp.repeat(sgu_b[l].T, 64, axis=1))
        L["mix_norm"] = mix_norm[l][None]
        layers.append(L)
    for token in first_tokens:
        layers[0]["ffn1"]["norm"] = _tie(layers[0]["ffn1"]["norm"], token)

    xs = x[0]
    saved = []
    for l, L in enumerate(layers):
        F1, F2 = L["ffn1"], L["ffn2"]
        got, token = weights_wait(3 * l, xs)
        F1["wg"], F1["wu"] = _ffn_in_weights(got["ffn1_w_in"])
        F1["norm"] = after_start(F1["norm"], token)
        h, g, u, a = _ffn_in(xs, F1["norm"], F1["wg"], F1["wu"])
        got, token = weights_wait(3 * l + 1, a)
        F1["wout"] = _ffn_out_weights(got["ffn1_w_out"])
        L["w_in"] = got["w_in"].reshape(D_IN, D)
        L["w_out"] = jnp.concatenate([got["w_out"][j] for j in range(N_CHIP)], axis=0)
        if "conv_w" in got:
            conv_full = jnp.concatenate([got["conv_w"][j] for j in range(N_CHIP)], axis=1)
            for ll in range(depth):
                layers[ll]["mix"]["conv_w"] = conv_full.reshape(depth, 4, D_RNN)[ll]
        a = after_start(a, token)
        x1 = _mm_res(a, F1["wout"], xs, 0.5, "ffn_out", tm=512, tn=D)
        s1 = (xs, h, g, u, a)
        hm, p = _mix_in(x1, L["mix_norm"], L["w_in"])
        ycat, hs = _mix_fwd(p, L["mix"])
        x2 = _mm_res(ycat, L["w_out"], x1, 1.0, "mix_out", tm=512, tn=D)
        got, token = weights_wait(3 * l + 2, x2)
        F2["wg"], F2["wu"] = _ffn_in_weights(got["ffn2_w_in"])
        F2["wout"] = _ffn_out_weights(got["ffn2_w_out"])
        F2["norm"] = after_start(F2["norm"], token)
        h2, g2, u2, a2 = _ffn_in(x2, F2["norm"], F2["wg"], F2["wu"])
        x3 = _mm_res(a2, F2["wout"], x2, 0.5, "ffn_out", tm=512, tn=D)
        saved.append((s1, (x1, hm, p, ycat, hs), (x2, h2, g2, u2, a2)))
        xs = x3

    dx, dxb, d_final, loss_part = _final(xs, loss_target[0], final_norm[None], 0.5)

    G = {n: [None] * depth for n in SMALL if n != "final_norm"}
    lands = {n: lax.empty((N_DEV,) + LAND_SHAPES[n], BF16) for n in BIG}
    in_flight = []

    def send_grads(l, windows, tag):
        names = list(windows)
        send, recv, thru, token = _exchange_start(l, [windows[n] for n in names], [lands[n] for n in names],
                                                  "grads_start_" + tag)
        lands.update(zip(names, thru))
        in_flight.append((l, names, [windows[n] for n in names], send, recv, "grads_wait_" + tag))
        return token

    def ffn_bwd(dx, dxb, F, s, f, l, pending, send_now):
        xin, h, g, u, a = s
        dg, du = _ffn_mid_bwd(dxb, F["wout"], g, u)
        dwout = _mm_tn(a, dxb, 1.0, "ffn_dwout")
        dwg = _mm_tn(dg, h, 1.0, "ffn_dwg")
        dwu = _mm_tn(du, h, 1.0, "ffn_dwu")
        pending[f + "_w_in"], pending[f + "_w_out"] = _ffn_grad_windows(dwg, dwu, dwout)
        if send_now:
            dg = _tie(dg, send_grads(l, pending, "l%d_%s" % (l, f)))
        dx, dxb, dn = _dh_rms_bwd([(dg, F["wg"]), (du, F["wu"])], xin, F["norm"], dx,
                                  1.0 if f == "ffn2" else 0.5, "ffn_dh")
        G[f + "_norm"][l] = dn[0]
        return dx, dxb

    for l in reversed(range(depth)):
        L = layers[l]
        s1, (x1, hm, p, ycat, hs), s2 = saved[l]
        pending = {}
        dx, dxb = ffn_bwd(dx, dxb, L["ffn2"], s2, "ffn2", l, pending, l == 0)
        if l == 0:
            pending = {}
        dycat = _mm_nt(dxb, L["w_out"], "mix_dy", tm=512, tn=D)
        dwo = _mm_tn(ycat, dxb, 1.0, "mix_dwout")
        mg = _mix_bwd(dycat, p, hs, L["mix"])
        dwin = _mm_tn(mg["dp"], hm, 1.0, "mix_dwin")
        pending["w_in"], pending["w_out"] = _mix_grad_windows(dwin, dwo)
        if l == 0:
            dp = _tie(mg["dp"], send_grads(l, pending, "l0_mix"))
            pending = {}
        else:
            dp = mg["dp"]
        dx, dxb, dn = _dh_rms_bwd([(dp, L["w_in"])], x1, L["mix_norm"], dx, 0.5, "mix_dh")
        G["mix_norm"][l] = dn[0]
        G["conv_w"][l], G["conv_b"][l] = mg["conv_w"], mg["conv_b"][0]
        G["rg_w_a"][l] = _diag_blocks(mg["wax"][:, :D_RNN], 8, 64)
        G["rg_w_x"][l] = _diag_blocks(mg["wax"][:, D_RNN:], 8, 64)
        G["rg_b_a"][l] = mg["bax"][0, :D_RNN].reshape(8, 64)
        G["rg_b_x"][l] = mg["bax"][0, D_RNN:].reshape(8, 64)
        G["lru_lambda"][l] = mg["lam"][0]
        G["pool_w"][l] = _diag_blocks(mg["wpool"], 4, 64)
        G["pool_scale"][l], G["sgu_norm"][l] = mg["pool_scale"][0], mg["sgu_norm"][0]
        G["sgu_w"][l] = mg["ws"]
        G["sgu_b"][l] = mg["bz"].reshape(CHUNK, 4, 64).sum(-1).T
        dx, dxb = ffn_bwd(dx, dxb, L["ffn1"], s1, "ffn1", l, pending, l == 0)
        if l > 0:
            dxb = _tie(dxb, send_grads(l, pending, "l%d" % l))
    grad_x = dx[None]
    G = {n: jnp.stack(v) for n, v in G.items()}
    G["final_norm"] = d_final[0]

    for l, names, windows, send, recv, tag in in_flight:
        lands.update(zip(names, _exchange_wait(l, windows, [lands[n] for n in names], send, recv, dx, tag)))
    both = [_sum_share(lands[n], "sum_share_" + n) for n in BIG]
    grads = dict(zip(BIG, both))

    small_sizes = [int(np.prod(G[n].shape)) for n in SMALL]
    srows = _round_up(sum(small_sizes) + 1, N_DEV * 8 * LANES) // (N_DEV * LANES)
    sflat = _flat_rows([G[n] for n in SMALL] + [loss_part[0, :1]], N_DEV * srows)
    sgot = _all_to_all(sflat.reshape(N_DEV, srows, LANES), "exchange_small_grads")
    sall = _all_gather8(_sum8(sgot, "sum_small_grads"), "share_small_grads").reshape(-1)
    off = 0
    for n, size in zip(SMALL, small_sizes):
        grads[n] = sall[off:off + size].reshape(G[n].shape)
        off += size
    loss = sall[off]
    grads["conv_w"] = lax.dynamic_slice_in_dim(grads["conv_w"], chip * conv_w.shape[2], conv_w.shape[2], axis=2)

    delta, new_m, new_v = {}, {}, {}
    for n in BIG:
        shp = grads[n].shape
        two_d = (shp[0] * shp[1], shp[2])
        outs = _adamw(stored(W[n], n).reshape(two_d), grads[n].reshape(two_d), stored(M[n], n).reshape(two_d),
                      stored(V[n], n).reshape(two_d), "adamw_" + n)
        delta[n], new_m[n], new_v[n] = (stored(o.reshape(shp), n) for o in outs)
        grads[n] = stored(grads[n], n)
    arows = _round_up(sum(int(np.prod(W[n].shape)) for n in SMALL), 8 * LANES) // LANES
    outs = _adamw(*(_flat_rows([src[n] for n in SMALL], arows) for src in (W, grads, M, V)), "adamw_small")
    outs = [o.reshape(-1) for o in outs]
    off = 0
    for n in SMALL:
        size = int(np.prod(W[n].shape))
        delta[n], new_m[n], new_v[n] = (o[off:off + size].reshape(W[n].shape) for o in outs)
        off += size

    return (loss, grad_x, *[grads[n] for n in WEIGHTS], *[delta[n] for n in WEIGHTS],
            *[new_m[n] for n in WEIGHTS], *[new_v[n] for n in WEIGHTS])
```

```python
import functools
import math

import jax
import jax.numpy as jnp
import numpy as np
from jax import lax
from jax.experimental import pallas as pl
from jax.experimental.pallas import tpu as pltpu

F32 = jnp.float32
BF16 = jnp.bfloat16
MESH = pl.DeviceIdType.MESH

D = 1024
D_RNN = 512
D_POOL = 256
D_SGU = 256
D_IN = 1792
D_FF = 2752
D_FFP = 2816
N_CHIP = 4
FF_SHARD = D_FF // 2
FF_HALF = D_FFP // 2
FF_ROWS = D_FF // N_CHIP
CHUNK = 128
HALO = 16
EPS = 1e-6
LRU_C = 8.0
N_DEV = 8
LANES = 1024
VMEM_LIMIT = 56 * 1024 * 1024

ADAM_LR, ADAM_B1, ADAM_B2, ADAM_EPS, ADAM_WD, ADAM_STEP = 0.001, 0.9, 0.999, 1e-08, 0.01, 10

BIG = ("ffn1_w_in", "ffn1_w_out", "w_in", "w_out", "ffn2_w_in", "ffn2_w_out")
SMALL = ("ffn1_norm", "mix_norm", "conv_w", "conv_b", "rg_w_a", "rg_b_a", "rg_w_x", "rg_b_x", "lru_lambda",
         "pool_w", "pool_scale", "sgu_norm", "sgu_w", "sgu_b", "ffn2_norm", "final_norm")
WEIGHTS = ("ffn1_norm", "ffn1_w_in", "ffn1_w_out", "mix_norm", "w_in", "conv_w", "conv_b", "rg_w_a", "rg_b_a",
           "rg_w_x", "rg_b_x", "lru_lambda", "pool_w", "pool_scale", "sgu_norm", "sgu_w", "sgu_b", "w_out",
           "ffn2_norm", "ffn2_w_in", "ffn2_w_out", "final_norm")


def _params(*sem):
    return pltpu.CompilerParams(dimension_semantics=sem, vmem_limit_bytes=VMEM_LIMIT)


def _gelu(x):
    c = math.sqrt(2.0 / math.pi)
    t = jnp.tanh(c * (x + 0.044715 * (x * x * x)))
    return 0.5 * x * (1.0 + t)


def _gelu_and_grad(x):
    c = math.sqrt(2.0 / math.pi)
    x2 = x * x
    t = jnp.tanh(c * (x + 0.044715 * (x2 * x)))
    g = 0.5 * x * (1.0 + t)
    dg = 0.5 * (1.0 + t) + 0.5 * x * (1.0 - t * t) * (c * (1.0 + 3.0 * 0.044715 * x2))
    return g, dg


def _sigmoid(x):
    return 0.5 * jnp.tanh(0.5 * x) + 0.5


def _dot(a, b):
    return jnp.dot(a, b, preferred_element_type=F32)


def _dot_tn(a, b):
    return lax.dot_general(a, b, (((0,), (0,)), ((), ())), preferred_element_type=F32)


def _dot_nt(a, b):
    return lax.dot_general(a, b, (((1,), (1,)), ((), ())), preferred_element_type=F32)


def _tile(n, limit):
    best = 128
    for t in range(128, min(n, limit) + 1, 128):
        if n % t == 0:
            best = t
    assert n % best == 0, (n, limit)
    return best


def _mm_res(a, b, res, scale, name, tm=1024, tn=512):
    M, K = a.shape
    N = b.shape[1]
    tm, tn = min(tm, M), _tile(N, tn)

    def body(a_ref, b_ref, r_ref, o_ref):
        o_ref[...] = r_ref[...] + scale * _dot(a_ref[...], b_ref[...])

    return pl.pallas_call(
        body, name=name, grid=(M // tm, N // tn),
        in_specs=[pl.BlockSpec((tm, K), lambda i, j: (i, 0)), pl.BlockSpec((K, tn), lambda i, j: (0, j)),
                  pl.BlockSpec((tm, tn), lambda i, j: (i, j))],
        out_specs=pl.BlockSpec((tm, tn), lambda i, j: (i, j)),
        out_shape=jax.ShapeDtypeStruct((M, N), F32),
        compiler_params=_params("parallel", "parallel"),
    )(a, b, res)


def _mm_nt(a, b, name, tm=1024, tn=512):
    M, K = a.shape
    N = b.shape[0]
    tm, tn = min(tm, M), _tile(N, tn)

    def body(a_ref, b_ref, o_ref):
        o_ref[...] = _dot_nt(a_ref[...], b_ref[...])

    return pl.pallas_call(
        body, name=name, grid=(M // tm, N // tn),
        in_specs=[pl.BlockSpec((tm, K), lambda i, j: (i, 0)), pl.BlockSpec((tn, K), lambda i, j: (j, 0))],
        out_specs=pl.BlockSpec((tm, tn), lambda i, j: (i, j)),
        out_shape=jax.ShapeDtypeStruct((M, N), F32),
        compiler_params=_params("parallel", "parallel"),
    )(a, b)


def _dh_rms_bwd(pairs, x, g, dres, copy_scale, name, tm=512):
    T = x.shape[0]
    tm = min(tm, T)
    n = len(pairs)

    def body(*refs):
        ab = refs[:2 * n]
        x_ref, g_ref, dres_ref, dx_ref, dxb_ref, dg_ref = refs[2 * n:]
        dy = _dot(ab[0][...], ab[1][...])
        for k in range(1, n):
            dy = dy + _dot(ab[2 * k][...], ab[2 * k + 1][...])
        xv = x_ref[...]
        r = lax.rsqrt(jnp.mean(xv * xv, axis=-1, keepdims=True) + EPS)
        xhat = xv * r
        dxhat = dy * g_ref[...]
        dx = dres_ref[...] + r * (dxhat - xhat * jnp.mean(dxhat * xhat, axis=-1, keepdims=True))
        dx_ref[...] = dx
        dxb_ref[...] = (copy_scale * dx).astype(BF16)

        @pl.when(pl.program_id(0) == 0)
        def _():
            dg_ref[...] = jnp.zeros_like(dg_ref)

        dg_ref[...] += jnp.sum(dy * xhat, axis=0, keepdims=True)

    row = pl.BlockSpec((tm, D), lambda i: (i, 0))
    vec = pl.BlockSpec((1, D), lambda i: (0, 0))
    in_specs, operands = [], []
    for a, b in pairs:
        in_specs += [pl.BlockSpec((tm, a.shape[1]), lambda i: (i, 0)),
                     pl.BlockSpec(b.shape, lambda i: (0, 0), pipeline_mode=pl.Buffered(1))]
        operands += [a, b]
    return pl.pallas_call(
        body, name=name, grid=(T // tm,),
        in_specs=in_specs + [row, vec, row], out_specs=[row, row, vec],
        out_shape=[jax.ShapeDtypeStruct((T, D), F32), jax.ShapeDtypeStruct((T, D), BF16),
                   jax.ShapeDtypeStruct((1, D), F32)],
        compiler_params=_params("arbitrary"),
    )(*operands, x, g, dres)


def _mm_tn(a, b, scale, name, tm=1792, tn=1792, tk=2048):
    T, M = a.shape
    N = b.shape[1]
    tm, tn, tk = _tile(M, tm), _tile(N, tn), min(tk, T)
    nk = T // tk

    def body(a_ref, b_ref, o_ref, acc_ref):
        k = pl.program_id(2)

        @pl.when(k == 0)
        def _():
            acc_ref[...] = jnp.zeros_like(acc_ref)

        acc_ref[...] += _dot_tn(a_ref[...], b_ref[...])

        @pl.when(k == nk - 1)
        def _():
            o_ref[...] = (scale * acc_ref[...]).astype(BF16)

    return pl.pallas_call(
        body, name=name, grid=(M // tm, N // tn, nk),
        in_specs=[pl.BlockSpec((tk, tm), lambda i, j, k: (k, i)), pl.BlockSpec((tk, tn), lambda i, j, k: (k, j))],
        out_specs=pl.BlockSpec((tm, tn), lambda i, j, k: (i, j)),
        out_shape=jax.ShapeDtypeStruct((M, N), BF16),
        scratch_shapes=[pltpu.VMEM((tm, tn), F32)],
        compiler_params=_params("parallel", "parallel", "arbitrary"),
    )(a, b)


def _rms_rows(x_ref, gain_ref):
    xv = x_ref[...]
    r = lax.rsqrt(jnp.mean(xv * xv, axis=-1, keepdims=True) + EPS)
    return (xv * r * gain_ref[...]).astype(BF16)


def _ffn_in(x, gain, wg_t, wu_t, tm=512, tn=FF_HALF):
    T = x.shape[0]
    tm = min(tm, T)

    def body(x_ref, gain_ref, wg_ref, wu_ref, h_ref, g_ref, u_ref, a_ref):
        hv = _rms_rows(x_ref, gain_ref)

        @pl.when(pl.program_id(0) == 0)
        def _():
            h_ref[...] = hv

        g = _dot_nt(hv, wg_ref[...])
        u = _dot_nt(hv, wu_ref[...])
        g_ref[...] = g.astype(BF16)
        u_ref[...] = u.astype(BF16)
        a_ref[...] = (g * _sigmoid(g) * u).astype(BF16)

    row = pl.BlockSpec((tm, D), lambda j, i: (i, 0))
    last = T // tm - 1
    h_spec = pl.BlockSpec((tm, D), lambda j, i: (jnp.where(j == 0, i, last), 0))
    wspec = pl.BlockSpec((tn, D), lambda j, i: (j, 0))
    ospec = pl.BlockSpec((tm, tn), lambda j, i: (i, j))
    oshape = jax.ShapeDtypeStruct((T, D_FFP), BF16)
    return pl.pallas_call(
        body, name="ffn_in", grid=(D_FFP // tn, T // tm),
        in_specs=[row, pl.BlockSpec((1, D), lambda j, i: (0, 0)), wspec, wspec],
        out_specs=[h_spec, ospec, ospec, ospec],
        out_shape=[jax.ShapeDtypeStruct((T, D), BF16), oshape, oshape, oshape],
        compiler_params=_params("arbitrary", "arbitrary"),
    )(x, gain, wg_t, wu_t)


def _mix_in(x, gain, w_in_t, tm=512):
    T = x.shape[0]
    tm = min(tm, T)

    def body(x_ref, gain_ref, w_ref, h_ref, p_ref):
        hv = _rms_rows(x_ref, gain_ref)
        h_ref[...] = hv
        p_ref[...] = _dot_nt(hv, w_ref[...])

    row = pl.BlockSpec((tm, D), lambda i: (i, 0))
    return pl.pallas_call(
        body, name="mix_in", grid=(T // tm,),
        in_specs=[row, pl.BlockSpec((1, D), lambda i: (0, 0)), pl.BlockSpec((D_IN, D), lambda i: (0, 0))],
        out_specs=[row, pl.BlockSpec((tm, D_IN), lambda i: (i, 0))],
        out_shape=[jax.ShapeDtypeStruct((T, D), BF16), jax.ShapeDtypeStruct((T, D_IN), F32)],
        compiler_params=_params("parallel"),
    )(x, gain, w_in_t)


def _ffn_mid_bwd(dyh, wout, g, u, tm=512, tn=FF_HALF):
    T = dyh.shape[0]
    tm = min(tm, T)

    def body(dy_ref, w_ref, g_ref, u_ref, dg_ref, du_ref):
        da = _dot_nt(dy_ref[...], w_ref[...])
        g = g_ref[...].astype(F32)
        s = _sigmoid(g)
        gs = g * s
        du_ref[...] = (da * gs).astype(BF16)
        dg_ref[...] = ((da * u_ref[...].astype(F32)) * (s + gs - gs * s)).astype(BF16)

    ospec = pl.BlockSpec((tm, tn), lambda j, i: (i, j))
    oshape = jax.ShapeDtypeStruct((T, D_FFP), BF16)
    return pl.pallas_call(
        body, name="ffn_mid_bwd", grid=(D_FFP // tn, T // tm),
        in_specs=[pl.BlockSpec((tm, D), lambda j, i: (i, 0)), pl.BlockSpec((tn, D), lambda j, i: (j, 0)),
                  ospec, ospec],
        out_specs=[ospec, ospec], out_shape=[oshape, oshape],
        compiler_params=_params("parallel", "parallel"),
    )(dyh, wout, g, u)


def _final(x, tgt, gf, copy_scale, tm=512):
    T = x.shape[0]
    tm = min(tm, T)

    def body(x_ref, t_ref, g_ref, dx_ref, dxb_ref, dg_ref, loss_ref):
        xv = x_ref[...]
        r = lax.rsqrt(jnp.mean(xv * xv, axis=-1, keepdims=True) + EPS)
        xhat = xv * r
        err = xhat * g_ref[...] - t_ref[...]
        dy = err * (1.0 / D)
        dxhat = dy * g_ref[...]
        dx = r * (dxhat - xhat * jnp.mean(dxhat * xhat, axis=-1, keepdims=True))
        dx_ref[...] = dx
        dxb_ref[...] = (copy_scale * dx).astype(BF16)

        @pl.when(pl.program_id(0) == 0)
        def _():
            dg_ref[...] = jnp.zeros_like(dg_ref)
            loss_ref[...] = jnp.zeros_like(loss_ref)

        dg_ref[...] += jnp.sum(dy * xhat, axis=0, keepdims=True)
        loss_ref[...] += (0.5 / D) * jnp.sum(err * err)

    row = pl.BlockSpec((tm, D), lambda i: (i, 0))
    vec = pl.BlockSpec((1, D), lambda i: (0, 0))
    return pl.pallas_call(
        body, name="final_loss", grid=(T // tm,),
        in_specs=[row, row, vec],
        out_specs=[row, row, vec, pl.BlockSpec((1, 128), lambda i: (0, 0))],
        out_shape=[jax.ShapeDtypeStruct((T, D), F32), jax.ShapeDtypeStruct((T, D), BF16),
                   jax.ShapeDtypeStruct((1, D), F32), jax.ShapeDtypeStruct((1, 128), F32)],
        compiler_params=_params("arbitrary"),
    )(x, tgt, gf)


def _mix_block(T, limit):
    return min(limit, T // 2)


def _rows(tb, width):
    return lax.broadcasted_iota(jnp.int32, (tb, width), 0)


def _rglru_gates(xc, wax_ref, bax_ref, lam_ref):
    pre = _dot(xc.astype(BF16), wax_ref[...]) + bax_ref[...]
    r = _sigmoid(pre[:, :D_RNN])
    ig = _sigmoid(pre[:, D_RNN:])
    z = -lam_ref[...]
    sp = jnp.maximum(z, 0.0) + jnp.log(1.0 + jnp.exp(-jnp.abs(z)))
    log_a = (-LRU_C) * r * sp
    a = jnp.exp(log_a)
    mult = jnp.sqrt(-jnp.tanh(log_a) * (1.0 + a * a))
    return r, ig, sp, a, mult


def _conv(xa_ext, cw_ref, cb_ref):
    y = cb_ref[...] + cw_ref[3:4, :] * xa_ext
    for k in range(1, 4):
        y = y + cw_ref[3 - k:4 - k, :] * pltpu.roll(xa_ext, k, 0)
    return y[HALO:]


def _pool_window_lanes():
    lane = lax.broadcasted_iota(jnp.int32, (1, D_POOL), 1)
    return jnp.where(lane < 64, 2, jnp.where(lane < 128, 4, jnp.where(lane < 192, 8, 16)))


def _pool_select(s2, s4, s8, s16):
    lane = lax.broadcasted_iota(jnp.int32, s2.shape, 1)
    return jnp.where(lane < 64, s2, jnp.where(lane < 128, s4, jnp.where(lane < 192, s8, s16)))


def _pool_diff(xp_ext, t0, tb):
    s2 = xp_ext + pltpu.roll(xp_ext, 1, 0)
    s4 = s2 + pltpu.roll(s2, 2, 0)
    s8 = s4 + pltpu.roll(s4, 4, 0)
    s16 = s8 + pltpu.roll(s8, 8, 0)
    sel = _pool_select(s2, s4, s8, s16)[HALO:]
    cnt = jnp.minimum(t0 + _rows(tb, D_POOL) + 1, _pool_window_lanes()).astype(F32)
    return sel / cnt - xp_ext[HALO:], cnt


def _head_masks():
    lane = lax.broadcasted_iota(jnp.int32, (1, D_SGU), 1)
    return [((lane >= 64 * h) & (lane < 64 * (h + 1))).astype(F32) for h in range(4)]


def _sgu_mix(w_ref, vch, masks):
    z = masks[0] * _dot(w_ref[0], vch)
    for h in range(1, 4):
        z = z + masks[h] * _dot(w_ref[h], vch)
    return z


def _mix_fwd(p, prm):
    T = p.shape[0]
    tb = _mix_block(T, 512)
    nb = T // tb

    def body(p_ref, xah_ref, xph_ref, cw_ref, cb_ref, wax_ref, bax_ref, lam_ref, wp_ref, ps_ref, sgn_ref,
             ws_ref, bz_ref, y_ref, hs_ref, carry_ref):
        i = pl.program_id(0)
        keep = (i > 0).astype(F32)

        @pl.when(i == 0)
        def _():
            carry_ref[...] = jnp.zeros_like(carry_ref)

        xa_ext = jnp.concatenate([xah_ref[...] * keep, p_ref[:, 512:1024]], axis=0)
        xc = _conv(xa_ext, cw_ref, cb_ref)
        r, ig, sp, a, mult = _rglru_gates(xc, wax_ref, bax_ref, lam_ref)
        bv = mult * (ig * xc)
        row = _rows(tb, D_RNN)
        s = 1
        while s < tb:
            m = row >= s
            bv = jnp.where(m, a * pltpu.roll(bv, s, 0) + bv, bv)
            a = jnp.where(m, a * pltpu.roll(a, s, 0), a)
            s *= 2
        h = bv + a * carry_ref[0:1, :]
        hs_ref[...] = h
        last = jnp.sum(jnp.where(_rows(8, D_RNN) == 7, hs_ref[tb - 8:tb, :], 0.0), axis=0, keepdims=True)
        carry_ref[...] = jnp.broadcast_to(last, carry_ref.shape)
        y_ref[:, 0:512] = (_gelu(p_ref[:, 0:512]) * h).astype(BF16)

        xp_ext = jnp.concatenate([xph_ref[...] * keep, p_ref[:, 1024:1280]], axis=0)
        d, _ = _pool_diff(xp_ext, i * tb, tb)
        y_ref[:, 512:768] = (_dot(d.astype(BF16), wp_ref[...]) * ps_ref[...]).astype(BF16)

        ug = _gelu(p_ref[:, 1280:1536])
        vg = _gelu(p_ref[:, 1536:1792])
        rv = lax.rsqrt(jnp.mean(vg * vg, axis=-1, keepdims=True) + EPS)
        vn = (vg * rv * sgn_ref[...]).astype(BF16)
        masks = _head_masks()
        for ci in range(tb // CHUNK):
            sl = slice(ci * CHUNK, (ci + 1) * CHUNK)
            z = _sgu_mix(ws_ref, vn[sl], masks) + bz_ref[...]
            y_ref[sl, 768:1024] = (ug[sl] * z).astype(BF16)

    hb = tb // HALO

    def halo(i):
        return jnp.maximum(i * hb - 1, 0)

    def full(shape):
        return pl.BlockSpec(shape, lambda i: (0,) * len(shape))

    return pl.pallas_call(
        body, name="mix_fwd", grid=(nb,),
        in_specs=[pl.BlockSpec((tb, D_IN), lambda i: (i, 0)),
                  pl.BlockSpec((HALO, D_RNN), lambda i: (halo(i), 1)),
                  pl.BlockSpec((HALO, D_POOL), lambda i: (halo(i), 4)),
                  full((4, D_RNN)), full((1, D_RNN)), full((D_RNN, 2 * D_RNN)), full((1, 2 * D_RNN)),
                  full((1, D_RNN)), full((D_POOL, D_POOL)), full((1, D_POOL)), full((1, D_SGU)),
                  full((4, CHUNK, CHUNK)), full((CHUNK, D_SGU))],
        out_specs=[pl.BlockSpec((tb, D), lambda i: (i, 0)), pl.BlockSpec((tb, D_RNN), lambda i: (i, 0))],
        out_shape=[jax.ShapeDtypeStruct((T, D), BF16), jax.ShapeDtypeStruct((T, D_RNN), F32)],
        scratch_shapes=[pltpu.VMEM((8, D_RNN), F32)],
        compiler_params=_params("arbitrary"),
    )(p, p, p, prm["conv_w"], prm["conv_b"], prm["wax"], prm["bax"], prm["lam"], prm["wpool"], prm["pool_scale"],
      prm["sgu_norm"], prm["ws"], prm["bz"])


def _mix_bwd(dy, p, hs, prm):
    T = p.shape[0]
    tb = _mix_block(T, 256)
    nb = T // tb
    hb = tb // HALO

    def body(dy_ref, p_ref, xah_ref, xph_ref, hs_ref, hsh_ref, cw_ref, cb_ref, wax_ref, waxt_ref, bax_ref,
             lam_ref, wp_ref, wpt_ref, ps_ref, sgn_ref, ws_ref, wst_ref, bz_ref,
             dp_ref, dcw_ref, dcb_ref, dwax_ref, dbax_ref, dlam_ref, dwp_ref, dps_ref, dsgn_ref, dws_ref,
             dbz_ref, gcarry_ref, xcfut_ref, mfut_ref):
        i = pl.program_id(0)
        bi = nb - 1 - i
        keep = (bi > 0).astype(F32)

        @pl.when(i == 0)
        def _():
            for ref in (dcw_ref, dcb_ref, dwax_ref, dbax_ref, dlam_ref, dwp_ref, dps_ref, dsgn_ref, dws_ref,
                        dbz_ref, gcarry_ref, xcfut_ref, mfut_ref):
                ref[...] = jnp.zeros_like(ref)

        xa_ext = jnp.concatenate([xah_ref[...] * keep, p_ref[:, 512:1024]], axis=0)
        xc = _conv(xa_ext, cw_ref, cb_ref)
        r, ig, sp, a, mult = _rglru_gates(xc, wax_ref, bax_ref, lam_ref)
        gg, dgg = _gelu_and_grad(p_ref[:, 0:512])
        dya = dy_ref[:, 0:512]
        dp_ref[:, 0:512] = (dya * hs_ref[...] * dgg).astype(BF16)
        row = _rows(tb, D_RNN)
        g = dya * gg + jnp.where(row == tb - 1, gcarry_ref[0:1, :], 0.0)
        al = pltpu.roll(a, tb - 1, 0)
        s = 1
        while s < tb:
            m = row < tb - s
            g = jnp.where(m, al * pltpu.roll(g, tb - s, 0) + g, g)
            al = jnp.where(m, al * pltpu.roll(al, tb - s, 0), al)
            s *= 2
        first = jnp.sum(jnp.where(_rows(8, D_RNN) == 0, (a * g)[0:8], 0.0), axis=0, keepdims=True)
        gcarry_ref[...] = jnp.broadcast_to(first, gcarry_ref.shape)
        hs_ext = jnp.concatenate([hsh_ref[...] * keep, hs_ref[...]], axis=0)
        h_prev = pltpu.roll(hs_ext, 1, 0)[HALO:]
        ix = ig * xc
        dlog_a = g * h_prev * a - (g * ix) * (a * a / mult)
        dlam_ref[...] += jnp.sum(dlog_a * r, axis=0, keepdims=True) * (LRU_C * _sigmoid(-lam_ref[...]))
        dpre_r = dlog_a * ((-LRU_C) * sp) * (r * (1.0 - r))
        dpre_i = (g * mult * xc) * (ig * (1.0 - ig))
        dpre = jnp.concatenate([dpre_r, dpre_i], axis=1)
        dbax_ref[...] += jnp.sum(dpre, axis=0, keepdims=True)
        dpre_b = dpre.astype(BF16)
        dwax_ref[...] += _dot_tn(xc.astype(BF16), dpre_b)
        dxc = g * mult * ig + _dot(dpre_b, waxt_ref[...])
        dcb_ref[...] += jnp.sum(dxc, axis=0, keepdims=True)
        for k in range(4):
            xs = xa_ext[HALO:] if k == 3 else pltpu.roll(xa_ext, 3 - k, 0)[HALO:]
            dcw_ref[k:k + 1, :] += jnp.sum(dxc * xs, axis=0, keepdims=True)
        dxc_ext = jnp.concatenate([dxc, xcfut_ref[...]], axis=0)
        n = tb + HALO
        dxa = cw_ref[3:4, :] * dxc_ext
        for k in range(1, 4):
            dxa = dxa + cw_ref[3 - k:4 - k, :] * pltpu.roll(dxc_ext, n - k, 0)
        dp_ref[:, 512:1024] = dxa[:tb].astype(BF16)
        xcfut_ref[...] = dxc[0:HALO]

        xp_ext = jnp.concatenate([xph_ref[...] * keep, p_ref[:, 1024:1280]], axis=0)
        d, cnt = _pool_diff(xp_ext, bi * tb, tb)
        db = d.astype(BF16)
        dyb = dy_ref[:, 512:768]
        dps_ref[...] += jnp.sum(dyb * _dot(db, wp_ref[...]), axis=0, keepdims=True)
        dq = (dyb * ps_ref[...]).astype(BF16)
        dwp_ref[...] += _dot_tn(db, dq)
        dd = _dot(dq, wpt_ref[...])
        mm = dd / cnt
        m_ext = jnp.concatenate([mm, mfut_ref[...]], axis=0)
        f2 = m_ext + pltpu.roll(m_ext, n - 1, 0)
        f4 = f2 + pltpu.roll(f2, n - 2, 0)
        f8 = f4 + pltpu.roll(f4, n - 4, 0)
        f16 = f8 + pltpu.roll(f8, n - 8, 0)
        dp_ref[:, 1024:1280] = (_pool_select(f2, f4, f8, f16)[:tb] - dd).astype(BF16)
        mfut_ref[...] = mm[0:HALO]

        ug, dug = _gelu_and_grad(p_ref[:, 1280:1536])
        vg, dvg = _gelu_and_grad(p_ref[:, 1536:1792])
        rv = lax.rsqrt(jnp.mean(vg * vg, axis=-1, keepdims=True) + EPS)
        vhat = vg * rv
        vn = (vhat * sgn_ref[...]).astype(BF16)
        dyc = dy_ref[:, 768:1024]
        masks = _head_masks()
        dz = dyc * ug
        dzb = dz.astype(BF16)
        dvn_parts = []
        for ci in range(tb // CHUNK):
            sl = slice(ci * CHUNK, (ci + 1) * CHUNK)
            z = _sgu_mix(ws_ref, vn[sl], masks) + bz_ref[...]
            dp_ref[sl, 1280:1536] = (dyc[sl] * z * dug[sl]).astype(BF16)
            dbz_ref[...] += dz[sl]
            for h in range(4):
                dws_ref[h] += _dot_nt((dz[sl] * masks[h]).astype(BF16), vn[sl])
            dvn_parts.append(_sgu_mix(wst_ref, dzb[sl], masks))
        dvn = jnp.concatenate(dvn_parts, axis=0)
        dsgn_ref[...] += jnp.sum(dvn * vhat, axis=0, keepdims=True)
        dvhat = dvn * sgn_ref[...]
        dvg_in = rv * (dvhat - vhat * jnp.mean(dvhat * vhat, axis=-1, keepdims=True))
        dp_ref[:, 1536:1792] = (dvg_in * dvg).astype(BF16)

        @pl.when(i == nb - 1)
        def _():
            tril = (lax.broadcasted_iota(jnp.int32, (CHUNK, CHUNK), 0)
                    >= lax.broadcasted_iota(jnp.int32, (CHUNK, CHUNK), 1)).astype(F32)
            for h in range(4):
                dws_ref[h] = dws_ref[h] * tril

    def blk(i):
        return nb - 1 - i

    def halo(i):
        return jnp.maximum(blk(i) * hb - 1, 0)

    def full(shape):
        return pl.BlockSpec(shape, lambda i: (0,) * len(shape))

    small_shapes = [(4, D_RNN), (1, D_RNN), (D_RNN, 2 * D_RNN), (1, 2 * D_RNN), (1, D_RNN), (D_POOL, D_POOL),
                    (1, D_POOL), (1, D_SGU), (4, CHUNK, CHUNK), (CHUNK, D_SGU)]
    outs = pl.pallas_call(
        body, name="mix_bwd", grid=(nb,),
        in_specs=[pl.BlockSpec((tb, D), lambda i: (blk(i), 0)),
                  pl.BlockSpec((tb, D_IN), lambda i: (blk(i), 0)),
                  pl.BlockSpec((HALO, D_RNN), lambda i: (halo(i), 1)),
                  pl.BlockSpec((HALO, D_POOL), lambda i: (halo(i), 4)),
                  pl.BlockSpec((tb, D_RNN), lambda i: (blk(i), 0)),
                  pl.BlockSpec((HALO, D_RNN), lambda i: (halo(i), 0)),
                  full((4, D_RNN)), full((1, D_RNN)), full((D_RNN, 2 * D_RNN)), full((2 * D_RNN, D_RNN)),
                  full((1, 2 * D_RNN)), full((1, D_RNN)), full((D_POOL, D_POOL)), full((D_POOL, D_POOL)),
                  full((1, D_POOL)), full((1, D_SGU)), full((4, CHUNK, CHUNK)), full((4, CHUNK, CHUNK)),
                  full((CHUNK, D_SGU))],
        out_specs=[pl.BlockSpec((tb, D_IN), lambda i: (blk(i), 0))] + [full(s) for s in small_shapes],
        out_shape=[jax.ShapeDtypeStruct((T, D_IN), BF16)] + [jax.ShapeDtypeStruct(s, F32) for s in small_shapes],
        scratch_shapes=[pltpu.VMEM((8, D_RNN), F32), pltpu.VMEM((HALO, D_RNN), F32),
                        pltpu.VMEM((HALO, D_POOL), F32)],
        compiler_params=_params("arbitrary"),
    )(dy, p, p, p, hs, hs, prm["conv_w"], prm["conv_b"], prm["wax"], prm["wax_t"], prm["bax"], prm["lam"],
      prm["wpool"], prm["wpool_t"], prm["pool_scale"], prm["sgu_norm"], prm["ws"], prm["ws_t"], prm["bz"])
    names = ("dp", "conv_w", "conv_b", "wax", "bax", "lam", "wpool", "pool_scale", "sgu_norm", "ws", "bz")
    return dict(zip(names, outs))


ANY = pl.BlockSpec(memory_space=pl.ANY)


def _place():
    x, y, c = lax.axis_index("x"), lax.axis_index("y"), lax.axis_index("c")
    return x, y, c


def _all_to_all(xs, name):
    def body(in_ref, out_ref, send_sems, recv_sems, local_sem):
        x, y, c = _place()
        me = 4 * x + 2 * y + c
        mine = pltpu.make_async_copy(in_ref.at[me], out_ref.at[me], local_sem)
        mine.start()
        copies = []
        for rel in range(1, N_DEV):
            tx = 1 - x if rel & 4 else x
            ty = 1 - y if rel & 2 else y
            tc = 1 - c if rel & 1 else c
            cp = pltpu.make_async_remote_copy(
                src_ref=in_ref.at[4 * tx + 2 * ty + tc], dst_ref=out_ref.at[me],
                send_sem=send_sems.at[rel - 1], recv_sem=recv_sems.at[rel - 1],
                device_id=(tx, ty, tc), device_id_type=MESH)
            cp.start()
            copies.append(cp)
        for cp in copies:
            cp.wait()
        mine.wait()

    return pl.pallas_call(
        body, name=name, in_specs=[ANY], out_specs=ANY,
        out_shape=jax.ShapeDtypeStruct(xs.shape, xs.dtype),
        scratch_shapes=[pltpu.SemaphoreType.DMA((N_DEV - 1,)), pltpu.SemaphoreType.DMA((N_DEV - 1,)),
                        pltpu.SemaphoreType.DMA],
    )(xs)


def _all_gather8(xs, name):
    def body(x_ref, out_ref, send_sems, recv_sems, local_sem):
        x, y, c = _place()
        me, sibling = (x, y, c), (x, y, 1 - c)
        chips = [(1 - x, y), (x, 1 - y), (1 - x, 1 - y)]

        def rows(px, py, pc):
            return out_ref.at[4 * px + 2 * py + pc]

        def copy(k, block, to, src=None):
            return pltpu.make_async_remote_copy(
                src_ref=rows(*block) if src is None else src, dst_ref=rows(*block),
                send_sem=send_sems.at[k], recv_sem=recv_sems.at[k], device_id=to, device_id_type=MESH)

        mine = pltpu.make_async_copy(x_ref, rows(*me), local_sem)
        mine.start()
        first = [copy(0, me, sibling, src=x_ref)]
        first += [copy(1 + j, me, (*chip, c), src=x_ref) for j, chip in enumerate(chips)]
        for cp in first:
            cp.start()
        passed = [copy(4 + j, (*chip, c), sibling) for j, chip in enumerate(chips)]
        for j, chip in enumerate(chips):
            copy(1 + j, (*chip, c), me).wait_recv()
            passed[j].start()
        copy(0, sibling, me).wait_recv()
        for j, chip in enumerate(chips):
            copy(4 + j, (*chip, 1 - c), me).wait_recv()
        for cp in first + passed:
            cp.wait_send()
        mine.wait()

    return pl.pallas_call(
        body, name=name, in_specs=[ANY], out_specs=ANY,
        out_shape=jax.ShapeDtypeStruct((N_DEV,) + xs.shape, xs.dtype),
        scratch_shapes=[pltpu.SemaphoreType.DMA((7,)), pltpu.SemaphoreType.DMA((7,)), pltpu.SemaphoreType.DMA],
    )(xs)


HBM = pl.BlockSpec(memory_space=pltpu.HBM)
SEM = pl.BlockSpec(memory_space=pltpu.SEMAPHORE)
EFFECT = pltpu.SideEffectType.DATAFLOW_SIDE_EFFECTING


def _in_hbm(a):
    return pltpu.with_memory_space_constraint(a, pltpu.HBM)


def _local_copy(src, dst, stage, sem):
    load = pltpu.make_async_copy(src, stage, sem)
    load.start()
    load.wait()
    store = pltpu.make_async_copy(stage, dst, sem)
    store.start()
    store.wait()


def _unique(windows):
    arrays = []
    for per_chip in windows:
        for arr, _ in per_chip:
            if not any(arr is a for a in arrays):
                arrays.append(arr)
    return arrays


def _exchange_start(layer, windows, lands, name):
    arrays = _unique(windows)
    na, nt = len(arrays), len(windows)

    def body(*refs):
        in_refs, land_refs = refs[:na], refs[na:na + nt]
        send_sems, recv_sems = refs[na + nt], refs[na + nt + 1]
        token = refs[-1]
        x, y, c = _place()
        me = 4 * x + 2 * y + c
        for t in range(nt):
            for j in range(N_CHIP):
                arr, window = windows[t][j]
                src = window(in_refs[next(i for i, a in enumerate(arrays) if a is arr)])

                @pl.when(me != 2 * j + layer)
                def _():
                    _start_in_pieces(dict(
                        src_ref=src, dst_ref=land_refs[t].at[me], send_sem=send_sems.at[N_CHIP * t + j],
                        recv_sem=recv_sems.at[N_DEV * t + me], device_id=(j // 2, j % 2, layer),
                        device_id_type=MESH))
        token[...] = jnp.zeros_like(token)

    outs = pl.pallas_call(
        body, name=name,
        out_shape=(pltpu.SemaphoreType.DMA((N_CHIP * nt,)), pltpu.SemaphoreType.DMA((N_DEV * nt,)),
                   *[pltpu.HBM(a.shape, a.dtype) for a in lands], jax.ShapeDtypeStruct((8, 128), F32)),
        in_specs=[HBM] * (na + nt),
        out_specs=(SEM, SEM, *[HBM] * nt, pl.BlockSpec(memory_space=pltpu.VMEM)),
        input_output_aliases={na + t: 2 + t for t in range(nt)},
        compiler_params=pltpu.CompilerParams(has_side_effects=EFFECT, vmem_limit_bytes=VMEM_LIMIT),
    )(*[_in_hbm(a) for a in arrays], *[_in_hbm(a) for a in lands])
    return outs[0], outs[1], list(outs[2:2 + nt]), outs[-1]


def _exchange_wait(layer, windows, lands, send_sems, recv_sems, after, name):
    arrays = _unique(windows)
    na, nt = len(arrays), len(windows)

    def body(*refs):
        in_refs, land_refs = refs[:na], refs[na:na + nt]
        send_sems, recv_sems = refs[na + nt], refs[na + nt + 1]
        stages, local_sem = refs[-1 - nt:-1], refs[-1]
        x, y, c = _place()
        me = 4 * x + 2 * y + c

        def source(t, j):
            arr, window = windows[t][j]
            return window(in_refs[next(i for i, a in enumerate(arrays) if a is arr)])

        @pl.when(c == layer)
        def _():
            for t in range(nt):
                for j in range(N_CHIP):
                    @pl.when(me == 2 * j + layer)
                    def _():
                        _local_copy(source(t, j), land_refs[t].at[me], stages[t], local_sem)

        for t in range(nt):
            for j in range(N_CHIP):
                @pl.when(me != 2 * j + layer)
                def _():
                    pltpu.make_async_remote_copy(
                        src_ref=source(t, j), dst_ref=land_refs[t].at[me], send_sem=send_sems.at[N_CHIP * t + j],
                        recv_sem=recv_sems.at[N_DEV * t + me], device_id=(j // 2, j % 2, layer),
                        device_id_type=MESH).wait_send()

        @pl.when(c == layer)
        def _():
            for t in range(nt):
                for s in range(N_DEV):
                    @pl.when(me != s)
                    def _():
                        slot = land_refs[t].at[s]
                        pltpu.make_async_remote_copy(
                            src_ref=slot, dst_ref=slot, send_sem=send_sems.at[N_CHIP * t],
                            recv_sem=recv_sems.at[N_DEV * t + s], device_id=(x, y, c),
                            device_id_type=MESH).wait_recv()

    outs = pl.pallas_call(
        body, name=name,
        out_shape=tuple(pltpu.HBM(a.shape, a.dtype) for a in lands),
        in_specs=[HBM] * (na + nt) + [SEM, SEM, ANY],
        out_specs=tuple([HBM] * nt),
        input_output_aliases={na + t: t for t in range(nt)},
        scratch_shapes=[pltpu.VMEM(a.shape[1:], a.dtype) for a in lands] + [pltpu.SemaphoreType.DMA],
        compiler_params=pltpu.CompilerParams(has_side_effects=EFFECT, vmem_limit_bytes=VMEM_LIMIT),
    )(*[_in_hbm(a) for a in arrays], *lands, send_sems, recv_sems, after)
    return list(outs)


def _other_chips(x, y):
    return [(1 - x, y), (x, 1 - y), (1 - x, 1 - y)]


def _split_rows(rows):
    if rows < 32:
        return [(0, rows), (rows, 0)]
    cut = -(-(rows // 2) // 16) * 16
    return [(0, cut), (cut, rows - cut)]


PIECES = 4


def _start_in_pieces(cp_args):
    src, dst = cp_args["src_ref"], cp_args["dst_ref"]
    rows = src.shape[0]
    step = -(-rows // (16 * PIECES)) * 16
    for r0 in range(0, rows, step):
        n = min(step, rows - r0)
        pltpu.make_async_remote_copy(**{**cp_args, "src_ref": src.at[pl.ds(r0, n)],
                                        "dst_ref": dst.at[pl.ds(r0, n)]}).start()


def _gather_copies(in_refs, land_refs, send_sems, recv_sems, x, y, sender_core, both):
    copies = []
    for t, src in enumerate(in_refs):
        r0, n = _split_rows(src.shape[0])[sender_core]
        if n == 0:
            continue
        for rel, (tx, ty) in enumerate(_other_chips(x, y)):
            for tc in (0, 1) if both else (sender_core,):
                copies.append(dict(
                    src_ref=src.at[pl.ds(r0, n)], dst_ref=land_refs[t].at[2 * x + y].at[pl.ds(r0, n)],
                    send_sem=send_sems.at[6 * t + 2 * rel + tc],
                    recv_sem=recv_sems.at[2 * (3 * t + rel) + sender_core],
                    device_id=(tx, ty, tc), device_id_type=MESH))
    return copies


def _gather_start(shards, lands, name, both):
    nt = len(shards)

    def body(*refs):
        in_refs, land_refs = refs[:nt], refs[nt:2 * nt]
        send_sems, recv_sems = refs[2 * nt], refs[2 * nt + 1]
        token = refs[-1]
        x, y, c = _place()
        for core in range(2):
            @pl.when(c == core)
            def _():
                for cp_args in _gather_copies(in_refs, land_refs, send_sems, recv_sems, x, y, core, both):
                    _start_in_pieces(cp_args)
        token[...] = jnp.zeros_like(token)

    outs = pl.pallas_call(
        body, name=name,
        out_shape=(pltpu.SemaphoreType.DMA((6 * nt,)), pltpu.SemaphoreType.DMA((6 * nt,)),
                   *[pltpu.HBM(a.shape, a.dtype) for a in lands], jax.ShapeDtypeStruct((8, 128), F32)),
        in_specs=[HBM] * (2 * nt),
        out_specs=(SEM, SEM, *[HBM] * nt, pl.BlockSpec(memory_space=pltpu.VMEM)),
        input_output_aliases={nt + t: 2 + t for t in range(nt)},
        compiler_params=pltpu.CompilerParams(has_side_effects=EFFECT, vmem_limit_bytes=VMEM_LIMIT),
    )(*[_in_hbm(a) for a in shards], *[_in_hbm(a) for a in lands])
    return outs[0], outs[1], list(outs[2:2 + nt]), outs[-1]


def _gather_wait(shards, lands, send_sems, recv_sems, after, name, both):
    nt = len(shards)

    def body(*refs):
        in_refs, land_refs = refs[:nt], refs[nt:2 * nt]
        send_sems, recv_sems = refs[2 * nt], refs[2 * nt + 1]
        stages, local_sem = refs[-1 - nt:-1], refs[-1]
        x, y, c = _place()

        for t in range(nt):
            _local_copy(in_refs[t], land_refs[t].at[2 * x + y], stages[t], local_sem)

        for core in range(2):
            @pl.when(c == core)
            def _():
                for cp_args in _gather_copies(in_refs, land_refs, send_sems, recv_sems, x, y, core, both):
                    pltpu.make_async_remote_copy(**cp_args).wait_send()

        def wait_parts_from(core):
            for t in range(nt):
                r0, n = _split_rows(in_refs[t].shape[0])[core]
                for rel, (tx, ty) in enumerate(_other_chips(x, y)):
                    if n > 0:
                        part = land_refs[t].at[2 * tx + ty].at[pl.ds(r0, n)]
                        pltpu.make_async_remote_copy(
                            src_ref=part, dst_ref=part, send_sem=send_sems.at[6 * t],
                            recv_sem=recv_sems.at[2 * (3 * t + rel) + core], device_id=(x, y, c),
                            device_id_type=MESH).wait_recv()

        for core in range(2):
            if both:
                wait_parts_from(core)
            else:
                pl.when(c == core)(functools.partial(wait_parts_from, core))

    outs = pl.pallas_call(
        body, name=name,
        out_shape=tuple(pltpu.HBM(a.shape, a.dtype) for a in lands),
        in_specs=[HBM] * (2 * nt) + [SEM, SEM, ANY],
        out_specs=tuple([HBM] * nt),
        input_output_aliases={nt + t: t for t in range(nt)},
        scratch_shapes=[pltpu.VMEM(a.shape, a.dtype) for a in shards] + [pltpu.SemaphoreType.DMA],
        compiler_params=pltpu.CompilerParams(has_side_effects=EFFECT, vmem_limit_bytes=VMEM_LIMIT),
    )(*[_in_hbm(a) for a in shards], *lands, send_sems, recv_sems, after)
    return list(outs)


def _pair_share(lands, name):
    nt = len(lands)
    splits = [_split_rows(a.shape[1]) for a in lands]

    def body(*refs):
        land_refs = refs[:nt]
        stages = refs[2 * nt:3 * nt]
        load_sems, send_sems, recv_sems = refs[3 * nt:]
        x, y, c = _place()

        def parts(core):
            out = []
            for t in range(nt):
                r0, n = splits[t][core]
                for rel, (tx, ty) in enumerate(_other_chips(x, y)):
                    if n > 0:
                        out.append((3 * t + rel, stages[t].at[rel, pl.ds(0, n)],
                                    land_refs[t].at[2 * tx + ty].at[pl.ds(r0, n)]))
            return out

        def send(core):
            loads = [pltpu.make_async_copy(part, stage, load_sems.at[k]) for k, stage, part in parts(core)]
            for cp in loads:
                cp.start()
            pushes = []
            for cp, (k, stage, part) in zip(loads, parts(core)):
                cp.wait()
                push = pltpu.make_async_remote_copy(src_ref=stage, dst_ref=part, send_sem=send_sems.at[k],
                                                    recv_sem=recv_sems.at[k], device_id=(x, y, 1 - c),
                                                    device_id_type=MESH)
                push.start()
                pushes.append(push)
            for push in pushes:
                push.wait_send()
            for k, _, part in parts(1 - core):
                pltpu.make_async_remote_copy(src_ref=part, dst_ref=part, send_sem=send_sems.at[k],
                                             recv_sem=recv_sems.at[k], device_id=(x, y, 1 - c),
                                             device_id_type=MESH).wait_recv()

        for core in range(2):
            pl.when(c == core)(functools.partial(send, core))

    outs = pl.pallas_call(
        body, name=name, in_specs=[ANY] * nt, out_specs=[ANY] * nt,
        out_shape=[jax.ShapeDtypeStruct(a.shape, a.dtype) for a in lands],
        input_output_aliases={t: t for t in range(nt)},
        scratch_shapes=[pltpu.VMEM((3, max(n for _, n in sp), a.shape[2]), a.dtype) for a, sp in zip(lands, splits)]
        + [pltpu.SemaphoreType.DMA((3 * nt,))] * 3,
        compiler_params=pltpu.CompilerParams(vmem_limit_bytes=VMEM_LIMIT),
    )(*lands)
    return list(outs)


def _tie(a, token):
    def body(a_ref, token_ref, o_ref):
        pass

    return pl.pallas_call(
        body, name="tie", in_specs=[ANY, ANY], out_specs=ANY,
        out_shape=jax.ShapeDtypeStruct(a.shape, a.dtype), input_output_aliases={0: 0},
    )(a, token)


def _sum_share(xs, name):
    _, r, cols = xs.shape
    tr = max(t for t in range(16, min(r, 704) + 1, 16) if r % t == 0)
    nblk = r // tr

    def body(x_ref, out_ref, acc_ref, send_sems, local_sems, recv_sem):
        i = pl.program_id(0)
        slot = i % 2
        x, y, c = _place()

        def copies(s, blk):
            dst = out_ref.at[c, pl.ds(blk * tr, tr), :]
            loc = pltpu.make_async_copy(acc_ref.at[s], dst, local_sems.at[s])
            rem = pltpu.make_async_remote_copy(src_ref=acc_ref.at[s], dst_ref=dst, send_sem=send_sems.at[s],
                                               recv_sem=recv_sem, device_id=(x, y, 1 - c), device_id_type=MESH)
            return loc, rem

        @pl.when(i >= 2)
        def _():
            loc, rem = copies(slot, i - 2)
            loc.wait()
            rem.wait_send()

        acc = x_ref[0].astype(F32)
        for k in range(1, N_DEV):
            acc = acc + x_ref[k].astype(F32)
        acc_ref[slot] = acc
        loc, rem = copies(slot, i)
        loc.start()
        rem.start()

        @pl.when(i == nblk - 1)
        def _():
            for back in range(min(2, nblk)):
                blk = nblk - 1 - back
                loc, rem = copies(blk % 2, blk)
                loc.wait()
                rem.wait_send()
            theirs = out_ref.at[1 - c]
            pltpu.make_async_remote_copy(src_ref=theirs, dst_ref=theirs, send_sem=send_sems.at[0],
                                         recv_sem=recv_sem, device_id=(x, y, 1 - c),
                                         device_id_type=MESH).wait_recv()

    return pl.pallas_call(
        body, name=name, grid=(nblk,),
        in_specs=[pl.BlockSpec((N_DEV, tr, cols), lambda i: (0, i, 0))],
        out_specs=ANY,
        out_shape=jax.ShapeDtypeStruct((2, r, cols), F32),
        scratch_shapes=[pltpu.VMEM((2, tr, cols), F32), pltpu.SemaphoreType.DMA((2,)),
                        pltpu.SemaphoreType.DMA((2,)), pltpu.SemaphoreType.DMA],
        compiler_params=_params("arbitrary"),
    )(xs)


def _sum8(xs, name):
    _, r, cols = xs.shape
    tr = 8
    for cand in (256, 128, 64, 32, 16):
        if r % cand == 0:
            tr = cand
            break

    def body(x_ref, o_ref):
        acc = x_ref[0].astype(F32)
        for k in range(1, N_DEV):
            acc = acc + x_ref[k].astype(F32)
        o_ref[...] = acc

    return pl.pallas_call(
        body, name=name, grid=(r // tr,),
        in_specs=[pl.BlockSpec((N_DEV, tr, cols), lambda i: (0, i, 0))],
        out_specs=pl.BlockSpec((tr, cols), lambda i: (i, 0)),
        out_shape=jax.ShapeDtypeStruct((r, cols), F32),
        compiler_params=_params("parallel"),
    )(xs)


def _adamw(w, g, m, v, name):
    R, C = w.shape
    tr = max([t for t in range(8, min(R, 512) + 1, 8) if R % t == 0] or [R])
    c1 = 1.0 / (1.0 - ADAM_B1 ** ADAM_STEP)
    c2 = 1.0 / (1.0 - ADAM_B2 ** ADAM_STEP)

    def body(w_ref, g_ref, m_ref, v_ref, d_ref, nm_ref, nv_ref):
        gv = g_ref[...]
        nm = ADAM_B1 * m_ref[...] + (1.0 - ADAM_B1) * gv
        nv = ADAM_B2 * v_ref[...] + (1.0 - ADAM_B2) * (gv * gv)
        d_ref[...] = (-ADAM_LR) * ((nm * c1) / (jnp.sqrt(nv * c2) + ADAM_EPS) + ADAM_WD * w_ref[...])
        nm_ref[...] = nm
        nv_ref[...] = nv

    spec = pl.BlockSpec((tr, C), lambda i: (i, 0))
    shape = jax.ShapeDtypeStruct((R, C), F32)
    return pl.pallas_call(
        body, name=name, grid=(R // tr,), in_specs=[spec] * 4, out_specs=[spec] * 3, out_shape=[shape] * 3,
        compiler_params=_params("parallel"),
    )(w, g, m, v)


def _flat_rows(parts, rows):
    flat = jnp.concatenate([q.reshape(-1) for q in parts])
    flat = jnp.pad(flat, (0, rows * LANES - flat.shape[0]))
    return flat.reshape(rows, LANES)


def _round_up(n, m):
    return (n + m - 1) // m * m


def _block_diag(w):
    H, n, _ = w.shape
    eye = jnp.eye(H, dtype=w.dtype)
    return (eye[:, None, :, None] * w[:, :, None, :]).reshape(H * n, H * n)


def _diag_blocks(w, H, n):
    w4 = w.reshape(H, n, H, n)
    return jnp.stack([w4[h, :, h, :] for h in range(H)])


W_IN_T = ("ffn1_w_in", "w_in", "ffn2_w_in")


def _ffn_in_weights(g_in):
    zeros = jnp.zeros((FF_HALF - FF_SHARD, D), g_in.dtype)
    wg_t = jnp.concatenate([g_in[0], zeros, g_in[1], zeros], axis=0)
    wu_t = jnp.concatenate([g_in[2], zeros, g_in[3], zeros], axis=0)
    return wg_t, wu_t


def _ffn_out_weights(g_out):
    zeros = jnp.zeros((FF_HALF - FF_SHARD, D), g_out.dtype)
    return jnp.concatenate([g_out[0], g_out[1], zeros, g_out[2], g_out[3], zeros], axis=0)


LAND_SHAPES = {"ffn1_w_in": (FF_SHARD, D), "ffn1_w_out": (FF_ROWS, D), "w_in": (D_IN // N_CHIP, D),
               "w_out": (D // N_CHIP, D), "ffn2_w_in": (FF_SHARD, D), "ffn2_w_out": (FF_ROWS, D)}


def _rows_window(arr, start, size):
    return arr, lambda r: r.at[pl.ds(start, size), :]


def _ffn_grad_windows(dwg_t, dwu_t, dwout):
    w_in = [_rows_window(dwg_t if j < 2 else dwu_t, (j % 2) * FF_HALF, FF_SHARD) for j in range(N_CHIP)]
    w_out = [_rows_window(dwout, (j // 2) * FF_HALF + (j % 2) * FF_ROWS, FF_ROWS) for j in range(N_CHIP)]
    return w_in, w_out


def _mix_grad_windows(dwin_t, dwo):
    win = [_rows_window(dwin_t, j * (D_IN // N_CHIP), D_IN // N_CHIP) for j in range(N_CHIP)]
    wo = [_rows_window(dwo, j * (D // N_CHIP), D // N_CHIP) for j in range(N_CHIP)]
    return win, wo


def kernel(x, ffn1_norm, ffn1_w_in, ffn1_w_out, mix_norm, w_in, conv_w, conv_b, rg_w_a, rg_b_a, rg_w_x, rg_b_x, lru_lambda, pool_w, pool_scale, sgu_norm, sgu_w, sgu_b, w_out, ffn2_norm, ffn2_w_in, ffn2_w_out, final_norm, loss_target, m_ffn1_norm, m_ffn1_w_in, m_ffn1_w_out, m_mix_norm, m_w_in, m_conv_w, m_conv_b, m_rg_w_a, m_rg_b_a, m_rg_w_x, m_rg_b_x, m_lru_lambda, m_pool_w, m_pool_scale, m_sgu_norm, m_sgu_w, m_sgu_b, m_w_out, m_ffn2_norm, m_ffn2_w_in, m_ffn2_w_out, m_final_norm, v_ffn1_norm, v_ffn1_w_in, v_ffn1_w_out, v_mix_norm, v_w_in, v_conv_w, v_conv_b, v_rg_w_a, v_rg_b_a, v_rg_w_x, v_rg_b_x, v_lru_lambda, v_pool_w, v_pool_scale, v_sgu_norm, v_sgu_w, v_sgu_b, v_w_out, v_ffn2_norm, v_ffn2_w_in, v_ffn2_w_out, v_final_norm):
    args = locals()
    W = {n: args[n] for n in WEIGHTS}
    M = {n: args["m_" + n] for n in WEIGHTS}
    V = {n: args["v_" + n] for n in WEIGHTS}
    depth = ffn1_norm.shape[0]
    T = x.shape[1]
    xi, yi, ci = _place()
    chip = 2 * xi + yi

    assert depth == 2, "core c of a chip sends and reduces layer c"
    groups = [(l, names) for l in range(depth)
              for names in (["ffn1_w_in"], ["ffn1_w_out", "w_in", "w_out"], ["ffn2_w_in", "ffn2_w_out"])]
    def stored(a, n):
        return jnp.swapaxes(a, 1, 2) if n in W_IN_T else a

    wb = {n: stored(W[n], n).astype(BF16) for n in BIG}
    groups[1][1].append("conv_w")
    conv_shard = conv_w.reshape(-1, conv_w.shape[-1])
    flights = {}

    def weights_start(k, dep=None):
        l, names = groups[k]
        shards = [conv_shard if n == "conv_w" else wb[n][l] for n in names]
        if dep is not None:
            shards[0] = _tie(shards[0], dep)
        lands = [lax.empty((N_CHIP,) + s.shape, s.dtype) for s in shards]
        send, recv, lands, token = _gather_start(shards, lands, "weights_start_%d" % k, k >= 2)
        flights[k] = (shards, lands, send, recv)
        return token

    def weights_wait(k, after):
        l, names = groups[k]
        shards, lands, send, recv = flights[k]
        got = _gather_wait(shards, lands, send, recv, after, "weights_wait_%d" % k, k >= 2)
        token = weights_start(k + 2, got[0]) if k + 2 < len(groups) else None
        if k < 2:
            got = _pair_share(got, "weights_share_%d" % k)
        return dict(zip(names, got)), token

    def after_start(a, token):
        return a if token is None else _tie(a, token)

    first_tokens = [weights_start(0), weights_start(1)]

    layers = []
    for l in range(depth):
        L = {f: dict(norm=W[f + "_norm"][l][None]) for f in ("ffn1", "ffn2")}
        ws = jnp.where(jnp.tril(jnp.ones((CHUNK, CHUNK), bool))[None], sgu_w[l], 0.0)
        wax = jnp.concatenate([_block_diag(rg_w_a[l]), _block_diag(rg_w_x[l])], axis=1)
        wpool = _block_diag(pool_w[l])
        L["mix"] = dict(
            conv_b=conv_b[l][None], wax=wax.astype(BF16), wax_t=wax.T.astype(BF16),
            bax=jnp.concatenate([rg_b_a[l].reshape(-1), rg_b_x[l].reshape(-1)])[None], lam=lru_lambda[l][None],
            wpool=wpool.astype(BF16), wpool_t=wpool.T.astype(BF16), pool_scale=pool_scale[l][None],
            sgu_norm=sgu_norm[l][None], ws=ws.astype(BF16), ws_t=jnp.swapaxes(ws, 1, 2).astype(BF16),
            bz=jnp.repeat(sgu_b[l].T, 64, axis=1))
        L["mix_norm"] = mix_norm[l][None]
        layers.append(L)
    for token in first_tokens:
        layers[0]["ffn1"]["norm"] = _tie(layers[0]["ffn1"]["norm"], token)

    xs = x[0]
    saved = []
    for l, L in enumerate(layers):
        F1, F2 = L["ffn1"], L["ffn2"]
        got, token = weights_wait(3 * l, xs)
        F1["wg"], F1["wu"] = _ffn_in_weights(got["ffn1_w_in"])
        F1["norm"] = after_start(F1["norm"], token)
        h, g, u, a = _ffn_in(xs, F1["norm"], F1["wg"], F1["wu"])
        got, token = weights_wait(3 * l + 1, a)
        F1["wout"] = _ffn_out_weights(got["ffn1_w_out"])
        L["w_in"] = got["w_in"].reshape(D_IN, D)
        L["w_out"] = jnp.concatenate([got["w_out"][j] for j in range(N_CHIP)], axis=0)
        if "conv_w" in got:
            conv_full = jnp.concatenate([got["conv_w"][j] for j in range(N_CHIP)], axis=1)
            for ll in range(depth):
                layers[ll]["mix"]["conv_w"] = conv_full.reshape(depth, 4, D_RNN)[ll]
        a = after_start(a, token)
        x1 = _mm_res(a, F1["wout"], xs, 0.5, "ffn_out", tm=512, tn=D)
        s1 = (xs, h, g, u, a)
        hm, p = _mix_in(x1, L["mix_norm"], L["w_in"])
        ycat, hs = _mix_fwd(p, L["mix"])
        x2 = _mm_res(ycat, L["w_out"], x1, 1.0, "mix_out", tm=512, tn=D)
        got, token = weights_wait(3 * l + 2, x2)
        F2["wg"], F2["wu"] = _ffn_in_weights(got["ffn2_w_in"])
        F2["wout"] = _ffn_out_weights(got["ffn2_w_out"])
        F2["norm"] = after_start(F2["norm"], token)
        h2, g2, u2, a2 = _ffn_in(x2, F2["norm"], F2["wg"], F2["wu"])
        x3 = _mm_res(a2, F2["wout"], x2, 0.5, "ffn_out", tm=512, tn=D)
        saved.append((s1, (x1, hm, p, ycat, hs), (x2, h2, g2, u2, a2)))
        xs = x3

    dx, dxb, d_final, loss_part = _final(xs, loss_target[0], final_norm[None], 0.5)

    G = {n: [None] * depth for n in SMALL if n != "final_norm"}
    lands = {n: lax.empty((N_DEV,) + LAND_SHAPES[n], BF16) for n in BIG}
    in_flight = []

    def send_grads(l, windows, tag):
        names = list(windows)
        send, recv, thru, token = _exchange_start(l, [windows[n] for n in names], [lands[n] for n in names],
                                                  "grads_start_" + tag)
        lands.update(zip(names, thru))
        in_flight.append((l, names, [windows[n] for n in names], send, recv, "grads_wait_" + tag))
        return token

    def ffn_bwd(dx, dxb, F, s, f, l, pending, send_now):
        xin, h, g, u, a = s
        dg, du = _ffn_mid_bwd(dxb, F["wout"], g, u)
        dwout = _mm_tn(a, dxb, 1.0, "ffn_dwout")
        dwg = _mm_tn(dg, h, 1.0, "ffn_dwg")
        dwu = _mm_tn(du, h, 1.0, "ffn_dwu")
        pending[f + "_w_in"], pending[f + "_w_out"] = _ffn_grad_windows(dwg, dwu, dwout)
        if send_now:
            dg = _tie(dg, send_grads(l, pending, "l%d_%s" % (l, f)))
        dx, dxb, dn = _dh_rms_bwd([(dg, F["wg"]), (du, F["wu"])], xin, F["norm"], dx,
                                  1.0 if f == "ffn2" else 0.5, "ffn_dh")
        G[f + "_norm"][l] = dn[0]
        return dx, dxb

    for l in reversed(range(depth)):
        L = layers[l]
        s1, (x1, hm, p, ycat, hs), s2 = saved[l]
        pending = {}
        dx, dxb = ffn_bwd(dx, dxb, L["ffn2"], s2, "ffn2", l, pending, l == 0)
        if l == 0:
            pending = {}
        dycat = _mm_nt(dxb, L["w_out"], "mix_dy", tm=512, tn=D)
        dwo = _mm_tn(ycat, dxb, 1.0, "mix_dwout")
        mg = _mix_bwd(dycat, p, hs, L["mix"])
        dwin = _mm_tn(mg["dp"], hm, 1.0, "mix_dwin")
        pending["w_in"], pending["w_out"] = _mix_grad_windows(dwin, dwo)
        if l == 0:
            dp = _tie(mg["dp"], send_grads(l, pending, "l0_mix"))
            pending = {}
        else:
            dp = mg["dp"]
        dx, dxb, dn = _dh_rms_bwd([(dp, L["w_in"])], x1, L["mix_norm"], dx, 0.5, "mix_dh")
        G["mix_norm"][l] = dn[0]
        G["conv_w"][l], G["conv_b"][l] = mg["conv_w"], mg["conv_b"][0]
        G["rg_w_a"][l] = _diag_blocks(mg["wax"][:, :D_RNN], 8, 64)
        G["rg_w_x"][l] = _diag_blocks(mg["wax"][:, D_RNN:], 8, 64)
        G["rg_b_a"][l] = mg["bax"][0, :D_RNN].reshape(8, 64)
        G["rg_b_x"][l] = mg["bax"][0, D_RNN:].reshape(8, 64)
        G["lru_lambda"][l] = mg["lam"][0]
        G["pool_w"][l] = _diag_blocks(mg["wpool"], 4, 64)
        G["pool_scale"][l], G["sgu_norm"][l] = mg["pool_scale"][0], mg["sgu_norm"][0]
        G["sgu_w"][l] = mg["ws"]
        G["sgu_b"][l] = mg["bz"].reshape(CHUNK, 4, 64).sum(-1).T
        dx, dxb = ffn_bwd(dx, dxb, L["ffn1"], s1, "ffn1", l, pending, l == 0)
        if l > 0:
            dxb = _tie(dxb, send_grads(l, pending, "l%d" % l))
    grad_x = dx[None]
    G = {n: jnp.stack(v) for n, v in G.items()}
    G["final_norm"] = d_final[0]

    for l, names, windows, send, recv, tag in in_flight:
        lands.update(zip(names, _exchange_wait(l, windows, [lands[n] for n in names], send, recv, dx, tag)))
    both = [_sum_share(lands[n], "sum_share_" + n) for n in BIG]
    grads = dict(zip(BIG, both))

    small_sizes = [int(np.prod(G[n].shape)) for n in SMALL]
    srows = _round_up(sum(small_sizes) + 1, N_DEV * 8 * LANES) // (N_DEV * LANES)
    sflat = _flat_rows([G[n] for n in SMALL] + [loss_part[0, :1]], N_DEV * srows)
    sgot = _all_to_all(sflat.reshape(N_DEV, srows, LANES), "exchange_small_grads")
    sall = _all_gather8(_sum8(sgot, "sum_small_grads"), "share_small_grads").reshape(-1)
    off = 0
    for n, size in zip(SMALL, small_sizes):
        grads[n] = sall[off:off + size].reshape(G[n].shape)
        off += size
    loss = sall[off]
    grads["conv_w"] = lax.dynamic_slice_in_dim(grads["conv_w"], chip * conv_w.shape[2], conv_w.shape[2], axis=2)

    delta, new_m, new_v = {}, {}, {}
    for n in BIG:
        shp = grads[n].shape
        two_d = (shp[0] * shp[1], shp[2])
        outs = _adamw(stored(W[n], n).reshape(two_d), grads[n].reshape(two_d), stored(M[n], n).reshape(two_d),
                      stored(V[n], n).reshape(two_d), "adamw_" + n)
        delta[n], new_m[n], new_v[n] = (stored(o.reshape(shp), n) for o in outs)
        grads[n] = stored(grads[n], n)
    arows = _round_up(sum(int(np.prod(W[n].shape)) for n in SMALL), 8 * LANES) // LANES
    outs = _adamw(*(_flat_rows([src[n] for n in SMALL], arows) for src in (W, grads, M, V)), "adamw_small")
    outs = [o.reshape(-1) for o in outs]
    off = 0
    for n in SMALL:
        size = int(np.prod(W[n].shape))
        delta[n], new_m[n], new_v[n] = (o[off:off + size].reshape(W[n].shape) for o in outs)
        off += size

    return (loss, grad_x, *[grads[n] for n in WEIGHTS], *[delta[n] for n in WEIGHTS],
            *[new_m[n] for n in WEIGHTS], *[new_v[n] for n in WEIGHTS])
```

```python
import functools
import math

import jax
import jax.numpy as jnp
import numpy as np
from jax import lax
from jax.experimental import pallas as pl
from jax.experimental.pallas import tpu as pltpu

F32 = jnp.float32
BF16 = jnp.bfloat16
MESH = pl.DeviceIdType.MESH

D = 1024
D_RNN = 512
D_POOL = 256
D_SGU = 256
D_IN = 1792
D_FF = 2752
D_FFP = 2816
N_CHIP = 4
FF_SHARD = D_FF // 2
FF_HALF = D_FFP // 2
FF_ROWS = D_FF // N_CHIP
CHUNK = 128
HALO = 16
EPS = 1e-6
LRU_C = 8.0
N_DEV = 8
LANES = 1024
VMEM_LIMIT = 56 * 1024 * 1024

ADAM_LR, ADAM_B1, ADAM_B2, ADAM_EPS, ADAM_WD, ADAM_STEP = 0.001, 0.9, 0.999, 1e-08, 0.01, 10

BIG = ("ffn1_w_in", "ffn1_w_out", "w_in", "w_out", "ffn2_w_in", "ffn2_w_out")
SMALL = ("ffn1_norm", "mix_norm", "conv_w", "conv_b", "rg_w_a", "rg_b_a", "rg_w_x", "rg_b_x", "lru_lambda",
         "pool_w", "pool_scale", "sgu_norm", "sgu_w", "sgu_b", "ffn2_norm", "final_norm")
WEIGHTS = ("ffn1_norm", "ffn1_w_in", "ffn1_w_out", "mix_norm", "w_in", "conv_w", "conv_b", "rg_w_a", "rg_b_a",
           "rg_w_x", "rg_b_x", "lru_lambda", "pool_w", "pool_scale", "sgu_norm", "sgu_w", "sgu_b", "w_out",
           "ffn2_norm", "ffn2_w_in", "ffn2_w_out", "final_norm")


def _params(*sem):
    return pltpu.CompilerParams(dimension_semantics=sem, vmem_limit_bytes=VMEM_LIMIT)


def _gelu(x):
    c = math.sqrt(2.0 / math.pi)
    t = jnp.tanh(c * (x + 0.044715 * (x * x * x)))
    return 0.5 * x * (1.0 + t)


def _gelu_and_grad(x):
    c = math.sqrt(2.0 / math.pi)
    x2 = x * x
    t = jnp.tanh(c * (x + 0.044715 * (x2 * x)))
    g = 0.5 * x * (1.0 + t)
    dg = 0.5 * (1.0 + t) + 0.5 * x * (1.0 - t * t) * (c * (1.0 + 3.0 * 0.044715 * x2))
    return g, dg


def _sigmoid(x):
    return 0.5 * jnp.tanh(0.5 * x) + 0.5


def _dot(a, b):
    return jnp.dot(a, b, preferred_element_type=F32)


def _dot_tn(a, b):
    return lax.dot_general(a, b, (((0,), (0,)), ((), ())), preferred_element_type=F32)


def _dot_nt(a, b):
    return lax.dot_general(a, b, (((1,), (1,)), ((), ())), preferred_element_type=F32)


def _tile(n, limit):
    best = 128
    for t in range(128, min(n, limit) + 1, 128):
        if n % t == 0:
            best = t
    assert n % best == 0, (n, limit)
    return best


def _mm_res(a, b, res, scale, name, tm=1024, tn=512):
    M, K = a.shape
    N = b.shape[1]
    tm, tn = min(tm, M), _tile(N, tn)

    def body(a_ref, b_ref, r_ref, o_ref):
        o_ref[...] = r_ref[...] + scale * _dot(a_ref[...], b_ref[...])

    return pl.pallas_call(
        body, name=name, grid=(M // tm, N // tn),
        in_specs=[pl.BlockSpec((tm, K), lambda i, j: (i, 0)), pl.BlockSpec((K, tn), lambda i, j: (0, j)),
                  pl.BlockSpec((tm, tn), lambda i, j: (i, j))],
        out_specs=pl.BlockSpec((tm, tn), lambda i, j: (i, j)),
        out_shape=jax.ShapeDtypeStruct((M, N), F32),
        compiler_params=_params("parallel", "parallel"),
    )(a, b, res)


def _mm_nt(a, b, name, tm=1024, tn=512):
    M, K = a.shape
    N = b.shape[0]
    tm, tn = min(tm, M), _tile(N, tn)

    def body(a_ref, b_ref, o_ref):
        o_ref[...] = _dot_nt(a_ref[...], b_ref[...])

    return pl.pallas_call(
        body, name=name, grid=(M // tm, N // tn),
        in_specs=[pl.BlockSpec((tm, K), lambda i, j: (i, 0)), pl.BlockSpec((tn, K), lambda i, j: (j, 0))],
        out_specs=pl.BlockSpec((tm, tn), lambda i, j: (i, j)),
        out_shape=jax.ShapeDtypeStruct((M, N), F32),
        compiler_params=_params("parallel", "parallel"),
    )(a, b)


def _dh_rms_bwd(pairs, x, g, dres, copy_scale, name, tm=512):
    T = x.shape[0]
    tm = min(tm, T)
    n = len(pairs)

    def body(*refs):
        ab = refs[:2 * n]
        x_ref, g_ref, dres_ref, dx_ref, dxb_ref, dg_ref = refs[2 * n:]
        dy = _dot(ab[0][...], ab[1][...])
        for k in range(1, n):
            dy = dy + _dot(ab[2 * k][...], ab[2 * k + 1][...])
        xv = x_ref[...]
        r = lax.rsqrt(jnp.mean(xv * xv, axis=-1, keepdims=True) + EPS)
        xhat = xv * r
        dxhat = dy * g_ref[...]
        dx = dres_ref[...] + r * (dxhat - xhat * jnp.mean(dxhat * xhat, axis=-1, keepdims=True))
        dx_ref[...] = dx
        dxb_ref[...] = (copy_scale * dx).astype(BF16)

        @pl.when(pl.program_id(0) == 0)
        def _():
            dg_ref[...] = jnp.zeros_like(dg_ref)

        dg_ref[...] += jnp.sum(dy * xhat, axis=0, keepdims=True)

    row = pl.BlockSpec((tm, D), lambda i: (i, 0))
    vec = pl.BlockSpec((1, D), lambda i: (0, 0))
    in_specs, operands = [], []
    for a, b in pairs:
        in_specs += [pl.BlockSpec((tm, a.shape[1]), lambda i: (i, 0)),
                     pl.BlockSpec(b.shape, lambda i: (0, 0), pipeline_mode=pl.Buffered(1))]
        operands += [a, b]
    return pl.pallas_call(
        body, name=name, grid=(T // tm,),
        in_specs=in_specs + [row, vec, row], out_specs=[row, row, vec],
        out_shape=[jax.ShapeDtypeStruct((T, D), F32), jax.ShapeDtypeStruct((T, D), BF16),
                   jax.ShapeDtypeStruct((1, D), F32)],
        compiler_params=_params("arbitrary"),
    )(*operands, x, g, dres)


def _mm_tn(a, b, scale, name, tm=1792, tn=1792, tk=2048):
    T, M = a.shape
    N = b.shape[1]
    tm, tn, tk = _tile(M, tm), _tile(N, tn), min(tk, T)
    nk = T // tk

    def body(a_ref, b_ref, o_ref, acc_ref):
        k = pl.program_id(2)

        @pl.when(k == 0)
        def _():
            acc_ref[...] = jnp.zeros_like(acc_ref)

        acc_ref[...] += _dot_tn(a_ref[...], b_ref[...])

        @pl.when(k == nk - 1)
        def _():
            o_ref[...] = (scale * acc_ref[...]).astype(BF16)

    return pl.pallas_call(
        body, name=name, grid=(M // tm, N // tn, nk),
        in_specs=[pl.BlockSpec((tk, tm), lambda i, j, k: (k, i)), pl.BlockSpec((tk, tn), lambda i, j, k: (k, j))],
        out_specs=pl.BlockSpec((tm, tn), lambda i, j, k: (i, j)),
        out_shape=jax.ShapeDtypeStruct((M, N), BF16),
        scratch_shapes=[pltpu.VMEM((tm, tn), F32)],
        compiler_params=_params("parallel", "parallel", "arbitrary"),
    )(a, b)


def _rms_rows(x_ref, gain_ref):
    xv = x_ref[...]
    r = lax.rsqrt(jnp.mean(xv * xv, axis=-1, keepdims=True) + EPS)
    return (xv * r * gain_ref[...]).astype(BF16)


def _ffn_in(x, gain, wg_t, wu_t, tm=512, tn=FF_HALF):
    T = x.shape[0]
    tm = min(tm, T)

    def body(x_ref, gain_ref, wg_ref, wu_ref, h_ref, g_ref, u_ref, a_ref):
        hv = _rms_rows(x_ref, gain_ref)

        @pl.when(pl.program_id(0) == 0)
        def _():
            h_ref[...] = hv

        g = _dot_nt(hv, wg_ref[...])
        u = _dot_nt(hv, wu_ref[...])
        g_ref[...] = g.astype(BF16)
        u_ref[...] = u.astype(BF16)
        a_ref[...] = (g * _sigmoid(g) * u).astype(BF16)

    row = pl.BlockSpec((tm, D), lambda j, i: (i, 0))
    last = T // tm - 1
    h_spec = pl.BlockSpec((tm, D), lambda j, i: (jnp.where(j == 0, i, last), 0))
    wspec = pl.BlockSpec((tn, D), lambda j, i: (j, 0))
    ospec = pl.BlockSpec((tm, tn), lambda j, i: (i, j))
    oshape = jax.ShapeDtypeStruct((T, D_FFP), BF16)
    return pl.pallas_call(
        body, name="ffn_in", grid=(D_FFP // tn, T // tm),
        in_specs=[row, pl.BlockSpec((1, D), lambda j, i: (0, 0)), wspec, wspec],
        out_specs=[h_spec, ospec, ospec, ospec],
        out_shape=[jax.ShapeDtypeStruct((T, D), BF16), oshape, oshape, oshape],
        compiler_params=_params("arbitrary", "arbitrary"),
    )(x, gain, wg_t, wu_t)


def _mix_in(x, gain, w_in_t, tm=512):
    T = x.shape[0]
    tm = min(tm, T)

    def body(x_ref, gain_ref, w_ref, h_ref, p_ref):
        hv = _rms_rows(x_ref, gain_ref)
        h_ref[...] = hv
        p_ref[...] = _dot_nt(hv, w_ref[...])

    row = pl.BlockSpec((tm, D), lambda i: (i, 0))
    return pl.pallas_call(
        body, name="mix_in", grid=(T // tm,),
        in_specs=[row, pl.BlockSpec((1, D), lambda i: (0, 0)), pl.BlockSpec((D_IN, D), lambda i: (0, 0))],
        out_specs=[row, pl.BlockSpec((tm, D_IN), lambda i: (i, 0))],
        out_shape=[jax.ShapeDtypeStruct((T, D), BF16), jax.ShapeDtypeStruct((T, D_IN), F32)],
        compiler_params=_params("parallel"),
    )(x, gain, w_in_t)


def _ffn_mid_bwd(dyh, wout, g, u, tm=512, tn=FF_HALF):
    T = dyh.shape[0]
    tm = min(tm, T)

    def body(dy_ref, w_ref, g_ref, u_ref, dg_ref, du_ref):
        da = _dot_nt(dy_ref[...], w_ref[...])
        g = g_ref[...].astype(F32)
        s = _sigmoid(g)
        gs = g * s
        du_ref[...] = (da * gs).astype(BF16)
        dg_ref[...] = ((da * u_ref[...].astype(F32)) * (s + gs - gs * s)).astype(BF16)

    ospec = pl.BlockSpec((tm, tn), lambda j, i: (i, j))
    oshape = jax.ShapeDtypeStruct((T, D_FFP), BF16)
    return pl.pallas_call(
        body, name="ffn_mid_bwd", grid=(D_FFP // tn, T // tm),
        in_specs=[pl.BlockSpec((tm, D), lambda j, i: (i, 0)), pl.BlockSpec((tn, D), lambda j, i: (j, 0)),
                  ospec, ospec],
        out_specs=[ospec, ospec], out_shape=[oshape, oshape],
        compiler_params=_params("parallel", "parallel"),
    )(dyh, wout, g, u)


def _final(x, tgt, gf, copy_scale, tm=512):
    T = x.shape[0]
    tm = min(tm, T)

    def body(x_ref, t_ref, g_ref, dx_ref, dxb_ref, dg_ref, loss_ref):
        xv = x_ref[...]
        r = lax.rsqrt(jnp.mean(xv * xv, axis=-1, keepdims=True) + EPS)
        xhat = xv * r
        err = xhat * g_ref[...] - t_ref[...]
        dy = err * (1.0 / D)
        dxhat = dy * g_ref[...]
        dx = r * (dxhat - xhat * jnp.mean(dxhat * xhat, axis=-1, keepdims=True))
        dx_ref[...] = dx
        dxb_ref[...] = (copy_scale * dx).astype(BF16)

        @pl.when(pl.program_id(0) == 0)
        def _():
            dg_ref[...] = jnp.zeros_like(dg_ref)
            loss_ref[...] = jnp.zeros_like(loss_ref)

        dg_ref[...] += jnp.sum(dy * xhat, axis=0, keepdims=True)
        loss_ref[...] += (0.5 / D) * jnp.sum(err * err)

    row = pl.BlockSpec((tm, D), lambda i: (i, 0))
    vec = pl.BlockSpec((1, D), lambda i: (0, 0))
    return pl.pallas_call(
        body, name="final_loss", grid=(T // tm,),
        in_specs=[row, row, vec],
        out_specs=[row, row, vec, pl.BlockSpec((1, 128), lambda i: (0, 0))],
        out_shape=[jax.ShapeDtypeStruct((T, D), F32), jax.ShapeDtypeStruct((T, D), BF16),
                   jax.ShapeDtypeStruct((1, D), F32), jax.ShapeDtypeStruct((1, 128), F32)],
        compiler_params=_params("arbitrary"),
    )(x, tgt, gf)


def _mix_block(T, limit):
    return min(limit, T // 2)


def _rows(tb, width):
    return lax.broadcasted_iota(jnp.int32, (tb, width), 0)


def _rglru_gates(xc, wax_ref, bax_ref, lam_ref):
    pre = _dot(xc.astype(BF16), wax_ref[...]) + bax_ref[...]
    r = _sigmoid(pre[:, :D_RNN])
    ig = _sigmoid(pre[:, D_RNN:])
    z = -lam_ref[...]
    sp = jnp.maximum(z, 0.0) + jnp.log(1.0 + jnp.exp(-jnp.abs(z)))
    log_a = (-LRU_C) * r * sp
    a = jnp.exp(log_a)
    mult = jnp.sqrt(-jnp.tanh(log_a) * (1.0 + a * a))
    return r, ig, sp, a, mult


def _conv(xa_ext, cw_ref, cb_ref):
    y = cb_ref[...] + cw_ref[3:4, :] * xa_ext
    for k in range(1, 4):
        y = y + cw_ref[3 - k:4 - k, :] * pltpu.roll(xa_ext, k, 0)
    return y[HALO:]


def _pool_window_lanes():
    lane = lax.broadcasted_iota(jnp.int32, (1, D_POOL), 1)
    return jnp.where(lane < 64, 2, jnp.where(lane < 128, 4, jnp.where(lane < 192, 8, 16)))


def _pool_select(s2, s4, s8, s16):
    lane = lax.broadcasted_iota(jnp.int32, s2.shape, 1)
    return jnp.where(lane < 64, s2, jnp.where(lane < 128, s4, jnp.where(lane < 192, s8, s16)))


def _pool_diff(xp_ext, t0, tb):
    s2 = xp_ext + pltpu.roll(xp_ext, 1, 0)
    s4 = s2 + pltpu.roll(s2, 2, 0)
    s8 = s4 + pltpu.roll(s4, 4, 0)
    s16 = s8 + pltpu.roll(s8, 8, 0)
    sel = _pool_select(s2, s4, s8, s16)[HALO:]
    cnt = jnp.minimum(t0 + _rows(tb, D_POOL) + 1, _pool_window_lanes()).astype(F32)
    return sel / cnt - xp_ext[HALO:], cnt


def _head_masks():
    lane = lax.broadcasted_iota(jnp.int32, (1, D_SGU), 1)
    return [((lane >= 64 * h) & (lane < 64 * (h + 1))).astype(F32) for h in range(4)]


def _sgu_mix(w_ref, vch, masks):
    z = masks[0] * _dot(w_ref[0], vch)
    for h in range(1, 4):
        z = z + masks[h] * _dot(w_ref[h], vch)
    return z


def _mix_fwd(p, prm):
    T = p.shape[0]
    tb = _mix_block(T, 512)
    nb = T // tb

    def body(p_ref, xah_ref, xph_ref, cw_ref, cb_ref, wax_ref, bax_ref, lam_ref, wp_ref, ps_ref, sgn_ref,
             ws_ref, bz_ref, y_ref, hs_ref, carry_ref):
        i = pl.program_id(0)
        keep = (i > 0).astype(F32)

        @pl.when(i == 0)
        def _():
            carry_ref[...] = jnp.zeros_like(carry_ref)

        xa_ext = jnp.concatenate([xah_ref[...] * keep, p_ref[:, 512:1024]], axis=0)
        xc = _conv(xa_ext, cw_ref, cb_ref)
        r, ig, sp, a, mult = _rglru_gates(xc, wax_ref, bax_ref, lam_ref)
        bv = mult * (ig * xc)
        row = _rows(tb, D_RNN)
        s = 1
        while s < tb:
            m = row >= s
            bv = jnp.where(m, a * pltpu.roll(bv, s, 0) + bv, bv)
            a = jnp.where(m, a * pltpu.roll(a, s, 0), a)
            s *= 2
        h = bv + a * carry_ref[0:1, :]
        hs_ref[...] = h
        last = jnp.sum(jnp.where(_rows(8, D_RNN) == 7, hs_ref[tb - 8:tb, :], 0.0), axis=0, keepdims=True)
        carry_ref[...] = jnp.broadcast_to(last, carry_ref.shape)
        y_ref[:, 0:512] = (_gelu(p_ref[:, 0:512]) * h).astype(BF16)

        xp_ext = jnp.concatenate([xph_ref[...] * keep, p_ref[:, 1024:1280]], axis=0)
        d, _ = _pool_diff(xp_ext, i * tb, tb)
        y_ref[:, 512:768] = (_dot(d.astype(BF16), wp_ref[...]) * ps_ref[...]).astype(BF16)

        ug = _gelu(p_ref[:, 1280:1536])
        vg = _gelu(p_ref[:, 1536:1792])
        rv = lax.rsqrt(jnp.mean(vg * vg, axis=-1, keepdims=True) + EPS)
        vn = (vg * rv * sgn_ref[...]).astype(BF16)
        masks = _head_masks()
        for ci in range(tb // CHUNK):
            sl = slice(ci * CHUNK, (ci + 1) * CHUNK)
            z = _sgu_mix(ws_ref, vn[sl], masks) + bz_ref[...]
            y_ref[sl, 768:1024] = (ug[sl] * z).astype(BF16)

    hb = tb // HALO

    def halo(i):
        return jnp.maximum(i * hb - 1, 0)

    def full(shape):
        return pl.BlockSpec(shape, lambda i: (0,) * len(shape))

    return pl.pallas_call(
        body, name="mix_fwd", grid=(nb,),
        in_specs=[pl.BlockSpec((tb, D_IN), lambda i: (i, 0)),
                  pl.BlockSpec((HALO, D_RNN), lambda i: (halo(i), 1)),
                  pl.BlockSpec((HALO, D_POOL), lambda i: (halo(i), 4)),
                  full((4, D_RNN)), full((1, D_RNN)), full((D_RNN, 2 * D_RNN)), full((1, 2 * D_RNN)),
                  full((1, D_RNN)), full((D_POOL, D_POOL)), full((1, D_POOL)), full((1, D_SGU)),
                  full((4, CHUNK, CHUNK)), full((CHUNK, D_SGU))],
        out_specs=[pl.BlockSpec((tb, D), lambda i: (i, 0)), pl.BlockSpec((tb, D_RNN), lambda i: (i, 0))],
        out_shape=[jax.ShapeDtypeStruct((T, D), BF16), jax.ShapeDtypeStruct((T, D_RNN), F32)],
        scratch_shapes=[pltpu.VMEM((8, D_RNN), F32)],
        compiler_params=_params("arbitrary"),
    )(p, p, p, prm["conv_w"], prm["conv_b"], prm["wax"], prm["bax"], prm["lam"], prm["wpool"], prm["pool_scale"],
      prm["sgu_norm"], prm["ws"], prm["bz"])


def _mix_bwd(dy, p, hs, prm):
    T = p.shape[0]
    tb = _mix_block(T, 256)
    nb = T // tb
    hb = tb // HALO

    def body(dy_ref, p_ref, xah_ref, xph_ref, hs_ref, hsh_ref, cw_ref, cb_ref, wax_ref, waxt_ref, bax_ref,
             lam_ref, wp_ref, wpt_ref, ps_ref, sgn_ref, ws_ref, wst_ref, bz_ref,
             dp_ref, dcw_ref, dcb_ref, dwax_ref, dbax_ref, dlam_ref, dwp_ref, dps_ref, dsgn_ref, dws_ref,
             dbz_ref, gcarry_ref, xcfut_ref, mfut_ref):
        i = pl.program_id(0)
        bi = nb - 1 - i
        keep = (bi > 0).astype(F32)

        @pl.when(i == 0)
        def _():
            for ref in (dcw_ref, dcb_ref, dwax_ref, dbax_ref, dlam_ref, dwp_ref, dps_ref, dsgn_ref, dws_ref,
                        dbz_ref, gcarry_ref, xcfut_ref, mfut_ref):
                ref[...] = jnp.zeros_like(ref)

        xa_ext = jnp.concatenate([xah_ref[...] * keep, p_ref[:, 512:1024]], axis=0)
        xc = _conv(xa_ext, cw_ref, cb_ref)
        r, ig, sp, a, mult = _rglru_gates(xc, wax_ref, bax_ref, lam_ref)
        gg, dgg = _gelu_and_grad(p_ref[:, 0:512])
        dya = dy_ref[:, 0:512]
        dp_ref[:, 0:512] = (dya * hs_ref[...] * dgg).astype(BF16)
        row = _rows(tb, D_RNN)
        g = dya * gg + jnp.where(row == tb - 1, gcarry_ref[0:1, :], 0.0)
        al = pltpu.roll(a, tb - 1, 0)
        s = 1
        while s < tb:
            m = row < tb - s
            g = jnp.where(m, al * pltpu.roll(g, tb - s, 0) + g, g)
            al = jnp.where(m, al * pltpu.roll(al, tb - s, 0), al)
            s *= 2
        first = jnp.sum(jnp.where(_rows(8, D_RNN) == 0, (a * g)[0:8], 0.0), axis=0, keepdims=True)
        gcarry_ref[...] = jnp.broadcast_to(first, gcarry_ref.shape)
        hs_ext = jnp.concatenate([hsh_ref[...] * keep, hs_ref[...]], axis=0)
        h_prev = pltpu.roll(hs_ext, 1, 0)[HALO:]
        ix = ig * xc
        dlog_a = g * h_prev * a - (g * ix) * (a * a / mult)
        dlam_ref[...] += jnp.sum(dlog_a * r, axis=0, keepdims=True) * (LRU_C * _sigmoid(-lam_ref[...]))
        dpre_r = dlog_a * ((-LRU_C) * sp) * (r * (1.0 - r))
        dpre_i = (g * mult * xc) * (ig * (1.0 - ig))
        dpre = jnp.concatenate([dpre_r, dpre_i], axis=1)
        dbax_ref[...] += jnp.sum(dpre, axis=0, keepdims=True)
        dpre_b = dpre.astype(BF16)
        dwax_ref[...] += _dot_tn(xc.astype(BF16), dpre_b)
        dxc = g * mult * ig + _dot(dpre_b, waxt_ref[...])
        dcb_ref[...] += jnp.sum(dxc, axis=0, keepdims=True)
        for k in range(4):
            xs = xa_ext[HALO:] if k == 3 else pltpu.roll(xa_ext, 3 - k, 0)[HALO:]
            dcw_ref[k:k + 1, :] += jnp.sum(dxc * xs, axis=0, keepdims=True)
        dxc_ext = jnp.concatenate([dxc, xcfut_ref[...]], axis=0)
        n = tb + HALO
        dxa = cw_ref[3:4, :] * dxc_ext
        for k in range(1, 4):
            dxa = dxa + cw_ref[3 - k:4 - k, :] * pltpu.roll(dxc_ext, n - k, 0)
        dp_ref[:, 512:1024] = dxa[:tb].astype(BF16)
        xcfut_ref[...] = dxc[0:HALO]

        xp_ext = jnp.concatenate([xph_ref[...] * keep, p_ref[:, 1024:1280]], axis=0)
        d, cnt = _pool_diff(xp_ext, bi * tb, tb)
        db = d.astype(BF16)
        dyb = dy_ref[:, 512:768]
        dps_ref[...] += jnp.sum(dyb * _dot(db, wp_ref[...]), axis=0, keepdims=True)
        dq = (dyb * ps_ref[...]).astype(BF16)
        dwp_ref[...] += _dot_tn(db, dq)
        dd = _dot(dq, wpt_ref[...])
        mm = dd / cnt
        m_ext = jnp.concatenate([mm, mfut_ref[...]], axis=0)
        f2 = m_ext + pltpu.roll(m_ext, n - 1, 0)
        f4 = f2 + pltpu.roll(f2, n - 2, 0)
        f8 = f4 + pltpu.roll(f4, n - 4, 0)
        f16 = f8 + pltpu.roll(f8, n - 8, 0)
        dp_ref[:, 1024:1280] = (_pool_select(f2, f4, f8, f16)[:tb] - dd).astype(BF16)
        mfut_ref[...] = mm[0:HALO]

        ug, dug = _gelu_and_grad(p_ref[:, 1280:1536])
        vg, dvg = _gelu_and_grad(p_ref[:, 1536:1792])
        rv = lax.rsqrt(jnp.mean(vg * vg, axis=-1, keepdims=True) + EPS)
        vhat = vg * rv
        vn = (vhat * sgn_ref[...]).astype(BF16)
        dyc = dy_ref[:, 768:1024]
        masks = _head_masks()
        dz = dyc * ug
        dzb = dz.astype(BF16)
        dvn_parts = []
        for ci in range(tb // CHUNK):
            sl = slice(ci * CHUNK, (ci + 1) * CHUNK)
            z = _sgu_mix(ws_ref, vn[sl], masks) + bz_ref[...]
            dp_ref[sl, 1280:1536] = (dyc[sl] * z * dug[sl]).astype(BF16)
            dbz_ref[...] += dz[sl]
            for h in range(4):
                dws_ref[h] += _dot_nt((dz[sl] * masks[h]).astype(BF16), vn[sl])
            dvn_parts.append(_sgu_mix(wst_ref, dzb[sl], masks))
        dvn = jnp.concatenate(dvn_parts, axis=0)
        dsgn_ref[...] += jnp.sum(dvn * vhat, axis=0, keepdims=True)
        dvhat = dvn * sgn_ref[...]
        dvg_in = rv * (dvhat - vhat * jnp.mean(dvhat * vhat, axis=-1, keepdims=True))
        dp_ref[:, 1536:1792] = (dvg_in * dvg).astype(BF16)

        @pl.when(i == nb - 1)
        def _():
            tril = (lax.broadcasted_iota(jnp.int32, (CHUNK, CHUNK), 0)
                    >= lax.broadcasted_iota(jnp.int32, (CHUNK, CHUNK), 1)).astype(F32)
            for h in range(4):
                dws_ref[h] = dws_ref[h] * tril

    def blk(i):
        return nb - 1 - i

    def halo(i):
        return jnp.maximum(blk(i) * hb - 1, 0)

    def full(shape):
        return pl.BlockSpec(shape, lambda i: (0,) * len(shape))

    small_shapes = [(4, D_RNN), (1, D_RNN), (D_RNN, 2 * D_RNN), (1, 2 * D_RNN), (1, D_RNN), (D_POOL, D_POOL),
                    (1, D_POOL), (1, D_SGU), (4, CHUNK, CHUNK), (CHUNK, D_SGU)]
    outs = pl.pallas_call(
        body, name="mix_bwd", grid=(nb,),
        in_specs=[pl.BlockSpec((tb, D), lambda i: (blk(i), 0)),
                  pl.BlockSpec((tb, D_IN), lambda i: (blk(i), 0)),
                  pl.BlockSpec((HALO, D_RNN), lambda i: (halo(i), 1)),
                  pl.BlockSpec((HALO, D_POOL), lambda i: (halo(i), 4)),
                  pl.BlockSpec((tb, D_RNN), lambda i: (blk(i), 0)),
                  pl.BlockSpec((HALO, D_RNN), lambda i: (halo(i), 0)),
                  full((4, D_RNN)), full((1, D_RNN)), full((D_RNN, 2 * D_RNN)), full((2 * D_RNN, D_RNN)),
                  full((1, 2 * D_RNN)), full((1, D_RNN)), full((D_POOL, D_POOL)), full((D_POOL, D_POOL)),
                  full((1, D_POOL)), full((1, D_SGU)), full((4, CHUNK, CHUNK)), full((4, CHUNK, CHUNK)),
                  full((CHUNK, D_SGU))],
        out_specs=[pl.BlockSpec((tb, D_IN), lambda i: (blk(i), 0))] + [full(s) for s in small_shapes],
        out_shape=[jax.ShapeDtypeStruct((T, D_IN), BF16)] + [jax.ShapeDtypeStruct(s, F32) for s in small_shapes],
        scratch_shapes=[pltpu.VMEM((8, D_RNN), F32), pltpu.VMEM((HALO, D_RNN), F32),
                        pltpu.VMEM((HALO, D_POOL), F32)],
        compiler_params=_params("arbitrary"),
    )(dy, p, p, p, hs, hs, prm["conv_w"], prm["conv_b"], prm["wax"], prm["wax_t"], prm["bax"], prm["lam"],
      prm["wpool"], prm["wpool_t"], prm["pool_scale"], prm["sgu_norm"], prm["ws"], prm["ws_t"], prm["bz"])
    names = ("dp", "conv_w", "conv_b", "wax", "bax", "lam", "wpool", "pool_scale", "sgu_norm", "ws", "bz")
    return dict(zip(names, outs))


ANY = pl.BlockSpec(memory_space=pl.ANY)


def _place():
    x, y, c = lax.axis_index("x"), lax.axis_index("y"), lax.axis_index("c")
    return x, y, c


def _all_to_all(xs, name):
    def body(in_ref, out_ref, send_sems, recv_sems, local_sem):
        x, y, c = _place()
        me = 4 * x + 2 * y + c
        mine = pltpu.make_async_copy(in_ref.at[me], out_ref.at[me], local_sem)
        mine.start()
        copies = []
        for rel in range(1, N_DEV):
            tx = 1 - x if rel & 4 else x
            ty = 1 - y if rel & 2 else y
            tc = 1 - c if rel & 1 else c
            cp = pltpu.make_async_remote_copy(
                src_ref=in_ref.at[4 * tx + 2 * ty + tc], dst_ref=out_ref.at[me],
                send_sem=send_sems.at[rel - 1], recv_sem=recv_sems.at[rel - 1],
                device_id=(tx, ty, tc), device_id_type=MESH)
            cp.start()
            copies.append(cp)
        for cp in copies:
            cp.wait()
        mine.wait()

    return pl.pallas_call(
        body, name=name, in_specs=[ANY], out_specs=ANY,
        out_shape=jax.ShapeDtypeStruct(xs.shape, xs.dtype),
        scratch_shapes=[pltpu.SemaphoreType.DMA((N_DEV - 1,)), pltpu.SemaphoreType.DMA((N_DEV - 1,)),
                        pltpu.SemaphoreType.DMA],
    )(xs)


def _all_gather8(xs, name):
    def body(x_ref, out_ref, send_sems, recv_sems, local_sem):
        x, y, c = _place()
        me, sibling = (x, y, c), (x, y, 1 - c)
        chips = [(1 - x, y), (x, 1 - y), (1 - x, 1 - y)]

        def rows(px, py, pc):
            return out_ref.at[4 * px + 2 * py + pc]

        def copy(k, block, to, src=None):
            return pltpu.make_async_remote_copy(
                src_ref=rows(*block) if src is None else src, dst_ref=rows(*block),
                send_sem=send_sems.at[k], recv_sem=recv_sems.at[k], device_id=to, device_id_type=MESH)

        mine = pltpu.make_async_copy(x_ref, rows(*me), local_sem)
        mine.start()
        first = [copy(0, me, sibling, src=x_ref)]
        first += [copy(1 + j, me, (*chip, c), src=x_ref) for j, chip in enumerate(chips)]
        for cp in first:
            cp.start()
        passed = [copy(4 + j, (*chip, c), sibling) for j, chip in enumerate(chips)]
        for j, chip in enumerate(chips):
            copy(1 + j, (*chip, c), me).wait_recv()
            passed[j].start()
        copy(0, sibling, me).wait_recv()
        for j, chip in enumerate(chips):
            copy(4 + j, (*chip, 1 - c), me).wait_recv()
        for cp in first + passed:
            cp.wait_send()
        mine.wait()

    return pl.pallas_call(
        body, name=name, in_specs=[ANY], out_specs=ANY,
        out_shape=jax.ShapeDtypeStruct((N_DEV,) + xs.shape, xs.dtype),
        scratch_shapes=[pltpu.SemaphoreType.DMA((7,)), pltpu.SemaphoreType.DMA((7,)), pltpu.SemaphoreType.DMA],
    )(xs)


HBM = pl.BlockSpec(memory_space=pltpu.HBM)
SEM = pl.BlockSpec(memory_space=pltpu.SEMAPHORE)
EFFECT = pltpu.SideEffectType.DATAFLOW_SIDE_EFFECTING


def _in_hbm(a):
    return pltpu.with_memory_space_constraint(a, pltpu.HBM)


def _local_copy(src, dst, stage, sem):
    load = pltpu.make_async_copy(src, stage, sem)
    load.start()
    load.wait()
    store = pltpu.make_async_copy(stage, dst, sem)
    store.start()
    store.wait()


def _unique(windows):
    arrays = []
    for per_chip in windows:
        for arr, _ in per_chip:
            if not any(arr is a for a in arrays):
                arrays.append(arr)
    return arrays


def _exchange_start(layer, windows, lands, name):
    arrays = _unique(windows)
    na, nt = len(arrays), len(windows)

    def body(*refs):
        in_refs, land_refs = refs[:na], refs[na:na + nt]
        send_sems, recv_sems = refs[na + nt], refs[na + nt + 1]
        token = refs[-1]
        x, y, c = _place()
        me = 4 * x + 2 * y + c
        for t in range(nt):
            for j in range(N_CHIP):
                arr, window = windows[t][j]
                src = window(in_refs[next(i for i, a in enumerate(arrays) if a is arr)])

                @pl.when(me != 2 * j + layer)
                def _():
                    _start_in_pieces(dict(
                        src_ref=src, dst_ref=land_refs[t].at[me], send_sem=send_sems.at[N_CHIP * t + j],
                        recv_sem=recv_sems.at[N_DEV * t + me], device_id=(j // 2, j % 2, layer),
                        device_id_type=MESH))
        token[...] = jnp.zeros_like(token)

    outs = pl.pallas_call(
        body, name=name,
        out_shape=(pltpu.SemaphoreType.DMA((N_CHIP * nt,)), pltpu.SemaphoreType.DMA((N_DEV * nt,)),
                   *[pltpu.HBM(a.shape, a.dtype) for a in lands], jax.ShapeDtypeStruct((8, 128), F32)),
        in_specs=[HBM] * (na + nt),
        out_specs=(SEM, SEM, *[HBM] * nt, pl.BlockSpec(memory_space=pltpu.VMEM)),
        input_output_aliases={na + t: 2 + t for t in range(nt)},
        compiler_params=pltpu.CompilerParams(has_side_effects=EFFECT, vmem_limit_bytes=VMEM_LIMIT),
    )(*[_in_hbm(a) for a in arrays], *[_in_hbm(a) for a in lands])
    return outs[0], outs[1], list(outs[2:2 + nt]), outs[-1]


def _exchange_wait(layer, windows, lands, send_sems, recv_sems, after, name):
    arrays = _unique(windows)
    na, nt = len(arrays), len(windows)

    def body(*refs):
        in_refs, land_refs = refs[:na], refs[na:na + nt]
        send_sems, recv_sems = refs[na + nt], refs[na + nt + 1]
        stages, local_sem = refs[-1 - nt:-1], refs[-1]
        x, y, c = _place()
        me = 4 * x + 2 * y + c

        def source(t, j):
            arr, window = windows[t][j]
            return window(in_refs[next(i for i, a in enumerate(arrays) if a is arr)])

        @pl.when(c == layer)
        def _():
            for t in range(nt):
                for j in range(N_CHIP):
                    @pl.when(me == 2 * j + layer)
                    def _():
                        _local_copy(source(t, j), land_refs[t].at[me], stages[t], local_sem)

        for t in range(nt):
            for j in range(N_CHIP):
                @pl.when(me != 2 * j + layer)
                def _():
                    pltpu.make_async_remote_copy(
                        src_ref=source(t, j), dst_ref=land_refs[t].at[me], send_sem=send_sems.at[N_CHIP * t + j],
                        recv_sem=recv_sems.at[N_DEV * t + me], device_id=(j // 2, j % 2, layer),
                        device_id_type=MESH).wait_send()

        @pl.when(c == layer)
        def _():
            for t in range(nt):
                for s in range(N_DEV):
                    @pl.when(me != s)
                    def _():
                        slot = land_refs[t].at[s]
                        pltpu.make_async_remote_copy(
                            src_ref=slot, dst_ref=slot, send_sem=send_sems.at[N_CHIP * t],
                            recv_sem=recv_sems.at[N_DEV * t + s], device_id=(x, y, c),
                            device_id_type=MESH).wait_recv()

    outs = pl.pallas_call(
        body, name=name,
        out_shape=tuple(pltpu.HBM(a.shape, a.dtype) for a in lands),
        in_specs=[HBM] * (na + nt) + [SEM, SEM, ANY],
        out_specs=tuple([HBM] * nt),
        input_output_aliases={na + t: t for t in range(nt)},
        scratch_shapes=[pltpu.VMEM(a.shape[1:], a.dtype) for a in lands] + [pltpu.SemaphoreType.DMA],
        compiler_params=pltpu.CompilerParams(has_side_effects=EFFECT, vmem_limit_bytes=VMEM_LIMIT),
    )(*[_in_hbm(a) for a in arrays], *lands, send_sems, recv_sems, after)
    return list(outs)


def _other_chips(x, y):
    return [(1 - x, y), (x, 1 - y), (1 - x, 1 - y)]


def _split_rows(rows):
    if rows < 32:
        return [(0, rows), (rows, 0)]
    cut = -(-(rows // 2) // 16) * 16
    return [(0, cut), (cut, rows - cut)]


PIECES = 4


def _start_in_pieces(cp_args):
    src, dst = cp_args["src_ref"], cp_args["dst_ref"]
    rows = src.shape[0]
    step = -(-rows // (16 * PIECES)) * 16
    for r0 in range(0, rows, step):
        n = min(step, rows - r0)
        pltpu.make_async_remote_copy(**{**cp_args, "src_ref": src.at[pl.ds(r0, n)],
                                        "dst_ref": dst.at[pl.ds(r0, n)]}).start()


def _gather_copies(in_refs, land_refs, send_sems, recv_sems, x, y, sender_core, both):
    copies = []
    for t, src in enumerate(in_refs):
        r0, n = _split_rows(src.shape[0])[sender_core]
        if n == 0:
            continue
        for rel, (tx, ty) in enumerate(_other_chips(x, y)):
            for tc in (0, 1) if both else (sender_core,):
                copies.append(dict(
                    src_ref=src.at[pl.ds(r0, n)], dst_ref=land_refs[t].at[2 * x + y].at[pl.ds(r0, n)],
                    send_sem=send_sems.at[6 * t + 2 * rel + tc],
                    recv_sem=recv_sems.at[2 * (3 * t + rel) + sender_core],
                    device_id=(tx, ty, tc), device_id_type=MESH))
    return copies


def _gather_start(shards, lands, name, both):
    nt = len(shards)

    def body(*refs):
        in_refs, land_refs = refs[:nt], refs[nt:2 * nt]
        send_sems, recv_sems = refs[2 * nt], refs[2 * nt + 1]
        token = refs[-1]
        x, y, c = _place()
        for core in range(2):
            @pl.when(c == core)
            def _():
                for cp_args in _gather_copies(in_refs, land_refs, send_sems, recv_sems, x, y, core, both):
                    _start_in_pieces(cp_args)
        token[...] = jnp.zeros_like(token)

    outs = pl.pallas_call(
        body, name=name,
        out_shape=(pltpu.SemaphoreType.DMA((6 * nt,)), pltpu.SemaphoreType.DMA((6 * nt,)),
                   *[pltpu.HBM(a.shape, a.dtype) for a in lands], jax.ShapeDtypeStruct((8, 128), F32)),
        in_specs=[HBM] * (2 * nt),
        out_specs=(SEM, SEM, *[HBM] * nt, pl.BlockSpec(memory_space=pltpu.VMEM)),
        input_output_aliases={nt + t: 2 + t for t in range(nt)},
        compiler_params=pltpu.CompilerParams(has_side_effects=EFFECT, vmem_limit_bytes=VMEM_LIMIT),
    )(*[_in_hbm(a) for a in shards], *[_in_hbm(a) for a in lands])
    return outs[0], outs[1], list(outs[2:2 + nt]), outs[-1]


def _gather_wait(shards, lands, send_sems, recv_sems, after, name, both):
    nt = len(shards)

    def body(*refs):
        in_refs, land_refs = refs[:nt], refs[nt:2 * nt]
        send_sems, recv_sems = refs[2 * nt], refs[2 * nt + 1]
        stages, local_sem = refs[-1 - nt:-1], refs[-1]
        x, y, c = _place()

        for t in range(nt):
            _local_copy(in_refs[t], land_refs[t].at[2 * x + y], stages[t], local_sem)

        for core in range(2):
            @pl.when(c == core)
            def _():
                for cp_args in _gather_copies(in_refs, land_refs, send_sems, recv_sems, x, y, core, both):
                    pltpu.make_async_remote_copy(**cp_args).wait_send()

        def wait_parts_from(core):
            for t in range(nt):
                r0, n = _split_rows(in_refs[t].shape[0])[core]
                for rel, (tx, ty) in enumerate(_other_chips(x, y)):
                    if n > 0:
                        part = land_refs[t].at[2 * tx + ty].at[pl.ds(r0, n)]
                        pltpu.make_async_remote_copy(
                            src_ref=part, dst_ref=part, send_sem=send_sems.at[6 * t],
                            recv_sem=recv_sems.at[2 * (3 * t + rel) + core], device_id=(x, y, c),
                            device_id_type=MESH).wait_recv()

        for core in range(2):
            if both:
                wait_parts_from(core)
            else:
                pl.when(c == core)(functools.partial(wait_parts_from, core))

    outs = pl.pallas_call(
        body, name=name,
        out_shape=tuple(pltpu.HBM(a.shape, a.dtype) for a in lands),
        in_specs=[HBM] * (2 * nt) + [SEM, SEM, ANY],
        out_specs=tuple([HBM] * nt),
        input_output_aliases={nt + t: t for t in range(nt)},
        scratch_shapes=[pltpu.VMEM(a.shape, a.dtype) for a in shards] + [pltpu.SemaphoreType.DMA],
        compiler_params=pltpu.CompilerParams(has_side_effects=EFFECT, vmem_limit_bytes=VMEM_LIMIT),
    )(*[_in_hbm(a) for a in shards], *lands, send_sems, recv_sems, after)
    return list(outs)


def _pair_share(lands, name):
    nt = len(lands)
    splits = [_split_rows(a.shape[1]) for a in lands]

    def body(*refs):
        land_refs = refs[:nt]
        stages = refs[2 * nt:3 * nt]
        load_sems, send_sems, recv_sems = refs[3 * nt:]
        x, y, c = _place()

        def parts(core):
            out = []
            for t in range(nt):
                r0, n = splits[t][core]
                for rel, (tx, ty) in enumerate(_other_chips(x, y)):
                    if n > 0:
                        out.append((3 * t + rel, stages[t].at[rel, pl.ds(0, n)],
                                    land_refs[t].at[2 * tx + ty].at[pl.ds(r0, n)]))
            return out

        def send(core):
            loads = [pltpu.make_async_copy(part, stage, load_sems.at[k]) for k, stage, part in parts(core)]
            for cp in loads:
                cp.start()
            pushes = []
            for cp, (k, stage, part) in zip(loads, parts(core)):
                cp.wait()
                push = pltpu.make_async_remote_copy(src_ref=stage, dst_ref=part, send_sem=send_sems.at[k],
                                                    recv_sem=recv_sems.at[k], device_id=(x, y, 1 - c),
                                                    device_id_type=MESH)
                push.start()
                pushes.append(push)
            for push in pushes:
                push.wait_send()
            for k, _, part in parts(1 - core):
                pltpu.make_async_remote_copy(src_ref=part, dst_ref=part, send_sem=send_sems.at[k],
                                             recv_sem=recv_sems.at[k], device_id=(x, y, 1 - c),
                                             device_id_type=MESH).wait_recv()

        for core in range(2):
            pl.when(c == core)(functools.partial(send, core))

    outs = pl.pallas_call(
        body, name=name, in_specs=[ANY] * nt, out_specs=[ANY] * nt,
        out_shape=[jax.ShapeDtypeStruct(a.shape, a.dtype) for a in lands],
        input_output_aliases={t: t for t in range(nt)},
        scratch_shapes=[pltpu.VMEM((3, max(n for _, n in sp), a.shape[2]), a.dtype) for a, sp in zip(lands, splits)]
        + [pltpu.SemaphoreType.DMA((3 * nt,))] * 3,
        compiler_params=pltpu.CompilerParams(vmem_limit_bytes=VMEM_LIMIT),
    )(*lands)
    return list(outs)


def _tie(a, token):
    def body(a_ref, token_ref, o_ref):
        pass

    return pl.pallas_call(
        body, name="tie", in_specs=[ANY, ANY], out_specs=ANY,
        out_shape=jax.ShapeDtypeStruct(a.shape, a.dtype), input_output_aliases={0: 0},
    )(a, token)


def _sum_share(xs, name):
    _, r, cols = xs.shape
    tr = max(t for t in range(16, min(r, 704) + 1, 16) if r % t == 0)
    nblk = r // tr

    def body(x_ref, out_ref, acc_ref, send_sems, local_sems, recv_sem):
        i = pl.program_id(0)
        slot = i % 2
        x, y, c = _place()

        def copies(s, blk):
            dst = out_ref.at[c, pl.ds(blk * tr, tr), :]
            loc = pltpu.make_async_copy(acc_ref.at[s], dst, local_sems.at[s])
            rem = pltpu.make_async_remote_copy(src_ref=acc_ref.at[s], dst_ref=dst, send_sem=send_sems.at[s],
                                               recv_sem=recv_sem, device_id=(x, y, 1 - c), device_id_type=MESH)
            return loc, rem

        @pl.when(i >= 2)
        def _():
            loc, rem = copies(slot, i - 2)
            loc.wait()
            rem.wait_send()

        acc = x_ref[0].astype(F32)
        for k in range(1, N_DEV):
            acc = acc + x_ref[k].astype(F32)
        acc_ref[slot] = acc
        loc, rem = copies(slot, i)
        loc.start()
        rem.start()

        @pl.when(i == nblk - 1)
        def _():
            for back in range(min(2, nblk)):
                blk = nblk - 1 - back
                loc, rem = copies(blk % 2, blk)
                loc.wait()
                rem.wait_send()
            theirs = out_ref.at[1 - c]
            pltpu.make_async_remote_copy(src_ref=theirs, dst_ref=theirs, send_sem=send_sems.at[0],
                                         recv_sem=recv_sem, device_id=(x, y, 1 - c),
                                         device_id_type=MESH).wait_recv()

    return pl.pallas_call(
        body, name=name, grid=(nblk,),
        in_specs=[pl.BlockSpec((N_DEV, tr, cols), lambda i: (0, i, 0))],
        out_specs=ANY,
        out_shape=jax.ShapeDtypeStruct((2, r, cols), F32),
        scratch_shapes=[pltpu.VMEM((2, tr, cols), F32), pltpu.SemaphoreType.DMA((2,)),
                        pltpu.SemaphoreType.DMA((2,)), pltpu.SemaphoreType.DMA],
        compiler_params=_params("arbitrary"),
    )(xs)


def _sum8(xs, name):
    _, r, cols = xs.shape
    tr = 8
    for cand in (256, 128, 64, 32, 16):
        if r % cand == 0:
            tr = cand
            break

    def body(x_ref, o_ref):
        acc = x_ref[0].astype(F32)
        for k in range(1, N_DEV):
            acc = acc + x_ref[k].astype(F32)
        o_ref[...] = acc

    return pl.pallas_call(
        body, name=name, grid=(r // tr,),
        in_specs=[pl.BlockSpec((N_DEV, tr, cols), lambda i: (0, i, 0))],
        out_specs=pl.BlockSpec((tr, cols), lambda i: (i, 0)),
        out_shape=jax.ShapeDtypeStruct((r, cols), F32),
        compiler_params=_params("parallel"),
    )(xs)


def _adamw(w, g, m, v, name):
    R, C = w.shape
    tr = max([t for t in range(8, min(R, 512) + 1, 8) if R % t == 0] or [R])
    c1 = 1.0 / (1.0 - ADAM_B1 ** ADAM_STEP)
    c2 = 1.0 / (1.0 - ADAM_B2 ** ADAM_STEP)

    def body(w_ref, g_ref, m_ref, v_ref, d_ref, nm_ref, nv_ref):
        gv = g_ref[...]
        nm = ADAM_B1 * m_ref[...] + (1.0 - ADAM_B1) * gv
        nv = ADAM_B2 * v_ref[...] + (1.0 - ADAM_B2) * (gv * gv)
        d_ref[...] = (-ADAM_LR) * ((nm * c1) / (jnp.sqrt(nv * c2) + ADAM_EPS) + ADAM_WD * w_ref[...])
        nm_ref[...] = nm
        nv_ref[...] = nv

    spec = pl.BlockSpec((tr, C), lambda i: (i, 0))
    shape = jax.ShapeDtypeStruct((R, C), F32)
    return pl.pallas_call(
        body, name=name, grid=(R // tr,), in_specs=[spec] * 4, out_specs=[spec] * 3, out_shape=[shape] * 3,
        compiler_params=_params("parallel"),
    )(w, g, m, v)


def _flat_rows(parts, rows):
    flat = jnp.concatenate([q.reshape(-1) for q in parts])
    flat = jnp.pad(flat, (0, rows * LANES - flat.shape[0]))
    return flat.reshape(rows, LANES)


def _round_up(n, m):
    return (n + m - 1) // m * m


def _block_diag(w):
    H, n, _ = w.shape
    eye = jnp.eye(H, dtype=w.dtype)
    return (eye[:, None, :, None] * w[:, :, None, :]).reshape(H * n, H * n)


def _diag_blocks(w, H, n):
    w4 = w.reshape(H, n, H, n)
    return jnp.stack([w4[h, :, h, :] for h in range(H)])


W_IN_T = ("ffn1_w_in", "w_in", "ffn2_w_in")


def _ffn_in_weights(g_in):
    zeros = jnp.zeros((FF_HALF - FF_SHARD, D), g_in.dtype)
    wg_t = jnp.concatenate([g_in[0], zeros, g_in[1], zeros], axis=0)
    wu_t = jnp.concatenate([g_in[2], zeros, g_in[3], zeros], axis=0)
    return wg_t, wu_t


def _ffn_out_weights(g_out):
    zeros = jnp.zeros((FF_HALF - FF_SHARD, D), g_out.dtype)
    return jnp.concatenate([g_out[0], g_out[1], zeros, g_out[2], g_out[3], zeros], axis=0)


LAND_SHAPES = {"ffn1_w_in": (FF_SHARD, D), "ffn1_w_out": (FF_ROWS, D), "w_in": (D_IN // N_CHIP, D),
               "w_out": (D // N_CHIP, D), "ffn2_w_in": (FF_SHARD, D), "ffn2_w_out": (FF_ROWS, D)}


def _rows_window(arr, start, size):
    return arr, lambda r: r.at[pl.ds(start, size), :]


def _w_in_grad_windows(dwg_t, dwu_t):
    return [_rows_window(dwg_t if j < 2 else dwu_t, (j % 2) * FF_HALF, FF_SHARD) for j in range(N_CHIP)]


def _w_out_grad_windows(dwout):
    return [_rows_window(dwout, (j // 2) * FF_HALF + (j % 2) * FF_ROWS, FF_ROWS) for j in range(N_CHIP)]


def _mix_grad_windows(dwin_t, dwo):
    win = [_rows_window(dwin_t, j * (D_IN // N_CHIP), D_IN // N_CHIP) for j in range(N_CHIP)]
    wo = [_rows_window(dwo, j * (D // N_CHIP), D // N_CHIP) for j in range(N_CHIP)]
    return win, wo


def kernel(x, ffn1_norm, ffn1_w_in, ffn1_w_out, mix_norm, w_in, conv_w, conv_b, rg_w_a, rg_b_a, rg_w_x, rg_b_x, lru_lambda, pool_w, pool_scale, sgu_norm, sgu_w, sgu_b, w_out, ffn2_norm, ffn2_w_in, ffn2_w_out, final_norm, loss_target, m_ffn1_norm, m_ffn1_w_in, m_ffn1_w_out, m_mix_norm, m_w_in, m_conv_w, m_conv_b, m_rg_w_a, m_rg_b_a, m_rg_w_x, m_rg_b_x, m_lru_lambda, m_pool_w, m_pool_scale, m_sgu_norm, m_sgu_w, m_sgu_b, m_w_out, m_ffn2_norm, m_ffn2_w_in, m_ffn2_w_out, m_final_norm, v_ffn1_norm, v_ffn1_w_in, v_ffn1_w_out, v_mix_norm, v_w_in, v_conv_w, v_conv_b, v_rg_w_a, v_rg_b_a, v_rg_w_x, v_rg_b_x, v_lru_lambda, v_pool_w, v_pool_scale, v_sgu_norm, v_sgu_w, v_sgu_b, v_w_out, v_ffn2_norm, v_ffn2_w_in, v_ffn2_w_out, v_final_norm):
    args = locals()
    W = {n: args[n] for n in WEIGHTS}
    M = {n: args["m_" + n] for n in WEIGHTS}
    V = {n: args["v_" + n] for n in WEIGHTS}
    depth = ffn1_norm.shape[0]
    T = x.shape[1]
    xi, yi, ci = _place()
    chip = 2 * xi + yi

    assert depth == 2, "core c of a chip sends and reduces layer c"
    groups = [(l, names) for l in range(depth)
              for names in (["ffn1_w_in"], ["ffn1_w_out"], ["w_in", "w_out", "ffn2_w_in", "ffn2_w_out"])]
    def stored(a, n):
        return jnp.swapaxes(a, 1, 2) if n in W_IN_T else a

    wb = {n: stored(W[n], n).astype(BF16) for n in BIG}
    groups[1][1].append("conv_w")
    conv_shard = conv_w.reshape(-1, conv_w.shape[-1])
    flights = {}

    def weights_start(k, dep=None):
        l, names = groups[k]
        shards = [conv_shard if n == "conv_w" else wb[n][l] for n in names]
        if dep is not None:
            shards[0] = _tie(shards[0], dep)
        lands = [lax.empty((N_CHIP,) + s.shape, s.dtype) for s in shards]
        send, recv, lands, token = _gather_start(shards, lands, "weights_start_%d" % k, k >= 2)
        flights[k] = (shards, lands, send, recv)
        return token

    def weights_wait(k, after):
        l, names = groups[k]
        shards, lands, send, recv = flights[k]
        got = _gather_wait(shards, lands, send, recv, after, "weights_wait_%d" % k, k >= 2)
        token = weights_start(k + 2, got[0]) if k + 2 < len(groups) else None
        if k < 2:
            got = _pair_share(got, "weights_share_%d" % k)
        return dict(zip(names, got)), token

    def after_start(a, token):
        return a if token is None else _tie(a, token)

    first_tokens = [weights_start(0), weights_start(1)]

    layers = []
    for l in range(depth):
        L = {f: dict(norm=W[f + "_norm"][l][None]) for f in ("ffn1", "ffn2")}
        ws = jnp.where(jnp.tril(jnp.ones((CHUNK, CHUNK), bool))[None], sgu_w[l], 0.0)
        wax = jnp.concatenate([_block_diag(rg_w_a[l]), _block_diag(rg_w_x[l])], axis=1)
        wpool = _block_diag(pool_w[l])
        L["mix"] = dict(
            conv_b=conv_b[l][None], wax=wax.astype(BF16), wax_t=wax.T.astype(BF16),
            bax=jnp.concatenate([rg_b_a[l].reshape(-1), rg_b_x[l].reshape(-1)])[None], lam=lru_lambda[l][None],
            wpool=wpool.astype(BF16), wpool_t=wpool.T.astype(BF16), pool_scale=pool_scale[l][None],
            sgu_norm=sgu_norm[l][None], ws=ws.astype(BF16), ws_t=jnp.swapaxes(ws, 1, 2).astype(BF16),
            bz=jnp.repeat(sgu_b[l].T, 64, axis=1))
        L["mix_norm"] = mix_norm[l][None]
        layers.append(L)
    for token in first_tokens:
        layers[0]["ffn1"]["norm"] = _tie(layers[0]["ffn1"]["norm"], token)

    xs = x[0]
    saved = []
    for l, L in enumerate(layers):
        F1, F2 = L["ffn1"], L["ffn2"]
        got, token = weights_wait(3 * l, xs)
        F1["wg"], F1["wu"] = _ffn_in_weights(got["ffn1_w_in"])
        F1["norm"] = after_start(F1["norm"], token)
        h, g, u, a = _ffn_in(xs, F1["norm"], F1["wg"], F1["wu"])
        got, token = weights_wait(3 * l + 1, a)
        F1["wout"] = _ffn_out_weights(got["ffn1_w_out"])
        if "conv_w" in got:
            conv_full = jnp.concatenate([got["conv_w"][j] for j in range(N_CHIP)], axis=1)
            for ll in range(depth):
                layers[ll]["mix"]["conv_w"] = conv_full.reshape(depth, 4, D_RNN)[ll]
        a = after_start(a, token)
        x1 = _mm_res(a, F1["wout"], xs, 0.5, "ffn_out", tm=512, tn=D)
        s1 = (xs, h, g, u, a)
        got, token = weights_wait(3 * l + 2, x1)
        L["w_in"] = got["w_in"].reshape(D_IN, D)
        L["w_out"] = got["w_out"].reshape(D, D)
        F2["wg"], F2["wu"] = _ffn_in_weights(got["ffn2_w_in"])
        F2["wout"] = _ffn_out_weights(got["ffn2_w_out"])
        L["mix_norm"] = after_start(L["mix_norm"], token)
        hm, p = _mix_in(x1, L["mix_norm"], L["w_in"])
        ycat, hs = _mix_fwd(p, L["mix"])
        x2 = _mm_res(ycat, L["w_out"], x1, 1.0, "mix_out", tm=512, tn=D)
        h2, g2, u2, a2 = _ffn_in(x2, F2["norm"], F2["wg"], F2["wu"])
        x3 = _mm_res(a2, F2["wout"], x2, 0.5, "ffn_out", tm=512, tn=D)
        saved.append((s1, (x1, hm, p, ycat, hs), (x2, h2, g2, u2, a2)))
        xs = x3

    dx, dxb, d_final, loss_part = _final(xs, loss_target[0], final_norm[None], 0.5)

    G = {n: [None] * depth for n in SMALL if n != "final_norm"}
    lands = {n: lax.empty((N_DEV,) + LAND_SHAPES[n], BF16) for n in BIG}
    in_flight = []

    def send_grads(l, windows, tag):
        names = list(windows)
        send, recv, thru, token = _exchange_start(l, [windows[n] for n in names], [lands[n] for n in names],
                                                  "grads_start_" + tag)
        lands.update(zip(names, thru))
        in_flight.append((l, names, [windows[n] for n in names], send, recv, "grads_wait_" + tag))
        return token

    def ffn_bwd(dx, dxb, F, s, f, l, pending, send_now):
        xin, h, g, u, a = s
        dwout = _mm_tn(a, dxb, 1.0, "ffn_dwout")
        if send_now:
            token = send_grads(l, {f + "_w_out": _w_out_grad_windows(dwout)}, "l%d_%s_out" % (l, f))
            dxb = _tie(dxb, token)
        else:
            pending[f + "_w_out"] = _w_out_grad_windows(dwout)
        dg, du = _ffn_mid_bwd(dxb, F["wout"], g, u)
        dwg = _mm_tn(dg, h, 1.0, "ffn_dwg")
        dwu = _mm_tn(du, h, 1.0, "ffn_dwu")
        pending[f + "_w_in"] = _w_in_grad_windows(dwg, dwu)
        if send_now:
            dg = _tie(dg, send_grads(l, pending, "l%d_%s" % (l, f)))
        dx, dxb, dn = _dh_rms_bwd([(dg, F["wg"]), (du, F["wu"])], xin, F["norm"], dx,
                                  1.0 if f == "ffn2" else 0.5, "ffn_dh")
        G[f + "_norm"][l] = dn[0]
        return dx, dxb

    for l in reversed(range(depth)):
        L = layers[l]
        s1, (x1, hm, p, ycat, hs), s2 = saved[l]
        pending = {}
        dx, dxb = ffn_bwd(dx, dxb, L["ffn2"], s2, "ffn2", l, pending, l == 0)
        if l == 0:
            pending = {}
        dycat = _mm_nt(dxb, L["w_out"], "mix_dy", tm=512, tn=D)
        dwo = _mm_tn(ycat, dxb, 1.0, "mix_dwout")
        mg = _mix_bwd(dycat, p, hs, L["mix"])
        dwin = _mm_tn(mg["dp"], hm, 1.0, "mix_dwin")
        pending["w_in"], pending["w_out"] = _mix_grad_windows(dwin, dwo)
        if l == 0:
            dp = _tie(mg["dp"], send_grads(l, pending, "l0_mix"))
            pending = {}
        else:
            dp = mg["dp"]
        dx, dxb, dn = _dh_rms_bwd([(dp, L["w_in"])], x1, L["mix_norm"], dx, 0.5, "mix_dh")
        G["mix_norm"][l] = dn[0]
        G["conv_w"][l], G["conv_b"][l] = mg["conv_w"], mg["conv_b"][0]
        G["rg_w_a"][l] = _diag_blocks(mg["wax"][:, :D_RNN], 8, 64)
        G["rg_w_x"][l] = _diag_blocks(mg["wax"][:, D_RNN:], 8, 64)
        G["rg_b_a"][l] = mg["bax"][0, :D_RNN].reshape(8, 64)
        G["rg_b_x"][l] = mg["bax"][0, D_RNN:].reshape(8, 64)
        G["lru_lambda"][l] = mg["lam"][0]
        G["pool_w"][l] = _diag_blocks(mg["wpool"], 4, 64)
        G["pool_scale"][l], G["sgu_norm"][l] = mg["pool_scale"][0], mg["sgu_norm"][0]
        G["sgu_w"][l] = mg["ws"]
        G["sgu_b"][l] = mg["bz"].reshape(CHUNK, 4, 64).sum(-1).T
        dx, dxb = ffn_bwd(dx, dxb, L["ffn1"], s1, "ffn1", l, pending, l == 0)
        if l > 0:
            dxb = _tie(dxb, send_grads(l, pending, "l%d" % l))
    grad_x = dx[None]
    G = {n: jnp.stack(v) for n, v in G.items()}
    G["final_norm"] = d_final[0]

    for l, names, windows, send, recv, tag in in_flight:
        lands.update(zip(names, _exchange_wait(l, windows, [lands[n] for n in names], send, recv, dx, tag)))
    both = [_sum_share(lands[n], "sum_share_" + n) for n in BIG]
    grads = dict(zip(BIG, both))

    small_sizes = [int(np.prod(G[n].shape)) for n in SMALL]
    srows = _round_up(sum(small_sizes) + 1, N_DEV * 8 * LANES) // (N_DEV * LANES)
    sflat = _flat_rows([G[n] for n in SMALL] + [loss_part[0, :1]], N_DEV * srows)
    sgot = _all_to_all(sflat.reshape(N_DEV, srows, LANES), "exchange_small_grads")
    sall = _all_gather8(_sum8(sgot, "sum_small_grads"), "share_small_grads").reshape(-1)
    off = 0
    for n, size in zip(SMALL, small_sizes):
        grads[n] = sall[off:off + size].reshape(G[n].shape)
        off += size
    loss = sall[off]
    grads["conv_w"] = lax.dynamic_slice_in_dim(grads["conv_w"], chip * conv_w.shape[2], conv_w.shape[2], axis=2)

    delta, new_m, new_v = {}, {}, {}
    for n in BIG:
        shp = grads[n].shape
        two_d = (shp[0] * shp[1], shp[2])
        outs = _adamw(stored(W[n], n).reshape(two_d), grads[n].reshape(two_d), stored(M[n], n).reshape(two_d),
                      stored(V[n], n).reshape(two_d), "adamw_" + n)
        delta[n], new_m[n], new_v[n] = (stored(o.reshape(shp), n) for o in outs)
        grads[n] = stored(grads[n], n)
    arows = _round_up(sum(int(np.prod(W[n].shape)) for n in SMALL), 8 * LANES) // LANES
    outs = _adamw(*(_flat_rows([src[n] for n in SMALL], arows) for src in (W, grads, M, V)), "adamw_small")
    outs = [o.reshape(-1) for o in outs]
    off = 0
    for n in SMALL:
        size = int(np.prod(W[n].shape))
        delta[n], new_m[n], new_v[n] = (o[off:off + size].reshape(W[n].shape) for o in outs)
        off += size

    return (loss, grad_x, *[grads[n] for n in WEIGHTS], *[delta[n] for n in WEIGHTS],
            *[new_m[n] for n in WEIGHTS], *[new_v[n] for n in WEIGHTS])
```

```python
import functools
import math

import jax
import jax.numpy as jnp
import numpy as np
from jax import lax
from jax.experimental import pallas as pl
from jax.experimental.pallas import tpu as pltpu

F32 = jnp.float32
BF16 = jnp.bfloat16
MESH = pl.DeviceIdType.MESH

D = 1024
D_RNN = 512
D_POOL = 256
D_SGU = 256
D_IN = 1792
D_FF = 2752
D_FFP = 2816
N_CHIP = 4
FF_SHARD = D_FF // 2
FF_HALF = D_FFP // 2
FF_ROWS = D_FF // N_CHIP
CHUNK = 128
HALO = 16
EPS = 1e-6
LRU_C = 8.0
N_DEV = 8
LANES = 1024
VMEM_LIMIT = 56 * 1024 * 1024

ADAM_LR, ADAM_B1, ADAM_B2, ADAM_EPS, ADAM_WD, ADAM_STEP = 0.001, 0.9, 0.999, 1e-08, 0.01, 10

BIG = ("ffn1_w_in", "ffn1_w_out", "w_in", "w_out", "ffn2_w_in", "ffn2_w_out")
SMALL = ("ffn1_norm", "mix_norm", "conv_w", "conv_b", "rg_w_a", "rg_b_a", "rg_w_x", "rg_b_x", "lru_lambda",
         "pool_w", "pool_scale", "sgu_norm", "sgu_w", "sgu_b", "ffn2_norm", "final_norm")
WEIGHTS = ("ffn1_norm", "ffn1_w_in", "ffn1_w_out", "mix_norm", "w_in", "conv_w", "conv_b", "rg_w_a", "rg_b_a",
           "rg_w_x", "rg_b_x", "lru_lambda", "pool_w", "pool_scale", "sgu_norm", "sgu_w", "sgu_b", "w_out",
           "ffn2_norm", "ffn2_w_in", "ffn2_w_out", "final_norm")


def _params(*sem):
    return pltpu.CompilerParams(dimension_semantics=sem, vmem_limit_bytes=VMEM_LIMIT)


def _gelu(x):
    c = math.sqrt(2.0 / math.pi)
    t = jnp.tanh(c * (x + 0.044715 * (x * x * x)))
    return 0.5 * x * (1.0 + t)


def _gelu_and_grad(x):
    c = math.sqrt(2.0 / math.pi)
    x2 = x * x
    t = jnp.tanh(c * (x + 0.044715 * (x2 * x)))
    g = 0.5 * x * (1.0 + t)
    dg = 0.5 * (1.0 + t) + 0.5 * x * (1.0 - t * t) * (c * (1.0 + 3.0 * 0.044715 * x2))
    return g, dg


def _sigmoid(x):
    return 0.5 * jnp.tanh(0.5 * x) + 0.5


def _dot(a, b):
    return jnp.dot(a, b, preferred_element_type=F32)


def _dot_tn(a, b):
    return lax.dot_general(a, b, (((0,), (0,)), ((), ())), preferred_element_type=F32)


def _dot_nt(a, b):
    return lax.dot_general(a, b, (((1,), (1,)), ((), ())), preferred_element_type=F32)


def _tile(n, limit):
    best = 128
    for t in range(128, min(n, limit) + 1, 128):
        if n % t == 0:
            best = t
    assert n % best == 0, (n, limit)
    return best


def _mm_res(a, b, res, scale, name, tm=1024, tn=512):
    M, K = a.shape
    N = b.shape[1]
    tm, tn = min(tm, M), _tile(N, tn)

    def body(a_ref, b_ref, r_ref, o_ref):
        o_ref[...] = r_ref[...] + scale * _dot(a_ref[...], b_ref[...])

    return pl.pallas_call(
        body, name=name, grid=(M // tm, N // tn),
        in_specs=[pl.BlockSpec((tm, K), lambda i, j: (i, 0)), pl.BlockSpec((K, tn), lambda i, j: (0, j)),
                  pl.BlockSpec((tm, tn), lambda i, j: (i, j))],
        out_specs=pl.BlockSpec((tm, tn), lambda i, j: (i, j)),
        out_shape=jax.ShapeDtypeStruct((M, N), F32),
        compiler_params=_params("parallel", "parallel"),
    )(a, b, res)


def _mm_nt(a, b, name, tm=1024, tn=512):
    M, K = a.shape
    N = b.shape[0]
    tm, tn = min(tm, M), _tile(N, tn)

    def body(a_ref, b_ref, o_ref):
        o_ref[...] = _dot_nt(a_ref[...], b_ref[...])

    return pl.pallas_call(
        body, name=name, grid=(M // tm, N // tn),
        in_specs=[pl.BlockSpec((tm, K), lambda i, j: (i, 0)), pl.BlockSpec((tn, K), lambda i, j: (j, 0))],
        out_specs=pl.BlockSpec((tm, tn), lambda i, j: (i, j)),
        out_shape=jax.ShapeDtypeStruct((M, N), F32),
        compiler_params=_params("parallel", "parallel"),
    )(a, b)


def _dh_rms_bwd(pairs, x, g, dres, copy_scale, name, tm=512):
    T = x.shape[0]
    tm = min(tm, T)
    n = len(pairs)

    def body(*refs):
        ab = refs[:2 * n]
        x_ref, g_ref, dres_ref, dx_ref, dxb_ref, dg_ref = refs[2 * n:]
        dy = _dot(ab[0][...], ab[1][...])
        for k in range(1, n):
            dy = dy + _dot(ab[2 * k][...], ab[2 * k + 1][...])
        xv = x_ref[...]
        r = lax.rsqrt(jnp.mean(xv * xv, axis=-1, keepdims=True) + EPS)
        xhat = xv * r
        dxhat = dy * g_ref[...]
        dx = dres_ref[...] + r * (dxhat - xhat * jnp.mean(dxhat * xhat, axis=-1, keepdims=True))
        dx_ref[...] = dx
        dxb_ref[...] = (copy_scale * dx).astype(BF16)

        @pl.when(pl.program_id(0) == 0)
        def _():
            dg_ref[...] = jnp.zeros_like(dg_ref)

        dg_ref[...] += jnp.sum(dy * xhat, axis=0, keepdims=True)

    row = pl.BlockSpec((tm, D), lambda i: (i, 0))
    vec = pl.BlockSpec((1, D), lambda i: (0, 0))
    in_specs, operands = [], []
    for a, b in pairs:
        in_specs += [pl.BlockSpec((tm, a.shape[1]), lambda i: (i, 0)),
                     pl.BlockSpec(b.shape, lambda i: (0, 0), pipeline_mode=pl.Buffered(1))]
        operands += [a, b]
    return pl.pallas_call(
        body, name=name, grid=(T // tm,),
        in_specs=in_specs + [row, vec, row], out_specs=[row, row, vec],
        out_shape=[jax.ShapeDtypeStruct((T, D), F32), jax.ShapeDtypeStruct((T, D), BF16),
                   jax.ShapeDtypeStruct((1, D), F32)],
        compiler_params=_params("arbitrary"),
    )(*operands, x, g, dres)


def _mm_tn(a, b, scale, name, tm=1792, tn=1792, tk=2048):
    T, M = a.shape
    N = b.shape[1]
    tm, tn, tk = _tile(M, tm), _tile(N, tn), min(tk, T)
    nk = T // tk

    def body(a_ref, b_ref, o_ref, acc_ref):
        k = pl.program_id(2)

        @pl.when(k == 0)
        def _():
            acc_ref[...] = jnp.zeros_like(acc_ref)

        acc_ref[...] += _dot_tn(a_ref[...], b_ref[...])

        @pl.when(k == nk - 1)
        def _():
            o_ref[...] = (scale * acc_ref[...]).astype(BF16)

    return pl.pallas_call(
        body, name=name, grid=(M // tm, N // tn, nk),
        in_specs=[pl.BlockSpec((tk, tm), lambda i, j, k: (k, i)), pl.BlockSpec((tk, tn), lambda i, j, k: (k, j))],
        out_specs=pl.BlockSpec((tm, tn), lambda i, j, k: (i, j)),
        out_shape=jax.ShapeDtypeStruct((M, N), BF16),
        scratch_shapes=[pltpu.VMEM((tm, tn), F32)],
        compiler_params=_params("parallel", "parallel", "arbitrary"),
    )(a, b)


def _rms_rows(x_ref, gain_ref):
    xv = x_ref[...]
    r = lax.rsqrt(jnp.mean(xv * xv, axis=-1, keepdims=True) + EPS)
    return (xv * r * gain_ref[...]).astype(BF16)


def _ffn_in(x, gain, wg_t, wu_t, tm=512, tn=FF_HALF):
    T = x.shape[0]
    tm = min(tm, T)

    def body(x_ref, gain_ref, wg_ref, wu_ref, h_ref, g_ref, u_ref, a_ref):
        hv = _rms_rows(x_ref, gain_ref)

        @pl.when(pl.program_id(0) == 0)
        def _():
            h_ref[...] = hv

        g = _dot_nt(hv, wg_ref[...])
        u = _dot_nt(hv, wu_ref[...])
        g_ref[...] = g.astype(BF16)
        u_ref[...] = u.astype(BF16)
        a_ref[...] = (g * _sigmoid(g) * u).astype(BF16)

    row = pl.BlockSpec((tm, D), lambda j, i: (i, 0))
    last = T // tm - 1
    h_spec = pl.BlockSpec((tm, D), lambda j, i: (jnp.where(j == 0, i, last), 0))
    wspec = pl.BlockSpec((tn, D), lambda j, i: (j, 0))
    ospec = pl.BlockSpec((tm, tn), lambda j, i: (i, j))
    oshape = jax.ShapeDtypeStruct((T, D_FFP), BF16)
    return pl.pallas_call(
        body, name="ffn_in", grid=(D_FFP // tn, T // tm),
        in_specs=[row, pl.BlockSpec((1, D), lambda j, i: (0, 0)), wspec, wspec],
        out_specs=[h_spec, ospec, ospec, ospec],
        out_shape=[jax.ShapeDtypeStruct((T, D), BF16), oshape, oshape, oshape],
        compiler_params=_params("arbitrary", "arbitrary"),
    )(x, gain, wg_t, wu_t)


def _mix_in(x, gain, w_in_t, tm=512):
    T = x.shape[0]
    tm = min(tm, T)

    def body(x_ref, gain_ref, w_ref, h_ref, p_ref):
        hv = _rms_rows(x_ref, gain_ref)
        h_ref[...] = hv
        p_ref[...] = _dot_nt(hv, w_ref[...])

    row = pl.BlockSpec((tm, D), lambda i: (i, 0))
    return pl.pallas_call(
        body, name="mix_in", grid=(T // tm,),
        in_specs=[row, pl.BlockSpec((1, D), lambda i: (0, 0)), pl.BlockSpec((D_IN, D), lambda i: (0, 0))],
        out_specs=[row, pl.BlockSpec((tm, D_IN), lambda i: (i, 0))],
        out_shape=[jax.ShapeDtypeStruct((T, D), BF16), jax.ShapeDtypeStruct((T, D_IN), F32)],
        compiler_params=_params("parallel"),
    )(x, gain, w_in_t)


def _ffn_mid_bwd(dyh, wout, g, u, tm=512, tn=FF_HALF):
    T = dyh.shape[0]
    tm = min(tm, T)

    def body(dy_ref, w_ref, g_ref, u_ref, dg_ref, du_ref):
        da = _dot_nt(dy_ref[...], w_ref[...])
        g = g_ref[...].astype(F32)
        s = _sigmoid(g)
        gs = g * s
        du_ref[...] = (da * gs).astype(BF16)
        dg_ref[...] = ((da * u_ref[...].astype(F32)) * (s + gs - gs * s)).astype(BF16)

    ospec = pl.BlockSpec((tm, tn), lambda j, i: (i, j))
    oshape = jax.ShapeDtypeStruct((T, D_FFP), BF16)
    return pl.pallas_call(
        body, name="ffn_mid_bwd", grid=(D_FFP // tn, T // tm),
        in_specs=[pl.BlockSpec((tm, D), lambda j, i: (i, 0)), pl.BlockSpec((tn, D), lambda j, i: (j, 0)),
                  ospec, ospec],
        out_specs=[ospec, ospec], out_shape=[oshape, oshape],
        compiler_params=_params("parallel", "parallel"),
    )(dyh, wout, g, u)


def _final(x, tgt, gf, copy_scale, tm=512):
    T = x.shape[0]
    tm = min(tm, T)

    def body(x_ref, t_ref, g_ref, dx_ref, dxb_ref, dg_ref, loss_ref):
        xv = x_ref[...]
        r = lax.rsqrt(jnp.mean(xv * xv, axis=-1, keepdims=True) + EPS)
        xhat = xv * r
        err = xhat * g_ref[...] - t_ref[...]
        dy = err * (1.0 / D)
        dxhat = dy * g_ref[...]
        dx = r * (dxhat - xhat * jnp.mean(dxhat * xhat, axis=-1, keepdims=True))
        dx_ref[...] = dx
        dxb_ref[...] = (copy_scale * dx).astype(BF16)

        @pl.when(pl.program_id(0) == 0)
        def _():
            dg_ref[...] = jnp.zeros_like(dg_ref)
            loss_ref[...] = jnp.zeros_like(loss_ref)

        dg_ref[...] += jnp.sum(dy * xhat, axis=0, keepdims=True)
        loss_ref[...] += (0.5 / D) * jnp.sum(err * err)

    row = pl.BlockSpec((tm, D), lambda i: (i, 0))
    vec = pl.BlockSpec((1, D), lambda i: (0, 0))
    return pl.pallas_call(
        body, name="final_loss", grid=(T // tm,),
        in_specs=[row, row, vec],
        out_specs=[row, row, vec, pl.BlockSpec((1, 128), lambda i: (0, 0))],
        out_shape=[jax.ShapeDtypeStruct((T, D), F32), jax.ShapeDtypeStruct((T, D), BF16),
                   jax.ShapeDtypeStruct((1, D), F32), jax.ShapeDtypeStruct((1, 128), F32)],
        compiler_params=_params("arbitrary"),
    )(x, tgt, gf)


def _mix_block(T, limit):
    return min(limit, T // 2)


def _rows(tb, width):
    return lax.broadcasted_iota(jnp.int32, (tb, width), 0)


def _rglru_gates(xc, wax_ref, bax_ref, lam_ref):
    pre = _dot(xc.astype(BF16), wax_ref[...]) + bax_ref[...]
    r = _sigmoid(pre[:, :D_RNN])
    ig = _sigmoid(pre[:, D_RNN:])
    z = -lam_ref[...]
    sp = jnp.maximum(z, 0.0) + jnp.log(1.0 + jnp.exp(-jnp.abs(z)))
    log_a = (-LRU_C) * r * sp
    a = jnp.exp(log_a)
    mult = jnp.sqrt(-jnp.tanh(log_a) * (1.0 + a * a))
    return r, ig, sp, a, mult


def _conv(xa_ext, cw_ref, cb_ref):
    y = cb_ref[...] + cw_ref[3:4, :] * xa_ext
    for k in range(1, 4):
        y = y + cw_ref[3 - k:4 - k, :] * pltpu.roll(xa_ext, k, 0)
    return y[HALO:]


def _pool_window_lanes():
    lane = lax.broadcasted_iota(jnp.int32, (1, D_POOL), 1)
    return jnp.where(lane < 64, 2, jnp.where(lane < 128, 4, jnp.where(lane < 192, 8, 16)))


def _pool_select(s2, s4, s8, s16):
    lane = lax.broadcasted_iota(jnp.int32, s2.shape, 1)
    return jnp.where(lane < 64, s2, jnp.where(lane < 128, s4, jnp.where(lane < 192, s8, s16)))


def _pool_diff(xp_ext, t0, tb):
    s2 = xp_ext + pltpu.roll(xp_ext, 1, 0)
    s4 = s2 + pltpu.roll(s2, 2, 0)
    s8 = s4 + pltpu.roll(s4, 4, 0)
    s16 = s8 + pltpu.roll(s8, 8, 0)
    sel = _pool_select(s2, s4, s8, s16)[HALO:]
    cnt = jnp.minimum(t0 + _rows(tb, D_POOL) + 1, _pool_window_lanes()).astype(F32)
    return sel / cnt - xp_ext[HALO:], cnt


def _head_masks():
    lane = lax.broadcasted_iota(jnp.int32, (1, D_SGU), 1)
    return [((lane >= 64 * h) & (lane < 64 * (h + 1))).astype(F32) for h in range(4)]


def _sgu_mix(w_ref, vch, masks):
    z = masks[0] * _dot(w_ref[0], vch)
    for h in range(1, 4):
        z = z + masks[h] * _dot(w_ref[h], vch)
    return z


def _mix_fwd(p, prm):
    T = p.shape[0]
    tb = _mix_block(T, 512)
    nb = T // tb

    def body(p_ref, xah_ref, xph_ref, cw_ref, cb_ref, wax_ref, bax_ref, lam_ref, wp_ref, ps_ref, sgn_ref,
             ws_ref, bz_ref, y_ref, hs_ref, carry_ref):
        i = pl.program_id(0)
        keep = (i > 0).astype(F32)

        @pl.when(i == 0)
        def _():
            carry_ref[...] = jnp.zeros_like(carry_ref)

        xa_ext = jnp.concatenate([xah_ref[...] * keep, p_ref[:, 512:1024]], axis=0)
        xc = _conv(xa_ext, cw_ref, cb_ref)
        r, ig, sp, a, mult = _rglru_gates(xc, wax_ref, bax_ref, lam_ref)
        bv = mult * (ig * xc)
        row = _rows(tb, D_RNN)
        s = 1
        while s < tb:
            m = row >= s
            bv = jnp.where(m, a * pltpu.roll(bv, s, 0) + bv, bv)
            a = jnp.where(m, a * pltpu.roll(a, s, 0), a)
            s *= 2
        h = bv + a * carry_ref[0:1, :]
        hs_ref[...] = h
        last = jnp.sum(jnp.where(_rows(8, D_RNN) == 7, hs_ref[tb - 8:tb, :], 0.0), axis=0, keepdims=True)
        carry_ref[...] = jnp.broadcast_to(last, carry_ref.shape)
        y_ref[:, 0:512] = (_gelu(p_ref[:, 0:512]) * h).astype(BF16)

        xp_ext = jnp.concatenate([xph_ref[...] * keep, p_ref[:, 1024:1280]], axis=0)
        d, _ = _pool_diff(xp_ext, i * tb, tb)
        y_ref[:, 512:768] = (_dot(d.astype(BF16), wp_ref[...]) * ps_ref[...]).astype(BF16)

        ug = _gelu(p_ref[:, 1280:1536])
        vg = _gelu(p_ref[:, 1536:1792])
        rv = lax.rsqrt(jnp.mean(vg * vg, axis=-1, keepdims=True) + EPS)
        vn = (vg * rv * sgn_ref[...]).astype(BF16)
        masks = _head_masks()
        for ci in range(tb // CHUNK):
            sl = slice(ci * CHUNK, (ci + 1) * CHUNK)
            z = _sgu_mix(ws_ref, vn[sl], masks) + bz_ref[...]
            y_ref[sl, 768:1024] = (ug[sl] * z).astype(BF16)

    hb = tb // HALO

    def halo(i):
        return jnp.maximum(i * hb - 1, 0)

    def full(shape):
        return pl.BlockSpec(shape, lambda i: (0,) * len(shape))

    return pl.pallas_call(
        body, name="mix_fwd", grid=(nb,),
        in_specs=[pl.BlockSpec((tb, D_IN), lambda i: (i, 0)),
                  pl.BlockSpec((HALO, D_RNN), lambda i: (halo(i), 1)),
                  pl.BlockSpec((HALO, D_POOL), lambda i: (halo(i), 4)),
                  full((4, D_RNN)), full((1, D_RNN)), full((D_RNN, 2 * D_RNN)), full((1, 2 * D_RNN)),
                  full((1, D_RNN)), full((D_POOL, D_POOL)), full((1, D_POOL)), full((1, D_SGU)),
                  full((4, CHUNK, CHUNK)), full((CHUNK, D_SGU))],
        out_specs=[pl.BlockSpec((tb, D), lambda i: (i, 0)), pl.BlockSpec((tb, D_RNN), lambda i: (i, 0))],
        out_shape=[jax.ShapeDtypeStruct((T, D), BF16), jax.ShapeDtypeStruct((T, D_RNN), F32)],
        scratch_shapes=[pltpu.VMEM((8, D_RNN), F32)],
        compiler_params=_params("arbitrary"),
    )(p, p, p, prm["conv_w"], prm["conv_b"], prm["wax"], prm["bax"], prm["lam"], prm["wpool"], prm["pool_scale"],
      prm["sgu_norm"], prm["ws"], prm["bz"])


def _mix_bwd(dy, p, hs, prm):
    T = p.shape[0]
    tb = _mix_block(T, 256)
    nb = T // tb
    hb = tb // HALO

    def body(dy_ref, p_ref, xah_ref, xph_ref, hs_ref, hsh_ref, cw_ref, cb_ref, wax_ref, waxt_ref, bax_ref,
             lam_ref, wp_ref, wpt_ref, ps_ref, sgn_ref, ws_ref, wst_ref, bz_ref,
             dp_ref, dcw_ref, dcb_ref, dwax_ref, dbax_ref, dlam_ref, dwp_ref, dps_ref, dsgn_ref, dws_ref,
             dbz_ref, gcarry_ref, xcfut_ref, mfut_ref):
        i = pl.program_id(0)
        bi = nb - 1 - i
        keep = (bi > 0).astype(F32)

        @pl.when(i == 0)
        def _():
            for ref in (dcw_ref, dcb_ref, dwax_ref, dbax_ref, dlam_ref, dwp_ref, dps_ref, dsgn_ref, dws_ref,
                        dbz_ref, gcarry_ref, xcfut_ref, mfut_ref):
                ref[...] = jnp.zeros_like(ref)

        xa_ext = jnp.concatenate([xah_ref[...] * keep, p_ref[:, 512:1024]], axis=0)
        xc = _conv(xa_ext, cw_ref, cb_ref)
        r, ig, sp, a, mult = _rglru_gates(xc, wax_ref, bax_ref, lam_ref)
        gg, dgg = _gelu_and_grad(p_ref[:, 0:512])
        dya = dy_ref[:, 0:512]
        dp_ref[:, 0:512] = (dya * hs_ref[...] * dgg).astype(BF16)
        row = _rows(tb, D_RNN)
        g = dya * gg + jnp.where(row == tb - 1, gcarry_ref[0:1, :], 0.0)
        al = pltpu.roll(a, tb - 1, 0)
        s = 1
        while s < tb:
            m = row < tb - s
            g = jnp.where(m, al * pltpu.roll(g, tb - s, 0) + g, g)
            al = jnp.where(m, al * pltpu.roll(al, tb - s, 0), al)
            s *= 2
        first = jnp.sum(jnp.where(_rows(8, D_RNN) == 0, (a * g)[0:8], 0.0), axis=0, keepdims=True)
        gcarry_ref[...] = jnp.broadcast_to(first, gcarry_ref.shape)
        hs_ext = jnp.concatenate([hsh_ref[...] * keep, hs_ref[...]], axis=0)
        h_prev = pltpu.roll(hs_ext, 1, 0)[HALO:]
        ix = ig * xc
        dlog_a = g * h_prev * a - (g * ix) * (a * a / mult)
        dlam_ref[...] += jnp.sum(dlog_a * r, axis=0, keepdims=True) * (LRU_C * _sigmoid(-lam_ref[...]))
        dpre_r = dlog_a * ((-LRU_C) * sp) * (r * (1.0 - r))
        dpre_i = (g * mult * xc) * (ig * (1.0 - ig))
        dpre = jnp.concatenate([dpre_r, dpre_i], axis=1)
        dbax_ref[...] += jnp.sum(dpre, axis=0, keepdims=True)
        dpre_b = dpre.astype(BF16)
        dwax_ref[...] += _dot_tn(xc.astype(BF16), dpre_b)
        dxc = g * mult * ig + _dot(dpre_b, waxt_ref[...])
        dcb_ref[...] += jnp.sum(dxc, axis=0, keepdims=True)
        for k in range(4):
            xs = xa_ext[HALO:] if k == 3 else pltpu.roll(xa_ext, 3 - k, 0)[HALO:]
            dcw_ref[k:k + 1, :] += jnp.sum(dxc * xs, axis=0, keepdims=True)
        dxc_ext = jnp.concatenate([dxc, xcfut_ref[...]], axis=0)
        n = tb + HALO
        dxa = cw_ref[3:4, :] * dxc_ext
        for k in range(1, 4):
            dxa = dxa + cw_ref[3 - k:4 - k, :] * pltpu.roll(dxc_ext, n - k, 0)
        dp_ref[:, 512:1024] = dxa[:tb].astype(BF16)
        xcfut_ref[...] = dxc[0:HALO]

        xp_ext = jnp.concatenate([xph_ref[...] * keep, p_ref[:, 1024:1280]], axis=0)
        d, cnt = _pool_diff(xp_ext, bi * tb, tb)
        db = d.astype(BF16)
        dyb = dy_ref[:, 512:768]
        dps_ref[...] += jnp.sum(dyb * _dot(db, wp_ref[...]), axis=0, keepdims=True)
        dq = (dyb * ps_ref[...]).astype(BF16)
        dwp_ref[...] += _dot_tn(db, dq)
        dd = _dot(dq, wpt_ref[...])
        mm = dd / cnt
        m_ext = jnp.concatenate([mm, mfut_ref[...]], axis=0)
        f2 = m_ext + pltpu.roll(m_ext, n - 1, 0)
        f4 = f2 + pltpu.roll(f2, n - 2, 0)
        f8 = f4 + pltpu.roll(f4, n - 4, 0)
        f16 = f8 + pltpu.roll(f8, n - 8, 0)
        dp_ref[:, 1024:1280] = (_pool_select(f2, f4, f8, f16)[:tb] - dd).astype(BF16)
        mfut_ref[...] = mm[0:HALO]

        ug, dug = _gelu_and_grad(p_ref[:, 1280:1536])
        vg, dvg = _gelu_and_grad(p_ref[:, 1536:1792])
        rv = lax.rsqrt(jnp.mean(vg * vg, axis=-1, keepdims=True) + EPS)
        vhat = vg * rv
        vn = (vhat * sgn_ref[...]).astype(BF16)
        dyc = dy_ref[:, 768:1024]
        masks = _head_masks()
        dz = dyc * ug
        dzb = dz.astype(BF16)
        dvn_parts = []
        for ci in range(tb // CHUNK):
            sl = slice(ci * CHUNK, (ci + 1) * CHUNK)
            z = _sgu_mix(ws_ref, vn[sl], masks) + bz_ref[...]
            dp_ref[sl, 1280:1536] = (dyc[sl] * z * dug[sl]).astype(BF16)
            dbz_ref[...] += dz[sl]
            for h in range(4):
                dws_ref[h] += _dot_nt((dz[sl] * masks[h]).astype(BF16), vn[sl])
            dvn_parts.append(_sgu_mix(wst_ref, dzb[sl], masks))
        dvn = jnp.concatenate(dvn_parts, axis=0)
        dsgn_ref[...] += jnp.sum(dvn * vhat, axis=0, keepdims=True)
        dvhat = dvn * sgn_ref[...]
        dvg_in = rv * (dvhat - vhat * jnp.mean(dvhat * vhat, axis=-1, keepdims=True))
        dp_ref[:, 1536:1792] = (dvg_in * dvg).astype(BF16)

        @pl.when(i == nb - 1)
        def _():
            tril = (lax.broadcasted_iota(jnp.int32, (CHUNK, CHUNK), 0)
                    >= lax.broadcasted_iota(jnp.int32, (CHUNK, CHUNK), 1)).astype(F32)
            for h in range(4):
                dws_ref[h] = dws_ref[h] * tril

    def blk(i):
        return nb - 1 - i

    def halo(i):
        return jnp.maximum(blk(i) * hb - 1, 0)

    def full(shape):
        return pl.BlockSpec(shape, lambda i: (0,) * len(shape))

    small_shapes = [(4, D_RNN), (1, D_RNN), (D_RNN, 2 * D_RNN), (1, 2 * D_RNN), (1, D_RNN), (D_POOL, D_POOL),
                    (1, D_POOL), (1, D_SGU), (4, CHUNK, CHUNK), (CHUNK, D_SGU)]
    outs = pl.pallas_call(
        body, name="mix_bwd", grid=(nb,),
        in_specs=[pl.BlockSpec((tb, D), lambda i: (blk(i), 0)),
                  pl.BlockSpec((tb, D_IN), lambda i: (blk(i), 0)),
                  pl.BlockSpec((HALO, D_RNN), lambda i: (halo(i), 1)),
                  pl.BlockSpec((HALO, D_POOL), lambda i: (halo(i), 4)),
                  pl.BlockSpec((tb, D_RNN), lambda i: (blk(i), 0)),
                  pl.BlockSpec((HALO, D_RNN), lambda i: (halo(i), 0)),
                  full((4, D_RNN)), full((1, D_RNN)), full((D_RNN, 2 * D_RNN)), full((2 * D_RNN, D_RNN)),
                  full((1, 2 * D_RNN)), full((1, D_RNN)), full((D_POOL, D_POOL)), full((D_POOL, D_POOL)),
                  full((1, D_POOL)), full((1, D_SGU)), full((4, CHUNK, CHUNK)), full((4, CHUNK, CHUNK)),
                  full((CHUNK, D_SGU))],
        out_specs=[pl.BlockSpec((tb, D_IN), lambda i: (blk(i), 0))] + [full(s) for s in small_shapes],
        out_shape=[jax.ShapeDtypeStruct((T, D_IN), BF16)] + [jax.ShapeDtypeStruct(s, F32) for s in small_shapes],
        scratch_shapes=[pltpu.VMEM((8, D_RNN), F32), pltpu.VMEM((HALO, D_RNN), F32),
                        pltpu.VMEM((HALO, D_POOL), F32)],
        compiler_params=_params("arbitrary"),
    )(dy, p, p, p, hs, hs, prm["conv_w"], prm["conv_b"], prm["wax"], prm["wax_t"], prm["bax"], prm["lam"],
      prm["wpool"], prm["wpool_t"], prm["pool_scale"], prm["sgu_norm"], prm["ws"], prm["ws_t"], prm["bz"])
    names = ("dp", "conv_w", "conv_b", "wax", "bax", "lam", "wpool", "pool_scale", "sgu_norm", "ws", "bz")
    return dict(zip(names, outs))


ANY = pl.BlockSpec(memory_space=pl.ANY)


def _place():
    x, y, c = lax.axis_index("x"), lax.axis_index("y"), lax.axis_index("c")
    return x, y, c


def _all_to_all(xs, name):
    def body(in_ref, out_ref, send_sems, recv_sems, local_sem):
        x, y, c = _place()
        me = 4 * x + 2 * y + c
        mine = pltpu.make_async_copy(in_ref.at[me], out_ref.at[me], local_sem)
        mine.start()
        copies = []
        for rel in range(1, N_DEV):
            tx = 1 - x if rel & 4 else x
            ty = 1 - y if rel & 2 else y
            tc = 1 - c if rel & 1 else c
            cp = pltpu.make_async_remote_copy(
                src_ref=in_ref.at[4 * tx + 2 * ty + tc], dst_ref=out_ref.at[me],
                send_sem=send_sems.at[rel - 1], recv_sem=recv_sems.at[rel - 1],
                device_id=(tx, ty, tc), device_id_type=MESH)
            cp.start()
            copies.append(cp)
        for cp in copies:
            cp.wait()
        mine.wait()

    return pl.pallas_call(
        body, name=name, in_specs=[ANY], out_specs=ANY,
        out_shape=jax.ShapeDtypeStruct(xs.shape, xs.dtype),
        scratch_shapes=[pltpu.SemaphoreType.DMA((N_DEV - 1,)), pltpu.SemaphoreType.DMA((N_DEV - 1,)),
                        pltpu.SemaphoreType.DMA],
    )(xs)


def _all_gather8(xs, name):
    def body(x_ref, out_ref, send_sems, recv_sems, local_sem):
        x, y, c = _place()
        me, sibling = (x, y, c), (x, y, 1 - c)
        chips = [(1 - x, y), (x, 1 - y), (1 - x, 1 - y)]

        def rows(px, py, pc):
            return out_ref.at[4 * px + 2 * py + pc]

        def copy(k, block, to, src=None):
            return pltpu.make_async_remote_copy(
                src_ref=rows(*block) if src is None else src, dst_ref=rows(*block),
                send_sem=send_sems.at[k], recv_sem=recv_sems.at[k], device_id=to, device_id_type=MESH)

        mine = pltpu.make_async_copy(x_ref, rows(*me), local_sem)
        mine.start()
        first = [copy(0, me, sibling, src=x_ref)]
        first += [copy(1 + j, me, (*chip, c), src=x_ref) for j, chip in enumerate(chips)]
        for cp in first:
            cp.start()
        passed = [copy(4 + j, (*chip, c), sibling) for j, chip in enumerate(chips)]
        for j, chip in enumerate(chips):
            copy(1 + j, (*chip, c), me).wait_recv()
            passed[j].start()
        copy(0, sibling, me).wait_recv()
        for j, chip in enumerate(chips):
            copy(4 + j, (*chip, 1 - c), me).wait_recv()
        for cp in first + passed:
            cp.wait_send()
        mine.wait()

    return pl.pallas_call(
        body, name=name, in_specs=[ANY], out_specs=ANY,
        out_shape=jax.ShapeDtypeStruct((N_DEV,) + xs.shape, xs.dtype),
        scratch_shapes=[pltpu.SemaphoreType.DMA((7,)), pltpu.SemaphoreType.DMA((7,)), pltpu.SemaphoreType.DMA],
    )(xs)


HBM = pl.BlockSpec(memory_space=pltpu.HBM)
SEM = pl.BlockSpec(memory_space=pltpu.SEMAPHORE)
EFFECT = pltpu.SideEffectType.DATAFLOW_SIDE_EFFECTING


def _in_hbm(a):
    return pltpu.with_memory_space_constraint(a, pltpu.HBM)


def _local_copy(src, dst, stage, sem):
    load = pltpu.make_async_copy(src, stage, sem)
    load.start()
    load.wait()
    store = pltpu.make_async_copy(stage, dst, sem)
    store.start()
    store.wait()


def _unique(windows):
    arrays = []
    for per_chip in windows:
        for arr, _ in per_chip:
            if not any(arr is a for a in arrays):
                arrays.append(arr)
    return arrays


def _exchange_start(layer, windows, lands, name):
    arrays = _unique(windows)
    na, nt = len(arrays), len(windows)

    def body(*refs):
        in_refs, land_refs = refs[:na], refs[na:na + nt]
        send_sems, recv_sems = refs[na + nt], refs[na + nt + 1]
        token = refs[-1]
        x, y, c = _place()
        me = 4 * x + 2 * y + c
        for t in range(nt):
            for j in range(N_CHIP):
                arr, window = windows[t][j]
                src = window(in_refs[next(i for i, a in enumerate(arrays) if a is arr)])

                @pl.when(me != 2 * j + layer)
                def _():
                    _start_in_pieces(dict(
                        src_ref=src, dst_ref=land_refs[t].at[me], send_sem=send_sems.at[N_CHIP * t + j],
                        recv_sem=recv_sems.at[N_DEV * t + me], device_id=(j // 2, j % 2, layer),
                        device_id_type=MESH))
        token[...] = jnp.zeros_like(token)

    outs = pl.pallas_call(
        body, name=name,
        out_shape=(pltpu.SemaphoreType.DMA((N_CHIP * nt,)), pltpu.SemaphoreType.DMA((N_DEV * nt,)),
                   *[pltpu.HBM(a.shape, a.dtype) for a in lands], jax.ShapeDtypeStruct((8, 128), F32)),
        in_specs=[HBM] * (na + nt),
        out_specs=(SEM, SEM, *[HBM] * nt, pl.BlockSpec(memory_space=pltpu.VMEM)),
        input_output_aliases={na + t: 2 + t for t in range(nt)},
        compiler_params=pltpu.CompilerParams(has_side_effects=EFFECT, vmem_limit_bytes=VMEM_LIMIT),
    )(*[_in_hbm(a) for a in arrays], *[_in_hbm(a) for a in lands])
    return outs[0], outs[1], list(outs[2:2 + nt]), outs[-1]


def _exchange_wait(layer, windows, lands, send_sems, recv_sems, after, name):
    arrays = _unique(windows)
    na, nt = len(arrays), len(windows)

    def body(*refs):
        in_refs, land_refs = refs[:na], refs[na:na + nt]
        send_sems, recv_sems = refs[na + nt], refs[na + nt + 1]
        stages, local_sem = refs[-1 - nt:-1], refs[-1]
        x, y, c = _place()
        me = 4 * x + 2 * y + c

        def source(t, j):
            arr, window = windows[t][j]
            return window(in_refs[next(i for i, a in enumerate(arrays) if a is arr)])

        @pl.when(c == layer)
        def _():
            for t in range(nt):
                for j in range(N_CHIP):
                    @pl.when(me == 2 * j + layer)
                    def _():
                        _local_copy(source(t, j), land_refs[t].at[me], stages[t], local_sem)

        for t in range(nt):
            for j in range(N_CHIP):
                @pl.when(me != 2 * j + layer)
                def _():
                    pltpu.make_async_remote_copy(
                        src_ref=source(t, j), dst_ref=land_refs[t].at[me], send_sem=send_sems.at[N_CHIP * t + j],
                        recv_sem=recv_sems.at[N_DEV * t + me], device_id=(j // 2, j % 2, layer),
                        device_id_type=MESH).wait_send()

        @pl.when(c == layer)
        def _():
            for t in range(nt):
                for s in range(N_DEV):
                    @pl.when(me != s)
                    def _():
                        slot = land_refs[t].at[s]
                        pltpu.make_async_remote_copy(
                            src_ref=slot, dst_ref=slot, send_sem=send_sems.at[N_CHIP * t],
                            recv_sem=recv_sems.at[N_DEV * t + s], device_id=(x, y, c),
                            device_id_type=MESH).wait_recv()

    outs = pl.pallas_call(
        body, name=name,
        out_shape=tuple(pltpu.HBM(a.shape, a.dtype) for a in lands),
        in_specs=[HBM] * (na + nt) + [SEM, SEM, ANY],
        out_specs=tuple([HBM] * nt),
        input_output_aliases={na + t: t for t in range(nt)},
        scratch_shapes=[pltpu.VMEM(a.shape[1:], a.dtype) for a in lands] + [pltpu.SemaphoreType.DMA],
        compiler_params=pltpu.CompilerParams(has_side_effects=EFFECT, vmem_limit_bytes=VMEM_LIMIT),
    )(*[_in_hbm(a) for a in arrays], *lands, send_sems, recv_sems, after)
    return list(outs)


def _other_chips(x, y):
    return [(1 - x, y), (x, 1 - y), (1 - x, 1 - y)]


def _split_rows(rows):
    if rows < 32:
        return [(0, rows), (rows, 0)]
    cut = -(-(rows // 2) // 16) * 16
    return [(0, cut), (cut, rows - cut)]


PIECES = 4


def _start_in_pieces(cp_args):
    src, dst = cp_args["src_ref"], cp_args["dst_ref"]
    rows = src.shape[0]
    step = -(-rows // (16 * PIECES)) * 16
    for r0 in range(0, rows, step):
        n = min(step, rows - r0)
        pltpu.make_async_remote_copy(**{**cp_args, "src_ref": src.at[pl.ds(r0, n)],
                                        "dst_ref": dst.at[pl.ds(r0, n)]}).start()


def _gather_copies(in_refs, land_refs, send_sems, recv_sems, x, y, sender_core, both):
    copies = []
    for t, src in enumerate(in_refs):
        r0, n = _split_rows(src.shape[0])[sender_core]
        if n == 0:
            continue
        for rel, (tx, ty) in enumerate(_other_chips(x, y)):
            for tc in (0, 1) if both else (sender_core,):
                copies.append(dict(
                    src_ref=src.at[pl.ds(r0, n)], dst_ref=land_refs[t].at[2 * x + y].at[pl.ds(r0, n)],
                    send_sem=send_sems.at[6 * t + 2 * rel + tc],
                    recv_sem=recv_sems.at[2 * (3 * t + rel) + sender_core],
                    device_id=(tx, ty, tc), device_id_type=MESH))
    return copies


def _gather_start(shards, lands, name, both):
    nt = len(shards)

    def body(*refs):
        in_refs, land_refs = refs[:nt], refs[nt:2 * nt]
        send_sems, recv_sems = refs[2 * nt], refs[2 * nt + 1]
        token = refs[-1]
        x, y, c = _place()
        for core in range(2):
            @pl.when(c == core)
            def _():
                for cp_args in _gather_copies(in_refs, land_refs, send_sems, recv_sems, x, y, core, both):
                    _start_in_pieces(cp_args)
        token[...] = jnp.zeros_like(token)

    outs = pl.pallas_call(
        body, name=name,
        out_shape=(pltpu.SemaphoreType.DMA((6 * nt,)), pltpu.SemaphoreType.DMA((6 * nt,)),
                   *[pltpu.HBM(a.shape, a.dtype) for a in lands], jax.ShapeDtypeStruct((8, 128), F32)),
        in_specs=[HBM] * (2 * nt),
        out_specs=(SEM, SEM, *[HBM] * nt, pl.BlockSpec(memory_space=pltpu.VMEM)),
        input_output_aliases={nt + t: 2 + t for t in range(nt)},
        compiler_params=pltpu.CompilerParams(has_side_effects=EFFECT, vmem_limit_bytes=VMEM_LIMIT),
    )(*[_in_hbm(a) for a in shards], *[_in_hbm(a) for a in lands])
    return outs[0], outs[1], list(outs[2:2 + nt]), outs[-1]


def _gather_wait(shards, lands, send_sems, recv_sems, after, name, both):
    nt = len(shards)

    def body(*refs):
        in_refs, land_refs = refs[:nt], refs[nt:2 * nt]
        send_sems, recv_sems = refs[2 * nt], refs[2 * nt + 1]
        stages, local_sem = refs[-1 - nt:-1], refs[-1]
        x, y, c = _place()

        for t in range(nt):
            _local_copy(in_refs[t], land_refs[t].at[2 * x + y], stages[t], local_sem)

        for core in range(2):
            @pl.when(c == core)
            def _():
                for cp_args in _gather_copies(in_refs, land_refs, send_sems, recv_sems, x, y, core, both):
                    pltpu.make_async_remote_copy(**cp_args).wait_send()

        def wait_parts_from(core):
            for t in range(nt):
                r0, n = _split_rows(in_refs[t].shape[0])[core]
                for rel, (tx, ty) in enumerate(_other_chips(x, y)):
                    if n > 0:
                        part = land_refs[t].at[2 * tx + ty].at[pl.ds(r0, n)]
                        pltpu.make_async_remote_copy(
                            src_ref=part, dst_ref=part, send_sem=send_sems.at[6 * t],
                            recv_sem=recv_sems.at[2 * (3 * t + rel) + core], device_id=(x, y, c),
                            device_id_type=MESH).wait_recv()

        for core in range(2):
            if both:
                wait_parts_from(core)
            else:
                pl.when(c == core)(functools.partial(wait_parts_from, core))

    outs = pl.pallas_call(
        body, name=name,
        out_shape=tuple(pltpu.HBM(a.shape, a.dtype) for a in lands),
        in_specs=[HBM] * (2 * nt) + [SEM, SEM, ANY],
        out_specs=tuple([HBM] * nt),
        input_output_aliases={nt + t: t for t in range(nt)},
        scratch_shapes=[pltpu.VMEM(a.shape, a.dtype) for a in shards] + [pltpu.SemaphoreType.DMA],
        compiler_params=pltpu.CompilerParams(has_side_effects=EFFECT, vmem_limit_bytes=VMEM_LIMIT),
    )(*[_in_hbm(a) for a in shards], *lands, send_sems, recv_sems, after)
    return list(outs)


def _pair_share(lands, name):
    nt = len(lands)
    splits = [_split_rows(a.shape[1]) for a in lands]

    def body(*refs):
        land_refs = refs[:nt]
        stages = refs[2 * nt:3 * nt]
        load_sems, send_sems, recv_sems = refs[3 * nt:]
        x, y, c = _place()

        def parts(core):
            out = []
            for t in range(nt):
                r0, n = splits[t][core]
                for rel, (tx, ty) in enumerate(_other_chips(x, y)):
                    if n > 0:
                        out.append((3 * t + rel, stages[t].at[rel, pl.ds(0, n)],
                                    land_refs[t].at[2 * tx + ty].at[pl.ds(r0, n)]))
            return out

        def send(core):
            loads = [pltpu.make_async_copy(part, stage, load_sems.at[k]) for k, stage, part in parts(core)]
            for cp in loads:
                cp.start()
            pushes = []
            for cp, (k, stage, part) in zip(loads, parts(core)):
                cp.wait()
                push = pltpu.make_async_remote_copy(src_ref=stage, dst_ref=part, send_sem=send_sems.at[k],
                                                    recv_sem=recv_sems.at[k], device_id=(x, y, 1 - c),
                                                    device_id_type=MESH)
                push.start()
                pushes.append(push)
            for push in pushes:
                push.wait_send()
            for k, _, part in parts(1 - core):
                pltpu.make_async_remote_copy(src_ref=part, dst_ref=part, send_sem=send_sems.at[k],
                                             recv_sem=recv_sems.at[k], device_id=(x, y, 1 - c),
                                             device_id_type=MESH).wait_recv()

        for core in range(2):
            pl.when(c == core)(functools.partial(send, core))

    outs = pl.pallas_call(
        body, name=name, in_specs=[ANY] * nt, out_specs=[ANY] * nt,
        out_shape=[jax.ShapeDtypeStruct(a.shape, a.dtype) for a in lands],
        input_output_aliases={t: t for t in range(nt)},
        scratch_shapes=[pltpu.VMEM((3, max(n for _, n in sp), a.shape[2]), a.dtype) for a, sp in zip(lands, splits)]
        + [pltpu.SemaphoreType.DMA((3 * nt,))] * 3,
        compiler_params=pltpu.CompilerParams(vmem_limit_bytes=VMEM_LIMIT),
    )(*lands)
    return list(outs)


def _tie(a, token):
    def body(a_ref, token_ref, o_ref):
        pass

    return pl.pallas_call(
        body, name="tie", in_specs=[ANY, ANY], out_specs=ANY,
        out_shape=jax.ShapeDtypeStruct(a.shape, a.dtype), input_output_aliases={0: 0},
    )(a, token)


def _sum_share(xs, name):
    _, r, cols = xs.shape
    tr = max(t for t in range(16, min(r, 704) + 1, 16) if r % t == 0)
    nblk = r // tr

    def body(x_ref, out_ref, acc_ref, send_sems, local_sems, recv_sem):
        i = pl.program_id(0)
        slot = i % 2
        x, y, c = _place()

        def copies(s, blk):
            dst = out_ref.at[c, pl.ds(blk * tr, tr), :]
            loc = pltpu.make_async_copy(acc_ref.at[s], dst, local_sems.at[s])
            rem = pltpu.make_async_remote_copy(src_ref=acc_ref.at[s], dst_ref=dst, send_sem=send_sems.at[s],
                                               recv_sem=recv_sem, device_id=(x, y, 1 - c), device_id_type=MESH)
            return loc, rem

        @pl.when(i >= 2)
        def _():
            loc, rem = copies(slot, i - 2)
            loc.wait()
            rem.wait_send()

        acc = x_ref[0].astype(F32)
        for k in range(1, N_DEV):
            acc = acc + x_ref[k].astype(F32)
        acc_ref[slot] = acc
        loc, rem = copies(slot, i)
        loc.start()
        rem.start()

        @pl.when(i == nblk - 1)
        def _():
            for back in range(min(2, nblk)):
                blk = nblk - 1 - back
                loc, rem = copies(blk % 2, blk)
                loc.wait()
                rem.wait_send()
            theirs = out_ref.at[1 - c]
            pltpu.make_async_remote_copy(src_ref=theirs, dst_ref=theirs, send_sem=send_sems.at[0],
                                         recv_sem=recv_sem, device_id=(x, y, 1 - c),
                                         device_id_type=MESH).wait_recv()

    return pl.pallas_call(
        body, name=name, grid=(nblk,),
        in_specs=[pl.BlockSpec((N_DEV, tr, cols), lambda i: (0, i, 0))],
        out_specs=ANY,
        out_shape=jax.ShapeDtypeStruct((2, r, cols), F32),
        scratch_shapes=[pltpu.VMEM((2, tr, cols), F32), pltpu.SemaphoreType.DMA((2,)),
                        pltpu.SemaphoreType.DMA((2,)), pltpu.SemaphoreType.DMA],
        compiler_params=_params("arbitrary"),
    )(xs)


def _sum8(xs, name):
    _, r, cols = xs.shape
    tr = 8
    for cand in (256, 128, 64, 32, 16):
        if r % cand == 0:
            tr = cand
            break

    def body(x_ref, o_ref):
        acc = x_ref[0].astype(F32)
        for k in range(1, N_DEV):
            acc = acc + x_ref[k].astype(F32)
        o_ref[...] = acc

    return pl.pallas_call(
        body, name=name, grid=(r // tr,),
        in_specs=[pl.BlockSpec((N_DEV, tr, cols), lambda i: (0, i, 0))],
        out_specs=pl.BlockSpec((tr, cols), lambda i: (i, 0)),
        out_shape=jax.ShapeDtypeStruct((r, cols), F32),
        compiler_params=_params("parallel"),
    )(xs)


def _adamw(w, g, m, v, name):
    R, C = w.shape
    tr = max([t for t in range(8, min(R, 512) + 1, 8) if R % t == 0] or [R])
    c1 = 1.0 / (1.0 - ADAM_B1 ** ADAM_STEP)
    c2 = 1.0 / (1.0 - ADAM_B2 ** ADAM_STEP)

    def body(w_ref, g_ref, m_ref, v_ref, d_ref, nm_ref, nv_ref):
        gv = g_ref[...]
        nm = ADAM_B1 * m_ref[...] + (1.0 - ADAM_B1) * gv
        nv = ADAM_B2 * v_ref[...] + (1.0 - ADAM_B2) * (gv * gv)
        d_ref[...] = (-ADAM_LR) * ((nm * c1) / (jnp.sqrt(nv * c2) + ADAM_EPS) + ADAM_WD * w_ref[...])
        nm_ref[...] = nm
        nv_ref[...] = nv

    spec = pl.BlockSpec((tr, C), lambda i: (i, 0))
    shape = jax.ShapeDtypeStruct((R, C), F32)
    return pl.pallas_call(
        body, name=name, grid=(R // tr,), in_specs=[spec] * 4, out_specs=[spec] * 3, out_shape=[shape] * 3,
        compiler_params=_params("parallel"),
    )(w, g, m, v)


def _flat_rows(parts, rows):
    flat = jnp.concatenate([q.reshape(-1) for q in parts])
    flat = jnp.pad(flat, (0, rows * LANES - flat.shape[0]))
    return flat.reshape(rows, LANES)


def _round_up(n, m):
    return (n + m - 1) // m * m


def _block_diag(w):
    H, n, _ = w.shape
    eye = jnp.eye(H, dtype=w.dtype)
    return (eye[:, None, :, None] * w[:, :, None, :]).reshape(H * n, H * n)


def _diag_blocks(w, H, n):
    w4 = w.reshape(H, n, H, n)
    return jnp.stack([w4[h, :, h, :] for h in range(H)])


W_IN_T = ("ffn1_w_in", "w_in", "ffn2_w_in")


def _ffn_in_weights(g_in):
    zeros = jnp.zeros((FF_HALF - FF_SHARD, D), g_in.dtype)
    wg_t = jnp.concatenate([g_in[0], zeros, g_in[1], zeros], axis=0)
    wu_t = jnp.concatenate([g_in[2], zeros, g_in[3], zeros], axis=0)
    return wg_t, wu_t


def _ffn_out_weights(g_out):
    zeros = jnp.zeros((FF_HALF - FF_SHARD, D), g_out.dtype)
    return jnp.concatenate([g_out[0], g_out[1], zeros, g_out[2], g_out[3], zeros], axis=0)


LAND_SHAPES = {"ffn1_w_in": (FF_SHARD, D), "ffn1_w_out": (FF_ROWS, D), "w_in": (D_IN // N_CHIP, D),
               "w_out": (D // N_CHIP, D), "ffn2_w_in": (FF_SHARD, D), "ffn2_w_out": (FF_ROWS, D)}


def _rows_window(arr, start, size):
    return arr, lambda r: r.at[pl.ds(start, size), :]


def _w_in_grad_windows(dwg_t, dwu_t):
    return [_rows_window(dwg_t if j < 2 else dwu_t, (j % 2) * FF_HALF, FF_SHARD) for j in range(N_CHIP)]


def _w_out_grad_windows(dwout):
    return [_rows_window(dwout, (j // 2) * FF_HALF + (j % 2) * FF_ROWS, FF_ROWS) for j in range(N_CHIP)]


def _mix_grad_windows(dwin_t, dwo):
    win = [_rows_window(dwin_t, j * (D_IN // N_CHIP), D_IN // N_CHIP) for j in range(N_CHIP)]
    wo = [_rows_window(dwo, j * (D // N_CHIP), D // N_CHIP) for j in range(N_CHIP)]
    return win, wo


def kernel(x, ffn1_norm, ffn1_w_in, ffn1_w_out, mix_norm, w_in, conv_w, conv_b, rg_w_a, rg_b_a, rg_w_x, rg_b_x, lru_lambda, pool_w, pool_scale, sgu_norm, sgu_w, sgu_b, w_out, ffn2_norm, ffn2_w_in, ffn2_w_out, final_norm, loss_target, m_ffn1_norm, m_ffn1_w_in, m_ffn1_w_out, m_mix_norm, m_w_in, m_conv_w, m_conv_b, m_rg_w_a, m_rg_b_a, m_rg_w_x, m_rg_b_x, m_lru_lambda, m_pool_w, m_pool_scale, m_sgu_norm, m_sgu_w, m_sgu_b, m_w_out, m_ffn2_norm, m_ffn2_w_in, m_ffn2_w_out, m_final_norm, v_ffn1_norm, v_ffn1_w_in, v_ffn1_w_out, v_mix_norm, v_w_in, v_conv_w, v_conv_b, v_rg_w_a, v_rg_b_a, v_rg_w_x, v_rg_b_x, v_lru_lambda, v_pool_w, v_pool_scale, v_sgu_norm, v_sgu_w, v_sgu_b, v_w_out, v_ffn2_norm, v_ffn2_w_in, v_ffn2_w_out, v_final_norm):
    args = locals()
    W = {n: args[n] for n in WEIGHTS}
    M = {n: args["m_" + n] for n in WEIGHTS}
    V = {n: args["v_" + n] for n in WEIGHTS}
    depth = ffn1_norm.shape[0]
    T = x.shape[1]
    xi, yi, ci = _place()
    chip = 2 * xi + yi

    assert depth == 2, "core c of a chip reduces layer c"
    groups = [(l, names) for l in range(depth)
              for names in (["ffn1_w_in"], ["ffn1_w_out", "w_in", "w_out"], ["ffn2_w_in", "ffn2_w_out"])]
    def stored(a, n):
        return jnp.swapaxes(a, 1, 2) if n in W_IN_T else a

    wb = {n: stored(W[n], n).astype(BF16) for n in BIG}
    groups[1][1].append("conv_w")
    conv_shard = conv_w.reshape(-1, conv_w.shape[-1])
    flights = {}

    def weights_start(k, dep=None):
        l, names = groups[k]
        shards = [conv_shard if n == "conv_w" else wb[n][l] for n in names]
        if dep is not None:
            shards[0] = _tie(shards[0], dep)
        lands = [lax.empty((N_CHIP,) + s.shape, s.dtype) for s in shards]
        send, recv, lands, token = _gather_start(shards, lands, "weights_start_%d" % k, k >= 2)
        flights[k] = (shards, lands, send, recv)
        return token

    def weights_wait(k, after):
        l, names = groups[k]
        shards, lands, send, recv = flights[k]
        got = _gather_wait(shards, lands, send, recv, after, "weights_wait_%d" % k, k >= 2)
        token = weights_start(k + 1, got[0]) if k + 1 < len(groups) else None
        if k < 2:
            got = _pair_share(got, "weights_share_%d" % k)
        return dict(zip(names, got)), token

    def after_start(a, token):
        return a if token is None else _tie(a, token)

    first_tokens = [weights_start(0)]

    layers = []
    for l in range(depth):
        L = {f: dict(norm=W[f + "_norm"][l][None]) for f in ("ffn1", "ffn2")}
        ws = jnp.where(jnp.tril(jnp.ones((CHUNK, CHUNK), bool))[None], sgu_w[l], 0.0)
        wax = jnp.concatenate([_block_diag(rg_w_a[l]), _block_diag(rg_w_x[l])], axis=1)
        wpool = _block_diag(pool_w[l])
        L["mix"] = dict(
            conv_b=conv_b[l][None], wax=wax.astype(BF16), wax_t=wax.T.astype(BF16),
            bax=jnp.concatenate([rg_b_a[l].reshape(-1), rg_b_x[l].reshape(-1)])[None], lam=lru_lambda[l][None],
            wpool=wpool.astype(BF16), wpool_t=wpool.T.astype(BF16), pool_scale=pool_scale[l][None],
            sgu_norm=sgu_norm[l][None], ws=ws.astype(BF16), ws_t=jnp.swapaxes(ws, 1, 2).astype(BF16),
            bz=jnp.repeat(sgu_b[l].T, 64, axis=1))
        L["mix_norm"] = mix_norm[l][None]
        layers.append(L)
    for token in first_tokens:
        layers[0]["ffn1"]["norm"] = _tie(layers[0]["ffn1"]["norm"], token)

    xs = x[0]
    saved = []
    for l, L in enumerate(layers):
        F1, F2 = L["ffn1"], L["ffn2"]
        got, token = weights_wait(3 * l, xs)
        F1["wg"], F1["wu"] = _ffn_in_weights(got["ffn1_w_in"])
        F1["norm"] = after_start(F1["norm"], token)
        h, g, u, a = _ffn_in(xs, F1["norm"], F1["wg"], F1["wu"])
        got, token = weights_wait(3 * l + 1, a)
        F1["wout"] = _ffn_out_weights(got["ffn1_w_out"])
        L["w_in"] = got["w_in"].reshape(D_IN, D)
        L["w_out"] = got["w_out"].reshape(D, D)
        if "conv_w" in got:
            conv_full = jnp.concatenate([got["conv_w"][j] for j in range(N_CHIP)], axis=1)
            for ll in range(depth):
                layers[ll]["mix"]["conv_w"] = conv_full.reshape(depth, 4, D_RNN)[ll]
        a = after_start(a, token)
        x1 = _mm_res(a, F1["wout"], xs, 0.5, "ffn_out", tm=512, tn=D)
        s1 = (xs, h, g, u, a)
        hm, p = _mix_in(x1, L["mix_norm"], L["w_in"])
        ycat, hs = _mix_fwd(p, L["mix"])
        x2 = _mm_res(ycat, L["w_out"], x1, 1.0, "mix_out", tm=512, tn=D)
        got, token = weights_wait(3 * l + 2, x2)
        F2["wg"], F2["wu"] = _ffn_in_weights(got["ffn2_w_in"])
        F2["wout"] = _ffn_out_weights(got["ffn2_w_out"])
        F2["norm"] = after_start(F2["norm"], token)
        h2, g2, u2, a2 = _ffn_in(x2, F2["norm"], F2["wg"], F2["wu"])
        x3 = _mm_res(a2, F2["wout"], x2, 0.5, "ffn_out", tm=512, tn=D)
        saved.append((s1, (x1, hm, p, ycat, hs), (x2, h2, g2, u2, a2)))
        xs = x3

    dx, dxb, d_final, loss_part = _final(xs, loss_target[0], final_norm[None], 0.5)

    G = {n: [None] * depth for n in SMALL if n != "final_norm"}
    lands = {n: lax.empty((N_DEV,) + LAND_SHAPES[n], BF16) for n in BIG}
    in_flight = []

    def send_grads(l, windows, tag):
        names = list(windows)
        send, recv, thru, token = _exchange_start(l, [windows[n] for n in names], [lands[n] for n in names],
                                                  "grads_start_" + tag)
        lands.update(zip(names, thru))
        in_flight.append((l, names, [windows[n] for n in names], send, recv, "grads_wait_" + tag))
        return token

    def ffn_bwd(dx, dxb, F, s, f, l, pending, send_now):
        xin, h, g, u, a = s
        dwout = _mm_tn(a, dxb, 1.0, "ffn_dwout")
        if send_now:
            token = send_grads(l, {f + "_w_out": _w_out_grad_windows(dwout)}, "l%d_%s_out" % (l, f))
            dxb = _tie(dxb, token)
        else:
            pending[f + "_w_out"] = _w_out_grad_windows(dwout)
        dg, du = _ffn_mid_bwd(dxb, F["wout"], g, u)
        dwg = _mm_tn(dg, h, 1.0, "ffn_dwg")
        dwu = _mm_tn(du, h, 1.0, "ffn_dwu")
        pending[f + "_w_in"] = _w_in_grad_windows(dwg, dwu)
        if send_now:
            dg = _tie(dg, send_grads(l, pending, "l%d_%s" % (l, f)))
        dx, dxb, dn = _dh_rms_bwd([(dg, F["wg"]), (du, F["wu"])], xin, F["norm"], dx,
                                  1.0 if f == "ffn2" else 0.5, "ffn_dh")
        G[f + "_norm"][l] = dn[0]
        return dx, dxb

    for l in reversed(range(depth)):
        L = layers[l]
        s1, (x1, hm, p, ycat, hs), s2 = saved[l]
        pending = {}
        dx, dxb = ffn_bwd(dx, dxb, L["ffn2"], s2, "ffn2", l, pending, l == 0)
        if l == 0:
            pending = {}
        dycat = _mm_nt(dxb, L["w_out"], "mix_dy", tm=512, tn=D)
        dwo = _mm_tn(ycat, dxb, 1.0, "mix_dwout")
        mg = _mix_bwd(dycat, p, hs, L["mix"])
        dwin = _mm_tn(mg["dp"], hm, 1.0, "mix_dwin")
        pending["w_in"], pending["w_out"] = _mix_grad_windows(dwin, dwo)
        if l == 0:
            dp = _tie(mg["dp"], send_grads(l, pending, "l0_mix"))
            pending = {}
        else:
            dp = mg["dp"]
        dx, dxb, dn = _dh_rms_bwd([(dp, L["w_in"])], x1, L["mix_norm"], dx, 0.5, "mix_dh")
        G["mix_norm"][l] = dn[0]
        G["conv_w"][l], G["conv_b"][l] = mg["conv_w"], mg["conv_b"][0]
        G["rg_w_a"][l] = _diag_blocks(mg["wax"][:, :D_RNN], 8, 64)
        G["rg_w_x"][l] = _diag_blocks(mg["wax"][:, D_RNN:], 8, 64)
        G["rg_b_a"][l] = mg["bax"][0, :D_RNN].reshape(8, 64)
        G["rg_b_x"][l] = mg["bax"][0, D_RNN:].reshape(8, 64)
        G["lru_lambda"][l] = mg["lam"][0]
        G["pool_w"][l] = _diag_blocks(mg["wpool"], 4, 64)
        G["pool_scale"][l], G["sgu_norm"][l] = mg["pool_scale"][0], mg["sgu_norm"][0]
        G["sgu_w"][l] = mg["ws"]
        G["sgu_b"][l] = mg["bz"].reshape(CHUNK, 4, 64).sum(-1).T
        dx, dxb = ffn_bwd(dx, dxb, L["ffn1"], s1, "ffn1", l, pending, l == 0)
        if l > 0:
            dxb = _tie(dxb, send_grads(l, pending, "l%d" % l))
    grad_x = dx[None]
    G = {n: jnp.stack(v) for n, v in G.items()}
    G["final_norm"] = d_final[0]

    for l, names, windows, send, recv, tag in in_flight:
        lands.update(zip(names, _exchange_wait(l, windows, [lands[n] for n in names], send, recv, dx, tag)))
    both = [_sum_share(lands[n], "sum_share_" + n) for n in BIG]
    grads = dict(zip(BIG, both))

    small_sizes = [int(np.prod(G[n].shape)) for n in SMALL]
    srows = _round_up(sum(small_sizes) + 1, N_DEV * 8 * LANES) // (N_DEV * LANES)
    sflat = _flat_rows([G[n] for n in SMALL] + [loss_part[0, :1]], N_DEV * srows)
    sgot = _all_to_all(sflat.reshape(N_DEV, srows, LANES), "exchange_small_grads")
    sall = _all_gather8(_sum8(sgot, "sum_small_grads"), "share_small_grads").reshape(-1)
    off = 0
    for n, size in zip(SMALL, small_sizes):
        grads[n] = sall[off:off + size].reshape(G[n].shape)
        off += size
    loss = sall[off]
    grads["conv_w"] = lax.dynamic_slice_in_dim(grads["conv_w"], chip * conv_w.shape[2], conv_w.shape[2], axis=2)

    delta, new_m, new_v = {}, {}, {}
    for n in BIG:
        shp = grads[n].shape
        two_d = (shp[0] * shp[1], shp[2])
        outs = _adamw(stored(W[n], n).reshape(two_d), grads[n].reshape(two_d), stored(M[n], n).reshape(two_d),
                      stored(V[n], n).reshape(two_d), "adamw_" + n)
        delta[n], new_m[n], new_v[n] = (stored(o.reshape(shp), n) for o in outs)
        grads[n] = stored(grads[n], n)
    arows = _round_up(sum(int(np.prod(W[n].shape)) for n in SMALL), 8 * LANES) // LANES
    outs = _adamw(*(_flat_rows([src[n] for n in SMALL], arows) for src in (W, grads, M, V)), "adamw_small")
    outs = [o.reshape(-1) for o in outs]
    off = 0
    for n in SMALL:
        size = int(np.prod(W[n].shape))
        delta[n], new_m[n], new_v[n] = (o[off:off + size].reshape(W[n].shape) for o in outs)
        off += size

    return (loss, grad_x, *[grads[n] for n in WEIGHTS], *[delta[n] for n in WEIGHTS],
            *[new_m[n] for n in WEIGHTS], *[new_v[n] for n in WEIGHTS])
```

```python
import functools
import math

import jax
import jax.numpy as jnp
import numpy as np
from jax import lax
from jax.experimental import pallas as pl
from jax.experimental.pallas import tpu as pltpu

F32 = jnp.float32
BF16 = jnp.bfloat16
MESH = pl.DeviceIdType.MESH

D = 1024
D_RNN = 512
D_POOL = 256
D_SGU = 256
D_IN = 1792
D_FF = 2752
D_FFP = 2816
N_CHIP = 4
FF_SHARD = D_FF // 2
FF_HALF = D_FFP // 2
FF_ROWS = D_FF // N_CHIP
CHUNK = 128
HALO = 16
EPS = 1e-6
LRU_C = 8.0
N_DEV = 8
LANES = 1024
VMEM_LIMIT = 56 * 1024 * 1024

ADAM_LR, ADAM_B1, ADAM_B2, ADAM_EPS, ADAM_WD, ADAM_STEP = 0.001, 0.9, 0.999, 1e-08, 0.01, 10

BIG = ("ffn1_w_in", "ffn1_w_out", "w_in", "w_out", "ffn2_w_in", "ffn2_w_out")
SMALL = ("ffn1_norm", "mix_norm", "conv_w", "conv_b", "rg_w_a", "rg_b_a", "rg_w_x", "rg_b_x", "lru_lambda",
         "pool_w", "pool_scale", "sgu_norm", "sgu_w", "sgu_b", "ffn2_norm", "final_norm")
WEIGHTS = ("ffn1_norm", "ffn1_w_in", "ffn1_w_out", "mix_norm", "w_in", "conv_w", "conv_b", "rg_w_a", "rg_b_a",
           "rg_w_x", "rg_b_x", "lru_lambda", "pool_w", "pool_scale", "sgu_norm", "sgu_w", "sgu_b", "w_out",
           "ffn2_norm", "ffn2_w_in", "ffn2_w_out", "final_norm")


def _params(*sem):
    return pltpu.CompilerParams(dimension_semantics=sem, vmem_limit_bytes=VMEM_LIMIT)


def _gelu(x):
    c = math.sqrt(2.0 / math.pi)
    t = jnp.tanh(c * (x + 0.044715 * (x * x * x)))
    return 0.5 * x * (1.0 + t)


def _gelu_and_grad(x):
    c = math.sqrt(2.0 / math.pi)
    x2 = x * x
    t = jnp.tanh(c * (x + 0.044715 * (x2 * x)))
    g = 0.5 * x * (1.0 + t)
    dg = 0.5 * (1.0 + t) + 0.5 * x * (1.0 - t * t) * (c * (1.0 + 3.0 * 0.044715 * x2))
    return g, dg


def _sigmoid(x):
    return 0.5 * jnp.tanh(0.5 * x) + 0.5


def _dot(a, b):
    return jnp.dot(a, b, preferred_element_type=F32)


def _dot_tn(a, b):
    return lax.dot_general(a, b, (((0,), (0,)), ((), ())), preferred_element_type=F32)


def _dot_nt(a, b):
    return lax.dot_general(a, b, (((1,), (1,)), ((), ())), preferred_element_type=F32)


def _tile(n, limit):
    best = 128
    for t in range(128, min(n, limit) + 1, 128):
        if n % t == 0:
            best = t
    assert n % best == 0, (n, limit)
    return best


def _mm_res(a, b, res, scale, name, tm=1024, tn=512):
    M, K = a.shape
    N = b.shape[1]
    tm, tn = min(tm, M), _tile(N, tn)

    def body(a_ref, b_ref, r_ref, o_ref):
        o_ref[...] = r_ref[...] + scale * _dot(a_ref[...], b_ref[...])

    return pl.pallas_call(
        body, name=name, grid=(M // tm, N // tn),
        in_specs=[pl.BlockSpec((tm, K), lambda i, j: (i, 0)), pl.BlockSpec((K, tn), lambda i, j: (0, j)),
                  pl.BlockSpec((tm, tn), lambda i, j: (i, j))],
        out_specs=pl.BlockSpec((tm, tn), lambda i, j: (i, j)),
        out_shape=jax.ShapeDtypeStruct((M, N), F32),
        compiler_params=_params("parallel", "parallel"),
    )(a, b, res)


def _mm_nt(a, b, name, tm=1024, tn=512):
    M, K = a.shape
    N = b.shape[0]
    tm, tn = min(tm, M), _tile(N, tn)

    def body(a_ref, b_ref, o_ref):
        o_ref[...] = _dot_nt(a_ref[...], b_ref[...])

    return pl.pallas_call(
        body, name=name, grid=(M // tm, N // tn),
        in_specs=[pl.BlockSpec((tm, K), lambda i, j: (i, 0)), pl.BlockSpec((tn, K), lambda i, j: (j, 0))],
        out_specs=pl.BlockSpec((tm, tn), lambda i, j: (i, j)),
        out_shape=jax.ShapeDtypeStruct((M, N), F32),
        compiler_params=_params("parallel", "parallel"),
    )(a, b)


def _dh_rms_bwd(pairs, x, g, dres, copy_scale, name, tm=512):
    T = x.shape[0]
    tm = min(tm, T)
    n = len(pairs)

    def body(*refs):
        ab = refs[:2 * n]
        x_ref, g_ref, dres_ref, dx_ref, dxb_ref, dg_ref = refs[2 * n:]
        dy = _dot(ab[0][...], ab[1][...])
        for k in range(1, n):
            dy = dy + _dot(ab[2 * k][...], ab[2 * k + 1][...])
        xv = x_ref[...]
        r = lax.rsqrt(jnp.mean(xv * xv, axis=-1, keepdims=True) + EPS)
        xhat = xv * r
        dxhat = dy * g_ref[...]
        dx = dres_ref[...] + r * (dxhat - xhat * jnp.mean(dxhat * xhat, axis=-1, keepdims=True))
        dx_ref[...] = dx
        dxb_ref[...] = (copy_scale * dx).astype(BF16)

        @pl.when(pl.program_id(0) == 0)
        def _():
            dg_ref[...] = jnp.zeros_like(dg_ref)

        dg_ref[...] += jnp.sum(dy * xhat, axis=0, keepdims=True)

    row = pl.BlockSpec((tm, D), lambda i: (i, 0))
    vec = pl.BlockSpec((1, D), lambda i: (0, 0))
    in_specs, operands = [], []
    for a, b in pairs:
        in_specs += [pl.BlockSpec((tm, a.shape[1]), lambda i: (i, 0)),
                     pl.BlockSpec(b.shape, lambda i: (0, 0), pipeline_mode=pl.Buffered(1))]
        operands += [a, b]
    return pl.pallas_call(
        body, name=name, grid=(T // tm,),
        in_specs=in_specs + [row, vec, row], out_specs=[row, row, vec],
        out_shape=[jax.ShapeDtypeStruct((T, D), F32), jax.ShapeDtypeStruct((T, D), BF16),
                   jax.ShapeDtypeStruct((1, D), F32)],
        compiler_params=_params("arbitrary"),
    )(*operands, x, g, dres)


def _mm_tn(a, b, scale, name, tm=1792, tn=1792, tk=2048):
    T, M = a.shape
    N = b.shape[1]
    tm, tn, tk = _tile(M, tm), _tile(N, tn), min(tk, T)
    nk = T // tk

    def body(a_ref, b_ref, o_ref, acc_ref):
        k = pl.program_id(2)

        @pl.when(k == 0)
        def _():
            acc_ref[...] = jnp.zeros_like(acc_ref)

        acc_ref[...] += _dot_tn(a_ref[...], b_ref[...])

        @pl.when(k == nk - 1)
        def _():
            o_ref[...] = (scale * acc_ref[...]).astype(BF16)

    return pl.pallas_call(
        body, name=name, grid=(M // tm, N // tn, nk),
        in_specs=[pl.BlockSpec((tk, tm), lambda i, j, k: (k, i)), pl.BlockSpec((tk, tn), lambda i, j, k: (k, j))],
        out_specs=pl.BlockSpec((tm, tn), lambda i, j, k: (i, j)),
        out_shape=jax.ShapeDtypeStruct((M, N), BF16),
        scratch_shapes=[pltpu.VMEM((tm, tn), F32)],
        compiler_params=_params("parallel", "parallel", "arbitrary"),
    )(a, b)


def _rms_rows(x_ref, gain_ref):
    xv = x_ref[...]
    r = lax.rsqrt(jnp.mean(xv * xv, axis=-1, keepdims=True) + EPS)
    return (xv * r * gain_ref[...]).astype(BF16)


def _ffn_in(x, gain, wg_t, wu_t, tm=512, tn=FF_HALF):
    T = x.shape[0]
    tm = min(tm, T)

    def body(x_ref, gain_ref, wg_ref, wu_ref, h_ref, g_ref, u_ref, a_ref):
        hv = _rms_rows(x_ref, gain_ref)

        @pl.when(pl.program_id(0) == 0)
        def _():
            h_ref[...] = hv

        g = _dot_nt(hv, wg_ref[...])
        u = _dot_nt(hv, wu_ref[...])
        g_ref[...] = g.astype(BF16)
        u_ref[...] = u.astype(BF16)
        a_ref[...] = (g * _sigmoid(g) * u).astype(BF16)

    row = pl.BlockSpec((tm, D), lambda j, i: (i, 0))
    last = T // tm - 1
    h_spec = pl.BlockSpec((tm, D), lambda j, i: (jnp.where(j == 0, i, last), 0))
    wspec = pl.BlockSpec((tn, D), lambda j, i: (j, 0))
    ospec = pl.BlockSpec((tm, tn), lambda j, i: (i, j))
    oshape = jax.ShapeDtypeStruct((T, D_FFP), BF16)
    return pl.pallas_call(
        body, name="ffn_in", grid=(D_FFP // tn, T // tm),
        in_specs=[row, pl.BlockSpec((1, D), lambda j, i: (0, 0)), wspec, wspec],
        out_specs=[h_spec, ospec, ospec, ospec],
        out_shape=[jax.ShapeDtypeStruct((T, D), BF16), oshape, oshape, oshape],
        compiler_params=_params("arbitrary", "arbitrary"),
    )(x, gain, wg_t, wu_t)


def _mix_in(x, gain, w_in_t, tm=512):
    T = x.shape[0]
    tm = min(tm, T)

    def body(x_ref, gain_ref, w_ref, h_ref, p_ref):
        hv = _rms_rows(x_ref, gain_ref)
        h_ref[...] = hv
        p_ref[...] = _dot_nt(hv, w_ref[...])

    row = pl.BlockSpec((tm, D), lambda i: (i, 0))
    return pl.pallas_call(
        body, name="mix_in", grid=(T // tm,),
        in_specs=[row, pl.BlockSpec((1, D), lambda i: (0, 0)), pl.BlockSpec((D_IN, D), lambda i: (0, 0))],
        out_specs=[row, pl.BlockSpec((tm, D_IN), lambda i: (i, 0))],
        out_shape=[jax.ShapeDtypeStruct((T, D), BF16), jax.ShapeDtypeStruct((T, D_IN), F32)],
        compiler_params=_params("parallel"),
    )(x, gain, w_in_t)


def _ffn_mid_bwd(dyh, wout, g, u, tm=512, tn=FF_HALF):
    T = dyh.shape[0]
    tm = min(tm, T)

    def body(dy_ref, w_ref, g_ref, u_ref, dg_ref, du_ref):
        da = _dot_nt(dy_ref[...], w_ref[...])
        g = g_ref[...].astype(F32)
        s = _sigmoid(g)
        gs = g * s
        du_ref[...] = (da * gs).astype(BF16)
        dg_ref[...] = ((da * u_ref[...].astype(F32)) * (s + gs - gs * s)).astype(BF16)

    ospec = pl.BlockSpec((tm, tn), lambda j, i: (i, j))
    oshape = jax.ShapeDtypeStruct((T, D_FFP), BF16)
    return pl.pallas_call(
        body, name="ffn_mid_bwd", grid=(D_FFP // tn, T // tm),
        in_specs=[pl.BlockSpec((tm, D), lambda j, i: (i, 0)), pl.BlockSpec((tn, D), lambda j, i: (j, 0)),
                  ospec, ospec],
        out_specs=[ospec, ospec], out_shape=[oshape, oshape],
        compiler_params=_params("parallel", "parallel"),
    )(dyh, wout, g, u)


def _final(x, tgt, gf, copy_scale, tm=512):
    T = x.shape[0]
    tm = min(tm, T)

    def body(x_ref, t_ref, g_ref, dx_ref, dxb_ref, dg_ref, loss_ref):
        xv = x_ref[...]
        r = lax.rsqrt(jnp.mean(xv * xv, axis=-1, keepdims=True) + EPS)
        xhat = xv * r
        err = xhat * g_ref[...] - t_ref[...]
        dy = err * (1.0 / D)
        dxhat = dy * g_ref[...]
        dx = r * (dxhat - xhat * jnp.mean(dxhat * xhat, axis=-1, keepdims=True))
        dx_ref[...] = dx
        dxb_ref[...] = (copy_scale * dx).astype(BF16)

        @pl.when(pl.program_id(0) == 0)
        def _():
            dg_ref[...] = jnp.zeros_like(dg_ref)
            loss_ref[...] = jnp.zeros_like(loss_ref)

        dg_ref[...] += jnp.sum(dy * xhat, axis=0, keepdims=True)
        loss_ref[...] += (0.5 / D) * jnp.sum(err * err)

    row = pl.BlockSpec((tm, D), lambda i: (i, 0))
    vec = pl.BlockSpec((1, D), lambda i: (0, 0))
    return pl.pallas_call(
        body, name="final_loss", grid=(T // tm,),
        in_specs=[row, row, vec],
        out_specs=[row, row, vec, pl.BlockSpec((1, 128), lambda i: (0, 0))],
        out_shape=[jax.ShapeDtypeStruct((T, D), F32), jax.ShapeDtypeStruct((T, D), BF16),
                   jax.ShapeDtypeStruct((1, D), F32), jax.ShapeDtypeStruct((1, 128), F32)],
        compiler_params=_params("arbitrary"),
    )(x, tgt, gf)


def _mix_block(T, limit):
    return min(limit, T // 2)


def _rows(tb, width):
    return lax.broadcasted_iota(jnp.int32, (tb, width), 0)


def _rglru_gates(xc, wax_ref, bax_ref, lam_ref):
    pre = _dot(xc.astype(BF16), wax_ref[...]) + bax_ref[...]
    r = _sigmoid(pre[:, :D_RNN])
    ig = _sigmoid(pre[:, D_RNN:])
    z = -lam_ref[...]
    sp = jnp.maximum(z, 0.0) + jnp.log(1.0 + jnp.exp(-jnp.abs(z)))
    log_a = (-LRU_C) * r * sp
    a = jnp.exp(log_a)
    mult = jnp.sqrt(-jnp.tanh(log_a) * (1.0 + a * a))
    return r, ig, sp, a, mult


def _conv(xa_ext, cw_ref, cb_ref):
    y = cb_ref[...] + cw_ref[3:4, :] * xa_ext
    for k in range(1, 4):
        y = y + cw_ref[3 - k:4 - k, :] * pltpu.roll(xa_ext, k, 0)
    return y[HALO:]


def _pool_window_lanes():
    lane = lax.broadcasted_iota(jnp.int32, (1, D_POOL), 1)
    return jnp.where(lane < 64, 2, jnp.where(lane < 128, 4, jnp.where(lane < 192, 8, 16)))


def _pool_select(s2, s4, s8, s16):
    lane = lax.broadcasted_iota(jnp.int32, s2.shape, 1)
    return jnp.where(lane < 64, s2, jnp.where(lane < 128, s4, jnp.where(lane < 192, s8, s16)))


def _pool_diff(xp_ext, t0, tb):
    s2 = xp_ext + pltpu.roll(xp_ext, 1, 0)
    s4 = s2 + pltpu.roll(s2, 2, 0)
    s8 = s4 + pltpu.roll(s4, 4, 0)
    s16 = s8 + pltpu.roll(s8, 8, 0)
    sel = _pool_select(s2, s4, s8, s16)[HALO:]
    cnt = jnp.minimum(t0 + _rows(tb, D_POOL) + 1, _pool_window_lanes()).astype(F32)
    return sel / cnt - xp_ext[HALO:], cnt


def _head_masks():
    lane = lax.broadcasted_iota(jnp.int32, (1, D_SGU), 1)
    return [((lane >= 64 * h) & (lane < 64 * (h + 1))).astype(F32) for h in range(4)]


def _sgu_mix(w_ref, vch, masks):
    z = masks[0] * _dot(w_ref[0], vch)
    for h in range(1, 4):
        z = z + masks[h] * _dot(w_ref[h], vch)
    return z


def _mix_fwd(p, prm):
    T = p.shape[0]
    tb = _mix_block(T, 512)
    nb = T // tb

    def body(p_ref, xah_ref, xph_ref, cw_ref, cb_ref, wax_ref, bax_ref, lam_ref, wp_ref, ps_ref, sgn_ref,
             ws_ref, bz_ref, y_ref, hs_ref, carry_ref):
        i = pl.program_id(0)
        keep = (i > 0).astype(F32)

        @pl.when(i == 0)
        def _():
            carry_ref[...] = jnp.zeros_like(carry_ref)

        xa_ext = jnp.concatenate([xah_ref[...] * keep, p_ref[:, 512:1024]], axis=0)
        xc = _conv(xa_ext, cw_ref, cb_ref)
        r, ig, sp, a, mult = _rglru_gates(xc, wax_ref, bax_ref, lam_ref)
        bv = mult * (ig * xc)
        row = _rows(tb, D_RNN)
        s = 1
        while s < tb:
            m = row >= s
            bv = jnp.where(m, a * pltpu.roll(bv, s, 0) + bv, bv)
            a = jnp.where(m, a * pltpu.roll(a, s, 0), a)
            s *= 2
        h = bv + a * carry_ref[0:1, :]
        hs_ref[...] = h
        last = jnp.sum(jnp.where(_rows(8, D_RNN) == 7, hs_ref[tb - 8:tb, :], 0.0), axis=0, keepdims=True)
        carry_ref[...] = jnp.broadcast_to(last, carry_ref.shape)
        y_ref[:, 0:512] = (_gelu(p_ref[:, 0:512]) * h).astype(BF16)

        xp_ext = jnp.concatenate([xph_ref[...] * keep, p_ref[:, 1024:1280]], axis=0)
        d, _ = _pool_diff(xp_ext, i * tb, tb)
        y_ref[:, 512:768] = (_dot(d.astype(BF16), wp_ref[...]) * ps_ref[...]).astype(BF16)

        ug = _gelu(p_ref[:, 1280:1536])
        vg = _gelu(p_ref[:, 1536:1792])
        rv = lax.rsqrt(jnp.mean(vg * vg, axis=-1, keepdims=True) + EPS)
        vn = (vg * rv * sgn_ref[...]).astype(BF16)
        masks = _head_masks()
        for ci in range(tb // CHUNK):
            sl = slice(ci * CHUNK, (ci + 1) * CHUNK)
            z = _sgu_mix(ws_ref, vn[sl], masks) + bz_ref[...]
            y_ref[sl, 768:1024] = (ug[sl] * z).astype(BF16)

    hb = tb // HALO

    def halo(i):
        return jnp.maximum(i * hb - 1, 0)

    def full(shape):
        return pl.BlockSpec(shape, lambda i: (0,) * len(shape))

    return pl.pallas_call(
        body, name="mix_fwd", grid=(nb,),
        in_specs=[pl.BlockSpec((tb, D_IN), lambda i: (i, 0)),
                  pl.BlockSpec((HALO, D_RNN), lambda i: (halo(i), 1)),
                  pl.BlockSpec((HALO, D_POOL), lambda i: (halo(i), 4)),
                  full((4, D_RNN)), full((1, D_RNN)), full((D_RNN, 2 * D_RNN)), full((1, 2 * D_RNN)),
                  full((1, D_RNN)), full((D_POOL, D_POOL)), full((1, D_POOL)), full((1, D_SGU)),
                  full((4, CHUNK, CHUNK)), full((CHUNK, D_SGU))],
        out_specs=[pl.BlockSpec((tb, D), lambda i: (i, 0)), pl.BlockSpec((tb, D_RNN), lambda i: (i, 0))],
        out_shape=[jax.ShapeDtypeStruct((T, D), BF16), jax.ShapeDtypeStruct((T, D_RNN), F32)],
        scratch_shapes=[pltpu.VMEM((8, D_RNN), F32)],
        compiler_params=_params("arbitrary"),
    )(p, p, p, prm["conv_w"], prm["conv_b"], prm["wax"], prm["bax"], prm["lam"], prm["wpool"], prm["pool_scale"],
      prm["sgu_norm"], prm["ws"], prm["bz"])


def _mix_bwd(dy, p, hs, prm):
    T = p.shape[0]
    tb = _mix_block(T, 256)
    nb = T // tb
    hb = tb // HALO

    def body(dy_ref, p_ref, xah_ref, xph_ref, hs_ref, hsh_ref, cw_ref, cb_ref, wax_ref, waxt_ref, bax_ref,
             lam_ref, wp_ref, wpt_ref, ps_ref, sgn_ref, ws_ref, wst_ref, bz_ref,
             dp_ref, dcw_ref, dcb_ref, dwax_ref, dbax_ref, dlam_ref, dwp_ref, dps_ref, dsgn_ref, dws_ref,
             dbz_ref, gcarry_ref, xcfut_ref, mfut_ref):
        i = pl.program_id(0)
        bi = nb - 1 - i
        keep = (bi > 0).astype(F32)

        @pl.when(i == 0)
        def _():
            for ref in (dcw_ref, dcb_ref, dwax_ref, dbax_ref, dlam_ref, dwp_ref, dps_ref, dsgn_ref, dws_ref,
                        dbz_ref, gcarry_ref, xcfut_ref, mfut_ref):
                ref[...] = jnp.zeros_like(ref)

        xa_ext = jnp.concatenate([xah_ref[...] * keep, p_ref[:, 512:1024]], axis=0)
        xc = _conv(xa_ext, cw_ref, cb_ref)
        r, ig, sp, a, mult = _rglru_gates(xc, wax_ref, bax_ref, lam_ref)
        gg, dgg = _gelu_and_grad(p_ref[:, 0:512])
        dya = dy_ref[:, 0:512]
        dp_ref[:, 0:512] = (dya * hs_ref[...] * dgg).astype(BF16)
        row = _rows(tb, D_RNN)
        g = dya * gg + jnp.where(row == tb - 1, gcarry_ref[0:1, :], 0.0)
        al = pltpu.roll(a, tb - 1, 0)
        s = 1
        while s < tb:
            m = row < tb - s
            g = jnp.where(m, al * pltpu.roll(g, tb - s, 0) + g, g)
            al = jnp.where(m, al * pltpu.roll(al, tb - s, 0), al)
            s *= 2
        first = jnp.sum(jnp.where(_rows(8, D_RNN) == 0, (a * g)[0:8], 0.0), axis=0, keepdims=True)
        gcarry_ref[...] = jnp.broadcast_to(first, gcarry_ref.shape)
        hs_ext = jnp.concatenate([hsh_ref[...] * keep, hs_ref[...]], axis=0)
        h_prev = pltpu.roll(hs_ext, 1, 0)[HALO:]
        ix = ig * xc
        dlog_a = g * h_prev * a - (g * ix) * (a * a / mult)
        dlam_ref[...] += jnp.sum(dlog_a * r, axis=0, keepdims=True) * (LRU_C * _sigmoid(-lam_ref[...]))
        dpre_r = dlog_a * ((-LRU_C) * sp) * (r * (1.0 - r))
        dpre_i = (g * mult * xc) * (ig * (1.0 - ig))
        dpre = jnp.concatenate([dpre_r, dpre_i], axis=1)
        dbax_ref[...] += jnp.sum(dpre, axis=0, keepdims=True)
        dpre_b = dpre.astype(BF16)
        dwax_ref[...] += _dot_tn(xc.astype(BF16), dpre_b)
        dxc = g * mult * ig + _dot(dpre_b, waxt_ref[...])
        dcb_ref[...] += jnp.sum(dxc, axis=0, keepdims=True)
        for k in range(4):
            xs = xa_ext[HALO:] if k == 3 else pltpu.roll(xa_ext, 3 - k, 0)[HALO:]
            dcw_ref[k:k + 1, :] += jnp.sum(dxc * xs, axis=0, keepdims=True)
        dxc_ext = jnp.concatenate([dxc, xcfut_ref[...]], axis=0)
        n = tb + HALO
        dxa = cw_ref[3:4, :] * dxc_ext
        for k in range(1, 4):
            dxa = dxa + cw_ref[3 - k:4 - k, :] * pltpu.roll(dxc_ext, n - k, 0)
        dp_ref[:, 512:1024] = dxa[:tb].astype(BF16)
        xcfut_ref[...] = dxc[0:HALO]

        xp_ext = jnp.concatenate([xph_ref[...] * keep, p_ref[:, 1024:1280]], axis=0)
        d, cnt = _pool_diff(xp_ext, bi * tb, tb)
        db = d.astype(BF16)
        dyb = dy_ref[:, 512:768]
        dps_ref[...] += jnp.sum(dyb * _dot(db, wp_ref[...]), axis=0, keepdims=True)
        dq = (dyb * ps_ref[...]).astype(BF16)
        dwp_ref[...] += _dot_tn(db, dq)
        dd = _dot(dq, wpt_ref[...])
        mm = dd / cnt
        m_ext = jnp.concatenate([mm, mfut_ref[...]], axis=0)
        f2 = m_ext + pltpu.roll(m_ext, n - 1, 0)
        f4 = f2 + pltpu.roll(f2, n - 2, 0)
        f8 = f4 + pltpu.roll(f4, n - 4, 0)
        f16 = f8 + pltpu.roll(f8, n - 8, 0)
        dp_ref[:, 1024:1280] = (_pool_select(f2, f4, f8, f16)[:tb] - dd).astype(BF16)
        mfut_ref[...] = mm[0:HALO]

        ug, dug = _gelu_and_grad(p_ref[:, 1280:1536])
        vg, dvg = _gelu_and_grad(p_ref[:, 1536:1792])
        rv = lax.rsqrt(jnp.mean(vg * vg, axis=-1, keepdims=True) + EPS)
        vhat = vg * rv
        vn = (vhat * sgn_ref[...]).astype(BF16)
        dyc = dy_ref[:, 768:1024]
        masks = _head_masks()
        dz = dyc * ug
        dzb = dz.astype(BF16)
        dvn_parts = []
        for ci in range(tb // CHUNK):
            sl = slice(ci * CHUNK, (ci + 1) * CHUNK)
            z = _sgu_mix(ws_ref, vn[sl], masks) + bz_ref[...]
            dp_ref[sl, 1280:1536] = (dyc[sl] * z * dug[sl]).astype(BF16)
            dbz_ref[...] += dz[sl]
            for h in range(4):
                dws_ref[h] += _dot_nt((dz[sl] * masks[h]).astype(BF16), vn[sl])
            dvn_parts.append(_sgu_mix(wst_ref, dzb[sl], masks))
        dvn = jnp.concatenate(dvn_parts, axis=0)
        dsgn_ref[...] += jnp.sum(dvn * vhat, axis=0, keepdims=True)
        dvhat = dvn * sgn_ref[...]
        dvg_in = rv * (dvhat - vhat * jnp.mean(dvhat * vhat, axis=-1, keepdims=True))
        dp_ref[:, 1536:1792] = (dvg_in * dvg).astype(BF16)

        @pl.when(i == nb - 1)
        def _():
            tril = (lax.broadcasted_iota(jnp.int32, (CHUNK, CHUNK), 0)
                    >= lax.broadcasted_iota(jnp.int32, (CHUNK, CHUNK), 1)).astype(F32)
            for h in range(4):
                dws_ref[h] = dws_ref[h] * tril

    def blk(i):
        return nb - 1 - i

    def halo(i):
        return jnp.maximum(blk(i) * hb - 1, 0)

    def full(shape):
        return pl.BlockSpec(shape, lambda i: (0,) * len(shape))

    small_shapes = [(4, D_RNN), (1, D_RNN), (D_RNN, 2 * D_RNN), (1, 2 * D_RNN), (1, D_RNN), (D_POOL, D_POOL),
                    (1, D_POOL), (1, D_SGU), (4, CHUNK, CHUNK), (CHUNK, D_SGU)]
    outs = pl.pallas_call(
        body, name="mix_bwd", grid=(nb,),
        in_specs=[pl.BlockSpec((tb, D), lambda i: (blk(i), 0)),
                  pl.BlockSpec((tb, D_IN), lambda i: (blk(i), 0)),
                  pl.BlockSpec((HALO, D_RNN), lambda i: (halo(i), 1)),
                  pl.BlockSpec((HALO, D_POOL), lambda i: (halo(i), 4)),
                  pl.BlockSpec((tb, D_RNN), lambda i: (blk(i), 0)),
                  pl.BlockSpec((HALO, D_RNN), lambda i: (halo(i), 0)),
                  full((4, D_RNN)), full((1, D_RNN)), full((D_RNN, 2 * D_RNN)), full((2 * D_RNN, D_RNN)),
                  full((1, 2 * D_RNN)), full((1, D_RNN)), full((D_POOL, D_POOL)), full((D_POOL, D_POOL)),
                  full((1, D_POOL)), full((1, D_SGU)), full((4, CHUNK, CHUNK)), full((4, CHUNK, CHUNK)),
                  full((CHUNK, D_SGU))],
        out_specs=[pl.BlockSpec((tb, D_IN), lambda i: (blk(i), 0))] + [full(s) for s in small_shapes],
        out_shape=[jax.ShapeDtypeStruct((T, D_IN), BF16)] + [jax.ShapeDtypeStruct(s, F32) for s in small_shapes],
        scratch_shapes=[pltpu.VMEM((8, D_RNN), F32), pltpu.VMEM((HALO, D_RNN), F32),
                        pltpu.VMEM((HALO, D_POOL), F32)],
        compiler_params=_params("arbitrary"),
    )(dy, p, p, p, hs, hs, prm["conv_w"], prm["conv_b"], prm["wax"], prm["wax_t"], prm["bax"], prm["lam"],
      prm["wpool"], prm["wpool_t"], prm["pool_scale"], prm["sgu_norm"], prm["ws"], prm["ws_t"], prm["bz"])
    names = ("dp", "conv_w", "conv_b", "wax", "bax", "lam", "wpool", "pool_scale", "sgu_norm", "ws", "bz")
    return dict(zip(names, outs))


ANY = pl.BlockSpec(memory_space=pl.ANY)


def _place():
    x, y, c = lax.axis_index("x"), lax.axis_index("y"), lax.axis_index("c")
    return x, y, c


def _all_to_all(xs, name):
    def body(in_ref, out_ref, send_sems, recv_sems, local_sem):
        x, y, c = _place()
        me = 4 * x + 2 * y + c
        mine = pltpu.make_async_copy(in_ref.at[me], out_ref.at[me], local_sem)
        mine.start()
        copies = []
        for rel in range(1, N_DEV):
            tx = 1 - x if rel & 4 else x
            ty = 1 - y if rel & 2 else y
            tc = 1 - c if rel & 1 else c
            cp = pltpu.make_async_remote_copy(
                src_ref=in_ref.at[4 * tx + 2 * ty + tc], dst_ref=out_ref.at[me],
                send_sem=send_sems.at[rel - 1], recv_sem=recv_sems.at[rel - 1],
                device_id=(tx, ty, tc), device_id_type=MESH)
            cp.start()
            copies.append(cp)
        for cp in copies:
            cp.wait()
        mine.wait()

    return pl.pallas_call(
        body, name=name, in_specs=[ANY], out_specs=ANY,
        out_shape=jax.ShapeDtypeStruct(xs.shape, xs.dtype),
        scratch_shapes=[pltpu.SemaphoreType.DMA((N_DEV - 1,)), pltpu.SemaphoreType.DMA((N_DEV - 1,)),
                        pltpu.SemaphoreType.DMA],
    )(xs)


def _all_gather8(xs, name):
    def body(x_ref, out_ref, send_sems, recv_sems, local_sem):
        x, y, c = _place()
        me, sibling = (x, y, c), (x, y, 1 - c)
        chips = [(1 - x, y), (x, 1 - y), (1 - x, 1 - y)]

        def rows(px, py, pc):
            return out_ref.at[4 * px + 2 * py + pc]

        def copy(k, block, to, src=None):
            return pltpu.make_async_remote_copy(
                src_ref=rows(*block) if src is None else src, dst_ref=rows(*block),
                send_sem=send_sems.at[k], recv_sem=recv_sems.at[k], device_id=to, device_id_type=MESH)

        mine = pltpu.make_async_copy(x_ref, rows(*me), local_sem)
        mine.start()
        first = [copy(0, me, sibling, src=x_ref)]
        first += [copy(1 + j, me, (*chip, c), src=x_ref) for j, chip in enumerate(chips)]
        for cp in first:
            cp.start()
        passed = [copy(4 + j, (*chip, c), sibling) for j, chip in enumerate(chips)]
        for j, chip in enumerate(chips):
            copy(1 + j, (*chip, c), me).wait_recv()
            passed[j].start()
        copy(0, sibling, me).wait_recv()
        for j, chip in enumerate(chips):
            copy(4 + j, (*chip, 1 - c), me).wait_recv()
        for cp in first + passed:
            cp.wait_send()
        mine.wait()

    return pl.pallas_call(
        body, name=name, in_specs=[ANY], out_specs=ANY,
        out_shape=jax.ShapeDtypeStruct((N_DEV,) + xs.shape, xs.dtype),
        scratch_shapes=[pltpu.SemaphoreType.DMA((7,)), pltpu.SemaphoreType.DMA((7,)), pltpu.SemaphoreType.DMA],
    )(xs)


HBM = pl.BlockSpec(memory_space=pltpu.HBM)
SEM = pl.BlockSpec(memory_space=pltpu.SEMAPHORE)
EFFECT = pltpu.SideEffectType.DATAFLOW_SIDE_EFFECTING


def _in_hbm(a):
    return pltpu.with_memory_space_constraint(a, pltpu.HBM)


def _local_copy(src, dst, stage, sem):
    load = pltpu.make_async_copy(src, stage, sem)
    load.start()
    load.wait()
    store = pltpu.make_async_copy(stage, dst, sem)
    store.start()
    store.wait()


def _unique(windows):
    arrays = []
    for per_chip in windows:
        for arr, _ in per_chip:
            if not any(arr is a for a in arrays):
                arrays.append(arr)
    return arrays


def _exchange_start(layer, windows, lands, name):
    arrays = _unique(windows)
    na, nt = len(arrays), len(windows)

    def body(*refs):
        in_refs, land_refs = refs[:na], refs[na:na + nt]
        send_sems, recv_sems = refs[na + nt], refs[na + nt + 1]
        token = refs[-1]
        x, y, c = _place()
        me = 4 * x + 2 * y + c
        for t in range(nt):
            for j in range(N_CHIP):
                arr, window = windows[t][j]
                src = window(in_refs[next(i for i, a in enumerate(arrays) if a is arr)])

                @pl.when(me != 2 * j + layer)
                def _():
                    pltpu.make_async_remote_copy(
                        src_ref=src, dst_ref=land_refs[t].at[me], send_sem=send_sems.at[N_CHIP * t + j],
                        recv_sem=recv_sems.at[N_DEV * t + me], device_id=(j // 2, j % 2, layer),
                        device_id_type=MESH).start()
        token[...] = jnp.zeros_like(token)

    outs = pl.pallas_call(
        body, name=name,
        out_shape=(pltpu.SemaphoreType.DMA((N_CHIP * nt,)), pltpu.SemaphoreType.DMA((N_DEV * nt,)),
                   *[pltpu.HBM(a.shape, a.dtype) for a in lands], jax.ShapeDtypeStruct((8, 128), F32)),
        in_specs=[HBM] * (na + nt),
        out_specs=(SEM, SEM, *[HBM] * nt, pl.BlockSpec(memory_space=pltpu.VMEM)),
        input_output_aliases={na + t: 2 + t for t in range(nt)},
        compiler_params=pltpu.CompilerParams(has_side_effects=EFFECT, vmem_limit_bytes=VMEM_LIMIT),
    )(*[_in_hbm(a) for a in arrays], *[_in_hbm(a) for a in lands])
    return outs[0], outs[1], list(outs[2:2 + nt]), outs[-1]


def _exchange_wait(layer, windows, lands, send_sems, recv_sems, after, name):
    arrays = _unique(windows)
    na, nt = len(arrays), len(windows)

    def body(*refs):
        in_refs, land_refs = refs[:na], refs[na:na + nt]
        send_sems, recv_sems = refs[na + nt], refs[na + nt + 1]
        stages, local_sem = refs[-1 - nt:-1], refs[-1]
        x, y, c = _place()
        me = 4 * x + 2 * y + c

        def source(t, j):
            arr, window = windows[t][j]
            return window(in_refs[next(i for i, a in enumerate(arrays) if a is arr)])

        @pl.when(c == layer)
        def _():
            for t in range(nt):
                for j in range(N_CHIP):
                    @pl.when(me == 2 * j + layer)
                    def _():
                        _local_copy(source(t, j), land_refs[t].at[me], stages[t], local_sem)

        for t in range(nt):
            for j in range(N_CHIP):
                @pl.when(me != 2 * j + layer)
                def _():
                    pltpu.make_async_remote_copy(
                        src_ref=source(t, j), dst_ref=land_refs[t].at[me], send_sem=send_sems.at[N_CHIP * t + j],
                        recv_sem=recv_sems.at[N_DEV * t + me], device_id=(j // 2, j % 2, layer),
                        device_id_type=MESH).wait_send()

        @pl.when(c == layer)
        def _():
            for t in range(nt):
                for s in range(N_DEV):
                    @pl.when(me != s)
                    def _():
                        slot = land_refs[t].at[s]
                        pltpu.make_async_remote_copy(
                            src_ref=slot, dst_ref=slot, send_sem=send_sems.at[N_CHIP * t],
                            recv_sem=recv_sems.at[N_DEV * t + s], device_id=(x, y, c),
                            device_id_type=MESH).wait_recv()

    outs = pl.pallas_call(
        body, name=name,
        out_shape=tuple(pltpu.HBM(a.shape, a.dtype) for a in lands),
        in_specs=[HBM] * (na + nt) + [SEM, SEM, ANY],
        out_specs=tuple([HBM] * nt),
        input_output_aliases={na + t: t for t in range(nt)},
        scratch_shapes=[pltpu.VMEM(a.shape[1:], a.dtype) for a in lands] + [pltpu.SemaphoreType.DMA],
        compiler_params=pltpu.CompilerParams(has_side_effects=EFFECT, vmem_limit_bytes=VMEM_LIMIT),
    )(*[_in_hbm(a) for a in arrays], *lands, send_sems, recv_sems, after)
    return list(outs)


def _other_chips(x, y):
    return [(1 - x, y), (x, 1 - y), (1 - x, 1 - y)]


def _split_rows(rows):
    if rows < 32:
        return [(0, rows), (rows, 0)]
    cut = -(-(rows // 2) // 16) * 16
    return [(0, cut), (cut, rows - cut)]


def _gather_copies(in_refs, land_refs, send_sems, recv_sems, x, y, sender_core, both):
    copies = []
    for t, src in enumerate(in_refs):
        r0, n = _split_rows(src.shape[0])[sender_core]
        if n == 0:
            continue
        for rel, (tx, ty) in enumerate(_other_chips(x, y)):
            for tc in (0, 1) if both else (sender_core,):
                copies.append(dict(
                    src_ref=src.at[pl.ds(r0, n)], dst_ref=land_refs[t].at[2 * x + y].at[pl.ds(r0, n)],
                    send_sem=send_sems.at[6 * t + 2 * rel + tc],
                    recv_sem=recv_sems.at[2 * (3 * t + rel) + sender_core],
                    device_id=(tx, ty, tc), device_id_type=MESH))
    return copies


def _gather_start(shards, lands, name, both):
    nt = len(shards)

    def body(*refs):
        in_refs, land_refs = refs[:nt], refs[nt:2 * nt]
        send_sems, recv_sems = refs[2 * nt], refs[2 * nt + 1]
        token = refs[-1]
        x, y, c = _place()
        for core in range(2):
            @pl.when(c == core)
            def _():
                for cp_args in _gather_copies(in_refs, land_refs, send_sems, recv_sems, x, y, core, both):
                    pltpu.make_async_remote_copy(**cp_args).start()
        token[...] = jnp.zeros_like(token)

    outs = pl.pallas_call(
        body, name=name,
        out_shape=(pltpu.SemaphoreType.DMA((6 * nt,)), pltpu.SemaphoreType.DMA((6 * nt,)),
                   *[pltpu.HBM(a.shape, a.dtype) for a in lands], jax.ShapeDtypeStruct((8, 128), F32)),
        in_specs=[HBM] * (2 * nt),
        out_specs=(SEM, SEM, *[HBM] * nt, pl.BlockSpec(memory_space=pltpu.VMEM)),
        input_output_aliases={nt + t: 2 + t for t in range(nt)},
        compiler_params=pltpu.CompilerParams(has_side_effects=EFFECT, vmem_limit_bytes=VMEM_LIMIT),
    )(*[_in_hbm(a) for a in shards], *[_in_hbm(a) for a in lands])
    return outs[0], outs[1], list(outs[2:2 + nt]), outs[-1]


def _gather_wait(shards, lands, send_sems, recv_sems, after, name, both):
    nt = len(shards)

    def body(*refs):
        in_refs, land_refs = refs[:nt], refs[nt:2 * nt]
        send_sems, recv_sems = refs[2 * nt], refs[2 * nt + 1]
        stages, local_sem = refs[-1 - nt:-1], refs[-1]
        x, y, c = _place()

        for t in range(nt):
            _local_copy(in_refs[t], land_refs[t].at[2 * x + y], stages[t], local_sem)

        for core in range(2):
            @pl.when(c == core)
            def _():
                for cp_args in _gather_copies(in_refs, land_refs, send_sems, recv_sems, x, y, core, both):
                    pltpu.make_async_remote_copy(**cp_args).wait_send()

        def wait_parts_from(core):
            for t in range(nt):
                r0, n = _split_rows(in_refs[t].shape[0])[core]
                for rel, (tx, ty) in enumerate(_other_chips(x, y)):
                    if n > 0:
                        part = land_refs[t].at[2 * tx + ty].at[pl.ds(r0, n)]
                        pltpu.make_async_remote_copy(
                            src_ref=part, dst_ref=part, send_sem=send_sems.at[6 * t],
                            recv_sem=recv_sems.at[2 * (3 * t + rel) + core], device_id=(x, y, c),
                            device_id_type=MESH).wait_recv()

        for core in range(2):
            if both:
                wait_parts_from(core)
            else:
                pl.when(c == core)(functools.partial(wait_parts_from, core))

    outs = pl.pallas_call(
        body, name=name,
        out_shape=tuple(pltpu.HBM(a.shape, a.dtype) for a in lands),
        in_specs=[HBM] * (2 * nt) + [SEM, SEM, ANY],
        out_specs=tuple([HBM] * nt),
        input_output_aliases={nt + t: t for t in range(nt)},
        scratch_shapes=[pltpu.VMEM(a.shape, a.dtype) for a in shards] + [pltpu.SemaphoreType.DMA],
        compiler_params=pltpu.CompilerParams(has_side_effects=EFFECT, vmem_limit_bytes=VMEM_LIMIT),
    )(*[_in_hbm(a) for a in shards], *lands, send_sems, recv_sems, after)
    return list(outs)


def _pair_share(lands, name):
    nt = len(lands)
    splits = [_split_rows(a.shape[1]) for a in lands]

    def body(*refs):
        land_refs = refs[:nt]
        stages = refs[2 * nt:3 * nt]
        load_sems, send_sems, recv_sems = refs[3 * nt:]
        x, y, c = _place()

        def parts(core):
            out = []
            for t in range(nt):
                r0, n = splits[t][core]
                for rel, (tx, ty) in enumerate(_other_chips(x, y)):
                    if n > 0:
                        out.append((3 * t + rel, stages[t].at[rel, pl.ds(0, n)],
                                    land_refs[t].at[2 * tx + ty].at[pl.ds(r0, n)]))
            return out

        def send(core):
            loads = [pltpu.make_async_copy(part, stage, load_sems.at[k]) for k, stage, part in parts(core)]
            for cp in loads:
                cp.start()
            pushes = []
            for cp, (k, stage, part) in zip(loads, parts(core)):
                cp.wait()
                push = pltpu.make_async_remote_copy(src_ref=stage, dst_ref=part, send_sem=send_sems.at[k],
                                                    recv_sem=recv_sems.at[k], device_id=(x, y, 1 - c),
                                                    device_id_type=MESH)
                push.start()
                pushes.append(push)
            for push in pushes:
                push.wait_send()
            for k, _, part in parts(1 - core):
                pltpu.make_async_remote_copy(src_ref=part, dst_ref=part, send_sem=send_sems.at[k],
                                             recv_sem=recv_sems.at[k], device_id=(x, y, 1 - c),
                                             device_id_type=MESH).wait_recv()

        for core in range(2):
            pl.when(c == core)(functools.partial(send, core))

    outs = pl.pallas_call(
        body, name=name, in_specs=[ANY] * nt, out_specs=[ANY] * nt,
        out_shape=[jax.ShapeDtypeStruct(a.shape, a.dtype) for a in lands],
        input_output_aliases={t: t for t in range(nt)},
        scratch_shapes=[pltpu.VMEM((3, max(n for _, n in sp), a.shape[2]), a.dtype) for a, sp in zip(lands, splits)]
        + [pltpu.SemaphoreType.DMA((3 * nt,))] * 3,
        compiler_params=pltpu.CompilerParams(vmem_limit_bytes=VMEM_LIMIT),
    )(*lands)
    return list(outs)


def _tie(a, token):
    def body(a_ref, token_ref, o_ref):
        pass

    return pl.pallas_call(
        body, name="tie", in_specs=[ANY, ANY], out_specs=ANY,
        out_shape=jax.ShapeDtypeStruct(a.shape, a.dtype), input_output_aliases={0: 0},
    )(a, token)


def _sum_share(xs, name):
    _, r, cols = xs.shape
    tr = max(t for t in range(16, min(r, 704) + 1, 16) if r % t == 0)
    nblk = r // tr

    def body(x_ref, out_ref, acc_ref, send_sems, local_sems, recv_sem):
        i = pl.program_id(0)
        slot = i % 2
        x, y, c = _place()

        def copies(s, blk):
            dst = out_ref.at[c, pl.ds(blk * tr, tr), :]
            loc = pltpu.make_async_copy(acc_ref.at[s], dst, local_sems.at[s])
            rem = pltpu.make_async_remote_copy(src_ref=acc_ref.at[s], dst_ref=dst, send_sem=send_sems.at[s],
                                               recv_sem=recv_sem, device_id=(x, y, 1 - c), device_id_type=MESH)
            return loc, rem

        @pl.when(i >= 2)
        def _():
            loc, rem = copies(slot, i - 2)
            loc.wait()
            rem.wait_send()

        acc = x_ref[0].astype(F32)
        for k in range(1, N_DEV):
            acc = acc + x_ref[k].astype(F32)
        acc_ref[slot] = acc
        loc, rem = copies(slot, i)
        loc.start()
        rem.start()

        @pl.when(i == nblk - 1)
        def _():
            for back in range(min(2, nblk)):
                blk = nblk - 1 - back
                loc, rem = copies(blk % 2, blk)
                loc.wait()
                rem.wait_send()
            theirs = out_ref.at[1 - c]
            pltpu.make_async_remote_copy(src_ref=theirs, dst_ref=theirs, send_sem=send_sems.at[0],
                                         recv_sem=recv_sem, device_id=(x, y, 1 - c),
                                         device_id_type=MESH).wait_recv()

    return pl.pallas_call(
        body, name=name, grid=(nblk,),
        in_specs=[pl.BlockSpec((N_DEV, tr, cols), lambda i: (0, i, 0))],
        out_specs=ANY,
        out_shape=jax.ShapeDtypeStruct((2, r, cols), F32),
        scratch_shapes=[pltpu.VMEM((2, tr, cols), F32), pltpu.SemaphoreType.DMA((2,)),
                        pltpu.SemaphoreType.DMA((2,)), pltpu.SemaphoreType.DMA],
        compiler_params=_params("arbitrary"),
    )(xs)


def _sum8(xs, name):
    _, r, cols = xs.shape
    tr = 8
    for cand in (256, 128, 64, 32, 16):
        if r % cand == 0:
            tr = cand
            break

    def body(x_ref, o_ref):
        acc = x_ref[0].astype(F32)
        for k in range(1, N_DEV):
            acc = acc + x_ref[k].astype(F32)
        o_ref[...] = acc

    return pl.pallas_call(
        body, name=name, grid=(r // tr,),
        in_specs=[pl.BlockSpec((N_DEV, tr, cols), lambda i: (0, i, 0))],
        out_specs=pl.BlockSpec((tr, cols), lambda i: (i, 0)),
        out_shape=jax.ShapeDtypeStruct((r, cols), F32),
        compiler_params=_params("parallel"),
    )(xs)


def _adamw(w, g, m, v, name):
    R, C = w.shape
    tr = max([t for t in range(8, min(R, 512) + 1, 8) if R % t == 0] or [R])
    c1 = 1.0 / (1.0 - ADAM_B1 ** ADAM_STEP)
    c2 = 1.0 / (1.0 - ADAM_B2 ** ADAM_STEP)

    def body(w_ref, g_ref, m_ref, v_ref, d_ref, nm_ref, nv_ref):
        gv = g_ref[...]
        nm = ADAM_B1 * m_ref[...] + (1.0 - ADAM_B1) * gv
        nv = ADAM_B2 * v_ref[...] + (1.0 - ADAM_B2) * (gv * gv)
        d_ref[...] = (-ADAM_LR) * ((nm * c1) / (jnp.sqrt(nv * c2) + ADAM_EPS) + ADAM_WD * w_ref[...])
        nm_ref[...] = nm
        nv_ref[...] = nv

    spec = pl.BlockSpec((tr, C), lambda i: (i, 0))
    shape = jax.ShapeDtypeStruct((R, C), F32)
    return pl.pallas_call(
        body, name=name, grid=(R // tr,), in_specs=[spec] * 4, out_specs=[spec] * 3, out_shape=[shape] * 3,
        compiler_params=_params("parallel"),
    )(w, g, m, v)


def _flat_rows(parts, rows):
    flat = jnp.concatenate([q.reshape(-1) for q in parts])
    flat = jnp.pad(flat, (0, rows * LANES - flat.shape[0]))
    return flat.reshape(rows, LANES)


def _round_up(n, m):
    return (n + m - 1) // m * m


def _block_diag(w):
    H, n, _ = w.shape
    eye = jnp.eye(H, dtype=w.dtype)
    return (eye[:, None, :, None] * w[:, :, None, :]).reshape(H * n, H * n)


def _diag_blocks(w, H, n):
    w4 = w.reshape(H, n, H, n)
    return jnp.stack([w4[h, :, h, :] for h in range(H)])


W_IN_T = ("ffn1_w_in", "w_in", "ffn2_w_in")


def _ffn_in_weights(g_in):
    zeros = jnp.zeros((FF_HALF - FF_SHARD, D), g_in.dtype)
    wg_t = jnp.concatenate([g_in[0], zeros, g_in[1], zeros], axis=0)
    wu_t = jnp.concatenate([g_in[2], zeros, g_in[3], zeros], axis=0)
    return wg_t, wu_t


def _ffn_out_weights(g_out):
    zeros = jnp.zeros((FF_HALF - FF_SHARD, D), g_out.dtype)
    return jnp.concatenate([g_out[0], g_out[1], zeros, g_out[2], g_out[3], zeros], axis=0)


LAND_SHAPES = {"ffn1_w_in": (FF_SHARD, D), "ffn1_w_out": (FF_ROWS, D), "w_in": (D_IN // N_CHIP, D),
               "w_out": (D // N_CHIP, D), "ffn2_w_in": (FF_SHARD, D), "ffn2_w_out": (FF_ROWS, D)}


def _rows_window(arr, start, size):
    return arr, lambda r: r.at[pl.ds(start, size), :]


def _w_in_grad_windows(dwg_t, dwu_t):
    return [_rows_window(dwg_t if j < 2 else dwu_t, (j % 2) * FF_HALF, FF_SHARD) for j in range(N_CHIP)]


def _w_out_grad_windows(dwout):
    return [_rows_window(dwout, (j // 2) * FF_HALF + (j % 2) * FF_ROWS, FF_ROWS) for j in range(N_CHIP)]


def _mix_grad_windows(dwin_t, dwo):
    win = [_rows_window(dwin_t, j * (D_IN // N_CHIP), D_IN // N_CHIP) for j in range(N_CHIP)]
    wo = [_rows_window(dwo, j * (D // N_CHIP), D // N_CHIP) for j in range(N_CHIP)]
    return win, wo


def kernel(x, ffn1_norm, ffn1_w_in, ffn1_w_out, mix_norm, w_in, conv_w, conv_b, rg_w_a, rg_b_a, rg_w_x, rg_b_x, lru_lambda, pool_w, pool_scale, sgu_norm, sgu_w, sgu_b, w_out, ffn2_norm, ffn2_w_in, ffn2_w_out, final_norm, loss_target, m_ffn1_norm, m_ffn1_w_in, m_ffn1_w_out, m_mix_norm, m_w_in, m_conv_w, m_conv_b, m_rg_w_a, m_rg_b_a, m_rg_w_x, m_rg_b_x, m_lru_lambda, m_pool_w, m_pool_scale, m_sgu_norm, m_sgu_w, m_sgu_b, m_w_out, m_ffn2_norm, m_ffn2_w_in, m_ffn2_w_out, m_final_norm, v_ffn1_norm, v_ffn1_w_in, v_ffn1_w_out, v_mix_norm, v_w_in, v_conv_w, v_conv_b, v_rg_w_a, v_rg_b_a, v_rg_w_x, v_rg_b_x, v_lru_lambda, v_pool_w, v_pool_scale, v_sgu_norm, v_sgu_w, v_sgu_b, v_w_out, v_ffn2_norm, v_ffn2_w_in, v_ffn2_w_out, v_final_norm):
    args = locals()
    W = {n: args[n] for n in WEIGHTS}
    M = {n: args["m_" + n] for n in WEIGHTS}
    V = {n: args["v_" + n] for n in WEIGHTS}
    depth = ffn1_norm.shape[0]
    T = x.shape[1]
    xi, yi, ci = _place()
    chip = 2 * xi + yi

    assert depth == 2, "core c of a chip reduces layer c"
    groups = [(l, names) for l in range(depth)
              for names in (["ffn1_w_in"], ["ffn1_w_out", "w_in", "w_out"], ["ffn2_w_in", "ffn2_w_out"])]
    def stored(a, n):
        return jnp.swapaxes(a, 1, 2) if n in W_IN_T else a

    wb = {n: stored(W[n], n).astype(BF16) for n in BIG}
    groups[1][1].append("conv_w")
    conv_shard = conv_w.reshape(-1, conv_w.shape[-1])
    flights = {}

    def weights_start(k, dep=None):
        l, names = groups[k]
        shards = [conv_shard if n == "conv_w" else wb[n][l] for n in names]
        if dep is not None:
            shards[0] = _tie(shards[0], dep)
        lands = [lax.empty((N_CHIP,) + s.shape, s.dtype) for s in shards]
        send, recv, lands, token = _gather_start(shards, lands, "weights_start_%d" % k, k >= 2)
        flights[k] = (shards, lands, send, recv)
        return token

    def weights_wait(k, after):
        l, names = groups[k]
        shards, lands, send, recv = flights[k]
        got = _gather_wait(shards, lands, send, recv, after, "weights_wait_%d" % k, k >= 2)
        token = weights_start(k + 1, got[0]) if k + 1 < len(groups) else None
        if k < 2:
            got = _pair_share(got, "weights_share_%d" % k)
        return dict(zip(names, got)), token

    def after_start(a, token):
        return a if token is None else _tie(a, token)

    first_tokens = [weights_start(0)]

    layers = []
    for l in range(depth):
        L = {f: dict(norm=W[f + "_norm"][l][None]) for f in ("ffn1", "ffn2")}
        ws = jnp.where(jnp.tril(jnp.ones((CHUNK, CHUNK), bool))[None], sgu_w[l], 0.0)
        wax = jnp.concatenate([_block_diag(rg_w_a[l]), _block_diag(rg_w_x[l])], axis=1)
        wpool = _block_diag(pool_w[l])
        L["mix"] = dict(
            conv_b=conv_b[l][None], wax=wax.astype(BF16), wax_t=wax.T.astype(BF16),
            bax=jnp.concatenate([rg_b_a[l].reshape(-1), rg_b_x[l].reshape(-1)])[None], lam=lru_lambda[l][None],
            wpool=wpool.astype(BF16), wpool_t=wpool.T.astype(BF16), pool_scale=pool_scale[l][None],
            sgu_norm=sgu_norm[l][None], ws=ws.astype(BF16), ws_t=jnp.swapaxes(ws, 1, 2).astype(BF16),
            bz=jnp.repeat(sgu_b[l].T, 64, axis=1))
        L["mix_norm"] = mix_norm[l][None]
        layers.append(L)
    for token in first_tokens:
        layers[0]["ffn1"]["norm"] = _tie(layers[0]["ffn1"]["norm"], token)

    xs = x[0]
    saved = []
    for l, L in enumerate(layers):
        F1, F2 = L["ffn1"], L["ffn2"]
        got, token = weights_wait(3 * l, xs)
        F1["wg"], F1["wu"] = _ffn_in_weights(got["ffn1_w_in"])
        F1["norm"] = after_start(F1["norm"], token)
        h, g, u, a = _ffn_in(xs, F1["norm"], F1["wg"], F1["wu"])
        got, token = weights_wait(3 * l + 1, a)
        F1["wout"] = _ffn_out_weights(got["ffn1_w_out"])
        L["w_in"] = got["w_in"].reshape(D_IN, D)
        L["w_out"] = got["w_out"].reshape(D, D)
        if "conv_w" in got:
            conv_full = jnp.concatenate([got["conv_w"][j] for j in range(N_CHIP)], axis=1)
            for ll in range(depth):
                layers[ll]["mix"]["conv_w"] = conv_full.reshape(depth, 4, D_RNN)[ll]
        a = after_start(a, token)
        x1 = _mm_res(a, F1["wout"], xs, 0.5, "ffn_out", tm=512, tn=D)
        s1 = (xs, h, g, u, a)
        hm, p = _mix_in(x1, L["mix_norm"], L["w_in"])
        ycat, hs = _mix_fwd(p, L["mix"])
        x2 = _mm_res(ycat, L["w_out"], x1, 1.0, "mix_out", tm=512, tn=D)
        got, token = weights_wait(3 * l + 2, x2)
        F2["wg"], F2["wu"] = _ffn_in_weights(got["ffn2_w_in"])
        F2["wout"] = _ffn_out_weights(got["ffn2_w_out"])
        F2["norm"] = after_start(F2["norm"], token)
        h2, g2, u2, a2 = _ffn_in(x2, F2["norm"], F2["wg"], F2["wu"])
        x3 = _mm_res(a2, F2["wout"], x2, 0.5, "ffn_out", tm=512, tn=D)
        saved.append((s1, (x1, hm, p, ycat, hs), (x2, h2, g2, u2, a2)))
        xs = x3

    dx, dxb, d_final, loss_part = _final(xs, loss_target[0], final_norm[None], 0.5)

    G = {n: [None] * depth for n in SMALL if n != "final_norm"}
    lands = {n: lax.empty((N_DEV,) + LAND_SHAPES[n], BF16) for n in BIG}
    in_flight = []

    def send_grads(l, windows, tag):
        names = list(windows)
        send, recv, thru, token = _exchange_start(l, [windows[n] for n in names], [lands[n] for n in names],
                                                  "grads_start_" + tag)
        lands.update(zip(names, thru))
        in_flight.append((l, names, [windows[n] for n in names], send, recv, "grads_wait_" + tag))
        return token

    def ffn_bwd(dx, dxb, F, s, f, l, pending, send_now):
        xin, h, g, u, a = s
        dwout = _mm_tn(a, dxb, 1.0, "ffn_dwout")
        if send_now:
            token = send_grads(l, {f + "_w_out": _w_out_grad_windows(dwout)}, "l%d_%s_out" % (l, f))
            dxb = _tie(dxb, token)
        else:
            pending[f + "_w_out"] = _w_out_grad_windows(dwout)
        dg, du = _ffn_mid_bwd(dxb, F["wout"], g, u)
        dwg = _mm_tn(dg, h, 1.0, "ffn_dwg")
        dwu = _mm_tn(du, h, 1.0, "ffn_dwu")
        pending[f + "_w_in"] = _w_in_grad_windows(dwg, dwu)
        if send_now:
            dg = _tie(dg, send_grads(l, pending, "l%d_%s" % (l, f)))
        dx, dxb, dn = _dh_rms_bwd([(dg, F["wg"]), (du, F["wu"])], xin, F["norm"], dx,
                                  1.0 if f == "ffn2" else 0.5, "ffn_dh")
        G[f + "_norm"][l] = dn[0]
        return dx, dxb

    for l in reversed(range(depth)):
        L = layers[l]
        s1, (x1, hm, p, ycat, hs), s2 = saved[l]
        pending = {}
        dx, dxb = ffn_bwd(dx, dxb, L["ffn2"], s2, "ffn2", l, pending, l == 0)
        if l == 0:
            pending = {}
        dycat = _mm_nt(dxb, L["w_out"], "mix_dy", tm=512, tn=D)
        dwo = _mm_tn(ycat, dxb, 1.0, "mix_dwout")
        mg = _mix_bwd(dycat, p, hs, L["mix"])
        dwin = _mm_tn(mg["dp"], hm, 1.0, "mix_dwin")
        pending["w_in"], pending["w_out"] = _mix_grad_windows(dwin, dwo)
        if l == 0:
            dp = _tie(mg["dp"], send_grads(l, pending, "l0_mix"))
            pending = {}
        else:
            dp = mg["dp"]
        dx, dxb, dn = _dh_rms_bwd([(dp, L["w_in"])], x1, L["mix_norm"], dx, 0.5, "mix_dh")
        G["mix_norm"][l] = dn[0]
        G["conv_w"][l], G["conv_b"][l] = mg["conv_w"], mg["conv_b"][0]
        G["rg_w_a"][l] = _diag_blocks(mg["wax"][:, :D_RNN], 8, 64)
        G["rg_w_x"][l] = _diag_blocks(mg["wax"][:, D_RNN:], 8, 64)
        G["rg_b_a"][l] = mg["bax"][0, :D_RNN].reshape(8, 64)
        G["rg_b_x"][l] = mg["bax"][0, D_RNN:].reshape(8, 64)
        G["lru_lambda"][l] = mg["lam"][0]
        G["pool_w"][l] = _diag_blocks(mg["wpool"], 4, 64)
        G["pool_scale"][l], G["sgu_norm"][l] = mg["pool_scale"][0], mg["sgu_norm"][0]
        G["sgu_w"][l] = mg["ws"]
        G["sgu_b"][l] = mg["bz"].reshape(CHUNK, 4, 64).sum(-1).T
        dx, dxb = ffn_bwd(dx, dxb, L["ffn1"], s1, "ffn1", l, pending, l == 0)
        if l > 0:
            dxb = _tie(dxb, send_grads(l, pending, "l%d" % l))
    grad_x = dx[None]
    G = {n: jnp.stack(v) for n, v in G.items()}
    G["final_norm"] = d_final[0]

    for l, names, windows, send, recv, tag in in_flight:
        lands.update(zip(names, _exchange_wait(l, windows, [lands[n] for n in names], send, recv, dx, tag)))
    both = [_sum_share(lands[n], "sum_share_" + n) for n in BIG]
    grads = dict(zip(BIG, both))

    small_sizes = [int(np.prod(G[n].shape)) for n in SMALL]
    srows = _round_up(sum(small_sizes) + 1, N_DEV * 8 * LANES) // (N_DEV * LANES)
    sflat = _flat_rows([G[n] for n in SMALL] + [loss_part[0, :1]], N_DEV * srows)
    sgot = _all_to_all(sflat.reshape(N_DEV, srows, LANES), "exchange_small_grads")
    sall = _all_gather8(_sum8(sgot, "sum_small_grads"), "share_small_grads").reshape(-1)
    off = 0
    for n, size in zip(SMALL, small_sizes):
        grads[n] = sall[off:off + size].reshape(G[n].shape)
        off += size
    loss = sall[off]
    grads["conv_w"] = lax.dynamic_slice_in_dim(grads["conv_w"], chip * conv_w.shape[2], conv_w.shape[2], axis=2)

    delta, new_m, new_v = {}, {}, {}
    for n in BIG:
        shp = grads[n].shape
        two_d = (shp[0] * shp[1], shp[2])
        outs = _adamw(stored(W[n], n).reshape(two_d), grads[n].reshape(two_d), stored(M[n], n).reshape(two_d),
                      stored(V[n], n).reshape(two_d), "adamw_" + n)
        delta[n], new_m[n], new_v[n] = (stored(o.reshape(shp), n) for o in outs)
        grads[n] = stored(grads[n], n)
    arows = _round_up(sum(int(np.prod(W[n].shape)) for n in SMALL), 8 * LANES) // LANES
    outs = _adamw(*(_flat_rows([src[n] for n in SMALL], arows) for src in (W, grads, M, V)), "adamw_small")
    outs = [o.reshape(-1) for o in outs]
    off = 0
    for n in SMALL:
        size = int(np.prod(W[n].shape))
        delta[n], new_m[n], new_v[n] = (o[off:off + size].reshape(W[n].shape) for o in outs)
        off += size

    return (loss, grad_x, *[grads[n] for n in WEIGHTS], *[delta[n] for n in WEIGHTS],
            *[new_m[n] for n in WEIGHTS], *[new_v[n] for n in WEIGHTS])
```

```python
import functools
import math

import jax
import jax.numpy as jnp
import numpy as np
from jax import lax
from jax.experimental import pallas as pl
from jax.experimental.pallas import tpu as pltpu

F32 = jnp.float32
BF16 = jnp.bfloat16
MESH = pl.DeviceIdType.MESH

D = 1024
D_RNN = 512
D_POOL = 256
D_SGU = 256
D_IN = 1792
D_FF = 2752
D_FFP = 2816
N_CHIP = 4
FF_SHARD = D_FF // 2
FF_HALF = D_FFP // 2
FF_ROWS = D_FF // N_CHIP
CHUNK = 128
HALO = 16
EPS = 1e-6
LRU_C = 8.0
N_DEV = 8
LANES = 1024
VMEM_LIMIT = 56 * 1024 * 1024

ADAM_LR, ADAM_B1, ADAM_B2, ADAM_EPS, ADAM_WD, ADAM_STEP = 0.001, 0.9, 0.999, 1e-08, 0.01, 10

BIG = ("ffn1_w_in", "ffn1_w_out", "w_in", "w_out", "ffn2_w_in", "ffn2_w_out")
SMALL = ("ffn1_norm", "mix_norm", "conv_w", "conv_b", "rg_w_a", "rg_b_a", "rg_w_x", "rg_b_x", "lru_lambda",
         "pool_w", "pool_scale", "sgu_norm", "sgu_w", "sgu_b", "ffn2_norm", "final_norm")
WEIGHTS = ("ffn1_norm", "ffn1_w_in", "ffn1_w_out", "mix_norm", "w_in", "conv_w", "conv_b", "rg_w_a", "rg_b_a",
           "rg_w_x", "rg_b_x", "lru_lambda", "pool_w", "pool_scale", "sgu_norm", "sgu_w", "sgu_b", "w_out",
           "ffn2_norm", "ffn2_w_in", "ffn2_w_out", "final_norm")


def _params(*sem):
    return pltpu.CompilerParams(dimension_semantics=sem, vmem_limit_bytes=VMEM_LIMIT)


def _gelu(x):
    c = math.sqrt(2.0 / math.pi)
    t = jnp.tanh(c * (x + 0.044715 * (x * x * x)))
    return 0.5 * x * (1.0 + t)


def _gelu_and_grad(x):
    c = math.sqrt(2.0 / math.pi)
    x2 = x * x
    t = jnp.tanh(c * (x + 0.044715 * (x2 * x)))
    g = 0.5 * x * (1.0 + t)
    dg = 0.5 * (1.0 + t) + 0.5 * x * (1.0 - t * t) * (c * (1.0 + 3.0 * 0.044715 * x2))
    return g, dg


def _sigmoid(x):
    return 0.5 * jnp.tanh(0.5 * x) + 0.5


def _dot(a, b):
    return jnp.dot(a, b, preferred_element_type=F32)


def _dot_tn(a, b):
    return lax.dot_general(a, b, (((0,), (0,)), ((), ())), preferred_element_type=F32)


def _dot_nt(a, b):
    return lax.dot_general(a, b, (((1,), (1,)), ((), ())), preferred_element_type=F32)


def _tile(n, limit):
    best = 128
    for t in range(128, min(n, limit) + 1, 128):
        if n % t == 0:
            best = t
    assert n % best == 0, (n, limit)
    return best


def _mm_res(a, b, res, scale, name, tm=1024, tn=512):
    M, K = a.shape
    N = b.shape[1]
    tm, tn = min(tm, M), _tile(N, tn)

    def body(a_ref, b_ref, r_ref, o_ref):
        o_ref[...] = r_ref[...] + scale * _dot(a_ref[...], b_ref[...])

    return pl.pallas_call(
        body, name=name, grid=(M // tm, N // tn),
        in_specs=[pl.BlockSpec((tm, K), lambda i, j: (i, 0)), pl.BlockSpec((K, tn), lambda i, j: (0, j)),
                  pl.BlockSpec((tm, tn), lambda i, j: (i, j))],
        out_specs=pl.BlockSpec((tm, tn), lambda i, j: (i, j)),
        out_shape=jax.ShapeDtypeStruct((M, N), F32),
        compiler_params=_params("parallel", "parallel"),
    )(a, b, res)


def _mm_nt(a, b, name, tm=1024, tn=512):
    M, K = a.shape
    N = b.shape[0]
    tm, tn = min(tm, M), _tile(N, tn)

    def body(a_ref, b_ref, o_ref):
        o_ref[...] = _dot_nt(a_ref[...], b_ref[...])

    return pl.pallas_call(
        body, name=name, grid=(M // tm, N // tn),
        in_specs=[pl.BlockSpec((tm, K), lambda i, j: (i, 0)), pl.BlockSpec((tn, K), lambda i, j: (j, 0))],
        out_specs=pl.BlockSpec((tm, tn), lambda i, j: (i, j)),
        out_shape=jax.ShapeDtypeStruct((M, N), F32),
        compiler_params=_params("parallel", "parallel"),
    )(a, b)


def _dh_rms_bwd(pairs, x, g, dres, copy_scale, name, tm=512):
    T = x.shape[0]
    tm = min(tm, T)
    n = len(pairs)

    def body(*refs):
        ab = refs[:2 * n]
        x_ref, g_ref, dres_ref, dx_ref, dxb_ref, dg_ref = refs[2 * n:]
        dy = _dot(ab[0][...], ab[1][...])
        for k in range(1, n):
            dy = dy + _dot(ab[2 * k][...], ab[2 * k + 1][...])
        xv = x_ref[...]
        r = lax.rsqrt(jnp.mean(xv * xv, axis=-1, keepdims=True) + EPS)
        xhat = xv * r
        dxhat = dy * g_ref[...]
        dx = dres_ref[...] + r * (dxhat - xhat * jnp.mean(dxhat * xhat, axis=-1, keepdims=True))
        dx_ref[...] = dx
        dxb_ref[...] = (copy_scale * dx).astype(BF16)

        @pl.when(pl.program_id(0) == 0)
        def _():
            dg_ref[...] = jnp.zeros_like(dg_ref)

        dg_ref[...] += jnp.sum(dy * xhat, axis=0, keepdims=True)

    row = pl.BlockSpec((tm, D), lambda i: (i, 0))
    vec = pl.BlockSpec((1, D), lambda i: (0, 0))
    in_specs, operands = [], []
    for a, b in pairs:
        in_specs += [pl.BlockSpec((tm, a.shape[1]), lambda i: (i, 0)),
                     pl.BlockSpec(b.shape, lambda i: (0, 0), pipeline_mode=pl.Buffered(1))]
        operands += [a, b]
    return pl.pallas_call(
        body, name=name, grid=(T // tm,),
        in_specs=in_specs + [row, vec, row], out_specs=[row, row, vec],
        out_shape=[jax.ShapeDtypeStruct((T, D), F32), jax.ShapeDtypeStruct((T, D), BF16),
                   jax.ShapeDtypeStruct((1, D), F32)],
        compiler_params=_params("arbitrary"),
    )(*operands, x, g, dres)


def _mm_tn(a, b, scale, name, tm=1792, tn=1792, tk=2048):
    T, M = a.shape
    N = b.shape[1]
    tm, tn, tk = _tile(M, tm), _tile(N, tn), min(tk, T)
    nk = T // tk

    def body(a_ref, b_ref, o_ref, acc_ref):
        k = pl.program_id(2)

        @pl.when(k == 0)
        def _():
            acc_ref[...] = jnp.zeros_like(acc_ref)

        acc_ref[...] += _dot_tn(a_ref[...], b_ref[...])

        @pl.when(k == nk - 1)
        def _():
            o_ref[...] = (scale * acc_ref[...]).astype(BF16)

    return pl.pallas_call(
        body, name=name, grid=(M // tm, N // tn, nk),
        in_specs=[pl.BlockSpec((tk, tm), lambda i, j, k: (k, i)), pl.BlockSpec((tk, tn), lambda i, j, k: (k, j))],
        out_specs=pl.BlockSpec((tm, tn), lambda i, j, k: (i, j)),
        out_shape=jax.ShapeDtypeStruct((M, N), BF16),
        scratch_shapes=[pltpu.VMEM((tm, tn), F32)],
        compiler_params=_params("parallel", "parallel", "arbitrary"),
    )(a, b)


def _rms_rows(x_ref, gain_ref):
    xv = x_ref[...]
    r = lax.rsqrt(jnp.mean(xv * xv, axis=-1, keepdims=True) + EPS)
    return (xv * r * gain_ref[...]).astype(BF16)


def _ffn_in(x, gain, wg_t, wu_t, tm=512, tn=FF_HALF):
    T = x.shape[0]
    tm = min(tm, T)

    def body(x_ref, gain_ref, wg_ref, wu_ref, h_ref, dg_ref, du_ref, a_ref):
        hv = _rms_rows(x_ref, gain_ref)

        @pl.when(pl.program_id(0) == 0)
        def _():
            h_ref[...] = hv

        g = _dot_nt(hv, wg_ref[...])
        u = _dot_nt(hv, wu_ref[...])
        s = _sigmoid(g)
        gs = g * s
        dg_ref[...] = (u * (s + gs - gs * s)).astype(BF16)
        du_ref[...] = gs.astype(BF16)
        a_ref[...] = (gs * u).astype(BF16)

    row = pl.BlockSpec((tm, D), lambda j, i: (i, 0))
    last = T // tm - 1
    h_spec = pl.BlockSpec((tm, D), lambda j, i: (jnp.where(j == 0, i, last), 0))
    wspec = pl.BlockSpec((tn, D), lambda j, i: (j, 0))
    ospec = pl.BlockSpec((tm, tn), lambda j, i: (i, j))
    oshape = jax.ShapeDtypeStruct((T, D_FFP), BF16)
    return pl.pallas_call(
        body, name="ffn_in", grid=(D_FFP // tn, T // tm),
        in_specs=[row, pl.BlockSpec((1, D), lambda j, i: (0, 0)), wspec, wspec],
        out_specs=[h_spec, ospec, ospec, ospec],
        out_shape=[jax.ShapeDtypeStruct((T, D), BF16), oshape, oshape, oshape],
        compiler_params=_params("arbitrary", "arbitrary"),
    )(x, gain, wg_t, wu_t)


def _mix_in(x, gain, w_in_t, tm=512):
    T = x.shape[0]
    tm = min(tm, T)

    def body(x_ref, gain_ref, w_ref, h_ref, p_ref):
        hv = _rms_rows(x_ref, gain_ref)
        h_ref[...] = hv
        p_ref[...] = _dot_nt(hv, w_ref[...])

    row = pl.BlockSpec((tm, D), lambda i: (i, 0))
    return pl.pallas_call(
        body, name="mix_in", grid=(T // tm,),
        in_specs=[row, pl.BlockSpec((1, D), lambda i: (0, 0)), pl.BlockSpec((D_IN, D), lambda i: (0, 0))],
        out_specs=[row, pl.BlockSpec((tm, D_IN), lambda i: (i, 0))],
        out_shape=[jax.ShapeDtypeStruct((T, D), BF16), jax.ShapeDtypeStruct((T, D_IN), F32)],
        compiler_params=_params("parallel"),
    )(x, gain, w_in_t)


def _ffn_mid_bwd(dyh, wout, da_dg, da_du, tm=512, tn=FF_HALF):
    T = dyh.shape[0]
    tm = min(tm, T)

    def body(dy_ref, w_ref, pg_ref, pu_ref, dg_ref, du_ref):
        da = _dot_nt(dy_ref[...], w_ref[...])
        dg_ref[...] = (da * pg_ref[...].astype(F32)).astype(BF16)
        du_ref[...] = (da * pu_ref[...].astype(F32)).astype(BF16)

    ospec = pl.BlockSpec((tm, tn), lambda j, i: (i, j))
    oshape = jax.ShapeDtypeStruct((T, D_FFP), BF16)
    return pl.pallas_call(
        body, name="ffn_mid_bwd", grid=(D_FFP // tn, T // tm),
        in_specs=[pl.BlockSpec((tm, D), lambda j, i: (i, 0)), pl.BlockSpec((tn, D), lambda j, i: (j, 0)),
                  ospec, ospec],
        out_specs=[ospec, ospec], out_shape=[oshape, oshape],
        compiler_params=_params("parallel", "parallel"),
    )(dyh, wout, da_dg, da_du)


def _final(x, tgt, gf, copy_scale, tm=512):
    T = x.shape[0]
    tm = min(tm, T)

    def body(x_ref, t_ref, g_ref, dx_ref, dxb_ref, dg_ref, loss_ref):
        xv = x_ref[...]
        r = lax.rsqrt(jnp.mean(xv * xv, axis=-1, keepdims=True) + EPS)
        xhat = xv * r
        err = xhat * g_ref[...] - t_ref[...]
        dy = err * (1.0 / D)
        dxhat = dy * g_ref[...]
        dx = r * (dxhat - xhat * jnp.mean(dxhat * xhat, axis=-1, keepdims=True))
        dx_ref[...] = dx
        dxb_ref[...] = (copy_scale * dx).astype(BF16)

        @pl.when(pl.program_id(0) == 0)
        def _():
            dg_ref[...] = jnp.zeros_like(dg_ref)
            loss_ref[...] = jnp.zeros_like(loss_ref)

        dg_ref[...] += jnp.sum(dy * xhat, axis=0, keepdims=True)
        loss_ref[...] += (0.5 / D) * jnp.sum(err * err)

    row = pl.BlockSpec((tm, D), lambda i: (i, 0))
    vec = pl.BlockSpec((1, D), lambda i: (0, 0))
    return pl.pallas_call(
        body, name="final_loss", grid=(T // tm,),
        in_specs=[row, row, vec],
        out_specs=[row, row, vec, pl.BlockSpec((1, 128), lambda i: (0, 0))],
        out_shape=[jax.ShapeDtypeStruct((T, D), F32), jax.ShapeDtypeStruct((T, D), BF16),
                   jax.ShapeDtypeStruct((1, D), F32), jax.ShapeDtypeStruct((1, 128), F32)],
        compiler_params=_params("arbitrary"),
    )(x, tgt, gf)


def _mix_block(T, limit):
    return min(limit, T // 2)


def _rows(tb, width):
    return lax.broadcasted_iota(jnp.int32, (tb, width), 0)


def _rglru_gates(xc, wax_ref, bax_ref, lam_ref):
    pre = _dot(xc.astype(BF16), wax_ref[...]) + bax_ref[...]
    r = _sigmoid(pre[:, :D_RNN])
    ig = _sigmoid(pre[:, D_RNN:])
    z = -lam_ref[...]
    sp = jnp.maximum(z, 0.0) + jnp.log(1.0 + jnp.exp(-jnp.abs(z)))
    log_a = (-LRU_C) * r * sp
    a = jnp.exp(log_a)
    mult = jnp.sqrt(-jnp.tanh(log_a) * (1.0 + a * a))
    return r, ig, sp, a, mult


def _conv(xa_ext, cw_ref, cb_ref):
    y = cb_ref[...] + cw_ref[3:4, :] * xa_ext
    for k in range(1, 4):
        y = y + cw_ref[3 - k:4 - k, :] * pltpu.roll(xa_ext, k, 0)
    return y[HALO:]


def _pool_window_lanes():
    lane = lax.broadcasted_iota(jnp.int32, (1, D_POOL), 1)
    return jnp.where(lane < 64, 2, jnp.where(lane < 128, 4, jnp.where(lane < 192, 8, 16)))


def _pool_select(s2, s4, s8, s16):
    lane = lax.broadcasted_iota(jnp.int32, s2.shape, 1)
    return jnp.where(lane < 64, s2, jnp.where(lane < 128, s4, jnp.where(lane < 192, s8, s16)))


def _pool_diff(xp_ext, t0, tb):
    s2 = xp_ext + pltpu.roll(xp_ext, 1, 0)
    s4 = s2 + pltpu.roll(s2, 2, 0)
    s8 = s4 + pltpu.roll(s4, 4, 0)
    s16 = s8 + pltpu.roll(s8, 8, 0)
    sel = _pool_select(s2, s4, s8, s16)[HALO:]
    cnt = jnp.minimum(t0 + _rows(tb, D_POOL) + 1, _pool_window_lanes()).astype(F32)
    return sel / cnt - xp_ext[HALO:], cnt


def _head_masks():
    lane = lax.broadcasted_iota(jnp.int32, (1, D_SGU), 1)
    return [((lane >= 64 * h) & (lane < 64 * (h + 1))).astype(F32) for h in range(4)]


def _sgu_mix(w_ref, vch, masks):
    z = masks[0] * _dot(w_ref[0], vch)
    for h in range(1, 4):
        z = z + masks[h] * _dot(w_ref[h], vch)
    return z


def _mix_fwd(p, prm):
    T = p.shape[0]
    tb = _mix_block(T, 512)
    nb = T // tb

    def body(p_ref, xah_ref, xph_ref, cw_ref, cb_ref, wax_ref, bax_ref, lam_ref, wp_ref, ps_ref, sgn_ref,
             ws_ref, bz_ref, y_ref, hs_ref, carry_ref):
        i = pl.program_id(0)
        keep = (i > 0).astype(F32)

        @pl.when(i == 0)
        def _():
            carry_ref[...] = jnp.zeros_like(carry_ref)

        xa_ext = jnp.concatenate([xah_ref[...] * keep, p_ref[:, 512:1024]], axis=0)
        xc = _conv(xa_ext, cw_ref, cb_ref)
        r, ig, sp, a, mult = _rglru_gates(xc, wax_ref, bax_ref, lam_ref)
        bv = mult * (ig * xc)
        row = _rows(tb, D_RNN)
        s = 1
        while s < tb:
            m = row >= s
            bv = jnp.where(m, a * pltpu.roll(bv, s, 0) + bv, bv)
            a = jnp.where(m, a * pltpu.roll(a, s, 0), a)
            s *= 2
        h = bv + a * carry_ref[0:1, :]
        hs_ref[...] = h
        last = jnp.sum(jnp.where(_rows(8, D_RNN) == 7, hs_ref[tb - 8:tb, :], 0.0), axis=0, keepdims=True)
        carry_ref[...] = jnp.broadcast_to(last, carry_ref.shape)
        y_ref[:, 0:512] = (_gelu(p_ref[:, 0:512]) * h).astype(BF16)

        xp_ext = jnp.concatenate([xph_ref[...] * keep, p_ref[:, 1024:1280]], axis=0)
        d, _ = _pool_diff(xp_ext, i * tb, tb)
        y_ref[:, 512:768] = (_dot(d.astype(BF16), wp_ref[...]) * ps_ref[...]).astype(BF16)

        ug = _gelu(p_ref[:, 1280:1536])
        vg = _gelu(p_ref[:, 1536:1792])
        rv = lax.rsqrt(jnp.mean(vg * vg, axis=-1, keepdims=True) + EPS)
        vn = (vg * rv * sgn_ref[...]).astype(BF16)
        masks = _head_masks()
        for ci in range(tb // CHUNK):
            sl = slice(ci * CHUNK, (ci + 1) * CHUNK)
            z = _sgu_mix(ws_ref, vn[sl], masks) + bz_ref[...]
            y_ref[sl, 768:1024] = (ug[sl] * z).astype(BF16)

    hb = tb // HALO

    def halo(i):
        return jnp.maximum(i * hb - 1, 0)

    def full(shape):
        return pl.BlockSpec(shape, lambda i: (0,) * len(shape))

    return pl.pallas_call(
        body, name="mix_fwd", grid=(nb,),
        in_specs=[pl.BlockSpec((tb, D_IN), lambda i: (i, 0)),
                  pl.BlockSpec((HALO, D_RNN), lambda i: (halo(i), 1)),
                  pl.BlockSpec((HALO, D_POOL), lambda i: (halo(i), 4)),
                  full((4, D_RNN)), full((1, D_RNN)), full((D_RNN, 2 * D_RNN)), full((1, 2 * D_RNN)),
                  full((1, D_RNN)), full((D_POOL, D_POOL)), full((1, D_POOL)), full((1, D_SGU)),
                  full((4, CHUNK, CHUNK)), full((CHUNK, D_SGU))],
        out_specs=[pl.BlockSpec((tb, D), lambda i: (i, 0)), pl.BlockSpec((tb, D_RNN), lambda i: (i, 0))],
        out_shape=[jax.ShapeDtypeStruct((T, D), BF16), jax.ShapeDtypeStruct((T, D_RNN), F32)],
        scratch_shapes=[pltpu.VMEM((8, D_RNN), F32)],
        compiler_params=_params("arbitrary"),
    )(p, p, p, prm["conv_w"], prm["conv_b"], prm["wax"], prm["bax"], prm["lam"], prm["wpool"], prm["pool_scale"],
      prm["sgu_norm"], prm["ws"], prm["bz"])


def _mix_bwd(dy, p, hs, prm):
    T = p.shape[0]
    tb = _mix_block(T, 256)
    nb = T // tb
    hb = tb // HALO

    def body(dy_ref, p_ref, xah_ref, xph_ref, hs_ref, hsh_ref, cw_ref, cb_ref, wax_ref, waxt_ref, bax_ref,
             lam_ref, wp_ref, wpt_ref, ps_ref, sgn_ref, ws_ref, wst_ref, bz_ref,
             dp_ref, dcw_ref, dcb_ref, dwax_ref, dbax_ref, dlam_ref, dwp_ref, dps_ref, dsgn_ref, dws_ref,
             dbz_ref, gcarry_ref, xcfut_ref, mfut_ref):
        i = pl.program_id(0)
        bi = nb - 1 - i
        keep = (bi > 0).astype(F32)

        @pl.when(i == 0)
        def _():
            for ref in (dcw_ref, dcb_ref, dwax_ref, dbax_ref, dlam_ref, dwp_ref, dps_ref, dsgn_ref, dws_ref,
                        dbz_ref, gcarry_ref, xcfut_ref, mfut_ref):
                ref[...] = jnp.zeros_like(ref)

        xa_ext = jnp.concatenate([xah_ref[...] * keep, p_ref[:, 512:1024]], axis=0)
        xc = _conv(xa_ext, cw_ref, cb_ref)
        r, ig, sp, a, mult = _rglru_gates(xc, wax_ref, bax_ref, lam_ref)
        gg, dgg = _gelu_and_grad(p_ref[:, 0:512])
        dya = dy_ref[:, 0:512]
        dp_ref[:, 0:512] = (dya * hs_ref[...] * dgg).astype(BF16)
        row = _rows(tb, D_RNN)
        g = dya * gg + jnp.where(row == tb - 1, gcarry_ref[0:1, :], 0.0)
        al = pltpu.roll(a, tb - 1, 0)
        s = 1
        while s < tb:
            m = row < tb - s
            g = jnp.where(m, al * pltpu.roll(g, tb - s, 0) + g, g)
            al = jnp.where(m, al * pltpu.roll(al, tb - s, 0), al)
            s *= 2
        first = jnp.sum(jnp.where(_rows(8, D_RNN) == 0, (a * g)[0:8], 0.0), axis=0, keepdims=True)
        gcarry_ref[...] = jnp.broadcast_to(first, gcarry_ref.shape)
        hs_ext = jnp.concatenate([hsh_ref[...] * keep, hs_ref[...]], axis=0)
        h_prev = pltpu.roll(hs_ext, 1, 0)[HALO:]
        ix = ig * xc
        dlog_a = g * h_prev * a - (g * ix) * (a * a / mult)
        dlam_ref[...] += jnp.sum(dlog_a * r, axis=0, keepdims=True) * (LRU_C * _sigmoid(-lam_ref[...]))
        dpre_r = dlog_a * ((-LRU_C) * sp) * (r * (1.0 - r))
        dpre_i = (g * mult * xc) * (ig * (1.0 - ig))
        dpre = jnp.concatenate([dpre_r, dpre_i], axis=1)
        dbax_ref[...] += jnp.sum(dpre, axis=0, keepdims=True)
        dpre_b = dpre.astype(BF16)
        dwax_ref[...] += _dot_tn(xc.astype(BF16), dpre_b)
        dxc = g * mult * ig + _dot(dpre_b, waxt_ref[...])
        dcb_ref[...] += jnp.sum(dxc, axis=0, keepdims=True)
        for k in range(4):
            xs = xa_ext[HALO:] if k == 3 else pltpu.roll(xa_ext, 3 - k, 0)[HALO:]
            dcw_ref[k:k + 1, :] += jnp.sum(dxc * xs, axis=0, keepdims=True)
        dxc_ext = jnp.concatenate([dxc, xcfut_ref[...]], axis=0)
        n = tb + HALO
        dxa = cw_ref[3:4, :] * dxc_ext
        for k in range(1, 4):
            dxa = dxa + cw_ref[3 - k:4 - k, :] * pltpu.roll(dxc_ext, n - k, 0)
        dp_ref[:, 512:1024] = dxa[:tb].astype(BF16)
        xcfut_ref[...] = dxc[0:HALO]

        xp_ext = jnp.concatenate([xph_ref[...] * keep, p_ref[:, 1024:1280]], axis=0)
        d, cnt = _pool_diff(xp_ext, bi * tb, tb)
        db = d.astype(BF16)
        dyb = dy_ref[:, 512:768]
        dps_ref[...] += jnp.sum(dyb * _dot(db, wp_ref[...]), axis=0, keepdims=True)
        dq = (dyb * ps_ref[...]).astype(BF16)
        dwp_ref[...] += _dot_tn(db, dq)
        dd = _dot(dq, wpt_ref[...])
        mm = dd / cnt
        m_ext = jnp.concatenate([mm, mfut_ref[...]], axis=0)
        f2 = m_ext + pltpu.roll(m_ext, n - 1, 0)
        f4 = f2 + pltpu.roll(f2, n - 2, 0)
        f8 = f4 + pltpu.roll(f4, n - 4, 0)
        f16 = f8 + pltpu.roll(f8, n - 8, 0)
        dp_ref[:, 1024:1280] = (_pool_select(f2, f4, f8, f16)[:tb] - dd).astype(BF16)
        mfut_ref[...] = mm[0:HALO]

        ug, dug = _gelu_and_grad(p_ref[:, 1280:1536])
        vg, dvg = _gelu_and_grad(p_ref[:, 1536:1792])
        rv = lax.rsqrt(jnp.mean(vg * vg, axis=-1, keepdims=True) + EPS)
        vhat = vg * rv
        vn = (vhat * sgn_ref[...]).astype(BF16)
        dyc = dy_ref[:, 768:1024]
        masks = _head_masks()
        dz = dyc * ug
        dzb = dz.astype(BF16)
        dvn_parts = []
        for ci in range(tb // CHUNK):
            sl = slice(ci * CHUNK, (ci + 1) * CHUNK)
            z = _sgu_mix(ws_ref, vn[sl], masks) + bz_ref[...]
            dp_ref[sl, 1280:1536] = (dyc[sl] * z * dug[sl]).astype(BF16)
            dbz_ref[...] += dz[sl]
            for h in range(4):
                dws_ref[h] += _dot_nt((dz[sl] * masks[h]).astype(BF16), vn[sl])
            dvn_parts.append(_sgu_mix(wst_ref, dzb[sl], masks))
        dvn = jnp.concatenate(dvn_parts, axis=0)
        dsgn_ref[...] += jnp.sum(dvn * vhat, axis=0, keepdims=True)
        dvhat = dvn * sgn_ref[...]
        dvg_in = rv * (dvhat - vhat * jnp.mean(dvhat * vhat, axis=-1, keepdims=True))
        dp_ref[:, 1536:1792] = (dvg_in * dvg).astype(BF16)

        @pl.when(i == nb - 1)
        def _():
            tril = (lax.broadcasted_iota(jnp.int32, (CHUNK, CHUNK), 0)
                    >= lax.broadcasted_iota(jnp.int32, (CHUNK, CHUNK), 1)).astype(F32)
            for h in range(4):
                dws_ref[h] = dws_ref[h] * tril

    def blk(i):
        return nb - 1 - i

    def halo(i):
        return jnp.maximum(blk(i) * hb - 1, 0)

    def full(shape):
        return pl.BlockSpec(shape, lambda i: (0,) * len(shape))

    small_shapes = [(4, D_RNN), (1, D_RNN), (D_RNN, 2 * D_RNN), (1, 2 * D_RNN), (1, D_RNN), (D_POOL, D_POOL),
                    (1, D_POOL), (1, D_SGU), (4, CHUNK, CHUNK), (CHUNK, D_SGU)]
    outs = pl.pallas_call(
        body, name="mix_bwd", grid=(nb,),
        in_specs=[pl.BlockSpec((tb, D), lambda i: (blk(i), 0)),
                  pl.BlockSpec((tb, D_IN), lambda i: (blk(i), 0)),
                  pl.BlockSpec((HALO, D_RNN), lambda i: (halo(i), 1)),
                  pl.BlockSpec((HALO, D_POOL), lambda i: (halo(i), 4)),
                  pl.BlockSpec((tb, D_RNN), lambda i: (blk(i), 0)),
                  pl.BlockSpec((HALO, D_RNN), lambda i: (halo(i), 0)),
                  full((4, D_RNN)), full((1, D_RNN)), full((D_RNN, 2 * D_RNN)), full((2 * D_RNN, D_RNN)),
                  full((1, 2 * D_RNN)), full((1, D_RNN)), full((D_POOL, D_POOL)), full((D_POOL, D_POOL)),
                  full((1, D_POOL)), full((1, D_SGU)), full((4, CHUNK, CHUNK)), full((4, CHUNK, CHUNK)),
                  full((CHUNK, D_SGU))],
        out_specs=[pl.BlockSpec((tb, D_IN), lambda i: (blk(i), 0))] + [full(s) for s in small_shapes],
        out_shape=[jax.ShapeDtypeStruct((T, D_IN), BF16)] + [jax.ShapeDtypeStruct(s, F32) for s in small_shapes],
        scratch_shapes=[pltpu.VMEM((8, D_RNN), F32), pltpu.VMEM((HALO, D_RNN), F32),
                        pltpu.VMEM((HALO, D_POOL), F32)],
        compiler_params=_params("arbitrary"),
    )(dy, p, p, p, hs, hs, prm["conv_w"], prm["conv_b"], prm["wax"], prm["wax_t"], prm["bax"], prm["lam"],
      prm["wpool"], prm["wpool_t"], prm["pool_scale"], prm["sgu_norm"], prm["ws"], prm["ws_t"], prm["bz"])
    names = ("dp", "conv_w", "conv_b", "wax", "bax", "lam", "wpool", "pool_scale", "sgu_norm", "ws", "bz")
    return dict(zip(names, outs))


ANY = pl.BlockSpec(memory_space=pl.ANY)


def _place():
    x, y, c = lax.axis_index("x"), lax.axis_index("y"), lax.axis_index("c")
    return x, y, c


def _all_to_all(xs, name):
    def body(in_ref, out_ref, send_sems, recv_sems, local_sem):
        x, y, c = _place()
        me = 4 * x + 2 * y + c
        mine = pltpu.make_async_copy(in_ref.at[me], out_ref.at[me], local_sem)
        mine.start()
        copies = []
        for rel in range(1, N_DEV):
            tx = 1 - x if rel & 4 else x
            ty = 1 - y if rel & 2 else y
            tc = 1 - c if rel & 1 else c
            cp = pltpu.make_async_remote_copy(
                src_ref=in_ref.at[4 * tx + 2 * ty + tc], dst_ref=out_ref.at[me],
                send_sem=send_sems.at[rel - 1], recv_sem=recv_sems.at[rel - 1],
                device_id=(tx, ty, tc), device_id_type=MESH)
            cp.start()
            copies.append(cp)
        for cp in copies:
            cp.wait()
        mine.wait()

    return pl.pallas_call(
        body, name=name, in_specs=[ANY], out_specs=ANY,
        out_shape=jax.ShapeDtypeStruct(xs.shape, xs.dtype),
        scratch_shapes=[pltpu.SemaphoreType.DMA((N_DEV - 1,)), pltpu.SemaphoreType.DMA((N_DEV - 1,)),
                        pltpu.SemaphoreType.DMA],
    )(xs)


def _all_gather8(xs, name):
    def body(x_ref, out_ref, send_sems, recv_sems, local_sem):
        x, y, c = _place()
        me, sibling = (x, y, c), (x, y, 1 - c)
        chips = [(1 - x, y), (x, 1 - y), (1 - x, 1 - y)]

        def rows(px, py, pc):
            return out_ref.at[4 * px + 2 * py + pc]

        def copy(k, block, to, src=None):
            return pltpu.make_async_remote_copy(
                src_ref=rows(*block) if src is None else src, dst_ref=rows(*block),
                send_sem=send_sems.at[k], recv_sem=recv_sems.at[k], device_id=to, device_id_type=MESH)

        mine = pltpu.make_async_copy(x_ref, rows(*me), local_sem)
        mine.start()
        first = [copy(0, me, sibling, src=x_ref)]
        first += [copy(1 + j, me, (*chip, c), src=x_ref) for j, chip in enumerate(chips)]
        for cp in first:
            cp.start()
        passed = [copy(4 + j, (*chip, c), sibling) for j, chip in enumerate(chips)]
        for j, chip in enumerate(chips):
            copy(1 + j, (*chip, c), me).wait_recv()
            passed[j].start()
        copy(0, sibling, me).wait_recv()
        for j, chip in enumerate(chips):
            copy(4 + j, (*chip, 1 - c), me).wait_recv()
        for cp in first + passed:
            cp.wait_send()
        mine.wait()

    return pl.pallas_call(
        body, name=name, in_specs=[ANY], out_specs=ANY,
        out_shape=jax.ShapeDtypeStruct((N_DEV,) + xs.shape, xs.dtype),
        scratch_shapes=[pltpu.SemaphoreType.DMA((7,)), pltpu.SemaphoreType.DMA((7,)), pltpu.SemaphoreType.DMA],
    )(xs)


HBM = pl.BlockSpec(memory_space=pltpu.HBM)
SEM = pl.BlockSpec(memory_space=pltpu.SEMAPHORE)
EFFECT = pltpu.SideEffectType.DATAFLOW_SIDE_EFFECTING


def _in_hbm(a):
    return pltpu.with_memory_space_constraint(a, pltpu.HBM)


def _local_copy(src, dst, stage, sem):
    load = pltpu.make_async_copy(src, stage, sem)
    load.start()
    load.wait()
    store = pltpu.make_async_copy(stage, dst, sem)
    store.start()
    store.wait()


def _unique(windows):
    arrays = []
    for per_chip in windows:
        for arr, _ in per_chip:
            if not any(arr is a for a in arrays):
                arrays.append(arr)
    return arrays


def _exchange_start(layer, windows, lands, name):
    arrays = _unique(windows)
    na, nt = len(arrays), len(windows)

    def body(*refs):
        in_refs, land_refs = refs[:na], refs[na:na + nt]
        send_sems, recv_sems = refs[na + nt], refs[na + nt + 1]
        token = refs[-1]
        x, y, c = _place()
        me = 4 * x + 2 * y + c
        for t in range(nt):
            for j in range(N_CHIP):
                arr, window = windows[t][j]
                src = window(in_refs[next(i for i, a in enumerate(arrays) if a is arr)])

                @pl.when(me != 2 * j + layer)
                def _():
                    pltpu.make_async_remote_copy(
                        src_ref=src, dst_ref=land_refs[t].at[me], send_sem=send_sems.at[N_CHIP * t + j],
                        recv_sem=recv_sems.at[N_DEV * t + me], device_id=(j // 2, j % 2, layer),
                        device_id_type=MESH).start()
        token[...] = jnp.zeros_like(token)

    outs = pl.pallas_call(
        body, name=name,
        out_shape=(pltpu.SemaphoreType.DMA((N_CHIP * nt,)), pltpu.SemaphoreType.DMA((N_DEV * nt,)),
                   *[pltpu.HBM(a.shape, a.dtype) for a in lands], jax.ShapeDtypeStruct((8, 128), F32)),
        in_specs=[HBM] * (na + nt),
        out_specs=(SEM, SEM, *[HBM] * nt, pl.BlockSpec(memory_space=pltpu.VMEM)),
        input_output_aliases={na + t: 2 + t for t in range(nt)},
        compiler_params=pltpu.CompilerParams(has_side_effects=EFFECT, vmem_limit_bytes=VMEM_LIMIT),
    )(*[_in_hbm(a) for a in arrays], *[_in_hbm(a) for a in lands])
    return outs[0], outs[1], list(outs[2:2 + nt]), outs[-1]


def _exchange_wait(layer, windows, lands, send_sems, recv_sems, after, name):
    arrays = _unique(windows)
    na, nt = len(arrays), len(windows)

    def body(*refs):
        in_refs, land_refs = refs[:na], refs[na:na + nt]
        send_sems, recv_sems = refs[na + nt], refs[na + nt + 1]
        stages, local_sem = refs[-1 - nt:-1], refs[-1]
        x, y, c = _place()
        me = 4 * x + 2 * y + c

        def source(t, j):
            arr, window = windows[t][j]
            return window(in_refs[next(i for i, a in enumerate(arrays) if a is arr)])

        @pl.when(c == layer)
        def _():
            for t in range(nt):
                for j in range(N_CHIP):
                    @pl.when(me == 2 * j + layer)
                    def _():
                        _local_copy(source(t, j), land_refs[t].at[me], stages[t], local_sem)

        for t in range(nt):
            for j in range(N_CHIP):
                @pl.when(me != 2 * j + layer)
                def _():
                    pltpu.make_async_remote_copy(
                        src_ref=source(t, j), dst_ref=land_refs[t].at[me], send_sem=send_sems.at[N_CHIP * t + j],
                        recv_sem=recv_sems.at[N_DEV * t + me], device_id=(j // 2, j % 2, layer),
                        device_id_type=MESH).wait_send()

        @pl.when(c == layer)
        def _():
            for t in range(nt):
                for s in range(N_DEV):
                    @pl.when(me != s)
                    def _():
                        slot = land_refs[t].at[s]
                        pltpu.make_async_remote_copy(
                            src_ref=slot, dst_ref=slot, send_sem=send_sems.at[N_CHIP * t],
                            recv_sem=recv_sems.at[N_DEV * t + s], device_id=(x, y, c),
                            device_id_type=MESH).wait_recv()

    outs = pl.pallas_call(
        body, name=name,
        out_shape=tuple(pltpu.HBM(a.shape, a.dtype) for a in lands),
        in_specs=[HBM] * (na + nt) + [SEM, SEM, ANY],
        out_specs=tuple([HBM] * nt),
        input_output_aliases={na + t: t for t in range(nt)},
        scratch_shapes=[pltpu.VMEM(a.shape[1:], a.dtype) for a in lands] + [pltpu.SemaphoreType.DMA],
        compiler_params=pltpu.CompilerParams(has_side_effects=EFFECT, vmem_limit_bytes=VMEM_LIMIT),
    )(*[_in_hbm(a) for a in arrays], *lands, send_sems, recv_sems, after)
    return list(outs)


def _other_chips(x, y):
    return [(1 - x, y), (x, 1 - y), (1 - x, 1 - y)]


def _split_rows(rows):
    if rows < 32:
        return [(0, rows), (rows, 0)]
    cut = -(-(rows // 2) // 16) * 16
    return [(0, cut), (cut, rows - cut)]


def _gather_copies(in_refs, land_refs, send_sems, recv_sems, x, y, sender_core, both):
    copies = []
    for t, src in enumerate(in_refs):
        r0, n = _split_rows(src.shape[0])[sender_core]
        if n == 0:
            continue
        for rel, (tx, ty) in enumerate(_other_chips(x, y)):
            for tc in (0, 1) if both else (sender_core,):
                copies.append(dict(
                    src_ref=src.at[pl.ds(r0, n)], dst_ref=land_refs[t].at[2 * x + y].at[pl.ds(r0, n)],
                    send_sem=send_sems.at[6 * t + 2 * rel + tc],
                    recv_sem=recv_sems.at[2 * (3 * t + rel) + sender_core],
                    device_id=(tx, ty, tc), device_id_type=MESH))
    return copies


def _gather_start(shards, lands, name, both):
    nt = len(shards)

    def body(*refs):
        in_refs, land_refs = refs[:nt], refs[nt:2 * nt]
        send_sems, recv_sems = refs[2 * nt], refs[2 * nt + 1]
        token = refs[-1]
        x, y, c = _place()
        for core in range(2):
            @pl.when(c == core)
            def _():
                for cp_args in _gather_copies(in_refs, land_refs, send_sems, recv_sems, x, y, core, both):
                    pltpu.make_async_remote_copy(**cp_args).start()
        token[...] = jnp.zeros_like(token)

    outs = pl.pallas_call(
        body, name=name,
        out_shape=(pltpu.SemaphoreType.DMA((6 * nt,)), pltpu.SemaphoreType.DMA((6 * nt,)),
                   *[pltpu.HBM(a.shape, a.dtype) for a in lands], jax.ShapeDtypeStruct((8, 128), F32)),
        in_specs=[HBM] * (2 * nt),
        out_specs=(SEM, SEM, *[HBM] * nt, pl.BlockSpec(memory_space=pltpu.VMEM)),
        input_output_aliases={nt + t: 2 + t for t in range(nt)},
        compiler_params=pltpu.CompilerParams(has_side_effects=EFFECT, vmem_limit_bytes=VMEM_LIMIT),
    )(*[_in_hbm(a) for a in shards], *[_in_hbm(a) for a in lands])
    return outs[0], outs[1], list(outs[2:2 + nt]), outs[-1]


def _gather_wait(shards, lands, send_sems, recv_sems, after, name, both):
    nt = len(shards)

    def body(*refs):
        in_refs, land_refs = refs[:nt], refs[nt:2 * nt]
        send_sems, recv_sems = refs[2 * nt], refs[2 * nt + 1]
        stages, local_sem = refs[-1 - nt:-1], refs[-1]
        x, y, c = _place()

        for t in range(nt):
            _local_copy(in_refs[t], land_refs[t].at[2 * x + y], stages[t], local_sem)

        for core in range(2):
            @pl.when(c == core)
            def _():
                for cp_args in _gather_copies(in_refs, land_refs, send_sems, recv_sems, x, y, core, both):
                    pltpu.make_async_remote_copy(**cp_args).wait_send()

        def wait_parts_from(core):
            for t in range(nt):
                r0, n = _split_rows(in_refs[t].shape[0])[core]
                for rel, (tx, ty) in enumerate(_other_chips(x, y)):
                    if n > 0:
                        part = land_refs[t].at[2 * tx + ty].at[pl.ds(r0, n)]
                        pltpu.make_async_remote_copy(
                            src_ref=part, dst_ref=part, send_sem=send_sems.at[6 * t],
                            recv_sem=recv_sems.at[2 * (3 * t + rel) + core], device_id=(x, y, c),
                            device_id_type=MESH).wait_recv()

        for core in range(2):
            if both:
                wait_parts_from(core)
            else:
                pl.when(c == core)(functools.partial(wait_parts_from, core))

    outs = pl.pallas_call(
        body, name=name,
        out_shape=tuple(pltpu.HBM(a.shape, a.dtype) for a in lands),
        in_specs=[HBM] * (2 * nt) + [SEM, SEM, ANY],
        out_specs=tuple([HBM] * nt),
        input_output_aliases={nt + t: t for t in range(nt)},
        scratch_shapes=[pltpu.VMEM(a.shape, a.dtype) for a in shards] + [pltpu.SemaphoreType.DMA],
        compiler_params=pltpu.CompilerParams(has_side_effects=EFFECT, vmem_limit_bytes=VMEM_LIMIT),
    )(*[_in_hbm(a) for a in shards], *lands, send_sems, recv_sems, after)
    return list(outs)


def _pair_share(lands, name):
    nt = len(lands)
    splits = [_split_rows(a.shape[1]) for a in lands]

    def body(*refs):
        land_refs = refs[:nt]
        stages = refs[2 * nt:3 * nt]
        load_sems, send_sems, recv_sems = refs[3 * nt:]
        x, y, c = _place()

        def parts(core):
            out = []
            for t in range(nt):
                r0, n = splits[t][core]
                for rel, (tx, ty) in enumerate(_other_chips(x, y)):
                    if n > 0:
                        out.append((3 * t + rel, stages[t].at[rel, pl.ds(0, n)],
                                    land_refs[t].at[2 * tx + ty].at[pl.ds(r0, n)]))
            return out

        def send(core):
            loads = [pltpu.make_async_copy(part, stage, load_sems.at[k]) for k, stage, part in parts(core)]
            for cp in loads:
                cp.start()
            pushes = []
            for cp, (k, stage, part) in zip(loads, parts(core)):
                cp.wait()
                push = pltpu.make_async_remote_copy(src_ref=stage, dst_ref=part, send_sem=send_sems.at[k],
                                                    recv_sem=recv_sems.at[k], device_id=(x, y, 1 - c),
                                                    device_id_type=MESH)
                push.start()
                pushes.append(push)
            for push in pushes:
                push.wait_send()
            for k, _, part in parts(1 - core):
                pltpu.make_async_remote_copy(src_ref=part, dst_ref=part, send_sem=send_sems.at[k],
                                             recv_sem=recv_sems.at[k], device_id=(x, y, 1 - c),
                                             device_id_type=MESH).wait_recv()

        for core in range(2):
            pl.when(c == core)(functools.partial(send, core))

    outs = pl.pallas_call(
        body, name=name, in_specs=[ANY] * nt, out_specs=[ANY] * nt,
        out_shape=[jax.ShapeDtypeStruct(a.shape, a.dtype) for a in lands],
        input_output_aliases={t: t for t in range(nt)},
        scratch_shapes=[pltpu.VMEM((3, max(n for _, n in sp), a.shape[2]), a.dtype) for a, sp in zip(lands, splits)]
        + [pltpu.SemaphoreType.DMA((3 * nt,))] * 3,
        compiler_params=pltpu.CompilerParams(vmem_limit_bytes=VMEM_LIMIT),
    )(*lands)
    return list(outs)


def _tie(a, token):
    def body(a_ref, token_ref, o_ref):
        pass

    return pl.pallas_call(
        body, name="tie", in_specs=[ANY, ANY], out_specs=ANY,
        out_shape=jax.ShapeDtypeStruct(a.shape, a.dtype), input_output_aliases={0: 0},
    )(a, token)


def _sum_share(xs, name):
    _, r, cols = xs.shape
    tr = max(t for t in range(16, min(r, 704) + 1, 16) if r % t == 0)
    nblk = r // tr

    def body(x_ref, out_ref, acc_ref, send_sems, local_sems, recv_sem):
        i = pl.program_id(0)
        slot = i % 2
        x, y, c = _place()

        def copies(s, blk):
            dst = out_ref.at[c, pl.ds(blk * tr, tr), :]
            loc = pltpu.make_async_copy(acc_ref.at[s], dst, local_sems.at[s])
            rem = pltpu.make_async_remote_copy(src_ref=acc_ref.at[s], dst_ref=dst, send_sem=send_sems.at[s],
                                               recv_sem=recv_sem, device_id=(x, y, 1 - c), device_id_type=MESH)
            return loc, rem

        @pl.when(i >= 2)
        def _():
            loc, rem = copies(slot, i - 2)
            loc.wait()
            rem.wait_send()

        acc = x_ref[0].astype(F32)
        for k in range(1, N_DEV):
            acc = acc + x_ref[k].astype(F32)
        acc_ref[slot] = acc
        loc, rem = copies(slot, i)
        loc.start()
        rem.start()

        @pl.when(i == nblk - 1)
        def _():
            for back in range(min(2, nblk)):
                blk = nblk - 1 - back
                loc, rem = copies(blk % 2, blk)
                loc.wait()
                rem.wait_send()
            theirs = out_ref.at[1 - c]
            pltpu.make_async_remote_copy(src_ref=theirs, dst_ref=theirs, send_sem=send_sems.at[0],
                                         recv_sem=recv_sem, device_id=(x, y, 1 - c),
                                         device_id_type=MESH).wait_recv()

    return pl.pallas_call(
        body, name=name, grid=(nblk,),
        in_specs=[pl.BlockSpec((N_DEV, tr, cols), lambda i: (0, i, 0))],
        out_specs=ANY,
        out_shape=jax.ShapeDtypeStruct((2, r, cols), F32),
        scratch_shapes=[pltpu.VMEM((2, tr, cols), F32), pltpu.SemaphoreType.DMA((2,)),
                        pltpu.SemaphoreType.DMA((2,)), pltpu.SemaphoreType.DMA],
        compiler_params=_params("arbitrary"),
    )(xs)


def _sum8(xs, name):
    _, r, cols = xs.shape
    tr = 8
    for cand in (256, 128, 64, 32, 16):
        if r % cand == 0:
            tr = cand
            break

    def body(x_ref, o_ref):
        acc = x_ref[0].astype(F32)
        for k in range(1, N_DEV):
            acc = acc + x_ref[k].astype(F32)
        o_ref[...] = acc

    return pl.pallas_call(
        body, name=name, grid=(r // tr,),
        in_specs=[pl.BlockSpec((N_DEV, tr, cols), lambda i: (0, i, 0))],
        out_specs=pl.BlockSpec((tr, cols), lambda i: (i, 0)),
        out_shape=jax.ShapeDtypeStruct((r, cols), F32),
        compiler_params=_params("parallel"),
    )(xs)


def _adamw(w, g, m, v, name):
    R, C = w.shape
    tr = max([t for t in range(8, min(R, 512) + 1, 8) if R % t == 0] or [R])
    c1 = 1.0 / (1.0 - ADAM_B1 ** ADAM_STEP)
    c2 = 1.0 / (1.0 - ADAM_B2 ** ADAM_STEP)

    def body(w_ref, g_ref, m_ref, v_ref, d_ref, nm_ref, nv_ref):
        gv = g_ref[...]
        nm = ADAM_B1 * m_ref[...] + (1.0 - ADAM_B1) * gv
        nv = ADAM_B2 * v_ref[...] + (1.0 - ADAM_B2) * (gv * gv)
        d_ref[...] = (-ADAM_LR) * ((nm * c1) / (jnp.sqrt(nv * c2) + ADAM_EPS) + ADAM_WD * w_ref[...])
        nm_ref[...] = nm
        nv_ref[...] = nv

    spec = pl.BlockSpec((tr, C), lambda i: (i, 0))
    shape = jax.ShapeDtypeStruct((R, C), F32)
    return pl.pallas_call(
        body, name=name, grid=(R // tr,), in_specs=[spec] * 4, out_specs=[spec] * 3, out_shape=[shape] * 3,
        compiler_params=_params("parallel"),
    )(w, g, m, v)


def _flat_rows(parts, rows):
    flat = jnp.concatenate([q.reshape(-1) for q in parts])
    flat = jnp.pad(flat, (0, rows * LANES - flat.shape[0]))
    return flat.reshape(rows, LANES)


def _round_up(n, m):
    return (n + m - 1) // m * m


def _block_diag(w):
    H, n, _ = w.shape
    eye = jnp.eye(H, dtype=w.dtype)
    return (eye[:, None, :, None] * w[:, :, None, :]).reshape(H * n, H * n)


def _diag_blocks(w, H, n):
    w4 = w.reshape(H, n, H, n)
    return jnp.stack([w4[h, :, h, :] for h in range(H)])


W_IN_T = ("ffn1_w_in", "w_in", "ffn2_w_in")


def _ffn_in_weights(g_in):
    zeros = jnp.zeros((FF_HALF - FF_SHARD, D), g_in.dtype)
    wg_t = jnp.concatenate([g_in[0], zeros, g_in[1], zeros], axis=0)
    wu_t = jnp.concatenate([g_in[2], zeros, g_in[3], zeros], axis=0)
    return wg_t, wu_t


def _ffn_out_weights(g_out):
    zeros = jnp.zeros((FF_HALF - FF_SHARD, D), g_out.dtype)
    return jnp.concatenate([g_out[0], g_out[1], zeros, g_out[2], g_out[3], zeros], axis=0)


LAND_SHAPES = {"ffn1_w_in": (FF_SHARD, D), "ffn1_w_out": (FF_ROWS, D), "w_in": (D_IN // N_CHIP, D),
               "w_out": (D // N_CHIP, D), "ffn2_w_in": (FF_SHARD, D), "ffn2_w_out": (FF_ROWS, D)}


def _rows_window(arr, start, size):
    return arr, lambda r: r.at[pl.ds(start, size), :]


def _w_in_grad_windows(dwg_t, dwu_t):
    return [_rows_window(dwg_t if j < 2 else dwu_t, (j % 2) * FF_HALF, FF_SHARD) for j in range(N_CHIP)]


def _w_out_grad_windows(dwout):
    return [_rows_window(dwout, (j // 2) * FF_HALF + (j % 2) * FF_ROWS, FF_ROWS) for j in range(N_CHIP)]


def _mix_grad_windows(dwin_t, dwo):
    win = [_rows_window(dwin_t, j * (D_IN // N_CHIP), D_IN // N_CHIP) for j in range(N_CHIP)]
    wo = [_rows_window(dwo, j * (D // N_CHIP), D // N_CHIP) for j in range(N_CHIP)]
    return win, wo


def kernel(x, ffn1_norm, ffn1_w_in, ffn1_w_out, mix_norm, w_in, conv_w, conv_b, rg_w_a, rg_b_a, rg_w_x, rg_b_x, lru_lambda, pool_w, pool_scale, sgu_norm, sgu_w, sgu_b, w_out, ffn2_norm, ffn2_w_in, ffn2_w_out, final_norm, loss_target, m_ffn1_norm, m_ffn1_w_in, m_ffn1_w_out, m_mix_norm, m_w_in, m_conv_w, m_conv_b, m_rg_w_a, m_rg_b_a, m_rg_w_x, m_rg_b_x, m_lru_lambda, m_pool_w, m_pool_scale, m_sgu_norm, m_sgu_w, m_sgu_b, m_w_out, m_ffn2_norm, m_ffn2_w_in, m_ffn2_w_out, m_final_norm, v_ffn1_norm, v_ffn1_w_in, v_ffn1_w_out, v_mix_norm, v_w_in, v_conv_w, v_conv_b, v_rg_w_a, v_rg_b_a, v_rg_w_x, v_rg_b_x, v_lru_lambda, v_pool_w, v_pool_scale, v_sgu_norm, v_sgu_w, v_sgu_b, v_w_out, v_ffn2_norm, v_ffn2_w_in, v_ffn2_w_out, v_final_norm):
    args = locals()
    W = {n: args[n] for n in WEIGHTS}
    M = {n: args["m_" + n] for n in WEIGHTS}
    V = {n: args["v_" + n] for n in WEIGHTS}
    depth = ffn1_norm.shape[0]
    T = x.shape[1]
    xi, yi, ci = _place()
    chip = 2 * xi + yi

    assert depth == 2, "core c of a chip reduces layer c"
    groups = [(l, names) for l in range(depth)
              for names in (["ffn1_w_in"], ["ffn1_w_out", "w_in", "w_out"], ["ffn2_w_in", "ffn2_w_out"])]
    def stored(a, n):
        return jnp.swapaxes(a, 1, 2) if n in W_IN_T else a

    wb = {n: stored(W[n], n).astype(BF16) for n in BIG}
    groups[1][1].append("conv_w")
    conv_shard = conv_w.reshape(-1, conv_w.shape[-1])
    flights = {}

    def weights_start(k, dep=None):
        l, names = groups[k]
        shards = [conv_shard if n == "conv_w" else wb[n][l] for n in names]
        if dep is not None:
            shards[0] = _tie(shards[0], dep)
        lands = [lax.empty((N_CHIP,) + s.shape, s.dtype) for s in shards]
        send, recv, lands, token = _gather_start(shards, lands, "weights_start_%d" % k, k >= 2)
        flights[k] = (shards, lands, send, recv)
        return token

    def weights_wait(k, after):
        l, names = groups[k]
        shards, lands, send, recv = flights[k]
        got = _gather_wait(shards, lands, send, recv, after, "weights_wait_%d" % k, k >= 2)
        token = weights_start(k + 1, got[0]) if k + 1 < len(groups) else None
        if k < 2:
            got = _pair_share(got, "weights_share_%d" % k)
        return dict(zip(names, got)), token

    def after_start(a, token):
        return a if token is None else _tie(a, token)

    first_tokens = [weights_start(0)]

    layers = []
    for l in range(depth):
        L = {f: dict(norm=W[f + "_norm"][l][None]) for f in ("ffn1", "ffn2")}
        ws = jnp.where(jnp.tril(jnp.ones((CHUNK, CHUNK), bool))[None], sgu_w[l], 0.0)
        wax = jnp.concatenate([_block_diag(rg_w_a[l]), _block_diag(rg_w_x[l])], axis=1)
        wpool = _block_diag(pool_w[l])
        L["mix"] = dict(
            conv_b=conv_b[l][None], wax=wax.astype(BF16), wax_t=wax.T.astype(BF16),
            bax=jnp.concatenate([rg_b_a[l].reshape(-1), rg_b_x[l].reshape(-1)])[None], lam=lru_lambda[l][None],
            wpool=wpool.astype(BF16), wpool_t=wpool.T.astype(BF16), pool_scale=pool_scale[l][None],
            sgu_norm=sgu_norm[l][None], ws=ws.astype(BF16), ws_t=jnp.swapaxes(ws, 1, 2).astype(BF16),
            bz=jnp.repeat(sgu_b[l].T, 64, axis=1))
        L["mix_norm"] = mix_norm[l][None]
        layers.append(L)
    for token in first_tokens:
        layers[0]["ffn1"]["norm"] = _tie(layers[0]["ffn1"]["norm"], token)

    xs = x[0]
    saved = []
    for l, L in enumerate(layers):
        F1, F2 = L["ffn1"], L["ffn2"]
        got, token = weights_wait(3 * l, xs)
        F1["wg"], F1["wu"] = _ffn_in_weights(got["ffn1_w_in"])
        F1["norm"] = after_start(F1["norm"], token)
        h, da_dg, da_du, a = _ffn_in(xs, F1["norm"], F1["wg"], F1["wu"])
        got, token = weights_wait(3 * l + 1, a)
        F1["wout"] = _ffn_out_weights(got["ffn1_w_out"])
        L["w_in"] = got["w_in"].reshape(D_IN, D)
        L["w_out"] = got["w_out"].reshape(D, D)
        if "conv_w" in got:
            conv_full = jnp.concatenate([got["conv_w"][j] for j in range(N_CHIP)], axis=1)
            for ll in range(depth):
                layers[ll]["mix"]["conv_w"] = conv_full.reshape(depth, 4, D_RNN)[ll]
        a = after_start(a, token)
        x1 = _mm_res(a, F1["wout"], xs, 0.5, "ffn_out", tm=512, tn=D)
        s1 = (xs, h, da_dg, da_du, a)
        hm, p = _mix_in(x1, L["mix_norm"], L["w_in"])
        ycat, hs = _mix_fwd(p, L["mix"])
        x2 = _mm_res(ycat, L["w_out"], x1, 1.0, "mix_out", tm=512, tn=D)
        got, token = weights_wait(3 * l + 2, x2)
        F2["wg"], F2["wu"] = _ffn_in_weights(got["ffn2_w_in"])
        F2["wout"] = _ffn_out_weights(got["ffn2_w_out"])
        F2["norm"] = after_start(F2["norm"], token)
        s2 = (x2,) + tuple(_ffn_in(x2, F2["norm"], F2["wg"], F2["wu"]))
        x3 = _mm_res(s2[-1], F2["wout"], x2, 0.5, "ffn_out", tm=512, tn=D)
        saved.append((s1, (x1, hm, p, ycat, hs), s2))
        xs = x3

    dx, dxb, d_final, loss_part = _final(xs, loss_target[0], final_norm[None], 0.5)

    G = {n: [None] * depth for n in SMALL if n != "final_norm"}
    lands = {n: lax.empty((N_DEV,) + LAND_SHAPES[n], BF16) for n in BIG}
    in_flight = []

    def send_grads(l, windows, tag):
        names = list(windows)
        send, recv, thru, token = _exchange_start(l, [windows[n] for n in names], [lands[n] for n in names],
                                                  "grads_start_" + tag)
        lands.update(zip(names, thru))
        in_flight.append((l, names, [windows[n] for n in names], send, recv, "grads_wait_" + tag))
        return token

    def ffn_bwd(dx, dxb, F, s, f, l, pending, send_now):
        xin, h, da_dg, da_du, a = s
        dwout = _mm_tn(a, dxb, 1.0, "ffn_dwout")
        if send_now:
            token = send_grads(l, {f + "_w_out": _w_out_grad_windows(dwout)}, "l%d_%s_out" % (l, f))
            dxb = _tie(dxb, token)
        else:
            pending[f + "_w_out"] = _w_out_grad_windows(dwout)
        dg, du = _ffn_mid_bwd(dxb, F["wout"], da_dg, da_du)
        dwg = _mm_tn(dg, h, 1.0, "ffn_dwg")
        dwu = _mm_tn(du, h, 1.0, "ffn_dwu")
        pending[f + "_w_in"] = _w_in_grad_windows(dwg, dwu)
        if send_now:
            dg = _tie(dg, send_grads(l, pending, "l%d_%s" % (l, f)))
        dx, dxb, dn = _dh_rms_bwd([(dg, F["wg"]), (du, F["wu"])], xin, F["norm"], dx,
                                  1.0 if f == "ffn2" else 0.5, "ffn_dh")
        G[f + "_norm"][l] = dn[0]
        return dx, dxb

    for l in reversed(range(depth)):
        L = layers[l]
        s1, (x1, hm, p, ycat, hs), s2 = saved[l]
        pending = {}
        dx, dxb = ffn_bwd(dx, dxb, L["ffn2"], s2, "ffn2", l, pending, l == 0)
        if l == 0:
            pending = {}
        dycat = _mm_nt(dxb, L["w_out"], "mix_dy", tm=512, tn=D)
        dwo = _mm_tn(ycat, dxb, 1.0, "mix_dwout")
        mg = _mix_bwd(dycat, p, hs, L["mix"])
        dwin = _mm_tn(mg["dp"], hm, 1.0, "mix_dwin")
        pending["w_in"], pending["w_out"] = _mix_grad_windows(dwin, dwo)
        if l == 0:
            dp = _tie(mg["dp"], send_grads(l, pending, "l0_mix"))
            pending = {}
        else:
            dp = mg["dp"]
        dx, dxb, dn = _dh_rms_bwd([(dp, L["w_in"])], x1, L["mix_norm"], dx, 0.5, "mix_dh")
        G["mix_norm"][l] = dn[0]
        G["conv_w"][l], G["conv_b"][l] = mg["conv_w"], mg["conv_b"][0]
        G["rg_w_a"][l] = _diag_blocks(mg["wax"][:, :D_RNN], 8, 64)
        G["rg_w_x"][l] = _diag_blocks(mg["wax"][:, D_RNN:], 8, 64)
        G["rg_b_a"][l] = mg["bax"][0, :D_RNN].reshape(8, 64)
        G["rg_b_x"][l] = mg["bax"][0, D_RNN:].reshape(8, 64)
        G["lru_lambda"][l] = mg["lam"][0]
        G["pool_w"][l] = _diag_blocks(mg["wpool"], 4, 64)
        G["pool_scale"][l], G["sgu_norm"][l] = mg["pool_scale"][0], mg["sgu_norm"][0]
        G["sgu_w"][l] = mg["ws"]
        G["sgu_b"][l] = mg["bz"].reshape(CHUNK, 4, 64).sum(-1).T
        dx, dxb = ffn_bwd(dx, dxb, L["ffn1"], s1, "ffn1", l, pending, l == 0)
        if l > 0:
            dxb = _tie(dxb, send_grads(l, pending, "l%d" % l))
    grad_x = dx[None]
    G = {n: jnp.stack(v) for n, v in G.items()}
    G["final_norm"] = d_final[0]

    for l, names, windows, send, recv, tag in in_flight:
        lands.update(zip(names, _exchange_wait(l, windows, [lands[n] for n in names], send, recv, dx, tag)))
    both = [_sum_share(lands[n], "sum_share_" + n) for n in BIG]
    grads = dict(zip(BIG, both))

    small_sizes = [int(np.prod(G[n].shape)) for n in SMALL]
    srows = _round_up(sum(small_sizes) + 1, N_DEV * 8 * LANES) // (N_DEV * LANES)
    sflat = _flat_rows([G[n] for n in SMALL] + [loss_part[0, :1]], N_DEV * srows)
    sgot = _all_to_all(sflat.reshape(N_DEV, srows, LANES), "exchange_small_grads")
    sall = _all_gather8(_sum8(sgot, "sum_small_grads"), "share_small_grads").reshape(-1)
    off = 0
    for n, size in zip(SMALL, small_sizes):
        grads[n] = sall[off:off + size].reshape(G[n].shape)
        off += size
    loss = sall[off]
    grads["conv_w"] = lax.dynamic_slice_in_dim(grads["conv_w"], chip * conv_w.shape[2], conv_w.shape[2], axis=2)

    delta, new_m, new_v = {}, {}, {}
    for n in BIG:
        shp = grads[n].shape
        two_d = (shp[0] * shp[1], shp[2])
        outs = _adamw(stored(W[n], n).reshape(two_d), grads[n].reshape(two_d), stored(M[n], n).reshape(two_d),
                      stored(V[n], n).reshape(two_d), "adamw_" + n)
        delta[n], new_m[n], new_v[n] = (stored(o.reshape(shp), n) for o in outs)
        grads[n] = stored(grads[n], n)
    arows = _round_up(sum(int(np.prod(W[n].shape)) for n in SMALL), 8 * LANES) // LANES
    outs = _adamw(*(_flat_rows([src[n] for n in SMALL], arows) for src in (W, grads, M, V)), "adamw_small")
    outs = [o.reshape(-1) for o in outs]
    off = 0
    for n in SMALL:
        size = int(np.prod(W[n].shape))
        delta[n], new_m[n], new_v[n] = (o[off:off + size].reshape(W[n].shape) for o in outs)
        off += size

    return (loss, grad_x, *[grads[n] for n in WEIGHTS], *[delta[n] for n in WEIGHTS],
            *[new_m[n] for n in WEIGHTS], *[new_v[n] for n in WEIGHTS])
```

```python
import functools
import math

import jax
import jax.numpy as jnp
import numpy as np
from jax import lax
from jax.experimental import pallas as pl
from jax.experimental.pallas import tpu as pltpu

F32 = jnp.float32
BF16 = jnp.bfloat16
MESH = pl.DeviceIdType.MESH

D = 1024
D_RNN = 512
D_POOL = 256
D_SGU = 256
D_IN = 1792
D_FF = 2752
D_FFP = 2816
N_CHIP = 4
FF_SHARD = D_FF // 2
FF_HALF = D_FFP // 2
FF_ROWS = D_FF // N_CHIP
CHUNK = 128
HALO = 16
EPS = 1e-6
LRU_C = 8.0
N_DEV = 8
LANES = 1024
VMEM_LIMIT = 56 * 1024 * 1024

ADAM_LR, ADAM_B1, ADAM_B2, ADAM_EPS, ADAM_WD, ADAM_STEP = 0.001, 0.9, 0.999, 1e-08, 0.01, 10

BIG = ("ffn1_w_in", "ffn1_w_out", "w_in", "w_out", "ffn2_w_in", "ffn2_w_out")
SMALL = ("ffn1_norm", "mix_norm", "conv_w", "conv_b", "rg_w_a", "rg_b_a", "rg_w_x", "rg_b_x", "lru_lambda",
         "pool_w", "pool_scale", "sgu_norm", "sgu_w", "sgu_b", "ffn2_norm", "final_norm")
WEIGHTS = ("ffn1_norm", "ffn1_w_in", "ffn1_w_out", "mix_norm", "w_in", "conv_w", "conv_b", "rg_w_a", "rg_b_a",
           "rg_w_x", "rg_b_x", "lru_lambda", "pool_w", "pool_scale", "sgu_norm", "sgu_w", "sgu_b", "w_out",
           "ffn2_norm", "ffn2_w_in", "ffn2_w_out", "final_norm")


def _params(*sem):
    return pltpu.CompilerParams(dimension_semantics=sem, vmem_limit_bytes=VMEM_LIMIT)


def _gelu(x):
    c = math.sqrt(2.0 / math.pi)
    t = jnp.tanh(c * (x + 0.044715 * (x * x * x)))
    return 0.5 * x * (1.0 + t)


def _gelu_and_grad(x):
    c = math.sqrt(2.0 / math.pi)
    x2 = x * x
    t = jnp.tanh(c * (x + 0.044715 * (x2 * x)))
    g = 0.5 * x * (1.0 + t)
    dg = 0.5 * (1.0 + t) + 0.5 * x * (1.0 - t * t) * (c * (1.0 + 3.0 * 0.044715 * x2))
    return g, dg


def _sigmoid(x):
    return 0.5 * jnp.tanh(0.5 * x) + 0.5


def _dot(a, b):
    return jnp.dot(a, b, preferred_element_type=F32)


def _dot_tn(a, b):
    return lax.dot_general(a, b, (((0,), (0,)), ((), ())), preferred_element_type=F32)


def _dot_nt(a, b):
    return lax.dot_general(a, b, (((1,), (1,)), ((), ())), preferred_element_type=F32)


def _tile(n, limit):
    best = 128
    for t in range(128, min(n, limit) + 1, 128):
        if n % t == 0:
            best = t
    assert n % best == 0, (n, limit)
    return best


def _mm_res(a, b, res, scale, name, tm=1024, tn=512):
    M, K = a.shape
    N = b.shape[1]
    tm, tn = min(tm, M), _tile(N, tn)

    def body(a_ref, b_ref, r_ref, o_ref):
        o_ref[...] = r_ref[...] + scale * _dot(a_ref[...], b_ref[...])

    return pl.pallas_call(
        body, name=name, grid=(M // tm, N // tn),
        in_specs=[pl.BlockSpec((tm, K), lambda i, j: (i, 0)), pl.BlockSpec((K, tn), lambda i, j: (0, j)),
                  pl.BlockSpec((tm, tn), lambda i, j: (i, j))],
        out_specs=pl.BlockSpec((tm, tn), lambda i, j: (i, j)),
        out_shape=jax.ShapeDtypeStruct((M, N), F32),
        compiler_params=_params("parallel", "parallel"),
    )(a, b, res)


def _mm_nt(a, b, name, tm=1024, tn=512):
    M, K = a.shape
    N = b.shape[0]
    tm, tn = min(tm, M), _tile(N, tn)

    def body(a_ref, b_ref, o_ref):
        o_ref[...] = _dot_nt(a_ref[...], b_ref[...])

    return pl.pallas_call(
        body, name=name, grid=(M // tm, N // tn),
        in_specs=[pl.BlockSpec((tm, K), lambda i, j: (i, 0)), pl.BlockSpec((tn, K), lambda i, j: (j, 0))],
        out_specs=pl.BlockSpec((tm, tn), lambda i, j: (i, j)),
        out_shape=jax.ShapeDtypeStruct((M, N), F32),
        compiler_params=_params("parallel", "parallel"),
    )(a, b)


def _dh_rms_bwd(pairs, x, g, dres, copy_scale, name, tm=512):
    T = x.shape[0]
    tm = min(tm, T)
    n = len(pairs)

    def body(*refs):
        ab = refs[:2 * n]
        x_ref, g_ref, dres_ref, dx_ref, dxb_ref, dg_ref = refs[2 * n:]
        dy = _dot(ab[0][...], ab[1][...])
        for k in range(1, n):
            dy = dy + _dot(ab[2 * k][...], ab[2 * k + 1][...])
        xv = x_ref[...]
        r = lax.rsqrt(jnp.mean(xv * xv, axis=-1, keepdims=True) + EPS)
        xhat = xv * r
        dxhat = dy * g_ref[...]
        dx = dres_ref[...] + r * (dxhat - xhat * jnp.mean(dxhat * xhat, axis=-1, keepdims=True))
        dx_ref[...] = dx
        dxb_ref[...] = (copy_scale * dx).astype(BF16)

        @pl.when(pl.program_id(0) == 0)
        def _():
            dg_ref[...] = jnp.zeros_like(dg_ref)

        dg_ref[...] += jnp.sum(dy * xhat, axis=0, keepdims=True)

    row = pl.BlockSpec((tm, D), lambda i: (i, 0))
    vec = pl.BlockSpec((1, D), lambda i: (0, 0))
    in_specs, operands = [], []
    for a, b in pairs:
        in_specs += [pl.BlockSpec((tm, a.shape[1]), lambda i: (i, 0)),
                     pl.BlockSpec(b.shape, lambda i: (0, 0), pipeline_mode=pl.Buffered(1))]
        operands += [a, b]
    return pl.pallas_call(
        body, name=name, grid=(T // tm,),
        in_specs=in_specs + [row, vec, row], out_specs=[row, row, vec],
        out_shape=[jax.ShapeDtypeStruct((T, D), F32), jax.ShapeDtypeStruct((T, D), BF16),
                   jax.ShapeDtypeStruct((1, D), F32)],
        compiler_params=_params("arbitrary"),
    )(*operands, x, g, dres)


def _mm_tn(a, b, scale, name, tm=1792, tn=1792, tk=2048):
    T, M = a.shape
    N = b.shape[1]
    tm, tn, tk = _tile(M, tm), _tile(N, tn), min(tk, T)
    nk = T // tk

    def body(a_ref, b_ref, o_ref, acc_ref):
        k = pl.program_id(2)

        @pl.when(k == 0)
        def _():
            acc_ref[...] = jnp.zeros_like(acc_ref)

        acc_ref[...] += _dot_tn(a_ref[...], b_ref[...])

        @pl.when(k == nk - 1)
        def _():
            o_ref[...] = (scale * acc_ref[...]).astype(BF16)

    return pl.pallas_call(
        body, name=name, grid=(M // tm, N // tn, nk),
        in_specs=[pl.BlockSpec((tk, tm), lambda i, j, k: (k, i)), pl.BlockSpec((tk, tn), lambda i, j, k: (k, j))],
        out_specs=pl.BlockSpec((tm, tn), lambda i, j, k: (i, j)),
        out_shape=jax.ShapeDtypeStruct((M, N), BF16),
        scratch_shapes=[pltpu.VMEM((tm, tn), F32)],
        compiler_params=_params("parallel", "parallel", "arbitrary"),
    )(a, b)


def _rms_rows(x_ref, gain_ref):
    xv = x_ref[...]
    r = lax.rsqrt(jnp.mean(xv * xv, axis=-1, keepdims=True) + EPS)
    return (xv * r * gain_ref[...]).astype(BF16)


def _ffn_in(x, gain, wg_t, wu_t, tm=512, tn=FF_HALF):
    T = x.shape[0]
    tm = min(tm, T)

    def body(x_ref, gain_ref, wg_ref, wu_ref, h_ref, dg_ref, du_ref, a_ref):
        hv = _rms_rows(x_ref, gain_ref)

        @pl.when(pl.program_id(0) == 0)
        def _():
            h_ref[...] = hv

        g = _dot_nt(hv, wg_ref[...])
        u = _dot_nt(hv, wu_ref[...])
        s = _sigmoid(g)
        gs = g * s
        dg_ref[...] = (u * (s + gs - gs * s)).astype(BF16)
        du_ref[...] = gs.astype(BF16)
        a_ref[...] = (gs * u).astype(BF16)

    row = pl.BlockSpec((tm, D), lambda j, i: (i, 0))
    last = T // tm - 1
    h_spec = pl.BlockSpec((tm, D), lambda j, i: (jnp.where(j == 0, i, last), 0))
    wspec = pl.BlockSpec((tn, D), lambda j, i: (j, 0))
    ospec = pl.BlockSpec((tm, tn), lambda j, i: (i, j))
    oshape = jax.ShapeDtypeStruct((T, D_FFP), BF16)
    return pl.pallas_call(
        body, name="ffn_in", grid=(D_FFP // tn, T // tm),
        in_specs=[row, pl.BlockSpec((1, D), lambda j, i: (0, 0)), wspec, wspec],
        out_specs=[h_spec, ospec, ospec, ospec],
        out_shape=[jax.ShapeDtypeStruct((T, D), BF16), oshape, oshape, oshape],
        compiler_params=_params("arbitrary", "arbitrary"),
    )(x, gain, wg_t, wu_t)


def _mix_in(x, gain, w_in_t, tm=512):
    T = x.shape[0]
    tm = min(tm, T)

    def body(x_ref, gain_ref, w_ref, h_ref, p_ref):
        hv = _rms_rows(x_ref, gain_ref)
        h_ref[...] = hv
        p_ref[...] = _dot_nt(hv, w_ref[...])

    row = pl.BlockSpec((tm, D), lambda i: (i, 0))
    return pl.pallas_call(
        body, name="mix_in", grid=(T // tm,),
        in_specs=[row, pl.BlockSpec((1, D), lambda i: (0, 0)), pl.BlockSpec((D_IN, D), lambda i: (0, 0))],
        out_specs=[row, pl.BlockSpec((tm, D_IN), lambda i: (i, 0))],
        out_shape=[jax.ShapeDtypeStruct((T, D), BF16), jax.ShapeDtypeStruct((T, D_IN), F32)],
        compiler_params=_params("parallel"),
    )(x, gain, w_in_t)


def _ffn_mid_bwd(dyh, wout, da_dg, da_du, tm=512, tn=FF_HALF):
    T = dyh.shape[0]
    tm = min(tm, T)

    def body(dy_ref, w_ref, pg_ref, pu_ref, dg_ref, du_ref):
        da = _dot_nt(dy_ref[...], w_ref[...])
        dg_ref[...] = (da * pg_ref[...].astype(F32)).astype(BF16)
        du_ref[...] = (da * pu_ref[...].astype(F32)).astype(BF16)

    ospec = pl.BlockSpec((tm, tn), lambda j, i: (i, j))
    oshape = jax.ShapeDtypeStruct((T, D_FFP), BF16)
    return pl.pallas_call(
        body, name="ffn_mid_bwd", grid=(D_FFP // tn, T // tm),
        in_specs=[pl.BlockSpec((tm, D), lambda j, i: (i, 0)), pl.BlockSpec((tn, D), lambda j, i: (j, 0)),
                  ospec, ospec],
        out_specs=[ospec, ospec], out_shape=[oshape, oshape],
        compiler_params=_params("parallel", "parallel"),
    )(dyh, wout, da_dg, da_du)


def _final(x, tgt, gf, copy_scale, tm=512):
    T = x.shape[0]
    tm = min(tm, T)

    def body(x_ref, t_ref, g_ref, dx_ref, dxb_ref, dg_ref, loss_ref):
        xv = x_ref[...]
        r = lax.rsqrt(jnp.mean(xv * xv, axis=-1, keepdims=True) + EPS)
        xhat = xv * r
        err = xhat * g_ref[...] - t_ref[...]
        dy = err * (1.0 / D)
        dxhat = dy * g_ref[...]
        dx = r * (dxhat - xhat * jnp.mean(dxhat * xhat, axis=-1, keepdims=True))
        dx_ref[...] = dx
        dxb_ref[...] = (copy_scale * dx).astype(BF16)

        @pl.when(pl.program_id(0) == 0)
        def _():
            dg_ref[...] = jnp.zeros_like(dg_ref)
            loss_ref[...] = jnp.zeros_like(loss_ref)

        dg_ref[...] += jnp.sum(dy * xhat, axis=0, keepdims=True)
        loss_ref[...] += (0.5 / D) * jnp.sum(err * err)

    row = pl.BlockSpec((tm, D), lambda i: (i, 0))
    vec = pl.BlockSpec((1, D), lambda i: (0, 0))
    return pl.pallas_call(
        body, name="final_loss", grid=(T // tm,),
        in_specs=[row, row, vec],
        out_specs=[row, row, vec, pl.BlockSpec((1, 128), lambda i: (0, 0))],
        out_shape=[jax.ShapeDtypeStruct((T, D), F32), jax.ShapeDtypeStruct((T, D), BF16),
                   jax.ShapeDtypeStruct((1, D), F32), jax.ShapeDtypeStruct((1, 128), F32)],
        compiler_params=_params("arbitrary"),
    )(x, tgt, gf)


def _mix_block(T, limit):
    return min(limit, T // 2)


def _rows(tb, width):
    return lax.broadcasted_iota(jnp.int32, (tb, width), 0)


def _rglru_gates(xc, wax_ref, bax_ref, lam_ref):
    pre = _dot(xc.astype(BF16), wax_ref[...]) + bax_ref[...]
    r = _sigmoid(pre[:, :D_RNN])
    ig = _sigmoid(pre[:, D_RNN:])
    z = -lam_ref[...]
    sp = jnp.maximum(z, 0.0) + jnp.log(1.0 + jnp.exp(-jnp.abs(z)))
    log_a = (-LRU_C) * r * sp
    a = jnp.exp(log_a)
    mult = jnp.sqrt(-jnp.tanh(log_a) * (1.0 + a * a))
    return r, ig, sp, a, mult


def _conv(xa_ext, cw_ref, cb_ref):
    y = cb_ref[...] + cw_ref[3:4, :] * xa_ext
    for k in range(1, 4):
        y = y + cw_ref[3 - k:4 - k, :] * pltpu.roll(xa_ext, k, 0)
    return y[HALO:]


def _pool_window_lanes():
    lane = lax.broadcasted_iota(jnp.int32, (1, D_POOL), 1)
    return jnp.where(lane < 64, 2, jnp.where(lane < 128, 4, jnp.where(lane < 192, 8, 16)))


def _pool_select(s2, s4, s8, s16):
    lane = lax.broadcasted_iota(jnp.int32, s2.shape, 1)
    return jnp.where(lane < 64, s2, jnp.where(lane < 128, s4, jnp.where(lane < 192, s8, s16)))


def _pool_diff(xp_ext, t0, tb):
    s2 = xp_ext + pltpu.roll(xp_ext, 1, 0)
    s4 = s2 + pltpu.roll(s2, 2, 0)
    s8 = s4 + pltpu.roll(s4, 4, 0)
    s16 = s8 + pltpu.roll(s8, 8, 0)
    sel = _pool_select(s2, s4, s8, s16)[HALO:]
    cnt = jnp.minimum(t0 + _rows(tb, D_POOL) + 1, _pool_window_lanes()).astype(F32)
    return sel / cnt - xp_ext[HALO:], cnt


def _head_masks():
    lane = lax.broadcasted_iota(jnp.int32, (1, D_SGU), 1)
    return [((lane >= 64 * h) & (lane < 64 * (h + 1))).astype(F32) for h in range(4)]


def _sgu_mix(w_ref, vch, masks):
    z = masks[0] * _dot(w_ref[0], vch)
    for h in range(1, 4):
        z = z + masks[h] * _dot(w_ref[h], vch)
    return z


def _mix_fwd(p, prm):
    T = p.shape[0]
    tb = _mix_block(T, 512)
    nb = T // tb

    def body(p_ref, xah_ref, xph_ref, cw_ref, cb_ref, wax_ref, bax_ref, lam_ref, wp_ref, ps_ref, sgn_ref,
             ws_ref, bz_ref, y_ref, hs_ref, carry_ref):
        i = pl.program_id(0)
        keep = (i > 0).astype(F32)

        @pl.when(i == 0)
        def _():
            carry_ref[...] = jnp.zeros_like(carry_ref)

        xa_ext = jnp.concatenate([xah_ref[...] * keep, p_ref[:, 512:1024]], axis=0)
        xc = _conv(xa_ext, cw_ref, cb_ref)
        r, ig, sp, a, mult = _rglru_gates(xc, wax_ref, bax_ref, lam_ref)
        bv = mult * (ig * xc)
        row = _rows(tb, D_RNN)
        s = 1
        while s < tb:
            m = row >= s
            bv = jnp.where(m, a * pltpu.roll(bv, s, 0) + bv, bv)
            a = jnp.where(m, a * pltpu.roll(a, s, 0), a)
            s *= 2
        h = bv + a * carry_ref[0:1, :]
        hs_ref[...] = h
        last = jnp.sum(jnp.where(_rows(8, D_RNN) == 7, hs_ref[tb - 8:tb, :], 0.0), axis=0, keepdims=True)
        carry_ref[...] = jnp.broadcast_to(last, carry_ref.shape)
        y_ref[:, 0:512] = (_gelu(p_ref[:, 0:512]) * h).astype(BF16)

        xp_ext = jnp.concatenate([xph_ref[...] * keep, p_ref[:, 1024:1280]], axis=0)
        d, _ = _pool_diff(xp_ext, i * tb, tb)
        y_ref[:, 512:768] = (_dot(d.astype(BF16), wp_ref[...]) * ps_ref[...]).astype(BF16)

        ug = _gelu(p_ref[:, 1280:1536])
        vg = _gelu(p_ref[:, 1536:1792])
        rv = lax.rsqrt(jnp.mean(vg * vg, axis=-1, keepdims=True) + EPS)
        vn = (vg * rv * sgn_ref[...]).astype(BF16)
        masks = _head_masks()
        for ci in range(tb // CHUNK):
            sl = slice(ci * CHUNK, (ci + 1) * CHUNK)
            z = _sgu_mix(ws_ref, vn[sl], masks) + bz_ref[...]
            y_ref[sl, 768:1024] = (ug[sl] * z).astype(BF16)

    hb = tb // HALO

    def halo(i):
        return jnp.maximum(i * hb - 1, 0)

    def full(shape):
        return pl.BlockSpec(shape, lambda i: (0,) * len(shape))

    return pl.pallas_call(
        body, name="mix_fwd", grid=(nb,),
        in_specs=[pl.BlockSpec((tb, D_IN), lambda i: (i, 0)),
                  pl.BlockSpec((HALO, D_RNN), lambda i: (halo(i), 1)),
                  pl.BlockSpec((HALO, D_POOL), lambda i: (halo(i), 4)),
                  full((4, D_RNN)), full((1, D_RNN)), full((D_RNN, 2 * D_RNN)), full((1, 2 * D_RNN)),
                  full((1, D_RNN)), full((D_POOL, D_POOL)), full((1, D_POOL)), full((1, D_SGU)),
                  full((4, CHUNK, CHUNK)), full((CHUNK, D_SGU))],
        out_specs=[pl.BlockSpec((tb, D), lambda i: (i, 0)), pl.BlockSpec((tb, D_RNN), lambda i: (i, 0))],
        out_shape=[jax.ShapeDtypeStruct((T, D), BF16), jax.ShapeDtypeStruct((T, D_RNN), F32)],
        scratch_shapes=[pltpu.VMEM((8, D_RNN), F32)],
        compiler_params=_params("arbitrary"),
    )(p, p, p, prm["conv_w"], prm["conv_b"], prm["wax"], prm["bax"], prm["lam"], prm["wpool"], prm["pool_scale"],
      prm["sgu_norm"], prm["ws"], prm["bz"])


def _mix_bwd(dy, p, hs, prm):
    T = p.shape[0]
    tb = _mix_block(T, 256)
    nb = T // tb
    hb = tb // HALO

    def body(dy_ref, p_ref, xah_ref, xph_ref, hs_ref, hsh_ref, cw_ref, cb_ref, wax_ref, waxt_ref, bax_ref,
             lam_ref, wp_ref, wpt_ref, ps_ref, sgn_ref, ws_ref, wst_ref, bz_ref,
             dp_ref, dcw_ref, dcb_ref, dwax_ref, dbax_ref, dlam_ref, dwp_ref, dps_ref, dsgn_ref, dws_ref,
             dbz_ref, gcarry_ref, xcfut_ref, mfut_ref):
        i = pl.program_id(0)
        bi = nb - 1 - i
        keep = (bi > 0).astype(F32)

        @pl.when(i == 0)
        def _():
            for ref in (dcw_ref, dcb_ref, dwax_ref, dbax_ref, dlam_ref, dwp_ref, dps_ref, dsgn_ref, dws_ref,
                        dbz_ref, gcarry_ref, xcfut_ref, mfut_ref):
                ref[...] = jnp.zeros_like(ref)

        xa_ext = jnp.concatenate([xah_ref[...] * keep, p_ref[:, 512:1024]], axis=0)
        xc = _conv(xa_ext, cw_ref, cb_ref)
        r, ig, sp, a, mult = _rglru_gates(xc, wax_ref, bax_ref, lam_ref)
        gg, dgg = _gelu_and_grad(p_ref[:, 0:512])
        dya = dy_ref[:, 0:512]
        dp_ref[:, 0:512] = (dya * hs_ref[...] * dgg).astype(BF16)
        row = _rows(tb, D_RNN)
        g = dya * gg + jnp.where(row == tb - 1, gcarry_ref[0:1, :], 0.0)
        al = pltpu.roll(a, tb - 1, 0)
        s = 1
        while s < tb:
            m = row < tb - s
            g = jnp.where(m, al * pltpu.roll(g, tb - s, 0) + g, g)
            al = jnp.where(m, al * pltpu.roll(al, tb - s, 0), al)
            s *= 2
        first = jnp.sum(jnp.where(_rows(8, D_RNN) == 0, (a * g)[0:8], 0.0), axis=0, keepdims=True)
        gcarry_ref[...] = jnp.broadcast_to(first, gcarry_ref.shape)
        hs_ext = jnp.concatenate([hsh_ref[...] * keep, hs_ref[...]], axis=0)
        h_prev = pltpu.roll(hs_ext, 1, 0)[HALO:]
        ix = ig * xc
        dlog_a = g * h_prev * a - (g * ix) * (a * a / mult)
        dlam_ref[...] += jnp.sum(dlog_a * r, axis=0, keepdims=True) * (LRU_C * _sigmoid(-lam_ref[...]))
        dpre_r = dlog_a * ((-LRU_C) * sp) * (r * (1.0 - r))
        dpre_i = (g * mult * xc) * (ig * (1.0 - ig))
        dpre = jnp.concatenate([dpre_r, dpre_i], axis=1)
        dbax_ref[...] += jnp.sum(dpre, axis=0, keepdims=True)
        dpre_b = dpre.astype(BF16)
        dwax_ref[...] += _dot_tn(xc.astype(BF16), dpre_b)
        dxc = g * mult * ig + _dot(dpre_b, waxt_ref[...])
        dcb_ref[...] += jnp.sum(dxc, axis=0, keepdims=True)
        for k in range(4):
            xs = xa_ext[HALO:] if k == 3 else pltpu.roll(xa_ext, 3 - k, 0)[HALO:]
            dcw_ref[k:k + 1, :] += jnp.sum(dxc * xs, axis=0, keepdims=True)
        dxc_ext = jnp.concatenate([dxc, xcfut_ref[...]], axis=0)
        n = tb + HALO
        dxa = cw_ref[3:4, :] * dxc_ext
        for k in range(1, 4):
            dxa = dxa + cw_ref[3 - k:4 - k, :] * pltpu.roll(dxc_ext, n - k, 0)
        dp_ref[:, 512:1024] = dxa[:tb].astype(BF16)
        xcfut_ref[...] = dxc[0:HALO]

        xp_ext = jnp.concatenate([xph_ref[...] * keep, p_ref[:, 1024:1280]], axis=0)
        d, cnt = _pool_diff(xp_ext, bi * tb, tb)
        db = d.astype(BF16)
        dyb = dy_ref[:, 512:768]
        dps_ref[...] += jnp.sum(dyb * _dot(db, wp_ref[...]), axis=0, keepdims=True)
        dq = (dyb * ps_ref[...]).astype(BF16)
        dwp_ref[...] += _dot_tn(db, dq)
        dd = _dot(dq, wpt_ref[...])
        mm = dd / cnt
        m_ext = jnp.concatenate([mm, mfut_ref[...]], axis=0)
        f2 = m_ext + pltpu.roll(m_ext, n - 1, 0)
        f4 = f2 + pltpu.roll(f2, n - 2, 0)
        f8 = f4 + pltpu.roll(f4, n - 4, 0)
        f16 = f8 + pltpu.roll(f8, n - 8, 0)
        dp_ref[:, 1024:1280] = (_pool_select(f2, f4, f8, f16)[:tb] - dd).astype(BF16)
        mfut_ref[...] = mm[0:HALO]

        ug, dug = _gelu_and_grad(p_ref[:, 1280:1536])
        vg, dvg = _gelu_and_grad(p_ref[:, 1536:1792])
        rv = lax.rsqrt(jnp.mean(vg * vg, axis=-1, keepdims=True) + EPS)
        vhat = vg * rv
        vn = (vhat * sgn_ref[...]).astype(BF16)
        dyc = dy_ref[:, 768:1024]
        masks = _head_masks()
        dz = dyc * ug
        dzb = dz.astype(BF16)
        dvn_parts = []
        for ci in range(tb // CHUNK):
            sl = slice(ci * CHUNK, (ci + 1) * CHUNK)
            z = _sgu_mix(ws_ref, vn[sl], masks) + bz_ref[...]
            dp_ref[sl, 1280:1536] = (dyc[sl] * z * dug[sl]).astype(BF16)
            dbz_ref[...] += dz[sl]
            for h in range(4):
                dws_ref[h] += _dot_nt((dz[sl] * masks[h]).astype(BF16), vn[sl])
            dvn_parts.append(_sgu_mix(wst_ref, dzb[sl], masks))
        dvn = jnp.concatenate(dvn_parts, axis=0)
        dsgn_ref[...] += jnp.sum(dvn * vhat, axis=0, keepdims=True)
        dvhat = dvn * sgn_ref[...]
        dvg_in = rv * (dvhat - vhat * jnp.mean(dvhat * vhat, axis=-1, keepdims=True))
        dp_ref[:, 1536:1792] = (dvg_in * dvg).astype(BF16)

        @pl.when(i == nb - 1)
        def _():
            tril = (lax.broadcasted_iota(jnp.int32, (CHUNK, CHUNK), 0)
                    >= lax.broadcasted_iota(jnp.int32, (CHUNK, CHUNK), 1)).astype(F32)
            for h in range(4):
                dws_ref[h] = dws_ref[h] * tril

    def blk(i):
        return nb - 1 - i

    def halo(i):
        return jnp.maximum(blk(i) * hb - 1, 0)

    def full(shape):
        return pl.BlockSpec(shape, lambda i: (0,) * len(shape))

    small_shapes = [(4, D_RNN), (1, D_RNN), (D_RNN, 2 * D_RNN), (1, 2 * D_RNN), (1, D_RNN), (D_POOL, D_POOL),
                    (1, D_POOL), (1, D_SGU), (4, CHUNK, CHUNK), (CHUNK, D_SGU)]
    outs = pl.pallas_call(
        body, name="mix_bwd", grid=(nb,),
        in_specs=[pl.BlockSpec((tb, D), lambda i: (blk(i), 0)),
                  pl.BlockSpec((tb, D_IN), lambda i: (blk(i), 0)),
                  pl.BlockSpec((HALO, D_RNN), lambda i: (halo(i), 1)),
                  pl.BlockSpec((HALO, D_POOL), lambda i: (halo(i), 4)),
                  pl.BlockSpec((tb, D_RNN), lambda i: (blk(i), 0)),
                  pl.BlockSpec((HALO, D_RNN), lambda i: (halo(i), 0)),
                  full((4, D_RNN)), full((1, D_RNN)), full((D_RNN, 2 * D_RNN)), full((2 * D_RNN, D_RNN)),
                  full((1, 2 * D_RNN)), full((1, D_RNN)), full((D_POOL, D_POOL)), full((D_POOL, D_POOL)),
                  full((1, D_POOL)), full((1, D_SGU)), full((4, CHUNK, CHUNK)), full((4, CHUNK, CHUNK)),
                  full((CHUNK, D_SGU))],
        out_specs=[pl.BlockSpec((tb, D_IN), lambda i: (blk(i), 0))] + [full(s) for s in small_shapes],
        out_shape=[jax.ShapeDtypeStruct((T, D_IN), BF16)] + [jax.ShapeDtypeStruct(s, F32) for s in small_shapes],
        scratch_shapes=[pltpu.VMEM((8, D_RNN), F32), pltpu.VMEM((HALO, D_RNN), F32),
                        pltpu.VMEM((HALO, D_POOL), F32)],
        compiler_params=_params("arbitrary"),
    )(dy, p, p, p, hs, hs, prm["conv_w"], prm["conv_b"], prm["wax"], prm["wax_t"], prm["bax"], prm["lam"],
      prm["wpool"], prm["wpool_t"], prm["pool_scale"], prm["sgu_norm"], prm["ws"], prm["ws_t"], prm["bz"])
    names = ("dp", "conv_w", "conv_b", "wax", "bax", "lam", "wpool", "pool_scale", "sgu_norm", "ws", "bz")
    return dict(zip(names, outs))


ANY = pl.BlockSpec(memory_space=pl.ANY)


def _place():
    x, y, c = lax.axis_index("x"), lax.axis_index("y"), lax.axis_index("c")
    return x, y, c


def _all_to_all(xs, name):
    def body(in_ref, out_ref, send_sems, recv_sems, local_sem):
        x, y, c = _place()
        me = 4 * x + 2 * y + c
        mine = pltpu.make_async_copy(in_ref.at[me], out_ref.at[me], local_sem)
        mine.start()
        copies = []
        for rel in range(1, N_DEV):
            tx = 1 - x if rel & 4 else x
            ty = 1 - y if rel & 2 else y
            tc = 1 - c if rel & 1 else c
            cp = pltpu.make_async_remote_copy(
                src_ref=in_ref.at[4 * tx + 2 * ty + tc], dst_ref=out_ref.at[me],
                send_sem=send_sems.at[rel - 1], recv_sem=recv_sems.at[rel - 1],
                device_id=(tx, ty, tc), device_id_type=MESH)
            cp.start()
            copies.append(cp)
        for cp in copies:
            cp.wait()
        mine.wait()

    return pl.pallas_call(
        body, name=name, in_specs=[ANY], out_specs=ANY,
        out_shape=jax.ShapeDtypeStruct(xs.shape, xs.dtype),
        scratch_shapes=[pltpu.SemaphoreType.DMA((N_DEV - 1,)), pltpu.SemaphoreType.DMA((N_DEV - 1,)),
                        pltpu.SemaphoreType.DMA],
    )(xs)


def _all_gather8(xs, name):
    def body(x_ref, out_ref, send_sems, recv_sems, local_sem):
        x, y, c = _place()
        me, sibling = (x, y, c), (x, y, 1 - c)
        chips = [(1 - x, y), (x, 1 - y), (1 - x, 1 - y)]

        def rows(px, py, pc):
            return out_ref.at[4 * px + 2 * py + pc]

        def copy(k, block, to, src=None):
            return pltpu.make_async_remote_copy(
                src_ref=rows(*block) if src is None else src, dst_ref=rows(*block),
                send_sem=send_sems.at[k], recv_sem=recv_sems.at[k], device_id=to, device_id_type=MESH)

        mine = pltpu.make_async_copy(x_ref, rows(*me), local_sem)
        mine.start()
        first = [copy(0, me, sibling, src=x_ref)]
        first += [copy(1 + j, me, (*chip, c), src=x_ref) for j, chip in enumerate(chips)]
        for cp in first:
            cp.start()
        passed = [copy(4 + j, (*chip, c), sibling) for j, chip in enumerate(chips)]
        for j, chip in enumerate(chips):
            copy(1 + j, (*chip, c), me).wait_recv()
            passed[j].start()
        copy(0, sibling, me).wait_recv()
        for j, chip in enumerate(chips):
            copy(4 + j, (*chip, 1 - c), me).wait_recv()
        for cp in first + passed:
            cp.wait_send()
        mine.wait()

    return pl.pallas_call(
        body, name=name, in_specs=[ANY], out_specs=ANY,
        out_shape=jax.ShapeDtypeStruct((N_DEV,) + xs.shape, xs.dtype),
        scratch_shapes=[pltpu.SemaphoreType.DMA((7,)), pltpu.SemaphoreType.DMA((7,)), pltpu.SemaphoreType.DMA],
    )(xs)


HBM = pl.BlockSpec(memory_space=pltpu.HBM)
SEM = pl.BlockSpec(memory_space=pltpu.SEMAPHORE)
EFFECT = pltpu.SideEffectType.DATAFLOW_SIDE_EFFECTING


def _in_hbm(a):
    return pltpu.with_memory_space_constraint(a, pltpu.HBM)


def _local_copy(src, dst, stage, sem):
    load = pltpu.make_async_copy(src, stage, sem)
    load.start()
    load.wait()
    store = pltpu.make_async_copy(stage, dst, sem)
    store.start()
    store.wait()


def _unique(windows):
    arrays = []
    for per_chip in windows:
        for arr, _ in per_chip:
            if not any(arr is a for a in arrays):
                arrays.append(arr)
    return arrays


def _exchange_start(layer, windows, lands, name):
    arrays = _unique(windows)
    na, nt = len(arrays), len(windows)

    def body(*refs):
        in_refs, land_refs = refs[:na], refs[na:na + nt]
        send_sems, recv_sems = refs[na + nt], refs[na + nt + 1]
        token = refs[-1]
        x, y, c = _place()
        me = 4 * x + 2 * y + c
        for t in range(nt):
            for j in range(N_CHIP):
                arr, window = windows[t][j]
                src = window(in_refs[next(i for i, a in enumerate(arrays) if a is arr)])

                @pl.when(me != 2 * j + layer)
                def _():
                    pltpu.make_async_remote_copy(
                        src_ref=src, dst_ref=land_refs[t].at[me], send_sem=send_sems.at[N_CHIP * t + j],
                        recv_sem=recv_sems.at[N_DEV * t + me], device_id=(j // 2, j % 2, layer),
                        device_id_type=MESH).start()
        token[...] = jnp.zeros_like(token)

    outs = pl.pallas_call(
        body, name=name,
        out_shape=(pltpu.SemaphoreType.DMA((N_CHIP * nt,)), pltpu.SemaphoreType.DMA((N_DEV * nt,)),
                   *[pltpu.HBM(a.shape, a.dtype) for a in lands], jax.ShapeDtypeStruct((8, 128), F32)),
        in_specs=[HBM] * (na + nt),
        out_specs=(SEM, SEM, *[HBM] * nt, pl.BlockSpec(memory_space=pltpu.VMEM)),
        input_output_aliases={na + t: 2 + t for t in range(nt)},
        compiler_params=pltpu.CompilerParams(has_side_effects=EFFECT, vmem_limit_bytes=VMEM_LIMIT),
    )(*[_in_hbm(a) for a in arrays], *[_in_hbm(a) for a in lands])
    return outs[0], outs[1], list(outs[2:2 + nt]), outs[-1]


def _exchange_wait(layer, windows, lands, send_sems, recv_sems, after, name):
    arrays = _unique(windows)
    na, nt = len(arrays), len(windows)

    def body(*refs):
        in_refs, land_refs = refs[:na], refs[na:na + nt]
        send_sems, recv_sems = refs[na + nt], refs[na + nt + 1]
        stages, local_sem = refs[-1 - nt:-1], refs[-1]
        x, y, c = _place()
        me = 4 * x + 2 * y + c

        def source(t, j):
            arr, window = windows[t][j]
            return window(in_refs[next(i for i, a in enumerate(arrays) if a is arr)])

        @pl.when(c == layer)
        def _():
            for t in range(nt):
                for j in range(N_CHIP):
                    @pl.when(me == 2 * j + layer)
                    def _():
                        _local_copy(source(t, j), land_refs[t].at[me], stages[t], local_sem)

        for t in range(nt):
            for j in range(N_CHIP):
                @pl.when(me != 2 * j + layer)
                def _():
                    pltpu.make_async_remote_copy(
                        src_ref=source(t, j), dst_ref=land_refs[t].at[me], send_sem=send_sems.at[N_CHIP * t + j],
                        recv_sem=recv_sems.at[N_DEV * t + me], device_id=(j // 2, j % 2, layer),
                        device_id_type=MESH).wait_send()

        @pl.when(c == layer)
        def _():
            for t in range(nt):
                for s in range(N_DEV):
                    @pl.when(me != s)
                    def _():
                        slot = land_refs[t].at[s]
                        pltpu.make_async_remote_copy(
                            src_ref=slot, dst_ref=slot, send_sem=send_sems.at[N_CHIP * t],
                            recv_sem=recv_sems.at[N_DEV * t + s], device_id=(x, y, c),
                            device_id_type=MESH).wait_recv()

    outs = pl.pallas_call(
        body, name=name,
        out_shape=tuple(pltpu.HBM(a.shape, a.dtype) for a in lands),
        in_specs=[HBM] * (na + nt) + [SEM, SEM, ANY],
        out_specs=tuple([HBM] * nt),
        input_output_aliases={na + t: t for t in range(nt)},
        scratch_shapes=[pltpu.VMEM(a.shape[1:], a.dtype) for a in lands] + [pltpu.SemaphoreType.DMA],
        compiler_params=pltpu.CompilerParams(has_side_effects=EFFECT, vmem_limit_bytes=VMEM_LIMIT),
    )(*[_in_hbm(a) for a in arrays], *lands, send_sems, recv_sems, after)
    return list(outs)


def _other_chips(x, y):
    return [(1 - x, y), (x, 1 - y), (1 - x, 1 - y)]


def _split_rows(rows):
    if rows < 32:
        return [(0, rows), (rows, 0)]
    cut = -(-(rows // 2) // 16) * 16
    return [(0, cut), (cut, rows - cut)]


def _gather_copies(in_refs, land_refs, send_sems, recv_sems, x, y, sender_core, both):
    copies = []
    for t, src in enumerate(in_refs):
        r0, n = _split_rows(src.shape[0])[sender_core]
        if n == 0:
            continue
        for rel, (tx, ty) in enumerate(_other_chips(x, y)):
            for tc in (0, 1) if both else (sender_core,):
                copies.append(dict(
                    src_ref=src.at[pl.ds(r0, n)], dst_ref=land_refs[t].at[2 * x + y].at[pl.ds(r0, n)],
                    send_sem=send_sems.at[6 * t + 2 * rel + tc],
                    recv_sem=recv_sems.at[2 * (3 * t + rel) + sender_core],
                    device_id=(tx, ty, tc), device_id_type=MESH))
    return copies


def _gather_start(shards, lands, name, both):
    nt = len(shards)

    def body(*refs):
        in_refs, land_refs = refs[:nt], refs[nt:2 * nt]
        send_sems, recv_sems = refs[2 * nt], refs[2 * nt + 1]
        token = refs[-1]
        x, y, c = _place()
        for core in range(2):
            @pl.when(c == core)
            def _():
                for cp_args in _gather_copies(in_refs, land_refs, send_sems, recv_sems, x, y, core, both):
                    pltpu.make_async_remote_copy(**cp_args).start()
        token[...] = jnp.zeros_like(token)

    outs = pl.pallas_call(
        body, name=name,
        out_shape=(pltpu.SemaphoreType.DMA((6 * nt,)), pltpu.SemaphoreType.DMA((6 * nt,)),
                   *[pltpu.HBM(a.shape, a.dtype) for a in lands], jax.ShapeDtypeStruct((8, 128), F32)),
        in_specs=[HBM] * (2 * nt),
        out_specs=(SEM, SEM, *[HBM] * nt, pl.BlockSpec(memory_space=pltpu.VMEM)),
        input_output_aliases={nt + t: 2 + t for t in range(nt)},
        compiler_params=pltpu.CompilerParams(has_side_effects=EFFECT, vmem_limit_bytes=VMEM_LIMIT),
    )(*[_in_hbm(a) for a in shards], *[_in_hbm(a) for a in lands])
    return outs[0], outs[1], list(outs[2:2 + nt]), outs[-1]


def _gather_wait(shards, lands, send_sems, recv_sems, after, name, both):
    nt = len(shards)

    def body(*refs):
        in_refs, land_refs = refs[:nt], refs[nt:2 * nt]
        send_sems, recv_sems = refs[2 * nt], refs[2 * nt + 1]
        stages, local_sem = refs[-1 - nt:-1], refs[-1]
        x, y, c = _place()

        for t in range(nt):
            _local_copy(in_refs[t], land_refs[t].at[2 * x + y], stages[t], local_sem)

        for core in range(2):
            @pl.when(c == core)
            def _():
                for cp_args in _gather_copies(in_refs, land_refs, send_sems, recv_sems, x, y, core, both):
                    pltpu.make_async_remote_copy(**cp_args).wait_send()

        def wait_parts_from(core):
            for t in range(nt):
                r0, n = _split_rows(in_refs[t].shape[0])[core]
                for rel, (tx, ty) in enumerate(_other_chips(x, y)):
                    if n > 0:
                        part = land_refs[t].at[2 * tx + ty].at[pl.ds(r0, n)]
                        pltpu.make_async_remote_copy(
                            src_ref=part, dst_ref=part, send_sem=send_sems.at[6 * t],
                            recv_sem=recv_sems.at[2 * (3 * t + rel) + core], device_id=(x, y, c),
                            device_id_type=MESH).wait_recv()

        for core in range(2):
            if both:
                wait_parts_from(core)
            else:
                pl.when(c == core)(functools.partial(wait_parts_from, core))

    outs = pl.pallas_call(
        body, name=name,
        out_shape=tuple(pltpu.HBM(a.shape, a.dtype) for a in lands),
        in_specs=[HBM] * (2 * nt) + [SEM, SEM] + [ANY] * len(after),
        out_specs=tuple([HBM] * nt),
        input_output_aliases={nt + t: t for t in range(nt)},
        scratch_shapes=[pltpu.VMEM(a.shape, a.dtype) for a in shards] + [pltpu.SemaphoreType.DMA],
        compiler_params=pltpu.CompilerParams(has_side_effects=EFFECT, vmem_limit_bytes=VMEM_LIMIT),
    )(*[_in_hbm(a) for a in shards], *lands, send_sems, recv_sems, *after)
    return list(outs)


def _pair_share(lands, name):
    nt = len(lands)
    splits = [_split_rows(a.shape[1]) for a in lands]

    def body(*refs):
        land_refs = refs[:nt]
        stages = refs[2 * nt:3 * nt]
        load_sems, send_sems, recv_sems = refs[3 * nt:]
        x, y, c = _place()

        def parts(core):
            out = []
            for t in range(nt):
                r0, n = splits[t][core]
                for rel, (tx, ty) in enumerate(_other_chips(x, y)):
                    if n > 0:
                        out.append((3 * t + rel, stages[t].at[rel, pl.ds(0, n)],
                                    land_refs[t].at[2 * tx + ty].at[pl.ds(r0, n)]))
            return out

        def send(core):
            loads = [pltpu.make_async_copy(part, stage, load_sems.at[k]) for k, stage, part in parts(core)]
            for cp in loads:
                cp.start()
            pushes = []
            for cp, (k, stage, part) in zip(loads, parts(core)):
                cp.wait()
                push = pltpu.make_async_remote_copy(src_ref=stage, dst_ref=part, send_sem=send_sems.at[k],
                                                    recv_sem=recv_sems.at[k], device_id=(x, y, 1 - c),
                                                    device_id_type=MESH)
                push.start()
                pushes.append(push)
            for push in pushes:
                push.wait_send()
            for k, _, part in parts(1 - core):
                pltpu.make_async_remote_copy(src_ref=part, dst_ref=part, send_sem=send_sems.at[k],
                                             recv_sem=recv_sems.at[k], device_id=(x, y, 1 - c),
                                             device_id_type=MESH).wait_recv()

        for core in range(2):
            pl.when(c == core)(functools.partial(send, core))

    outs = pl.pallas_call(
        body, name=name, in_specs=[ANY] * nt, out_specs=[ANY] * nt,
        out_shape=[jax.ShapeDtypeStruct(a.shape, a.dtype) for a in lands],
        input_output_aliases={t: t for t in range(nt)},
        scratch_shapes=[pltpu.VMEM((3, max(n for _, n in sp), a.shape[2]), a.dtype) for a, sp in zip(lands, splits)]
        + [pltpu.SemaphoreType.DMA((3 * nt,))] * 3,
        compiler_params=pltpu.CompilerParams(vmem_limit_bytes=VMEM_LIMIT),
    )(*lands)
    return list(outs)


def _tie(a, token):
    def body(a_ref, token_ref, o_ref):
        pass

    return pl.pallas_call(
        body, name="tie", in_specs=[ANY, ANY], out_specs=ANY,
        out_shape=jax.ShapeDtypeStruct(a.shape, a.dtype), input_output_aliases={0: 0},
    )(a, token)


def _sum_share(xs, name):
    _, r, cols = xs.shape
    tr = max(t for t in range(16, min(r, 704) + 1, 16) if r % t == 0)
    nblk = r // tr

    def body(x_ref, out_ref, acc_ref, send_sems, local_sems, recv_sem):
        i = pl.program_id(0)
        slot = i % 2
        x, y, c = _place()

        def copies(s, blk):
            dst = out_ref.at[c, pl.ds(blk * tr, tr), :]
            loc = pltpu.make_async_copy(acc_ref.at[s], dst, local_sems.at[s])
            rem = pltpu.make_async_remote_copy(src_ref=acc_ref.at[s], dst_ref=dst, send_sem=send_sems.at[s],
                                               recv_sem=recv_sem, device_id=(x, y, 1 - c), device_id_type=MESH)
            return loc, rem

        @pl.when(i >= 2)
        def _():
            loc, rem = copies(slot, i - 2)
            loc.wait()
            rem.wait_send()

        acc = x_ref[0].astype(F32)
        for k in range(1, N_DEV):
            acc = acc + x_ref[k].astype(F32)
        acc_ref[slot] = acc
        loc, rem = copies(slot, i)
        loc.start()
        rem.start()

        @pl.when(i == nblk - 1)
        def _():
            for back in range(min(2, nblk)):
                blk = nblk - 1 - back
                loc, rem = copies(blk % 2, blk)
                loc.wait()
                rem.wait_send()
            theirs = out_ref.at[1 - c]
            pltpu.make_async_remote_copy(src_ref=theirs, dst_ref=theirs, send_sem=send_sems.at[0],
                                         recv_sem=recv_sem, device_id=(x, y, 1 - c),
                                         device_id_type=MESH).wait_recv()

    return pl.pallas_call(
        body, name=name, grid=(nblk,),
        in_specs=[pl.BlockSpec((N_DEV, tr, cols), lambda i: (0, i, 0))],
        out_specs=ANY,
        out_shape=jax.ShapeDtypeStruct((2, r, cols), F32),
        scratch_shapes=[pltpu.VMEM((2, tr, cols), F32), pltpu.SemaphoreType.DMA((2,)),
                        pltpu.SemaphoreType.DMA((2,)), pltpu.SemaphoreType.DMA],
        compiler_params=_params("arbitrary"),
    )(xs)


def _sum8(xs, name):
    _, r, cols = xs.shape
    tr = 8
    for cand in (256, 128, 64, 32, 16):
        if r % cand == 0:
            tr = cand
            break

    def body(x_ref, o_ref):
        acc = x_ref[0].astype(F32)
        for k in range(1, N_DEV):
            acc = acc + x_ref[k].astype(F32)
        o_ref[...] = acc

    return pl.pallas_call(
        body, name=name, grid=(r // tr,),
        in_specs=[pl.BlockSpec((N_DEV, tr, cols), lambda i: (0, i, 0))],
        out_specs=pl.BlockSpec((tr, cols), lambda i: (i, 0)),
        out_shape=jax.ShapeDtypeStruct((r, cols), F32),
        compiler_params=_params("parallel"),
    )(xs)


def _adamw(w, g, m, v, name):
    R, C = w.shape
    tr = max([t for t in range(8, min(R, 512) + 1, 8) if R % t == 0] or [R])
    c1 = 1.0 / (1.0 - ADAM_B1 ** ADAM_STEP)
    c2 = 1.0 / (1.0 - ADAM_B2 ** ADAM_STEP)

    def body(w_ref, g_ref, m_ref, v_ref, d_ref, nm_ref, nv_ref):
        gv = g_ref[...]
        nm = ADAM_B1 * m_ref[...] + (1.0 - ADAM_B1) * gv
        nv = ADAM_B2 * v_ref[...] + (1.0 - ADAM_B2) * (gv * gv)
        d_ref[...] = (-ADAM_LR) * ((nm * c1) / (jnp.sqrt(nv * c2) + ADAM_EPS) + ADAM_WD * w_ref[...])
        nm_ref[...] = nm
        nv_ref[...] = nv

    spec = pl.BlockSpec((tr, C), lambda i: (i, 0))
    shape = jax.ShapeDtypeStruct((R, C), F32)
    return pl.pallas_call(
        body, name=name, grid=(R // tr,), in_specs=[spec] * 4, out_specs=[spec] * 3, out_shape=[shape] * 3,
        compiler_params=_params("parallel"),
    )(w, g, m, v)


def _flat_rows(parts, rows):
    flat = jnp.concatenate([q.reshape(-1) for q in parts])
    flat = jnp.pad(flat, (0, rows * LANES - flat.shape[0]))
    return flat.reshape(rows, LANES)


def _round_up(n, m):
    return (n + m - 1) // m * m


def _block_diag(w):
    H, n, _ = w.shape
    eye = jnp.eye(H, dtype=w.dtype)
    return (eye[:, None, :, None] * w[:, :, None, :]).reshape(H * n, H * n)


def _diag_blocks(w, H, n):
    w4 = w.reshape(H, n, H, n)
    return jnp.stack([w4[h, :, h, :] for h in range(H)])


W_IN_T = ("ffn1_w_in", "w_in", "ffn2_w_in")


def _ffn_in_weights(g_in):
    zeros = jnp.zeros((FF_HALF - FF_SHARD, D), g_in.dtype)
    wg_t = jnp.concatenate([g_in[0], zeros, g_in[1], zeros], axis=0)
    wu_t = jnp.concatenate([g_in[2], zeros, g_in[3], zeros], axis=0)
    return wg_t, wu_t


def _ffn_out_weights(g_out):
    zeros = jnp.zeros((FF_HALF - FF_SHARD, D), g_out.dtype)
    return jnp.concatenate([g_out[0], g_out[1], zeros, g_out[2], g_out[3], zeros], axis=0)


LAND_SHAPES = {"ffn1_w_in": (FF_SHARD, D), "ffn1_w_out": (FF_ROWS, D), "w_in": (D_IN // N_CHIP, D),
               "w_out": (D // N_CHIP, D), "ffn2_w_in": (FF_SHARD, D), "ffn2_w_out": (FF_ROWS, D)}


def _rows_window(arr, start, size):
    return arr, lambda r: r.at[pl.ds(start, size), :]


def _w_in_grad_windows(dwg_t, dwu_t):
    return [_rows_window(dwg_t if j < 2 else dwu_t, (j % 2) * FF_HALF, FF_SHARD) for j in range(N_CHIP)]


def _w_out_grad_windows(dwout):
    return [_rows_window(dwout, (j // 2) * FF_HALF + (j % 2) * FF_ROWS, FF_ROWS) for j in range(N_CHIP)]


def _mix_grad_windows(dwin_t, dwo):
    win = [_rows_window(dwin_t, j * (D_IN // N_CHIP), D_IN // N_CHIP) for j in range(N_CHIP)]
    wo = [_rows_window(dwo, j * (D // N_CHIP), D // N_CHIP) for j in range(N_CHIP)]
    return win, wo


def kernel(x, ffn1_norm, ffn1_w_in, ffn1_w_out, mix_norm, w_in, conv_w, conv_b, rg_w_a, rg_b_a, rg_w_x, rg_b_x, lru_lambda, pool_w, pool_scale, sgu_norm, sgu_w, sgu_b, w_out, ffn2_norm, ffn2_w_in, ffn2_w_out, final_norm, loss_target, m_ffn1_norm, m_ffn1_w_in, m_ffn1_w_out, m_mix_norm, m_w_in, m_conv_w, m_conv_b, m_rg_w_a, m_rg_b_a, m_rg_w_x, m_rg_b_x, m_lru_lambda, m_pool_w, m_pool_scale, m_sgu_norm, m_sgu_w, m_sgu_b, m_w_out, m_ffn2_norm, m_ffn2_w_in, m_ffn2_w_out, m_final_norm, v_ffn1_norm, v_ffn1_w_in, v_ffn1_w_out, v_mix_norm, v_w_in, v_conv_w, v_conv_b, v_rg_w_a, v_rg_b_a, v_rg_w_x, v_rg_b_x, v_lru_lambda, v_pool_w, v_pool_scale, v_sgu_norm, v_sgu_w, v_sgu_b, v_w_out, v_ffn2_norm, v_ffn2_w_in, v_ffn2_w_out, v_final_norm):
    args = locals()
    W = {n: args[n] for n in WEIGHTS}
    M = {n: args["m_" + n] for n in WEIGHTS}
    V = {n: args["v_" + n] for n in WEIGHTS}
    depth = ffn1_norm.shape[0]
    T = x.shape[1]
    xi, yi, ci = _place()
    chip = 2 * xi + yi

    assert depth == 2, "core c of a chip reduces layer c"
    groups = [(l, names) for l in range(depth)
              for names in (["ffn1_w_in"], ["ffn1_w_out", "w_in", "w_out"], ["ffn2_w_in", "ffn2_w_out"])]
    def stored(a, n):
        return jnp.swapaxes(a, 1, 2) if n in W_IN_T else a

    wb = {n: stored(W[n], n).astype(BF16) for n in BIG}
    groups[1][1].append("conv_w")
    conv_shard = conv_w.reshape(-1, conv_w.shape[-1])
    flights = {}

    def weights_start(k, dep=None):
        l, names = groups[k]
        shards = [conv_shard if n == "conv_w" else wb[n][l] for n in names]
        if dep is not None:
            shards[0] = _tie(shards[0], dep)
        lands = [lax.empty((N_CHIP,) + s.shape, s.dtype) for s in shards]
        send, recv, lands, token = _gather_start(shards, lands, "weights_start_%d" % k, k >= 2)
        flights[k] = (shards, lands, send, recv)
        return token

    def weights_wait(k, after):
        l, names = groups[k]
        shards, lands, send, recv = flights[k]
        got = _gather_wait(shards, lands, send, recv, after, "weights_wait_%d" % k, k >= 2)
        token = weights_start(k + 1, got[0]) if k + 1 < len(groups) else None
        if k < 2:
            got = _pair_share(got, "weights_share_%d" % k)
        return dict(zip(names, got)), token

    def after_start(a, token):
        return a if token is None else _tie(a, token)

    first_tokens = [weights_start(0)]

    layers = []
    for l in range(depth):
        L = {f: dict(norm=W[f + "_norm"][l][None]) for f in ("ffn1", "ffn2")}
        ws = jnp.where(jnp.tril(jnp.ones((CHUNK, CHUNK), bool))[None], sgu_w[l], 0.0)
        wax = jnp.concatenate([_block_diag(rg_w_a[l]), _block_diag(rg_w_x[l])], axis=1)
        wpool = _block_diag(pool_w[l])
        L["mix"] = dict(
            conv_b=conv_b[l][None], wax=wax.astype(BF16), wax_t=wax.T.astype(BF16),
            bax=jnp.concatenate([rg_b_a[l].reshape(-1), rg_b_x[l].reshape(-1)])[None], lam=lru_lambda[l][None],
            wpool=wpool.astype(BF16), wpool_t=wpool.T.astype(BF16), pool_scale=pool_scale[l][None],
            sgu_norm=sgu_norm[l][None], ws=ws.astype(BF16), ws_t=jnp.swapaxes(ws, 1, 2).astype(BF16),
            bz=jnp.repeat(sgu_b[l].T, 64, axis=1))
        L["mix_norm"] = mix_norm[l][None]
        layers.append(L)
    for token in first_tokens:
        layers[0]["ffn1"]["norm"] = _tie(layers[0]["ffn1"]["norm"], token)

    xs = x[0]
    saved = []
    for l, L in enumerate(layers):
        F1, F2 = L["ffn1"], L["ffn2"]
        got, token = weights_wait(3 * l, [xs] + ([wb[n] for n in BIG] if l == 0 else []))
        F1["wg"], F1["wu"] = _ffn_in_weights(got["ffn1_w_in"])
        F1["norm"] = after_start(F1["norm"], token)
        h, da_dg, da_du, a = _ffn_in(xs, F1["norm"], F1["wg"], F1["wu"])
        got, token = weights_wait(3 * l + 1, [a])
        F1["wout"] = _ffn_out_weights(got["ffn1_w_out"])
        L["w_in"] = got["w_in"].reshape(D_IN, D)
        L["w_out"] = got["w_out"].reshape(D, D)
        if "conv_w" in got:
            conv_full = jnp.concatenate([got["conv_w"][j] for j in range(N_CHIP)], axis=1)
            for ll in range(depth):
                layers[ll]["mix"]["conv_w"] = conv_full.reshape(depth, 4, D_RNN)[ll]
        a = after_start(a, token)
        x1 = _mm_res(a, F1["wout"], xs, 0.5, "ffn_out", tm=512, tn=D)
        s1 = (xs, h, da_dg, da_du, a)
        hm, p = _mix_in(x1, L["mix_norm"], L["w_in"])
        ycat, hs = _mix_fwd(p, L["mix"])
        x2 = _mm_res(ycat, L["w_out"], x1, 1.0, "mix_out", tm=512, tn=D)
        got, token = weights_wait(3 * l + 2, [x2])
        F2["wg"], F2["wu"] = _ffn_in_weights(got["ffn2_w_in"])
        F2["wout"] = _ffn_out_weights(got["ffn2_w_out"])
        F2["norm"] = after_start(F2["norm"], token)
        s2 = (x2,) + tuple(_ffn_in(x2, F2["norm"], F2["wg"], F2["wu"]))
        x3 = _mm_res(s2[-1], F2["wout"], x2, 0.5, "ffn_out", tm=512, tn=D)
        saved.append((s1, (x1, hm, p, ycat, hs), s2))
        xs = x3

    dx, dxb, d_final, loss_part = _final(xs, loss_target[0], final_norm[None], 0.5)

    G = {n: [None] * depth for n in SMALL if n != "final_norm"}
    lands = {n: lax.empty((N_DEV,) + LAND_SHAPES[n], BF16) for n in BIG}
    in_flight = []

    def send_grads(l, windows, tag):
        names = list(windows)
        send, recv, thru, token = _exchange_start(l, [windows[n] for n in names], [lands[n] for n in names],
                                                  "grads_start_" + tag)
        lands.update(zip(names, thru))
        in_flight.append((l, names, [windows[n] for n in names], send, recv, "grads_wait_" + tag))
        return token

    def ffn_bwd(dx, dxb, F, s, f, l, pending, send_now):
        xin, h, da_dg, da_du, a = s
        dwout = _mm_tn(a, dxb, 1.0, "ffn_dwout")
        if send_now:
            token = send_grads(l, {f + "_w_out": _w_out_grad_windows(dwout)}, "l%d_%s_out" % (l, f))
            dxb = _tie(dxb, token)
        else:
            pending[f + "_w_out"] = _w_out_grad_windows(dwout)
        dg, du = _ffn_mid_bwd(dxb, F["wout"], da_dg, da_du)
        dwg = _mm_tn(dg, h, 1.0, "ffn_dwg")
        dwu = _mm_tn(du, h, 1.0, "ffn_dwu")
        pending[f + "_w_in"] = _w_in_grad_windows(dwg, dwu)
        if send_now:
            dg = _tie(dg, send_grads(l, pending, "l%d_%s" % (l, f)))
        dx, dxb, dn = _dh_rms_bwd([(dg, F["wg"]), (du, F["wu"])], xin, F["norm"], dx,
                                  1.0 if f == "ffn2" else 0.5, "ffn_dh")
        G[f + "_norm"][l] = dn[0]
        return dx, dxb

    for l in reversed(range(depth)):
        L = layers[l]
        s1, (x1, hm, p, ycat, hs), s2 = saved[l]
        pending = {}
        dx, dxb = ffn_bwd(dx, dxb, L["ffn2"], s2, "ffn2", l, pending, l == 0)
        if l == 0:
            pending = {}
        dycat = _mm_nt(dxb, L["w_out"], "mix_dy", tm=512, tn=D)
        dwo = _mm_tn(ycat, dxb, 1.0, "mix_dwout")
        mg = _mix_bwd(dycat, p, hs, L["mix"])
        dwin = _mm_tn(mg["dp"], hm, 1.0, "mix_dwin")
        pending["w_in"], pending["w_out"] = _mix_grad_windows(dwin, dwo)
        if l == 0:
            dp = _tie(mg["dp"], send_grads(l, pending, "l0_mix"))
            pending = {}
        else:
            dp = mg["dp"]
        dx, dxb, dn = _dh_rms_bwd([(dp, L["w_in"])], x1, L["mix_norm"], dx, 0.5, "mix_dh")
        G["mix_norm"][l] = dn[0]
        G["conv_w"][l], G["conv_b"][l] = mg["conv_w"], mg["conv_b"][0]
        G["rg_w_a"][l] = _diag_blocks(mg["wax"][:, :D_RNN], 8, 64)
        G["rg_w_x"][l] = _diag_blocks(mg["wax"][:, D_RNN:], 8, 64)
        G["rg_b_a"][l] = mg["bax"][0, :D_RNN].reshape(8, 64)
        G["rg_b_x"][l] = mg["bax"][0, D_RNN:].reshape(8, 64)
        G["lru_lambda"][l] = mg["lam"][0]
        G["pool_w"][l] = _diag_blocks(mg["wpool"], 4, 64)
        G["pool_scale"][l], G["sgu_norm"][l] = mg["pool_scale"][0], mg["sgu_norm"][0]
        G["sgu_w"][l] = mg["ws"]
        G["sgu_b"][l] = mg["bz"].reshape(CHUNK, 4, 64).sum(-1).T
        dx, dxb = ffn_bwd(dx, dxb, L["ffn1"], s1, "ffn1", l, pending, l == 0)
        if l > 0:
            dxb = _tie(dxb, send_grads(l, pending, "l%d" % l))
    grad_x = dx[None]
    G = {n: jnp.stack(v) for n, v in G.items()}
    G["final_norm"] = d_final[0]

    for l, names, windows, send, recv, tag in in_flight:
        lands.update(zip(names, _exchange_wait(l, windows, [lands[n] for n in names], send, recv, dx, tag)))
    both = [_sum_share(lands[n], "sum_share_" + n) for n in BIG]
    grads = dict(zip(BIG, both))

    small_sizes = [int(np.prod(G[n].shape)) for n in SMALL]
    srows = _round_up(sum(small_sizes) + 1, N_DEV * 8 * LANES) // (N_DEV * LANES)
    sflat = _flat_rows([G[n] for n in SMALL] + [loss_part[0, :1]], N_DEV * srows)
    sgot = _all_to_all(sflat.reshape(N_DEV, srows, LANES), "exchange_small_grads")
    sall = _all_gather8(_sum8(sgot, "sum_small_grads"), "share_small_grads").reshape(-1)
    off = 0
    for n, size in zip(SMALL, small_sizes):
        grads[n] = sall[off:off + size].reshape(G[n].shape)
        off += size
    loss = sall[off]
    grads["conv_w"] = lax.dynamic_slice_in_dim(grads["conv_w"], chip * conv_w.shape[2], conv_w.shape[2], axis=2)

    delta, new_m, new_v = {}, {}, {}
    for n in BIG:
        shp = grads[n].shape
        two_d = (shp[0] * shp[1], shp[2])
        outs = _adamw(stored(W[n], n).reshape(two_d), grads[n].reshape(two_d), stored(M[n], n).reshape(two_d),
                      stored(V[n], n).reshape(two_d), "adamw_" + n)
        delta[n], new_m[n], new_v[n] = (stored(o.reshape(shp), n) for o in outs)
        grads[n] = stored(grads[n], n)
    arows = _round_up(sum(int(np.prod(W[n].shape)) for n in SMALL), 8 * LANES) // LANES
    outs = _adamw(*(_flat_rows([src[n] for n in SMALL], arows) for src in (W, grads, M, V)), "adamw_small")
    outs = [o.reshape(-1) for o in outs]
    off = 0
    for n in SMALL:
        size = int(np.prod(W[n].shape))
        delta[n], new_m[n], new_v[n] = (o[off:off + size].reshape(W[n].shape) for o in outs)
        off += size

    return (loss, grad_x, *[grads[n] for n in WEIGHTS], *[delta[n] for n in WEIGHTS],
            *[new_m[n] for n in WEIGHTS], *[new_v[n] for n in WEIGHTS])
```

```python
import functools
import math

import jax
import jax.numpy as jnp
import numpy as np
from jax import lax
from jax.experimental import pallas as pl
from jax.experimental.pallas import tpu as pltpu

F32 = jnp.float32
BF16 = jnp.bfloat16
MESH = pl.DeviceIdType.MESH

D = 1024
D_RNN = 512
D_POOL = 256
D_SGU = 256
D_IN = 1792
D_FF = 2752
D_FFP = 2816
N_CHIP = 4
FF_SHARD = D_FF // 2
FF_HALF = D_FFP // 2
FF_ROWS = D_FF // N_CHIP
CHUNK = 128
HALO = 16
EPS = 1e-6
LRU_C = 8.0
N_DEV = 8
LANES = 1024
VMEM_LIMIT = 56 * 1024 * 1024

ADAM_LR, ADAM_B1, ADAM_B2, ADAM_EPS, ADAM_WD, ADAM_STEP = 0.001, 0.9, 0.999, 1e-08, 0.01, 10

BIG = ("ffn1_w_in", "ffn1_w_out", "w_in", "w_out", "ffn2_w_in", "ffn2_w_out")
SMALL = ("ffn1_norm", "mix_norm", "conv_w", "conv_b", "rg_w_a", "rg_b_a", "rg_w_x", "rg_b_x", "lru_lambda",
         "pool_w", "pool_scale", "sgu_norm", "sgu_w", "sgu_b", "ffn2_norm", "final_norm")
WEIGHTS = ("ffn1_norm", "ffn1_w_in", "ffn1_w_out", "mix_norm", "w_in", "conv_w", "conv_b", "rg_w_a", "rg_b_a",
           "rg_w_x", "rg_b_x", "lru_lambda", "pool_w", "pool_scale", "sgu_norm", "sgu_w", "sgu_b", "w_out",
           "ffn2_norm", "ffn2_w_in", "ffn2_w_out", "final_norm")


def _params(*sem):
    return pltpu.CompilerParams(dimension_semantics=sem, vmem_limit_bytes=VMEM_LIMIT)


def _gelu(x):
    c = math.sqrt(2.0 / math.pi)
    t = jnp.tanh(c * (x + 0.044715 * (x * x * x)))
    return 0.5 * x * (1.0 + t)


def _gelu_and_grad(x):
    c = math.sqrt(2.0 / math.pi)
    x2 = x * x
    t = jnp.tanh(c * (x + 0.044715 * (x2 * x)))
    g = 0.5 * x * (1.0 + t)
    dg = 0.5 * (1.0 + t) + 0.5 * x * (1.0 - t * t) * (c * (1.0 + 3.0 * 0.044715 * x2))
    return g, dg


def _sigmoid(x):
    return 0.5 * jnp.tanh(0.5 * x) + 0.5


def _dot(a, b):
    return jnp.dot(a, b, preferred_element_type=F32)


def _dot_tn(a, b):
    return lax.dot_general(a, b, (((0,), (0,)), ((), ())), preferred_element_type=F32)


def _dot_nt(a, b):
    return lax.dot_general(a, b, (((1,), (1,)), ((), ())), preferred_element_type=F32)


def _tile(n, limit):
    best = 128
    for t in range(128, min(n, limit) + 1, 128):
        if n % t == 0:
            best = t
    assert n % best == 0, (n, limit)
    return best


def _mm_res(a, b, res, scale, name, tm=1024, tn=512):
    M, K = a.shape
    N = b.shape[1]
    tm, tn = min(tm, M), _tile(N, tn)

    def body(a_ref, b_ref, r_ref, o_ref):
        o_ref[...] = r_ref[...] + scale * _dot(a_ref[...], b_ref[...])

    return pl.pallas_call(
        body, name=name, grid=(M // tm, N // tn),
        in_specs=[pl.BlockSpec((tm, K), lambda i, j: (i, 0)), pl.BlockSpec((K, tn), lambda i, j: (0, j)),
                  pl.BlockSpec((tm, tn), lambda i, j: (i, j))],
        out_specs=pl.BlockSpec((tm, tn), lambda i, j: (i, j)),
        out_shape=jax.ShapeDtypeStruct((M, N), F32),
        compiler_params=_params("parallel", "parallel"),
    )(a, b, res)


def _mm_nt(a, b, name, tm=1024, tn=512):
    M, K = a.shape
    N = b.shape[0]
    tm, tn = min(tm, M), _tile(N, tn)

    def body(a_ref, b_ref, o_ref):
        o_ref[...] = _dot_nt(a_ref[...], b_ref[...])

    return pl.pallas_call(
        body, name=name, grid=(M // tm, N // tn),
        in_specs=[pl.BlockSpec((tm, K), lambda i, j: (i, 0)), pl.BlockSpec((tn, K), lambda i, j: (j, 0))],
        out_specs=pl.BlockSpec((tm, tn), lambda i, j: (i, j)),
        out_shape=jax.ShapeDtypeStruct((M, N), F32),
        compiler_params=_params("parallel", "parallel"),
    )(a, b)


def _dh_rms_bwd(pairs, x, g, dres, copy_scale, name, tm=512):
    T = x.shape[0]
    tm = min(tm, T)
    n = len(pairs)

    def body(*refs):
        ab = refs[:2 * n]
        x_ref, g_ref, dres_ref, dx_ref, dxb_ref, dg_ref = refs[2 * n:]
        dy = _dot(ab[0][...], ab[1][...])
        for k in range(1, n):
            dy = dy + _dot(ab[2 * k][...], ab[2 * k + 1][...])
        xv = x_ref[...]
        r = lax.rsqrt(jnp.mean(xv * xv, axis=-1, keepdims=True) + EPS)
        xhat = xv * r
        dxhat = dy * g_ref[...]
        dx = dres_ref[...] + r * (dxhat - xhat * jnp.mean(dxhat * xhat, axis=-1, keepdims=True))
        dx_ref[...] = dx
        dxb_ref[...] = (copy_scale * dx).astype(BF16)

        @pl.when(pl.program_id(0) == 0)
        def _():
            dg_ref[...] = jnp.zeros_like(dg_ref)

        dg_ref[...] += jnp.sum(dy * xhat, axis=0, keepdims=True)

    row = pl.BlockSpec((tm, D), lambda i: (i, 0))
    vec = pl.BlockSpec((1, D), lambda i: (0, 0))
    in_specs, operands = [], []
    for a, b in pairs:
        in_specs += [pl.BlockSpec((tm, a.shape[1]), lambda i: (i, 0)),
                     pl.BlockSpec(b.shape, lambda i: (0, 0), pipeline_mode=pl.Buffered(1))]
        operands += [a, b]
    return pl.pallas_call(
        body, name=name, grid=(T // tm,),
        in_specs=in_specs + [row, vec, row], out_specs=[row, row, vec],
        out_shape=[jax.ShapeDtypeStruct((T, D), F32), jax.ShapeDtypeStruct((T, D), BF16),
                   jax.ShapeDtypeStruct((1, D), F32)],
        compiler_params=_params("arbitrary"),
    )(*operands, x, g, dres)


def _mm_tn(a, b, scale, name, tm=1792, tn=1792, tk=2048):
    T, M = a.shape
    N = b.shape[1]
    tm, tn, tk = _tile(M, tm), _tile(N, tn), min(tk, T)
    nk = T // tk

    def body(a_ref, b_ref, o_ref, acc_ref):
        k = pl.program_id(2)

        @pl.when(k == 0)
        def _():
            acc_ref[...] = jnp.zeros_like(acc_ref)

        acc_ref[...] += _dot_tn(a_ref[...], b_ref[...])

        @pl.when(k == nk - 1)
        def _():
            o_ref[...] = (scale * acc_ref[...]).astype(BF16)

    return pl.pallas_call(
        body, name=name, grid=(M // tm, N // tn, nk),
        in_specs=[pl.BlockSpec((tk, tm), lambda i, j, k: (k, i)), pl.BlockSpec((tk, tn), lambda i, j, k: (k, j))],
        out_specs=pl.BlockSpec((tm, tn), lambda i, j, k: (i, j)),
        out_shape=jax.ShapeDtypeStruct((M, N), BF16),
        scratch_shapes=[pltpu.VMEM((tm, tn), F32)],
        compiler_params=_params("parallel", "parallel", "arbitrary"),
    )(a, b)


def _rms_rows(x_ref, gain_ref):
    xv = x_ref[...]
    r = lax.rsqrt(jnp.mean(xv * xv, axis=-1, keepdims=True) + EPS)
    return (xv * r * gain_ref[...]).astype(BF16)


def _ffn_in(x, gain, wg_t, wu_t, tm=256, tn=D_FFP):
    T = x.shape[0]
    tm = min(tm, T)

    def body(x_ref, gain_ref, wg_ref, wu_ref, h_ref, dg_ref, du_ref, a_ref):
        hv = _rms_rows(x_ref, gain_ref)

        @pl.when(pl.program_id(0) == 0)
        def _():
            h_ref[...] = hv

        g = _dot_nt(hv, wg_ref[...])
        u = _dot_nt(hv, wu_ref[...])
        s = _sigmoid(g)
        gs = g * s
        dg_ref[...] = (u * (s + gs - gs * s)).astype(BF16)
        du_ref[...] = gs.astype(BF16)
        a_ref[...] = (gs * u).astype(BF16)

    row = pl.BlockSpec((tm, D), lambda j, i: (i, 0))
    last = T // tm - 1
    h_spec = pl.BlockSpec((tm, D), lambda j, i: (jnp.where(j == 0, i, last), 0))
    wspec = pl.BlockSpec((tn, D), lambda j, i: (j, 0))
    ospec = pl.BlockSpec((tm, tn), lambda j, i: (i, j))
    oshape = jax.ShapeDtypeStruct((T, D_FFP), BF16)
    return pl.pallas_call(
        body, name="ffn_in", grid=(D_FFP // tn, T // tm),
        in_specs=[row, pl.BlockSpec((1, D), lambda j, i: (0, 0)), wspec, wspec],
        out_specs=[h_spec, ospec, ospec, ospec],
        out_shape=[jax.ShapeDtypeStruct((T, D), BF16), oshape, oshape, oshape],
        compiler_params=_params("arbitrary", "arbitrary"),
    )(x, gain, wg_t, wu_t)


def _mix_in(x, gain, w_in_t, tm=512):
    T = x.shape[0]
    tm = min(tm, T)

    def body(x_ref, gain_ref, w_ref, h_ref, p_ref):
        hv = _rms_rows(x_ref, gain_ref)
        h_ref[...] = hv
        p_ref[...] = _dot_nt(hv, w_ref[...])

    row = pl.BlockSpec((tm, D), lambda i: (i, 0))
    return pl.pallas_call(
        body, name="mix_in", grid=(T // tm,),
        in_specs=[row, pl.BlockSpec((1, D), lambda i: (0, 0)), pl.BlockSpec((D_IN, D), lambda i: (0, 0))],
        out_specs=[row, pl.BlockSpec((tm, D_IN), lambda i: (i, 0))],
        out_shape=[jax.ShapeDtypeStruct((T, D), BF16), jax.ShapeDtypeStruct((T, D_IN), F32)],
        compiler_params=_params("parallel"),
    )(x, gain, w_in_t)


def _ffn_mid_bwd(dyh, wout, da_dg, da_du, tm=512, tn=FF_HALF):
    T = dyh.shape[0]
    tm = min(tm, T)

    def body(dy_ref, w_ref, pg_ref, pu_ref, dg_ref, du_ref):
        da = _dot_nt(dy_ref[...], w_ref[...])
        dg_ref[...] = (da * pg_ref[...].astype(F32)).astype(BF16)
        du_ref[...] = (da * pu_ref[...].astype(F32)).astype(BF16)

    ospec = pl.BlockSpec((tm, tn), lambda j, i: (i, j))
    oshape = jax.ShapeDtypeStruct((T, D_FFP), BF16)
    return pl.pallas_call(
        body, name="ffn_mid_bwd", grid=(D_FFP // tn, T // tm),
        in_specs=[pl.BlockSpec((tm, D), lambda j, i: (i, 0)), pl.BlockSpec((tn, D), lambda j, i: (j, 0)),
                  ospec, ospec],
        out_specs=[ospec, ospec], out_shape=[oshape, oshape],
        compiler_params=_params("parallel", "parallel"),
    )(dyh, wout, da_dg, da_du)


def _final(x, tgt, gf, copy_scale, tm=512):
    T = x.shape[0]
    tm = min(tm, T)

    def body(x_ref, t_ref, g_ref, dx_ref, dxb_ref, dg_ref, loss_ref):
        xv = x_ref[...]
        r = lax.rsqrt(jnp.mean(xv * xv, axis=-1, keepdims=True) + EPS)
        xhat = xv * r
        err = xhat * g_ref[...] - t_ref[...]
        dy = err * (1.0 / D)
        dxhat = dy * g_ref[...]
        dx = r * (dxhat - xhat * jnp.mean(dxhat * xhat, axis=-1, keepdims=True))
        dx_ref[...] = dx
        dxb_ref[...] = (copy_scale * dx).astype(BF16)

        @pl.when(pl.program_id(0) == 0)
        def _():
            dg_ref[...] = jnp.zeros_like(dg_ref)
            loss_ref[...] = jnp.zeros_like(loss_ref)

        dg_ref[...] += jnp.sum(dy * xhat, axis=0, keepdims=True)
        loss_ref[...] += (0.5 / D) * jnp.sum(err * err)

    row = pl.BlockSpec((tm, D), lambda i: (i, 0))
    vec = pl.BlockSpec((1, D), lambda i: (0, 0))
    return pl.pallas_call(
        body, name="final_loss", grid=(T // tm,),
        in_specs=[row, row, vec],
        out_specs=[row, row, vec, pl.BlockSpec((1, 128), lambda i: (0, 0))],
        out_shape=[jax.ShapeDtypeStruct((T, D), F32), jax.ShapeDtypeStruct((T, D), BF16),
                   jax.ShapeDtypeStruct((1, D), F32), jax.ShapeDtypeStruct((1, 128), F32)],
        compiler_params=_params("arbitrary"),
    )(x, tgt, gf)


def _mix_block(T, limit):
    return min(limit, T // 2)


def _rows(tb, width):
    return lax.broadcasted_iota(jnp.int32, (tb, width), 0)


def _rglru_gates(xc, wax_ref, bax_ref, lam_ref):
    pre = _dot(xc.astype(BF16), wax_ref[...]) + bax_ref[...]
    r = _sigmoid(pre[:, :D_RNN])
    ig = _sigmoid(pre[:, D_RNN:])
    z = -lam_ref[...]
    sp = jnp.maximum(z, 0.0) + jnp.log(1.0 + jnp.exp(-jnp.abs(z)))
    log_a = (-LRU_C) * r * sp
    a = jnp.exp(log_a)
    mult = jnp.sqrt(-jnp.tanh(log_a) * (1.0 + a * a))
    return r, ig, sp, a, mult


def _conv(xa_ext, cw_ref, cb_ref):
    y = cb_ref[...] + cw_ref[3:4, :] * xa_ext
    for k in range(1, 4):
        y = y + cw_ref[3 - k:4 - k, :] * pltpu.roll(xa_ext, k, 0)
    return y[HALO:]


def _pool_window_lanes():
    lane = lax.broadcasted_iota(jnp.int32, (1, D_POOL), 1)
    return jnp.where(lane < 64, 2, jnp.where(lane < 128, 4, jnp.where(lane < 192, 8, 16)))


def _pool_select(s2, s4, s8, s16):
    lane = lax.broadcasted_iota(jnp.int32, s2.shape, 1)
    return jnp.where(lane < 64, s2, jnp.where(lane < 128, s4, jnp.where(lane < 192, s8, s16)))


def _pool_diff(xp_ext, t0, tb):
    s2 = xp_ext + pltpu.roll(xp_ext, 1, 0)
    s4 = s2 + pltpu.roll(s2, 2, 0)
    s8 = s4 + pltpu.roll(s4, 4, 0)
    s16 = s8 + pltpu.roll(s8, 8, 0)
    sel = _pool_select(s2, s4, s8, s16)[HALO:]
    cnt = jnp.minimum(t0 + _rows(tb, D_POOL) + 1, _pool_window_lanes()).astype(F32)
    return sel / cnt - xp_ext[HALO:], cnt


def _head_masks():
    lane = lax.broadcasted_iota(jnp.int32, (1, D_SGU), 1)
    return [((lane >= 64 * h) & (lane < 64 * (h + 1))).astype(F32) for h in range(4)]


def _sgu_mix(w_ref, vch, masks):
    z = masks[0] * _dot(w_ref[0], vch)
    for h in range(1, 4):
        z = z + masks[h] * _dot(w_ref[h], vch)
    return z


def _mix_fwd(p, prm):
    T = p.shape[0]
    tb = _mix_block(T, 512)
    nb = T // tb

    def body(p_ref, xah_ref, xph_ref, cw_ref, cb_ref, wax_ref, bax_ref, lam_ref, wp_ref, ps_ref, sgn_ref,
             ws_ref, bz_ref, y_ref, hs_ref, carry_ref):
        i = pl.program_id(0)
        keep = (i > 0).astype(F32)

        @pl.when(i == 0)
        def _():
            carry_ref[...] = jnp.zeros_like(carry_ref)

        xa_ext = jnp.concatenate([xah_ref[...] * keep, p_ref[:, 512:1024]], axis=0)
        xc = _conv(xa_ext, cw_ref, cb_ref)
        r, ig, sp, a, mult = _rglru_gates(xc, wax_ref, bax_ref, lam_ref)
        bv = mult * (ig * xc)
        row = _rows(tb, D_RNN)
        s = 1
        while s < tb:
            m = row >= s
            bv = jnp.where(m, a * pltpu.roll(bv, s, 0) + bv, bv)
            a = jnp.where(m, a * pltpu.roll(a, s, 0), a)
            s *= 2
        h = bv + a * carry_ref[0:1, :]
        hs_ref[...] = h
        last = jnp.sum(jnp.where(_rows(8, D_RNN) == 7, hs_ref[tb - 8:tb, :], 0.0), axis=0, keepdims=True)
        carry_ref[...] = jnp.broadcast_to(last, carry_ref.shape)
        y_ref[:, 0:512] = (_gelu(p_ref[:, 0:512]) * h).astype(BF16)

        xp_ext = jnp.concatenate([xph_ref[...] * keep, p_ref[:, 1024:1280]], axis=0)
        d, _ = _pool_diff(xp_ext, i * tb, tb)
        y_ref[:, 512:768] = (_dot(d.astype(BF16), wp_ref[...]) * ps_ref[...]).astype(BF16)

        ug = _gelu(p_ref[:, 1280:1536])
        vg = _gelu(p_ref[:, 1536:1792])
        rv = lax.rsqrt(jnp.mean(vg * vg, axis=-1, keepdims=True) + EPS)
        vn = (vg * rv * sgn_ref[...]).astype(BF16)
        masks = _head_masks()
        for ci in range(tb // CHUNK):
            sl = slice(ci * CHUNK, (ci + 1) * CHUNK)
            z = _sgu_mix(ws_ref, vn[sl], masks) + bz_ref[...]
            y_ref[sl, 768:1024] = (ug[sl] * z).astype(BF16)

    hb = tb // HALO

    def halo(i):
        return jnp.maximum(i * hb - 1, 0)

    def full(shape):
        return pl.BlockSpec(shape, lambda i: (0,) * len(shape))

    return pl.pallas_call(
        body, name="mix_fwd", grid=(nb,),
        in_specs=[pl.BlockSpec((tb, D_IN), lambda i: (i, 0)),
                  pl.BlockSpec((HALO, D_RNN), lambda i: (halo(i), 1)),
                  pl.BlockSpec((HALO, D_POOL), lambda i: (halo(i), 4)),
                  full((4, D_RNN)), full((1, D_RNN)), full((D_RNN, 2 * D_RNN)), full((1, 2 * D_RNN)),
                  full((1, D_RNN)), full((D_POOL, D_POOL)), full((1, D_POOL)), full((1, D_SGU)),
                  full((4, CHUNK, CHUNK)), full((CHUNK, D_SGU))],
        out_specs=[pl.BlockSpec((tb, D), lambda i: (i, 0)), pl.BlockSpec((tb, D_RNN), lambda i: (i, 0))],
        out_shape=[jax.ShapeDtypeStruct((T, D), BF16), jax.ShapeDtypeStruct((T, D_RNN), F32)],
        scratch_shapes=[pltpu.VMEM((8, D_RNN), F32)],
        compiler_params=_params("arbitrary"),
    )(p, p, p, prm["conv_w"], prm["conv_b"], prm["wax"], prm["bax"], prm["lam"], prm["wpool"], prm["pool_scale"],
      prm["sgu_norm"], prm["ws"], prm["bz"])


def _mix_bwd(dy, p, hs, prm):
    T = p.shape[0]
    tb = _mix_block(T, 256)
    nb = T // tb
    hb = tb // HALO

    def body(dy_ref, p_ref, xah_ref, xph_ref, hs_ref, hsh_ref, cw_ref, cb_ref, wax_ref, waxt_ref, bax_ref,
             lam_ref, wp_ref, wpt_ref, ps_ref, sgn_ref, ws_ref, wst_ref, bz_ref,
             dp_ref, dcw_ref, dcb_ref, dwax_ref, dbax_ref, dlam_ref, dwp_ref, dps_ref, dsgn_ref, dws_ref,
             dbz_ref, gcarry_ref, xcfut_ref, mfut_ref):
        i = pl.program_id(0)
        bi = nb - 1 - i
        keep = (bi > 0).astype(F32)

        @pl.when(i == 0)
        def _():
            for ref in (dcw_ref, dcb_ref, dwax_ref, dbax_ref, dlam_ref, dwp_ref, dps_ref, dsgn_ref, dws_ref,
                        dbz_ref, gcarry_ref, xcfut_ref, mfut_ref):
                ref[...] = jnp.zeros_like(ref)

        xa_ext = jnp.concatenate([xah_ref[...] * keep, p_ref[:, 512:1024]], axis=0)
        xc = _conv(xa_ext, cw_ref, cb_ref)
        r, ig, sp, a, mult = _rglru_gates(xc, wax_ref, bax_ref, lam_ref)
        gg, dgg = _gelu_and_grad(p_ref[:, 0:512])
        dya = dy_ref[:, 0:512]
        dp_ref[:, 0:512] = (dya * hs_ref[...] * dgg).astype(BF16)
        row = _rows(tb, D_RNN)
        g = dya * gg + jnp.where(row == tb - 1, gcarry_ref[0:1, :], 0.0)
        al = pltpu.roll(a, tb - 1, 0)
        s = 1
        while s < tb:
            m = row < tb - s
            g = jnp.where(m, al * pltpu.roll(g, tb - s, 0) + g, g)
            al = jnp.where(m, al * pltpu.roll(al, tb - s, 0), al)
            s *= 2
        first = jnp.sum(jnp.where(_rows(8, D_RNN) == 0, (a * g)[0:8], 0.0), axis=0, keepdims=True)
        gcarry_ref[...] = jnp.broadcast_to(first, gcarry_ref.shape)
        hs_ext = jnp.concatenate([hsh_ref[...] * keep, hs_ref[...]], axis=0)
        h_prev = pltpu.roll(hs_ext, 1, 0)[HALO:]
        ix = ig * xc
        dlog_a = g * h_prev * a - (g * ix) * (a * a / mult)
        dlam_ref[...] += jnp.sum(dlog_a * r, axis=0, keepdims=True) * (LRU_C * _sigmoid(-lam_ref[...]))
        dpre_r = dlog_a * ((-LRU_C) * sp) * (r * (1.0 - r))
        dpre_i = (g * mult * xc) * (ig * (1.0 - ig))
        dpre = jnp.concatenate([dpre_r, dpre_i], axis=1)
        dbax_ref[...] += jnp.sum(dpre, axis=0, keepdims=True)
        dpre_b = dpre.astype(BF16)
        dwax_ref[...] += _dot_tn(xc.astype(BF16), dpre_b)
        dxc = g * mult * ig + _dot(dpre_b, waxt_ref[...])
        dcb_ref[...] += jnp.sum(dxc, axis=0, keepdims=True)
        for k in range(4):
            xs = xa_ext[HALO:] if k == 3 else pltpu.roll(xa_ext, 3 - k, 0)[HALO:]
            dcw_ref[k:k + 1, :] += jnp.sum(dxc * xs, axis=0, keepdims=True)
        dxc_ext = jnp.concatenate([dxc, xcfut_ref[...]], axis=0)
        n = tb + HALO
        dxa = cw_ref[3:4, :] * dxc_ext
        for k in range(1, 4):
            dxa = dxa + cw_ref[3 - k:4 - k, :] * pltpu.roll(dxc_ext, n - k, 0)
        dp_ref[:, 512:1024] = dxa[:tb].astype(BF16)
        xcfut_ref[...] = dxc[0:HALO]

        xp_ext = jnp.concatenate([xph_ref[...] * keep, p_ref[:, 1024:1280]], axis=0)
        d, cnt = _pool_diff(xp_ext, bi * tb, tb)
        db = d.astype(BF16)
        dyb = dy_ref[:, 512:768]
        dps_ref[...] += jnp.sum(dyb * _dot(db, wp_ref[...]), axis=0, keepdims=True)
        dq = (dyb * ps_ref[...]).astype(BF16)
        dwp_ref[...] += _dot_tn(db, dq)
        dd = _dot(dq, wpt_ref[...])
        mm = dd / cnt
        m_ext = jnp.concatenate([mm, mfut_ref[...]], axis=0)
        f2 = m_ext + pltpu.roll(m_ext, n - 1, 0)
        f4 = f2 + pltpu.roll(f2, n - 2, 0)
        f8 = f4 + pltpu.roll(f4, n - 4, 0)
        f16 = f8 + pltpu.roll(f8, n - 8, 0)
        dp_ref[:, 1024:1280] = (_pool_select(f2, f4, f8, f16)[:tb] - dd).astype(BF16)
        mfut_ref[...] = mm[0:HALO]

        ug, dug = _gelu_and_grad(p_ref[:, 1280:1536])
        vg, dvg = _gelu_and_grad(p_ref[:, 1536:1792])
        rv = lax.rsqrt(jnp.mean(vg * vg, axis=-1, keepdims=True) + EPS)
        vhat = vg * rv
        vn = (vhat * sgn_ref[...]).astype(BF16)
        dyc = dy_ref[:, 768:1024]
        masks = _head_masks()
        dz = dyc * ug
        dzb = dz.astype(BF16)
        dvn_parts = []
        for ci in range(tb // CHUNK):
            sl = slice(ci * CHUNK, (ci + 1) * CHUNK)
            z = _sgu_mix(ws_ref, vn[sl], masks) + bz_ref[...]
            dp_ref[sl, 1280:1536] = (dyc[sl] * z * dug[sl]).astype(BF16)
            dbz_ref[...] += dz[sl]
            for h in range(4):
                dws_ref[h] += _dot_nt((dz[sl] * masks[h]).astype(BF16), vn[sl])
            dvn_parts.append(_sgu_mix(wst_ref, dzb[sl], masks))
        dvn = jnp.concatenate(dvn_parts, axis=0)
        dsgn_ref[...] += jnp.sum(dvn * vhat, axis=0, keepdims=True)
        dvhat = dvn * sgn_ref[...]
        dvg_in = rv * (dvhat - vhat * jnp.mean(dvhat * vhat, axis=-1, keepdims=True))
        dp_ref[:, 1536:1792] = (dvg_in * dvg).astype(BF16)

        @pl.when(i == nb - 1)
        def _():
            tril = (lax.broadcasted_iota(jnp.int32, (CHUNK, CHUNK), 0)
                    >= lax.broadcasted_iota(jnp.int32, (CHUNK, CHUNK), 1)).astype(F32)
            for h in range(4):
                dws_ref[h] = dws_ref[h] * tril

    def blk(i):
        return nb - 1 - i

    def halo(i):
        return jnp.maximum(blk(i) * hb - 1, 0)

    def full(shape):
        return pl.BlockSpec(shape, lambda i: (0,) * len(shape))

    small_shapes = [(4, D_RNN), (1, D_RNN), (D_RNN, 2 * D_RNN), (1, 2 * D_RNN), (1, D_RNN), (D_POOL, D_POOL),
                    (1, D_POOL), (1, D_SGU), (4, CHUNK, CHUNK), (CHUNK, D_SGU)]
    outs = pl.pallas_call(
        body, name="mix_bwd", grid=(nb,),
        in_specs=[pl.BlockSpec((tb, D), lambda i: (blk(i), 0)),
                  pl.BlockSpec((tb, D_IN), lambda i: (blk(i), 0)),
                  pl.BlockSpec((HALO, D_RNN), lambda i: (halo(i), 1)),
                  pl.BlockSpec((HALO, D_POOL), lambda i: (halo(i), 4)),
                  pl.BlockSpec((tb, D_RNN), lambda i: (blk(i), 0)),
                  pl.BlockSpec((HALO, D_RNN), lambda i: (halo(i), 0)),
                  full((4, D_RNN)), full((1, D_RNN)), full((D_RNN, 2 * D_RNN)), full((2 * D_RNN, D_RNN)),
                  full((1, 2 * D_RNN)), full((1, D_RNN)), full((D_POOL, D_POOL)), full((D_POOL, D_POOL)),
                  full((1, D_POOL)), full((1, D_SGU)), full((4, CHUNK, CHUNK)), full((4, CHUNK, CHUNK)),
                  full((CHUNK, D_SGU))],
        out_specs=[pl.BlockSpec((tb, D_IN), lambda i: (blk(i), 0))] + [full(s) for s in small_shapes],
        out_shape=[jax.ShapeDtypeStruct((T, D_IN), BF16)] + [jax.ShapeDtypeStruct(s, F32) for s in small_shapes],
        scratch_shapes=[pltpu.VMEM((8, D_RNN), F32), pltpu.VMEM((HALO, D_RNN), F32),
                        pltpu.VMEM((HALO, D_POOL), F32)],
        compiler_params=_params("arbitrary"),
    )(dy, p, p, p, hs, hs, prm["conv_w"], prm["conv_b"], prm["wax"], prm["wax_t"], prm["bax"], prm["lam"],
      prm["wpool"], prm["wpool_t"], prm["pool_scale"], prm["sgu_norm"], prm["ws"], prm["ws_t"], prm["bz"])
    names = ("dp", "conv_w", "conv_b", "wax", "bax", "lam", "wpool", "pool_scale", "sgu_norm", "ws", "bz")
    return dict(zip(names, outs))


ANY = pl.BlockSpec(memory_space=pl.ANY)


def _place():
    x, y, c = lax.axis_index("x"), lax.axis_index("y"), lax.axis_index("c")
    return x, y, c


def _all_to_all(xs, name):
    def body(in_ref, out_ref, send_sems, recv_sems, local_sem):
        x, y, c = _place()
        me = 4 * x + 2 * y + c
        mine = pltpu.make_async_copy(in_ref.at[me], out_ref.at[me], local_sem)
        mine.start()
        copies = []
        for rel in range(1, N_DEV):
            tx = 1 - x if rel & 4 else x
            ty = 1 - y if rel & 2 else y
            tc = 1 - c if rel & 1 else c
            cp = pltpu.make_async_remote_copy(
                src_ref=in_ref.at[4 * tx + 2 * ty + tc], dst_ref=out_ref.at[me],
                send_sem=send_sems.at[rel - 1], recv_sem=recv_sems.at[rel - 1],
                device_id=(tx, ty, tc), device_id_type=MESH)
            cp.start()
            copies.append(cp)
        for cp in copies:
            cp.wait()
        mine.wait()

    return pl.pallas_call(
        body, name=name, in_specs=[ANY], out_specs=ANY,
        out_shape=jax.ShapeDtypeStruct(xs.shape, xs.dtype),
        scratch_shapes=[pltpu.SemaphoreType.DMA((N_DEV - 1,)), pltpu.SemaphoreType.DMA((N_DEV - 1,)),
                        pltpu.SemaphoreType.DMA],
    )(xs)


def _all_gather8(xs, name):
    def body(x_ref, out_ref, send_sems, recv_sems, local_sem):
        x, y, c = _place()
        me, sibling = (x, y, c), (x, y, 1 - c)
        chips = [(1 - x, y), (x, 1 - y), (1 - x, 1 - y)]

        def rows(px, py, pc):
            return out_ref.at[4 * px + 2 * py + pc]

        def copy(k, block, to, src=None):
            return pltpu.make_async_remote_copy(
                src_ref=rows(*block) if src is None else src, dst_ref=rows(*block),
                send_sem=send_sems.at[k], recv_sem=recv_sems.at[k], device_id=to, device_id_type=MESH)

        mine = pltpu.make_async_copy(x_ref, rows(*me), local_sem)
        mine.start()
        first = [copy(0, me, sibling, src=x_ref)]
        first += [copy(1 + j, me, (*chip, c), src=x_ref) for j, chip in enumerate(chips)]
        for cp in first:
            cp.start()
        passed = [copy(4 + j, (*chip, c), sibling) for j, chip in enumerate(chips)]
        for j, chip in enumerate(chips):
            copy(1 + j, (*chip, c), me).wait_recv()
            passed[j].start()
        copy(0, sibling, me).wait_recv()
        for j, chip in enumerate(chips):
            copy(4 + j, (*chip, 1 - c), me).wait_recv()
        for cp in first + passed:
            cp.wait_send()
        mine.wait()

    return pl.pallas_call(
        body, name=name, in_specs=[ANY], out_specs=ANY,
        out_shape=jax.ShapeDtypeStruct((N_DEV,) + xs.shape, xs.dtype),
        scratch_shapes=[pltpu.SemaphoreType.DMA((7,)), pltpu.SemaphoreType.DMA((7,)), pltpu.SemaphoreType.DMA],
    )(xs)


HBM = pl.BlockSpec(memory_space=pltpu.HBM)
SEM = pl.BlockSpec(memory_space=pltpu.SEMAPHORE)
EFFECT = pltpu.SideEffectType.DATAFLOW_SIDE_EFFECTING


def _in_hbm(a):
    return pltpu.with_memory_space_constraint(a, pltpu.HBM)


def _local_copy(src, dst, stage, sem):
    load = pltpu.make_async_copy(src, stage, sem)
    load.start()
    load.wait()
    store = pltpu.make_async_copy(stage, dst, sem)
    store.start()
    store.wait()


def _unique(windows):
    arrays = []
    for per_chip in windows:
        for arr, _ in per_chip:
            if not any(arr is a for a in arrays):
                arrays.append(arr)
    return arrays


def _exchange_start(layer, windows, lands, name):
    arrays = _unique(windows)
    na, nt = len(arrays), len(windows)

    def body(*refs):
        in_refs, land_refs = refs[:na], refs[na:na + nt]
        send_sems, recv_sems = refs[na + nt], refs[na + nt + 1]
        token = refs[-1]
        x, y, c = _place()
        me = 4 * x + 2 * y + c
        for t in range(nt):
            for j in range(N_CHIP):
                arr, window = windows[t][j]
                src = window(in_refs[next(i for i, a in enumerate(arrays) if a is arr)])

                @pl.when(me != 2 * j + layer)
                def _():
                    pltpu.make_async_remote_copy(
                        src_ref=src, dst_ref=land_refs[t].at[me], send_sem=send_sems.at[N_CHIP * t + j],
                        recv_sem=recv_sems.at[N_DEV * t + me], device_id=(j // 2, j % 2, layer),
                        device_id_type=MESH).start()
        token[...] = jnp.zeros_like(token)

    outs = pl.pallas_call(
        body, name=name,
        out_shape=(pltpu.SemaphoreType.DMA((N_CHIP * nt,)), pltpu.SemaphoreType.DMA((N_DEV * nt,)),
                   *[pltpu.HBM(a.shape, a.dtype) for a in lands], jax.ShapeDtypeStruct((8, 128), F32)),
        in_specs=[HBM] * (na + nt),
        out_specs=(SEM, SEM, *[HBM] * nt, pl.BlockSpec(memory_space=pltpu.VMEM)),
        input_output_aliases={na + t: 2 + t for t in range(nt)},
        compiler_params=pltpu.CompilerParams(has_side_effects=EFFECT, vmem_limit_bytes=VMEM_LIMIT),
    )(*[_in_hbm(a) for a in arrays], *[_in_hbm(a) for a in lands])
    return outs[0], outs[1], list(outs[2:2 + nt]), outs[-1]


def _exchange_wait(layer, windows, lands, send_sems, recv_sems, after, name):
    arrays = _unique(windows)
    na, nt = len(arrays), len(windows)

    def body(*refs):
        in_refs, land_refs = refs[:na], refs[na:na + nt]
        send_sems, recv_sems = refs[na + nt], refs[na + nt + 1]
        stages, local_sem = refs[-1 - nt:-1], refs[-1]
        x, y, c = _place()
        me = 4 * x + 2 * y + c

        def source(t, j):
            arr, window = windows[t][j]
            return window(in_refs[next(i for i, a in enumerate(arrays) if a is arr)])

        @pl.when(c == layer)
        def _():
            for t in range(nt):
                for j in range(N_CHIP):
                    @pl.when(me == 2 * j + layer)
                    def _():
                        _local_copy(source(t, j), land_refs[t].at[me], stages[t], local_sem)

        for t in range(nt):
            for j in range(N_CHIP):
                @pl.when(me != 2 * j + layer)
                def _():
                    pltpu.make_async_remote_copy(
                        src_ref=source(t, j), dst_ref=land_refs[t].at[me], send_sem=send_sems.at[N_CHIP * t + j],
                        recv_sem=recv_sems.at[N_DEV * t + me], device_id=(j // 2, j % 2, layer),
                        device_id_type=MESH).wait_send()

        @pl.when(c == layer)
        def _():
            for t in range(nt):
                for s in range(N_DEV):
                    @pl.when(me != s)
                    def _():
                        slot = land_refs[t].at[s]
                        pltpu.make_async_remote_copy(
                            src_ref=slot, dst_ref=slot, send_sem=send_sems.at[N_CHIP * t],
                            recv_sem=recv_sems.at[N_DEV * t + s], device_id=(x, y, c),
                            device_id_type=MESH).wait_recv()

    outs = pl.pallas_call(
        body, name=name,
        out_shape=tuple(pltpu.HBM(a.shape, a.dtype) for a in lands),
        in_specs=[HBM] * (na + nt) + [SEM, SEM, ANY],
        out_specs=tuple([HBM] * nt),
        input_output_aliases={na + t: t for t in range(nt)},
        scratch_shapes=[pltpu.VMEM(a.shape[1:], a.dtype) for a in lands] + [pltpu.SemaphoreType.DMA],
        compiler_params=pltpu.CompilerParams(has_side_effects=EFFECT, vmem_limit_bytes=VMEM_LIMIT),
    )(*[_in_hbm(a) for a in arrays], *lands, send_sems, recv_sems, after)
    return list(outs)


def _other_chips(x, y):
    return [(1 - x, y), (x, 1 - y), (1 - x, 1 - y)]


def _split_rows(rows):
    if rows < 32:
        return [(0, rows), (rows, 0)]
    cut = -(-(rows // 2) // 16) * 16
    return [(0, cut), (cut, rows - cut)]


def _gather_copies(in_refs, land_refs, send_sems, recv_sems, x, y, sender_core, both):
    copies = []
    for t, src in enumerate(in_refs):
        r0, n = _split_rows(src.shape[0])[sender_core]
        if n == 0:
            continue
        for rel, (tx, ty) in enumerate(_other_chips(x, y)):
            for tc in (0, 1) if both else (sender_core,):
                copies.append(dict(
                    src_ref=src.at[pl.ds(r0, n)], dst_ref=land_refs[t].at[2 * x + y].at[pl.ds(r0, n)],
                    send_sem=send_sems.at[6 * t + 2 * rel + tc],
                    recv_sem=recv_sems.at[2 * (3 * t + rel) + sender_core],
                    device_id=(tx, ty, tc), device_id_type=MESH))
    return copies


def _gather_start(shards, lands, name, both):
    nt = len(shards)

    def body(*refs):
        in_refs, land_refs = refs[:nt], refs[nt:2 * nt]
        send_sems, recv_sems = refs[2 * nt], refs[2 * nt + 1]
        token = refs[-1]
        x, y, c = _place()
        for core in range(2):
            @pl.when(c == core)
            def _():
                for cp_args in _gather_copies(in_refs, land_refs, send_sems, recv_sems, x, y, core, both):
                    pltpu.make_async_remote_copy(**cp_args).start()
        token[...] = jnp.zeros_like(token)

    outs = pl.pallas_call(
        body, name=name,
        out_shape=(pltpu.SemaphoreType.DMA((6 * nt,)), pltpu.SemaphoreType.DMA((6 * nt,)),
                   *[pltpu.HBM(a.shape, a.dtype) for a in lands], jax.ShapeDtypeStruct((8, 128), F32)),
        in_specs=[HBM] * (2 * nt),
        out_specs=(SEM, SEM, *[HBM] * nt, pl.BlockSpec(memory_space=pltpu.VMEM)),
        input_output_aliases={nt + t: 2 + t for t in range(nt)},
        compiler_params=pltpu.CompilerParams(has_side_effects=EFFECT, vmem_limit_bytes=VMEM_LIMIT),
    )(*[_in_hbm(a) for a in shards], *[_in_hbm(a) for a in lands])
    return outs[0], outs[1], list(outs[2:2 + nt]), outs[-1]


def _gather_wait(shards, lands, send_sems, recv_sems, after, name, both):
    nt = len(shards)

    def body(*refs):
        in_refs, land_refs = refs[:nt], refs[nt:2 * nt]
        send_sems, recv_sems = refs[2 * nt], refs[2 * nt + 1]
        stages, local_sem = refs[-1 - nt:-1], refs[-1]
        x, y, c = _place()

        for t in range(nt):
            _local_copy(in_refs[t], land_refs[t].at[2 * x + y], stages[t], local_sem)

        for core in range(2):
            @pl.when(c == core)
            def _():
                for cp_args in _gather_copies(in_refs, land_refs, send_sems, recv_sems, x, y, core, both):
                    pltpu.make_async_remote_copy(**cp_args).wait_send()

        def wait_parts_from(core):
            for t in range(nt):
                r0, n = _split_rows(in_refs[t].shape[0])[core]
                for rel, (tx, ty) in enumerate(_other_chips(x, y)):
                    if n > 0:
                        part = land_refs[t].at[2 * tx + ty].at[pl.ds(r0, n)]
                        pltpu.make_async_remote_copy(
                            src_ref=part, dst_ref=part, send_sem=send_sems.at[6 * t],
                            recv_sem=recv_sems.at[2 * (3 * t + rel) + core], device_id=(x, y, c),
                            device_id_type=MESH).wait_recv()

        for core in range(2):
            if both:
                wait_parts_from(core)
            else:
                pl.when(c == core)(functools.partial(wait_parts_from, core))

    outs = pl.pallas_call(
        body, name=name,
        out_shape=tuple(pltpu.HBM(a.shape, a.dtype) for a in lands),
        in_specs=[HBM] * (2 * nt) + [SEM, SEM] + [ANY] * len(after),
        out_specs=tuple([HBM] * nt),
        input_output_aliases={nt + t: t for t in range(nt)},
        scratch_shapes=[pltpu.VMEM(a.shape, a.dtype) for a in shards] + [pltpu.SemaphoreType.DMA],
        compiler_params=pltpu.CompilerParams(has_side_effects=EFFECT, vmem_limit_bytes=VMEM_LIMIT),
    )(*[_in_hbm(a) for a in shards], *lands, send_sems, recv_sems, *after)
    return list(outs)


def _pair_share(lands, name):
    nt = len(lands)
    splits = [_split_rows(a.shape[1]) for a in lands]

    def body(*refs):
        land_refs = refs[:nt]
        stages = refs[2 * nt:3 * nt]
        load_sems, send_sems, recv_sems = refs[3 * nt:]
        x, y, c = _place()

        def parts(core):
            out = []
            for t in range(nt):
                r0, n = splits[t][core]
                for rel, (tx, ty) in enumerate(_other_chips(x, y)):
                    if n > 0:
                        out.append((3 * t + rel, stages[t].at[rel, pl.ds(0, n)],
                                    land_refs[t].at[2 * tx + ty].at[pl.ds(r0, n)]))
            return out

        def send(core):
            loads = [pltpu.make_async_copy(part, stage, load_sems.at[k]) for k, stage, part in parts(core)]
            for cp in loads:
                cp.start()
            pushes = []
            for cp, (k, stage, part) in zip(loads, parts(core)):
                cp.wait()
                push = pltpu.make_async_remote_copy(src_ref=stage, dst_ref=part, send_sem=send_sems.at[k],
                                                    recv_sem=recv_sems.at[k], device_id=(x, y, 1 - c),
                                                    device_id_type=MESH)
                push.start()
                pushes.append(push)
            for push in pushes:
                push.wait_send()
            for k, _, part in parts(1 - core):
                pltpu.make_async_remote_copy(src_ref=part, dst_ref=part, send_sem=send_sems.at[k],
                                             recv_sem=recv_sems.at[k], device_id=(x, y, 1 - c),
                                             device_id_type=MESH).wait_recv()

        for core in range(2):
            pl.when(c == core)(functools.partial(send, core))

    outs = pl.pallas_call(
        body, name=name, in_specs=[ANY] * nt, out_specs=[ANY] * nt,
        out_shape=[jax.ShapeDtypeStruct(a.shape, a.dtype) for a in lands],
        input_output_aliases={t: t for t in range(nt)},
        scratch_shapes=[pltpu.VMEM((3, max(n for _, n in sp), a.shape[2]), a.dtype) for a, sp in zip(lands, splits)]
        + [pltpu.SemaphoreType.DMA((3 * nt,))] * 3,
        compiler_params=pltpu.CompilerParams(vmem_limit_bytes=VMEM_LIMIT),
    )(*lands)
    return list(outs)


def _tie(a, token):
    def body(a_ref, token_ref, o_ref):
        pass

    return pl.pallas_call(
        body, name="tie", in_specs=[ANY, ANY], out_specs=ANY,
        out_shape=jax.ShapeDtypeStruct(a.shape, a.dtype), input_output_aliases={0: 0},
    )(a, token)


def _sum_share(xs, name):
    _, r, cols = xs.shape
    tr = max(t for t in range(16, min(r, 704) + 1, 16) if r % t == 0)
    nblk = r // tr

    def body(x_ref, out_ref, acc_ref, send_sems, local_sems, recv_sem):
        i = pl.program_id(0)
        slot = i % 2
        x, y, c = _place()

        def copies(s, blk):
            dst = out_ref.at[c, pl.ds(blk * tr, tr), :]
            loc = pltpu.make_async_copy(acc_ref.at[s], dst, local_sems.at[s])
            rem = pltpu.make_async_remote_copy(src_ref=acc_ref.at[s], dst_ref=dst, send_sem=send_sems.at[s],
                                               recv_sem=recv_sem, device_id=(x, y, 1 - c), device_id_type=MESH)
            return loc, rem

        @pl.when(i >= 2)
        def _():
            loc, rem = copies(slot, i - 2)
            loc.wait()
            rem.wait_send()

        acc = x_ref[0].astype(F32)
        for k in range(1, N_DEV):
            acc = acc + x_ref[k].astype(F32)
        acc_ref[slot] = acc
        loc, rem = copies(slot, i)
        loc.start()
        rem.start()

        @pl.when(i == nblk - 1)
        def _():
            for back in range(min(2, nblk)):
                blk = nblk - 1 - back
                loc, rem = copies(blk % 2, blk)
                loc.wait()
                rem.wait_send()
            theirs = out_ref.at[1 - c]
            pltpu.make_async_remote_copy(src_ref=theirs, dst_ref=theirs, send_sem=send_sems.at[0],
                                         recv_sem=recv_sem, device_id=(x, y, 1 - c),
                                         device_id_type=MESH).wait_recv()

    return pl.pallas_call(
        body, name=name, grid=(nblk,),
        in_specs=[pl.BlockSpec((N_DEV, tr, cols), lambda i: (0, i, 0))],
        out_specs=ANY,
        out_shape=jax.ShapeDtypeStruct((2, r, cols), F32),
        scratch_shapes=[pltpu.VMEM((2, tr, cols), F32), pltpu.SemaphoreType.DMA((2,)),
                        pltpu.SemaphoreType.DMA((2,)), pltpu.SemaphoreType.DMA],
        compiler_params=_params("arbitrary"),
    )(xs)


def _sum8(xs, name):
    _, r, cols = xs.shape
    tr = 8
    for cand in (256, 128, 64, 32, 16):
        if r % cand == 0:
            tr = cand
            break

    def body(x_ref, o_ref):
        acc = x_ref[0].astype(F32)
        for k in range(1, N_DEV):
            acc = acc + x_ref[k].astype(F32)
        o_ref[...] = acc

    return pl.pallas_call(
        body, name=name, grid=(r // tr,),
        in_specs=[pl.BlockSpec((N_DEV, tr, cols), lambda i: (0, i, 0))],
        out_specs=pl.BlockSpec((tr, cols), lambda i: (i, 0)),
        out_shape=jax.ShapeDtypeStruct((r, cols), F32),
        compiler_params=_params("parallel"),
    )(xs)


def _adamw(w, g, m, v, name):
    R, C = w.shape
    tr = max([t for t in range(8, min(R, 512) + 1, 8) if R % t == 0] or [R])
    c1 = 1.0 / (1.0 - ADAM_B1 ** ADAM_STEP)
    c2 = 1.0 / (1.0 - ADAM_B2 ** ADAM_STEP)

    def body(w_ref, g_ref, m_ref, v_ref, d_ref, nm_ref, nv_ref):
        gv = g_ref[...]
        nm = ADAM_B1 * m_ref[...] + (1.0 - ADAM_B1) * gv
        nv = ADAM_B2 * v_ref[...] + (1.0 - ADAM_B2) * (gv * gv)
        d_ref[...] = (-ADAM_LR) * ((nm * c1) / (jnp.sqrt(nv * c2) + ADAM_EPS) + ADAM_WD * w_ref[...])
        nm_ref[...] = nm
        nv_ref[...] = nv

    spec = pl.BlockSpec((tr, C), lambda i: (i, 0))
    shape = jax.ShapeDtypeStruct((R, C), F32)
    return pl.pallas_call(
        body, name=name, grid=(R // tr,), in_specs=[spec] * 4, out_specs=[spec] * 3, out_shape=[shape] * 3,
        compiler_params=_params("parallel"),
    )(w, g, m, v)


def _flat_rows(parts, rows):
    flat = jnp.concatenate([q.reshape(-1) for q in parts])
    flat = jnp.pad(flat, (0, rows * LANES - flat.shape[0]))
    return flat.reshape(rows, LANES)


def _round_up(n, m):
    return (n + m - 1) // m * m


def _block_diag(w):
    H, n, _ = w.shape
    eye = jnp.eye(H, dtype=w.dtype)
    return (eye[:, None, :, None] * w[:, :, None, :]).reshape(H * n, H * n)


def _diag_blocks(w, H, n):
    w4 = w.reshape(H, n, H, n)
    return jnp.stack([w4[h, :, h, :] for h in range(H)])


W_IN_T = ("ffn1_w_in", "w_in", "ffn2_w_in")


def _ffn_in_weights(g_in):
    zeros = jnp.zeros((FF_HALF - FF_SHARD, D), g_in.dtype)
    wg_t = jnp.concatenate([g_in[0], zeros, g_in[1], zeros], axis=0)
    wu_t = jnp.concatenate([g_in[2], zeros, g_in[3], zeros], axis=0)
    return wg_t, wu_t


def _ffn_out_weights(g_out):
    zeros = jnp.zeros((FF_HALF - FF_SHARD, D), g_out.dtype)
    return jnp.concatenate([g_out[0], g_out[1], zeros, g_out[2], g_out[3], zeros], axis=0)


LAND_SHAPES = {"ffn1_w_in": (FF_SHARD, D), "ffn1_w_out": (FF_ROWS, D), "w_in": (D_IN // N_CHIP, D),
               "w_out": (D // N_CHIP, D), "ffn2_w_in": (FF_SHARD, D), "ffn2_w_out": (FF_ROWS, D)}


def _rows_window(arr, start, size):
    return arr, lambda r: r.at[pl.ds(start, size), :]


def _w_in_grad_windows(dwg_t, dwu_t):
    return [_rows_window(dwg_t if j < 2 else dwu_t, (j % 2) * FF_HALF, FF_SHARD) for j in range(N_CHIP)]


def _w_out_grad_windows(dwout):
    return [_rows_window(dwout, (j // 2) * FF_HALF + (j % 2) * FF_ROWS, FF_ROWS) for j in range(N_CHIP)]


def _mix_grad_windows(dwin_t, dwo):
    win = [_rows_window(dwin_t, j * (D_IN // N_CHIP), D_IN // N_CHIP) for j in range(N_CHIP)]
    wo = [_rows_window(dwo, j * (D // N_CHIP), D // N_CHIP) for j in range(N_CHIP)]
    return win, wo


def kernel(x, ffn1_norm, ffn1_w_in, ffn1_w_out, mix_norm, w_in, conv_w, conv_b, rg_w_a, rg_b_a, rg_w_x, rg_b_x, lru_lambda, pool_w, pool_scale, sgu_norm, sgu_w, sgu_b, w_out, ffn2_norm, ffn2_w_in, ffn2_w_out, final_norm, loss_target, m_ffn1_norm, m_ffn1_w_in, m_ffn1_w_out, m_mix_norm, m_w_in, m_conv_w, m_conv_b, m_rg_w_a, m_rg_b_a, m_rg_w_x, m_rg_b_x, m_lru_lambda, m_pool_w, m_pool_scale, m_sgu_norm, m_sgu_w, m_sgu_b, m_w_out, m_ffn2_norm, m_ffn2_w_in, m_ffn2_w_out, m_final_norm, v_ffn1_norm, v_ffn1_w_in, v_ffn1_w_out, v_mix_norm, v_w_in, v_conv_w, v_conv_b, v_rg_w_a, v_rg_b_a, v_rg_w_x, v_rg_b_x, v_lru_lambda, v_pool_w, v_pool_scale, v_sgu_norm, v_sgu_w, v_sgu_b, v_w_out, v_ffn2_norm, v_ffn2_w_in, v_ffn2_w_out, v_final_norm):
    args = locals()
    W = {n: args[n] for n in WEIGHTS}
    M = {n: args["m_" + n] for n in WEIGHTS}
    V = {n: args["v_" + n] for n in WEIGHTS}
    depth = ffn1_norm.shape[0]
    T = x.shape[1]
    xi, yi, ci = _place()
    chip = 2 * xi + yi

    assert depth == 2, "core c of a chip reduces layer c"
    groups = [(l, names) for l in range(depth)
              for names in (["ffn1_w_in"], ["ffn1_w_out", "w_in", "w_out"], ["ffn2_w_in", "ffn2_w_out"])]
    def stored(a, n):
        return jnp.swapaxes(a, 1, 2) if n in W_IN_T else a

    wb = {n: stored(W[n], n).astype(BF16) for n in BIG}
    groups[1][1].append("conv_w")
    conv_shard = conv_w.reshape(-1, conv_w.shape[-1])
    flights = {}

    def weights_start(k, dep=None):
        l, names = groups[k]
        shards = [conv_shard if n == "conv_w" else wb[n][l] for n in names]
        if dep is not None:
            shards[0] = _tie(shards[0], dep)
        lands = [lax.empty((N_CHIP,) + s.shape, s.dtype) for s in shards]
        send, recv, lands, token = _gather_start(shards, lands, "weights_start_%d" % k, k >= 2)
        flights[k] = (shards, lands, send, recv)
        return token

    def weights_wait(k, after):
        l, names = groups[k]
        shards, lands, send, recv = flights[k]
        got = _gather_wait(shards, lands, send, recv, after, "weights_wait_%d" % k, k >= 2)
        token = weights_start(k + 1, got[0]) if k + 1 < len(groups) else None
        if k < 2:
            got = _pair_share(got, "weights_share_%d" % k)
        return dict(zip(names, got)), token

    def after_start(a, token):
        return a if token is None else _tie(a, token)

    first_tokens = [weights_start(0)]

    layers = []
    for l in range(depth):
        L = {f: dict(norm=W[f + "_norm"][l][None]) for f in ("ffn1", "ffn2")}
        ws = jnp.where(jnp.tril(jnp.ones((CHUNK, CHUNK), bool))[None], sgu_w[l], 0.0)
        wax = jnp.concatenate([_block_diag(rg_w_a[l]), _block_diag(rg_w_x[l])], axis=1)
        wpool = _block_diag(pool_w[l])
        L["mix"] = dict(
            conv_b=conv_b[l][None], wax=wax.astype(BF16), wax_t=wax.T.astype(BF16),
            bax=jnp.concatenate([rg_b_a[l].reshape(-1), rg_b_x[l].reshape(-1)])[None], lam=lru_lambda[l][None],
            wpool=wpool.astype(BF16), wpool_t=wpool.T.astype(BF16), pool_scale=pool_scale[l][None],
            sgu_norm=sgu_norm[l][None], ws=ws.astype(BF16), ws_t=jnp.swapaxes(ws, 1, 2).astype(BF16),
            bz=jnp.repeat(sgu_b[l].T, 64, axis=1))
        L["mix_norm"] = mix_norm[l][None]
        layers.append(L)
    for token in first_tokens:
        layers[0]["ffn1"]["norm"] = _tie(layers[0]["ffn1"]["norm"], token)

    xs = x[0]
    saved = []
    for l, L in enumerate(layers):
        F1, F2 = L["ffn1"], L["ffn2"]
        got, token = weights_wait(3 * l, [xs] + ([wb[n] for n in BIG] if l == 0 else []))
        F1["wg"], F1["wu"] = _ffn_in_weights(got["ffn1_w_in"])
        F1["norm"] = after_start(F1["norm"], token)
        h, da_dg, da_du, a = _ffn_in(xs, F1["norm"], F1["wg"], F1["wu"])
        got, token = weights_wait(3 * l + 1, [a])
        F1["wout"] = _ffn_out_weights(got["ffn1_w_out"])
        L["w_in"] = got["w_in"].reshape(D_IN, D)
        L["w_out"] = got["w_out"].reshape(D, D)
        if "conv_w" in got:
            conv_full = jnp.concatenate([got["conv_w"][j] for j in range(N_CHIP)], axis=1)
            for ll in range(depth):
                layers[ll]["mix"]["conv_w"] = conv_full.reshape(depth, 4, D_RNN)[ll]
        a = after_start(a, token)
        x1 = _mm_res(a, F1["wout"], xs, 0.5, "ffn_out", tm=512, tn=D)
        s1 = (xs, h, da_dg, da_du, a)
        hm, p = _mix_in(x1, L["mix_norm"], L["w_in"])
        ycat, hs = _mix_fwd(p, L["mix"])
        x2 = _mm_res(ycat, L["w_out"], x1, 1.0, "mix_out", tm=512, tn=D)
        got, token = weights_wait(3 * l + 2, [x2])
        F2["wg"], F2["wu"] = _ffn_in_weights(got["ffn2_w_in"])
        F2["wout"] = _ffn_out_weights(got["ffn2_w_out"])
        F2["norm"] = after_start(F2["norm"], token)
        s2 = (x2,) + tuple(_ffn_in(x2, F2["norm"], F2["wg"], F2["wu"]))
        x3 = _mm_res(s2[-1], F2["wout"], x2, 0.5, "ffn_out", tm=512, tn=D)
        saved.append((s1, (x1, hm, p, ycat, hs), s2))
        xs = x3

    dx, dxb, d_final, loss_part = _final(xs, loss_target[0], final_norm[None], 0.5)

    G = {n: [None] * depth for n in SMALL if n != "final_norm"}
    lands = {n: lax.empty((N_DEV,) + LAND_SHAPES[n], BF16) for n in BIG}
    in_flight = []

    def send_grads(l, windows, tag):
        names = list(windows)
        send, recv, thru, token = _exchange_start(l, [windows[n] for n in names], [lands[n] for n in names],
                                                  "grads_start_" + tag)
        lands.update(zip(names, thru))
        in_flight.append((l, names, [windows[n] for n in names], send, recv, "grads_wait_" + tag))
        return token

    def ffn_bwd(dx, dxb, F, s, f, l, pending, send_now):
        xin, h, da_dg, da_du, a = s
        dwout = _mm_tn(a, dxb, 1.0, "ffn_dwout")
        if send_now:
            token = send_grads(l, {f + "_w_out": _w_out_grad_windows(dwout)}, "l%d_%s_out" % (l, f))
            dxb = _tie(dxb, token)
        else:
            pending[f + "_w_out"] = _w_out_grad_windows(dwout)
        dg, du = _ffn_mid_bwd(dxb, F["wout"], da_dg, da_du)
        dwg = _mm_tn(dg, h, 1.0, "ffn_dwg")
        dwu = _mm_tn(du, h, 1.0, "ffn_dwu")
        pending[f + "_w_in"] = _w_in_grad_windows(dwg, dwu)
        if send_now:
            dg = _tie(dg, send_grads(l, pending, "l%d_%s" % (l, f)))
        dx, dxb, dn = _dh_rms_bwd([(dg, F["wg"]), (du, F["wu"])], xin, F["norm"], dx,
                                  1.0 if f == "ffn2" else 0.5, "ffn_dh")
        G[f + "_norm"][l] = dn[0]
        return dx, dxb

    for l in reversed(range(depth)):
        L = layers[l]
        s1, (x1, hm, p, ycat, hs), s2 = saved[l]
        pending = {}
        dx, dxb = ffn_bwd(dx, dxb, L["ffn2"], s2, "ffn2", l, pending, l == 0)
        if l == 0:
            pending = {}
        dycat = _mm_nt(dxb, L["w_out"], "mix_dy", tm=512, tn=D)
        dwo = _mm_tn(ycat, dxb, 1.0, "mix_dwout")
        mg = _mix_bwd(dycat, p, hs, L["mix"])
        dwin = _mm_tn(mg["dp"], hm, 1.0, "mix_dwin")
        pending["w_in"], pending["w_out"] = _mix_grad_windows(dwin, dwo)
        if l == 0:
            dp = _tie(mg["dp"], send_grads(l, pending, "l0_mix"))
            pending = {}
        else:
            dp = mg["dp"]
        dx, dxb, dn = _dh_rms_bwd([(dp, L["w_in"])], x1, L["mix_norm"], dx, 0.5, "mix_dh")
        G["mix_norm"][l] = dn[0]
        G["conv_w"][l], G["conv_b"][l] = mg["conv_w"], mg["conv_b"][0]
        G["rg_w_a"][l] = _diag_blocks(mg["wax"][:, :D_RNN], 8, 64)
        G["rg_w_x"][l] = _diag_blocks(mg["wax"][:, D_RNN:], 8, 64)
        G["rg_b_a"][l] = mg["bax"][0, :D_RNN].reshape(8, 64)
        G["rg_b_x"][l] = mg["bax"][0, D_RNN:].reshape(8, 64)
        G["lru_lambda"][l] = mg["lam"][0]
        G["pool_w"][l] = _diag_blocks(mg["wpool"], 4, 64)
        G["pool_scale"][l], G["sgu_norm"][l] = mg["pool_scale"][0], mg["sgu_norm"][0]
        G["sgu_w"][l] = mg["ws"]
        G["sgu_b"][l] = mg["bz"].reshape(CHUNK, 4, 64).sum(-1).T
        dx, dxb = ffn_bwd(dx, dxb, L["ffn1"], s1, "ffn1", l, pending, l == 0)
        if l > 0:
            dxb = _tie(dxb, send_grads(l, pending, "l%d" % l))
    grad_x = dx[None]
    G = {n: jnp.stack(v) for n, v in G.items()}
    G["final_norm"] = d_final[0]

    for l, names, windows, send, recv, tag in in_flight:
        lands.update(zip(names, _exchange_wait(l, windows, [lands[n] for n in names], send, recv, dx, tag)))
    both = [_sum_share(lands[n], "sum_share_" + n) for n in BIG]
    grads = dict(zip(BIG, both))

    small_sizes = [int(np.prod(G[n].shape)) for n in SMALL]
    srows = _round_up(sum(small_sizes) + 1, N_DEV * 8 * LANES) // (N_DEV * LANES)
    sflat = _flat_rows([G[n] for n in SMALL] + [loss_part[0, :1]], N_DEV * srows)
    sgot = _all_to_all(sflat.reshape(N_DEV, srows, LANES), "exchange_small_grads")
    sall = _all_gather8(_sum8(sgot, "sum_small_grads"), "share_small_grads").reshape(-1)
    off = 0
    for n, size in zip(SMALL, small_sizes):
        grads[n] = sall[off:off + size].reshape(G[n].shape)
        off += size
    loss = sall[off]
    grads["conv_w"] = lax.dynamic_slice_in_dim(grads["conv_w"], chip * conv_w.shape[2], conv_w.shape[2], axis=2)

    delta, new_m, new_v = {}, {}, {}
    for n in BIG:
        shp = grads[n].shape
        two_d = (shp[0] * shp[1], shp[2])
        outs = _adamw(stored(W[n], n).reshape(two_d), grads[n].reshape(two_d), stored(M[n], n).reshape(two_d),
                      stored(V[n], n).reshape(two_d), "adamw_" + n)
        delta[n], new_m[n], new_v[n] = (stored(o.reshape(shp), n) for o in outs)
        grads[n] = stored(grads[n], n)
    arows = _round_up(sum(int(np.prod(W[n].shape)) for n in SMALL), 8 * LANES) // LANES
    outs = _adamw(*(_flat_rows([src[n] for n in SMALL], arows) for src in (W, grads, M, V)), "adamw_small")
    outs = [o.reshape(-1) for o in outs]
    off = 0
    for n in SMALL:
        size = int(np.prod(W[n].shape))
        delta[n], new_m[n], new_v[n] = (o[off:off + size].reshape(W[n].shape) for o in outs)
        off += size

    return (loss, grad_x, *[grads[n] for n in WEIGHTS], *[delta[n] for n in WEIGHTS],
            *[new_m[n] for n in WEIGHTS], *[new_v[n] for n in WEIGHTS])
```

```python
import functools
import math

import jax
import jax.numpy as jnp
import numpy as np
from jax import lax
from jax.experimental import pallas as pl
from jax.experimental.pallas import tpu as pltpu

F32 = jnp.float32
BF16 = jnp.bfloat16
MESH = pl.DeviceIdType.MESH

D = 1024
D_RNN = 512
D_POOL = 256
D_SGU = 256
D_IN = 1792
D_FF = 2752
D_FFP = 2816
N_CHIP = 4
FF_SHARD = D_FF // 2
FF_HALF = D_FFP // 2
FF_ROWS = D_FF // N_CHIP
CHUNK = 128
HALO = 16
EPS = 1e-6
LRU_C = 8.0
N_DEV = 8
LANES = 1024
VMEM_LIMIT = 56 * 1024 * 1024

ADAM_LR, ADAM_B1, ADAM_B2, ADAM_EPS, ADAM_WD, ADAM_STEP = 0.001, 0.9, 0.999, 1e-08, 0.01, 10

BIG = ("ffn1_w_in", "ffn1_w_out", "w_in", "w_out", "ffn2_w_in", "ffn2_w_out")
SMALL = ("ffn1_norm", "mix_norm", "conv_w", "conv_b", "rg_w_a", "rg_b_a", "rg_w_x", "rg_b_x", "lru_lambda",
         "pool_w", "pool_scale", "sgu_norm", "sgu_w", "sgu_b", "ffn2_norm", "final_norm")
WEIGHTS = ("ffn1_norm", "ffn1_w_in", "ffn1_w_out", "mix_norm", "w_in", "conv_w", "conv_b", "rg_w_a", "rg_b_a",
           "rg_w_x", "rg_b_x", "lru_lambda", "pool_w", "pool_scale", "sgu_norm", "sgu_w", "sgu_b", "w_out",
           "ffn2_norm", "ffn2_w_in", "ffn2_w_out", "final_norm")


def _params(*sem):
    return pltpu.CompilerParams(dimension_semantics=sem, vmem_limit_bytes=VMEM_LIMIT)


def _gelu(x):
    c = math.sqrt(2.0 / math.pi)
    t = jnp.tanh(c * (x + 0.044715 * (x * x * x)))
    return 0.5 * x * (1.0 + t)


def _gelu_and_grad(x):
    c = math.sqrt(2.0 / math.pi)
    x2 = x * x
    t = jnp.tanh(c * (x + 0.044715 * (x2 * x)))
    g = 0.5 * x * (1.0 + t)
    dg = 0.5 * (1.0 + t) + 0.5 * x * (1.0 - t * t) * (c * (1.0 + 3.0 * 0.044715 * x2))
    return g, dg


def _sigmoid(x):
    return 0.5 * jnp.tanh(0.5 * x) + 0.5


def _dot(a, b):
    return jnp.dot(a, b, preferred_element_type=F32)


def _dot_tn(a, b):
    return lax.dot_general(a, b, (((0,), (0,)), ((), ())), preferred_element_type=F32)


def _dot_nt(a, b):
    return lax.dot_general(a, b, (((1,), (1,)), ((), ())), preferred_element_type=F32)


def _tile(n, limit):
    best = 128
    for t in range(128, min(n, limit) + 1, 128):
        if n % t == 0:
            best = t
    assert n % best == 0, (n, limit)
    return best


def _mm_res(a, b, res, scale, name, tm=1024, tn=512):
    M, K = a.shape
    N = b.shape[1]
    tm, tn = min(tm, M), _tile(N, tn)

    def body(a_ref, b_ref, r_ref, o_ref):
        o_ref[...] = r_ref[...] + scale * _dot(a_ref[...], b_ref[...])

    return pl.pallas_call(
        body, name=name, grid=(M // tm, N // tn),
        in_specs=[pl.BlockSpec((tm, K), lambda i, j: (i, 0)), pl.BlockSpec((K, tn), lambda i, j: (0, j)),
                  pl.BlockSpec((tm, tn), lambda i, j: (i, j))],
        out_specs=pl.BlockSpec((tm, tn), lambda i, j: (i, j)),
        out_shape=jax.ShapeDtypeStruct((M, N), F32),
        compiler_params=_params("parallel", "parallel"),
    )(a, b, res)


def _mm_nt(a, b, name, tm=1024, tn=512):
    M, K = a.shape
    N = b.shape[0]
    tm, tn = min(tm, M), _tile(N, tn)

    def body(a_ref, b_ref, o_ref):
        o_ref[...] = _dot_nt(a_ref[...], b_ref[...])

    return pl.pallas_call(
        body, name=name, grid=(M // tm, N // tn),
        in_specs=[pl.BlockSpec((tm, K), lambda i, j: (i, 0)), pl.BlockSpec((tn, K), lambda i, j: (j, 0))],
        out_specs=pl.BlockSpec((tm, tn), lambda i, j: (i, j)),
        out_shape=jax.ShapeDtypeStruct((M, N), F32),
        compiler_params=_params("parallel", "parallel"),
    )(a, b)


def _dh_rms_bwd(pairs, x, g, dres, copy_scale, name, tm=512):
    T = x.shape[0]
    tm = min(tm, T)
    n = len(pairs)

    def body(*refs):
        ab = refs[:2 * n]
        x_ref, g_ref, dres_ref, dx_ref, dxb_ref, dg_ref = refs[2 * n:]
        dy = _dot(ab[0][...], ab[1][...])
        for k in range(1, n):
            dy = dy + _dot(ab[2 * k][...], ab[2 * k + 1][...])
        xv = x_ref[...]
        r = lax.rsqrt(jnp.mean(xv * xv, axis=-1, keepdims=True) + EPS)
        xhat = xv * r
        dxhat = dy * g_ref[...]
        dx = dres_ref[...] + r * (dxhat - xhat * jnp.mean(dxhat * xhat, axis=-1, keepdims=True))
        dx_ref[...] = dx
        dxb_ref[...] = (copy_scale * dx).astype(BF16)

        @pl.when(pl.program_id(0) == 0)
        def _():
            dg_ref[...] = jnp.zeros_like(dg_ref)

        dg_ref[...] += jnp.sum(dy * xhat, axis=0, keepdims=True)

    row = pl.BlockSpec((tm, D), lambda i: (i, 0))
    vec = pl.BlockSpec((1, D), lambda i: (0, 0))
    in_specs, operands = [], []
    for a, b in pairs:
        in_specs += [pl.BlockSpec((tm, a.shape[1]), lambda i: (i, 0)),
                     pl.BlockSpec(b.shape, lambda i: (0, 0), pipeline_mode=pl.Buffered(1))]
        operands += [a, b]
    return pl.pallas_call(
        body, name=name, grid=(T // tm,),
        in_specs=in_specs + [row, vec, row], out_specs=[row, row, vec],
        out_shape=[jax.ShapeDtypeStruct((T, D), F32), jax.ShapeDtypeStruct((T, D), BF16),
                   jax.ShapeDtypeStruct((1, D), F32)],
        compiler_params=_params("arbitrary"),
    )(*operands, x, g, dres)


def _mm_tn(a, b, scale, name, tm=1792, tn=1792, tk=2048):
    T, M = a.shape
    N = b.shape[1]
    tm, tn, tk = _tile(M, tm), _tile(N, tn), min(tk, T)
    nk = T // tk

    def body(a_ref, b_ref, o_ref, acc_ref):
        k = pl.program_id(2)

        @pl.when(k == 0)
        def _():
            acc_ref[...] = jnp.zeros_like(acc_ref)

        acc_ref[...] += _dot_tn(a_ref[...], b_ref[...])

        @pl.when(k == nk - 1)
        def _():
            o_ref[...] = (scale * acc_ref[...]).astype(BF16)

    return pl.pallas_call(
        body, name=name, grid=(M // tm, N // tn, nk),
        in_specs=[pl.BlockSpec((tk, tm), lambda i, j, k: (k, i)), pl.BlockSpec((tk, tn), lambda i, j, k: (k, j))],
        out_specs=pl.BlockSpec((tm, tn), lambda i, j, k: (i, j)),
        out_shape=jax.ShapeDtypeStruct((M, N), BF16),
        scratch_shapes=[pltpu.VMEM((tm, tn), F32)],
        compiler_params=_params("parallel", "parallel", "arbitrary"),
    )(a, b)


def _rms_rows(x_ref, gain_ref):
    xv = x_ref[...]
    r = lax.rsqrt(jnp.mean(xv * xv, axis=-1, keepdims=True) + EPS)
    return (xv * r * gain_ref[...]).astype(BF16)


def _ffn_in(x, gain, wg_t, wu_t, tm=256, tn=D_FFP):
    T = x.shape[0]
    tm = min(tm, T)

    def body(x_ref, gain_ref, wg_ref, wu_ref, h_ref, dg_ref, du_ref, a_ref):
        hv = _rms_rows(x_ref, gain_ref)

        @pl.when(pl.program_id(0) == 0)
        def _():
            h_ref[...] = hv

        g = _dot_nt(hv, wg_ref[...])
        u = _dot_nt(hv, wu_ref[...])
        s = _sigmoid(g)
        gs = g * s
        dg_ref[...] = (u * (s + gs - gs * s)).astype(BF16)
        du_ref[...] = gs.astype(BF16)
        a_ref[...] = (gs * u).astype(BF16)

    row = pl.BlockSpec((tm, D), lambda j, i: (i, 0))
    last = T // tm - 1
    h_spec = pl.BlockSpec((tm, D), lambda j, i: (jnp.where(j == 0, i, last), 0))
    wspec = pl.BlockSpec((tn, D), lambda j, i: (j, 0))
    ospec = pl.BlockSpec((tm, tn), lambda j, i: (i, j))
    oshape = jax.ShapeDtypeStruct((T, D_FFP), BF16)
    return pl.pallas_call(
        body, name="ffn_in", grid=(D_FFP // tn, T // tm),
        in_specs=[row, pl.BlockSpec((1, D), lambda j, i: (0, 0)), wspec, wspec],
        out_specs=[h_spec, ospec, ospec, ospec],
        out_shape=[jax.ShapeDtypeStruct((T, D), BF16), oshape, oshape, oshape],
        compiler_params=_params("arbitrary", "arbitrary"),
    )(x, gain, wg_t, wu_t)


def _mix_in(x, gain, w_in_t, tm=512):
    T = x.shape[0]
    tm = min(tm, T)

    def body(x_ref, gain_ref, w_ref, h_ref, p_ref):
        hv = _rms_rows(x_ref, gain_ref)
        h_ref[...] = hv
        p_ref[...] = _dot_nt(hv, w_ref[...])

    row = pl.BlockSpec((tm, D), lambda i: (i, 0))
    return pl.pallas_call(
        body, name="mix_in", grid=(T // tm,),
        in_specs=[row, pl.BlockSpec((1, D), lambda i: (0, 0)), pl.BlockSpec((D_IN, D), lambda i: (0, 0))],
        out_specs=[row, pl.BlockSpec((tm, D_IN), lambda i: (i, 0))],
        out_shape=[jax.ShapeDtypeStruct((T, D), BF16), jax.ShapeDtypeStruct((T, D_IN), F32)],
        compiler_params=_params("parallel"),
    )(x, gain, w_in_t)


def _ffn_mid_bwd(dyh, wout, da_dg, da_du, tm=256, tn=D_FFP):
    T = dyh.shape[0]
    tm = min(tm, T)

    def body(dy_ref, w_ref, pg_ref, pu_ref, dg_ref, du_ref):
        da = _dot_nt(dy_ref[...], w_ref[...])
        dg_ref[...] = (da * pg_ref[...].astype(F32)).astype(BF16)
        du_ref[...] = (da * pu_ref[...].astype(F32)).astype(BF16)

    ospec = pl.BlockSpec((tm, tn), lambda j, i: (i, j))
    oshape = jax.ShapeDtypeStruct((T, D_FFP), BF16)
    return pl.pallas_call(
        body, name="ffn_mid_bwd", grid=(D_FFP // tn, T // tm),
        in_specs=[pl.BlockSpec((tm, D), lambda j, i: (i, 0)), pl.BlockSpec((tn, D), lambda j, i: (j, 0)),
                  ospec, ospec],
        out_specs=[ospec, ospec], out_shape=[oshape, oshape],
        compiler_params=_params("parallel", "parallel"),
    )(dyh, wout, da_dg, da_du)


def _final(x, tgt, gf, copy_scale, tm=512):
    T = x.shape[0]
    tm = min(tm, T)

    def body(x_ref, t_ref, g_ref, dx_ref, dxb_ref, dg_ref, loss_ref):
        xv = x_ref[...]
        r = lax.rsqrt(jnp.mean(xv * xv, axis=-1, keepdims=True) + EPS)
        xhat = xv * r
        err = xhat * g_ref[...] - t_ref[...]
        dy = err * (1.0 / D)
        dxhat = dy * g_ref[...]
        dx = r * (dxhat - xhat * jnp.mean(dxhat * xhat, axis=-1, keepdims=True))
        dx_ref[...] = dx
        dxb_ref[...] = (copy_scale * dx).astype(BF16)

        @pl.when(pl.program_id(0) == 0)
        def _():
            dg_ref[...] = jnp.zeros_like(dg_ref)
            loss_ref[...] = jnp.zeros_like(loss_ref)

        dg_ref[...] += jnp.sum(dy * xhat, axis=0, keepdims=True)
        loss_ref[...] += (0.5 / D) * jnp.sum(err * err)

    row = pl.BlockSpec((tm, D), lambda i: (i, 0))
    vec = pl.BlockSpec((1, D), lambda i: (0, 0))
    return pl.pallas_call(
        body, name="final_loss", grid=(T // tm,),
        in_specs=[row, row, vec],
        out_specs=[row, row, vec, pl.BlockSpec((1, 128), lambda i: (0, 0))],
        out_shape=[jax.ShapeDtypeStruct((T, D), F32), jax.ShapeDtypeStruct((T, D), BF16),
                   jax.ShapeDtypeStruct((1, D), F32), jax.ShapeDtypeStruct((1, 128), F32)],
        compiler_params=_params("arbitrary"),
    )(x, tgt, gf)


def _mix_block(T, limit):
    return min(limit, T // 2)


def _rows(tb, width):
    return lax.broadcasted_iota(jnp.int32, (tb, width), 0)


def _rglru_gates(xc, wax_ref, bax_ref, lam_ref):
    pre = _dot(xc.astype(BF16), wax_ref[...]) + bax_ref[...]
    r = _sigmoid(pre[:, :D_RNN])
    ig = _sigmoid(pre[:, D_RNN:])
    z = -lam_ref[...]
    sp = jnp.maximum(z, 0.0) + jnp.log(1.0 + jnp.exp(-jnp.abs(z)))
    log_a = (-LRU_C) * r * sp
    a = jnp.exp(log_a)
    mult = jnp.sqrt(-jnp.tanh(log_a) * (1.0 + a * a))
    return r, ig, sp, a, mult


def _conv(xa_ext, cw_ref, cb_ref):
    y = cb_ref[...] + cw_ref[3:4, :] * xa_ext
    for k in range(1, 4):
        y = y + cw_ref[3 - k:4 - k, :] * pltpu.roll(xa_ext, k, 0)
    return y[HALO:]


def _pool_window_lanes():
    lane = lax.broadcasted_iota(jnp.int32, (1, D_POOL), 1)
    return jnp.where(lane < 64, 2, jnp.where(lane < 128, 4, jnp.where(lane < 192, 8, 16)))


def _pool_select(s2, s4, s8, s16):
    lane = lax.broadcasted_iota(jnp.int32, s2.shape, 1)
    return jnp.where(lane < 64, s2, jnp.where(lane < 128, s4, jnp.where(lane < 192, s8, s16)))


def _pool_diff(xp_ext, t0, tb):
    s2 = xp_ext + pltpu.roll(xp_ext, 1, 0)
    s4 = s2 + pltpu.roll(s2, 2, 0)
    s8 = s4 + pltpu.roll(s4, 4, 0)
    s16 = s8 + pltpu.roll(s8, 8, 0)
    sel = _pool_select(s2, s4, s8, s16)[HALO:]
    cnt = jnp.minimum(t0 + _rows(tb, D_POOL) + 1, _pool_window_lanes()).astype(F32)
    return sel / cnt - xp_ext[HALO:], cnt


def _head_masks():
    lane = lax.broadcasted_iota(jnp.int32, (1, D_SGU), 1)
    return [((lane >= 64 * h) & (lane < 64 * (h + 1))).astype(F32) for h in range(4)]


def _sgu_mix(w_ref, vch, masks):
    z = masks[0] * _dot(w_ref[0], vch)
    for h in range(1, 4):
        z = z + masks[h] * _dot(w_ref[h], vch)
    return z


def _mix_fwd(p, prm):
    T = p.shape[0]
    tb = _mix_block(T, 512)
    nb = T // tb

    def body(p_ref, xah_ref, xph_ref, cw_ref, cb_ref, wax_ref, bax_ref, lam_ref, wp_ref, ps_ref, sgn_ref,
             ws_ref, bz_ref, y_ref, hs_ref, carry_ref):
        i = pl.program_id(0)
        keep = (i > 0).astype(F32)

        @pl.when(i == 0)
        def _():
            carry_ref[...] = jnp.zeros_like(carry_ref)

        xa_ext = jnp.concatenate([xah_ref[...] * keep, p_ref[:, 512:1024]], axis=0)
        xc = _conv(xa_ext, cw_ref, cb_ref)
        r, ig, sp, a, mult = _rglru_gates(xc, wax_ref, bax_ref, lam_ref)
        bv = mult * (ig * xc)
        row = _rows(tb, D_RNN)
        s = 1
        while s < tb:
            m = row >= s
            bv = jnp.where(m, a * pltpu.roll(bv, s, 0) + bv, bv)
            a = jnp.where(m, a * pltpu.roll(a, s, 0), a)
            s *= 2
        h = bv + a * carry_ref[0:1, :]
        hs_ref[...] = h
        last = jnp.sum(jnp.where(_rows(8, D_RNN) == 7, hs_ref[tb - 8:tb, :], 0.0), axis=0, keepdims=True)
        carry_ref[...] = jnp.broadcast_to(last, carry_ref.shape)
        y_ref[:, 0:512] = (_gelu(p_ref[:, 0:512]) * h).astype(BF16)

        xp_ext = jnp.concatenate([xph_ref[...] * keep, p_ref[:, 1024:1280]], axis=0)
        d, _ = _pool_diff(xp_ext, i * tb, tb)
        y_ref[:, 512:768] = (_dot(d.astype(BF16), wp_ref[...]) * ps_ref[...]).astype(BF16)

        ug = _gelu(p_ref[:, 1280:1536])
        vg = _gelu(p_ref[:, 1536:1792])
        rv = lax.rsqrt(jnp.mean(vg * vg, axis=-1, keepdims=True) + EPS)
        vn = (vg * rv * sgn_ref[...]).astype(BF16)
        masks = _head_masks()
        for ci in range(tb // CHUNK):
            sl = slice(ci * CHUNK, (ci + 1) * CHUNK)
            z = _sgu_mix(ws_ref, vn[sl], masks) + bz_ref[...]
            y_ref[sl, 768:1024] = (ug[sl] * z).astype(BF16)

    hb = tb // HALO

    def halo(i):
        return jnp.maximum(i * hb - 1, 0)

    def full(shape):
        return pl.BlockSpec(shape, lambda i: (0,) * len(shape))

    return pl.pallas_call(
        body, name="mix_fwd", grid=(nb,),
        in_specs=[pl.BlockSpec((tb, D_IN), lambda i: (i, 0)),
                  pl.BlockSpec((HALO, D_RNN), lambda i: (halo(i), 1)),
                  pl.BlockSpec((HALO, D_POOL), lambda i: (halo(i), 4)),
                  full((4, D_RNN)), full((1, D_RNN)), full((D_RNN, 2 * D_RNN)), full((1, 2 * D_RNN)),
                  full((1, D_RNN)), full((D_POOL, D_POOL)), full((1, D_POOL)), full((1, D_SGU)),
                  full((4, CHUNK, CHUNK)), full((CHUNK, D_SGU))],
        out_specs=[pl.BlockSpec((tb, D), lambda i: (i, 0)), pl.BlockSpec((tb, D_RNN), lambda i: (i, 0))],
        out_shape=[jax.ShapeDtypeStruct((T, D), BF16), jax.ShapeDtypeStruct((T, D_RNN), F32)],
        scratch_shapes=[pltpu.VMEM((8, D_RNN), F32)],
        compiler_params=_params("arbitrary"),
    )(p, p, p, prm["conv_w"], prm["conv_b"], prm["wax"], prm["bax"], prm["lam"], prm["wpool"], prm["pool_scale"],
      prm["sgu_norm"], prm["ws"], prm["bz"])


def _mix_bwd(dy, p, hs, prm):
    T = p.shape[0]
    tb = _mix_block(T, 256)
    nb = T // tb
    hb = tb // HALO

    def body(dy_ref, p_ref, xah_ref, xph_ref, hs_ref, hsh_ref, cw_ref, cb_ref, wax_ref, waxt_ref, bax_ref,
             lam_ref, wp_ref, wpt_ref, ps_ref, sgn_ref, ws_ref, wst_ref, bz_ref,
             dp_ref, dcw_ref, dcb_ref, dwax_ref, dbax_ref, dlam_ref, dwp_ref, dps_ref, dsgn_ref, dws_ref,
             dbz_ref, gcarry_ref, xcfut_ref, mfut_ref):
        i = pl.program_id(0)
        bi = nb - 1 - i
        keep = (bi > 0).astype(F32)

        @pl.when(i == 0)
        def _():
            for ref in (dcw_ref, dcb_ref, dwax_ref, dbax_ref, dlam_ref, dwp_ref, dps_ref, dsgn_ref, dws_ref,
                        dbz_ref, gcarry_ref, xcfut_ref, mfut_ref):
                ref[...] = jnp.zeros_like(ref)

        xa_ext = jnp.concatenate([xah_ref[...] * keep, p_ref[:, 512:1024]], axis=0)
        xc = _conv(xa_ext, cw_ref, cb_ref)
        r, ig, sp, a, mult = _rglru_gates(xc, wax_ref, bax_ref, lam_ref)
        gg, dgg = _gelu_and_grad(p_ref[:, 0:512])
        dya = dy_ref[:, 0:512]
        dp_ref[:, 0:512] = (dya * hs_ref[...] * dgg).astype(BF16)
        row = _rows(tb, D_RNN)
        g = dya * gg + jnp.where(row == tb - 1, gcarry_ref[0:1, :], 0.0)
        al = pltpu.roll(a, tb - 1, 0)
        s = 1
        while s < tb:
            m = row < tb - s
            g = jnp.where(m, al * pltpu.roll(g, tb - s, 0) + g, g)
            al = jnp.where(m, al * pltpu.roll(al, tb - s, 0), al)
            s *= 2
        first = jnp.sum(jnp.where(_rows(8, D_RNN) == 0, (a * g)[0:8], 0.0), axis=0, keepdims=True)
        gcarry_ref[...] = jnp.broadcast_to(first, gcarry_ref.shape)
        hs_ext = jnp.concatenate([hsh_ref[...] * keep, hs_ref[...]], axis=0)
        h_prev = pltpu.roll(hs_ext, 1, 0)[HALO:]
        ix = ig * xc
        dlog_a = g * h_prev * a - (g * ix) * (a * a / mult)
        dlam_ref[...] += jnp.sum(dlog_a * r, axis=0, keepdims=True) * (LRU_C * _sigmoid(-lam_ref[...]))
        dpre_r = dlog_a * ((-LRU_C) * sp) * (r * (1.0 - r))
        dpre_i = (g * mult * xc) * (ig * (1.0 - ig))
        dpre = jnp.concatenate([dpre_r, dpre_i], axis=1)
        dbax_ref[...] += jnp.sum(dpre, axis=0, keepdims=True)
        dpre_b = dpre.astype(BF16)
        dwax_ref[...] += _dot_tn(xc.astype(BF16), dpre_b)
        dxc = g * mult * ig + _dot(dpre_b, waxt_ref[...])
        dcb_ref[...] += jnp.sum(dxc, axis=0, keepdims=True)
        for k in range(4):
            xs = xa_ext[HALO:] if k == 3 else pltpu.roll(xa_ext, 3 - k, 0)[HALO:]
            dcw_ref[k:k + 1, :] += jnp.sum(dxc * xs, axis=0, keepdims=True)
        dxc_ext = jnp.concatenate([dxc, xcfut_ref[...]], axis=0)
        n = tb + HALO
        dxa = cw_ref[3:4, :] * dxc_ext
        for k in range(1, 4):
            dxa = dxa + cw_ref[3 - k:4 - k, :] * pltpu.roll(dxc_ext, n - k, 0)
        dp_ref[:, 512:1024] = dxa[:tb].astype(BF16)
        xcfut_ref[...] = dxc[0:HALO]

        xp_ext = jnp.concatenate([xph_ref[...] * keep, p_ref[:, 1024:1280]], axis=0)
        d, cnt = _pool_diff(xp_ext, bi * tb, tb)
        db = d.astype(BF16)
        dyb = dy_ref[:, 512:768]
        dps_ref[...] += jnp.sum(dyb * _dot(db, wp_ref[...]), axis=0, keepdims=True)
        dq = (dyb * ps_ref[...]).astype(BF16)
        dwp_ref[...] += _dot_tn(db, dq)
        dd = _dot(dq, wpt_ref[...])
        mm = dd / cnt
        m_ext = jnp.concatenate([mm, mfut_ref[...]], axis=0)
        f2 = m_ext + pltpu.roll(m_ext, n - 1, 0)
        f4 = f2 + pltpu.roll(f2, n - 2, 0)
        f8 = f4 + pltpu.roll(f4, n - 4, 0)
        f16 = f8 + pltpu.roll(f8, n - 8, 0)
        dp_ref[:, 1024:1280] = (_pool_select(f2, f4, f8, f16)[:tb] - dd).astype(BF16)
        mfut_ref[...] = mm[0:HALO]

        ug, dug = _gelu_and_grad(p_ref[:, 1280:1536])
        vg, dvg = _gelu_and_grad(p_ref[:, 1536:1792])
        rv = lax.rsqrt(jnp.mean(vg * vg, axis=-1, keepdims=True) + EPS)
        vhat = vg * rv
        vn = (vhat * sgn_ref[...]).astype(BF16)
        dyc = dy_ref[:, 768:1024]
        masks = _head_masks()
        dz = dyc * ug
        dzb = dz.astype(BF16)
        dvn_parts = []
        for ci in range(tb // CHUNK):
            sl = slice(ci * CHUNK, (ci + 1) * CHUNK)
            z = _sgu_mix(ws_ref, vn[sl], masks) + bz_ref[...]
            dp_ref[sl, 1280:1536] = (dyc[sl] * z * dug[sl]).astype(BF16)
            dbz_ref[...] += dz[sl]
            for h in range(4):
                dws_ref[h] += _dot_nt((dz[sl] * masks[h]).astype(BF16), vn[sl])
            dvn_parts.append(_sgu_mix(wst_ref, dzb[sl], masks))
        dvn = jnp.concatenate(dvn_parts, axis=0)
        dsgn_ref[...] += jnp.sum(dvn * vhat, axis=0, keepdims=True)
        dvhat = dvn * sgn_ref[...]
        dvg_in = rv * (dvhat - vhat * jnp.mean(dvhat * vhat, axis=-1, keepdims=True))
        dp_ref[:, 1536:1792] = (dvg_in * dvg).astype(BF16)

        @pl.when(i == nb - 1)
        def _():
            tril = (lax.broadcasted_iota(jnp.int32, (CHUNK, CHUNK), 0)
                    >= lax.broadcasted_iota(jnp.int32, (CHUNK, CHUNK), 1)).astype(F32)
            for h in range(4):
                dws_ref[h] = dws_ref[h] * tril

    def blk(i):
        return nb - 1 - i

    def halo(i):
        return jnp.maximum(blk(i) * hb - 1, 0)

    def full(shape):
        return pl.BlockSpec(shape, lambda i: (0,) * len(shape))

    small_shapes = [(4, D_RNN), (1, D_RNN), (D_RNN, 2 * D_RNN), (1, 2 * D_RNN), (1, D_RNN), (D_POOL, D_POOL),
                    (1, D_POOL), (1, D_SGU), (4, CHUNK, CHUNK), (CHUNK, D_SGU)]
    outs = pl.pallas_call(
        body, name="mix_bwd", grid=(nb,),
        in_specs=[pl.BlockSpec((tb, D), lambda i: (blk(i), 0)),
                  pl.BlockSpec((tb, D_IN), lambda i: (blk(i), 0)),
                  pl.BlockSpec((HALO, D_RNN), lambda i: (halo(i), 1)),
                  pl.BlockSpec((HALO, D_POOL), lambda i: (halo(i), 4)),
                  pl.BlockSpec((tb, D_RNN), lambda i: (blk(i), 0)),
                  pl.BlockSpec((HALO, D_RNN), lambda i: (halo(i), 0)),
                  full((4, D_RNN)), full((1, D_RNN)), full((D_RNN, 2 * D_RNN)), full((2 * D_RNN, D_RNN)),
                  full((1, 2 * D_RNN)), full((1, D_RNN)), full((D_POOL, D_POOL)), full((D_POOL, D_POOL)),
                  full((1, D_POOL)), full((1, D_SGU)), full((4, CHUNK, CHUNK)), full((4, CHUNK, CHUNK)),
                  full((CHUNK, D_SGU))],
        out_specs=[pl.BlockSpec((tb, D_IN), lambda i: (blk(i), 0))] + [full(s) for s in small_shapes],
        out_shape=[jax.ShapeDtypeStruct((T, D_IN), BF16)] + [jax.ShapeDtypeStruct(s, F32) for s in small_shapes],
        scratch_shapes=[pltpu.VMEM((8, D_RNN), F32), pltpu.VMEM((HALO, D_RNN), F32),
                        pltpu.VMEM((HALO, D_POOL), F32)],
        compiler_params=_params("arbitrary"),
    )(dy, p, p, p, hs, hs, prm["conv_w"], prm["conv_b"], prm["wax"], prm["wax_t"], prm["bax"], prm["lam"],
      prm["wpool"], prm["wpool_t"], prm["pool_scale"], prm["sgu_norm"], prm["ws"], prm["ws_t"], prm["bz"])
    names = ("dp", "conv_w", "conv_b", "wax", "bax", "lam", "wpool", "pool_scale", "sgu_norm", "ws", "bz")
    return dict(zip(names, outs))


ANY = pl.BlockSpec(memory_space=pl.ANY)


def _place():
    x, y, c = lax.axis_index("x"), lax.axis_index("y"), lax.axis_index("c")
    return x, y, c


def _all_to_all(xs, name):
    def body(in_ref, out_ref, send_sems, recv_sems, local_sem):
        x, y, c = _place()
        me = 4 * x + 2 * y + c
        mine = pltpu.make_async_copy(in_ref.at[me], out_ref.at[me], local_sem)
        mine.start()
        copies = []
        for rel in range(1, N_DEV):
            tx = 1 - x if rel & 4 else x
            ty = 1 - y if rel & 2 else y
            tc = 1 - c if rel & 1 else c
            cp = pltpu.make_async_remote_copy(
                src_ref=in_ref.at[4 * tx + 2 * ty + tc], dst_ref=out_ref.at[me],
                send_sem=send_sems.at[rel - 1], recv_sem=recv_sems.at[rel - 1],
                device_id=(tx, ty, tc), device_id_type=MESH)
            cp.start()
            copies.append(cp)
        for cp in copies:
            cp.wait()
        mine.wait()

    return pl.pallas_call(
        body, name=name, in_specs=[ANY], out_specs=ANY,
        out_shape=jax.ShapeDtypeStruct(xs.shape, xs.dtype),
        scratch_shapes=[pltpu.SemaphoreType.DMA((N_DEV - 1,)), pltpu.SemaphoreType.DMA((N_DEV - 1,)),
                        pltpu.SemaphoreType.DMA],
    )(xs)


def _all_gather8(xs, name):
    def body(x_ref, out_ref, send_sems, recv_sems, local_sem):
        x, y, c = _place()
        me, sibling = (x, y, c), (x, y, 1 - c)
        chips = [(1 - x, y), (x, 1 - y), (1 - x, 1 - y)]

        def rows(px, py, pc):
            return out_ref.at[4 * px + 2 * py + pc]

        def copy(k, block, to, src=None):
            return pltpu.make_async_remote_copy(
                src_ref=rows(*block) if src is None else src, dst_ref=rows(*block),
                send_sem=send_sems.at[k], recv_sem=recv_sems.at[k], device_id=to, device_id_type=MESH)

        mine = pltpu.make_async_copy(x_ref, rows(*me), local_sem)
        mine.start()
        first = [copy(0, me, sibling, src=x_ref)]
        first += [copy(1 + j, me, (*chip, c), src=x_ref) for j, chip in enumerate(chips)]
        for cp in first:
            cp.start()
        passed = [copy(4 + j, (*chip, c), sibling) for j, chip in enumerate(chips)]
        for j, chip in enumerate(chips):
            copy(1 + j, (*chip, c), me).wait_recv()
            passed[j].start()
        copy(0, sibling, me).wait_recv()
        for j, chip in enumerate(chips):
            copy(4 + j, (*chip, 1 - c), me).wait_recv()
        for cp in first + passed:
            cp.wait_send()
        mine.wait()

    return pl.pallas_call(
        body, name=name, in_specs=[ANY], out_specs=ANY,
        out_shape=jax.ShapeDtypeStruct((N_DEV,) + xs.shape, xs.dtype),
        scratch_shapes=[pltpu.SemaphoreType.DMA((7,)), pltpu.SemaphoreType.DMA((7,)), pltpu.SemaphoreType.DMA],
    )(xs)


HBM = pl.BlockSpec(memory_space=pltpu.HBM)
SEM = pl.BlockSpec(memory_space=pltpu.SEMAPHORE)
EFFECT = pltpu.SideEffectType.DATAFLOW_SIDE_EFFECTING


def _in_hbm(a):
    return pltpu.with_memory_space_constraint(a, pltpu.HBM)


def _local_copy(src, dst, stage, sem):
    load = pltpu.make_async_copy(src, stage, sem)
    load.start()
    load.wait()
    store = pltpu.make_async_copy(stage, dst, sem)
    store.start()
    store.wait()


def _unique(windows):
    arrays = []
    for per_chip in windows:
        for arr, _ in per_chip:
            if not any(arr is a for a in arrays):
                arrays.append(arr)
    return arrays


def _exchange_start(layer, windows, lands, name):
    arrays = _unique(windows)
    na, nt = len(arrays), len(windows)

    def body(*refs):
        in_refs, land_refs = refs[:na], refs[na:na + nt]
        send_sems, recv_sems = refs[na + nt], refs[na + nt + 1]
        token = refs[-1]
        x, y, c = _place()
        me = 4 * x + 2 * y + c
        for t in range(nt):
            for j in range(N_CHIP):
                arr, window = windows[t][j]
                src = window(in_refs[next(i for i, a in enumerate(arrays) if a is arr)])

                @pl.when(me != 2 * j + layer)
                def _():
                    pltpu.make_async_remote_copy(
                        src_ref=src, dst_ref=land_refs[t].at[me], send_sem=send_sems.at[N_CHIP * t + j],
                        recv_sem=recv_sems.at[N_DEV * t + me], device_id=(j // 2, j % 2, layer),
                        device_id_type=MESH).start()
        token[...] = jnp.zeros_like(token)

    outs = pl.pallas_call(
        body, name=name,
        out_shape=(pltpu.SemaphoreType.DMA((N_CHIP * nt,)), pltpu.SemaphoreType.DMA((N_DEV * nt,)),
                   *[pltpu.HBM(a.shape, a.dtype) for a in lands], jax.ShapeDtypeStruct((8, 128), F32)),
        in_specs=[HBM] * (na + nt),
        out_specs=(SEM, SEM, *[HBM] * nt, pl.BlockSpec(memory_space=pltpu.VMEM)),
        input_output_aliases={na + t: 2 + t for t in range(nt)},
        compiler_params=pltpu.CompilerParams(has_side_effects=EFFECT, vmem_limit_bytes=VMEM_LIMIT),
    )(*[_in_hbm(a) for a in arrays], *[_in_hbm(a) for a in lands])
    return outs[0], outs[1], list(outs[2:2 + nt]), outs[-1]


def _exchange_wait(layer, windows, lands, send_sems, recv_sems, after, name):
    arrays = _unique(windows)
    na, nt = len(arrays), len(windows)

    def body(*refs):
        in_refs, land_refs = refs[:na], refs[na:na + nt]
        send_sems, recv_sems = refs[na + nt], refs[na + nt + 1]
        stages, local_sem = refs[-1 - nt:-1], refs[-1]
        x, y, c = _place()
        me = 4 * x + 2 * y + c

        def source(t, j):
            arr, window = windows[t][j]
            return window(in_refs[next(i for i, a in enumerate(arrays) if a is arr)])

        @pl.when(c == layer)
        def _():
            for t in range(nt):
                for j in range(N_CHIP):
                    @pl.when(me == 2 * j + layer)
                    def _():
                        _local_copy(source(t, j), land_refs[t].at[me], stages[t], local_sem)

        for t in range(nt):
            for j in range(N_CHIP):
                @pl.when(me != 2 * j + layer)
                def _():
                    pltpu.make_async_remote_copy(
                        src_ref=source(t, j), dst_ref=land_refs[t].at[me], send_sem=send_sems.at[N_CHIP * t + j],
                        recv_sem=recv_sems.at[N_DEV * t + me], device_id=(j // 2, j % 2, layer),
                        device_id_type=MESH).wait_send()

        @pl.when(c == layer)
        def _():
            for t in range(nt):
                for s in range(N_DEV):
                    @pl.when(me != s)
                    def _():
                        slot = land_refs[t].at[s]
                        pltpu.make_async_remote_copy(
                            src_ref=slot, dst_ref=slot, send_sem=send_sems.at[N_CHIP * t],
                            recv_sem=recv_sems.at[N_DEV * t + s], device_id=(x, y, c),
                            device_id_type=MESH).wait_recv()

    outs = pl.pallas_call(
        body, name=name,
        out_shape=tuple(pltpu.HBM(a.shape, a.dtype) for a in lands),
        in_specs=[HBM] * (na + nt) + [SEM, SEM, ANY],
        out_specs=tuple([HBM] * nt),
        input_output_aliases={na + t: t for t in range(nt)},
        scratch_shapes=[pltpu.VMEM(a.shape[1:], a.dtype) for a in lands] + [pltpu.SemaphoreType.DMA],
        compiler_params=pltpu.CompilerParams(has_side_effects=EFFECT, vmem_limit_bytes=VMEM_LIMIT),
    )(*[_in_hbm(a) for a in arrays], *lands, send_sems, recv_sems, after)
    return list(outs)


def _other_chips(x, y):
    return [(1 - x, y), (x, 1 - y), (1 - x, 1 - y)]


def _split_rows(rows):
    if rows < 32:
        return [(0, rows), (rows, 0)]
    cut = -(-(rows // 2) // 16) * 16
    return [(0, cut), (cut, rows - cut)]


def _gather_copies(in_refs, land_refs, send_sems, recv_sems, x, y, sender_core, both):
    copies = []
    for t, src in enumerate(in_refs):
        r0, n = _split_rows(src.shape[0])[sender_core]
        if n == 0:
            continue
        for rel, (tx, ty) in enumerate(_other_chips(x, y)):
            for tc in (0, 1) if both else (sender_core,):
                copies.append(dict(
                    src_ref=src.at[pl.ds(r0, n)], dst_ref=land_refs[t].at[2 * x + y].at[pl.ds(r0, n)],
                    send_sem=send_sems.at[6 * t + 2 * rel + tc],
                    recv_sem=recv_sems.at[2 * (3 * t + rel) + sender_core],
                    device_id=(tx, ty, tc), device_id_type=MESH))
    return copies


def _gather_start(shards, lands, name, both):
    nt = len(shards)

    def body(*refs):
        in_refs, land_refs = refs[:nt], refs[nt:2 * nt]
        send_sems, recv_sems = refs[2 * nt], refs[2 * nt + 1]
        token = refs[-1]
        x, y, c = _place()
        for core in range(2):
            @pl.when(c == core)
            def _():
                for cp_args in _gather_copies(in_refs, land_refs, send_sems, recv_sems, x, y, core, both):
                    pltpu.make_async_remote_copy(**cp_args).start()
        token[...] = jnp.zeros_like(token)

    outs = pl.pallas_call(
        body, name=name,
        out_shape=(pltpu.SemaphoreType.DMA((6 * nt,)), pltpu.SemaphoreType.DMA((6 * nt,)),
                   *[pltpu.HBM(a.shape, a.dtype) for a in lands], jax.ShapeDtypeStruct((8, 128), F32)),
        in_specs=[HBM] * (2 * nt),
        out_specs=(SEM, SEM, *[HBM] * nt, pl.BlockSpec(memory_space=pltpu.VMEM)),
        input_output_aliases={nt + t: 2 + t for t in range(nt)},
        compiler_params=pltpu.CompilerParams(has_side_effects=EFFECT, vmem_limit_bytes=VMEM_LIMIT),
    )(*[_in_hbm(a) for a in shards], *[_in_hbm(a) for a in lands])
    return outs[0], outs[1], list(outs[2:2 + nt]), outs[-1]


def _gather_wait(shards, lands, send_sems, recv_sems, after, name, both):
    nt = len(shards)

    def body(*refs):
        in_refs, land_refs = refs[:nt], refs[nt:2 * nt]
        send_sems, recv_sems = refs[2 * nt], refs[2 * nt + 1]
        stages, local_sem = refs[-1 - nt:-1], refs[-1]
        x, y, c = _place()

        for t in range(nt):
            _local_copy(in_refs[t], land_refs[t].at[2 * x + y], stages[t], local_sem)

        for core in range(2):
            @pl.when(c == core)
            def _():
                for cp_args in _gather_copies(in_refs, land_refs, send_sems, recv_sems, x, y, core, both):
                    pltpu.make_async_remote_copy(**cp_args).wait_send()

        def wait_parts_from(core):
            for t in range(nt):
                r0, n = _split_rows(in_refs[t].shape[0])[core]
                for rel, (tx, ty) in enumerate(_other_chips(x, y)):
                    if n > 0:
                        part = land_refs[t].at[2 * tx + ty].at[pl.ds(r0, n)]
                        pltpu.make_async_remote_copy(
                            src_ref=part, dst_ref=part, send_sem=send_sems.at[6 * t],
                            recv_sem=recv_sems.at[2 * (3 * t + rel) + core], device_id=(x, y, c),
                            device_id_type=MESH).wait_recv()

        for core in range(2):
            if both:
                wait_parts_from(core)
            else:
                pl.when(c == core)(functools.partial(wait_parts_from, core))

    outs = pl.pallas_call(
        body, name=name,
        out_shape=tuple(pltpu.HBM(a.shape, a.dtype) for a in lands),
        in_specs=[HBM] * (2 * nt) + [SEM, SEM] + [ANY] * len(after),
        out_specs=tuple([HBM] * nt),
        input_output_aliases={nt + t: t for t in range(nt)},
        scratch_shapes=[pltpu.VMEM(a.shape, a.dtype) for a in shards] + [pltpu.SemaphoreType.DMA],
        compiler_params=pltpu.CompilerParams(has_side_effects=EFFECT, vmem_limit_bytes=VMEM_LIMIT),
    )(*[_in_hbm(a) for a in shards], *lands, send_sems, recv_sems, *after)
    return list(outs)


def _pair_share(lands, name):
    nt = len(lands)
    splits = [_split_rows(a.shape[1]) for a in lands]

    def body(*refs):
        land_refs = refs[:nt]
        stages = refs[2 * nt:3 * nt]
        load_sems, send_sems, recv_sems = refs[3 * nt:]
        x, y, c = _place()

        def parts(core):
            out = []
            for t in range(nt):
                r0, n = splits[t][core]
                for rel, (tx, ty) in enumerate(_other_chips(x, y)):
                    if n > 0:
                        out.append((3 * t + rel, stages[t].at[rel, pl.ds(0, n)],
                                    land_refs[t].at[2 * tx + ty].at[pl.ds(r0, n)]))
            return out

        def send(core):
            loads = [pltpu.make_async_copy(part, stage, load_sems.at[k]) for k, stage, part in parts(core)]
            for cp in loads:
                cp.start()
            pushes = []
            for cp, (k, stage, part) in zip(loads, parts(core)):
                cp.wait()
                push = pltpu.make_async_remote_copy(src_ref=stage, dst_ref=part, send_sem=send_sems.at[k],
                                                    recv_sem=recv_sems.at[k], device_id=(x, y, 1 - c),
                                                    device_id_type=MESH)
                push.start()
                pushes.append(push)
            for push in pushes:
                push.wait_send()
            for k, _, part in parts(1 - core):
                pltpu.make_async_remote_copy(src_ref=part, dst_ref=part, send_sem=send_sems.at[k],
                                             recv_sem=recv_sems.at[k], device_id=(x, y, 1 - c),
                                             device_id_type=MESH).wait_recv()

        for core in range(2):
            pl.when(c == core)(functools.partial(send, core))

    outs = pl.pallas_call(
        body, name=name, in_specs=[ANY] * nt, out_specs=[ANY] * nt,
        out_shape=[jax.ShapeDtypeStruct(a.shape, a.dtype) for a in lands],
        input_output_aliases={t: t for t in range(nt)},
        scratch_shapes=[pltpu.VMEM((3, max(n for _, n in sp), a.shape[2]), a.dtype) for a, sp in zip(lands, splits)]
        + [pltpu.SemaphoreType.DMA((3 * nt,))] * 3,
        compiler_params=pltpu.CompilerParams(vmem_limit_bytes=VMEM_LIMIT),
    )(*lands)
    return list(outs)


def _tie(a, token):
    def body(a_ref, token_ref, o_ref):
        pass

    return pl.pallas_call(
        body, name="tie", in_specs=[ANY, ANY], out_specs=ANY,
        out_shape=jax.ShapeDtypeStruct(a.shape, a.dtype), input_output_aliases={0: 0},
    )(a, token)


def _sum_share(xs, name):
    _, r, cols = xs.shape
    tr = max(t for t in range(16, min(r, 704) + 1, 16) if r % t == 0)
    nblk = r // tr

    def body(x_ref, out_ref, acc_ref, send_sems, local_sems, recv_sem):
        i = pl.program_id(0)
        slot = i % 2
        x, y, c = _place()

        def copies(s, blk):
            dst = out_ref.at[c, pl.ds(blk * tr, tr), :]
            loc = pltpu.make_async_copy(acc_ref.at[s], dst, local_sems.at[s])
            rem = pltpu.make_async_remote_copy(src_ref=acc_ref.at[s], dst_ref=dst, send_sem=send_sems.at[s],
                                               recv_sem=recv_sem, device_id=(x, y, 1 - c), device_id_type=MESH)
            return loc, rem

        @pl.when(i >= 2)
        def _():
            loc, rem = copies(slot, i - 2)
            loc.wait()
            rem.wait_send()

        acc = x_ref[0].astype(F32)
        for k in range(1, N_DEV):
            acc = acc + x_ref[k].astype(F32)
        acc_ref[slot] = acc
        loc, rem = copies(slot, i)
        loc.start()
        rem.start()

        @pl.when(i == nblk - 1)
        def _():
            for back in range(min(2, nblk)):
                blk = nblk - 1 - back
                loc, rem = copies(blk % 2, blk)
                loc.wait()
                rem.wait_send()
            theirs = out_ref.at[1 - c]
            pltpu.make_async_remote_copy(src_ref=theirs, dst_ref=theirs, send_sem=send_sems.at[0],
                                         recv_sem=recv_sem, device_id=(x, y, 1 - c),
                                         device_id_type=MESH).wait_recv()

    return pl.pallas_call(
        body, name=name, grid=(nblk,),
        in_specs=[pl.BlockSpec((N_DEV, tr, cols), lambda i: (0, i, 0))],
        out_specs=ANY,
        out_shape=jax.ShapeDtypeStruct((2, r, cols), F32),
        scratch_shapes=[pltpu.VMEM((2, tr, cols), F32), pltpu.SemaphoreType.DMA((2,)),
                        pltpu.SemaphoreType.DMA((2,)), pltpu.SemaphoreType.DMA],
        compiler_params=_params("arbitrary"),
    )(xs)


def _sum8(xs, name):
    _, r, cols = xs.shape
    tr = 8
    for cand in (256, 128, 64, 32, 16):
        if r % cand == 0:
            tr = cand
            break

    def body(x_ref, o_ref):
        acc = x_ref[0].astype(F32)
        for k in range(1, N_DEV):
            acc = acc + x_ref[k].astype(F32)
        o_ref[...] = acc

    return pl.pallas_call(
        body, name=name, grid=(r // tr,),
        in_specs=[pl.BlockSpec((N_DEV, tr, cols), lambda i: (0, i, 0))],
        out_specs=pl.BlockSpec((tr, cols), lambda i: (i, 0)),
        out_shape=jax.ShapeDtypeStruct((r, cols), F32),
        compiler_params=_params("parallel"),
    )(xs)


def _adamw(w, g, m, v, name):
    R, C = w.shape
    tr = max([t for t in range(8, min(R, 512) + 1, 8) if R % t == 0] or [R])
    c1 = 1.0 / (1.0 - ADAM_B1 ** ADAM_STEP)
    c2 = 1.0 / (1.0 - ADAM_B2 ** ADAM_STEP)

    def body(w_ref, g_ref, m_ref, v_ref, d_ref, nm_ref, nv_ref):
        gv = g_ref[...]
        nm = ADAM_B1 * m_ref[...] + (1.0 - ADAM_B1) * gv
        nv = ADAM_B2 * v_ref[...] + (1.0 - ADAM_B2) * (gv * gv)
        d_ref[...] = (-ADAM_LR) * ((nm * c1) / (jnp.sqrt(nv * c2) + ADAM_EPS) + ADAM_WD * w_ref[...])
        nm_ref[...] = nm
        nv_ref[...] = nv

    spec = pl.BlockSpec((tr, C), lambda i: (i, 0))
    shape = jax.ShapeDtypeStruct((R, C), F32)
    return pl.pallas_call(
        body, name=name, grid=(R // tr,), in_specs=[spec] * 4, out_specs=[spec] * 3, out_shape=[shape] * 3,
        compiler_params=_params("parallel"),
    )(w, g, m, v)


def _flat_rows(parts, rows):
    flat = jnp.concatenate([q.reshape(-1) for q in parts])
    flat = jnp.pad(flat, (0, rows * LANES - flat.shape[0]))
    return flat.reshape(rows, LANES)


def _round_up(n, m):
    return (n + m - 1) // m * m


def _block_diag(w):
    H, n, _ = w.shape
    eye = jnp.eye(H, dtype=w.dtype)
    return (eye[:, None, :, None] * w[:, :, None, :]).reshape(H * n, H * n)


def _diag_blocks(w, H, n):
    w4 = w.reshape(H, n, H, n)
    return jnp.stack([w4[h, :, h, :] for h in range(H)])


W_IN_T = ("ffn1_w_in", "w_in", "ffn2_w_in")


def _ffn_in_weights(g_in):
    zeros = jnp.zeros((FF_HALF - FF_SHARD, D), g_in.dtype)
    wg_t = jnp.concatenate([g_in[0], zeros, g_in[1], zeros], axis=0)
    wu_t = jnp.concatenate([g_in[2], zeros, g_in[3], zeros], axis=0)
    return wg_t, wu_t


def _ffn_out_weights(g_out):
    zeros = jnp.zeros((FF_HALF - FF_SHARD, D), g_out.dtype)
    return jnp.concatenate([g_out[0], g_out[1], zeros, g_out[2], g_out[3], zeros], axis=0)


LAND_SHAPES = {"ffn1_w_in": (FF_SHARD, D), "ffn1_w_out": (FF_ROWS, D), "w_in": (D_IN // N_CHIP, D),
               "w_out": (D // N_CHIP, D), "ffn2_w_in": (FF_SHARD, D), "ffn2_w_out": (FF_ROWS, D)}


def _rows_window(arr, start, size):
    return arr, lambda r: r.at[pl.ds(start, size), :]


def _w_in_grad_windows(dwg_t, dwu_t):
    return [_rows_window(dwg_t if j < 2 else dwu_t, (j % 2) * FF_HALF, FF_SHARD) for j in range(N_CHIP)]


def _w_out_grad_windows(dwout):
    return [_rows_window(dwout, (j // 2) * FF_HALF + (j % 2) * FF_ROWS, FF_ROWS) for j in range(N_CHIP)]


def _mix_grad_windows(dwin_t, dwo):
    win = [_rows_window(dwin_t, j * (D_IN // N_CHIP), D_IN // N_CHIP) for j in range(N_CHIP)]
    wo = [_rows_window(dwo, j * (D // N_CHIP), D // N_CHIP) for j in range(N_CHIP)]
    return win, wo


def kernel(x, ffn1_norm, ffn1_w_in, ffn1_w_out, mix_norm, w_in, conv_w, conv_b, rg_w_a, rg_b_a, rg_w_x, rg_b_x, lru_lambda, pool_w, pool_scale, sgu_norm, sgu_w, sgu_b, w_out, ffn2_norm, ffn2_w_in, ffn2_w_out, final_norm, loss_target, m_ffn1_norm, m_ffn1_w_in, m_ffn1_w_out, m_mix_norm, m_w_in, m_conv_w, m_conv_b, m_rg_w_a, m_rg_b_a, m_rg_w_x, m_rg_b_x, m_lru_lambda, m_pool_w, m_pool_scale, m_sgu_norm, m_sgu_w, m_sgu_b, m_w_out, m_ffn2_norm, m_ffn2_w_in, m_ffn2_w_out, m_final_norm, v_ffn1_norm, v_ffn1_w_in, v_ffn1_w_out, v_mix_norm, v_w_in, v_conv_w, v_conv_b, v_rg_w_a, v_rg_b_a, v_rg_w_x, v_rg_b_x, v_lru_lambda, v_pool_w, v_pool_scale, v_sgu_norm, v_sgu_w, v_sgu_b, v_w_out, v_ffn2_norm, v_ffn2_w_in, v_ffn2_w_out, v_final_norm):
    args = locals()
    W = {n: args[n] for n in WEIGHTS}
    M = {n: args["m_" + n] for n in WEIGHTS}
    V = {n: args["v_" + n] for n in WEIGHTS}
    depth = ffn1_norm.shape[0]
    T = x.shape[1]
    xi, yi, ci = _place()
    chip = 2 * xi + yi

    assert depth == 2, "core c of a chip reduces layer c"
    groups = [(l, names) for l in range(depth)
              for names in (["ffn1_w_in"], ["ffn1_w_out", "w_in", "w_out"], ["ffn2_w_in", "ffn2_w_out"])]
    def stored(a, n):
        return jnp.swapaxes(a, 1, 2) if n in W_IN_T else a

    wb = {n: stored(W[n], n).astype(BF16) for n in BIG}
    groups[1][1].append("conv_w")
    conv_shard = conv_w.reshape(-1, conv_w.shape[-1])
    flights = {}

    def weights_start(k, dep=None):
        l, names = groups[k]
        shards = [conv_shard if n == "conv_w" else wb[n][l] for n in names]
        if dep is not None:
            shards[0] = _tie(shards[0], dep)
        lands = [lax.empty((N_CHIP,) + s.shape, s.dtype) for s in shards]
        send, recv, lands, token = _gather_start(shards, lands, "weights_start_%d" % k, k >= 2)
        flights[k] = (shards, lands, send, recv)
        return token

    def weights_wait(k, after):
        l, names = groups[k]
        shards, lands, send, recv = flights[k]
        got = _gather_wait(shards, lands, send, recv, after, "weights_wait_%d" % k, k >= 2)
        token = weights_start(k + 1, got[0]) if k + 1 < len(groups) else None
        if k < 2:
            got = _pair_share(got, "weights_share_%d" % k)
        return dict(zip(names, got)), token

    def after_start(a, token):
        return a if token is None else _tie(a, token)

    first_tokens = [weights_start(0)]

    layers = []
    for l in range(depth):
        L = {f: dict(norm=W[f + "_norm"][l][None]) for f in ("ffn1", "ffn2")}
        ws = jnp.where(jnp.tril(jnp.ones((CHUNK, CHUNK), bool))[None], sgu_w[l], 0.0)
        wax = jnp.concatenate([_block_diag(rg_w_a[l]), _block_diag(rg_w_x[l])], axis=1)
        wpool = _block_diag(pool_w[l])
        L["mix"] = dict(
            conv_b=conv_b[l][None], wax=wax.astype(BF16), wax_t=wax.T.astype(BF16),
            bax=jnp.concatenate([rg_b_a[l].reshape(-1), rg_b_x[l].reshape(-1)])[None], lam=lru_lambda[l][None],
            wpool=wpool.astype(BF16), wpool_t=wpool.T.astype(BF16), pool_scale=pool_scale[l][None],
            sgu_norm=sgu_norm[l][None], ws=ws.astype(BF16), ws_t=jnp.swapaxes(ws, 1, 2).astype(BF16),
            bz=jnp.repeat(sgu_b[l].T, 64, axis=1))
        L["mix_norm"] = mix_norm[l][None]
        layers.append(L)
    for token in first_tokens:
        layers[0]["ffn1"]["norm"] = _tie(layers[0]["ffn1"]["norm"], token)

    xs = x[0]
    saved = []
    for l, L in enumerate(layers):
        F1, F2 = L["ffn1"], L["ffn2"]
        got, token = weights_wait(3 * l, [xs] + ([wb[n] for n in BIG] if l == 0 else []))
        F1["wg"], F1["wu"] = _ffn_in_weights(got["ffn1_w_in"])
        F1["norm"] = after_start(F1["norm"], token)
        h, da_dg, da_du, a = _ffn_in(xs, F1["norm"], F1["wg"], F1["wu"])
        got, token = weights_wait(3 * l + 1, [a])
        F1["wout"] = _ffn_out_weights(got["ffn1_w_out"])
        L["w_in"] = got["w_in"].reshape(D_IN, D)
        L["w_out"] = got["w_out"].reshape(D, D)
        if "conv_w" in got:
            conv_full = jnp.concatenate([got["conv_w"][j] for j in range(N_CHIP)], axis=1)
            for ll in range(depth):
                layers[ll]["mix"]["conv_w"] = conv_full.reshape(depth, 4, D_RNN)[ll]
        a = after_start(a, token)
        x1 = _mm_res(a, F1["wout"], xs, 0.5, "ffn_out", tm=512, tn=D)
        s1 = (xs, h, da_dg, da_du, a)
        hm, p = _mix_in(x1, L["mix_norm"], L["w_in"])
        ycat, hs = _mix_fwd(p, L["mix"])
        x2 = _mm_res(ycat, L["w_out"], x1, 1.0, "mix_out", tm=512, tn=D)
        got, token = weights_wait(3 * l + 2, [x2])
        F2["wg"], F2["wu"] = _ffn_in_weights(got["ffn2_w_in"])
        F2["wout"] = _ffn_out_weights(got["ffn2_w_out"])
        F2["norm"] = after_start(F2["norm"], token)
        s2 = (x2,) + tuple(_ffn_in(x2, F2["norm"], F2["wg"], F2["wu"]))
        x3 = _mm_res(s2[-1], F2["wout"], x2, 0.5, "ffn_out", tm=512, tn=D)
        saved.append((s1, (x1, hm, p, ycat, hs), s2))
        xs = x3

    dx, dxb, d_final, loss_part = _final(xs, loss_target[0], final_norm[None], 0.5)

    G = {n: [None] * depth for n in SMALL if n != "final_norm"}
    lands = {n: lax.empty((N_DEV,) + LAND_SHAPES[n], BF16) for n in BIG}
    in_flight = []

    def send_grads(l, windows, tag):
        names = list(windows)
        send, recv, thru, token = _exchange_start(l, [windows[n] for n in names], [lands[n] for n in names],
                                                  "grads_start_" + tag)
        lands.update(zip(names, thru))
        in_flight.append((l, names, [windows[n] for n in names], send, recv, "grads_wait_" + tag))
        return token

    def ffn_bwd(dx, dxb, F, s, f, l, pending, send_now):
        xin, h, da_dg, da_du, a = s
        dwout = _mm_tn(a, dxb, 1.0, "ffn_dwout")
        if send_now:
            token = send_grads(l, {f + "_w_out": _w_out_grad_windows(dwout)}, "l%d_%s_out" % (l, f))
            dxb = _tie(dxb, token)
        else:
            pending[f + "_w_out"] = _w_out_grad_windows(dwout)
        dg, du = _ffn_mid_bwd(dxb, F["wout"], da_dg, da_du)
        dwg = _mm_tn(dg, h, 1.0, "ffn_dwg")
        dwu = _mm_tn(du, h, 1.0, "ffn_dwu")
        pending[f + "_w_in"] = _w_in_grad_windows(dwg, dwu)
        if send_now:
            dg = _tie(dg, send_grads(l, pending, "l%d_%s" % (l, f)))
        dx, dxb, dn = _dh_rms_bwd([(dg, F["wg"]), (du, F["wu"])], xin, F["norm"], dx,
                                  1.0 if f == "ffn2" else 0.5, "ffn_dh")
        G[f + "_norm"][l] = dn[0]
        return dx, dxb

    for l in reversed(range(depth)):
        L = layers[l]
        s1, (x1, hm, p, ycat, hs), s2 = saved[l]
        pending = {}
        dx, dxb = ffn_bwd(dx, dxb, L["ffn2"], s2, "ffn2", l, pending, l == 0)
        if l == 0:
            pending = {}
        dycat = _mm_nt(dxb, L["w_out"], "mix_dy", tm=512, tn=D)
        dwo = _mm_tn(ycat, dxb, 1.0, "mix_dwout")
        mg = _mix_bwd(dycat, p, hs, L["mix"])
        dwin = _mm_tn(mg["dp"], hm, 1.0, "mix_dwin")
        pending["w_in"], pending["w_out"] = _mix_grad_windows(dwin, dwo)
        if l == 0:
            dp = _tie(mg["dp"], send_grads(l, pending, "l0_mix"))
            pending = {}
        else:
            dp = mg["dp"]
        dx, dxb, dn = _dh_rms_bwd([(dp, L["w_in"])], x1, L["mix_norm"], dx, 0.5, "mix_dh")
        G["mix_norm"][l] = dn[0]
        G["conv_w"][l], G["conv_b"][l] = mg["conv_w"], mg["conv_b"][0]
        G["rg_w_a"][l] = _diag_blocks(mg["wax"][:, :D_RNN], 8, 64)
        G["rg_w_x"][l] = _diag_blocks(mg["wax"][:, D_RNN:], 8, 64)
        G["rg_b_a"][l] = mg["bax"][0, :D_RNN].reshape(8, 64)
        G["rg_b_x"][l] = mg["bax"][0, D_RNN:].reshape(8, 64)
        G["lru_lambda"][l] = mg["lam"][0]
        G["pool_w"][l] = _diag_blocks(mg["wpool"], 4, 64)
        G["pool_scale"][l], G["sgu_norm"][l] = mg["pool_scale"][0], mg["sgu_norm"][0]
        G["sgu_w"][l] = mg["ws"]
        G["sgu_b"][l] = mg["bz"].reshape(CHUNK, 4, 64).sum(-1).T
        dx, dxb = ffn_bwd(dx, dxb, L["ffn1"], s1, "ffn1", l, pending, l == 0)
        if l > 0:
            dxb = _tie(dxb, send_grads(l, pending, "l%d" % l))
    grad_x = dx[None]
    G = {n: jnp.stack(v) for n, v in G.items()}
    G["final_norm"] = d_final[0]

    for l, names, windows, send, recv, tag in in_flight:
        lands.update(zip(names, _exchange_wait(l, windows, [lands[n] for n in names], send, recv, dx, tag)))
    both = [_sum_share(lands[n], "sum_share_" + n) for n in BIG]
    grads = dict(zip(BIG, both))

    small_sizes = [int(np.prod(G[n].shape)) for n in SMALL]
    srows = _round_up(sum(small_sizes) + 1, N_DEV * 8 * LANES) // (N_DEV * LANES)
    sflat = _flat_rows([G[n] for n in SMALL] + [loss_part[0, :1]], N_DEV * srows)
    sgot = _all_to_all(sflat.reshape(N_DEV, srows, LANES), "exchange_small_grads")
    sall = _all_gather8(_sum8(sgot, "sum_small_grads"), "share_small_grads").reshape(-1)
    off = 0
    for n, size in zip(SMALL, small_sizes):
        grads[n] = sall[off:off + size].reshape(G[n].shape)
        off += size
    loss = sall[off]
    grads["conv_w"] = lax.dynamic_slice_in_dim(grads["conv_w"], chip * conv_w.shape[2], conv_w.shape[2], axis=2)

    delta, new_m, new_v = {}, {}, {}
    for n in BIG:
        shp = grads[n].shape
        two_d = (shp[0] * shp[1], shp[2])
        outs = _adamw(stored(W[n], n).reshape(two_d), grads[n].reshape(two_d), stored(M[n], n).reshape(two_d),
                      stored(V[n], n).reshape(two_d), "adamw_" + n)
        delta[n], new_m[n], new_v[n] = (stored(o.reshape(shp), n) for o in outs)
        grads[n] = stored(grads[n], n)
    arows = _round_up(sum(int(np.prod(W[n].shape)) for n in SMALL), 8 * LANES) // LANES
    outs = _adamw(*(_flat_rows([src[n] for n in SMALL], arows) for src in (W, grads, M, V)), "adamw_small")
    outs = [o.reshape(-1) for o in outs]
    off = 0
    for n in SMALL:
        size = int(np.prod(W[n].shape))
        delta[n], new_m[n], new_v[n] = (o[off:off + size].reshape(W[n].shape) for o in outs)
        off += size

    return (loss, grad_x, *[grads[n] for n in WEIGHTS], *[delta[n] for n in WEIGHTS],
            *[new_m[n] for n in WEIGHTS], *[new_v[n] for n in WEIGHTS])
```

```python
import functools
import math

import jax
import jax.numpy as jnp
import numpy as np
from jax import lax
from jax.experimental import pallas as pl
from jax.experimental.pallas import tpu as pltpu

F32 = jnp.float32
BF16 = jnp.bfloat16
MESH = pl.DeviceIdType.MESH

D = 1024
D_RNN = 512
D_POOL = 256
D_SGU = 256
D_IN = 1792
D_FF = 2752
D_FFP = 2816
N_CHIP = 4
FF_SHARD = D_FF // 2
FF_HALF = D_FFP // 2
FF_ROWS = D_FF // N_CHIP
CHUNK = 128
HALO = 16
SCAN_SUB = 64
EPS = 1e-6
LRU_C = 8.0
N_DEV = 8
LANES = 1024
VMEM_LIMIT = 56 * 1024 * 1024

ADAM_LR, ADAM_B1, ADAM_B2, ADAM_EPS, ADAM_WD, ADAM_STEP = 0.001, 0.9, 0.999, 1e-08, 0.01, 10

BIG = ("ffn1_w_in", "ffn1_w_out", "w_in", "w_out", "ffn2_w_in", "ffn2_w_out")
SMALL = ("ffn1_norm", "mix_norm", "conv_w", "conv_b", "rg_w_a", "rg_b_a", "rg_w_x", "rg_b_x", "lru_lambda",
         "pool_w", "pool_scale", "sgu_norm", "sgu_w", "sgu_b", "ffn2_norm", "final_norm")
WEIGHTS = ("ffn1_norm", "ffn1_w_in", "ffn1_w_out", "mix_norm", "w_in", "conv_w", "conv_b", "rg_w_a", "rg_b_a",
           "rg_w_x", "rg_b_x", "lru_lambda", "pool_w", "pool_scale", "sgu_norm", "sgu_w", "sgu_b", "w_out",
           "ffn2_norm", "ffn2_w_in", "ffn2_w_out", "final_norm")


def _params(*sem):
    return pltpu.CompilerParams(dimension_semantics=sem, vmem_limit_bytes=VMEM_LIMIT)


def _gelu(x):
    c = math.sqrt(2.0 / math.pi)
    t = jnp.tanh(c * (x + 0.044715 * (x * x * x)))
    return 0.5 * x * (1.0 + t)


def _gelu_and_grad(x):
    c = math.sqrt(2.0 / math.pi)
    x2 = x * x
    t = jnp.tanh(c * (x + 0.044715 * (x2 * x)))
    g = 0.5 * x * (1.0 + t)
    dg = 0.5 * (1.0 + t) + 0.5 * x * (1.0 - t * t) * (c * (1.0 + 3.0 * 0.044715 * x2))
    return g, dg


def _sigmoid(x):
    return 0.5 * jnp.tanh(0.5 * x) + 0.5


def _dot(a, b):
    return jnp.dot(a, b, preferred_element_type=F32)


def _dot_tn(a, b):
    return lax.dot_general(a, b, (((0,), (0,)), ((), ())), preferred_element_type=F32)


def _dot_nt(a, b):
    return lax.dot_general(a, b, (((1,), (1,)), ((), ())), preferred_element_type=F32)


def _tile(n, limit):
    best = 128
    for t in range(128, min(n, limit) + 1, 128):
        if n % t == 0:
            best = t
    assert n % best == 0, (n, limit)
    return best


def _mm_res(a, b, res, scale, name, tm=1024, tn=512):
    M, K = a.shape
    N = b.shape[1]
    tm, tn = min(tm, M), _tile(N, tn)

    def body(a_ref, b_ref, r_ref, o_ref):
        o_ref[...] = r_ref[...] + scale * _dot(a_ref[...], b_ref[...])

    return pl.pallas_call(
        body, name=name, grid=(M // tm, N // tn),
        in_specs=[pl.BlockSpec((tm, K), lambda i, j: (i, 0)), pl.BlockSpec((K, tn), lambda i, j: (0, j)),
                  pl.BlockSpec((tm, tn), lambda i, j: (i, j))],
        out_specs=pl.BlockSpec((tm, tn), lambda i, j: (i, j)),
        out_shape=jax.ShapeDtypeStruct((M, N), F32),
        compiler_params=_params("parallel", "parallel"),
    )(a, b, res)


def _mm_nt(a, b, name, tm=1024, tn=512):
    M, K = a.shape
    N = b.shape[0]
    tm, tn = min(tm, M), _tile(N, tn)

    def body(a_ref, b_ref, o_ref):
        o_ref[...] = _dot_nt(a_ref[...], b_ref[...])

    return pl.pallas_call(
        body, name=name, grid=(M // tm, N // tn),
        in_specs=[pl.BlockSpec((tm, K), lambda i, j: (i, 0)), pl.BlockSpec((tn, K), lambda i, j: (j, 0))],
        out_specs=pl.BlockSpec((tm, tn), lambda i, j: (i, j)),
        out_shape=jax.ShapeDtypeStruct((M, N), F32),
        compiler_params=_params("parallel", "parallel"),
    )(a, b)


def _dh_rms_bwd(pairs, x, g, dres, copy_scale, name, tm=512):
    T = x.shape[0]
    tm = min(tm, T)
    n = len(pairs)

    def body(*refs):
        ab = refs[:2 * n]
        x_ref, g_ref, dres_ref, dx_ref, dxb_ref, dg_ref = refs[2 * n:]
        dy = _dot(ab[0][...], ab[1][...])
        for k in range(1, n):
            dy = dy + _dot(ab[2 * k][...], ab[2 * k + 1][...])
        xv = x_ref[...]
        r = lax.rsqrt(jnp.mean(xv * xv, axis=-1, keepdims=True) + EPS)
        xhat = xv * r
        dxhat = dy * g_ref[...]
        dx = dres_ref[...] + r * (dxhat - xhat * jnp.mean(dxhat * xhat, axis=-1, keepdims=True))
        dx_ref[...] = dx
        dxb_ref[...] = (copy_scale * dx).astype(BF16)

        @pl.when(pl.program_id(0) == 0)
        def _():
            dg_ref[...] = jnp.zeros_like(dg_ref)

        dg_ref[...] += jnp.sum(dy * xhat, axis=0, keepdims=True)

    row = pl.BlockSpec((tm, D), lambda i: (i, 0))
    vec = pl.BlockSpec((1, D), lambda i: (0, 0))
    in_specs, operands = [], []
    for a, b in pairs:
        in_specs += [pl.BlockSpec((tm, a.shape[1]), lambda i: (i, 0)),
                     pl.BlockSpec(b.shape, lambda i: (0, 0), pipeline_mode=pl.Buffered(1))]
        operands += [a, b]
    return pl.pallas_call(
        body, name=name, grid=(T // tm,),
        in_specs=in_specs + [row, vec, row], out_specs=[row, row, vec],
        out_shape=[jax.ShapeDtypeStruct((T, D), F32), jax.ShapeDtypeStruct((T, D), BF16),
                   jax.ShapeDtypeStruct((1, D), F32)],
        compiler_params=_params("arbitrary"),
    )(*operands, x, g, dres)


def _mm_tn(a, b, scale, name, tm=1792, tn=1792, tk=2048):
    T, M = a.shape
    N = b.shape[1]
    tm, tn, tk = _tile(M, tm), _tile(N, tn), min(tk, T)
    nk = T // tk

    def body(a_ref, b_ref, o_ref, acc_ref):
        k = pl.program_id(2)

        @pl.when(k == 0)
        def _():
            acc_ref[...] = jnp.zeros_like(acc_ref)

        acc_ref[...] += _dot_tn(a_ref[...], b_ref[...])

        @pl.when(k == nk - 1)
        def _():
            o_ref[...] = (scale * acc_ref[...]).astype(BF16)

    return pl.pallas_call(
        body, name=name, grid=(M // tm, N // tn, nk),
        in_specs=[pl.BlockSpec((tk, tm), lambda i, j, k: (k, i)), pl.BlockSpec((tk, tn), lambda i, j, k: (k, j))],
        out_specs=pl.BlockSpec((tm, tn), lambda i, j, k: (i, j)),
        out_shape=jax.ShapeDtypeStruct((M, N), BF16),
        scratch_shapes=[pltpu.VMEM((tm, tn), F32)],
        compiler_params=_params("parallel", "parallel", "arbitrary"),
    )(a, b)


def _rms_rows(x_ref, gain_ref):
    xv = x_ref[...]
    r = lax.rsqrt(jnp.mean(xv * xv, axis=-1, keepdims=True) + EPS)
    return (xv * r * gain_ref[...]).astype(BF16)


def _ffn_in(x, gain, wg_t, wu_t, tm=256, tn=D_FFP):
    T = x.shape[0]
    tm = min(tm, T)

    def body(x_ref, gain_ref, wg_ref, wu_ref, h_ref, dg_ref, du_ref, a_ref):
        hv = _rms_rows(x_ref, gain_ref)

        @pl.when(pl.program_id(0) == 0)
        def _():
            h_ref[...] = hv

        g = _dot_nt(hv, wg_ref[...])
        u = _dot_nt(hv, wu_ref[...])
        s = _sigmoid(g)
        gs = g * s
        dg_ref[...] = (u * (s + gs - gs * s)).astype(BF16)
        du_ref[...] = gs.astype(BF16)
        a_ref[...] = (gs * u).astype(BF16)

    row = pl.BlockSpec((tm, D), lambda j, i: (i, 0))
    last = T // tm - 1
    h_spec = pl.BlockSpec((tm, D), lambda j, i: (jnp.where(j == 0, i, last), 0))
    wspec = pl.BlockSpec((tn, D), lambda j, i: (j, 0))
    ospec = pl.BlockSpec((tm, tn), lambda j, i: (i, j))
    oshape = jax.ShapeDtypeStruct((T, D_FFP), BF16)
    return pl.pallas_call(
        body, name="ffn_in", grid=(D_FFP // tn, T // tm),
        in_specs=[row, pl.BlockSpec((1, D), lambda j, i: (0, 0)), wspec, wspec],
        out_specs=[h_spec, ospec, ospec, ospec],
        out_shape=[jax.ShapeDtypeStruct((T, D), BF16), oshape, oshape, oshape],
        compiler_params=_params("arbitrary", "arbitrary"),
    )(x, gain, wg_t, wu_t)


def _mix_in(x, gain, w_in_t, tm=512):
    T = x.shape[0]
    tm = min(tm, T)

    def body(x_ref, gain_ref, w_ref, h_ref, p_ref):
        hv = _rms_rows(x_ref, gain_ref)
        h_ref[...] = hv
        p_ref[...] = _dot_nt(hv, w_ref[...])

    row = pl.BlockSpec((tm, D), lambda i: (i, 0))
    return pl.pallas_call(
        body, name="mix_in", grid=(T // tm,),
        in_specs=[row, pl.BlockSpec((1, D), lambda i: (0, 0)), pl.BlockSpec((D_IN, D), lambda i: (0, 0))],
        out_specs=[row, pl.BlockSpec((tm, D_IN), lambda i: (i, 0))],
        out_shape=[jax.ShapeDtypeStruct((T, D), BF16), jax.ShapeDtypeStruct((T, D_IN), F32)],
        compiler_params=_params("parallel"),
    )(x, gain, w_in_t)


def _ffn_mid_bwd(dyh, wout, da_dg, da_du, tm=256, tn=D_FFP):
    T = dyh.shape[0]
    tm = min(tm, T)

    def body(dy_ref, w_ref, pg_ref, pu_ref, dg_ref, du_ref):
        da = _dot_nt(dy_ref[...], w_ref[...])
        dg_ref[...] = (da * pg_ref[...].astype(F32)).astype(BF16)
        du_ref[...] = (da * pu_ref[...].astype(F32)).astype(BF16)

    ospec = pl.BlockSpec((tm, tn), lambda j, i: (i, j))
    oshape = jax.ShapeDtypeStruct((T, D_FFP), BF16)
    return pl.pallas_call(
        body, name="ffn_mid_bwd", grid=(D_FFP // tn, T // tm),
        in_specs=[pl.BlockSpec((tm, D), lambda j, i: (i, 0)), pl.BlockSpec((tn, D), lambda j, i: (j, 0)),
                  ospec, ospec],
        out_specs=[ospec, ospec], out_shape=[oshape, oshape],
        compiler_params=_params("parallel", "parallel"),
    )(dyh, wout, da_dg, da_du)


def _final(x, tgt, gf, copy_scale, tm=512):
    T = x.shape[0]
    tm = min(tm, T)

    def body(x_ref, t_ref, g_ref, dx_ref, dxb_ref, dg_ref, loss_ref):
        xv = x_ref[...]
        r = lax.rsqrt(jnp.mean(xv * xv, axis=-1, keepdims=True) + EPS)
        xhat = xv * r
        err = xhat * g_ref[...] - t_ref[...]
        dy = err * (1.0 / D)
        dxhat = dy * g_ref[...]
        dx = r * (dxhat - xhat * jnp.mean(dxhat * xhat, axis=-1, keepdims=True))
        dx_ref[...] = dx
        dxb_ref[...] = (copy_scale * dx).astype(BF16)

        @pl.when(pl.program_id(0) == 0)
        def _():
            dg_ref[...] = jnp.zeros_like(dg_ref)
            loss_ref[...] = jnp.zeros_like(loss_ref)

        dg_ref[...] += jnp.sum(dy * xhat, axis=0, keepdims=True)
        loss_ref[...] += (0.5 / D) * jnp.sum(err * err)

    row = pl.BlockSpec((tm, D), lambda i: (i, 0))
    vec = pl.BlockSpec((1, D), lambda i: (0, 0))
    return pl.pallas_call(
        body, name="final_loss", grid=(T // tm,),
        in_specs=[row, row, vec],
        out_specs=[row, row, vec, pl.BlockSpec((1, 128), lambda i: (0, 0))],
        out_shape=[jax.ShapeDtypeStruct((T, D), F32), jax.ShapeDtypeStruct((T, D), BF16),
                   jax.ShapeDtypeStruct((1, D), F32), jax.ShapeDtypeStruct((1, 128), F32)],
        compiler_params=_params("arbitrary"),
    )(x, tgt, gf)


def _mix_block(T, limit):
    return min(limit, T // 2)


def _rows(tb, width):
    return lax.broadcasted_iota(jnp.int32, (tb, width), 0)


def _rglru_gates(xc, wax_ref, bax_ref, lam_ref):
    pre = _dot(xc.astype(BF16), wax_ref[...]) + bax_ref[...]
    r = _sigmoid(pre[:, :D_RNN])
    ig = _sigmoid(pre[:, D_RNN:])
    z = -lam_ref[...]
    sp = jnp.maximum(z, 0.0) + jnp.log(1.0 + jnp.exp(-jnp.abs(z)))
    log_a = (-LRU_C) * r * sp
    a = jnp.exp(log_a)
    mult = jnp.sqrt(-jnp.tanh(log_a) * (1.0 + a * a))
    return r, ig, sp, a, mult


def _conv(xa_ext, cw_ref, cb_ref):
    y = cb_ref[...] + cw_ref[3:4, :] * xa_ext
    for k in range(1, 4):
        y = y + cw_ref[3 - k:4 - k, :] * pltpu.roll(xa_ext, k, 0)
    return y[HALO:]


def _pool_window_lanes():
    lane = lax.broadcasted_iota(jnp.int32, (1, D_POOL), 1)
    return jnp.where(lane < 64, 2, jnp.where(lane < 128, 4, jnp.where(lane < 192, 8, 16)))


def _pool_select(s2, s4, s8, s16):
    lane = lax.broadcasted_iota(jnp.int32, s2.shape, 1)
    return jnp.where(lane < 64, s2, jnp.where(lane < 128, s4, jnp.where(lane < 192, s8, s16)))


def _pool_diff(xp_ext, t0, tb):
    s2 = xp_ext + pltpu.roll(xp_ext, 1, 0)
    s4 = s2 + pltpu.roll(s2, 2, 0)
    s8 = s4 + pltpu.roll(s4, 4, 0)
    s16 = s8 + pltpu.roll(s8, 8, 0)
    sel = _pool_select(s2, s4, s8, s16)[HALO:]
    cnt = jnp.minimum(t0 + _rows(tb, D_POOL) + 1, _pool_window_lanes()).astype(F32)
    return sel / cnt - xp_ext[HALO:], cnt


def _head_masks():
    lane = lax.broadcasted_iota(jnp.int32, (1, D_SGU), 1)
    return [((lane >= 64 * h) & (lane < 64 * (h + 1))).astype(F32) for h in range(4)]


def _sgu_mix(w_ref, vch, masks):
    z = masks[0] * _dot(w_ref[0], vch)
    for h in range(1, 4):
        z = z + masks[h] * _dot(w_ref[h], vch)
    return z


def _mix_fwd(p, prm):
    T = p.shape[0]
    tb = _mix_block(T, 512)
    nb = T // tb

    def body(p_ref, xah_ref, xph_ref, cw_ref, cb_ref, wax_ref, bax_ref, lam_ref, wp_ref, ps_ref, sgn_ref,
             ws_ref, bz_ref, y_ref, hs_ref, carry_ref):
        i = pl.program_id(0)
        keep = (i > 0).astype(F32)

        @pl.when(i == 0)
        def _():
            carry_ref[...] = jnp.zeros_like(carry_ref)

        xa_ext = jnp.concatenate([xah_ref[...] * keep, p_ref[:, 512:1024]], axis=0)
        xc = _conv(xa_ext, cw_ref, cb_ref)
        r, ig, sp, a, mult = _rglru_gates(xc, wax_ref, bax_ref, lam_ref)
        bv = mult * (ig * xc)
        row = _rows(tb, D_RNN)
        sub_row = jnp.bitwise_and(row, SCAN_SUB - 1)
        s = 1
        while s < SCAN_SUB:
            m = sub_row >= s
            bv = jnp.where(m, a * pltpu.roll(bv, s, 0) + bv, bv)
            a = jnp.where(m, a * pltpu.roll(a, s, 0), a)
            s *= 2
        carry = carry_ref[0:1, :]
        runs = []
        for k in range(tb // SCAN_SUB):
            sl = slice(k * SCAN_SUB, (k + 1) * SCAN_SUB)
            runs.append(bv[sl] + a[sl] * carry)
            carry = jnp.sum(jnp.where(_rows(8, D_RNN) == 7, runs[-1][SCAN_SUB - 8:], 0.0), axis=0, keepdims=True)
        h = jnp.concatenate(runs, axis=0)
        hs_ref[...] = h
        carry_ref[...] = jnp.broadcast_to(carry, carry_ref.shape)
        y_ref[:, 0:512] = (_gelu(p_ref[:, 0:512]) * h).astype(BF16)

        xp_ext = jnp.concatenate([xph_ref[...] * keep, p_ref[:, 1024:1280]], axis=0)
        d, _ = _pool_diff(xp_ext, i * tb, tb)
        y_ref[:, 512:768] = (_dot(d.astype(BF16), wp_ref[...]) * ps_ref[...]).astype(BF16)

        ug = _gelu(p_ref[:, 1280:1536])
        vg = _gelu(p_ref[:, 1536:1792])
        rv = lax.rsqrt(jnp.mean(vg * vg, axis=-1, keepdims=True) + EPS)
        vn = (vg * rv * sgn_ref[...]).astype(BF16)
        masks = _head_masks()
        for ci in range(tb // CHUNK):
            sl = slice(ci * CHUNK, (ci + 1) * CHUNK)
            z = _sgu_mix(ws_ref, vn[sl], masks) + bz_ref[...]
            y_ref[sl, 768:1024] = (ug[sl] * z).astype(BF16)

    hb = tb // HALO

    def halo(i):
        return jnp.maximum(i * hb - 1, 0)

    def full(shape):
        return pl.BlockSpec(shape, lambda i: (0,) * len(shape))

    return pl.pallas_call(
        body, name="mix_fwd", grid=(nb,),
        in_specs=[pl.BlockSpec((tb, D_IN), lambda i: (i, 0)),
                  pl.BlockSpec((HALO, D_RNN), lambda i: (halo(i), 1)),
                  pl.BlockSpec((HALO, D_POOL), lambda i: (halo(i), 4)),
                  full((4, D_RNN)), full((1, D_RNN)), full((D_RNN, 2 * D_RNN)), full((1, 2 * D_RNN)),
                  full((1, D_RNN)), full((D_POOL, D_POOL)), full((1, D_POOL)), full((1, D_SGU)),
                  full((4, CHUNK, CHUNK)), full((CHUNK, D_SGU))],
        out_specs=[pl.BlockSpec((tb, D), lambda i: (i, 0)), pl.BlockSpec((tb, D_RNN), lambda i: (i, 0))],
        out_shape=[jax.ShapeDtypeStruct((T, D), BF16), jax.ShapeDtypeStruct((T, D_RNN), F32)],
        scratch_shapes=[pltpu.VMEM((8, D_RNN), F32)],
        compiler_params=_params("arbitrary"),
    )(p, p, p, prm["conv_w"], prm["conv_b"], prm["wax"], prm["bax"], prm["lam"], prm["wpool"], prm["pool_scale"],
      prm["sgu_norm"], prm["ws"], prm["bz"])


def _mix_bwd(dy, p, hs, prm):
    T = p.shape[0]
    tb = _mix_block(T, 256)
    nb = T // tb
    hb = tb // HALO

    def body(dy_ref, p_ref, xah_ref, xph_ref, hs_ref, hsh_ref, cw_ref, cb_ref, wax_ref, waxt_ref, bax_ref,
             lam_ref, wp_ref, wpt_ref, ps_ref, sgn_ref, ws_ref, wst_ref, bz_ref,
             dp_ref, dcw_ref, dcb_ref, dwax_ref, dbax_ref, dlam_ref, dwp_ref, dps_ref, dsgn_ref, dws_ref,
             dbz_ref, gcarry_ref, xcfut_ref, mfut_ref):
        i = pl.program_id(0)
        bi = nb - 1 - i
        keep = (bi > 0).astype(F32)

        @pl.when(i == 0)
        def _():
            for ref in (dcw_ref, dcb_ref, dwax_ref, dbax_ref, dlam_ref, dwp_ref, dps_ref, dsgn_ref, dws_ref,
                        dbz_ref, gcarry_ref, xcfut_ref, mfut_ref):
                ref[...] = jnp.zeros_like(ref)

        xa_ext = jnp.concatenate([xah_ref[...] * keep, p_ref[:, 512:1024]], axis=0)
        xc = _conv(xa_ext, cw_ref, cb_ref)
        r, ig, sp, a, mult = _rglru_gates(xc, wax_ref, bax_ref, lam_ref)
        gg, dgg = _gelu_and_grad(p_ref[:, 0:512])
        dya = dy_ref[:, 0:512]
        dp_ref[:, 0:512] = (dya * hs_ref[...] * dgg).astype(BF16)
        row = _rows(tb, D_RNN)
        g = dya * gg + jnp.where(row == tb - 1, gcarry_ref[0:1, :], 0.0)
        al = pltpu.roll(a, tb - 1, 0)
        sub_row = jnp.bitwise_and(row, SCAN_SUB - 1)
        s = 1
        while s < SCAN_SUB:
            m = sub_row < SCAN_SUB - s
            g = jnp.where(m, al * pltpu.roll(g, tb - s, 0) + g, g)
            al = jnp.where(m, al * pltpu.roll(al, tb - s, 0), al)
            s *= 2
        nxt = jnp.zeros((1, D_RNN), F32)
        runs = []
        for k in reversed(range(tb // SCAN_SUB)):
            sl = slice(k * SCAN_SUB, (k + 1) * SCAN_SUB)
            runs.insert(0, g[sl] + al[sl] * nxt)
            nxt = jnp.sum(jnp.where(_rows(8, D_RNN) == 0, runs[0][:8], 0.0), axis=0, keepdims=True)
        g = jnp.concatenate(runs, axis=0)
        first = jnp.sum(jnp.where(_rows(8, D_RNN) == 0, (a * g)[0:8], 0.0), axis=0, keepdims=True)
        gcarry_ref[...] = jnp.broadcast_to(first, gcarry_ref.shape)
        hs_ext = jnp.concatenate([hsh_ref[...] * keep, hs_ref[...]], axis=0)
        h_prev = pltpu.roll(hs_ext, 1, 0)[HALO:]
        ix = ig * xc
        dlog_a = g * h_prev * a - (g * ix) * (a * a / mult)
        dlam_ref[...] += jnp.sum(dlog_a * r, axis=0, keepdims=True) * (LRU_C * _sigmoid(-lam_ref[...]))
        dpre_r = dlog_a * ((-LRU_C) * sp) * (r * (1.0 - r))
        dpre_i = (g * mult * xc) * (ig * (1.0 - ig))
        dpre = jnp.concatenate([dpre_r, dpre_i], axis=1)
        dbax_ref[...] += jnp.sum(dpre, axis=0, keepdims=True)
        dpre_b = dpre.astype(BF16)
        dwax_ref[...] += _dot_tn(xc.astype(BF16), dpre_b)
        dxc = g * mult * ig + _dot(dpre_b, waxt_ref[...])
        dcb_ref[...] += jnp.sum(dxc, axis=0, keepdims=True)
        for k in range(4):
            xs = xa_ext[HALO:] if k == 3 else pltpu.roll(xa_ext, 3 - k, 0)[HALO:]
            dcw_ref[k:k + 1, :] += jnp.sum(dxc * xs, axis=0, keepdims=True)
        dxc_ext = jnp.concatenate([dxc, xcfut_ref[...]], axis=0)
        n = tb + HALO
        dxa = cw_ref[3:4, :] * dxc_ext
        for k in range(1, 4):
            dxa = dxa + cw_ref[3 - k:4 - k, :] * pltpu.roll(dxc_ext, n - k, 0)
        dp_ref[:, 512:1024] = dxa[:tb].astype(BF16)
        xcfut_ref[...] = dxc[0:HALO]

        xp_ext = jnp.concatenate([xph_ref[...] * keep, p_ref[:, 1024:1280]], axis=0)
        d, cnt = _pool_diff(xp_ext, bi * tb, tb)
        db = d.astype(BF16)
        dyb = dy_ref[:, 512:768]
        dps_ref[...] += jnp.sum(dyb * _dot(db, wp_ref[...]), axis=0, keepdims=True)
        dq = (dyb * ps_ref[...]).astype(BF16)
        dwp_ref[...] += _dot_tn(db, dq)
        dd = _dot(dq, wpt_ref[...])
        mm = dd / cnt
        m_ext = jnp.concatenate([mm, mfut_ref[...]], axis=0)
        f2 = m_ext + pltpu.roll(m_ext, n - 1, 0)
        f4 = f2 + pltpu.roll(f2, n - 2, 0)
        f8 = f4 + pltpu.roll(f4, n - 4, 0)
        f16 = f8 + pltpu.roll(f8, n - 8, 0)
        dp_ref[:, 1024:1280] = (_pool_select(f2, f4, f8, f16)[:tb] - dd).astype(BF16)
        mfut_ref[...] = mm[0:HALO]

        ug, dug = _gelu_and_grad(p_ref[:, 1280:1536])
        vg, dvg = _gelu_and_grad(p_ref[:, 1536:1792])
        rv = lax.rsqrt(jnp.mean(vg * vg, axis=-1, keepdims=True) + EPS)
        vhat = vg * rv
        vn = (vhat * sgn_ref[...]).astype(BF16)
        dyc = dy_ref[:, 768:1024]
        masks = _head_masks()
        dz = dyc * ug
        dzb = dz.astype(BF16)
        dvn_parts = []
        for ci in range(tb // CHUNK):
            sl = slice(ci * CHUNK, (ci + 1) * CHUNK)
            z = _sgu_mix(ws_ref, vn[sl], masks) + bz_ref[...]
            dp_ref[sl, 1280:1536] = (dyc[sl] * z * dug[sl]).astype(BF16)
            dbz_ref[...] += dz[sl]
            for h in range(4):
                dws_ref[h] += _dot_nt((dz[sl] * masks[h]).astype(BF16), vn[sl])
            dvn_parts.append(_sgu_mix(wst_ref, dzb[sl], masks))
        dvn = jnp.concatenate(dvn_parts, axis=0)
        dsgn_ref[...] += jnp.sum(dvn * vhat, axis=0, keepdims=True)
        dvhat = dvn * sgn_ref[...]
        dvg_in = rv * (dvhat - vhat * jnp.mean(dvhat * vhat, axis=-1, keepdims=True))
        dp_ref[:, 1536:1792] = (dvg_in * dvg).astype(BF16)

        @pl.when(i == nb - 1)
        def _():
            tril = (lax.broadcasted_iota(jnp.int32, (CHUNK, CHUNK), 0)
                    >= lax.broadcasted_iota(jnp.int32, (CHUNK, CHUNK), 1)).astype(F32)
            for h in range(4):
                dws_ref[h] = dws_ref[h] * tril

    def blk(i):
        return nb - 1 - i

    def halo(i):
        return jnp.maximum(blk(i) * hb - 1, 0)

    def full(shape):
        return pl.BlockSpec(shape, lambda i: (0,) * len(shape))

    small_shapes = [(4, D_RNN), (1, D_RNN), (D_RNN, 2 * D_RNN), (1, 2 * D_RNN), (1, D_RNN), (D_POOL, D_POOL),
                    (1, D_POOL), (1, D_SGU), (4, CHUNK, CHUNK), (CHUNK, D_SGU)]
    outs = pl.pallas_call(
        body, name="mix_bwd", grid=(nb,),
        in_specs=[pl.BlockSpec((tb, D), lambda i: (blk(i), 0)),
                  pl.BlockSpec((tb, D_IN), lambda i: (blk(i), 0)),
                  pl.BlockSpec((HALO, D_RNN), lambda i: (halo(i), 1)),
                  pl.BlockSpec((HALO, D_POOL), lambda i: (halo(i), 4)),
                  pl.BlockSpec((tb, D_RNN), lambda i: (blk(i), 0)),
                  pl.BlockSpec((HALO, D_RNN), lambda i: (halo(i), 0)),
                  full((4, D_RNN)), full((1, D_RNN)), full((D_RNN, 2 * D_RNN)), full((2 * D_RNN, D_RNN)),
                  full((1, 2 * D_RNN)), full((1, D_RNN)), full((D_POOL, D_POOL)), full((D_POOL, D_POOL)),
                  full((1, D_POOL)), full((1, D_SGU)), full((4, CHUNK, CHUNK)), full((4, CHUNK, CHUNK)),
                  full((CHUNK, D_SGU))],
        out_specs=[pl.BlockSpec((tb, D_IN), lambda i: (blk(i), 0))] + [full(s) for s in small_shapes],
        out_shape=[jax.ShapeDtypeStruct((T, D_IN), BF16)] + [jax.ShapeDtypeStruct(s, F32) for s in small_shapes],
        scratch_shapes=[pltpu.VMEM((8, D_RNN), F32), pltpu.VMEM((HALO, D_RNN), F32),
                        pltpu.VMEM((HALO, D_POOL), F32)],
        compiler_params=_params("arbitrary"),
    )(dy, p, p, p, hs, hs, prm["conv_w"], prm["conv_b"], prm["wax"], prm["wax_t"], prm["bax"], prm["lam"],
      prm["wpool"], prm["wpool_t"], prm["pool_scale"], prm["sgu_norm"], prm["ws"], prm["ws_t"], prm["bz"])
    names = ("dp", "conv_w", "conv_b", "wax", "bax", "lam", "wpool", "pool_scale", "sgu_norm", "ws", "bz")
    return dict(zip(names, outs))


ANY = pl.BlockSpec(memory_space=pl.ANY)


def _place():
    x, y, c = lax.axis_index("x"), lax.axis_index("y"), lax.axis_index("c")
    return x, y, c


def _all_to_all(xs, name):
    def body(in_ref, out_ref, send_sems, recv_sems, local_sem):
        x, y, c = _place()
        me = 4 * x + 2 * y + c
        mine = pltpu.make_async_copy(in_ref.at[me], out_ref.at[me], local_sem)
        mine.start()
        copies = []
        for rel in range(1, N_DEV):
            tx = 1 - x if rel & 4 else x
            ty = 1 - y if rel & 2 else y
            tc = 1 - c if rel & 1 else c
            cp = pltpu.make_async_remote_copy(
                src_ref=in_ref.at[4 * tx + 2 * ty + tc], dst_ref=out_ref.at[me],
                send_sem=send_sems.at[rel - 1], recv_sem=recv_sems.at[rel - 1],
                device_id=(tx, ty, tc), device_id_type=MESH)
            cp.start()
            copies.append(cp)
        for cp in copies:
            cp.wait()
        mine.wait()

    return pl.pallas_call(
        body, name=name, in_specs=[ANY], out_specs=ANY,
        out_shape=jax.ShapeDtypeStruct(xs.shape, xs.dtype),
        scratch_shapes=[pltpu.SemaphoreType.DMA((N_DEV - 1,)), pltpu.SemaphoreType.DMA((N_DEV - 1,)),
                        pltpu.SemaphoreType.DMA],
    )(xs)


def _all_gather8(xs, name):
    def body(x_ref, out_ref, send_sems, recv_sems, local_sem):
        x, y, c = _place()
        me, sibling = (x, y, c), (x, y, 1 - c)
        chips = [(1 - x, y), (x, 1 - y), (1 - x, 1 - y)]

        def rows(px, py, pc):
            return out_ref.at[4 * px + 2 * py + pc]

        def copy(k, block, to, src=None):
            return pltpu.make_async_remote_copy(
                src_ref=rows(*block) if src is None else src, dst_ref=rows(*block),
                send_sem=send_sems.at[k], recv_sem=recv_sems.at[k], device_id=to, device_id_type=MESH)

        mine = pltpu.make_async_copy(x_ref, rows(*me), local_sem)
        mine.start()
        first = [copy(0, me, sibling, src=x_ref)]
        first += [copy(1 + j, me, (*chip, c), src=x_ref) for j, chip in enumerate(chips)]
        for cp in first:
            cp.start()
        passed = [copy(4 + j, (*chip, c), sibling) for j, chip in enumerate(chips)]
        for j, chip in enumerate(chips):
            copy(1 + j, (*chip, c), me).wait_recv()
            passed[j].start()
        copy(0, sibling, me).wait_recv()
        for j, chip in enumerate(chips):
            copy(4 + j, (*chip, 1 - c), me).wait_recv()
        for cp in first + passed:
            cp.wait_send()
        mine.wait()

    return pl.pallas_call(
        body, name=name, in_specs=[ANY], out_specs=ANY,
        out_shape=jax.ShapeDtypeStruct((N_DEV,) + xs.shape, xs.dtype),
        scratch_shapes=[pltpu.SemaphoreType.DMA((7,)), pltpu.SemaphoreType.DMA((7,)), pltpu.SemaphoreType.DMA],
    )(xs)


HBM = pl.BlockSpec(memory_space=pltpu.HBM)
SEM = pl.BlockSpec(memory_space=pltpu.SEMAPHORE)
EFFECT = pltpu.SideEffectType.DATAFLOW_SIDE_EFFECTING


def _in_hbm(a):
    return pltpu.with_memory_space_constraint(a, pltpu.HBM)


def _local_copy(src, dst, stage, sem):
    load = pltpu.make_async_copy(src, stage, sem)
    load.start()
    load.wait()
    store = pltpu.make_async_copy(stage, dst, sem)
    store.start()
    store.wait()


def _unique(windows):
    arrays = []
    for per_chip in windows:
        for arr, _ in per_chip:
            if not any(arr is a for a in arrays):
                arrays.append(arr)
    return arrays


def _exchange_start(layer, windows, lands, name):
    arrays = _unique(windows)
    na, nt = len(arrays), len(windows)

    def body(*refs):
        in_refs, land_refs = refs[:na], refs[na:na + nt]
        send_sems, recv_sems = refs[na + nt], refs[na + nt + 1]
        token = refs[-1]
        x, y, c = _place()
        me = 4 * x + 2 * y + c
        for t in range(nt):
            for j in range(N_CHIP):
                arr, window = windows[t][j]
                src = window(in_refs[next(i for i, a in enumerate(arrays) if a is arr)])

                @pl.when(me != 2 * j + layer)
                def _():
                    pltpu.make_async_remote_copy(
                        src_ref=src, dst_ref=land_refs[t].at[me], send_sem=send_sems.at[N_CHIP * t + j],
                        recv_sem=recv_sems.at[N_DEV * t + me], device_id=(j // 2, j % 2, layer),
                        device_id_type=MESH).start()
        token[...] = jnp.zeros_like(token)

    outs = pl.pallas_call(
        body, name=name,
        out_shape=(pltpu.SemaphoreType.DMA((N_CHIP * nt,)), pltpu.SemaphoreType.DMA((N_DEV * nt,)),
                   *[pltpu.HBM(a.shape, a.dtype) for a in lands], jax.ShapeDtypeStruct((8, 128), F32)),
        in_specs=[HBM] * (na + nt),
        out_specs=(SEM, SEM, *[HBM] * nt, pl.BlockSpec(memory_space=pltpu.VMEM)),
        input_output_aliases={na + t: 2 + t for t in range(nt)},
        compiler_params=pltpu.CompilerParams(has_side_effects=EFFECT, vmem_limit_bytes=VMEM_LIMIT),
    )(*[_in_hbm(a) for a in arrays], *[_in_hbm(a) for a in lands])
    return outs[0], outs[1], list(outs[2:2 + nt]), outs[-1]


def _exchange_wait(layer, windows, lands, send_sems, recv_sems, after, name):
    arrays = _unique(windows)
    na, nt = len(arrays), len(windows)

    def body(*refs):
        in_refs, land_refs = refs[:na], refs[na:na + nt]
        send_sems, recv_sems = refs[na + nt], refs[na + nt + 1]
        stages, local_sem = refs[-1 - nt:-1], refs[-1]
        x, y, c = _place()
        me = 4 * x + 2 * y + c

        def source(t, j):
            arr, window = windows[t][j]
            return window(in_refs[next(i for i, a in enumerate(arrays) if a is arr)])

        @pl.when(c == layer)
        def _():
            for t in range(nt):
                for j in range(N_CHIP):
                    @pl.when(me == 2 * j + layer)
                    def _():
                        _local_copy(source(t, j), land_refs[t].at[me], stages[t], local_sem)

        for t in range(nt):
            for j in range(N_CHIP):
                @pl.when(me != 2 * j + layer)
                def _():
                    pltpu.make_async_remote_copy(
                        src_ref=source(t, j), dst_ref=land_refs[t].at[me], send_sem=send_sems.at[N_CHIP * t + j],
                        recv_sem=recv_sems.at[N_DEV * t + me], device_id=(j // 2, j % 2, layer),
                        device_id_type=MESH).wait_send()

        @pl.when(c == layer)
        def _():
            for t in range(nt):
                for s in range(N_DEV):
                    @pl.when(me != s)
                    def _():
                        slot = land_refs[t].at[s]
                        pltpu.make_async_remote_copy(
                            src_ref=slot, dst_ref=slot, send_sem=send_sems.at[N_CHIP * t],
                            recv_sem=recv_sems.at[N_DEV * t + s], device_id=(x, y, c),
                            device_id_type=MESH).wait_recv()

    outs = pl.pallas_call(
        body, name=name,
        out_shape=tuple(pltpu.HBM(a.shape, a.dtype) for a in lands),
        in_specs=[HBM] * (na + nt) + [SEM, SEM, ANY],
        out_specs=tuple([HBM] * nt),
        input_output_aliases={na + t: t for t in range(nt)},
        scratch_shapes=[pltpu.VMEM(a.shape[1:], a.dtype) for a in lands] + [pltpu.SemaphoreType.DMA],
        compiler_params=pltpu.CompilerParams(has_side_effects=EFFECT, vmem_limit_bytes=VMEM_LIMIT),
    )(*[_in_hbm(a) for a in arrays], *lands, send_sems, recv_sems, after)
    return list(outs)


def _other_chips(x, y):
    return [(1 - x, y), (x, 1 - y), (1 - x, 1 - y)]


def _split_rows(rows):
    if rows < 32:
        return [(0, rows), (rows, 0)]
    cut = -(-(rows // 2) // 16) * 16
    return [(0, cut), (cut, rows - cut)]


def _gather_copies(in_refs, land_refs, send_sems, recv_sems, x, y, sender_core, both):
    copies = []
    for t, src in enumerate(in_refs):
        r0, n = _split_rows(src.shape[0])[sender_core]
        if n == 0:
            continue
        for rel, (tx, ty) in enumerate(_other_chips(x, y)):
            for tc in (0, 1) if both else (sender_core,):
                copies.append(dict(
                    src_ref=src.at[pl.ds(r0, n)], dst_ref=land_refs[t].at[2 * x + y].at[pl.ds(r0, n)],
                    send_sem=send_sems.at[6 * t + 2 * rel + tc],
                    recv_sem=recv_sems.at[2 * (3 * t + rel) + sender_core],
                    device_id=(tx, ty, tc), device_id_type=MESH))
    return copies


def _gather_start(shards, lands, name, both):
    nt = len(shards)

    def body(*refs):
        in_refs, land_refs = refs[:nt], refs[nt:2 * nt]
        send_sems, recv_sems = refs[2 * nt], refs[2 * nt + 1]
        token = refs[-1]
        x, y, c = _place()
        for core in range(2):
            @pl.when(c == core)
            def _():
                for cp_args in _gather_copies(in_refs, land_refs, send_sems, recv_sems, x, y, core, both):
                    pltpu.make_async_remote_copy(**cp_args).start()
        token[...] = jnp.zeros_like(token)

    outs = pl.pallas_call(
        body, name=name,
        out_shape=(pltpu.SemaphoreType.DMA((6 * nt,)), pltpu.SemaphoreType.DMA((6 * nt,)),
                   *[pltpu.HBM(a.shape, a.dtype) for a in lands], jax.ShapeDtypeStruct((8, 128), F32)),
        in_specs=[HBM] * (2 * nt),
        out_specs=(SEM, SEM, *[HBM] * nt, pl.BlockSpec(memory_space=pltpu.VMEM)),
        input_output_aliases={nt + t: 2 + t for t in range(nt)},
        compiler_params=pltpu.CompilerParams(has_side_effects=EFFECT, vmem_limit_bytes=VMEM_LIMIT),
    )(*[_in_hbm(a) for a in shards], *[_in_hbm(a) for a in lands])
    return outs[0], outs[1], list(outs[2:2 + nt]), outs[-1]


def _gather_wait(shards, lands, send_sems, recv_sems, after, name, both):
    nt = len(shards)

    def body(*refs):
        in_refs, land_refs = refs[:nt], refs[nt:2 * nt]
        send_sems, recv_sems = refs[2 * nt], refs[2 * nt + 1]
        stages, local_sem = refs[-1 - nt:-1], refs[-1]
        x, y, c = _place()

        for t in range(nt):
            _local_copy(in_refs[t], land_refs[t].at[2 * x + y], stages[t], local_sem)

        for core in range(2):
            @pl.when(c == core)
            def _():
                for cp_args in _gather_copies(in_refs, land_refs, send_sems, recv_sems, x, y, core, both):
                    pltpu.make_async_remote_copy(**cp_args).wait_send()

        def wait_parts_from(core):
            for t in range(nt):
                r0, n = _split_rows(in_refs[t].shape[0])[core]
                for rel, (tx, ty) in enumerate(_other_chips(x, y)):
                    if n > 0:
                        part = land_refs[t].at[2 * tx + ty].at[pl.ds(r0, n)]
                        pltpu.make_async_remote_copy(
                            src_ref=part, dst_ref=part, send_sem=send_sems.at[6 * t],
                            recv_sem=recv_sems.at[2 * (3 * t + rel) + core], device_id=(x, y, c),
                            device_id_type=MESH).wait_recv()

        for core in range(2):
            if both:
                wait_parts_from(core)
            else:
                pl.when(c == core)(functools.partial(wait_parts_from, core))

    outs = pl.pallas_call(
        body, name=name,
        out_shape=tuple(pltpu.HBM(a.shape, a.dtype) for a in lands),
        in_specs=[HBM] * (2 * nt) + [SEM, SEM] + [ANY] * len(after),
        out_specs=tuple([HBM] * nt),
        input_output_aliases={nt + t: t for t in range(nt)},
        scratch_shapes=[pltpu.VMEM(a.shape, a.dtype) for a in shards] + [pltpu.SemaphoreType.DMA],
        compiler_params=pltpu.CompilerParams(has_side_effects=EFFECT, vmem_limit_bytes=VMEM_LIMIT),
    )(*[_in_hbm(a) for a in shards], *lands, send_sems, recv_sems, *after)
    return list(outs)


def _pair_share(lands, name):
    nt = len(lands)
    splits = [_split_rows(a.shape[1]) for a in lands]

    def body(*refs):
        land_refs = refs[:nt]
        stages = refs[2 * nt:3 * nt]
        load_sems, send_sems, recv_sems = refs[3 * nt:]
        x, y, c = _place()

        def parts(core):
            out = []
            for t in range(nt):
                r0, n = splits[t][core]
                for rel, (tx, ty) in enumerate(_other_chips(x, y)):
                    if n > 0:
                        out.append((3 * t + rel, stages[t].at[rel, pl.ds(0, n)],
                                    land_refs[t].at[2 * tx + ty].at[pl.ds(r0, n)]))
            return out

        def send(core):
            loads = [pltpu.make_async_copy(part, stage, load_sems.at[k]) for k, stage, part in parts(core)]
            for cp in loads:
                cp.start()
            pushes = []
            for cp, (k, stage, part) in zip(loads, parts(core)):
                cp.wait()
                push = pltpu.make_async_remote_copy(src_ref=stage, dst_ref=part, send_sem=send_sems.at[k],
                                                    recv_sem=recv_sems.at[k], device_id=(x, y, 1 - c),
                                                    device_id_type=MESH)
                push.start()
                pushes.append(push)
            for push in pushes:
                push.wait_send()
            for k, _, part in parts(1 - core):
                pltpu.make_async_remote_copy(src_ref=part, dst_ref=part, send_sem=send_sems.at[k],
                                             recv_sem=recv_sems.at[k], device_id=(x, y, 1 - c),
                                             device_id_type=MESH).wait_recv()

        for core in range(2):
            pl.when(c == core)(functools.partial(send, core))

    outs = pl.pallas_call(
        body, name=name, in_specs=[ANY] * nt, out_specs=[ANY] * nt,
        out_shape=[jax.ShapeDtypeStruct(a.shape, a.dtype) for a in lands],
        input_output_aliases={t: t for t in range(nt)},
        scratch_shapes=[pltpu.VMEM((3, max(n for _, n in sp), a.shape[2]), a.dtype) for a, sp in zip(lands, splits)]
        + [pltpu.SemaphoreType.DMA((3 * nt,))] * 3,
        compiler_params=pltpu.CompilerParams(vmem_limit_bytes=VMEM_LIMIT),
    )(*lands)
    return list(outs)


def _tie(a, token):
    def body(a_ref, token_ref, o_ref):
        pass

    return pl.pallas_call(
        body, name="tie", in_specs=[ANY, ANY], out_specs=ANY,
        out_shape=jax.ShapeDtypeStruct(a.shape, a.dtype), input_output_aliases={0: 0},
    )(a, token)


def _sum_share(xs, name):
    _, r, cols = xs.shape
    tr = max(t for t in range(16, min(r, 704) + 1, 16) if r % t == 0)
    nblk = r // tr

    def body(x_ref, out_ref, acc_ref, send_sems, local_sems, recv_sem):
        i = pl.program_id(0)
        slot = i % 2
        x, y, c = _place()

        def copies(s, blk):
            dst = out_ref.at[c, pl.ds(blk * tr, tr), :]
            loc = pltpu.make_async_copy(acc_ref.at[s], dst, local_sems.at[s])
            rem = pltpu.make_async_remote_copy(src_ref=acc_ref.at[s], dst_ref=dst, send_sem=send_sems.at[s],
                                               recv_sem=recv_sem, device_id=(x, y, 1 - c), device_id_type=MESH)
            return loc, rem

        @pl.when(i >= 2)
        def _():
            loc, rem = copies(slot, i - 2)
            loc.wait()
            rem.wait_send()

        acc = x_ref[0].astype(F32)
        for k in range(1, N_DEV):
            acc = acc + x_ref[k].astype(F32)
        acc_ref[slot] = acc
        loc, rem = copies(slot, i)
        loc.start()
        rem.start()

        @pl.when(i == nblk - 1)
        def _():
            for back in range(min(2, nblk)):
                blk = nblk - 1 - back
                loc, rem = copies(blk % 2, blk)
                loc.wait()
                rem.wait_send()
            theirs = out_ref.at[1 - c]
            pltpu.make_async_remote_copy(src_ref=theirs, dst_ref=theirs, send_sem=send_sems.at[0],
                                         recv_sem=recv_sem, device_id=(x, y, 1 - c),
                                         device_id_type=MESH).wait_recv()

    return pl.pallas_call(
        body, name=name, grid=(nblk,),
        in_specs=[pl.BlockSpec((N_DEV, tr, cols), lambda i: (0, i, 0))],
        out_specs=ANY,
        out_shape=jax.ShapeDtypeStruct((2, r, cols), F32),
        scratch_shapes=[pltpu.VMEM((2, tr, cols), F32), pltpu.SemaphoreType.DMA((2,)),
                        pltpu.SemaphoreType.DMA((2,)), pltpu.SemaphoreType.DMA],
        compiler_params=_params("arbitrary"),
    )(xs)


def _sum8(xs, name):
    _, r, cols = xs.shape
    tr = 8
    for cand in (256, 128, 64, 32, 16):
        if r % cand == 0:
            tr = cand
            break

    def body(x_ref, o_ref):
        acc = x_ref[0].astype(F32)
        for k in range(1, N_DEV):
            acc = acc + x_ref[k].astype(F32)
        o_ref[...] = acc

    return pl.pallas_call(
        body, name=name, grid=(r // tr,),
        in_specs=[pl.BlockSpec((N_DEV, tr, cols), lambda i: (0, i, 0))],
        out_specs=pl.BlockSpec((tr, cols), lambda i: (i, 0)),
        out_shape=jax.ShapeDtypeStruct((r, cols), F32),
        compiler_params=_params("parallel"),
    )(xs)


def _adamw(w, g, m, v, name):
    R, C = w.shape
    tr = max([t for t in range(8, min(R, 512) + 1, 8) if R % t == 0] or [R])
    c1 = 1.0 / (1.0 - ADAM_B1 ** ADAM_STEP)
    c2 = 1.0 / (1.0 - ADAM_B2 ** ADAM_STEP)

    def body(w_ref, g_ref, m_ref, v_ref, d_ref, nm_ref, nv_ref):
        gv = g_ref[...]
        nm = ADAM_B1 * m_ref[...] + (1.0 - ADAM_B1) * gv
        nv = ADAM_B2 * v_ref[...] + (1.0 - ADAM_B2) * (gv * gv)
        d_ref[...] = (-ADAM_LR) * ((nm * c1) / (jnp.sqrt(nv * c2) + ADAM_EPS) + ADAM_WD * w_ref[...])
        nm_ref[...] = nm
        nv_ref[...] = nv

    spec = pl.BlockSpec((tr, C), lambda i: (i, 0))
    shape = jax.ShapeDtypeStruct((R, C), F32)
    return pl.pallas_call(
        body, name=name, grid=(R // tr,), in_specs=[spec] * 4, out_specs=[spec] * 3, out_shape=[shape] * 3,
        compiler_params=_params("parallel"),
    )(w, g, m, v)


def _flat_rows(parts, rows):
    flat = jnp.concatenate([q.reshape(-1) for q in parts])
    flat = jnp.pad(flat, (0, rows * LANES - flat.shape[0]))
    return flat.reshape(rows, LANES)


def _round_up(n, m):
    return (n + m - 1) // m * m


def _block_diag(w):
    H, n, _ = w.shape
    eye = jnp.eye(H, dtype=w.dtype)
    return (eye[:, None, :, None] * w[:, :, None, :]).reshape(H * n, H * n)


def _diag_blocks(w, H, n):
    w4 = w.reshape(H, n, H, n)
    return jnp.stack([w4[h, :, h, :] for h in range(H)])


W_IN_T = ("ffn1_w_in", "w_in", "ffn2_w_in")


def _ffn_in_weights(g_in):
    zeros = jnp.zeros((FF_HALF - FF_SHARD, D), g_in.dtype)
    wg_t = jnp.concatenate([g_in[0], zeros, g_in[1], zeros], axis=0)
    wu_t = jnp.concatenate([g_in[2], zeros, g_in[3], zeros], axis=0)
    return wg_t, wu_t


def _ffn_out_weights(g_out):
    zeros = jnp.zeros((FF_HALF - FF_SHARD, D), g_out.dtype)
    return jnp.concatenate([g_out[0], g_out[1], zeros, g_out[2], g_out[3], zeros], axis=0)


LAND_SHAPES = {"ffn1_w_in": (FF_SHARD, D), "ffn1_w_out": (FF_ROWS, D), "w_in": (D_IN // N_CHIP, D),
               "w_out": (D // N_CHIP, D), "ffn2_w_in": (FF_SHARD, D), "ffn2_w_out": (FF_ROWS, D)}


def _rows_window(arr, start, size):
    return arr, lambda r: r.at[pl.ds(start, size), :]


def _w_in_grad_windows(dwg_t, dwu_t):
    return [_rows_window(dwg_t if j < 2 else dwu_t, (j % 2) * FF_HALF, FF_SHARD) for j in range(N_CHIP)]


def _w_out_grad_windows(dwout):
    return [_rows_window(dwout, (j // 2) * FF_HALF + (j % 2) * FF_ROWS, FF_ROWS) for j in range(N_CHIP)]


def _mix_grad_windows(dwin_t, dwo):
    win = [_rows_window(dwin_t, j * (D_IN // N_CHIP), D_IN // N_CHIP) for j in range(N_CHIP)]
    wo = [_rows_window(dwo, j * (D // N_CHIP), D // N_CHIP) for j in range(N_CHIP)]
    return win, wo


def kernel(x, ffn1_norm, ffn1_w_in, ffn1_w_out, mix_norm, w_in, conv_w, conv_b, rg_w_a, rg_b_a, rg_w_x, rg_b_x, lru_lambda, pool_w, pool_scale, sgu_norm, sgu_w, sgu_b, w_out, ffn2_norm, ffn2_w_in, ffn2_w_out, final_norm, loss_target, m_ffn1_norm, m_ffn1_w_in, m_ffn1_w_out, m_mix_norm, m_w_in, m_conv_w, m_conv_b, m_rg_w_a, m_rg_b_a, m_rg_w_x, m_rg_b_x, m_lru_lambda, m_pool_w, m_pool_scale, m_sgu_norm, m_sgu_w, m_sgu_b, m_w_out, m_ffn2_norm, m_ffn2_w_in, m_ffn2_w_out, m_final_norm, v_ffn1_norm, v_ffn1_w_in, v_ffn1_w_out, v_mix_norm, v_w_in, v_conv_w, v_conv_b, v_rg_w_a, v_rg_b_a, v_rg_w_x, v_rg_b_x, v_lru_lambda, v_pool_w, v_pool_scale, v_sgu_norm, v_sgu_w, v_sgu_b, v_w_out, v_ffn2_norm, v_ffn2_w_in, v_ffn2_w_out, v_final_norm):
    args = locals()
    W = {n: args[n] for n in WEIGHTS}
    M = {n: args["m_" + n] for n in WEIGHTS}
    V = {n: args["v_" + n] for n in WEIGHTS}
    depth = ffn1_norm.shape[0]
    T = x.shape[1]
    xi, yi, ci = _place()
    chip = 2 * xi + yi

    assert depth == 2, "core c of a chip reduces layer c"
    groups = [(l, names) for l in range(depth)
              for names in (["ffn1_w_in"], ["ffn1_w_out", "w_in", "w_out"], ["ffn2_w_in", "ffn2_w_out"])]
    def stored(a, n):
        return jnp.swapaxes(a, 1, 2) if n in W_IN_T else a

    wb = {n: stored(W[n], n).astype(BF16) for n in BIG}
    groups[1][1].append("conv_w")
    conv_shard = conv_w.reshape(-1, conv_w.shape[-1])
    flights = {}

    def weights_start(k, dep=None):
        l, names = groups[k]
        shards = [conv_shard if n == "conv_w" else wb[n][l] for n in names]
        if dep is not None:
            shards[0] = _tie(shards[0], dep)
        lands = [lax.empty((N_CHIP,) + s.shape, s.dtype) for s in shards]
        send, recv, lands, token = _gather_start(shards, lands, "weights_start_%d" % k, k >= 2)
        flights[k] = (shards, lands, send, recv)
        return token

    def weights_wait(k, after):
        l, names = groups[k]
        shards, lands, send, recv = flights[k]
        got = _gather_wait(shards, lands, send, recv, after, "weights_wait_%d" % k, k >= 2)
        token = weights_start(k + 1, got[0]) if k + 1 < len(groups) else None
        if k < 2:
            got = _pair_share(got, "weights_share_%d" % k)
        return dict(zip(names, got)), token

    def after_start(a, token):
        return a if token is None else _tie(a, token)

    first_tokens = [weights_start(0)]

    layers = []
    for l in range(depth):
        L = {f: dict(norm=W[f + "_norm"][l][None]) for f in ("ffn1", "ffn2")}
        ws = jnp.where(jnp.tril(jnp.ones((CHUNK, CHUNK), bool))[None], sgu_w[l], 0.0)
        wax = jnp.concatenate([_block_diag(rg_w_a[l]), _block_diag(rg_w_x[l])], axis=1)
        wpool = _block_diag(pool_w[l])
        L["mix"] = dict(
            conv_b=conv_b[l][None], wax=wax.astype(BF16), wax_t=wax.T.astype(BF16),
            bax=jnp.concatenate([rg_b_a[l].reshape(-1), rg_b_x[l].reshape(-1)])[None], lam=lru_lambda[l][None],
            wpool=wpool.astype(BF16), wpool_t=wpool.T.astype(BF16), pool_scale=pool_scale[l][None],
            sgu_norm=sgu_norm[l][None], ws=ws.astype(BF16), ws_t=jnp.swapaxes(ws, 1, 2).astype(BF16),
            bz=jnp.repeat(sgu_b[l].T, 64, axis=1))
        L["mix_norm"] = mix_norm[l][None]
        layers.append(L)
    for token in first_tokens:
        layers[0]["ffn1"]["norm"] = _tie(layers[0]["ffn1"]["norm"], token)

    xs = x[0]
    saved = []
    for l, L in enumerate(layers):
        F1, F2 = L["ffn1"], L["ffn2"]
        got, token = weights_wait(3 * l, [xs] + ([wb[n] for n in BIG] if l == 0 else []))
        F1["wg"], F1["wu"] = _ffn_in_weights(got["ffn1_w_in"])
        F1["norm"] = after_start(F1["norm"], token)
        h, da_dg, da_du, a = _ffn_in(xs, F1["norm"], F1["wg"], F1["wu"])
        got, token = weights_wait(3 * l + 1, [a])
        F1["wout"] = _ffn_out_weights(got["ffn1_w_out"])
        L["w_in"] = got["w_in"].reshape(D_IN, D)
        L["w_out"] = got["w_out"].reshape(D, D)
        if "conv_w" in got:
            conv_full = jnp.concatenate([got["conv_w"][j] for j in range(N_CHIP)], axis=1)
            for ll in range(depth):
                layers[ll]["mix"]["conv_w"] = conv_full.reshape(depth, 4, D_RNN)[ll]
        a = after_start(a, token)
        x1 = _mm_res(a, F1["wout"], xs, 0.5, "ffn_out", tm=512, tn=D)
        s1 = (xs, h, da_dg, da_du, a)
        hm, p = _mix_in(x1, L["mix_norm"], L["w_in"])
        ycat, hs = _mix_fwd(p, L["mix"])
        x2 = _mm_res(ycat, L["w_out"], x1, 1.0, "mix_out", tm=512, tn=D)
        got, token = weights_wait(3 * l + 2, [x2])
        F2["wg"], F2["wu"] = _ffn_in_weights(got["ffn2_w_in"])
        F2["wout"] = _ffn_out_weights(got["ffn2_w_out"])
        F2["norm"] = after_start(F2["norm"], token)
        s2 = (x2,) + tuple(_ffn_in(x2, F2["norm"], F2["wg"], F2["wu"]))
        x3 = _mm_res(s2[-1], F2["wout"], x2, 0.5, "ffn_out", tm=512, tn=D)
        saved.append((s1, (x1, hm, p, ycat, hs), s2))
        xs = x3

    dx, dxb, d_final, loss_part = _final(xs, loss_target[0], final_norm[None], 0.5)

    G = {n: [None] * depth for n in SMALL if n != "final_norm"}
    lands = {n: lax.empty((N_DEV,) + LAND_SHAPES[n], BF16) for n in BIG}
    in_flight = []

    def send_grads(l, windows, tag):
        names = list(windows)
        send, recv, thru, token = _exchange_start(l, [windows[n] for n in names], [lands[n] for n in names],
                                                  "grads_start_" + tag)
        lands.update(zip(names, thru))
        in_flight.append((l, names, [windows[n] for n in names], send, recv, "grads_wait_" + tag))
        return token

    def ffn_bwd(dx, dxb, F, s, f, l, pending, send_now):
        xin, h, da_dg, da_du, a = s
        dwout = _mm_tn(a, dxb, 1.0, "ffn_dwout")
        if send_now:
            token = send_grads(l, {f + "_w_out": _w_out_grad_windows(dwout)}, "l%d_%s_out" % (l, f))
            dxb = _tie(dxb, token)
        else:
            pending[f + "_w_out"] = _w_out_grad_windows(dwout)
        dg, du = _ffn_mid_bwd(dxb, F["wout"], da_dg, da_du)
        dwg = _mm_tn(dg, h, 1.0, "ffn_dwg")
        dwu = _mm_tn(du, h, 1.0, "ffn_dwu")
        pending[f + "_w_in"] = _w_in_grad_windows(dwg, dwu)
        if send_now:
            dg = _tie(dg, send_grads(l, pending, "l%d_%s" % (l, f)))
        dx, dxb, dn = _dh_rms_bwd([(dg, F["wg"]), (du, F["wu"])], xin, F["norm"], dx,
                                  1.0 if f == "ffn2" else 0.5, "ffn_dh")
        G[f + "_norm"][l] = dn[0]
        return dx, dxb

    for l in reversed(range(depth)):
        L = layers[l]
        s1, (x1, hm, p, ycat, hs), s2 = saved[l]
        pending = {}
        dx, dxb = ffn_bwd(dx, dxb, L["ffn2"], s2, "ffn2", l, pending, l == 0)
        if l == 0:
            pending = {}
        dycat = _mm_nt(dxb, L["w_out"], "mix_dy", tm=512, tn=D)
        dwo = _mm_tn(ycat, dxb, 1.0, "mix_dwout")
        mg = _mix_bwd(dycat, p, hs, L["mix"])
        dwin = _mm_tn(mg["dp"], hm, 1.0, "mix_dwin")
        pending["w_in"], pending["w_out"] = _mix_grad_windows(dwin, dwo)
        if l == 0:
            dp = _tie(mg["dp"], send_grads(l, pending, "l0_mix"))
            pending = {}
        else:
            dp = mg["dp"]
        dx, dxb, dn = _dh_rms_bwd([(dp, L["w_in"])], x1, L["mix_norm"], dx, 0.5, "mix_dh")
        G["mix_norm"][l] = dn[0]
        G["conv_w"][l], G["conv_b"][l] = mg["conv_w"], mg["conv_b"][0]
        G["rg_w_a"][l] = _diag_blocks(mg["wax"][:, :D_RNN], 8, 64)
        G["rg_w_x"][l] = _diag_blocks(mg["wax"][:, D_RNN:], 8, 64)
        G["rg_b_a"][l] = mg["bax"][0, :D_RNN].reshape(8, 64)
        G["rg_b_x"][l] = mg["bax"][0, D_RNN:].reshape(8, 64)
        G["lru_lambda"][l] = mg["lam"][0]
        G["pool_w"][l] = _diag_blocks(mg["wpool"], 4, 64)
        G["pool_scale"][l], G["sgu_norm"][l] = mg["pool_scale"][0], mg["sgu_norm"][0]
        G["sgu_w"][l] = mg["ws"]
        G["sgu_b"][l] = mg["bz"].reshape(CHUNK, 4, 64).sum(-1).T
        dx, dxb = ffn_bwd(dx, dxb, L["ffn1"], s1, "ffn1", l, pending, l == 0)
        if l > 0:
            dxb = _tie(dxb, send_grads(l, pending, "l%d" % l))
    grad_x = dx[None]
    G = {n: jnp.stack(v) for n, v in G.items()}
    G["final_norm"] = d_final[0]

    for l, names, windows, send, recv, tag in in_flight:
        lands.update(zip(names, _exchange_wait(l, windows, [lands[n] for n in names], send, recv, dx, tag)))
    both = [_sum_share(lands[n], "sum_share_" + n) for n in BIG]
    grads = dict(zip(BIG, both))

    small_sizes = [int(np.prod(G[n].shape)) for n in SMALL]
    srows = _round_up(sum(small_sizes) + 1, N_DEV * 8 * LANES) // (N_DEV * LANES)
    sflat = _flat_rows([G[n] for n in SMALL] + [loss_part[0, :1]], N_DEV * srows)
    sgot = _all_to_all(sflat.reshape(N_DEV, srows, LANES), "exchange_small_grads")
    sall = _all_gather8(_sum8(sgot, "sum_small_grads"), "share_small_grads").reshape(-1)
    off = 0
    for n, size in zip(SMALL, small_sizes):
        grads[n] = sall[off:off + size].reshape(G[n].shape)
        off += size
    loss = sall[off]
    grads["conv_w"] = lax.dynamic_slice_in_dim(grads["conv_w"], chip * conv_w.shape[2], conv_w.shape[2], axis=2)

    delta, new_m, new_v = {}, {}, {}
    for n in BIG:
        shp = grads[n].shape
        two_d = (shp[0] * shp[1], shp[2])
        outs = _adamw(stored(W[n], n).reshape(two_d), grads[n].reshape(two_d), stored(M[n], n).reshape(two_d),
                      stored(V[n], n).reshape(two_d), "adamw_" + n)
        delta[n], new_m[n], new_v[n] = (stored(o.reshape(shp), n) for o in outs)
        grads[n] = stored(grads[n], n)
    arows = _round_up(sum(int(np.prod(W[n].shape)) for n in SMALL), 8 * LANES) // LANES
    outs = _adamw(*(_flat_rows([src[n] for n in SMALL], arows) for src in (W, grads, M, V)), "adamw_small")
    outs = [o.reshape(-1) for o in outs]
    off = 0
    for n in SMALL:
        size = int(np.prod(W[n].shape))
        delta[n], new_m[n], new_v[n] = (o[off:off + size].reshape(W[n].shape) for o in outs)
        off += size

    return (loss, grad_x, *[grads[n] for n in WEIGHTS], *[delta[n] for n in WEIGHTS],
            *[new_m[n] for n in WEIGHTS], *[new_v[n] for n in WEIGHTS])
```
